```python
import math
import jax, jax.numpy as jnp
from jax import lax
import numpy as np

D_MODEL = 2048
BATCH = 4
SEQ = 2048
DEPTH = 1
DEC_BATCH = 8
DEC_SEQ = 1
PAST_LEN = 16384
PAGE_SIZE = 128

NSA_HEADS = 16
NSA_KV_HEADS = 4
NSA_HPG = NSA_HEADS // NSA_KV_HEADS
NSA_HEAD_DIM = 64
NSA_Q = NSA_HEADS * NSA_HEAD_DIM
NSA_KV = NSA_KV_HEADS * NSA_HEAD_DIM
CMP_STRIDE = 16
CMP_BLOCK = 32
CMP_HIDDEN = 128
SLC_BLOCK = 64
N_SELECT = 16
WINDOW = 512
NSA_Q_BLOCK = 64
GLA_HEADS = 4
GLA_DK = (D_MODEL // 4) // GLA_HEADS
GLA_DV = (D_MODEL // 2) // GLA_HEADS
GLA_RANK = 16
GLA_TAU = 16.0
GLA_CHUNK = 16
D_FF = 4 * D_MODEL
EPS = 1e-6
NEG = -1e30
FORCE = 1e4
SPLITS = (NSA_Q, 6 * NSA_KV, 3 * NSA_HEADS,
          GLA_HEADS * GLA_DK, GLA_HEADS * GLA_DK, GLA_HEADS * GLA_DV, GLA_HEADS * GLA_DV,
          GLA_RANK, 2 * D_MODEL)
D_IN = sum(SPLITS)

kernel_name = 'nsa_gla_hybrid_decode_step'


def rmsnorm(x, g):
    xf = x.astype(jnp.float32)
    y = xf * lax.rsqrt(jnp.mean(xf * xf, axis=-1, keepdims=True) + EPS)
    return (y * g.astype(jnp.float32)).astype(x.dtype)


def softmax_masked(s, mask):
    p = jax.nn.softmax(jnp.where(mask, s, NEG), axis=-1)
    return jnp.where(mask, p, 0.0)


def compress(k, pe, w1, w2):
    B, L, G, dh = k.shape
    m = CMP_BLOCK // CMP_STRIDE
    n_chunk = L // CMP_STRIDE
    n_cmp = n_chunk - m + 1
    ch = k[:, :n_chunk * CMP_STRIDE].reshape(B, n_chunk, CMP_STRIDE, G, dh)
    blk = jnp.concatenate([ch[:, j:j + n_cmp] for j in range(m)], axis=2)
    blk = blk + pe[None, None, :, None, :].astype(k.dtype)
    blk = blk.transpose(0, 1, 3, 2, 4).reshape(B, n_cmp, G, CMP_BLOCK * dh)
    return jax.nn.gelu(blk @ w1) @ w2


def slc_importance(p_cmp, n_slc):
    r = SLC_BLOCK // CMP_STRIDE
    m = CMP_BLOCK // CMP_STRIDE
    n_cmp = p_cmp.shape[-1]
    pad = [(0, 0)] * (p_cmp.ndim - 1) + [(m - 1, n_slc * r + r - n_cmp)]
    pp = jnp.pad(p_cmp, pad)
    return sum(pp[..., o:o + n_slc * r:r] for o in range(r + m - 1))


def nsa_attend(q, gates, kc, vc, ks, vs, kw, vw, start, w):
    B, T, G, HPG, dh = q.shape
    L = kc.shape[1]
    scale = dh ** -0.5
    kcmp = compress(kc, w['cmp_pe_k'], w['cmp_k_w1'], w['cmp_k_w2'])
    vcmp = compress(vc, w['cmp_pe_v'], w['cmp_v_w1'], w['cmp_v_w2'])
    n_cmp = kcmp.shape[1]
    cmp_end = jnp.arange(n_cmp) * CMP_STRIDE + CMP_BLOCK - 1
    n_slc = -(-L // SLC_BLOCK)
    n_sel = min(N_SELECT, n_slc)
    pad_l = n_slc * SLC_BLOCK - L

    def blocks(a):
        a = jnp.pad(a, ((0, 0), (0, pad_l), (0, 0), (0, 0)))
        return a.reshape(B, n_slc, SLC_BLOCK, G, dh).transpose(0, 3, 1, 2, 4)

    ks_b, vs_b = blocks(ks), blocks(vs)
    qb = min(NSA_Q_BLOCK, T)
    n_qb = -(-T // qb)
    Tp = n_qb * qb
    qpad = ((0, 0), (0, Tp - T), (0, 0), (0, 0), (0, 0))
    q = jnp.pad(q, qpad)
    gates = jnp.pad(gates, qpad)
    kvpad = ((0, 0), (0, Tp - T), (0, 0), (0, 0))
    kw = jnp.pad(kw, kvpad)
    vw = jnp.pad(vw, kvpad)
    b_ix = jnp.arange(B)[:, None, None, None]
    g_ix = jnp.arange(G)[None, None, :, None]
    slc_ids = jnp.arange(n_slc)
    f32 = jnp.float32

    def body(i):
        q_b = lax.dynamic_slice_in_dim(q, i * qb, qb, axis=1)
        g_b = lax.dynamic_slice_in_dim(gates, i * qb, qb, axis=1)
        pos = start + i * qb + jnp.arange(qb)
        s_c = jnp.einsum('bqghd,bngd->bqghn', q_b, kcmp, preferred_element_type=f32) * scale
        m_c = (cmp_end[None, :] <= pos[:, None])[None, :, None, None, :]
        p_c = softmax_masked(s_c, m_c)
        o_c = jnp.einsum('bqghn,bngd->bqghd', p_c.astype(vcmp.dtype), vcmp)
        imp = slc_importance(p_c.sum(axis=3), n_slc)
        valid = (slc_ids[None, :] * SLC_BLOCK <= pos[:, None])
        forced = (slc_ids[None, :] == 0) | (slc_ids[None, :] == pos[:, None] // SLC_BLOCK)
        score = jnp.where(valid[None, :, None, :],
                          imp + jnp.where(forced, FORCE, 0.0)[None, :, None, :], NEG)
        _, idx = lax.top_k(score, n_sel)
        k_sel = ks_b[b_ix, g_ix, idx]
        v_sel = vs_b[b_ix, g_ix, idx]
        kpos = idx[..., None] * SLC_BLOCK + jnp.arange(SLC_BLOCK)
        m_s = (kpos <= pos[None, :, None, None, None])[:, :, :, None]
        s_s = jnp.einsum('bqghd,bqgnkd->bqghnk', q_b, k_sel, preferred_element_type=f32) * scale
        p_s = softmax_masked(s_s.reshape(B, qb, G, HPG, n_sel * SLC_BLOCK),
                             m_s.reshape(B, qb, G, 1, n_sel * SLC_BLOCK)).reshape(s_s.shape)
        o_s = jnp.einsum('bqghnk,bqgnkd->bqghd', p_s.astype(v_sel.dtype), v_sel)
        k_w = lax.dynamic_slice_in_dim(kw, i * qb, WINDOW + qb, axis=1)
        v_w = lax.dynamic_slice_in_dim(vw, i * qb, WINDOW + qb, axis=1)
        wpos = start - WINDOW + i * qb + jnp.arange(WINDOW + qb)
        m_w = ((wpos[None, :] <= pos[:, None]) & (wpos[None, :] >= pos[:, None] - WINDOW)
               & (wpos[None, :] >= 0))[None, :, None, None, :]
        s_w = jnp.einsum('bqghd,bkgd->bqghk', q_b, k_w, preferred_element_type=f32) * scale
        p_w = softmax_masked(s_w, m_w)
        o_w = jnp.einsum('bqghk,bkgd->bqghd', p_w.astype(v_w.dtype), v_w)
        return g_b[..., 0:1] * o_c + g_b[..., 1:2] * o_s + g_b[..., 2:3] * o_w

    out = lax.map(body, jnp.arange(n_qb))
    return out.transpose(1, 0, 2, 3, 4, 5).reshape(B, Tp, G, HPG, dh)[:, :T]


def gla_chunked(q, k, v, g, s0):
    B, T, H, dk = q.shape
    C = min(GLA_CHUNK, T)
    n = -(-T // C)
    Tp = n * C

    def chunks(a):
        a = jnp.pad(a, ((0, 0), (0, Tp - T), (0, 0), (0, 0)))
        return a.reshape(B, n, C, H, a.shape[-1]).transpose(1, 0, 3, 2, 4)

    qc, kc, vc, gc = chunks(q), chunks(k), chunks(v), chunks(g)
    causal = jnp.tril(jnp.ones((C, C), dtype=bool))

    def step(S, inp):
        qi, ki, vi, gi = inp
        b = jnp.cumsum(gi, axis=2)
        b_last = b[:, :, -1:, :]
        q_t = qi * jnp.exp(b)
        k_t = ki * jnp.exp(-b)
        a = jnp.where(causal, jnp.einsum('bhcd,bhsd->bhcs', q_t, k_t), 0.0)
        o = jnp.einsum('bhcd,bhde->bhce', q_t, S) + jnp.einsum('bhcs,bhse->bhce', a, vi)
        S = (jnp.exp(b_last)[:, :, 0, :, None] * S
             + jnp.einsum('bhcd,bhce->bhde', ki * jnp.exp(b_last - b), vi))
        return S, o

    S, o = lax.scan(step, s0, (qc, kc, vc, gc))
    o = o.transpose(1, 0, 3, 2, 4).reshape(B, Tp, H, v.shape[-1])[:, :T]
    return o, S


def layer(x, past, win, s0, w):
    B, T, _ = x.shape
    G, HPG, dh = NSA_KV_HEADS, NSA_HPG, NSA_HEAD_DIM
    f32 = jnp.float32
    start = past[0].shape[1]
    h = rmsnorm(x, w['norm1_g'])
    proj = h @ w['w_in']
    pts = np.cumsum(SPLITS)[:-1].tolist()
    q_n, kv_n, g_n, q_l, k_l, v_l, r_l, lr_l, br = jnp.split(proj, pts, axis=-1)
    kv = kv_n.reshape(B, T, 6, G, dh)
    rows = [kv[:, :, j] for j in range(6)]
    full = [jnp.concatenate([p.astype(x.dtype), r], axis=1) for p, r in zip(past, rows[:4])]
    wext = [jnp.concatenate([jnp.zeros((B, WINDOW - b.shape[1], G, dh), x.dtype), b.astype(x.dtype), r], axis=1)
            for b, r in zip(win, rows[4:])]
    n_keep = min(WINDOW, win[0].shape[1] + T)
    new_win = [a[:, a.shape[1] - n_keep:] for a in wext]
    q = q_n.reshape(B, T, G, HPG, dh)
    gates = jax.nn.sigmoid(g_n + w['b_nsa_gate']).reshape(B, T, G, HPG, 3)
    o_a = nsa_attend(q, gates, full[0], full[1], full[2], full[3], wext[0], wext[1], start, w)
    o_a = o_a.reshape(B, T, NSA_Q)
    gq = q_l.reshape(B, T, GLA_HEADS, GLA_DK).astype(f32) * (GLA_DK ** -0.5)
    gk = k_l.reshape(B, T, GLA_HEADS, GLA_DK).astype(f32)
    gv = v_l.reshape(B, T, GLA_HEADS, GLA_DV).astype(f32)
    log_a = jax.nn.log_sigmoid((lr_l @ w['gla_w_a2'] + w['gla_b_a']).astype(f32)) / GLA_TAU
    log_a = log_a.reshape(B, T, GLA_HEADS, GLA_DK)
    o_g, s_new = gla_chunked(gq, gk, gv, log_a, s0.astype(f32))
    o_g = rmsnorm(o_g, w['gla_norm_g']) * jax.nn.silu(r_l.reshape(B, T, GLA_HEADS, GLA_DV).astype(f32))
    o_b = o_g.reshape(B, T, GLA_HEADS * GLA_DV).astype(x.dtype)
    ga, gb = jnp.split(br, 2, axis=-1)
    mix = (jax.nn.sigmoid(ga) * (o_a @ w['w_br_a']) + jax.nn.sigmoid(gb) * (o_b @ w['w_br_b'])) @ w['w_o']
    x = x + mix
    hf = jax.nn.relu(rmsnorm(x, w['norm2_g']) @ w['w_up'])
    x = x + (hf * hf) @ w['w_down']
    return x, rows[:4], new_win, s_new


def setup_inputs(seed: int = 0) -> dict:
    key = jax.random.key(seed)
    keys = iter(jax.random.split(key, 48))

    def nrm(shape, scale):
        return jax.random.normal(next(keys), shape, jnp.float32) * scale

    n_pages = PAST_LEN // PAGE_SIZE
    n_pool = (DEC_BATCH * n_pages * 5) // 4
    win_buf = min(WINDOW, PAST_LEN)
    kv_shape = (DEPTH, n_pool, PAGE_SIZE, NSA_KV_HEADS, NSA_HEAD_DIM)
    win_shape = (DEPTH, DEC_BATCH, win_buf, NSA_KV_HEADS, NSA_HEAD_DIM)
    x_prompt = nrm((BATCH, SEQ, D_MODEL), 1.0)
    x_sample = nrm((DEC_BATCH, DEC_SEQ, D_MODEL), 1.0)
    cache_cmp_k = nrm(kv_shape, 1.0)
    cache_cmp_v = nrm(kv_shape, 1.0)
    cache_slc_k = nrm(kv_shape, 1.0)
    cache_slc_v = nrm(kv_shape, 1.0)
    cache_win_k = nrm(win_shape, 1.0)
    cache_win_v = nrm(win_shape, 1.0)
    state_gla = nrm((DEPTH, DEC_BATCH, GLA_HEADS, GLA_DK, GLA_DV), 1.0)
    page_table = jax.random.permutation(next(keys), n_pool)[:DEC_BATCH * n_pages]
    page_table = page_table.reshape(DEC_BATCH, n_pages).astype(jnp.int32)
    return {
        'x_prompt': x_prompt,
        'x_sample': x_sample,
        'cache_cmp_k': cache_cmp_k,
        'cache_cmp_v': cache_cmp_v,
        'cache_slc_k': cache_slc_k,
        'cache_slc_v': cache_slc_v,
        'cache_win_k': cache_win_k,
        'cache_win_v': cache_win_v,
        'state_gla': state_gla,
        'page_table': page_table,
        'norm1_g': 1.0 + nrm((DEPTH, D_MODEL), 0.05),
        'w_in': nrm((DEPTH, D_MODEL, D_IN), D_MODEL ** -0.5),
        'b_nsa_gate': nrm((DEPTH, 3 * NSA_HEADS), 0.01),
        'cmp_pe_k': nrm((DEPTH, CMP_BLOCK, NSA_HEAD_DIM), 0.1),
        'cmp_pe_v': nrm((DEPTH, CMP_BLOCK, NSA_HEAD_DIM), 0.1),
        'cmp_k_w1': nrm((DEPTH, CMP_BLOCK * NSA_HEAD_DIM, CMP_HIDDEN), (CMP_BLOCK * NSA_HEAD_DIM) ** -0.5),
        'cmp_k_w2': nrm((DEPTH, CMP_HIDDEN, NSA_HEAD_DIM), CMP_HIDDEN ** -0.5),
        'cmp_v_w1': nrm((DEPTH, CMP_BLOCK * NSA_HEAD_DIM, CMP_HIDDEN), (CMP_BLOCK * NSA_HEAD_DIM) ** -0.5),
        'cmp_v_w2': nrm((DEPTH, CMP_HIDDEN, NSA_HEAD_DIM), CMP_HIDDEN ** -0.5),
        'gla_w_a2': nrm((DEPTH, GLA_RANK, GLA_HEADS * GLA_DK), GLA_RANK ** -0.5),
        'gla_b_a': nrm((DEPTH, GLA_HEADS * GLA_DK), 0.1),
        'gla_norm_g': 1.0 + nrm((DEPTH, GLA_DV), 0.05),
        'w_br_a': nrm((DEPTH, NSA_Q, D_MODEL), NSA_Q ** -0.5),
        'w_br_b': nrm((DEPTH, GLA_HEADS * GLA_DV, D_MODEL), (GLA_HEADS * GLA_DV) ** -0.5),
        'w_o': nrm((DEPTH, D_MODEL, D_MODEL), D_MODEL ** -0.5),
        'norm2_g': 1.0 + nrm((DEPTH, D_MODEL), 0.05),
        'w_up': nrm((DEPTH, D_MODEL, D_FF), D_MODEL ** -0.5),
        'w_down': nrm((DEPTH, D_FF, D_MODEL), D_FF ** -0.5),
        'norm_f': 1.0 + nrm((D_MODEL,), 0.05),
    }


def reference(x_prompt, x_sample, cache_cmp_k, cache_cmp_v, cache_slc_k, cache_slc_v, cache_win_k,
              cache_win_v, state_gla, page_table, norm1_g, w_in, b_nsa_gate, cmp_pe_k, cmp_pe_v,
              cmp_k_w1, cmp_k_w2, cmp_v_w1, cmp_v_w2, gla_w_a2, gla_b_a, gla_norm_g, w_br_a, w_br_b,
              w_o, norm2_g, w_up, w_down, norm_f):
    G, dh = NSA_KV_HEADS, NSA_HEAD_DIM
    xp, xs = x_prompt, x_sample
    bp, db = xp.shape[0], xs.shape[0]
    outs_p = [[] for _ in range(7)]
    outs_s = [[] for _ in range(7)]
    for l in range(DEPTH):
        w = dict(norm1_g=norm1_g[l], w_in=w_in[l], b_nsa_gate=b_nsa_gate[l], cmp_pe_k=cmp_pe_k[l],
                 cmp_pe_v=cmp_pe_v[l], cmp_k_w1=cmp_k_w1[l], cmp_k_w2=cmp_k_w2[l], cmp_v_w1=cmp_v_w1[l],
                 cmp_v_w2=cmp_v_w2[l], gla_w_a2=gla_w_a2[l], gla_b_a=gla_b_a[l], gla_norm_g=gla_norm_g[l],
                 w_br_a=w_br_a[l], w_br_b=w_br_b[l], w_o=w_o[l], norm2_g=norm2_g[l], w_up=w_up[l],
                 w_down=w_down[l])
        empty = jnp.zeros((bp, 0, G, dh), xp.dtype)
        s0_p = jnp.zeros((bp, GLA_HEADS, GLA_DK, GLA_DV), jnp.float32)
        xp, rows_p, win_p, s_p = layer(xp, [empty, empty, empty, empty], [empty, empty], s0_p, w)
        past_s = [c[l][page_table].reshape(db, -1, G, dh)
                  for c in (cache_cmp_k, cache_cmp_v, cache_slc_k, cache_slc_v)]
        xs, rows_s, win_s, s_s = layer(xs, past_s, [cache_win_k[l], cache_win_v[l]], state_gla[l], w)
        for lst, a in zip(outs_p, rows_p + win_p + [s_p]):
            lst.append(a)
        for lst, a in zip(outs_s, rows_s + win_s + [s_s]):
            lst.append(a)
    y_prompt = rmsnorm(xp, norm_f)
    y_sample = rmsnorm(xs, norm_f)
    p_cmp_k, p_cmp_v, p_slc_k, p_slc_v, p_win_k, p_win_v, p_gla = [jnp.stack(a) for a in outs_p]
    s_cmp_k, s_cmp_v, s_slc_k, s_slc_v, s_win_k, s_win_v, s_gla = [jnp.stack(a) for a in outs_s]
    return (y_prompt, y_sample, p_cmp_k, p_cmp_v, p_slc_k, p_slc_v, p_win_k, p_win_v, p_gla,
            s_cmp_k, s_cmp_v, s_slc_k, s_slc_v, s_win_k, s_win_v, s_gla)
```

```python
import functools

import jax
import jax.numpy as jnp
from jax import lax
from jax.experimental import pallas as pl
from jax.experimental.pallas import tpu as pltpu

D_MODEL = 2048
DEPTH = 1
PAGE_SIZE = 128
NSA_HEADS = 16
NSA_KV_HEADS = 4
NSA_HPG = NSA_HEADS // NSA_KV_HEADS
NSA_HEAD_DIM = 64
NSA_Q = NSA_HEADS * NSA_HEAD_DIM
NSA_KV = NSA_KV_HEADS * NSA_HEAD_DIM
CMP_STRIDE = 16
CMP_BLOCK = 32
CMP_HIDDEN = 128
SLC_BLOCK = 64
N_SELECT = 16
WINDOW = 512
GLA_HEADS = 4
GLA_DK = (D_MODEL // 4) // GLA_HEADS
GLA_DV = (D_MODEL // 2) // GLA_HEADS
GLA_RANK = 16
GLA_TAU = 16.0
D_FF = 4 * D_MODEL
EPS = 1e-6
NEG = -1e30
FORCE = 1e4
SPLITS = (NSA_Q, 6 * NSA_KV, 3 * NSA_HEADS,
          GLA_HEADS * GLA_DK, GLA_HEADS * GLA_DK, GLA_HEADS * GLA_DV, GLA_HEADS * GLA_DV,
          GLA_RANK, 2 * D_MODEL)

F32 = jnp.float32
BF16 = jnp.bfloat16
LANES = 128
VMEM_LIMIT_BYTES = 56 * 1024 * 1024
N_GATE = 3 * NSA_HEADS
GLA_CHUNK = 32
NSA_TQ = 128
NSA_TK = 256
CMP_PAGES = 16
SDS = jax.ShapeDtypeStruct


def _params(*sem):
    return pltpu.CompilerParams(dimension_semantics=sem, vmem_limit_bytes=VMEM_LIMIT_BYTES)


def _resident(shape):
    nd = len(shape)
    return pl.BlockSpec(shape, lambda *_: (0,) * nd, pipeline_mode=pl.Buffered(1))


def _rms(x, g):
    return x * lax.rsqrt(jnp.mean(x * x, axis=-1, keepdims=True) + EPS) * g


def _proj_kernel(x_ref, g_ref, w_ref, b_ref, *out_refs, spec):
    h = _rms(x_ref[...], g_ref[...]).astype(BF16)
    misc = None
    for o_ref, (kind, c0, c1, scale) in zip(out_refs, spec):
        if kind == 'gates':
            for g in range(NSA_KV_HEADS):
                o_ref[g] = misc if g == 0 else pltpu.roll(misc, LANES - g * 3 * NSA_HPG, axis=1)
            continue
        r = jnp.dot(h, w_ref[:, c0:c1], preferred_element_type=F32)
        if scale != 1.0:
            r = r * scale
        if kind == 'f32':
            o_ref[...] = r
        elif kind == 'bf16':
            o_ref[...] = r.astype(BF16)
        elif kind == 'sigmoid':
            o_ref[...] = jax.nn.sigmoid(r)
        elif kind == 'hm':
            for i in range((c1 - c0) // NSA_HEAD_DIM):
                o_ref[i] = r[:, i * NSA_HEAD_DIM:(i + 1) * NSA_HEAD_DIM].astype(BF16)
        elif kind == 'misc':
            lane = lax.broadcasted_iota(jnp.int32, r.shape, 1)
            misc = jnp.where(lane < N_GATE, jax.nn.sigmoid(r + b_ref[...]), r)
            o_ref[...] = misc


def _proj(x, norm_g, w, bias, spec, tm):
    m, d = x.shape
    n = w.shape[1]
    assert m % tm == 0
    out_shape, out_specs = [], []
    for kind, c0, c1, _ in spec:
        if kind == 'hm':
            nh = (c1 - c0) // NSA_HEAD_DIM
            out_shape.append(SDS((nh, m, NSA_HEAD_DIM), BF16))
            out_specs.append(pl.BlockSpec((nh, tm, NSA_HEAD_DIM), lambda i: (0, i, 0)))
        elif kind == 'gates':
            out_shape.append(SDS((NSA_KV_HEADS, m, LANES), F32))
            out_specs.append(pl.BlockSpec((NSA_KV_HEADS, tm, LANES), lambda i: (0, i, 0)))
        else:
            out_shape.append(SDS((m, c1 - c0), BF16 if kind == 'bf16' else F32))
            out_specs.append(pl.BlockSpec((tm, c1 - c0), lambda i: (i, 0)))
    return pl.pallas_call(
        functools.partial(_proj_kernel, spec=tuple(spec)),
        grid=(m // tm,),
        in_specs=[pl.BlockSpec((tm, d), lambda i: (i, 0)), _resident((1, d)), _resident((d, n)),
                  _resident((1, LANES))],
        out_specs=out_specs, out_shape=out_shape,
        compiler_params=_params("parallel"),
    )(x, norm_g, w, bias)


def _cmp_kernel(pt_ref, k_hbm, v_hbm, pek_ref, pev_ref, w1k_ref, w1v_ref, w2k_ref, w2v_ref,
                ok_ref, ov_ref, kbuf, vbuf, hk, hv, sem, *, n_pages_step):
    b = pl.program_id(0)
    s = pl.program_id(1)
    ns = pl.num_programs(1)
    t = b * ns + s
    total = pl.num_programs(0) * ns
    rows = n_pages_step * 8

    def copies(tt, slot):
        bb = tt // ns
        ss = tt % ns
        out = []
        for p in range(n_pages_step):
            page = pt_ref[bb, ss * n_pages_step + p]
            out.append(pltpu.make_async_copy(k_hbm.at[page], kbuf.at[slot, pl.ds(p * 8, 8)], sem.at[slot, 0]))
            out.append(pltpu.make_async_copy(v_hbm.at[page], vbuf.at[slot, pl.ds(p * 8, 8)], sem.at[slot, 1]))
        return out

    slot = t % 2

    @pl.when(t == 0)
    def _():
        for c in copies(t, slot):
            c.start()

    @pl.when(t + 1 < total)
    def _():
        for c in copies(t + 1, 1 - slot):
            c.start()

    for c in copies(t, slot):
        c.wait()

    r0 = pl.multiple_of(s * rows, rows)
    for buf, pe_ref, w1_ref, h_ref in ((kbuf, pek_ref, w1k_ref, hk), (vbuf, pev_ref, w1v_ref, hv)):
        x = buf[slot]
        for role in range(2):
            xr = (x + pe_ref[role]).astype(BF16)
            h_ref[role, pl.ds(r0, rows), :] = jnp.dot(xr, w1_ref[role], preferred_element_type=F32)

    @pl.when(s == ns - 1)
    def _():
        for h_ref, w2_ref, o_ref in ((hk, w2k_ref, ok_ref), (hv, w2v_ref, ov_ref)):
            n_chunk = h_ref.shape[1]
            hid = h_ref[0] + pltpu.roll(h_ref[1], n_chunk - 1, axis=0)
            res = jnp.dot(jax.nn.gelu(hid).astype(BF16), w2_ref[...], preferred_element_type=F32)
            for g in range(NSA_KV_HEADS):
                o_ref[0, g] = res[:, g * NSA_HEAD_DIM:(g + 1) * NSA_HEAD_DIM].astype(BF16)


def _cmp_weights(pe, w1, w2):
    g = NSA_KV_HEADS
    eye = jnp.eye(g, dtype=F32)
    pe_t = jnp.broadcast_to(pe.reshape(2, CMP_STRIDE, 1, NSA_HEAD_DIM), (2, CMP_STRIDE, g, NSA_HEAD_DIM))
    pe_t = pe_t.reshape(2, 1, CMP_STRIDE * g * NSA_HEAD_DIM)
    w1r = w1.reshape(2, CMP_STRIDE, NSA_HEAD_DIM, CMP_HIDDEN)
    w1b = jnp.einsum('rldj,gh->rlgdhj', w1r, eye).reshape(2, CMP_STRIDE * g * NSA_HEAD_DIM, g * CMP_HIDDEN)
    w2b = jnp.einsum('jd,gh->gjhd', w2, eye).reshape(g * CMP_HIDDEN, g * NSA_HEAD_DIM)
    return pe_t, w1b.astype(BF16), w2b.astype(BF16)


def _compress(page_table, k_pages, v_pages, cw):
    bsz, n_pages = page_table.shape
    p_step = min(CMP_PAGES, n_pages)
    ns = n_pages // p_step
    n_chunk = n_pages * 8
    kd = k_pages.shape[-1]
    gh = NSA_KV_HEADS * CMP_HIDDEN
    out_sds = SDS((bsz, NSA_KV_HEADS, n_chunk, NSA_HEAD_DIM), BF16)
    out_spec = pl.BlockSpec((1, NSA_KV_HEADS, n_chunk, NSA_HEAD_DIM), lambda b, s, pt: (b, 0, 0, 0))
    grid_spec = pltpu.PrefetchScalarGridSpec(
        num_scalar_prefetch=1, grid=(bsz, ns),
        in_specs=[pl.BlockSpec(memory_space=pl.ANY), pl.BlockSpec(memory_space=pl.ANY),
                  _resident((2, 1, kd)), _resident((2, 1, kd)),
                  _resident((2, kd, gh)), _resident((2, kd, gh)),
                  _resident((gh, NSA_KV)), _resident((gh, NSA_KV))],
        out_specs=[out_spec, out_spec],
        scratch_shapes=[pltpu.VMEM((2, p_step * 8, kd), F32), pltpu.VMEM((2, p_step * 8, kd), F32),
                        pltpu.VMEM((2, n_chunk, gh), F32), pltpu.VMEM((2, n_chunk, gh), F32),
                        pltpu.SemaphoreType.DMA((2, 2))])
    return pl.pallas_call(
        functools.partial(_cmp_kernel, n_pages_step=p_step),
        grid_spec=grid_spec, out_shape=[out_sds, out_sds],
        compiler_params=_params("arbitrary", "arbitrary"),
    )(page_table, k_pages, v_pages, cw['pe_k'], cw['pe_v'], cw['w1_k'], cw['w1_v'], cw['w2_k'], cw['w2_v'])


def _softmax_rows(s, mask):
    sm = jnp.where(mask, s, NEG)
    m = jnp.max(sm, axis=-1, keepdims=True)
    e = jnp.where(mask, jnp.exp(sm - m), 0.0)
    return e, jnp.sum(e, axis=-1, keepdims=True)


def _safe_inv(l):
    return jnp.where(l > 0.0, 1.0 / jnp.where(l > 0.0, l, 1.0), 0.0)


def _nsa_prompt_kernel(q_ref, kc_ref, vc_ref, ks_ref, vs_ref, kw_ref, vw_ref, gate_ref, exp_ref,
                       o_ref, sel_ref, m_ref, l_ref, acc_ref, *, n_cmp, n_slc, n_sel):
    tq, tk, hpg, dh = NSA_TQ, NSA_TK, NSA_HPG, NSA_HEAD_DIM
    rows = hpg * tq
    qi = pl.program_id(2)
    q0 = qi * tq
    q = q_ref[0].reshape(rows, dh)
    nt = (((1,), (1,)), ((), ()))
    pos1 = q0 + lax.broadcasted_iota(jnp.int32, (tq, LANES), 0)

    def tile4(x):
        return jnp.concatenate([x] * hpg, axis=0)

    lane = lax.broadcasted_iota(jnp.int32, (tq, LANES), 1)
    m_c = (lane < n_cmp) & (lane * CMP_STRIDE + (CMP_BLOCK - 1) <= pos1)
    s_c = lax.dot_general(q, kc_ref[0, 0], nt, preferred_element_type=F32)
    e_c, l_c = _softmax_rows(s_c, tile4(m_c))
    p_c = e_c * _safe_inv(l_c)
    o_c = jnp.dot(p_c.astype(BF16), vc_ref[0, 0], preferred_element_type=F32)

    psum = p_c[0:tq]
    for h in range(1, hpg):
        psum = psum + p_c[h * tq:(h + 1) * tq]
    imp = pltpu.roll(psum, 1, axis=1) + psum
    for o in range(1, SLC_BLOCK // CMP_STRIDE):
        imp = imp + pltpu.roll(psum, LANES - o, axis=1)
    r = SLC_BLOCK // CMP_STRIDE
    blk = lane // r
    is_blk = (lane % r == 0) & (blk < n_slc)
    valid = blk * SLC_BLOCK <= pos1
    forced = (blk == 0) | (blk == pos1 // SLC_BLOCK)
    score = jnp.where(is_blk & valid, imp + jnp.where(forced, FORCE, 0.0), NEG)
    rank = jnp.zeros((tq, LANES), F32)
    for k in range(n_slc):
        ck = score[:, r * k:r * k + 1]
        beats = (ck > score) | ((ck == score) & (lane > r * k))
        rank = rank + jnp.where(beats, 1.0, 0.0)
    sel = jnp.where(is_blk & (rank < n_sel), 1.0, 0.0).astype(BF16)
    sel_ref[...] = jnp.dot(sel, exp_ref[...], preferred_element_type=F32)

    def flash(k_ref, v_ref, lo, hi, mask_fn):
        m_ref[...] = jnp.full((rows, 1), NEG, F32)
        l_ref[...] = jnp.zeros((rows, 1), F32)
        acc_ref[...] = jnp.zeros((rows, dh), F32)

        def body(kb, carry):
            k0 = pl.multiple_of(kb * tk, tk)
            kk = k_ref[0, 0, 0, pl.ds(k0, tk), :]
            vv = v_ref[0, 0, 0, pl.ds(k0, tk), :]
            s = lax.dot_general(q, kk, nt, preferred_element_type=F32)
            mask = tile4(mask_fn(k0))
            sm = jnp.where(mask, s, NEG)
            m_prev = m_ref[...]
            m_new = jnp.maximum(m_prev, jnp.max(sm, axis=-1, keepdims=True))
            alpha = jnp.exp(m_prev - m_new)
            p = jnp.where(mask, jnp.exp(sm - m_new), 0.0)
            l_ref[...] = alpha * l_ref[...] + jnp.sum(p, axis=-1, keepdims=True)
            acc_ref[...] = alpha * acc_ref[...] + jnp.dot(p.astype(BF16), vv, preferred_element_type=F32)
            m_ref[...] = m_new
            return carry

        lax.fori_loop(lo, hi, body, 0)
        return acc_ref[...] * _safe_inv(l_ref[...])

    pos2 = q0 + lax.broadcasted_iota(jnp.int32, (tq, tk), 0)
    kio = lax.broadcasted_iota(jnp.int32, (tq, tk), 1)

    def slc_mask(k0):
        return (sel_ref[:, pl.ds(k0, tk)] > 0.5) & (k0 + kio <= pos2)

    def win_mask(k0):
        key = k0 + kio
        return (key <= pos2) & (key >= pos2 - WINDOW)

    hi = (q0 + tq - 1) // tk + 1
    o_s = flash(ks_ref, vs_ref, 0, hi, slc_mask)
    o_w = flash(kw_ref, vw_ref, jnp.maximum(q0 - WINDOW, 0) // tk, hi, win_mask)

    gates = gate_ref[0]
    outs = []
    for h in range(hpg):
        sl = slice(h * tq, (h + 1) * tq)
        outs.append(gates[:, h:h + 1] * o_c[sl] + gates[:, hpg + h:hpg + h + 1] * o_s[sl]
                    + gates[:, 2 * hpg + h:2 * hpg + h + 1] * o_w[sl])
    o_ref[...] = jnp.concatenate(outs, axis=-1).astype(BF16)


def _nsa_prompt(q_hm, kv_hm, kcmp, vcmp, gates_hm, bsz, t):
    g, hpg, dh, tq = NSA_KV_HEADS, NSA_HPG, NSA_HEAD_DIM, NSA_TQ
    assert t % NSA_TK == 0 and t // SLC_BLOCK * (SLC_BLOCK // CMP_STRIDE) <= LANES
    nq = t // tq
    n_cmp = kcmp.shape[2] - 1
    if kcmp.shape[2] < LANES:
        padw = ((0, 0), (0, 0), (0, LANES - kcmp.shape[2]), (0, 0))
        kcmp, vcmp = jnp.pad(kcmp, padw), jnp.pad(vcmp, padw)
    n_chunk = kcmp.shape[2]
    assert n_chunk == LANES
    n_slc = t // SLC_BLOCK
    q4 =q_hm.reshape(g, hpg, bsz * t, dh)
    kv6 = kv_hm.reshape(6, g, bsz, t, dh)
    key_blk = jnp.arange(t, dtype=jnp.int32)[None, :] // SLC_BLOCK
    lane = jnp.arange(LANES, dtype=jnp.int32)[:, None]
    r = SLC_BLOCK // CMP_STRIDE
    expand = ((lane % r == 0) & (lane // r == key_blk)).astype(BF16)

    def kv_spec(j):
        return pl.BlockSpec((1, 1, 1, t, dh), lambda b, gg, qi: (j, gg, b, 0, 0))

    cmp_spec = pl.BlockSpec((1, 1, n_chunk, dh), lambda b, gg, qi: (b, gg, 0, 0))
    rows = hpg * tq
    return pl.pallas_call(
        functools.partial(_nsa_prompt_kernel, n_cmp=n_cmp, n_slc=n_slc, n_sel=min(N_SELECT, n_slc)),
        grid=(bsz, g, nq),
        in_specs=[pl.BlockSpec((1, hpg, tq, dh), lambda b, gg, qi: (gg, 0, b * nq + qi, 0)),
                  cmp_spec, cmp_spec, kv_spec(2), kv_spec(3), kv_spec(4), kv_spec(5),
                  pl.BlockSpec((1, tq, LANES), lambda b, gg, qi: (gg, b * nq + qi, 0)),
                  _resident((LANES, t))],
        out_specs=pl.BlockSpec((tq, hpg * dh), lambda b, gg, qi: (b * nq + qi, gg)),
        out_shape=SDS((bsz * t, NSA_Q), BF16),
        scratch_shapes=[pltpu.VMEM((tq, t), F32), pltpu.VMEM((rows, 1), F32), pltpu.VMEM((rows, 1), F32),
                        pltpu.VMEM((rows, dh), F32)],
        compiler_params=_params("parallel", "parallel", "arbitrary"),
    )(q4, kcmp, vcmp, kv6, kv6, kv6, kv6, gates_hm, expand)


def _gla_prompt_kernel(q_ref, k_ref, v_ref, r_ref, misc_ref, wa_ref, ba_ref, ng_ref,
                       o_ref, s_ref, qe_ref, qt_ref, kt_ref, kh_ref, d_ref, u_ref, st_ref):
    t = q_ref.shape[0]
    c = GLA_CHUNK
    n = t // c
    dk, dv = GLA_DK, GLA_DV
    lr = misc_ref[:, N_GATE:N_GATE + GLA_RANK].astype(BF16)
    x = jnp.dot(lr, wa_ref[...], preferred_element_type=F32) + ba_ref[...]
    g = jax.nn.log_sigmoid(x) / GLA_TAU
    row = lax.broadcasted_iota(jnp.int32, (t, dk), 0) % c
    b = g
    sh = 1
    while sh < c:
        b = b + jnp.where(row >= sh, pltpu.roll(b, sh, axis=0), 0.0)
        sh *= 2
    b3 = b.reshape(n, c, dk)
    b_last = jnp.broadcast_to(b3[:, c - 1:c, :], (n, c, dk)).reshape(t, dk)
    b_mid = jnp.broadcast_to(b3[:, c // 2 - 1:c // 2, :], (n, c, dk)).reshape(t, dk)
    q = q_ref[...] * (GLA_DK ** -0.5)
    k = k_ref[...]
    qe_ref[...] = (q * jnp.exp(b)).astype(BF16)
    qt_ref[...] = (q * jnp.exp(b - b_mid)).astype(BF16)
    kt_ref[...] = (k * jnp.exp(b_mid - b)).astype(BF16)
    kh_ref[...] = (k * jnp.exp(b_last - b)).astype(BF16)
    d_ref[...] = jnp.exp(b_last)
    tn = (((0,), (0,)), ((), ()))
    nt = (((1,), (1,)), ((), ()))

    def chunk_update(i, carry):
        r0 = pl.multiple_of(i * c, c)
        u_ref[i] = lax.dot_general(v_ref[pl.ds(r0, c), :], kh_ref[pl.ds(r0, c), :], tn,
                                   preferred_element_type=F32)
        return carry

    lax.fori_loop(0, n, chunk_update, 0)

    st_ref[...] = jnp.zeros((dv, dk), F32)
    tril = lax.broadcasted_iota(jnp.int32, (c, c), 0) >= lax.broadcasted_iota(jnp.int32, (c, c), 1)
    ng = ng_ref[...]

    def chunk_out(i, carry):
        r0 = pl.multiple_of(i * c, c)
        st = st_ref[...]
        vv = v_ref[pl.ds(r0, c), :]
        o = lax.dot_general(qe_ref[pl.ds(r0, c), :], st.astype(BF16), nt, preferred_element_type=F32)
        a = lax.dot_general(qt_ref[pl.ds(r0, c), :], kt_ref[pl.ds(r0, c), :], nt, preferred_element_type=F32)
        a = jnp.where(tril, a, 0.0).astype(BF16)
        o = o + jnp.dot(a, vv, preferred_element_type=F32)
        o = _rms(o, ng) * jax.nn.silu(r_ref[pl.ds(r0, c), :])
        o_ref[pl.ds(r0, c), :] = o.astype(BF16)
        st_ref[...] = st * d_ref[pl.ds(r0, 1), :] + u_ref[i]
        return carry

    lax.fori_loop(0, n, chunk_out, 0)
    s_ref[0, 0] = st_ref[...].T


def _gla_prompt(q_l, k_l, v_l, r_l, misc, wa, ba, ng, bsz, t):
    h, dk, dv = GLA_HEADS, GLA_DK, GLA_DV
    assert t % GLA_CHUNK == 0
    n = t // GLA_CHUNK
    return pl.pallas_call(
        _gla_prompt_kernel,
        grid=(bsz, h),
        in_specs=[pl.BlockSpec((t, dk), lambda b, hh: (b, hh)), pl.BlockSpec((t, dk), lambda b, hh: (b, hh)),
                  pl.BlockSpec((t, dv), lambda b, hh: (b, hh)), pl.BlockSpec((t, dv), lambda b, hh: (b, hh)),
                  pl.BlockSpec((t, LANES), lambda b, hh: (b, 0)),
                  pl.BlockSpec((GLA_RANK, dk), lambda b, hh: (0, hh)), pl.BlockSpec((1, dk), lambda b, hh: (0, hh)),
                  _resident((1, dv))],
        out_specs=[pl.BlockSpec((t, dv), lambda b, hh: (b, hh)),
                   pl.BlockSpec((1, 1, dk, dv), lambda b, hh: (b, hh, 0, 0))],
        out_shape=[SDS((bsz * t, h * dv), BF16), SDS((bsz, h, dk, dv), F32)],
        scratch_shapes=[pltpu.VMEM((t, dk), BF16), pltpu.VMEM((t, dk), BF16), pltpu.VMEM((t, dk), BF16),
                        pltpu.VMEM((t, dk), BF16), pltpu.VMEM((t, dk), F32),
                        pltpu.VMEM((n, dv, dk), F32), pltpu.VMEM((dv, dk), F32)],
        compiler_params=_params("parallel", "parallel"),
    )(q_l, k_l, v_l, r_l, misc, wa, ba, ng)


def _merge_kernel(oa_ref, ob_ref, x_ref, g_ref, wg_ref, wa_ref, wb_ref, u_ref):
    h = _rms(x_ref[...], g_ref[...]).astype(BF16)
    d = D_MODEL
    ga = jax.nn.sigmoid(jnp.dot(h, wg_ref[:, 0:d], preferred_element_type=F32))
    u = ga * jnp.dot(oa_ref[...], wa_ref[...], preferred_element_type=F32)
    gb = jax.nn.sigmoid(jnp.dot(h, wg_ref[:, d:2 * d], preferred_element_type=F32))
    u = u + gb * jnp.dot(ob_ref[...], wb_ref[...], preferred_element_type=F32)
    u_ref[...] = u.astype(BF16)


def _merge(o_a, o_b, x, norm_g, w_gate, w_a, w_b, tm):
    m, d = x.shape
    assert m % tm == 0
    return pl.pallas_call(
        _merge_kernel, grid=(m // tm,),
        in_specs=[pl.BlockSpec((tm, NSA_Q), lambda i: (i, 0)), pl.BlockSpec((tm, GLA_HEADS * GLA_DV), lambda i: (i, 0)),
                  pl.BlockSpec((tm, d), lambda i: (i, 0)), _resident((1, d)), _resident((d, 2 * d)),
                  _resident(w_a.shape), _resident(w_b.shape)],
        out_specs=pl.BlockSpec((tm, d), lambda i: (i, 0)),
        out_shape=SDS((m, d), BF16),
        compiler_params=_params("parallel"),
    )(o_a, o_b, x, norm_g, w_gate, w_a, w_b)


def _mlp_kernel(x_ref, u_ref, wo_ref, g2_ref, wu_ref, wd_ref, gf_ref, y_ref, x1_ref, h_ref, acc_ref):
    j = pl.program_id(1)

    @pl.when(j == 0)
    def _():
        x1 = x_ref[...] + jnp.dot(u_ref[...], wo_ref[...], preferred_element_type=F32)
        x1_ref[...] = x1
        h_ref[...] = _rms(x1, g2_ref[...]).astype(BF16)
        acc_ref[...] = jnp.zeros_like(acc_ref)

    up = jnp.maximum(jnp.dot(h_ref[...], wu_ref[...], preferred_element_type=F32), 0.0)
    acc_ref[...] += jnp.dot((up * up).astype(BF16), wd_ref[...], preferred_element_type=F32)

    @pl.when(j == pl.num_programs(1) - 1)
    def _():
        y_ref[...] = _rms(x1_ref[...] + acc_ref[...], gf_ref[...])


def _mlp(x, u, w_o, g2, w_up, w_down, gf, tm, tf):
    m, d = x.shape
    ff = w_up.shape[1]
    assert m % tm == 0 and ff % tf == 0
    return pl.pallas_call(
        _mlp_kernel, grid=(m // tm, ff // tf),
        in_specs=[pl.BlockSpec((tm, d), lambda i, j: (i, 0)), pl.BlockSpec((tm, d), lambda i, j: (i, 0)),
                  _resident((d, d)), _resident((1, d)),
                  pl.BlockSpec((d, tf), lambda i, j: (0, j)), pl.BlockSpec((tf, d), lambda i, j: (j, 0)),
                  _resident((1, d))],
        out_specs=pl.BlockSpec((tm, d), lambda i, j: (i, 0)),
        out_shape=SDS((m, d), F32),
        scratch_shapes=[pltpu.VMEM((tm, d), F32), pltpu.VMEM((tm, d), BF16), pltpu.VMEM((tm, d), F32)],
        compiler_params=_params("parallel", "arbitrary"),
    )(x, u, w_o, g2, w_up, w_down, gf)


def _prep_weights(norm1_g, w_in, b_nsa_gate, cmp_pe_k, cmp_pe_v, cmp_k_w1, cmp_k_w2, cmp_v_w1, cmp_v_w2,
                  gla_w_a2, gla_b_a, gla_norm_g, w_br_a, w_br_b, w_o, norm2_g, w_up, w_down, norm_f):
    pts = [0]
    for s in SPLITS:
        pts.append(pts[-1] + s)
    c_q, c_kv, c_g, c_ql, c_kl, c_vl, c_rl, c_lr, c_br, c_end = pts
    gcols = jnp.asarray([c_g + (g * NSA_HPG + h) * 3 + c for g in range(NSA_KV_HEADS)
                         for c in range(3) for h in range(NSA_HPG)], jnp.int32)
    pad = jnp.zeros((D_MODEL, LANES - N_GATE - GLA_RANK), F32)
    w_misc = jnp.concatenate([w_in[:, gcols], w_in[:, c_lr:c_br], pad], axis=1)
    b_misc = jnp.concatenate([b_nsa_gate[gcols - c_g], jnp.zeros((LANES - N_GATE,), F32)])[None, :]
    w = dict(
        norm1=norm1_g[None, :], norm2=norm2_g[None, :], norm_f=norm_f[None, :],
        w_a=w_in[:, c_q:c_g].astype(BF16),
        w_b=jnp.concatenate([w_in[:, c_ql:c_lr], w_misc], axis=1).astype(BF16),
        w_gate=w_in[:, c_br:c_end].astype(BF16),
        b_misc=b_misc,
        gla_wa=gla_w_a2.astype(BF16), gla_ba=gla_b_a[None, :], gla_ng=gla_norm_g[None, :],
        w_br_a=w_br_a.astype(BF16), w_br_b=w_br_b.astype(BF16), w_o=w_o.astype(BF16),
        w_up=w_up.astype(BF16), w_down=w_down.astype(BF16),
    )
    cw = {}
    cw['pe_k'], cw['w1_k'], cw['w2_k'] = _cmp_weights(cmp_pe_k, cmp_k_w1, cmp_k_w2)
    cw['pe_v'], cw['w1_v'], cw['w2_v'] = _cmp_weights(cmp_pe_v, cmp_v_w1, cmp_v_w2)
    w['cmp'] = cw
    return w


_KV6 = 6 * NSA_KV
_GQ = GLA_HEADS * GLA_DK
_GV = GLA_HEADS * GLA_DV


def _layer_prompt(x, w):
    bsz, t, d = x.shape
    m = bsz * t
    x2 = x.reshape(m, d)
    zb = jnp.zeros((1, LANES), F32)
    spec_a = [('hm', 0, NSA_Q, NSA_HEAD_DIM ** -0.5)]
    spec_a += [('f32', NSA_Q + j * NSA_KV, NSA_Q + (j + 1) * NSA_KV, 1.0) for j in range(6)]
    spec_a += [('hm', NSA_Q, NSA_Q + _KV6, 1.0)]
    q_hm, r0, r1, r2, r3, r4, r5, kv_hm = _proj(x2, w['norm1'], w['w_a'], zb, spec_a, 512)
    spec_b = [('f32', 0, _GQ, 1.0), ('f32', _GQ, 2 * _GQ, 1.0), ('bf16', 2 * _GQ, 2 * _GQ + _GV, 1.0),
              ('f32', 2 * _GQ + _GV, 2 * _GQ + 2 * _GV, 1.0),
              ('misc', 2 * _GQ + 2 * _GV, 2 * _GQ + 2 * _GV + LANES, 1.0), ('gates', 0, 0, 1.0)]
    q_l, k_l, v_l, r_l, misc, gates_hm = _proj(x2, w['norm1'], w['w_b'], w['b_misc'], spec_b, 512)

    n_pages = t // PAGE_SIZE
    ident = jnp.arange(bsz * n_pages, dtype=jnp.int32).reshape(bsz, n_pages)
    kd = CMP_STRIDE * NSA_KV
    kcmp, vcmp = _compress(ident, r0.reshape(bsz * n_pages, 8, kd), r1.reshape(bsz * n_pages, 8, kd), w['cmp'])
    o_a = _nsa_prompt(q_hm, kv_hm, kcmp, vcmp, gates_hm, bsz, t)
    o_b, s_new = _gla_prompt(q_l, k_l, v_l, r_l, misc, w['gla_wa'], w['gla_ba'], w['gla_ng'], bsz, t)
    u = _merge(o_a, o_b, x2, w['norm1'], w['w_gate'], w['w_br_a'], w['w_br_b'], 512)
    y = _mlp(x2, u, w['w_o'], w['norm2'], w['w_up'], w['w_down'], w['norm_f'], 512, 512)
    kvh = (bsz, t, NSA_KV_HEADS, NSA_HEAD_DIM)
    rows = [a.reshape(kvh) for a in (r0, r1, r2, r3)]
    n_keep = min(WINDOW, t)
    wins = [a.reshape(kvh)[:, t - n_keep:] for a in (r4, r5)]
    return y.reshape(bsz, t, d), rows, wins, s_new


def _group_rows(parts):
    rowg = lax.broadcasted_iota(jnp.int32, parts[0].shape, 0) // NSA_HPG
    out = parts[0]
    for g in range(1, NSA_KV_HEADS):
        out = jnp.where(rowg == g, parts[g], out)
    return out


def _sample_select_kernel(q_ref, kc_ref, vc_ref, oc_ref, idx_ref, *, n_cmp, pos, n_pick):
    q = q_ref[0]
    nt = (((1,), (1,)), ((), ()))
    n_chunk = kc_ref.shape[2]
    s = _group_rows([lax.dot_general(q, kc_ref[0, g], nt, preferred_element_type=F32)
                     for g in range(NSA_KV_HEADS)])
    lane = lax.broadcasted_iota(jnp.int32, s.shape, 1)
    mask = (lane < n_cmp) & (lane * CMP_STRIDE + (CMP_BLOCK - 1) <= pos)
    e, l = _softmax_rows(s, mask)
    p = e * _safe_inv(l)
    pb = p.astype(BF16)
    oc_ref[0] = _group_rows([jnp.dot(pb, vc_ref[0, g], preferred_element_type=F32)
                             for g in range(NSA_KV_HEADS)])
    nr = p.shape[0]
    y = p + pltpu.roll(p, nr - 1, axis=0)
    psum = y + pltpu.roll(y, nr - 2, axis=0)
    imp = pltpu.roll(psum, 1, axis=1) + psum
    r = SLC_BLOCK // CMP_STRIDE
    for o in range(1, r):
        imp = imp + pltpu.roll(psum, n_chunk - o, axis=1)
    blk = lane // r
    is_blk = lane % r == 0
    valid = blk * SLC_BLOCK <= pos
    forced = (blk == 0) | (blk == pos // SLC_BLOCK)
    score = jnp.where(is_blk & valid, imp + jnp.where(forced, FORCE, 0.0), NEG)
    lane_f = lane.astype(F32)
    out_lane = lax.broadcasted_iota(jnp.int32, (nr, LANES), 1)
    picked = jnp.zeros((nr, LANES), F32)
    for k in range(n_pick):
        mx = jnp.max(score, axis=-1, keepdims=True)
        ix = jnp.min(jnp.where(score == mx, lane_f, float(n_chunk)), axis=-1, keepdims=True)
        picked = jnp.where(out_lane == k, ix, picked)
        score = jnp.where(lane_f == ix, 2.0 * NEG, score)
    idx_ref[0] = picked.astype(jnp.int32) // r


def _sample_select(q_s, kcmp, vcmp, pos, n_pick):
    bsz, g, n_chunk, dh = kcmp.shape
    nh = NSA_HEADS
    cmp_spec = pl.BlockSpec((1, g, n_chunk, dh), lambda b: (b, 0, 0, 0))
    return pl.pallas_call(
        functools.partial(_sample_select_kernel, n_cmp=n_chunk - 1, pos=pos, n_pick=n_pick),
        grid=(bsz,),
        in_specs=[pl.BlockSpec((1, nh, dh), lambda b: (b, 0, 0)), cmp_spec, cmp_spec],
        out_specs=[pl.BlockSpec((1, nh, dh), lambda b: (b, 0, 0)), pl.BlockSpec((1, nh, LANES), lambda b: (b, 0, 0))],
        out_shape=[SDS((bsz, nh, dh), F32), SDS((bsz, nh, LANES), jnp.int32)],
        compiler_params=_params("parallel"),
    )(q_s, kcmp, vcmp)


def _sample_attend_kernel(idx_ref, pt_ref, q_ref, oc_ref, ks_hbm, vs_hbm, kw_ref, vw_ref,
                          nks_ref, nvs_ref, nkw_ref, nvw_ref, gate_ref, o_ref, ksel, vsel, sem, *, n_pick):
    b = pl.program_id(0)
    g_n, dh = NSA_KV_HEADS, NSA_HEAD_DIM
    half = PAGE_SIZE // SLC_BLOCK

    def copies():
        out = []
        for g in range(g_n):
            for r in range(n_pick):
                blk = idx_ref[b * g_n + g, r]
                page = pt_ref[b, blk // half]
                off = (blk % half) * SLC_BLOCK
                dst = pl.ds(r * SLC_BLOCK, SLC_BLOCK)
                out.append(pltpu.make_async_copy(ks_hbm.at[page, pl.ds(off, SLC_BLOCK), g, :],
                                                 ksel.at[g, dst, :], sem.at[0]))
                out.append(pltpu.make_async_copy(vs_hbm.at[page, pl.ds(off, SLC_BLOCK), g, :],
                                                 vsel.at[g, dst, :], sem.at[1]))
        return out

    for c in copies():
        c.start()

    q = q_ref[0]
    qf = q.astype(F32)
    nt = (((1,), (1,)), ((), ()))

    def attend(keys, vals, k_new, v_new):
        s = _group_rows([lax.dot_general(q, keys[g].astype(BF16), nt, preferred_element_type=F32)
                         for g in range(g_n)])
        s_new = _group_rows([jnp.sum(qf * k_new[:, g * dh:(g + 1) * dh], axis=-1, keepdims=True)
                             for g in range(g_n)])
        m = jnp.maximum(jnp.max(s, axis=-1, keepdims=True), s_new)
        e = jnp.exp(s - m)
        e_new = jnp.exp(s_new - m)
        l = jnp.sum(e, axis=-1, keepdims=True) + e_new
        eb = e.astype(BF16)
        acc = _group_rows([jnp.dot(eb, vals[g].astype(BF16), preferred_element_type=F32)
                           + e_new * v_new[:, g * dh:(g + 1) * dh] for g in range(g_n)])
        return acc / l

    kw = kw_ref[0]
    vw = vw_ref[0]
    o_w = attend([kw[:, g * dh:(g + 1) * dh] for g in range(g_n)],
                 [vw[:, g * dh:(g + 1) * dh] for g in range(g_n)], nkw_ref[0], nvw_ref[0])
    for c in copies():
        c.wait()
    o_s = attend([ksel[g] for g in range(g_n)], [vsel[g] for g in range(g_n)], nks_ref[0], nvs_ref[0])

    gt = jnp.broadcast_to(gate_ref[0], (LANES, LANES)).T
    nh = NSA_HEADS
    o_ref[0] = (gt[0:nh, 0:dh] * oc_ref[0] + gt[nh:2 * nh, 0:dh] * o_s + gt[2 * nh:3 * nh, 0:dh] * o_w)


def _sample_attend(idx, page_table, q_s, o_c, slc_k, slc_v, win_k, win_v, new_rows, gates, n_pick):
    bsz = page_table.shape[0]
    nh, dh, g = NSA_HEADS, NSA_HEAD_DIM, NSA_KV_HEADS
    wlen = win_k.shape[1]
    row_spec = pl.BlockSpec((1, 1, NSA_KV), lambda b, *_: (b, 0, 0))
    win_spec = pl.BlockSpec((1, wlen, NSA_KV), lambda b, *_: (b, 0, 0))
    head_spec = pl.BlockSpec((1, nh, dh), lambda b, *_: (b, 0, 0))
    any_spec = pl.BlockSpec(memory_space=pl.ANY)
    grid_spec = pltpu.PrefetchScalarGridSpec(
        num_scalar_prefetch=2, grid=(bsz,),
        in_specs=[head_spec, head_spec, any_spec, any_spec, win_spec, win_spec,
                  row_spec, row_spec, row_spec, row_spec, pl.BlockSpec((1, 1, LANES), lambda b, *_: (b, 0, 0))],
        out_specs=head_spec,
        scratch_shapes=[pltpu.VMEM((g, n_pick * SLC_BLOCK, dh), F32), pltpu.VMEM((g, n_pick * SLC_BLOCK, dh), F32),
                        pltpu.SemaphoreType.DMA((2,))])
    return pl.pallas_call(
        functools.partial(_sample_attend_kernel, n_pick=n_pick),
        grid_spec=grid_spec, out_shape=SDS((bsz, nh, dh), F32),
        compiler_params=_params("arbitrary"),
    )(idx, page_table, q_s, o_c, slc_k, slc_v, win_k, win_v, *new_rows, gates)


def _gla_sample_kernel(q_ref, k_ref, v_ref, r_ref, misc_ref, wa_ref, ba_ref, ng_ref, s0_ref, o_ref, s_ref):
    dk, dv = GLA_DK, GLA_DV
    lr = jnp.broadcast_to(misc_ref[0][:, N_GATE:N_GATE + GLA_RANK], (16, GLA_RANK)).astype(BF16)
    x = jnp.dot(lr, wa_ref[...], preferred_element_type=F32)[0:1] + ba_ref[...]
    g_all = jax.nn.log_sigmoid(x) / GLA_TAU

    def col(v):
        t = jnp.broadcast_to(v, (dk, dk)).T
        return jnp.concatenate([t] * (dv // dk), axis=1)

    outs = []
    for h in range(GLA_HEADS):
        g = g_all[:, h * dk:(h + 1) * dk]
        q = q_ref[0][:, h * dk:(h + 1) * dk] * (GLA_DK ** -0.5)
        k = k_ref[0][:, h * dk:(h + 1) * dk]
        v = v_ref[0][:, h * dv:(h + 1) * dv]
        s0 = s0_ref[0, h]
        q_t = q * jnp.exp(g)
        k_t = k * jnp.exp(-g)
        a = jnp.sum(q_t * k_t, axis=-1, keepdims=True)
        o = jnp.sum(col(q_t) * s0, axis=0, keepdims=True) + a * v
        s_ref[0, h] = col(jnp.exp(g)) * s0 + col(k) * v
        outs.append(_rms(o, ng_ref[...]) * jax.nn.silu(r_ref[0][:, h * dv:(h + 1) * dv]))
    o_ref[0] = jnp.concatenate(outs, axis=-1)


def _gla_sample(q_l, k_l, v_l, r_l, misc, wa, ba, ng, s0):
    bsz, h, dk, dv = s0.shape

    def row(n):
        return pl.BlockSpec((1, 1, n), lambda b: (b, 0, 0))

    st_spec = pl.BlockSpec((1, h, dk, dv), lambda b: (b, 0, 0, 0))
    return pl.pallas_call(
        _gla_sample_kernel, grid=(bsz,),
        in_specs=[row(h * dk), row(h * dk), row(h * dv), row(h * dv), row(LANES),
                  _resident(wa.shape), _resident(ba.shape), _resident(ng.shape), st_spec],
        out_specs=[row(h * dv), st_spec],
        out_shape=[SDS((bsz, 1, h * dv), F32), SDS((bsz, h, dk, dv), F32)],
        compiler_params=_params("parallel"),
    )(q_l, k_l, v_l, r_l, misc, wa, ba, ng, s0)


SAMPLE_ROWS = 16


def _layer_sample(x, caches, wins, s0, page_table, w):
    bsz, t, d = x.shape
    n_pages = page_table.shape[1]
    pos = n_pages * PAGE_SIZE
    assert t == 1 and bsz <= SAMPLE_ROWS and pos % SLC_BLOCK == 0 and wins[0].shape[1] == WINDOW
    assert pos // SLC_BLOCK >= N_SELECT
    mp = SAMPLE_ROWS
    x2 = jnp.pad(x.reshape(bsz, d), ((0, mp - bsz), (0, 0)))
    zb = jnp.zeros((1, LANES), F32)
    spec_a = [('bf16', 0, NSA_Q, NSA_HEAD_DIM ** -0.5)]
    spec_a += [('f32', NSA_Q + j * NSA_KV, NSA_Q + (j + 1) * NSA_KV, 1.0) for j in range(6)]
    q_s, r0, r1, r2, r3, r4, r5 = _proj(x2, w['norm1'], w['w_a'], zb, spec_a, mp)
    spec_b = [('f32', 0, _GQ, 1.0), ('f32', _GQ, 2 * _GQ, 1.0), ('f32', 2 * _GQ, 2 * _GQ + _GV, 1.0),
              ('f32', 2 * _GQ + _GV, 2 * _GQ + 2 * _GV, 1.0),
              ('misc', 2 * _GQ + 2 * _GV, 2 * _GQ + 2 * _GV + LANES, 1.0)]
    q_l, k_l, v_l, r_l, misc = _proj(x2, w['norm1'], w['w_b'], w['b_misc'], spec_b, mp)

    n_pool = caches[0].shape[0]
    kd = CMP_STRIDE * NSA_KV
    kcmp, vcmp = _compress(page_table, caches[0].reshape(n_pool, 8, kd), caches[1].reshape(n_pool, 8, kd), w['cmp'])
    q_h = q_s.reshape(mp, NSA_HEADS, NSA_HEAD_DIM)
    n_pick = N_SELECT - 1
    o_c, idx = _sample_select(q_h, kcmp, vcmp, pos, n_pick)
    idx = idx[:, ::NSA_HPG, :n_pick].reshape(bsz * NSA_KV_HEADS, n_pick)
    gates = misc[:, :N_GATE].reshape(mp, NSA_KV_HEADS, 3, NSA_HPG).transpose(0, 2, 1, 3).reshape(mp, 1, N_GATE)
    gates = jnp.pad(gates, ((0, 0), (0, 0), (0, LANES - N_GATE)))
    new_rows = [a.reshape(mp, 1, NSA_KV) for a in (r2, r3, r4, r5)]
    win2 = [a.reshape(bsz, WINDOW, NSA_KV) for a in wins]
    o_a = _sample_attend(idx, page_table, q_h, o_c, caches[2], caches[3], win2[0], win2[1], new_rows, gates, n_pick)
    o_a = jnp.pad(o_a.reshape(bsz, NSA_Q), ((0, mp - bsz), (0, 0))).astype(BF16)

    def r3d(a):
        return a.reshape(mp, 1, a.shape[-1])

    o_b, s_new = _gla_sample(r3d(q_l), r3d(k_l), r3d(v_l), r3d(r_l), r3d(misc), w['gla_wa'], w['gla_ba'],
                             w['gla_ng'], s0)
    o_b = jnp.pad(o_b.reshape(bsz, _GV), ((0, mp - bsz), (0, 0))).astype(BF16)
    u = _merge(o_a, o_b, x2, w['norm1'], w['w_gate'], w['w_br_a'], w['w_br_b'], mp)
    y = _mlp(x2, u, w['w_o'], w['norm2'], w['w_up'], w['w_down'], w['norm_f'], mp, 512)
    kvh = (bsz, 1, NSA_KV_HEADS, NSA_HEAD_DIM)
    rows = [a[:bsz].reshape(kvh) for a in (r0, r1, r2, r3)]
    new_wins = [jnp.concatenate([c[:, 1:], a[:bsz].reshape(kvh)], axis=1) for c, a in zip(wins, (r4, r5))]
    return y[:bsz].reshape(bsz, 1, d), rows, new_wins, s_new


def kernel(x_prompt, x_sample, cache_cmp_k, cache_cmp_v, cache_slc_k, cache_slc_v, cache_win_k, cache_win_v, state_gla, page_table, norm1_g, w_in, b_nsa_gate, cmp_pe_k, cmp_pe_v, cmp_k_w1, cmp_k_w2, cmp_v_w1, cmp_v_w2, gla_w_a2, gla_b_a, gla_norm_g, w_br_a, w_br_b, w_o, norm2_g, w_up, w_down, norm_f):
    assert DEPTH == 1 and norm1_g.shape[0] == 1
    w = _prep_weights(norm1_g[0], w_in[0], b_nsa_gate[0], cmp_pe_k[0], cmp_pe_v[0], cmp_k_w1[0], cmp_k_w2[0],
                      cmp_v_w1[0], cmp_v_w2[0], gla_w_a2[0], gla_b_a[0], gla_norm_g[0], w_br_a[0], w_br_b[0],
                      w_o[0], norm2_g[0], w_up[0], w_down[0], norm_f)
    y_p, rows_p, wins_p, s_p = _layer_prompt(x_prompt, w)
    caches = [c[0] for c in (cache_cmp_k, cache_cmp_v, cache_slc_k, cache_slc_v)]
    y_s, rows_s, wins_s, s_s = _layer_sample(x_sample, caches, [cache_win_k[0], cache_win_v[0]], state_gla[0],
                                             page_table, w)
    outs_p = [a[None] for a in rows_p + wins_p + [s_p]]
    outs_s = [a[None] for a in rows_s + wins_s + [s_s]]
    return (y_p, y_s, *outs_p, *outs_s)
```

```python
import functools

import jax
import jax.numpy as jnp
from jax import lax
from jax.experimental import pallas as pl
from jax.experimental.pallas import tpu as pltpu

D_MODEL = 2048
DEPTH = 1
PAGE_SIZE = 128
NSA_HEADS = 16
NSA_KV_HEADS = 4
NSA_HPG = NSA_HEADS // NSA_KV_HEADS
NSA_HEAD_DIM = 64
NSA_Q = NSA_HEADS * NSA_HEAD_DIM
NSA_KV = NSA_KV_HEADS * NSA_HEAD_DIM
CMP_STRIDE = 16
CMP_BLOCK = 32
CMP_HIDDEN = 128
SLC_BLOCK = 64
N_SELECT = 16
WINDOW = 512
GLA_HEADS = 4
GLA_DK = (D_MODEL // 4) // GLA_HEADS
GLA_DV = (D_MODEL // 2) // GLA_HEADS
GLA_RANK = 16
GLA_TAU = 16.0
D_FF = 4 * D_MODEL
EPS = 1e-6
NEG = -1e30
FORCE = 1e4
SPLITS = (NSA_Q, 6 * NSA_KV, 3 * NSA_HEADS,
          GLA_HEADS * GLA_DK, GLA_HEADS * GLA_DK, GLA_HEADS * GLA_DV, GLA_HEADS * GLA_DV,
          GLA_RANK, 2 * D_MODEL)

F32 = jnp.float32
BF16 = jnp.bfloat16
LANES = 128
VMEM_LIMIT_BYTES = 56 * 1024 * 1024
N_GATE = 3 * NSA_HEADS
GLA_CHUNK = 32
NSA_TQ = 128
NSA_TK = 256
CMP_PAGES = 16
SDS = jax.ShapeDtypeStruct


def _params(*sem):
    return pltpu.CompilerParams(dimension_semantics=sem, vmem_limit_bytes=VMEM_LIMIT_BYTES)


def _resident(shape):
    nd = len(shape)
    return pl.BlockSpec(shape, lambda *_: (0,) * nd, pipeline_mode=pl.Buffered(1))


def _rms(x, g):
    return x * lax.rsqrt(jnp.mean(x * x, axis=-1, keepdims=True) + EPS) * g


def _proj_kernel(x_ref, g_ref, w_ref, b_ref, *out_refs, spec):
    h = _rms(x_ref[...], g_ref[...]).astype(BF16)
    misc = None
    for o_ref, (kind, c0, c1, scale) in zip(out_refs, spec):
        if kind == 'gates':
            for g in range(NSA_KV_HEADS):
                o_ref[g] = misc if g == 0 else pltpu.roll(misc, LANES - g * 3 * NSA_HPG, axis=1)
            continue
        r = jnp.dot(h, w_ref[:, c0:c1], preferred_element_type=F32)
        if scale != 1.0:
            r = r * scale
        if kind == 'f32':
            o_ref[...] = r
        elif kind == 'bf16':
            o_ref[...] = r.astype(BF16)
        elif kind == 'sigmoid':
            o_ref[...] = jax.nn.sigmoid(r)
        elif kind == 'hm':
            for i in range((c1 - c0) // NSA_HEAD_DIM):
                o_ref[i] = r[:, i * NSA_HEAD_DIM:(i + 1) * NSA_HEAD_DIM].astype(BF16)
        elif kind == 'hmv':
            lane = lax.broadcasted_iota(jnp.int32, (r.shape[0], LANES - NSA_HEAD_DIM), 1)
            ones = jnp.where(lane == 0, 1.0, 0.0)
            for i in range((c1 - c0) // NSA_HEAD_DIM):
                piece = r[:, i * NSA_HEAD_DIM:(i + 1) * NSA_HEAD_DIM]
                o_ref[i] = jnp.concatenate([piece, ones], axis=1).astype(BF16)
        elif kind == 'misc':
            lane = lax.broadcasted_iota(jnp.int32, r.shape, 1)
            misc = jnp.where(lane < N_GATE, jax.nn.sigmoid(r + b_ref[...]), r)
            o_ref[...] = misc


def _proj(x, norm_g, w, bias, spec, tm):
    m, d = x.shape
    n = w.shape[1]
    assert m % tm == 0
    out_shape, out_specs = [], []
    for kind, c0, c1, _ in spec:
        if kind in ('hm', 'hmv'):
            nh = (c1 - c0) // NSA_HEAD_DIM
            width = NSA_HEAD_DIM if kind == 'hm' else LANES
            out_shape.append(SDS((nh, m, width), BF16))
            out_specs.append(pl.BlockSpec((nh, tm, width), lambda i: (0, i, 0)))
        elif kind == 'gates':
            out_shape.append(SDS((NSA_KV_HEADS, m, LANES), F32))
            out_specs.append(pl.BlockSpec((NSA_KV_HEADS, tm, LANES), lambda i: (0, i, 0)))
        else:
            out_shape.append(SDS((m, c1 - c0), BF16 if kind == 'bf16' else F32))
            out_specs.append(pl.BlockSpec((tm, c1 - c0), lambda i: (i, 0)))
    return pl.pallas_call(
        functools.partial(_proj_kernel, spec=tuple(spec)),
        grid=(m // tm,),
        in_specs=[pl.BlockSpec((tm, d), lambda i: (i, 0)), _resident((1, d)), _resident((d, n)),
                  _resident((1, LANES))],
        out_specs=out_specs, out_shape=out_shape,
        compiler_params=_params("parallel"),
    )(x, norm_g, w, bias)


def _cmp_kernel(pt_ref, k_hbm, v_hbm, pek_ref, pev_ref, w1k_ref, w1v_ref, w2k_ref, w2v_ref,
                ok_ref, ov_ref, kbuf, vbuf, hk, hv, sem, *, n_pages_step):
    b = pl.program_id(0)
    s = pl.program_id(1)
    ns = pl.num_programs(1)
    t = b * ns + s
    total = pl.num_programs(0) * ns
    rows = n_pages_step * 8

    def copies(tt, slot):
        bb = tt // ns
        ss = tt % ns
        out = []
        for p in range(n_pages_step):
            page = pt_ref[bb, ss * n_pages_step + p]
            out.append(pltpu.make_async_copy(k_hbm.at[page], kbuf.at[slot, pl.ds(p * 8, 8)], sem.at[slot, 0]))
            out.append(pltpu.make_async_copy(v_hbm.at[page], vbuf.at[slot, pl.ds(p * 8, 8)], sem.at[slot, 1]))
        return out

    slot = t % 2

    @pl.when(t == 0)
    def _():
        for c in copies(t, slot):
            c.start()

    @pl.when(t + 1 < total)
    def _():
        for c in copies(t + 1, 1 - slot):
            c.start()

    for c in copies(t, slot):
        c.wait()

    r0 = pl.multiple_of(s * rows, rows)
    for buf, pe_ref, w1_ref, h_ref in ((kbuf, pek_ref, w1k_ref, hk), (vbuf, pev_ref, w1v_ref, hv)):
        x = buf[slot]
        for role in range(2):
            xr = (x + pe_ref[role]).astype(BF16)
            h_ref[role, pl.ds(r0, rows), :] = jnp.dot(xr, w1_ref[role], preferred_element_type=F32)

    @pl.when(s == ns - 1)
    def _():
        for h_ref, w2_ref, o_ref in ((hk, w2k_ref, ok_ref), (hv, w2v_ref, ov_ref)):
            n_chunk = h_ref.shape[1]
            hid = h_ref[0] + pltpu.roll(h_ref[1], n_chunk - 1, axis=0)
            res = jnp.dot(jax.nn.gelu(hid).astype(BF16), w2_ref[...], preferred_element_type=F32)
            for g in range(NSA_KV_HEADS):
                o_ref[0, g] = res[:, g * NSA_HEAD_DIM:(g + 1) * NSA_HEAD_DIM].astype(BF16)


def _cmp_weights(pe, w1, w2):
    g = NSA_KV_HEADS
    eye = jnp.eye(g, dtype=F32)
    pe_t = jnp.broadcast_to(pe.reshape(2, CMP_STRIDE, 1, NSA_HEAD_DIM), (2, CMP_STRIDE, g, NSA_HEAD_DIM))
    pe_t = pe_t.reshape(2, 1, CMP_STRIDE * g * NSA_HEAD_DIM)
    w1r = w1.reshape(2, CMP_STRIDE, NSA_HEAD_DIM, CMP_HIDDEN)
    w1b = jnp.einsum('rldj,gh->rlgdhj', w1r, eye).reshape(2, CMP_STRIDE * g * NSA_HEAD_DIM, g * CMP_HIDDEN)
    w2b = jnp.einsum('jd,gh->gjhd', w2, eye).reshape(g * CMP_HIDDEN, g * NSA_HEAD_DIM)
    return pe_t, w1b.astype(BF16), w2b.astype(BF16)


def _compress(page_table, k_pages, v_pages, cw):
    bsz, n_pages = page_table.shape
    p_step = min(CMP_PAGES, n_pages)
    ns = n_pages // p_step
    n_chunk = n_pages * 8
    kd = k_pages.shape[-1]
    gh = NSA_KV_HEADS * CMP_HIDDEN
    out_sds = SDS((bsz, NSA_KV_HEADS, n_chunk, NSA_HEAD_DIM), BF16)
    out_spec = pl.BlockSpec((1, NSA_KV_HEADS, n_chunk, NSA_HEAD_DIM), lambda b, s, pt: (b, 0, 0, 0))
    grid_spec = pltpu.PrefetchScalarGridSpec(
        num_scalar_prefetch=1, grid=(bsz, ns),
        in_specs=[pl.BlockSpec(memory_space=pl.ANY), pl.BlockSpec(memory_space=pl.ANY),
                  _resident((2, 1, kd)), _resident((2, 1, kd)),
                  _resident((2, kd, gh)), _resident((2, kd, gh)),
                  _resident((gh, NSA_KV)), _resident((gh, NSA_KV))],
        out_specs=[out_spec, out_spec],
        scratch_shapes=[pltpu.VMEM((2, p_step * 8, kd), F32), pltpu.VMEM((2, p_step * 8, kd), F32),
                        pltpu.VMEM((2, n_chunk, gh), F32), pltpu.VMEM((2, n_chunk, gh), F32),
                        pltpu.SemaphoreType.DMA((2, 2))])
    return pl.pallas_call(
        functools.partial(_cmp_kernel, n_pages_step=p_step),
        grid_spec=grid_spec, out_shape=[out_sds, out_sds],
        compiler_params=_params("arbitrary", "arbitrary"),
    )(page_table, k_pages, v_pages, cw['pe_k'], cw['pe_v'], cw['w1_k'], cw['w1_v'], cw['w2_k'], cw['w2_v'])


def _softmax_rows(s, mask):
    sm = jnp.where(mask, s, NEG)
    m = jnp.max(sm, axis=-1, keepdims=True)
    e = jnp.where(mask, jnp.exp(sm - m), 0.0)
    return e, jnp.sum(e, axis=-1, keepdims=True)


def _safe_inv(l):
    return jnp.where(l > 0.0, 1.0 / jnp.where(l > 0.0, l, 1.0), 0.0)


M_INIT = -1e20


def _nsa_prompt_kernel(q_ref, kc_ref, vc_ref, ks_ref, vs_ref, kw_ref, vw_ref, gate_ref, exp_ref,
                       o_ref, bias_ref, rk_ref, m_ref, acc_ref, *, n_cmp, n_slc, n_sel):
    tq, tk, hpg, dh = NSA_TQ, NSA_TK, NSA_HPG, NSA_HEAD_DIM
    rows = hpg * tq
    qi = pl.program_id(2)
    q0 = qi * tq
    q = q_ref[0].reshape(rows, dh)
    nt = (((1,), (1,)), ((), ()))
    pos1 = q0 + lax.broadcasted_iota(jnp.int32, (tq, LANES), 0)

    def tile4(x):
        return jnp.concatenate([x] * hpg, axis=0)

    lane = lax.broadcasted_iota(jnp.int32, (tq, LANES), 1)
    m_c = (lane < n_cmp) & (lane * CMP_STRIDE + (CMP_BLOCK - 1) <= pos1)
    s_c = lax.dot_general(q, kc_ref[0, 0], nt, preferred_element_type=F32)
    e_c, l_c = _softmax_rows(s_c, tile4(m_c))
    p_c = e_c * _safe_inv(l_c)
    o_c = jnp.dot(p_c.astype(BF16), vc_ref[0, 0], preferred_element_type=F32)

    psum = p_c[0:tq]
    for h in range(1, hpg):
        psum = psum + p_c[h * tq:(h + 1) * tq]
    imp = pltpu.roll(psum, 1, axis=1) + psum
    for o in range(1, SLC_BLOCK // CMP_STRIDE):
        imp = imp + pltpu.roll(psum, LANES - o, axis=1)
    r = SLC_BLOCK // CMP_STRIDE
    blk = lane // r
    is_blk = (lane % r == 0) & (blk < n_slc)
    valid = blk * SLC_BLOCK <= pos1
    forced = (blk == 0) | (blk == pos1 // SLC_BLOCK)
    score = jnp.where(is_blk & valid, imp + jnp.where(forced, FORCE, 0.0), NEG)
    rk_ref[...] = score.T
    n_row = LANES // r
    sc = rk_ref[pl.ds(0, n_row, stride=r), :]
    j_io = lax.broadcasted_iota(jnp.int32, (n_row, tq), 0)
    rank = jnp.zeros((n_row, tq), F32)
    for k in range(n_slc):
        ck = sc[k:k + 1, :]
        beats = (ck > sc) | ((ck == sc) & (j_io > k))
        rank = rank + jnp.where(beats, 1.0, 0.0)
    sel_t = jnp.where(rank < n_sel, 1.0, 0.0).astype(BF16)
    tn = (((0,), (0,)), ((), ()))
    picked = lax.dot_general(sel_t, exp_ref[...], tn, preferred_element_type=F32)
    bias_ref[...] = (picked - 1.0) * (-NEG)
    hi = (q0 + tq - 1) // tk + 1
    k_last = pl.multiple_of((hi - 1) * tk, tk)
    dlt = lax.broadcasted_iota(jnp.int32, (tq, tk), 1) - lax.broadcasted_iota(jnp.int32, (tq, tk), 0)
    bias_ref[:, pl.ds(k_last, tk)] = jnp.where(dlt + (k_last - q0) <= 0, bias_ref[:, pl.ds(k_last, tk)], NEG)

    def flash(k_ref, v_ref, lo, bias_fn):
        m_ref[...] = jnp.full((rows, LANES), M_INIT, F32)
        acc_ref[...] = jnp.zeros((rows, LANES), F32)

        def body(kb, carry):
            k0 = pl.multiple_of(kb * tk, tk)
            kk = k_ref[0, 0, pl.ds(k0, tk), :]
            vv = v_ref[0, 0, pl.ds(k0, tk), :]
            bias = bias_fn(k0)
            for h in range(hpg):
                rs = pl.ds(h * tq, tq)
                s = lax.dot_general(q_ref[0, h], kk, nt, preferred_element_type=F32) + bias
                m_prev = m_ref[rs, :]
                m_new = jnp.maximum(m_prev, jnp.max(s, axis=-1, keepdims=True))
                p = jnp.exp(s - jnp.concatenate([m_new] * (tk // LANES), axis=1))
                acc_ref[rs, :] = (jnp.exp(m_prev - m_new) * acc_ref[rs, :]
                                  + jnp.dot(p.astype(BF16), vv, preferred_element_type=F32))
                m_ref[rs, :] = m_new
            return carry

        lax.fori_loop(lo, hi, body, 0)
        acc = acc_ref[...]
        return acc[:, 0:dh] * _safe_inv(acc[:, dh:dh + 1])

    def slc_bias(k0):
        return bias_ref[:, pl.ds(k0, tk)]

    def win_bias(k0):
        d = dlt + (k0 - q0)
        return jnp.where((d <= 0) & (d >= -WINDOW), 0.0, NEG)

    o_s = flash(ks_ref, vs_ref, 0, slc_bias)
    o_w = flash(kw_ref, vw_ref, jnp.maximum(q0 - WINDOW, 0) // tk, win_bias)

    gates = gate_ref[0]
    outs = []
    for h in range(hpg):
        sl = slice(h * tq, (h + 1) * tq)
        outs.append(gates[:, h:h + 1] * o_c[sl] + gates[:, hpg + h:hpg + h + 1] * o_s[sl]
                    + gates[:, 2 * hpg + h:2 * hpg + h + 1] * o_w[sl])
    o_ref[...] = jnp.concatenate(outs, axis=-1).astype(BF16)


def _nsa_prompt(q_hm, ks_hm, vs_hm, kw_hm, vw_hm, kcmp, vcmp, gates_hm, bsz, t):
    g, hpg, dh, tq = NSA_KV_HEADS, NSA_HPG, NSA_HEAD_DIM, NSA_TQ
    assert t % NSA_TK == 0 and t // SLC_BLOCK * (SLC_BLOCK // CMP_STRIDE) <= LANES
    nq = t // tq
    n_cmp = kcmp.shape[2] - 1
    if kcmp.shape[2] < LANES:
        padw = ((0, 0), (0, 0), (0, LANES - kcmp.shape[2]), (0, 0))
        kcmp, vcmp = jnp.pad(kcmp, padw), jnp.pad(vcmp, padw)
    n_chunk = kcmp.shape[2]
    assert n_chunk == LANES
    n_slc = t // SLC_BLOCK
    q4 = q_hm.reshape(g, hpg, bsz * t, dh)
    r = SLC_BLOCK // CMP_STRIDE
    n_row = LANES // r
    key_blk = jnp.arange(t, dtype=jnp.int32)[None, :] // SLC_BLOCK
    expand = (jnp.arange(n_row, dtype=jnp.int32)[:, None] == key_blk).astype(BF16)

    def kv_spec(width):
        return pl.BlockSpec((1, 1, t, width), lambda b, gg, qi: (gg, b, 0, 0))

    def per_bt(a):
        return a.reshape(g, bsz, t, a.shape[-1])

    cmp_spec = pl.BlockSpec((1, 1, n_chunk, dh), lambda b, gg, qi: (b, gg, 0, 0))
    rows = hpg * tq
    return pl.pallas_call(
        functools.partial(_nsa_prompt_kernel, n_cmp=n_cmp, n_slc=n_slc, n_sel=min(N_SELECT, n_slc)),
        grid=(bsz, g, nq),
        in_specs=[pl.BlockSpec((1, hpg, tq, dh), lambda b, gg, qi: (gg, 0, b * nq + qi, 0)),
                  cmp_spec, cmp_spec, kv_spec(dh), kv_spec(LANES), kv_spec(dh), kv_spec(LANES),
                  pl.BlockSpec((1, tq, LANES), lambda b, gg, qi: (gg, b * nq + qi, 0)),
                  _resident((n_row, t))],
        out_specs=pl.BlockSpec((tq, hpg * dh), lambda b, gg, qi: (b * nq + qi, gg)),
        out_shape=SDS((bsz * t, NSA_Q), BF16),
        scratch_shapes=[pltpu.VMEM((tq, t), F32), pltpu.VMEM((LANES, tq), F32),
                        pltpu.VMEM((rows, LANES), F32), pltpu.VMEM((rows, LANES), F32)],
        compiler_params=_params("parallel", "parallel", "arbitrary"),
    )(q4, kcmp, vcmp, per_bt(ks_hm), per_bt(vs_hm), per_bt(kw_hm), per_bt(vw_hm), gates_hm, expand)


def _gla_prompt_kernel(q_ref, k_ref, v_ref, r_ref, misc_ref, wa_ref, ba_ref, ng_ref,
                       o_ref, s_ref, qe_ref, qt_ref, kt_ref, kh_ref, d_ref, u_ref, st_ref):
    t = q_ref.shape[0]
    c = GLA_CHUNK
    n = t // c
    dk, dv = GLA_DK, GLA_DV
    lr = misc_ref[:, N_GATE:N_GATE + GLA_RANK].astype(BF16)
    x = jnp.dot(lr, wa_ref[...], preferred_element_type=F32) + ba_ref[...]
    g = jax.nn.log_sigmoid(x) / GLA_TAU
    row = lax.broadcasted_iota(jnp.int32, (t, dk), 0) % c
    b = g
    sh = 1
    while sh < c:
        b = b + jnp.where(row >= sh, pltpu.roll(b, sh, axis=0), 0.0)
        sh *= 2
    b3 = b.reshape(n, c, dk)
    b_last = jnp.broadcast_to(b3[:, c - 1:c, :], (n, c, dk)).reshape(t, dk)
    b_mid = jnp.broadcast_to(b3[:, c // 2 - 1:c // 2, :], (n, c, dk)).reshape(t, dk)
    q = q_ref[...] * (GLA_DK ** -0.5)
    k = k_ref[...]
    qe_ref[...] = (q * jnp.exp(b)).astype(BF16)
    qt_ref[...] = (q * jnp.exp(b - b_mid)).astype(BF16)
    kt_ref[...] = (k * jnp.exp(b_mid - b)).astype(BF16)
    kh_ref[...] = (k * jnp.exp(b_last - b)).astype(BF16)
    d_ref[...] = jnp.exp(b_last)
    tn = (((0,), (0,)), ((), ()))
    nt = (((1,), (1,)), ((), ()))

    def chunk_update(i, carry):
        r0 = pl.multiple_of(i * c, c)
        u_ref[i] = lax.dot_general(v_ref[pl.ds(r0, c), :], kh_ref[pl.ds(r0, c), :], tn,
                                   preferred_element_type=F32)
        return carry

    lax.fori_loop(0, n, chunk_update, 0)

    st_ref[...] = jnp.zeros((dv, dk), F32)
    tril = lax.broadcasted_iota(jnp.int32, (c, c), 0) >= lax.broadcasted_iota(jnp.int32, (c, c), 1)
    ng = ng_ref[...]

    def chunk_out(i, carry):
        r0 = pl.multiple_of(i * c, c)
        st = st_ref[...]
        vv = v_ref[pl.ds(r0, c), :]
        o = lax.dot_general(qe_ref[pl.ds(r0, c), :], st.astype(BF16), nt, preferred_element_type=F32)
        a = lax.dot_general(qt_ref[pl.ds(r0, c), :], kt_ref[pl.ds(r0, c), :], nt, preferred_element_type=F32)
        a = jnp.where(tril, a, 0.0).astype(BF16)
        o = o + jnp.dot(a, vv, preferred_element_type=F32)
        o = _rms(o, ng) * jax.nn.silu(r_ref[pl.ds(r0, c), :])
        o_ref[pl.ds(r0, c), :] = o.astype(BF16)
        st_ref[...] = st * d_ref[pl.ds(r0, 1), :] + u_ref[i]
        return carry

    lax.fori_loop(0, n, chunk_out, 0)
    s_ref[0, 0] = st_ref[...].T


def _gla_prompt(q_l, k_l, v_l, r_l, misc, wa, ba, ng, bsz, t):
    h, dk, dv = GLA_HEADS, GLA_DK, GLA_DV
    assert t % GLA_CHUNK == 0
    n = t // GLA_CHUNK
    return pl.pallas_call(
        _gla_prompt_kernel,
        grid=(bsz, h),
        in_specs=[pl.BlockSpec((t, dk), lambda b, hh: (b, hh)), pl.BlockSpec((t, dk), lambda b, hh: (b, hh)),
                  pl.BlockSpec((t, dv), lambda b, hh: (b, hh)), pl.BlockSpec((t, dv), lambda b, hh: (b, hh)),
                  pl.BlockSpec((t, LANES), lambda b, hh: (b, 0)),
                  pl.BlockSpec((GLA_RANK, dk), lambda b, hh: (0, hh)), pl.BlockSpec((1, dk), lambda b, hh: (0, hh)),
                  _resident((1, dv))],
        out_specs=[pl.BlockSpec((t, dv), lambda b, hh: (b, hh)),
                   pl.BlockSpec((1, 1, dk, dv), lambda b, hh: (b, hh, 0, 0))],
        out_shape=[SDS((bsz * t, h * dv), BF16), SDS((bsz, h, dk, dv), F32)],
        scratch_shapes=[pltpu.VMEM((t, dk), BF16), pltpu.VMEM((t, dk), BF16), pltpu.VMEM((t, dk), BF16),
                        pltpu.VMEM((t, dk), BF16), pltpu.VMEM((t, dk), F32),
                        pltpu.VMEM((n, dv, dk), F32), pltpu.VMEM((dv, dk), F32)],
        compiler_params=_params("parallel", "parallel"),
    )(q_l, k_l, v_l, r_l, misc, wa, ba, ng)


def _merge_kernel(oa_ref, ob_ref, x_ref, g_ref, wg_ref, wa_ref, wb_ref, u_ref):
    h = _rms(x_ref[...], g_ref[...]).astype(BF16)
    d = D_MODEL
    ga = jax.nn.sigmoid(jnp.dot(h, wg_ref[:, 0:d], preferred_element_type=F32))
    u = ga * jnp.dot(oa_ref[...], wa_ref[...], preferred_element_type=F32)
    gb = jax.nn.sigmoid(jnp.dot(h, wg_ref[:, d:2 * d], preferred_element_type=F32))
    u = u + gb * jnp.dot(ob_ref[...], wb_ref[...], preferred_element_type=F32)
    u_ref[...] = u.astype(BF16)


def _merge(o_a, o_b, x, norm_g, w_gate, w_a, w_b, tm):
    m, d = x.shape
    assert m % tm == 0
    return pl.pallas_call(
        _merge_kernel, grid=(m // tm,),
        in_specs=[pl.BlockSpec((tm, NSA_Q), lambda i: (i, 0)), pl.BlockSpec((tm, GLA_HEADS * GLA_DV), lambda i: (i, 0)),
                  pl.BlockSpec((tm, d), lambda i: (i, 0)), _resident((1, d)), _resident((d, 2 * d)),
                  _resident(w_a.shape), _resident(w_b.shape)],
        out_specs=pl.BlockSpec((tm, d), lambda i: (i, 0)),
        out_shape=SDS((m, d), BF16),
        compiler_params=_params("parallel"),
    )(o_a, o_b, x, norm_g, w_gate, w_a, w_b)


def _mlp_kernel(x_ref, u_ref, wo_ref, g2_ref, wu_ref, wd_ref, gf_ref, y_ref, x1_ref, h_ref, acc_ref):
    j = pl.program_id(1)

    @pl.when(j == 0)
    def _():
        x1 = x_ref[...] + jnp.dot(u_ref[...], wo_ref[...], preferred_element_type=F32)
        x1_ref[...] = x1
        h_ref[...] = _rms(x1, g2_ref[...]).astype(BF16)
        acc_ref[...] = jnp.zeros_like(acc_ref)

    up = jnp.maximum(jnp.dot(h_ref[...], wu_ref[...], preferred_element_type=F32), 0.0)
    acc_ref[...] += jnp.dot((up * up).astype(BF16), wd_ref[...], preferred_element_type=F32)

    @pl.when(j == pl.num_programs(1) - 1)
    def _():
        y_ref[...] = _rms(x1_ref[...] + acc_ref[...], gf_ref[...])


def _mlp(x, u, w_o, g2, w_up, w_down, gf, tm, tf):
    m, d = x.shape
    ff = w_up.shape[1]
    assert m % tm == 0 and ff % tf == 0
    return pl.pallas_call(
        _mlp_kernel, grid=(m // tm, ff // tf),
        in_specs=[pl.BlockSpec((tm, d), lambda i, j: (i, 0)), pl.BlockSpec((tm, d), lambda i, j: (i, 0)),
                  _resident((d, d)), _resident((1, d)),
                  pl.BlockSpec((d, tf), lambda i, j: (0, j)), pl.BlockSpec((tf, d), lambda i, j: (j, 0)),
                  _resident((1, d))],
        out_specs=pl.BlockSpec((tm, d), lambda i, j: (i, 0)),
        out_shape=SDS((m, d), F32),
        scratch_shapes=[pltpu.VMEM((tm, d), F32), pltpu.VMEM((tm, d), BF16), pltpu.VMEM((tm, d), F32)],
        compiler_params=_params("parallel", "arbitrary"),
    )(x, u, w_o, g2, w_up, w_down, gf)


def _prep_weights(norm1_g, w_in, b_nsa_gate, cmp_pe_k, cmp_pe_v, cmp_k_w1, cmp_k_w2, cmp_v_w1, cmp_v_w2,
                  gla_w_a2, gla_b_a, gla_norm_g, w_br_a, w_br_b, w_o, norm2_g, w_up, w_down, norm_f):
    pts = [0]
    for s in SPLITS:
        pts.append(pts[-1] + s)
    c_q, c_kv, c_g, c_ql, c_kl, c_vl, c_rl, c_lr, c_br, c_end = pts
    gcols = jnp.asarray([c_g + (g * NSA_HPG + h) * 3 + c for g in range(NSA_KV_HEADS)
                         for c in range(3) for h in range(NSA_HPG)], jnp.int32)
    pad = jnp.zeros((D_MODEL, LANES - N_GATE - GLA_RANK), F32)
    w_misc = jnp.concatenate([w_in[:, gcols], w_in[:, c_lr:c_br], pad], axis=1)
    b_misc = jnp.concatenate([b_nsa_gate[gcols - c_g], jnp.zeros((LANES - N_GATE,), F32)])[None, :]
    w = dict(
        norm1=norm1_g[None, :], norm2=norm2_g[None, :], norm_f=norm_f[None, :],
        w_a=w_in[:, c_q:c_g].astype(BF16),
        w_b=jnp.concatenate([w_in[:, c_ql:c_lr], w_misc], axis=1).astype(BF16),
        w_gate=w_in[:, c_br:c_end].astype(BF16),
        b_misc=b_misc,
        gla_wa=gla_w_a2.astype(BF16), gla_ba=gla_b_a[None, :], gla_ng=gla_norm_g[None, :],
        w_br_a=w_br_a.astype(BF16), w_br_b=w_br_b.astype(BF16), w_o=w_o.astype(BF16),
        w_up=w_up.astype(BF16), w_down=w_down.astype(BF16),
    )
    cw = {}
    cw['pe_k'], cw['w1_k'], cw['w2_k'] = _cmp_weights(cmp_pe_k, cmp_k_w1, cmp_k_w2)
    cw['pe_v'], cw['w1_v'], cw['w2_v'] = _cmp_weights(cmp_pe_v, cmp_v_w1, cmp_v_w2)
    w['cmp'] = cw
    return w


_KV6 = 6 * NSA_KV
_GQ = GLA_HEADS * GLA_DK
_GV = GLA_HEADS * GLA_DV


def _layer_prompt(x, w):
    bsz, t, d = x.shape
    m = bsz * t
    x2 = x.reshape(m, d)
    zb = jnp.zeros((1, LANES), F32)
    spec_a = [('hm', 0, NSA_Q, NSA_HEAD_DIM ** -0.5)]
    spec_a += [('f32', NSA_Q + j * NSA_KV, NSA_Q + (j + 1) * NSA_KV, 1.0) for j in range(6)]
    spec_a += [('hm' if j % 2 == 0 else 'hmv', NSA_Q + j * NSA_KV, NSA_Q + (j + 1) * NSA_KV, 1.0)
               for j in range(2, 6)]
    q_hm, r0, r1, r2, r3, r4, r5, ks_hm, vs_hm, kw_hm, vw_hm = _proj(x2, w['norm1'], w['w_a'], zb, spec_a, 512)
    spec_b = [('f32', 0, _GQ, 1.0), ('f32', _GQ, 2 * _GQ, 1.0), ('bf16', 2 * _GQ, 2 * _GQ + _GV, 1.0),
              ('f32', 2 * _GQ + _GV, 2 * _GQ + 2 * _GV, 1.0),
              ('misc', 2 * _GQ + 2 * _GV, 2 * _GQ + 2 * _GV + LANES, 1.0), ('gates', 0, 0, 1.0)]
    q_l, k_l, v_l, r_l, misc, gates_hm = _proj(x2, w['norm1'], w['w_b'], w['b_misc'], spec_b, 512)

    n_pages = t // PAGE_SIZE
    ident = jnp.arange(bsz * n_pages, dtype=jnp.int32).reshape(bsz, n_pages)
    kd = CMP_STRIDE * NSA_KV
    kcmp, vcmp = _compress(ident, r0.reshape(bsz * n_pages, 8, kd), r1.reshape(bsz * n_pages, 8, kd), w['cmp'])
    o_a = _nsa_prompt(q_hm, ks_hm, vs_hm, kw_hm, vw_hm, kcmp, vcmp, gates_hm, bsz, t)
    o_b, s_new = _gla_prompt(q_l, k_l, v_l, r_l, misc, w['gla_wa'], w['gla_ba'], w['gla_ng'], bsz, t)
    u = _merge(o_a, o_b, x2, w['norm1'], w['w_gate'], w['w_br_a'], w['w_br_b'], 512)
    y = _mlp(x2, u, w['w_o'], w['norm2'], w['w_up'], w['w_down'], w['norm_f'], 512, 512)
    kvh = (bsz, t, NSA_KV_HEADS, NSA_HEAD_DIM)
    rows = [a.reshape(kvh) for a in (r0, r1, r2, r3)]
    n_keep = min(WINDOW, t)
    wins = [a.reshape(kvh)[:, t - n_keep:] for a in (r4, r5)]
    return y.reshape(bsz, t, d), rows, wins, s_new


def _group_rows(parts):
    rowg = lax.broadcasted_iota(jnp.int32, parts[0].shape, 0) // NSA_HPG
    out = parts[0]
    for g in range(1, NSA_KV_HEADS):
        out = jnp.where(rowg == g, parts[g], out)
    return out


def _sample_select_kernel(q_ref, kc_ref, vc_ref, oc_ref, idx_ref, *, n_cmp, pos, n_pick):
    q = q_ref[0]
    nt = (((1,), (1,)), ((), ()))
    n_chunk = kc_ref.shape[2]
    s = _group_rows([lax.dot_general(q, kc_ref[0, g], nt, preferred_element_type=F32)
                     for g in range(NSA_KV_HEADS)])
    lane = lax.broadcasted_iota(jnp.int32, s.shape, 1)
    mask = (lane < n_cmp) & (lane * CMP_STRIDE + (CMP_BLOCK - 1) <= pos)
    e, l = _softmax_rows(s, mask)
    p = e * _safe_inv(l)
    pb = p.astype(BF16)
    oc_ref[0] = _group_rows([jnp.dot(pb, vc_ref[0, g], preferred_element_type=F32)
                             for g in range(NSA_KV_HEADS)])
    nr = p.shape[0]
    y = p + pltpu.roll(p, nr - 1, axis=0)
    psum = y + pltpu.roll(y, nr - 2, axis=0)
    imp = pltpu.roll(psum, 1, axis=1) + psum
    r = SLC_BLOCK // CMP_STRIDE
    for o in range(1, r):
        imp = imp + pltpu.roll(psum, n_chunk - o, axis=1)
    blk = lane // r
    is_blk = lane % r == 0
    valid = blk * SLC_BLOCK <= pos
    forced = (blk == 0) | (blk == pos // SLC_BLOCK)
    score = jnp.where(is_blk & valid, imp + jnp.where(forced, FORCE, 0.0), NEG)
    lane_f = lane.astype(F32)
    out_lane = lax.broadcasted_iota(jnp.int32, (nr, LANES), 1)
    picked = jnp.zeros((nr, LANES), F32)
    for k in range(n_pick):
        mx = jnp.max(score, axis=-1, keepdims=True)
        ix = jnp.min(jnp.where(score == mx, lane_f, float(n_chunk)), axis=-1, keepdims=True)
        picked = jnp.where(out_lane == k, ix, picked)
        score = jnp.where(lane_f == ix, 2.0 * NEG, score)
    idx_ref[0] = picked.astype(jnp.int32) // r


def _sample_select(q_s, kcmp, vcmp, pos, n_pick):
    bsz, g, n_chunk, dh = kcmp.shape
    nh = NSA_HEADS
    cmp_spec = pl.BlockSpec((1, g, n_chunk, dh), lambda b: (b, 0, 0, 0))
    return pl.pallas_call(
        functools.partial(_sample_select_kernel, n_cmp=n_chunk - 1, pos=pos, n_pick=n_pick),
        grid=(bsz,),
        in_specs=[pl.BlockSpec((1, nh, dh), lambda b: (b, 0, 0)), cmp_spec, cmp_spec],
        out_specs=[pl.BlockSpec((1, nh, dh), lambda b: (b, 0, 0)), pl.BlockSpec((1, nh, LANES), lambda b: (b, 0, 0))],
        out_shape=[SDS((bsz, nh, dh), F32), SDS((bsz, nh, LANES), jnp.int32)],
        compiler_params=_params("parallel"),
    )(q_s, kcmp, vcmp)


def _sample_attend_kernel(idx_ref, pt_ref, q_ref, oc_ref, ks_hbm, vs_hbm, kw_ref, vw_ref,
                          nks_ref, nvs_ref, nkw_ref, nvw_ref, gate_ref, o_ref, ksel, vsel, sem, *, n_pick):
    b = pl.program_id(0)
    g_n, dh = NSA_KV_HEADS, NSA_HEAD_DIM
    half = PAGE_SIZE // SLC_BLOCK

    def copies():
        out = []
        for g in range(g_n):
            for r in range(n_pick):
                page = pt_ref[b, idx_ref[b * g_n + g, r] // half]
                out.append(pltpu.make_async_copy(ks_hbm.at[page, g], ksel.at[g, r], sem.at[0]))
                out.append(pltpu.make_async_copy(vs_hbm.at[page, g], vsel.at[g, r], sem.at[1]))
        return out

    for c in copies():
        c.start()

    q = q_ref[0]
    qf = q.astype(F32)
    nt = (((1,), (1,)), ((), ()))

    def attend(keys_t, vals_t, bias, k_new, v_new):
        s = _group_rows([jnp.dot(q, keys_t[g].astype(BF16), preferred_element_type=F32)
                         + (0.0 if bias is None else bias[g]) for g in range(g_n)])
        s_new = _group_rows([jnp.sum(qf * k_new[:, g * dh:(g + 1) * dh], axis=-1, keepdims=True)
                             for g in range(g_n)])
        m = jnp.maximum(jnp.max(s, axis=-1, keepdims=True), s_new)
        e = jnp.exp(s - m)
        e_new = jnp.exp(s_new - m)
        l = jnp.sum(e, axis=-1, keepdims=True) + e_new
        eb = e.astype(BF16)
        acc = _group_rows([lax.dot_general(eb, vals_t[g].astype(BF16), nt, preferred_element_type=F32)
                           + e_new * v_new[:, g * dh:(g + 1) * dh] for g in range(g_n)])
        return acc / l

    o_w = attend([kw_ref[0, g] for g in range(g_n)], [vw_ref[0, g] for g in range(g_n)], None,
                 nkw_ref[0], nvw_ref[0])
    for c in copies():
        c.wait()
    lin = lax.broadcasted_iota(jnp.int32, (1, PAGE_SIZE), 1)
    bias = []
    for g in range(g_n):
        parts = []
        for r in range(n_pick):
            off = (idx_ref[b * g_n + g, r] % half) * SLC_BLOCK
            parts.append(jnp.where((lin >= off) & (lin < off + SLC_BLOCK), 0.0, NEG))
        bias.append(jnp.concatenate(parts, axis=1))

    def tiles(buf, g):
        return jnp.concatenate([buf[g, r] for r in range(n_pick)], axis=1)

    o_s = attend([tiles(ksel, g) for g in range(g_n)], [tiles(vsel, g) for g in range(g_n)], bias,
                 nks_ref[0], nvs_ref[0])

    gt = jnp.broadcast_to(gate_ref[0], (LANES, LANES)).T
    nh = NSA_HEADS
    o_ref[0] = (gt[0:nh, 0:dh] * oc_ref[0] + gt[nh:2 * nh, 0:dh] * o_s + gt[2 * nh:3 * nh, 0:dh] * o_w)


def _sample_attend(idx, page_table, q_s, o_c, slc_k, slc_v, win_k, win_v, new_rows, gates, n_pick):
    bsz = page_table.shape[0]
    nh, dh, g = NSA_HEADS, NSA_HEAD_DIM, NSA_KV_HEADS
    wlen = win_k.shape[-1]
    row_spec = pl.BlockSpec((1, 1, NSA_KV), lambda b, *_: (b, 0, 0))
    win_spec = pl.BlockSpec((1, g, dh, wlen), lambda b, *_: (b, 0, 0, 0))
    head_spec = pl.BlockSpec((1, nh, dh), lambda b, *_: (b, 0, 0))
    any_spec = pl.BlockSpec(memory_space=pl.ANY)
    grid_spec = pltpu.PrefetchScalarGridSpec(
        num_scalar_prefetch=2, grid=(bsz,),
        in_specs=[head_spec, head_spec, any_spec, any_spec, win_spec, win_spec,
                  row_spec, row_spec, row_spec, row_spec, pl.BlockSpec((1, 1, LANES), lambda b, *_: (b, 0, 0))],
        out_specs=head_spec,
        scratch_shapes=[pltpu.VMEM((g, n_pick, dh, PAGE_SIZE), F32), pltpu.VMEM((g, n_pick, dh, PAGE_SIZE), F32),
                        pltpu.SemaphoreType.DMA((2,))])
    return pl.pallas_call(
        functools.partial(_sample_attend_kernel, n_pick=n_pick),
        grid_spec=grid_spec, out_shape=SDS((bsz, nh, dh), F32),
        compiler_params=_params("arbitrary"),
    )(idx, page_table, q_s, o_c, slc_k, slc_v, win_k, win_v, *new_rows, gates)


def _gla_sample_kernel(q_ref, k_ref, v_ref, r_ref, misc_ref, wa_ref, ba_ref, ng_ref, s0_ref, o_ref, s_ref):
    dk, dv = GLA_DK, GLA_DV
    lr = jnp.broadcast_to(misc_ref[0][:, N_GATE:N_GATE + GLA_RANK], (16, GLA_RANK)).astype(BF16)
    x = jnp.dot(lr, wa_ref[...], preferred_element_type=F32)[0:1] + ba_ref[...]
    g_all = jax.nn.log_sigmoid(x) / GLA_TAU

    def col(v):
        t = jnp.broadcast_to(v, (dk, dk)).T
        return jnp.concatenate([t] * (dv // dk), axis=1)

    outs = []
    for h in range(GLA_HEADS):
        g = g_all[:, h * dk:(h + 1) * dk]
        q = q_ref[0][:, h * dk:(h + 1) * dk] * (GLA_DK ** -0.5)
        k = k_ref[0][:, h * dk:(h + 1) * dk]
        v = v_ref[0][:, h * dv:(h + 1) * dv]
        s0 = s0_ref[0, h]
        q_t = q * jnp.exp(g)
        k_t = k * jnp.exp(-g)
        a = jnp.sum(q_t * k_t, axis=-1, keepdims=True)
        o = jnp.sum(col(q_t) * s0, axis=0, keepdims=True) + a * v
        s_ref[0, h] = col(jnp.exp(g)) * s0 + col(k) * v
        outs.append(_rms(o, ng_ref[...]) * jax.nn.silu(r_ref[0][:, h * dv:(h + 1) * dv]))
    o_ref[0] = jnp.concatenate(outs, axis=-1)


def _gla_sample(q_l, k_l, v_l, r_l, misc, wa, ba, ng, s0):
    bsz, h, dk, dv = s0.shape

    def row(n):
        return pl.BlockSpec((1, 1, n), lambda b: (b, 0, 0))

    st_spec = pl.BlockSpec((1, h, dk, dv), lambda b: (b, 0, 0, 0))
    return pl.pallas_call(
        _gla_sample_kernel, grid=(bsz,),
        in_specs=[row(h * dk), row(h * dk), row(h * dv), row(h * dv), row(LANES),
                  _resident(wa.shape), _resident(ba.shape), _resident(ng.shape), st_spec],
        out_specs=[row(h * dv), st_spec],
        out_shape=[SDS((bsz, 1, h * dv), F32), SDS((bsz, h, dk, dv), F32)],
        compiler_params=_params("parallel"),
    )(q_l, k_l, v_l, r_l, misc, wa, ba, ng, s0)


SAMPLE_ROWS = 16


def _layer_sample(x, caches, wins, s0, page_table, w):
    bsz, t, d = x.shape
    n_pages = page_table.shape[1]
    pos = n_pages * PAGE_SIZE
    assert t == 1 and bsz <= SAMPLE_ROWS and pos % SLC_BLOCK == 0 and wins[0].shape[1] == WINDOW
    assert pos // SLC_BLOCK >= N_SELECT
    mp = SAMPLE_ROWS
    x2 = jnp.pad(x.reshape(bsz, d), ((0, mp - bsz), (0, 0)))
    zb = jnp.zeros((1, LANES), F32)
    spec_a = [('bf16', 0, NSA_Q, NSA_HEAD_DIM ** -0.5)]
    spec_a += [('f32', NSA_Q + j * NSA_KV, NSA_Q + (j + 1) * NSA_KV, 1.0) for j in range(6)]
    q_s, r0, r1, r2, r3, r4, r5 = _proj(x2, w['norm1'], w['w_a'], zb, spec_a, mp)
    spec_b = [('f32', 0, _GQ, 1.0), ('f32', _GQ, 2 * _GQ, 1.0), ('f32', 2 * _GQ, 2 * _GQ + _GV, 1.0),
              ('f32', 2 * _GQ + _GV, 2 * _GQ + 2 * _GV, 1.0),
              ('misc', 2 * _GQ + 2 * _GV, 2 * _GQ + 2 * _GV + LANES, 1.0)]
    q_l, k_l, v_l, r_l, misc = _proj(x2, w['norm1'], w['w_b'], w['b_misc'], spec_b, mp)

    n_pool = caches[0].shape[0]
    kd = CMP_STRIDE * NSA_KV
    kcmp, vcmp = _compress(page_table, caches[0].reshape(n_pool, 8, kd), caches[1].reshape(n_pool, 8, kd), w['cmp'])
    q_h = q_s.reshape(mp, NSA_HEADS, NSA_HEAD_DIM)
    n_pick = N_SELECT - 1
    o_c, idx = _sample_select(q_h, kcmp, vcmp, pos, n_pick)
    idx = idx[:, ::NSA_HPG, :n_pick].reshape(bsz * NSA_KV_HEADS, n_pick)
    gates = misc[:, :N_GATE].reshape(mp, NSA_KV_HEADS, 3, NSA_HPG).transpose(0, 2, 1, 3).reshape(mp, 1, N_GATE)
    gates = jnp.pad(gates, ((0, 0), (0, 0), (0, LANES - N_GATE)))
    new_rows = [a.reshape(mp, 1, NSA_KV) for a in (r2, r3, r4, r5)]
    slc_t = [a.transpose(0, 2, 3, 1) for a in caches[2:4]]
    win_t = [a.transpose(0, 2, 3, 1) for a in wins]
    o_a = _sample_attend(idx, page_table, q_h, o_c, slc_t[0], slc_t[1], win_t[0], win_t[1], new_rows, gates, n_pick)
    o_a = jnp.pad(o_a.reshape(bsz, NSA_Q), ((0, mp - bsz), (0, 0))).astype(BF16)

    def r3d(a):
        return a.reshape(mp, 1, a.shape[-1])

    o_b, s_new = _gla_sample(r3d(q_l), r3d(k_l), r3d(v_l), r3d(r_l), r3d(misc), w['gla_wa'], w['gla_ba'],
                             w['gla_ng'], s0)
    o_b = jnp.pad(o_b.reshape(bsz, _GV), ((0, mp - bsz), (0, 0))).astype(BF16)
    u = _merge(o_a, o_b, x2, w['norm1'], w['w_gate'], w['w_br_a'], w['w_br_b'], mp)
    y = _mlp(x2, u, w['w_o'], w['norm2'], w['w_up'], w['w_down'], w['norm_f'], mp, 512)
    kvh = (bsz, 1, NSA_KV_HEADS, NSA_HEAD_DIM)
    rows = [a[:bsz].reshape(kvh) for a in (r0, r1, r2, r3)]
    new_wins = [jnp.concatenate([c[:, 1:], a[:bsz].reshape(kvh)], axis=1) for c, a in zip(wins, (r4, r5))]
    return y[:bsz].reshape(bsz, 1, d), rows, new_wins, s_new


def kernel(x_prompt, x_sample, cache_cmp_k, cache_cmp_v, cache_slc_k, cache_slc_v, cache_win_k, cache_win_v, state_gla, page_table, norm1_g, w_in, b_nsa_gate, cmp_pe_k, cmp_pe_v, cmp_k_w1, cmp_k_w2, cmp_v_w1, cmp_v_w2, gla_w_a2, gla_b_a, gla_norm_g, w_br_a, w_br_b, w_o, norm2_g, w_up, w_down, norm_f):
    assert DEPTH == 1 and norm1_g.shape[0] == 1
    w = _prep_weights(norm1_g[0], w_in[0], b_nsa_gate[0], cmp_pe_k[0], cmp_pe_v[0], cmp_k_w1[0], cmp_k_w2[0],
                      cmp_v_w1[0], cmp_v_w2[0], gla_w_a2[0], gla_b_a[0], gla_norm_g[0], w_br_a[0], w_br_b[0],
                      w_o[0], norm2_g[0], w_up[0], w_down[0], norm_f)
    y_p, rows_p, wins_p, s_p = _layer_prompt(x_prompt, w)
    caches = [c[0] for c in (cache_cmp_k, cache_cmp_v, cache_slc_k, cache_slc_v)]
    y_s, rows_s, wins_s, s_s = _layer_sample(x_sample, caches, [cache_win_k[0], cache_win_v[0]], state_gla[0],
                                             page_table, w)
    outs_p = [a[None] for a in rows_p + wins_p + [s_p]]
    outs_s = [a[None] for a in rows_s + wins_s + [s_s]]
    return (y_p, y_s, *outs_p, *outs_s)
```

```python
import functools

import jax
import jax.numpy as jnp
from jax import lax
from jax.experimental import pallas as pl
from jax.experimental.pallas import tpu as pltpu

D_MODEL = 2048
DEPTH = 1
PAGE_SIZE = 128
NSA_HEADS = 16
NSA_KV_HEADS = 4
NSA_HPG = NSA_HEADS // NSA_KV_HEADS
NSA_HEAD_DIM = 64
NSA_Q = NSA_HEADS * NSA_HEAD_DIM
NSA_KV = NSA_KV_HEADS * NSA_HEAD_DIM
CMP_STRIDE = 16
CMP_BLOCK = 32
CMP_HIDDEN = 128
SLC_BLOCK = 64
N_SELECT = 16
WINDOW = 512
GLA_HEADS = 4
GLA_DK = (D_MODEL // 4) // GLA_HEADS
GLA_DV = (D_MODEL // 2) // GLA_HEADS
GLA_RANK = 16
GLA_TAU = 16.0
D_FF = 4 * D_MODEL
EPS = 1e-6
NEG = -1e30
FORCE = 1e4
SPLITS = (NSA_Q, 6 * NSA_KV, 3 * NSA_HEADS,
          GLA_HEADS * GLA_DK, GLA_HEADS * GLA_DK, GLA_HEADS * GLA_DV, GLA_HEADS * GLA_DV,
          GLA_RANK, 2 * D_MODEL)

F32 = jnp.float32
BF16 = jnp.bfloat16
LANES = 128
VMEM_LIMIT_BYTES = 56 * 1024 * 1024
N_GATE = 3 * NSA_HEADS
GLA_CHUNK = 32
NSA_TQ = 256
NSA_TK = 256
VT_ROWS = NSA_HEAD_DIM + 16
CMP_PAGES = 16
SDS = jax.ShapeDtypeStruct


def _params(*sem):
    return pltpu.CompilerParams(dimension_semantics=sem, vmem_limit_bytes=VMEM_LIMIT_BYTES)


def _resident(shape):
    nd = len(shape)
    return pl.BlockSpec(shape, lambda *_: (0,) * nd, pipeline_mode=pl.Buffered(1))


def _rms(x, g):
    return x * lax.rsqrt(jnp.mean(x * x, axis=-1, keepdims=True) + EPS) * g


def _proj_kernel(x_ref, g_ref, w_ref, b_ref, *out_refs, spec):
    h = _rms(x_ref[...], g_ref[...]).astype(BF16)
    misc = None
    for o_ref, (kind, c0, c1, scale) in zip(out_refs, spec):
        if kind == 'gates':
            for g in range(NSA_KV_HEADS):
                o_ref[g] = misc if g == 0 else pltpu.roll(misc, LANES - g * 3 * NSA_HPG, axis=1)
            continue
        r = jnp.dot(h, w_ref[:, c0:c1], preferred_element_type=F32)
        if scale != 1.0:
            r = r * scale
        if kind == 'f32':
            o_ref[...] = r
        elif kind == 'bf16':
            o_ref[...] = r.astype(BF16)
        elif kind == 'sigmoid':
            o_ref[...] = jax.nn.sigmoid(r)
        elif kind == 'hm':
            for i in range((c1 - c0) // NSA_HEAD_DIM):
                o_ref[i] = r[:, i * NSA_HEAD_DIM:(i + 1) * NSA_HEAD_DIM].astype(BF16)
        elif kind == 'hmt':
            rt = r.T
            for i in range((c1 - c0) // NSA_HEAD_DIM):
                o_ref[i] = rt[i * NSA_HEAD_DIM:(i + 1) * NSA_HEAD_DIM].astype(BF16)
        elif kind == 'hmvt':
            rt = r.T
            sub = lax.broadcasted_iota(jnp.int32, (VT_ROWS - NSA_HEAD_DIM, r.shape[0]), 0)
            ones = jnp.where(sub == 0, 1.0, 0.0)
            for i in range((c1 - c0) // NSA_HEAD_DIM):
                piece = rt[i * NSA_HEAD_DIM:(i + 1) * NSA_HEAD_DIM]
                o_ref[i] = jnp.concatenate([piece, ones], axis=0).astype(BF16)
        elif kind == 'misc':
            lane = lax.broadcasted_iota(jnp.int32, r.shape, 1)
            misc = jnp.where(lane < N_GATE, jax.nn.sigmoid(r + b_ref[...]), r)
            o_ref[...] = misc


def _proj(x, norm_g, w, bias, spec, tm):
    m, d = x.shape
    n = w.shape[1]
    assert m % tm == 0
    out_shape, out_specs = [], []
    for kind, c0, c1, _ in spec:
        if kind == 'hm':
            nh = (c1 - c0) // NSA_HEAD_DIM
            out_shape.append(SDS((nh, m, NSA_HEAD_DIM), BF16))
            out_specs.append(pl.BlockSpec((nh, tm, NSA_HEAD_DIM), lambda i: (0, i, 0)))
        elif kind in ('hmt', 'hmvt'):
            nh = (c1 - c0) // NSA_HEAD_DIM
            nrow = NSA_HEAD_DIM if kind == 'hmt' else VT_ROWS
            out_shape.append(SDS((nh, nrow, m), BF16))
            out_specs.append(pl.BlockSpec((nh, nrow, tm), lambda i: (0, 0, i)))
        elif kind == 'gates':
            out_shape.append(SDS((NSA_KV_HEADS, m, LANES), F32))
            out_specs.append(pl.BlockSpec((NSA_KV_HEADS, tm, LANES), lambda i: (0, i, 0)))
        else:
            out_shape.append(SDS((m, c1 - c0), BF16 if kind == 'bf16' else F32))
            out_specs.append(pl.BlockSpec((tm, c1 - c0), lambda i: (i, 0)))
    return pl.pallas_call(
        functools.partial(_proj_kernel, spec=tuple(spec)),
        grid=(m // tm,),
        in_specs=[pl.BlockSpec((tm, d), lambda i: (i, 0)), _resident((1, d)), _resident((d, n)),
                  _resident((1, LANES))],
        out_specs=out_specs, out_shape=out_shape,
        compiler_params=_params("parallel"),
    )(x, norm_g, w, bias)


def _cmp_kernel(pt_ref, k_hbm, v_hbm, pek_ref, pev_ref, w1k_ref, w1v_ref, w2k_ref, w2v_ref,
                ok_ref, ov_ref, kbuf, vbuf, hk, hv, sem, *, n_pages_step):
    b = pl.program_id(0)
    s = pl.program_id(1)
    ns = pl.num_programs(1)
    t = b * ns + s
    total = pl.num_programs(0) * ns
    rows = n_pages_step * 8

    def copies(tt, slot):
        bb = tt // ns
        ss = tt % ns
        out = []
        for p in range(n_pages_step):
            page = pt_ref[bb, ss * n_pages_step + p]
            out.append(pltpu.make_async_copy(k_hbm.at[page], kbuf.at[slot, pl.ds(p * 8, 8)], sem.at[slot, 0]))
            out.append(pltpu.make_async_copy(v_hbm.at[page], vbuf.at[slot, pl.ds(p * 8, 8)], sem.at[slot, 1]))
        return out

    slot = t % 2

    @pl.when(t == 0)
    def _():
        for c in copies(t, slot):
            c.start()

    @pl.when(t + 1 < total)
    def _():
        for c in copies(t + 1, 1 - slot):
            c.start()

    for c in copies(t, slot):
        c.wait()

    r0 = pl.multiple_of(s * rows, rows)
    for buf, pe_ref, w1_ref, h_ref in ((kbuf, pek_ref, w1k_ref, hk), (vbuf, pev_ref, w1v_ref, hv)):
        x = buf[slot]
        for role in range(2):
            xr = (x + pe_ref[role]).astype(BF16)
            h_ref[role, pl.ds(r0, rows), :] = jnp.dot(xr, w1_ref[role], preferred_element_type=F32)

    @pl.when(s == ns - 1)
    def _():
        for h_ref, w2_ref, o_ref in ((hk, w2k_ref, ok_ref), (hv, w2v_ref, ov_ref)):
            n_chunk = h_ref.shape[1]
            hid = h_ref[0] + pltpu.roll(h_ref[1], n_chunk - 1, axis=0)
            res = jnp.dot(jax.nn.gelu(hid).astype(BF16), w2_ref[...], preferred_element_type=F32)
            if o_ref is ok_ref:
                for g in range(NSA_KV_HEADS):
                    o_ref[0, g] = res[:, g * NSA_HEAD_DIM:(g + 1) * NSA_HEAD_DIM].astype(BF16)
            else:
                res_t = res.T
                for g in range(NSA_KV_HEADS):
                    o_ref[0, g] = res_t[g * NSA_HEAD_DIM:(g + 1) * NSA_HEAD_DIM].astype(BF16)


def _cmp_weights(pe, w1, w2):
    g = NSA_KV_HEADS
    eye = jnp.eye(g, dtype=F32)
    pe_t = jnp.broadcast_to(pe.reshape(2, CMP_STRIDE, 1, NSA_HEAD_DIM), (2, CMP_STRIDE, g, NSA_HEAD_DIM))
    pe_t = pe_t.reshape(2, 1, CMP_STRIDE * g * NSA_HEAD_DIM)
    w1r = w1.reshape(2, CMP_STRIDE, NSA_HEAD_DIM, CMP_HIDDEN)
    w1b = jnp.einsum('rldj,gh->rlgdhj', w1r, eye).reshape(2, CMP_STRIDE * g * NSA_HEAD_DIM, g * CMP_HIDDEN)
    w2b = jnp.einsum('jd,gh->gjhd', w2, eye).reshape(g * CMP_HIDDEN, g * NSA_HEAD_DIM)
    return pe_t, w1b.astype(BF16), w2b.astype(BF16)


def _compress(page_table, k_pages, v_pages, cw):
    bsz, n_pages = page_table.shape
    p_step = min(CMP_PAGES, n_pages)
    ns = n_pages // p_step
    n_chunk = n_pages * 8
    kd = k_pages.shape[-1]
    gh = NSA_KV_HEADS * CMP_HIDDEN
    k_sds = SDS((bsz, NSA_KV_HEADS, n_chunk, NSA_HEAD_DIM), BF16)
    v_sds = SDS((bsz, NSA_KV_HEADS, NSA_HEAD_DIM, n_chunk), BF16)
    k_spec = pl.BlockSpec((1, NSA_KV_HEADS, n_chunk, NSA_HEAD_DIM), lambda b, s, pt: (b, 0, 0, 0))
    v_spec = pl.BlockSpec((1, NSA_KV_HEADS, NSA_HEAD_DIM, n_chunk), lambda b, s, pt: (b, 0, 0, 0))
    grid_spec = pltpu.PrefetchScalarGridSpec(
        num_scalar_prefetch=1, grid=(bsz, ns),
        in_specs=[pl.BlockSpec(memory_space=pl.ANY), pl.BlockSpec(memory_space=pl.ANY),
                  _resident((2, 1, kd)), _resident((2, 1, kd)),
                  _resident((2, kd, gh)), _resident((2, kd, gh)),
                  _resident((gh, NSA_KV)), _resident((gh, NSA_KV))],
        out_specs=[k_spec, v_spec],
        scratch_shapes=[pltpu.VMEM((2, p_step * 8, kd), F32), pltpu.VMEM((2, p_step * 8, kd), F32),
                        pltpu.VMEM((2, n_chunk, gh), F32), pltpu.VMEM((2, n_chunk, gh), F32),
                        pltpu.SemaphoreType.DMA((2, 2))])
    return pl.pallas_call(
        functools.partial(_cmp_kernel, n_pages_step=p_step),
        grid_spec=grid_spec, out_shape=[k_sds, v_sds],
        compiler_params=_params("arbitrary", "arbitrary"),
    )(page_table, k_pages, v_pages, cw['pe_k'], cw['pe_v'], cw['w1_k'], cw['w1_v'], cw['w2_k'], cw['w2_v'])


def _softmax_rows(s, mask):
    sm = jnp.where(mask, s, NEG)
    m = jnp.max(sm, axis=-1, keepdims=True)
    e = jnp.where(mask, jnp.exp(sm - m), 0.0)
    return e, jnp.sum(e, axis=-1, keepdims=True)


def _safe_inv(l):
    return jnp.where(l > 0.0, 1.0 / jnp.where(l > 0.0, l, 1.0), 0.0)


M_INIT = -1e20


def _nsa_prompt_kernel(qt_ref, kc_ref, vct_ref, ks_ref, vst_ref, kw_ref, vwt_ref, gate_ref,
                       o_ref, *scratch, n_cmp, n_slc, n_sel):
    tq, tk, hpg, dh = NSA_TQ, NSA_TK, NSA_HPG, NSA_HEAD_DIM
    rk_refs = scratch[:tq // LANES]
    selb_ref, m_ref, acc_ref = scratch[tq // LANES:tq // LANES + 3]
    s_refs, p_refs, a_refs = (scratch[tq // LANES + 3 + 2 * i:tq // LANES + 5 + 2 * i] for i in range(3))
    qi = pl.program_id(2)
    q0 = qi * tq
    n_chunk = kc_ref.shape[2]
    r = SLC_BLOCK // CMP_STRIDE
    n_row = n_chunk // r

    ci = lax.broadcasted_iota(jnp.int32, (n_chunk, tq), 0)
    pos_c = q0 + lax.broadcasted_iota(jnp.int32, (n_chunk, tq), 1)
    m_c = (ci < n_cmp) & (ci * CMP_STRIDE + (CMP_BLOCK - 1) <= pos_c)
    kc = kc_ref[0, 0]
    vct = vct_ref[0, 0]
    o_c = []
    psum = None
    for h in range(hpg):
        s = jnp.where(m_c, jnp.dot(kc, qt_ref[0, h], preferred_element_type=F32), NEG)
        e = jnp.where(m_c, jnp.exp(s - jnp.max(s, axis=0, keepdims=True)), 0.0)
        p = e * _safe_inv(jnp.sum(e, axis=0, keepdims=True))
        o_c.append(jnp.dot(vct, p.astype(BF16), preferred_element_type=F32))
        psum = p if psum is None else psum + p

    imp = pltpu.roll(psum, 1, axis=0) + psum
    for o in range(1, r):
        imp = imp + pltpu.roll(psum, n_chunk - o, axis=0)
    parts = []
    for i, rk_ref in enumerate(rk_refs):
        rk_ref[...] = imp[:, i * LANES:(i + 1) * LANES]
        parts.append(rk_ref[pl.ds(0, n_row, stride=r), :])
    imp_b = jnp.concatenate(parts, axis=1)
    j_io = lax.broadcasted_iota(jnp.int32, (n_row, tq), 0)
    pos_b = q0 + lax.broadcasted_iota(jnp.int32, (n_row, tq), 1)
    valid = (j_io < n_slc) & (j_io * SLC_BLOCK <= pos_b)
    forced = (j_io == 0) | (j_io == pos_b // SLC_BLOCK)
    sc = jnp.where(valid, imp_b + jnp.where(forced, FORCE, 0.0), NEG)
    rank = jnp.zeros((n_row, tq), F32)
    for k in range(n_slc):
        ck = sc[k:k + 1, :]
        beats = (ck > sc) | ((ck == sc) & (j_io > k))
        rank = rank + jnp.where(beats, 1.0, 0.0)
    selb_ref[...] = jnp.where(rank < n_sel, 0.0, NEG)

    dlt = lax.broadcasted_iota(jnp.int32, (tk, tq), 0) - lax.broadcasted_iota(jnp.int32, (tk, tq), 1)
    blocks_per_tile = tk // SLC_BLOCK

    def flash(k_ref, vt_ref, lo, hi, bias_fn):
        n_tiles = k_ref.shape[2] // tk
        m_ref[...] = jnp.full(m_ref.shape, M_INIT, F32)
        acc_ref[...] = jnp.zeros(acc_ref.shape, F32)
        p_refs[1][...] = jnp.zeros(p_refs[1].shape, BF16)
        a_refs[1][...] = jnp.ones(a_refs[1].shape, F32)

        def tile_start(kb):
            return pl.multiple_of(jnp.clip(kb, 0, n_tiles - 1) * tk, tk)

        def scores(kb, s_ref):
            kk = k_ref[0, 0, pl.ds(tile_start(kb), tk), :]
            bias = bias_fn(kb)
            for h in range(hpg):
                s_ref[h] = jnp.dot(kk, qt_ref[0, h], preferred_element_type=F32) + bias

        def softmax(s_ref, p_ref, a_ref):
            for h in range(hpg):
                s = s_ref[h]
                m_prev = m_ref[h]
                m_new = jnp.maximum(m_prev, jnp.max(s, axis=0, keepdims=True))
                p_ref[h] = jnp.exp(s - m_new).astype(BF16)
                a_ref[h] = jnp.exp(m_prev - m_new)
                m_ref[h] = m_new

        def weighted_values(kb, p_ref, a_ref):
            vt = vt_ref[0, :, pl.ds(tile_start(kb), tk)]
            for h in range(hpg):
                acc_ref[h] = a_ref[h] * acc_ref[h] + jnp.dot(vt, p_ref[h], preferred_element_type=F32)

        scores(lo, s_refs[0])

        def body(j, carry):
            i = lo + 2 * j
            scores(i + 1, s_refs[1])
            softmax(s_refs[0], p_refs[0], a_refs[0])
            weighted_values(i - 1, p_refs[1], a_refs[1])
            scores(i + 2, s_refs[0])
            softmax(s_refs[1], p_refs[1], a_refs[1])
            weighted_values(i, p_refs[0], a_refs[0])
            return carry

        n_pairs = (hi - lo + 1) // 2
        lax.fori_loop(0, n_pairs, body, 0)
        weighted_values(lo + 2 * n_pairs - 1, p_refs[1], a_refs[1])
        return [acc_ref[h, 0:dh, :] * _safe_inv(acc_ref[h, dh:dh + 1, :]) for h in range(hpg)]

    def slc_bias(kb):
        first = jnp.minimum(kb, n_slc // blocks_per_tile - 1) * blocks_per_tile
        rows = [jnp.broadcast_to(selb_ref[pl.ds(first + i, 1), :], (SLC_BLOCK, tq))
                for i in range(blocks_per_tile)]
        return jnp.where(dlt + (kb * tk - q0) <= 0, jnp.concatenate(rows, axis=0), NEG)

    def win_bias(kb):
        d = dlt + (kb * tk - q0)
        return jnp.where((d <= 0) & (d >= -WINDOW), 0.0, NEG)

    hi = (q0 + tq - 1) // tk + 1
    o_s = flash(ks_ref, vst_ref, 0, hi, slc_bias)
    o_w = flash(kw_ref, vwt_ref, jnp.maximum(q0 - WINDOW, 0) // tk, hi, win_bias)

    gt = gate_ref[0].T
    outs = [gt[h:h + 1] * o_c[h] + gt[hpg + h:hpg + h + 1] * o_s[h] + gt[2 * hpg + h:2 * hpg + h + 1] * o_w[h]
            for h in range(hpg)]
    o_ref[...] = jnp.concatenate(outs, axis=0).T.astype(BF16)


def _nsa_prompt(q_t, ks_hm, vs_t, kw_hm, vw_t, kcmp, vcmp_t, gates_hm, bsz, t):
    g, hpg, dh, tq = NSA_KV_HEADS, NSA_HPG, NSA_HEAD_DIM, NSA_TQ
    r = SLC_BLOCK // CMP_STRIDE
    n_slc = t // SLC_BLOCK
    assert t % NSA_TK == 0 and t % tq == 0 and n_slc * r <= LANES and NSA_TK % SLC_BLOCK == 0
    nq = t // tq
    n_cmp = kcmp.shape[2] - 1
    if kcmp.shape[2] < LANES:
        fill = LANES - kcmp.shape[2]
        kcmp = jnp.pad(kcmp, ((0, 0), (0, 0), (0, fill), (0, 0)))
        vcmp_t = jnp.pad(vcmp_t, ((0, 0), (0, 0), (0, 0), (0, fill)))
    n_chunk = kcmp.shape[2]
    assert n_chunk == LANES
    q4 = q_t.reshape(g, hpg, dh, bsz * t)

    def k_spec():
        return pl.BlockSpec((1, 1, t, dh), lambda b, gg, qi: (gg, b, 0, 0))

    def vt_spec():
        return pl.BlockSpec((1, VT_ROWS, t), lambda b, gg, qi: (gg, 0, b))

    def per_bt(a):
        return a.reshape(g, bsz, t, a.shape[-1])

    return pl.pallas_call(
        functools.partial(_nsa_prompt_kernel, n_cmp=n_cmp, n_slc=n_slc, n_sel=min(N_SELECT, n_slc)),
        grid=(bsz, g, nq),
        in_specs=[pl.BlockSpec((1, hpg, dh, tq), lambda b, gg, qi: (gg, 0, 0, b * nq + qi)),
                  pl.BlockSpec((1, 1, n_chunk, dh), lambda b, gg, qi: (b, gg, 0, 0)),
                  pl.BlockSpec((1, 1, dh, n_chunk), lambda b, gg, qi: (b, gg, 0, 0)),
                  k_spec(), vt_spec(), k_spec(), vt_spec(),
                  pl.BlockSpec((1, tq, LANES), lambda b, gg, qi: (gg, b * nq + qi, 0))],
        out_specs=pl.BlockSpec((tq, hpg * dh), lambda b, gg, qi: (b * nq + qi, gg)),
        out_shape=SDS((bsz * t, NSA_Q), BF16),
        scratch_shapes=[pltpu.VMEM((n_chunk, LANES), F32)] * (tq // LANES) + [
                        pltpu.VMEM((n_chunk // r, tq), F32),
                        pltpu.VMEM((hpg, 1, tq), F32), pltpu.VMEM((hpg, VT_ROWS, tq), F32),
                        pltpu.VMEM((hpg, NSA_TK, tq), F32), pltpu.VMEM((hpg, NSA_TK, tq), F32),
                        pltpu.VMEM((hpg, NSA_TK, tq), BF16), pltpu.VMEM((hpg, NSA_TK, tq), BF16),
                        pltpu.VMEM((hpg, 1, tq), F32), pltpu.VMEM((hpg, 1, tq), F32)],
        compiler_params=_params("parallel", "parallel", "arbitrary"),
    )(q4, kcmp, vcmp_t, per_bt(ks_hm), vs_t, per_bt(kw_hm), vw_t, gates_hm)


def _gla_prompt_kernel(q_ref, k_ref, v_ref, r_ref, misc_ref, wa_ref, ba_ref, ng_ref,
                       o_ref, s_ref, qe_ref, qt_ref, kt_ref, kh_ref, d_ref, u_ref, st_ref):
    t = q_ref.shape[0]
    c = GLA_CHUNK
    n = t // c
    dk, dv = GLA_DK, GLA_DV
    lr = misc_ref[:, N_GATE:N_GATE + GLA_RANK].astype(BF16)
    x = jnp.dot(lr, wa_ref[...], preferred_element_type=F32) + ba_ref[...]
    g = jax.nn.log_sigmoid(x) / GLA_TAU
    row = lax.broadcasted_iota(jnp.int32, (t, dk), 0) % c
    b = g
    sh = 1
    while sh < c:
        b = b + jnp.where(row >= sh, pltpu.roll(b, sh, axis=0), 0.0)
        sh *= 2
    b3 = b.reshape(n, c, dk)
    b_last = jnp.broadcast_to(b3[:, c - 1:c, :], (n, c, dk)).reshape(t, dk)
    b_mid = jnp.broadcast_to(b3[:, c // 2 - 1:c // 2, :], (n, c, dk)).reshape(t, dk)
    q = q_ref[...] * (GLA_DK ** -0.5)
    k = k_ref[...]
    qe_ref[...] = (q * jnp.exp(b)).astype(BF16)
    qt_ref[...] = (q * jnp.exp(b - b_mid)).astype(BF16)
    kt_ref[...] = (k * jnp.exp(b_mid - b)).astype(BF16)
    kh_ref[...] = (k * jnp.exp(b_last - b)).astype(BF16)
    d_ref[...] = jnp.exp(b_last)
    tn = (((0,), (0,)), ((), ()))
    nt = (((1,), (1,)), ((), ()))

    def chunk_update(i, carry):
        r0 = pl.multiple_of(i * c, c)
        u_ref[i] = lax.dot_general(v_ref[pl.ds(r0, c), :], kh_ref[pl.ds(r0, c), :], tn,
                                   preferred_element_type=F32)
        return carry

    lax.fori_loop(0, n, chunk_update, 0)

    st_ref[...] = jnp.zeros((dv, dk), F32)
    tril = lax.broadcasted_iota(jnp.int32, (c, c), 0) >= lax.broadcasted_iota(jnp.int32, (c, c), 1)
    ng = ng_ref[...]

    def chunk_out(i, carry):
        r0 = pl.multiple_of(i * c, c)
        st = st_ref[...]
        vv = v_ref[pl.ds(r0, c), :]
        o = lax.dot_general(qe_ref[pl.ds(r0, c), :], st.astype(BF16), nt, preferred_element_type=F32)
        a = lax.dot_general(qt_ref[pl.ds(r0, c), :], kt_ref[pl.ds(r0, c), :], nt, preferred_element_type=F32)
        a = jnp.where(tril, a, 0.0).astype(BF16)
        o = o + jnp.dot(a, vv, preferred_element_type=F32)
        o = _rms(o, ng) * jax.nn.silu(r_ref[pl.ds(r0, c), :])
        o_ref[pl.ds(r0, c), :] = o.astype(BF16)
        st_ref[...] = st * d_ref[pl.ds(r0, 1), :] + u_ref[i]
        return carry

    lax.fori_loop(0, n, chunk_out, 0)
    s_ref[0, 0] = st_ref[...].T


def _gla_prompt(q_l, k_l, v_l, r_l, misc, wa, ba, ng, bsz, t):
    h, dk, dv = GLA_HEADS, GLA_DK, GLA_DV
    assert t % GLA_CHUNK == 0
    n = t // GLA_CHUNK
    return pl.pallas_call(
        _gla_prompt_kernel,
        grid=(bsz, h),
        in_specs=[pl.BlockSpec((t, dk), lambda b, hh: (b, hh)), pl.BlockSpec((t, dk), lambda b, hh: (b, hh)),
                  pl.BlockSpec((t, dv), lambda b, hh: (b, hh)), pl.BlockSpec((t, dv), lambda b, hh: (b, hh)),
                  pl.BlockSpec((t, LANES), lambda b, hh: (b, 0)),
                  pl.BlockSpec((GLA_RANK, dk), lambda b, hh: (0, hh)), pl.BlockSpec((1, dk), lambda b, hh: (0, hh)),
                  _resident((1, dv))],
        out_specs=[pl.BlockSpec((t, dv), lambda b, hh: (b, hh)),
                   pl.BlockSpec((1, 1, dk, dv), lambda b, hh: (b, hh, 0, 0))],
        out_shape=[SDS((bsz * t, h * dv), BF16), SDS((bsz, h, dk, dv), F32)],
        scratch_shapes=[pltpu.VMEM((t, dk), BF16), pltpu.VMEM((t, dk), BF16), pltpu.VMEM((t, dk), BF16),
                        pltpu.VMEM((t, dk), BF16), pltpu.VMEM((t, dk), F32),
                        pltpu.VMEM((n, dv, dk), F32), pltpu.VMEM((dv, dk), F32)],
        compiler_params=_params("parallel", "parallel"),
    )(q_l, k_l, v_l, r_l, misc, wa, ba, ng)


def _merge_kernel(oa_ref, ob_ref, x_ref, g_ref, wg_ref, wa_ref, wb_ref, u_ref):
    h = _rms(x_ref[...], g_ref[...]).astype(BF16)
    d = D_MODEL
    ga = jax.nn.sigmoid(jnp.dot(h, wg_ref[:, 0:d], preferred_element_type=F32))
    u = ga * jnp.dot(oa_ref[...], wa_ref[...], preferred_element_type=F32)
    gb = jax.nn.sigmoid(jnp.dot(h, wg_ref[:, d:2 * d], preferred_element_type=F32))
    u = u + gb * jnp.dot(ob_ref[...], wb_ref[...], preferred_element_type=F32)
    u_ref[...] = u.astype(BF16)


def _merge(o_a, o_b, x, norm_g, w_gate, w_a, w_b, tm):
    m, d = x.shape
    assert m % tm == 0
    return pl.pallas_call(
        _merge_kernel, grid=(m // tm,),
        in_specs=[pl.BlockSpec((tm, NSA_Q), lambda i: (i, 0)), pl.BlockSpec((tm, GLA_HEADS * GLA_DV), lambda i: (i, 0)),
                  pl.BlockSpec((tm, d), lambda i: (i, 0)), _resident((1, d)), _resident((d, 2 * d)),
                  _resident(w_a.shape), _resident(w_b.shape)],
        out_specs=pl.BlockSpec((tm, d), lambda i: (i, 0)),
        out_shape=SDS((m, d), BF16),
        compiler_params=_params("parallel"),
    )(o_a, o_b, x, norm_g, w_gate, w_a, w_b)


def _mlp_kernel(x_ref, u_ref, wo_ref, g2_ref, wu_ref, wd_ref, gf_ref, y_ref, x1_ref, h_ref, acc_ref):
    j = pl.program_id(1)

    @pl.when(j == 0)
    def _():
        x1 = x_ref[...] + jnp.dot(u_ref[...], wo_ref[...], preferred_element_type=F32)
        x1_ref[...] = x1
        h_ref[...] = _rms(x1, g2_ref[...]).astype(BF16)
        acc_ref[...] = jnp.zeros_like(acc_ref)

    up = jnp.maximum(jnp.dot(h_ref[...], wu_ref[...], preferred_element_type=F32), 0.0)
    acc_ref[...] += jnp.dot((up * up).astype(BF16), wd_ref[...], preferred_element_type=F32)

    @pl.when(j == pl.num_programs(1) - 1)
    def _():
        y_ref[...] = _rms(x1_ref[...] + acc_ref[...], gf_ref[...])


def _mlp(x, u, w_o, g2, w_up, w_down, gf, tm, tf):
    m, d = x.shape
    ff = w_up.shape[1]
    assert m % tm == 0 and ff % tf == 0
    return pl.pallas_call(
        _mlp_kernel, grid=(m // tm, ff // tf),
        in_specs=[pl.BlockSpec((tm, d), lambda i, j: (i, 0)), pl.BlockSpec((tm, d), lambda i, j: (i, 0)),
                  _resident((d, d)), _resident((1, d)),
                  pl.BlockSpec((d, tf), lambda i, j: (0, j)), pl.BlockSpec((tf, d), lambda i, j: (j, 0)),
                  _resident((1, d))],
        out_specs=pl.BlockSpec((tm, d), lambda i, j: (i, 0)),
        out_shape=SDS((m, d), F32),
        scratch_shapes=[pltpu.VMEM((tm, d), F32), pltpu.VMEM((tm, d), BF16), pltpu.VMEM((tm, d), F32)],
        compiler_params=_params("parallel", "arbitrary"),
    )(x, u, w_o, g2, w_up, w_down, gf)


def _prep_weights(norm1_g, w_in, b_nsa_gate, cmp_pe_k, cmp_pe_v, cmp_k_w1, cmp_k_w2, cmp_v_w1, cmp_v_w2,
                  gla_w_a2, gla_b_a, gla_norm_g, w_br_a, w_br_b, w_o, norm2_g, w_up, w_down, norm_f):
    pts = [0]
    for s in SPLITS:
        pts.append(pts[-1] + s)
    c_q, c_kv, c_g, c_ql, c_kl, c_vl, c_rl, c_lr, c_br, c_end = pts
    gcols = jnp.asarray([c_g + (g * NSA_HPG + h) * 3 + c for g in range(NSA_KV_HEADS)
                         for c in range(3) for h in range(NSA_HPG)], jnp.int32)
    pad = jnp.zeros((D_MODEL, LANES - N_GATE - GLA_RANK), F32)
    w_misc = jnp.concatenate([w_in[:, gcols], w_in[:, c_lr:c_br], pad], axis=1)
    b_misc = jnp.concatenate([b_nsa_gate[gcols - c_g], jnp.zeros((LANES - N_GATE,), F32)])[None, :]
    w = dict(
        norm1=norm1_g[None, :], norm2=norm2_g[None, :], norm_f=norm_f[None, :],
        w_a=w_in[:, c_q:c_g].astype(BF16),
        w_b=jnp.concatenate([w_in[:, c_ql:c_lr], w_misc], axis=1).astype(BF16),
        w_gate=w_in[:, c_br:c_end].astype(BF16),
        b_misc=b_misc,
        gla_wa=gla_w_a2.astype(BF16), gla_ba=gla_b_a[None, :], gla_ng=gla_norm_g[None, :],
        w_br_a=w_br_a.astype(BF16), w_br_b=w_br_b.astype(BF16), w_o=w_o.astype(BF16),
        w_up=w_up.astype(BF16), w_down=w_down.astype(BF16),
    )
    cw = {}
    cw['pe_k'], cw['w1_k'], cw['w2_k'] = _cmp_weights(cmp_pe_k, cmp_k_w1, cmp_k_w2)
    cw['pe_v'], cw['w1_v'], cw['w2_v'] = _cmp_weights(cmp_pe_v, cmp_v_w1, cmp_v_w2)
    w['cmp'] = cw
    return w


_KV6 = 6 * NSA_KV
_GQ = GLA_HEADS * GLA_DK
_GV = GLA_HEADS * GLA_DV


def _layer_prompt(x, w):
    bsz, t, d = x.shape
    m = bsz * t
    x2 = x.reshape(m, d)
    zb = jnp.zeros((1, LANES), F32)
    spec_a = [('hmt', 0, NSA_Q, NSA_HEAD_DIM ** -0.5)]
    spec_a += [('f32', NSA_Q + j * NSA_KV, NSA_Q + (j + 1) * NSA_KV, 1.0) for j in range(6)]
    spec_a += [('hm' if j % 2 == 0 else 'hmvt', NSA_Q + j * NSA_KV, NSA_Q + (j + 1) * NSA_KV, 1.0)
               for j in range(2, 6)]
    q_t, r0, r1, r2, r3, r4, r5, ks_hm, vs_t, kw_hm, vw_t = _proj(x2, w['norm1'], w['w_a'], zb, spec_a, 512)
    spec_b = [('f32', 0, _GQ, 1.0), ('f32', _GQ, 2 * _GQ, 1.0), ('bf16', 2 * _GQ, 2 * _GQ + _GV, 1.0),
              ('f32', 2 * _GQ + _GV, 2 * _GQ + 2 * _GV, 1.0),
              ('misc', 2 * _GQ + 2 * _GV, 2 * _GQ + 2 * _GV + LANES, 1.0), ('gates', 0, 0, 1.0)]
    q_l, k_l, v_l, r_l, misc, gates_hm = _proj(x2, w['norm1'], w['w_b'], w['b_misc'], spec_b, 512)

    n_pages = t // PAGE_SIZE
    ident = jnp.arange(bsz * n_pages, dtype=jnp.int32).reshape(bsz, n_pages)
    kd = CMP_STRIDE * NSA_KV
    kcmp, vcmp = _compress(ident, r0.reshape(bsz * n_pages, 8, kd), r1.reshape(bsz * n_pages, 8, kd), w['cmp'])
    o_a = _nsa_prompt(q_t, ks_hm, vs_t, kw_hm, vw_t, kcmp, vcmp, gates_hm, bsz, t)
    o_b, s_new = _gla_prompt(q_l, k_l, v_l, r_l, misc, w['gla_wa'], w['gla_ba'], w['gla_ng'], bsz, t)
    u = _merge(o_a, o_b, x2, w['norm1'], w['w_gate'], w['w_br_a'], w['w_br_b'], 512)
    y = _mlp(x2, u, w['w_o'], w['norm2'], w['w_up'], w['w_down'], w['norm_f'], 512, 512)
    kvh = (bsz, t, NSA_KV_HEADS, NSA_HEAD_DIM)
    rows = [a.reshape(kvh) for a in (r0, r1, r2, r3)]
    n_keep = min(WINDOW, t)
    wins = [a.reshape(kvh)[:, t - n_keep:] for a in (r4, r5)]
    return y.reshape(bsz, t, d), rows, wins, s_new


def _group_rows(parts):
    rowg = lax.broadcasted_iota(jnp.int32, parts[0].shape, 0) // NSA_HPG
    out = parts[0]
    for g in range(1, NSA_KV_HEADS):
        out = jnp.where(rowg == g, parts[g], out)
    return out


def _sample_select_kernel(q_ref, kc_ref, vc_ref, oc_ref, idx_ref, *, n_cmp, pos, n_pick):
    q = q_ref[0]
    nt = (((1,), (1,)), ((), ()))
    n_chunk = kc_ref.shape[2]
    s = _group_rows([lax.dot_general(q, kc_ref[0, g], nt, preferred_element_type=F32)
                     for g in range(NSA_KV_HEADS)])
    lane = lax.broadcasted_iota(jnp.int32, s.shape, 1)
    mask = (lane < n_cmp) & (lane * CMP_STRIDE + (CMP_BLOCK - 1) <= pos)
    e, l = _softmax_rows(s, mask)
    p = e * _safe_inv(l)
    pb = p.astype(BF16)
    oc_ref[0] = _group_rows([lax.dot_general(pb, vc_ref[0, g], nt, preferred_element_type=F32)
                             for g in range(NSA_KV_HEADS)])
    nr = p.shape[0]
    y = p + pltpu.roll(p, nr - 1, axis=0)
    psum = y + pltpu.roll(y, nr - 2, axis=0)
    imp = pltpu.roll(psum, 1, axis=1) + psum
    r = SLC_BLOCK // CMP_STRIDE
    for o in range(1, r):
        imp = imp + pltpu.roll(psum, n_chunk - o, axis=1)
    blk = lane // r
    is_blk = lane % r == 0
    valid = blk * SLC_BLOCK <= pos
    forced = (blk == 0) | (blk == pos // SLC_BLOCK)
    score = jnp.where(is_blk & valid, imp + jnp.where(forced, FORCE, 0.0), NEG)
    lane_f = lane.astype(F32)
    out_lane = lax.broadcasted_iota(jnp.int32, (nr, LANES), 1)
    picked = jnp.zeros((nr, LANES), F32)
    for k in range(n_pick):
        mx = jnp.max(score, axis=-1, keepdims=True)
        ix = jnp.min(jnp.where(score == mx, lane_f, float(n_chunk)), axis=-1, keepdims=True)
        picked = jnp.where(out_lane == k, ix, picked)
        score = jnp.where(lane_f == ix, 2.0 * NEG, score)
    idx_ref[0] = picked.astype(jnp.int32) // r


def _sample_select(q_s, kcmp, vcmp, pos, n_pick):
    bsz, g, n_chunk, dh = kcmp.shape
    nh = NSA_HEADS
    cmp_spec = pl.BlockSpec((1, g, n_chunk, dh), lambda b: (b, 0, 0, 0))
    cmpt_spec = pl.BlockSpec((1, g, dh, n_chunk), lambda b: (b, 0, 0, 0))
    return pl.pallas_call(
        functools.partial(_sample_select_kernel, n_cmp=n_chunk - 1, pos=pos, n_pick=n_pick),
        grid=(bsz,),
        in_specs=[pl.BlockSpec((1, nh, dh), lambda b: (b, 0, 0)), cmp_spec, cmpt_spec],
        out_specs=[pl.BlockSpec((1, nh, dh), lambda b: (b, 0, 0)), pl.BlockSpec((1, nh, LANES), lambda b: (b, 0, 0))],
        out_shape=[SDS((bsz, nh, dh), F32), SDS((bsz, nh, LANES), jnp.int32)],
        compiler_params=_params("parallel"),
    )(q_s, kcmp, vcmp)


def _sample_attend_kernel(idx_ref, pt_ref, q_ref, oc_ref, ks_hbm, vs_hbm, kw_ref, vw_ref,
                          nks_ref, nvs_ref, nkw_ref, nvw_ref, gate_ref, o_ref, ksel, vsel, sem, *, n_pick):
    b = pl.program_id(0)
    g_n, dh = NSA_KV_HEADS, NSA_HEAD_DIM
    half = PAGE_SIZE // SLC_BLOCK

    def copies():
        out = []
        for g in range(g_n):
            for r in range(n_pick):
                page = pt_ref[b, idx_ref[b * g_n + g, r] // half]
                out.append(pltpu.make_async_copy(ks_hbm.at[page, g], ksel.at[g, r], sem.at[0]))
                out.append(pltpu.make_async_copy(vs_hbm.at[page, g], vsel.at[g, r], sem.at[1]))
        return out

    for c in copies():
        c.start()

    q = q_ref[0]
    qf = q.astype(F32)
    nt = (((1,), (1,)), ((), ()))

    def attend(keys_t, vals_t, bias, k_new, v_new):
        s = _group_rows([jnp.dot(q, keys_t[g].astype(BF16), preferred_element_type=F32)
                         + (0.0 if bias is None else bias[g]) for g in range(g_n)])
        s_new = _group_rows([jnp.sum(qf * k_new[:, g * dh:(g + 1) * dh], axis=-1, keepdims=True)
                             for g in range(g_n)])
        m = jnp.maximum(jnp.max(s, axis=-1, keepdims=True), s_new)
        e = jnp.exp(s - m)
        e_new = jnp.exp(s_new - m)
        l = jnp.sum(e, axis=-1, keepdims=True) + e_new
        eb = e.astype(BF16)
        acc = _group_rows([lax.dot_general(eb, vals_t[g].astype(BF16), nt, preferred_element_type=F32)
                           + e_new * v_new[:, g * dh:(g + 1) * dh] for g in range(g_n)])
        return acc / l

    o_w = attend([kw_ref[0, g] for g in range(g_n)], [vw_ref[0, g] for g in range(g_n)], None,
                 nkw_ref[0], nvw_ref[0])
    for c in copies():
        c.wait()
    lin = lax.broadcasted_iota(jnp.int32, (1, PAGE_SIZE), 1)
    bias = []
    for g in range(g_n):
        parts = []
        for r in range(n_pick):
            off = (idx_ref[b * g_n + g, r] % half) * SLC_BLOCK
            parts.append(jnp.where((lin >= off) & (lin < off + SLC_BLOCK), 0.0, NEG))
        bias.append(jnp.concatenate(parts, axis=1))

    def tiles(buf, g):
        return jnp.concatenate([buf[g, r] for r in range(n_pick)], axis=1)

    o_s = attend([tiles(ksel, g) for g in range(g_n)], [tiles(vsel, g) for g in range(g_n)], bias,
                 nks_ref[0], nvs_ref[0])

    gt = jnp.broadcast_to(gate_ref[0], (LANES, LANES)).T
    nh = NSA_HEADS
    o_ref[0] = (gt[0:nh, 0:dh] * oc_ref[0] + gt[nh:2 * nh, 0:dh] * o_s + gt[2 * nh:3 * nh, 0:dh] * o_w)


def _sample_attend(idx, page_table, q_s, o_c, slc_k, slc_v, win_k, win_v, new_rows, gates, n_pick):
    bsz = page_table.shape[0]
    nh, dh, g = NSA_HEADS, NSA_HEAD_DIM, NSA_KV_HEADS
    wlen = win_k.shape[-1]
    row_spec = pl.BlockSpec((1, 1, NSA_KV), lambda b, *_: (b, 0, 0))
    win_spec = pl.BlockSpec((1, g, dh, wlen), lambda b, *_: (b, 0, 0, 0))
    head_spec = pl.BlockSpec((1, nh, dh), lambda b, *_: (b, 0, 0))
    any_spec = pl.BlockSpec(memory_space=pl.ANY)
    grid_spec = pltpu.PrefetchScalarGridSpec(
        num_scalar_prefetch=2, grid=(bsz,),
        in_specs=[head_spec, head_spec, any_spec, any_spec, win_spec, win_spec,
                  row_spec, row_spec, row_spec, row_spec, pl.BlockSpec((1, 1, LANES), lambda b, *_: (b, 0, 0))],
        out_specs=head_spec,
        scratch_shapes=[pltpu.VMEM((g, n_pick, dh, PAGE_SIZE), F32), pltpu.VMEM((g, n_pick, dh, PAGE_SIZE), F32),
                        pltpu.SemaphoreType.DMA((2,))])
    return pl.pallas_call(
        functools.partial(_sample_attend_kernel, n_pick=n_pick),
        grid_spec=grid_spec, out_shape=SDS((bsz, nh, dh), F32),
        compiler_params=_params("arbitrary"),
    )(idx, page_table, q_s, o_c, slc_k, slc_v, win_k, win_v, *new_rows, gates)


def _gla_sample_kernel(q_ref, k_ref, v_ref, r_ref, misc_ref, wa_ref, ba_ref, ng_ref, s0_ref, o_ref, s_ref):
    dk, dv = GLA_DK, GLA_DV
    lr = jnp.broadcast_to(misc_ref[0][:, N_GATE:N_GATE + GLA_RANK], (16, GLA_RANK)).astype(BF16)
    x = jnp.dot(lr, wa_ref[...], preferred_element_type=F32)[0:1] + ba_ref[...]
    g_all = jax.nn.log_sigmoid(x) / GLA_TAU

    def col(v):
        t = jnp.broadcast_to(v, (dk, dk)).T
        return jnp.concatenate([t] * (dv // dk), axis=1)

    outs = []
    for h in range(GLA_HEADS):
        g = g_all[:, h * dk:(h + 1) * dk]
        q = q_ref[0][:, h * dk:(h + 1) * dk] * (GLA_DK ** -0.5)
        k = k_ref[0][:, h * dk:(h + 1) * dk]
        v = v_ref[0][:, h * dv:(h + 1) * dv]
        s0 = s0_ref[0, h]
        q_t = q * jnp.exp(g)
        k_t = k * jnp.exp(-g)
        a = jnp.sum(q_t * k_t, axis=-1, keepdims=True)
        o = jnp.sum(col(q_t) * s0, axis=0, keepdims=True) + a * v
        s_ref[0, h] = col(jnp.exp(g)) * s0 + col(k) * v
        outs.append(_rms(o, ng_ref[...]) * jax.nn.silu(r_ref[0][:, h * dv:(h + 1) * dv]))
    o_ref[0] = jnp.concatenate(outs, axis=-1)


def _gla_sample(q_l, k_l, v_l, r_l, misc, wa, ba, ng, s0):
    bsz, h, dk, dv = s0.shape

    def row(n):
        return pl.BlockSpec((1, 1, n), lambda b: (b, 0, 0))

    st_spec = pl.BlockSpec((1, h, dk, dv), lambda b: (b, 0, 0, 0))
    return pl.pallas_call(
        _gla_sample_kernel, grid=(bsz,),
        in_specs=[row(h * dk), row(h * dk), row(h * dv), row(h * dv), row(LANES),
                  _resident(wa.shape), _resident(ba.shape), _resident(ng.shape), st_spec],
        out_specs=[row(h * dv), st_spec],
        out_shape=[SDS((bsz, 1, h * dv), F32), SDS((bsz, h, dk, dv), F32)],
        compiler_params=_params("parallel"),
    )(q_l, k_l, v_l, r_l, misc, wa, ba, ng, s0)


SAMPLE_ROWS = 16


def _layer_sample(x, caches, wins, s0, page_table, w):
    bsz, t, d = x.shape
    n_pages = page_table.shape[1]
    pos = n_pages * PAGE_SIZE
    assert t == 1 and bsz <= SAMPLE_ROWS and pos % SLC_BLOCK == 0 and wins[0].shape[1] == WINDOW
    assert pos // SLC_BLOCK >= N_SELECT
    mp = SAMPLE_ROWS
    x2 = jnp.pad(x.reshape(bsz, d), ((0, mp - bsz), (0, 0)))
    zb = jnp.zeros((1, LANES), F32)
    spec_a = [('bf16', 0, NSA_Q, NSA_HEAD_DIM ** -0.5)]
    spec_a += [('f32', NSA_Q + j * NSA_KV, NSA_Q + (j + 1) * NSA_KV, 1.0) for j in range(6)]
    q_s, r0, r1, r2, r3, r4, r5 = _proj(x2, w['norm1'], w['w_a'], zb, spec_a, mp)
    spec_b = [('f32', 0, _GQ, 1.0), ('f32', _GQ, 2 * _GQ, 1.0), ('f32', 2 * _GQ, 2 * _GQ + _GV, 1.0),
              ('f32', 2 * _GQ + _GV, 2 * _GQ + 2 * _GV, 1.0),
              ('misc', 2 * _GQ + 2 * _GV, 2 * _GQ + 2 * _GV + LANES, 1.0)]
    q_l, k_l, v_l, r_l, misc = _proj(x2, w['norm1'], w['w_b'], w['b_misc'], spec_b, mp)

    n_pool = caches[0].shape[0]
    kd = CMP_STRIDE * NSA_KV
    kcmp, vcmp = _compress(page_table, caches[0].reshape(n_pool, 8, kd), caches[1].reshape(n_pool, 8, kd), w['cmp'])
    q_h = q_s.reshape(mp, NSA_HEADS, NSA_HEAD_DIM)
    n_pick = N_SELECT - 1
    o_c, idx = _sample_select(q_h, kcmp, vcmp, pos, n_pick)
    idx = idx[:, ::NSA_HPG, :n_pick].reshape(bsz * NSA_KV_HEADS, n_pick)
    gates = misc[:, :N_GATE].reshape(mp, NSA_KV_HEADS, 3, NSA_HPG).transpose(0, 2, 1, 3).reshape(mp, 1, N_GATE)
    gates = jnp.pad(gates, ((0, 0), (0, 0), (0, LANES - N_GATE)))
    new_rows = [a.reshape(mp, 1, NSA_KV) for a in (r2, r3, r4, r5)]
    slc_t = [a.transpose(0, 2, 3, 1) for a in caches[2:4]]
    win_t = [a.transpose(0, 2, 3, 1) for a in wins]
    o_a = _sample_attend(idx, page_table, q_h, o_c, slc_t[0], slc_t[1], win_t[0], win_t[1], new_rows, gates, n_pick)
    o_a = jnp.pad(o_a.reshape(bsz, NSA_Q), ((0, mp - bsz), (0, 0))).astype(BF16)

    def r3d(a):
        return a.reshape(mp, 1, a.shape[-1])

    o_b, s_new = _gla_sample(r3d(q_l), r3d(k_l), r3d(v_l), r3d(r_l), r3d(misc), w['gla_wa'], w['gla_ba'],
                             w['gla_ng'], s0)
    o_b = jnp.pad(o_b.reshape(bsz, _GV), ((0, mp - bsz), (0, 0))).astype(BF16)
    u = _merge(o_a, o_b, x2, w['norm1'], w['w_gate'], w['w_br_a'], w['w_br_b'], mp)
    y = _mlp(x2, u, w['w_o'], w['norm2'], w['w_up'], w['w_down'], w['norm_f'], mp, 512)
    kvh = (bsz, 1, NSA_KV_HEADS, NSA_HEAD_DIM)
    rows = [a[:bsz].reshape(kvh) for a in (r0, r1, r2, r3)]
    new_wins = [jnp.concatenate([c[:, 1:], a[:bsz].reshape(kvh)], axis=1) for c, a in zip(wins, (r4, r5))]
    return y[:bsz].reshape(bsz, 1, d), rows, new_wins, s_new


def kernel(x_prompt, x_sample, cache_cmp_k, cache_cmp_v, cache_slc_k, cache_slc_v, cache_win_k, cache_win_v, state_gla, page_table, norm1_g, w_in, b_nsa_gate, cmp_pe_k, cmp_pe_v, cmp_k_w1, cmp_k_w2, cmp_v_w1, cmp_v_w2, gla_w_a2, gla_b_a, gla_norm_g, w_br_a, w_br_b, w_o, norm2_g, w_up, w_down, norm_f):
    assert DEPTH == 1 and norm1_g.shape[0] == 1
    w = _prep_weights(norm1_g[0], w_in[0], b_nsa_gate[0], cmp_pe_k[0], cmp_pe_v[0], cmp_k_w1[0], cmp_k_w2[0],
                      cmp_v_w1[0], cmp_v_w2[0], gla_w_a2[0], gla_b_a[0], gla_norm_g[0], w_br_a[0], w_br_b[0],
                      w_o[0], norm2_g[0], w_up[0], w_down[0], norm_f)
    y_p, rows_p, wins_p, s_p = _layer_prompt(x_prompt, w)
    caches = [c[0] for c in (cache_cmp_k, cache_cmp_v, cache_slc_k, cache_slc_v)]
    y_s, rows_s, wins_s, s_s = _layer_sample(x_sample, caches, [cache_win_k[0], cache_win_v[0]], state_gla[0],
                                             page_table, w)
    outs_p = [a[None] for a in rows_p + wins_p + [s_p]]
    outs_s = [a[None] for a in rows_s + wins_s + [s_s]]
    return (y_p, y_s, *outs_p, *outs_s)
```

```python
import functools

import jax
import jax.numpy as jnp
from jax import lax
from jax.experimental import pallas as pl
from jax.experimental.pallas import tpu as pltpu

D_MODEL = 2048
DEPTH = 1
PAGE_SIZE = 128
NSA_HEADS = 16
NSA_KV_HEADS = 4
NSA_HPG = NSA_HEADS // NSA_KV_HEADS
NSA_HEAD_DIM = 64
NSA_Q = NSA_HEADS * NSA_HEAD_DIM
NSA_KV = NSA_KV_HEADS * NSA_HEAD_DIM
CMP_STRIDE = 16
CMP_BLOCK = 32
CMP_HIDDEN = 128
SLC_BLOCK = 64
N_SELECT = 16
WINDOW = 512
GLA_HEADS = 4
GLA_DK = (D_MODEL // 4) // GLA_HEADS
GLA_DV = (D_MODEL // 2) // GLA_HEADS
GLA_RANK = 16
GLA_TAU = 16.0
D_FF = 4 * D_MODEL
EPS = 1e-6
NEG = -1e30
FORCE = 1e4
SPLITS = (NSA_Q, 6 * NSA_KV, 3 * NSA_HEADS,
          GLA_HEADS * GLA_DK, GLA_HEADS * GLA_DK, GLA_HEADS * GLA_DV, GLA_HEADS * GLA_DV,
          GLA_RANK, 2 * D_MODEL)

F32 = jnp.float32
BF16 = jnp.bfloat16
LANES = 128
VMEM_LIMIT_BYTES = 56 * 1024 * 1024
N_GATE = 3 * NSA_HEADS
GLA_CHUNK = 32
GLA_UNROLL = 8
NSA_TQ = 256
NSA_TK = 256
VT_ROWS = NSA_HEAD_DIM + 16
CMP_PAGES = 16
SDS = jax.ShapeDtypeStruct


def _params(*sem):
    return pltpu.CompilerParams(dimension_semantics=sem, vmem_limit_bytes=VMEM_LIMIT_BYTES)


def _resident(shape):
    nd = len(shape)
    return pl.BlockSpec(shape, lambda *_: (0,) * nd, pipeline_mode=pl.Buffered(1))


def _rms(x, g):
    return x * lax.rsqrt(jnp.mean(x * x, axis=-1, keepdims=True) + EPS) * g


def _proj_kernel(x_ref, g_ref, w_ref, b_ref, *out_refs, spec):
    h = _rms(x_ref[...], g_ref[...]).astype(BF16)
    misc = None
    for o_ref, (kind, c0, c1, scale) in zip(out_refs, spec):
        if kind == 'gates':
            for g in range(NSA_KV_HEADS):
                o_ref[g] = misc if g == 0 else pltpu.roll(misc, LANES - g * 3 * NSA_HPG, axis=1)
            continue
        r = jnp.dot(h, w_ref[:, c0:c1], preferred_element_type=F32)
        if scale != 1.0:
            r = r * scale
        if kind == 'f32':
            o_ref[...] = r
        elif kind == 'bf16':
            o_ref[...] = r.astype(BF16)
        elif kind == 'sigmoid':
            o_ref[...] = jax.nn.sigmoid(r)
        elif kind == 'hm':
            for i in range((c1 - c0) // NSA_HEAD_DIM):
                o_ref[i] = r[:, i * NSA_HEAD_DIM:(i + 1) * NSA_HEAD_DIM].astype(BF16)
        elif kind == 'hmt':
            rt = r.T
            for i in range((c1 - c0) // NSA_HEAD_DIM):
                o_ref[i] = rt[i * NSA_HEAD_DIM:(i + 1) * NSA_HEAD_DIM].astype(BF16)
        elif kind == 'hmvt':
            rt = r.T
            sub = lax.broadcasted_iota(jnp.int32, (VT_ROWS - NSA_HEAD_DIM, r.shape[0]), 0)
            ones = jnp.where(sub == 0, 1.0, 0.0)
            for i in range((c1 - c0) // NSA_HEAD_DIM):
                piece = rt[i * NSA_HEAD_DIM:(i + 1) * NSA_HEAD_DIM]
                o_ref[i] = jnp.concatenate([piece, ones], axis=0).astype(BF16)
        elif kind == 'misc':
            lane = lax.broadcasted_iota(jnp.int32, r.shape, 1)
            misc = jnp.where(lane < N_GATE, jax.nn.sigmoid(r + b_ref[...]), r)
            o_ref[...] = misc


def _proj(x, norm_g, w, bias, spec, tm):
    m, d = x.shape
    n = w.shape[1]
    assert m % tm == 0
    out_shape, out_specs = [], []
    for kind, c0, c1, _ in spec:
        if kind == 'hm':
            nh = (c1 - c0) // NSA_HEAD_DIM
            out_shape.append(SDS((nh, m, NSA_HEAD_DIM), BF16))
            out_specs.append(pl.BlockSpec((nh, tm, NSA_HEAD_DIM), lambda i: (0, i, 0)))
        elif kind in ('hmt', 'hmvt'):
            nh = (c1 - c0) // NSA_HEAD_DIM
            nrow = NSA_HEAD_DIM if kind == 'hmt' else VT_ROWS
            out_shape.append(SDS((nh, nrow, m), BF16))
            out_specs.append(pl.BlockSpec((nh, nrow, tm), lambda i: (0, 0, i)))
        elif kind == 'gates':
            out_shape.append(SDS((NSA_KV_HEADS, m, LANES), F32))
            out_specs.append(pl.BlockSpec((NSA_KV_HEADS, tm, LANES), lambda i: (0, i, 0)))
        else:
            out_shape.append(SDS((m, c1 - c0), BF16 if kind == 'bf16' else F32))
            out_specs.append(pl.BlockSpec((tm, c1 - c0), lambda i: (i, 0)))
    return pl.pallas_call(
        functools.partial(_proj_kernel, spec=tuple(spec)),
        grid=(m // tm,),
        in_specs=[pl.BlockSpec((tm, d), lambda i: (i, 0)), _resident((1, d)), _resident((d, n)),
                  _resident((1, LANES))],
        out_specs=out_specs, out_shape=out_shape,
        compiler_params=_params("parallel"),
    )(x, norm_g, w, bias)


def _cmp_kernel(pt_ref, k_hbm, v_hbm, pek_ref, pev_ref, w1k_ref, w1v_ref, w2k_ref, w2v_ref,
                ok_ref, ov_ref, kbuf, vbuf, tb0, tb1, hk, hv, sem, *, n_pages_step, native):
    b = pl.program_id(0)
    s = pl.program_id(1)
    ns = pl.num_programs(1)
    t = b * ns + s
    total = pl.num_programs(0) * ns
    rows = n_pages_step * 8
    dh, hid_n = NSA_HEAD_DIM, CMP_HIDDEN

    def copies(tt, slot):
        bb = tt // ns
        ss = tt % ns
        out = []
        for p in range(n_pages_step):
            page = pt_ref[bb, ss * n_pages_step + p]
            out.append(pltpu.make_async_copy(k_hbm.at[page], kbuf.at[slot, p], sem.at[slot, 0]))
            out.append(pltpu.make_async_copy(v_hbm.at[page], vbuf.at[slot, p], sem.at[slot, 1]))
        return out

    slot = t % 2

    @pl.when(t == 0)
    def _():
        for c in copies(t, slot):
            c.start()

    @pl.when(t + 1 < total)
    def _():
        for c in copies(t + 1, 1 - slot):
            c.start()

    for c in copies(t, slot):
        c.wait()

    r0 = pl.multiple_of(s * rows, rows)
    tbufs = (tb0, tb1)
    low = lax.broadcasted_iota(jnp.int32, (rows, LANES), 1) < dh
    tok_pairs = CMP_STRIDE // 2
    for buf, w1_ref, h_ref in ((kbuf, w1k_ref, hk), (vbuf, w1v_ref, hv)):
        for p in range(n_pages_step):
            for q in range(2):
                if native:
                    tile = buf[slot, p, q * LANES:(q + 1) * LANES, :].T
                else:
                    tile = buf[slot, p, :, q * LANES:(q + 1) * LANES]
                tbufs[q][p * PAGE_SIZE:(p + 1) * PAGE_SIZE, :] = tile
        for q in range(2):
            even, odd = [], []
            for m in range(tok_pairs):
                x0 = tbufs[q][pl.ds(2 * m, rows, stride=CMP_STRIDE), :]
                x1 = tbufs[q][pl.ds(2 * m + 1, rows, stride=CMP_STRIDE), :]
                even.append(jnp.where(low, x0, pltpu.roll(x1, dh, axis=1)))
                odd.append(jnp.where(low, pltpu.roll(x0, dh, axis=1), x1))
            for g, parts in ((2 * q, even), (2 * q + 1, odd)):
                z = jnp.concatenate(parts, axis=1).astype(BF16)
                h_ref[pl.ds(r0, rows), g * 2 * hid_n:(g + 1) * 2 * hid_n] = jnp.dot(
                    z, w1_ref[...], preferred_element_type=F32)

    @pl.when(s == ns - 1)
    def _():
        for h_ref, pe_ref, w1_ref, w2_ref, o_ref in ((hk, pek_ref, w1k_ref, w2k_ref, ok_ref),
                                                     (hv, pev_ref, w1v_ref, w2v_ref, ov_ref)):
            n_chunk = h_ref.shape[0]
            pe = jnp.broadcast_to(pe_ref[...], (2, 16, pe_ref.shape[2])).astype(BF16)
            pe_term = (jnp.dot(pe[0], w1_ref[:, 0:hid_n], preferred_element_type=F32)
                       + jnp.dot(pe[1], w1_ref[:, hid_n:2 * hid_n], preferred_element_type=F32))[0:1]
            hid = []
            for g in range(NSA_KV_HEADS):
                first = h_ref[:, g * 2 * hid_n:g * 2 * hid_n + hid_n]
                last = h_ref[:, g * 2 * hid_n + hid_n:(g + 1) * 2 * hid_n]
                hid.append(first + pltpu.roll(last, n_chunk - 1, axis=0) + pe_term)
            hid = jnp.concatenate(hid, axis=1)
            res = jnp.dot(jax.nn.gelu(hid).astype(BF16), w2_ref[...], preferred_element_type=F32)
            if o_ref is ok_ref:
                for g in range(NSA_KV_HEADS):
                    o_ref[0, g] = res[:, g * NSA_HEAD_DIM:(g + 1) * NSA_HEAD_DIM].astype(BF16)
            else:
                res_t = res.T
                for g in range(NSA_KV_HEADS):
                    o_ref[0, g] = res_t[g * NSA_HEAD_DIM:(g + 1) * NSA_HEAD_DIM].astype(BF16)


def _cmp_weights(pe, w1, w2):
    g = NSA_KV_HEADS
    half = CMP_STRIDE * NSA_HEAD_DIM
    pe_t = pe.reshape(2, 1, half)
    w1ab = jnp.concatenate([w1[:half], w1[half:]], axis=1)
    w2b = jnp.einsum('jd,gh->gjhd', w2, jnp.eye(g, dtype=F32)).reshape(g * CMP_HIDDEN, g * NSA_HEAD_DIM)
    return pe_t, w1ab.astype(BF16), w2b.astype(BF16)


def _compress(page_table, k_pages, v_pages, cw, native):
    bsz, n_pages = page_table.shape
    p_step = min(CMP_PAGES, n_pages)
    ns = n_pages // p_step
    n_chunk = n_pages * 8
    page_shape = k_pages.shape[1:]
    assert page_shape == ((NSA_KV, PAGE_SIZE) if native else (PAGE_SIZE, NSA_KV)) and NSA_KV == 2 * LANES
    half = CMP_STRIDE * NSA_HEAD_DIM
    gh = NSA_KV_HEADS * CMP_HIDDEN
    k_sds = SDS((bsz, NSA_KV_HEADS, n_chunk, NSA_HEAD_DIM), BF16)
    v_sds = SDS((bsz, NSA_KV_HEADS, NSA_HEAD_DIM, n_chunk), BF16)
    k_spec = pl.BlockSpec((1, NSA_KV_HEADS, n_chunk, NSA_HEAD_DIM), lambda b, s, pt: (b, 0, 0, 0))
    v_spec = pl.BlockSpec((1, NSA_KV_HEADS, NSA_HEAD_DIM, n_chunk), lambda b, s, pt: (b, 0, 0, 0))
    grid_spec = pltpu.PrefetchScalarGridSpec(
        num_scalar_prefetch=1, grid=(bsz, ns),
        in_specs=[pl.BlockSpec(memory_space=pl.ANY), pl.BlockSpec(memory_space=pl.ANY),
                  _resident((2, 1, half)), _resident((2, 1, half)),
                  _resident((half, 2 * CMP_HIDDEN)), _resident((half, 2 * CMP_HIDDEN)),
                  _resident((gh, NSA_KV)), _resident((gh, NSA_KV))],
        out_specs=[k_spec, v_spec],
        scratch_shapes=[pltpu.VMEM((2, p_step) + page_shape, F32), pltpu.VMEM((2, p_step) + page_shape, F32),
                        pltpu.VMEM((p_step * PAGE_SIZE, LANES), F32), pltpu.VMEM((p_step * PAGE_SIZE, LANES), F32),
                        pltpu.VMEM((n_chunk, 2 * gh), F32), pltpu.VMEM((n_chunk, 2 * gh), F32),
                        pltpu.SemaphoreType.DMA((2, 2))])
    return pl.pallas_call(
        functools.partial(_cmp_kernel, n_pages_step=p_step, native=native),
        grid_spec=grid_spec, out_shape=[k_sds, v_sds],
        compiler_params=_params("arbitrary", "arbitrary"),
    )(page_table, k_pages, v_pages, cw['pe_k'], cw['pe_v'], cw['w1_k'], cw['w1_v'], cw['w2_k'], cw['w2_v'])


def _softmax_rows(s, mask):
    sm = jnp.where(mask, s, NEG)
    m = jnp.max(sm, axis=-1, keepdims=True)
    e = jnp.where(mask, jnp.exp(sm - m), 0.0)
    return e, jnp.sum(e, axis=-1, keepdims=True)


def _safe_inv(l):
    return jnp.where(l > 0.0, 1.0 / jnp.where(l > 0.0, l, 1.0), 0.0)


M_INIT = -1e20


def _nsa_prompt_kernel(qt_ref, kc_ref, vct_ref, ks_ref, vst_ref, kw_ref, vwt_ref, gate_ref,
                       o_ref, *scratch, n_cmp, n_slc, n_sel):
    tq, tk, hpg, dh = NSA_TQ, NSA_TK, NSA_HPG, NSA_HEAD_DIM
    rk_refs = scratch[:tq // LANES]
    selb_ref, m_ref, acc_ref = scratch[tq // LANES:tq // LANES + 3]
    s_refs, p_refs, a_refs = (scratch[tq // LANES + 3 + 2 * i:tq // LANES + 5 + 2 * i] for i in range(3))
    qi = pl.program_id(2)
    q0 = qi * tq
    n_chunk = kc_ref.shape[2]
    r = SLC_BLOCK // CMP_STRIDE
    n_row = n_chunk // r

    ci = lax.broadcasted_iota(jnp.int32, (n_chunk, tq), 0)
    pos_c = q0 + lax.broadcasted_iota(jnp.int32, (n_chunk, tq), 1)
    m_c = (ci < n_cmp) & (ci * CMP_STRIDE + (CMP_BLOCK - 1) <= pos_c)
    kc = kc_ref[0, 0]
    vct = vct_ref[0, 0]
    o_c = []
    psum = None
    for h in range(hpg):
        s = jnp.where(m_c, jnp.dot(kc, qt_ref[0, h], preferred_element_type=F32), NEG)
        e = jnp.where(m_c, jnp.exp(s - jnp.max(s, axis=0, keepdims=True)), 0.0)
        p = e * _safe_inv(jnp.sum(e, axis=0, keepdims=True))
        o_c.append(jnp.dot(vct, p.astype(BF16), preferred_element_type=F32))
        psum = p if psum is None else psum + p

    imp = pltpu.roll(psum, 1, axis=0) + psum
    for o in range(1, r):
        imp = imp + pltpu.roll(psum, n_chunk - o, axis=0)
    parts = []
    for i, rk_ref in enumerate(rk_refs):
        rk_ref[...] = imp[:, i * LANES:(i + 1) * LANES]
        parts.append(rk_ref[pl.ds(0, n_row, stride=r), :])
    imp_b = jnp.concatenate(parts, axis=1)
    j_io = lax.broadcasted_iota(jnp.int32, (n_row, tq), 0)
    pos_b = q0 + lax.broadcasted_iota(jnp.int32, (n_row, tq), 1)
    valid = (j_io < n_slc) & (j_io * SLC_BLOCK <= pos_b)
    forced = (j_io == 0) | (j_io == pos_b // SLC_BLOCK)
    sc = jnp.where(valid, imp_b + jnp.where(forced, FORCE, 0.0), NEG)
    rank = jnp.zeros((n_row, tq), F32)
    for k in range(n_slc):
        ck = sc[k:k + 1, :]
        beats = (ck > sc) | ((ck == sc) & (j_io > k))
        rank = rank + jnp.where(beats, 1.0, 0.0)
    selb_ref[...] = jnp.where(rank < n_sel, 0.0, NEG)

    dlt = lax.broadcasted_iota(jnp.int32, (tk, tq), 0) - lax.broadcasted_iota(jnp.int32, (tk, tq), 1)
    blocks_per_tile = tk // SLC_BLOCK

    def flash(k_ref, vt_ref, lo, hi, bias_fn):
        n_tiles = k_ref.shape[2] // tk
        m_ref[...] = jnp.full(m_ref.shape, M_INIT, F32)
        acc_ref[...] = jnp.zeros(acc_ref.shape, F32)
        p_refs[1][...] = jnp.zeros(p_refs[1].shape, BF16)
        a_refs[1][...] = jnp.ones(a_refs[1].shape, F32)

        def tile_start(kb):
            return pl.multiple_of(jnp.clip(kb, 0, n_tiles - 1) * tk, tk)

        def scores(kb, s_ref):
            kk = k_ref[0, 0, pl.ds(tile_start(kb), tk), :]
            bias = bias_fn(kb)
            for h in range(hpg):
                s_ref[h] = jnp.dot(kk, qt_ref[0, h], preferred_element_type=F32) + bias

        def softmax(s_ref, p_ref, a_ref):
            for h in range(hpg):
                s = s_ref[h]
                m_prev = m_ref[h]
                m_new = jnp.maximum(m_prev, jnp.max(s, axis=0, keepdims=True))
                p_ref[h] = jnp.exp(s - m_new).astype(BF16)
                a_ref[h] = jnp.exp(m_prev - m_new)
                m_ref[h] = m_new

        def weighted_values(kb, p_ref, a_ref):
            vt = vt_ref[0, :, pl.ds(tile_start(kb), tk)]
            for h in range(hpg):
                acc_ref[h] = a_ref[h] * acc_ref[h] + jnp.dot(vt, p_ref[h], preferred_element_type=F32)

        scores(lo, s_refs[0])

        def body(j, carry):
            i = lo + 2 * j
            scores(i + 1, s_refs[1])
            softmax(s_refs[0], p_refs[0], a_refs[0])
            weighted_values(i - 1, p_refs[1], a_refs[1])
            scores(i + 2, s_refs[0])
            softmax(s_refs[1], p_refs[1], a_refs[1])
            weighted_values(i, p_refs[0], a_refs[0])
            return carry

        n_pairs = (hi - lo + 1) // 2
        lax.fori_loop(0, n_pairs, body, 0)
        weighted_values(lo + 2 * n_pairs - 1, p_refs[1], a_refs[1])
        return [acc_ref[h, 0:dh, :] * _safe_inv(acc_ref[h, dh:dh + 1, :]) for h in range(hpg)]

    def slc_bias(kb):
        first = jnp.minimum(kb, n_slc // blocks_per_tile - 1) * blocks_per_tile
        rows = [jnp.broadcast_to(selb_ref[pl.ds(first + i, 1), :], (SLC_BLOCK, tq))
                for i in range(blocks_per_tile)]
        return jnp.where(dlt + (kb * tk - q0) <= 0, jnp.concatenate(rows, axis=0), NEG)

    def win_bias(kb):
        d = dlt + (kb * tk - q0)
        return jnp.where((d <= 0) & (d >= -WINDOW), 0.0, NEG)

    hi = (q0 + tq - 1) // tk + 1
    o_s = flash(ks_ref, vst_ref, 0, hi, slc_bias)
    o_w = flash(kw_ref, vwt_ref, jnp.maximum(q0 - WINDOW, 0) // tk, hi, win_bias)

    gt = gate_ref[0].T
    outs = [gt[h:h + 1] * o_c[h] + gt[hpg + h:hpg + h + 1] * o_s[h] + gt[2 * hpg + h:2 * hpg + h + 1] * o_w[h]
            for h in range(hpg)]
    o_ref[...] = jnp.concatenate(outs, axis=0).T.astype(BF16)


def _nsa_prompt(q_t, ks_hm, vs_t, kw_hm, vw_t, kcmp, vcmp_t, gates_hm, bsz, t):
    g, hpg, dh, tq = NSA_KV_HEADS, NSA_HPG, NSA_HEAD_DIM, NSA_TQ
    r = SLC_BLOCK // CMP_STRIDE
    n_slc = t // SLC_BLOCK
    assert t % NSA_TK == 0 and t % tq == 0 and n_slc * r <= LANES and NSA_TK % SLC_BLOCK == 0
    nq = t // tq
    n_cmp = kcmp.shape[2] - 1
    if kcmp.shape[2] < LANES:
        fill = LANES - kcmp.shape[2]
        kcmp = jnp.pad(kcmp, ((0, 0), (0, 0), (0, fill), (0, 0)))
        vcmp_t = jnp.pad(vcmp_t, ((0, 0), (0, 0), (0, 0), (0, fill)))
    n_chunk = kcmp.shape[2]
    assert n_chunk == LANES
    q4 = q_t.reshape(g, hpg, dh, bsz * t)

    def k_spec():
        return pl.BlockSpec((1, 1, t, dh), lambda b, gg, qi: (gg, b, 0, 0))

    def vt_spec():
        return pl.BlockSpec((1, VT_ROWS, t), lambda b, gg, qi: (gg, 0, b))

    def per_bt(a):
        return a.reshape(g, bsz, t, a.shape[-1])

    return pl.pallas_call(
        functools.partial(_nsa_prompt_kernel, n_cmp=n_cmp, n_slc=n_slc, n_sel=min(N_SELECT, n_slc)),
        grid=(bsz, g, nq),
        in_specs=[pl.BlockSpec((1, hpg, dh, tq), lambda b, gg, qi: (gg, 0, 0, b * nq + qi)),
                  pl.BlockSpec((1, 1, n_chunk, dh), lambda b, gg, qi: (b, gg, 0, 0)),
                  pl.BlockSpec((1, 1, dh, n_chunk), lambda b, gg, qi: (b, gg, 0, 0)),
                  k_spec(), vt_spec(), k_spec(), vt_spec(),
                  pl.BlockSpec((1, tq, LANES), lambda b, gg, qi: (gg, b * nq + qi, 0))],
        out_specs=pl.BlockSpec((tq, hpg * dh), lambda b, gg, qi: (b * nq + qi, gg)),
        out_shape=SDS((bsz * t, NSA_Q), BF16),
        scratch_shapes=[pltpu.VMEM((n_chunk, LANES), F32)] * (tq // LANES) + [
                        pltpu.VMEM((n_chunk // r, tq), F32),
                        pltpu.VMEM((hpg, 1, tq), F32), pltpu.VMEM((hpg, VT_ROWS, tq), F32),
                        pltpu.VMEM((hpg, NSA_TK, tq), F32), pltpu.VMEM((hpg, NSA_TK, tq), F32),
                        pltpu.VMEM((hpg, NSA_TK, tq), BF16), pltpu.VMEM((hpg, NSA_TK, tq), BF16),
                        pltpu.VMEM((hpg, 1, tq), F32), pltpu.VMEM((hpg, 1, tq), F32)],
        compiler_params=_params("parallel", "parallel", "arbitrary"),
    )(q4, kcmp, vcmp_t, per_bt(ks_hm), vs_t, per_bt(kw_hm), vw_t, gates_hm)


def _gla_prompt_kernel(q_ref, k_ref, v_ref, r_ref, misc_ref, wa_ref, ba_ref, ng_ref,
                       o_ref, s_ref, qe_ref, qt_ref, kt_ref, kh_ref, d_ref, u_ref, st_ref):
    t = q_ref.shape[0]
    c = GLA_CHUNK
    n = t // c
    dk, dv = GLA_DK, GLA_DV
    lr = misc_ref[:, N_GATE:N_GATE + GLA_RANK].astype(BF16)
    x = jnp.dot(lr, wa_ref[...], preferred_element_type=F32) + ba_ref[...]
    g = jax.nn.log_sigmoid(x) / GLA_TAU
    row = lax.broadcasted_iota(jnp.int32, (t, dk), 0) % c
    b = g
    sh = 1
    while sh < c:
        b = b + jnp.where(row >= sh, pltpu.roll(b, sh, axis=0), 0.0)
        sh *= 2
    b3 = b.reshape(n, c, dk)
    b_last = jnp.broadcast_to(b3[:, c - 1:c, :], (n, c, dk)).reshape(t, dk)
    b_mid = jnp.broadcast_to(b3[:, c // 2 - 1:c // 2, :], (n, c, dk)).reshape(t, dk)
    q = q_ref[...] * (GLA_DK ** -0.5)
    k = k_ref[...]
    qe_ref[...] = (q * jnp.exp(b)).astype(BF16)
    qt_ref[...] = (q * jnp.exp(b - b_mid)).astype(BF16)
    kt_ref[...] = (k * jnp.exp(b_mid - b)).astype(BF16)
    kh_ref[...] = (k * jnp.exp(b_last - b)).astype(BF16)
    d_ref[...] = jnp.exp(b_last)
    tn = (((0,), (0,)), ((), ()))
    nt = (((1,), (1,)), ((), ()))

    def chunk_update(i, carry):
        r0 = pl.multiple_of(i * c, c)
        u_ref[i] = lax.dot_general(v_ref[pl.ds(r0, c), :], kh_ref[pl.ds(r0, c), :], tn,
                                   preferred_element_type=F32)
        return carry

    lax.fori_loop(0, n, chunk_update, 0, unroll=GLA_UNROLL)

    st_ref[...] = jnp.zeros((dv, dk), F32)

    def recur(i, carry):
        st = st_ref[...]
        st_ref[...] = st * d_ref[pl.ds(pl.multiple_of(i * c, c), 1), :] + u_ref[i]
        u_ref[i] = st
        return carry

    lax.fori_loop(0, n, recur, 0)
    s_ref[0, 0] = st_ref[...].T

    tril = lax.broadcasted_iota(jnp.int32, (c, c), 0) >= lax.broadcasted_iota(jnp.int32, (c, c), 1)
    ng = ng_ref[...]

    def chunk_out(i, carry):
        r0 = pl.multiple_of(i * c, c)
        vv = v_ref[pl.ds(r0, c), :]
        o = lax.dot_general(qe_ref[pl.ds(r0, c), :], u_ref[i].astype(BF16), nt, preferred_element_type=F32)
        a = lax.dot_general(qt_ref[pl.ds(r0, c), :], kt_ref[pl.ds(r0, c), :], nt, preferred_element_type=F32)
        a = jnp.where(tril, a, 0.0).astype(BF16)
        o = o + jnp.dot(a, vv, preferred_element_type=F32)
        o = _rms(o, ng) * jax.nn.silu(r_ref[pl.ds(r0, c), :])
        o_ref[pl.ds(r0, c), :] = o.astype(BF16)
        return carry

    lax.fori_loop(0, n, chunk_out, 0, unroll=GLA_UNROLL)


def _gla_prompt(q_l, k_l, v_l, r_l, misc, wa, ba, ng, bsz, t):
    h, dk, dv = GLA_HEADS, GLA_DK, GLA_DV
    assert t % GLA_CHUNK == 0
    n = t // GLA_CHUNK
    return pl.pallas_call(
        _gla_prompt_kernel,
        grid=(bsz, h),
        in_specs=[pl.BlockSpec((t, dk), lambda b, hh: (b, hh)), pl.BlockSpec((t, dk), lambda b, hh: (b, hh)),
                  pl.BlockSpec((t, dv), lambda b, hh: (b, hh)), pl.BlockSpec((t, dv), lambda b, hh: (b, hh)),
                  pl.BlockSpec((t, LANES), lambda b, hh: (b, 0)),
                  pl.BlockSpec((GLA_RANK, dk), lambda b, hh: (0, hh)), pl.BlockSpec((1, dk), lambda b, hh: (0, hh)),
                  _resident((1, dv))],
        out_specs=[pl.BlockSpec((t, dv), lambda b, hh: (b, hh)),
                   pl.BlockSpec((1, 1, dk, dv), lambda b, hh: (b, hh, 0, 0))],
        out_shape=[SDS((bsz * t, h * dv), BF16), SDS((bsz, h, dk, dv), F32)],
        scratch_shapes=[pltpu.VMEM((t, dk), BF16), pltpu.VMEM((t, dk), BF16), pltpu.VMEM((t, dk), BF16),
                        pltpu.VMEM((t, dk), BF16), pltpu.VMEM((t, dk), F32),
                        pltpu.VMEM((n, dv, dk), F32), pltpu.VMEM((dv, dk), F32)],
        compiler_params=_params("parallel", "parallel"),
    )(q_l, k_l, v_l, r_l, misc, wa, ba, ng)


def _merge_kernel(oa_ref, ob_ref, x_ref, g_ref, wg_ref, wa_ref, wb_ref, u_ref):
    h = _rms(x_ref[...], g_ref[...]).astype(BF16)
    d = D_MODEL
    ga = jax.nn.sigmoid(jnp.dot(h, wg_ref[:, 0:d], preferred_element_type=F32))
    u = ga * jnp.dot(oa_ref[...], wa_ref[...], preferred_element_type=F32)
    gb = jax.nn.sigmoid(jnp.dot(h, wg_ref[:, d:2 * d], preferred_element_type=F32))
    u = u + gb * jnp.dot(ob_ref[...], wb_ref[...], preferred_element_type=F32)
    u_ref[...] = u.astype(BF16)


def _merge(o_a, o_b, x, norm_g, w_gate, w_a, w_b, tm):
    m, d = x.shape
    assert m % tm == 0
    return pl.pallas_call(
        _merge_kernel, grid=(m // tm,),
        in_specs=[pl.BlockSpec((tm, NSA_Q), lambda i: (i, 0)), pl.BlockSpec((tm, GLA_HEADS * GLA_DV), lambda i: (i, 0)),
                  pl.BlockSpec((tm, d), lambda i: (i, 0)), _resident((1, d)), _resident((d, 2 * d)),
                  _resident(w_a.shape), _resident(w_b.shape)],
        out_specs=pl.BlockSpec((tm, d), lambda i: (i, 0)),
        out_shape=SDS((m, d), BF16),
        compiler_params=_params("parallel"),
    )(o_a, o_b, x, norm_g, w_gate, w_a, w_b)


def _mlp_kernel(x_ref, u_ref, wo_ref, g2_ref, wu_ref, wd_ref, gf_ref, y_ref, x1_ref, h_ref, acc_ref):
    j = pl.program_id(1)

    @pl.when(j == 0)
    def _():
        x1 = x_ref[...] + jnp.dot(u_ref[...], wo_ref[...], preferred_element_type=F32)
        x1_ref[...] = x1
        h_ref[...] = _rms(x1, g2_ref[...]).astype(BF16)
        acc_ref[...] = jnp.zeros_like(acc_ref)

    up = jnp.maximum(jnp.dot(h_ref[...], wu_ref[...], preferred_element_type=F32), 0.0)
    acc_ref[...] += jnp.dot((up * up).astype(BF16), wd_ref[...], preferred_element_type=F32)

    @pl.when(j == pl.num_programs(1) - 1)
    def _():
        y_ref[...] = _rms(x1_ref[...] + acc_ref[...], gf_ref[...])


def _mlp(x, u, w_o, g2, w_up, w_down, gf, tm, tf):
    m, d = x.shape
    ff = w_up.shape[1]
    assert m % tm == 0 and ff % tf == 0
    return pl.pallas_call(
        _mlp_kernel, grid=(m // tm, ff // tf),
        in_specs=[pl.BlockSpec((tm, d), lambda i, j: (i, 0)), pl.BlockSpec((tm, d), lambda i, j: (i, 0)),
                  _resident((d, d)), _resident((1, d)),
                  pl.BlockSpec((d, tf), lambda i, j: (0, j)), pl.BlockSpec((tf, d), lambda i, j: (j, 0)),
                  _resident((1, d))],
        out_specs=pl.BlockSpec((tm, d), lambda i, j: (i, 0)),
        out_shape=SDS((m, d), F32),
        scratch_shapes=[pltpu.VMEM((tm, d), F32), pltpu.VMEM((tm, d), BF16), pltpu.VMEM((tm, d), F32)],
        compiler_params=_params("parallel", "arbitrary"),
    )(x, u, w_o, g2, w_up, w_down, gf)


def _prep_weights(norm1_g, w_in, b_nsa_gate, cmp_pe_k, cmp_pe_v, cmp_k_w1, cmp_k_w2, cmp_v_w1, cmp_v_w2,
                  gla_w_a2, gla_b_a, gla_norm_g, w_br_a, w_br_b, w_o, norm2_g, w_up, w_down, norm_f):
    pts = [0]
    for s in SPLITS:
        pts.append(pts[-1] + s)
    c_q, c_kv, c_g, c_ql, c_kl, c_vl, c_rl, c_lr, c_br, c_end = pts
    gcols = jnp.asarray([c_g + (g * NSA_HPG + h) * 3 + c for g in range(NSA_KV_HEADS)
                         for c in range(3) for h in range(NSA_HPG)], jnp.int32)
    pad = jnp.zeros((D_MODEL, LANES - N_GATE - GLA_RANK), F32)
    w_misc = jnp.concatenate([w_in[:, gcols], w_in[:, c_lr:c_br], pad], axis=1)
    b_misc = jnp.concatenate([b_nsa_gate[gcols - c_g], jnp.zeros((LANES - N_GATE,), F32)])[None, :]
    w = dict(
        norm1=norm1_g[None, :], norm2=norm2_g[None, :], norm_f=norm_f[None, :],
        w_a=w_in[:, c_q:c_g].astype(BF16),
        w_b=jnp.concatenate([w_in[:, c_ql:c_lr], w_misc], axis=1).astype(BF16),
        w_gate=w_in[:, c_br:c_end].astype(BF16),
        b_misc=b_misc,
        gla_wa=gla_w_a2.astype(BF16), gla_ba=gla_b_a[None, :], gla_ng=gla_norm_g[None, :],
        w_br_a=w_br_a.astype(BF16), w_br_b=w_br_b.astype(BF16), w_o=w_o.astype(BF16),
        w_up=w_up.astype(BF16), w_down=w_down.astype(BF16),
    )
    cw = {}
    cw['pe_k'], cw['w1_k'], cw['w2_k'] = _cmp_weights(cmp_pe_k, cmp_k_w1, cmp_k_w2)
    cw['pe_v'], cw['w1_v'], cw['w2_v'] = _cmp_weights(cmp_pe_v, cmp_v_w1, cmp_v_w2)
    w['cmp'] = cw
    return w


_KV6 = 6 * NSA_KV
_GQ = GLA_HEADS * GLA_DK
_GV = GLA_HEADS * GLA_DV


def _layer_prompt(x, w):
    bsz, t, d = x.shape
    m = bsz * t
    x2 = x.reshape(m, d)
    zb = jnp.zeros((1, LANES), F32)
    spec_a = [('hmt', 0, NSA_Q, NSA_HEAD_DIM ** -0.5)]
    spec_a += [('f32', NSA_Q + j * NSA_KV, NSA_Q + (j + 1) * NSA_KV, 1.0) for j in range(6)]
    spec_a += [('hm' if j % 2 == 0 else 'hmvt', NSA_Q + j * NSA_KV, NSA_Q + (j + 1) * NSA_KV, 1.0)
               for j in range(2, 6)]
    q_t, r0, r1, r2, r3, r4, r5, ks_hm, vs_t, kw_hm, vw_t = _proj(x2, w['norm1'], w['w_a'], zb, spec_a, 512)
    spec_b = [('f32', 0, _GQ, 1.0), ('f32', _GQ, 2 * _GQ, 1.0), ('bf16', 2 * _GQ, 2 * _GQ + _GV, 1.0),
              ('f32', 2 * _GQ + _GV, 2 * _GQ + 2 * _GV, 1.0),
              ('misc', 2 * _GQ + 2 * _GV, 2 * _GQ + 2 * _GV + LANES, 1.0), ('gates', 0, 0, 1.0)]
    q_l, k_l, v_l, r_l, misc, gates_hm = _proj(x2, w['norm1'], w['w_b'], w['b_misc'], spec_b, 512)

    n_pages = t // PAGE_SIZE
    ident = jnp.arange(bsz * n_pages, dtype=jnp.int32).reshape(bsz, n_pages)
    pages = (bsz * n_pages, PAGE_SIZE, NSA_KV)
    kcmp, vcmp = _compress(ident, r0.reshape(pages), r1.reshape(pages), w['cmp'], native=False)
    o_a = _nsa_prompt(q_t, ks_hm, vs_t, kw_hm, vw_t, kcmp, vcmp, gates_hm, bsz, t)
    o_b, s_new = _gla_prompt(q_l, k_l, v_l, r_l, misc, w['gla_wa'], w['gla_ba'], w['gla_ng'], bsz, t)
    u = _merge(o_a, o_b, x2, w['norm1'], w['w_gate'], w['w_br_a'], w['w_br_b'], 512)
    y = _mlp(x2, u, w['w_o'], w['norm2'], w['w_up'], w['w_down'], w['norm_f'], 512, 512)
    kvh = (bsz, t, NSA_KV_HEADS, NSA_HEAD_DIM)
    rows = [a.reshape(kvh) for a in (r0, r1, r2, r3)]
    n_keep = min(WINDOW, t)
    wins = [a.reshape(kvh)[:, t - n_keep:] for a in (r4, r5)]
    return y.reshape(bsz, t, d), rows, wins, s_new


def _group_rows(parts):
    rowg = lax.broadcasted_iota(jnp.int32, parts[0].shape, 0) // NSA_HPG
    out = parts[0]
    for g in range(1, NSA_KV_HEADS):
        out = jnp.where(rowg == g, parts[g], out)
    return out


def _sample_select_kernel(q_ref, kc_ref, vc_ref, oc_ref, idx_ref, *, n_cmp, pos, n_pick):
    q = q_ref[0]
    nt = (((1,), (1,)), ((), ()))
    n_chunk = kc_ref.shape[2]
    s = _group_rows([lax.dot_general(q, kc_ref[0, g], nt, preferred_element_type=F32)
                     for g in range(NSA_KV_HEADS)])
    lane = lax.broadcasted_iota(jnp.int32, s.shape, 1)
    mask = (lane < n_cmp) & (lane * CMP_STRIDE + (CMP_BLOCK - 1) <= pos)
    e, l = _softmax_rows(s, mask)
    p = e * _safe_inv(l)
    pb = p.astype(BF16)
    oc_ref[0] = _group_rows([lax.dot_general(pb, vc_ref[0, g], nt, preferred_element_type=F32)
                             for g in range(NSA_KV_HEADS)])
    nr = p.shape[0]
    y = p + pltpu.roll(p, nr - 1, axis=0)
    psum = y + pltpu.roll(y, nr - 2, axis=0)
    imp = pltpu.roll(psum, 1, axis=1) + psum
    r = SLC_BLOCK // CMP_STRIDE
    for o in range(1, r):
        imp = imp + pltpu.roll(psum, n_chunk - o, axis=1)
    blk = lane // r
    is_blk = lane % r == 0
    valid = blk * SLC_BLOCK <= pos
    forced = (blk == 0) | (blk == pos // SLC_BLOCK)
    score = jnp.where(is_blk & valid, imp + jnp.where(forced, FORCE, 0.0), NEG)
    lane_f = lane.astype(F32)
    out_lane = lax.broadcasted_iota(jnp.int32, (nr, LANES), 1)
    picked = jnp.zeros((nr, LANES), F32)
    for k in range(n_pick):
        mx = jnp.max(score, axis=-1, keepdims=True)
        ix = jnp.min(jnp.where(score == mx, lane_f, float(n_chunk)), axis=-1, keepdims=True)
        picked = jnp.where(out_lane == k, ix, picked)
        score = jnp.where(lane_f == ix, 2.0 * NEG, score)
    idx_ref[0] = picked.astype(jnp.int32) // r


def _sample_select(q_s, kcmp, vcmp, pos, n_pick):
    bsz, g, n_chunk, dh = kcmp.shape
    nh = NSA_HEADS
    cmp_spec = pl.BlockSpec((1, g, n_chunk, dh), lambda b: (b, 0, 0, 0))
    cmpt_spec = pl.BlockSpec((1, g, dh, n_chunk), lambda b: (b, 0, 0, 0))
    return pl.pallas_call(
        functools.partial(_sample_select_kernel, n_cmp=n_chunk - 1, pos=pos, n_pick=n_pick),
        grid=(bsz,),
        in_specs=[pl.BlockSpec((1, nh, dh), lambda b: (b, 0, 0)), cmp_spec, cmpt_spec],
        out_specs=[pl.BlockSpec((1, nh, dh), lambda b: (b, 0, 0)), pl.BlockSpec((1, nh, LANES), lambda b: (b, 0, 0))],
        out_shape=[SDS((bsz, nh, dh), F32), SDS((bsz, nh, LANES), jnp.int32)],
        compiler_params=_params("parallel"),
    )(q_s, kcmp, vcmp)


def _sample_attend_kernel(idx_ref, pt_ref, q_ref, oc_ref, ks_hbm, vs_hbm, kw_ref, vw_ref,
                          nks_ref, nvs_ref, nkw_ref, nvw_ref, gate_ref, o_ref, ksel, vsel, sem, *, n_pick):
    b = pl.program_id(0)
    g_n, dh = NSA_KV_HEADS, NSA_HEAD_DIM
    half = PAGE_SIZE // SLC_BLOCK

    def copies():
        out = []
        for g in range(g_n):
            for r in range(n_pick):
                page = pt_ref[b, idx_ref[b * g_n + g, r] // half]
                out.append(pltpu.make_async_copy(ks_hbm.at[page, g], ksel.at[g, r], sem.at[0]))
                out.append(pltpu.make_async_copy(vs_hbm.at[page, g], vsel.at[g, r], sem.at[1]))
        return out

    for c in copies():
        c.start()

    q = q_ref[0]
    qf = q.astype(F32)
    nt = (((1,), (1,)), ((), ()))

    def attend(keys_t, vals_t, bias, k_new, v_new):
        s = _group_rows([jnp.dot(q, keys_t[g].astype(BF16), preferred_element_type=F32)
                         + (0.0 if bias is None else bias[g]) for g in range(g_n)])
        s_new = _group_rows([jnp.sum(qf * k_new[:, g * dh:(g + 1) * dh], axis=-1, keepdims=True)
                             for g in range(g_n)])
        m = jnp.maximum(jnp.max(s, axis=-1, keepdims=True), s_new)
        e = jnp.exp(s - m)
        e_new = jnp.exp(s_new - m)
        l = jnp.sum(e, axis=-1, keepdims=True) + e_new
        eb = e.astype(BF16)
        acc = _group_rows([lax.dot_general(eb, vals_t[g].astype(BF16), nt, preferred_element_type=F32)
                           + e_new * v_new[:, g * dh:(g + 1) * dh] for g in range(g_n)])
        return acc / l

    o_w = attend([kw_ref[0, g] for g in range(g_n)], [vw_ref[0, g] for g in range(g_n)], None,
                 nkw_ref[0], nvw_ref[0])
    for c in copies():
        c.wait()
    lin = lax.broadcasted_iota(jnp.int32, (1, PAGE_SIZE), 1)
    bias = []
    for g in range(g_n):
        parts = []
        for r in range(n_pick):
            off = (idx_ref[b * g_n + g, r] % half) * SLC_BLOCK
            parts.append(jnp.where((lin >= off) & (lin < off + SLC_BLOCK), 0.0, NEG))
        bias.append(jnp.concatenate(parts, axis=1))

    def tiles(buf, g):
        return jnp.concatenate([buf[g, r] for r in range(n_pick)], axis=1)

    o_s = attend([tiles(ksel, g) for g in range(g_n)], [tiles(vsel, g) for g in range(g_n)], bias,
                 nks_ref[0], nvs_ref[0])

    gt = jnp.broadcast_to(gate_ref[0], (LANES, LANES)).T
    nh = NSA_HEADS
    o_ref[0] = (gt[0:nh, 0:dh] * oc_ref[0] + gt[nh:2 * nh, 0:dh] * o_s + gt[2 * nh:3 * nh, 0:dh] * o_w)


def _sample_attend(idx, page_table, q_s, o_c, slc_k, slc_v, win_k, win_v, new_rows, gates, n_pick):
    bsz = page_table.shape[0]
    nh, dh, g = NSA_HEADS, NSA_HEAD_DIM, NSA_KV_HEADS
    wlen = win_k.shape[-1]
    row_spec = pl.BlockSpec((1, 1, NSA_KV), lambda b, *_: (b, 0, 0))
    win_spec = pl.BlockSpec((1, g, dh, wlen), lambda b, *_: (b, 0, 0, 0))
    head_spec = pl.BlockSpec((1, nh, dh), lambda b, *_: (b, 0, 0))
    any_spec = pl.BlockSpec(memory_space=pl.ANY)
    grid_spec = pltpu.PrefetchScalarGridSpec(
        num_scalar_prefetch=2, grid=(bsz,),
        in_specs=[head_spec, head_spec, any_spec, any_spec, win_spec, win_spec,
                  row_spec, row_spec, row_spec, row_spec, pl.BlockSpec((1, 1, LANES), lambda b, *_: (b, 0, 0))],
        out_specs=head_spec,
        scratch_shapes=[pltpu.VMEM((g, n_pick, dh, PAGE_SIZE), F32), pltpu.VMEM((g, n_pick, dh, PAGE_SIZE), F32),
                        pltpu.SemaphoreType.DMA((2,))])
    return pl.pallas_call(
        functools.partial(_sample_attend_kernel, n_pick=n_pick),
        grid_spec=grid_spec, out_shape=SDS((bsz, nh, dh), F32),
        compiler_params=_params("arbitrary"),
    )(idx, page_table, q_s, o_c, slc_k, slc_v, win_k, win_v, *new_rows, gates)


def _gla_sample_kernel(q_ref, k_ref, v_ref, r_ref, misc_ref, wa_ref, ba_ref, ng_ref, s0_ref, o_ref, s_ref):
    dk, dv = GLA_DK, GLA_DV
    lr = jnp.broadcast_to(misc_ref[0][:, N_GATE:N_GATE + GLA_RANK], (16, GLA_RANK)).astype(BF16)
    x = jnp.dot(lr, wa_ref[...], preferred_element_type=F32)[0:1] + ba_ref[...]
    g_all = jax.nn.log_sigmoid(x) / GLA_TAU

    def col(v):
        t = jnp.broadcast_to(v, (dk, dk)).T
        return jnp.concatenate([t] * (dv // dk), axis=1)

    outs = []
    for h in range(GLA_HEADS):
        g = g_all[:, h * dk:(h + 1) * dk]
        q = q_ref[0][:, h * dk:(h + 1) * dk] * (GLA_DK ** -0.5)
        k = k_ref[0][:, h * dk:(h + 1) * dk]
        v = v_ref[0][:, h * dv:(h + 1) * dv]
        s0 = s0_ref[0, h]
        q_t = q * jnp.exp(g)
        k_t = k * jnp.exp(-g)
        a = jnp.sum(q_t * k_t, axis=-1, keepdims=True)
        o = jnp.sum(col(q_t) * s0, axis=0, keepdims=True) + a * v
        s_ref[0, h] = col(jnp.exp(g)) * s0 + col(k) * v
        outs.append(_rms(o, ng_ref[...]) * jax.nn.silu(r_ref[0][:, h * dv:(h + 1) * dv]))
    o_ref[0] = jnp.concatenate(outs, axis=-1)


def _gla_sample(q_l, k_l, v_l, r_l, misc, wa, ba, ng, s0):
    bsz, h, dk, dv = s0.shape

    def row(n):
        return pl.BlockSpec((1, 1, n), lambda b: (b, 0, 0))

    st_spec = pl.BlockSpec((1, h, dk, dv), lambda b: (b, 0, 0, 0))
    return pl.pallas_call(
        _gla_sample_kernel, grid=(bsz,),
        in_specs=[row(h * dk), row(h * dk), row(h * dv), row(h * dv), row(LANES),
                  _resident(wa.shape), _resident(ba.shape), _resident(ng.shape), st_spec],
        out_specs=[row(h * dv), st_spec],
        out_shape=[SDS((bsz, 1, h * dv), F32), SDS((bsz, h, dk, dv), F32)],
        compiler_params=_params("parallel"),
    )(q_l, k_l, v_l, r_l, misc, wa, ba, ng, s0)


SAMPLE_ROWS = 16


def _layer_sample(x, caches, wins, s0, page_table, w):
    bsz, t, d = x.shape
    n_pages = page_table.shape[1]
    pos = n_pages * PAGE_SIZE
    assert t == 1 and bsz <= SAMPLE_ROWS and pos % SLC_BLOCK == 0 and wins[0].shape[1] == WINDOW
    assert pos // SLC_BLOCK >= N_SELECT
    mp = SAMPLE_ROWS
    x2 = jnp.pad(x.reshape(bsz, d), ((0, mp - bsz), (0, 0)))
    zb = jnp.zeros((1, LANES), F32)
    spec_a = [('bf16', 0, NSA_Q, NSA_HEAD_DIM ** -0.5)]
    spec_a += [('f32', NSA_Q + j * NSA_KV, NSA_Q + (j + 1) * NSA_KV, 1.0) for j in range(6)]
    q_s, r0, r1, r2, r3, r4, r5 = _proj(x2, w['norm1'], w['w_a'], zb, spec_a, mp)
    spec_b = [('f32', 0, _GQ, 1.0), ('f32', _GQ, 2 * _GQ, 1.0), ('f32', 2 * _GQ, 2 * _GQ + _GV, 1.0),
              ('f32', 2 * _GQ + _GV, 2 * _GQ + 2 * _GV, 1.0),
              ('misc', 2 * _GQ + 2 * _GV, 2 * _GQ + 2 * _GV + LANES, 1.0)]
    q_l, k_l, v_l, r_l, misc = _proj(x2, w['norm1'], w['w_b'], w['b_misc'], spec_b, mp)

    cache_t = [a.transpose(0, 2, 3, 1) for a in caches]
    win_t = [a.transpose(0, 2, 3, 1) for a in wins]
    n_pool = caches[0].shape[0]
    cmp_pages = [a.reshape(n_pool, NSA_KV, PAGE_SIZE) for a in cache_t[:2]]
    kcmp, vcmp = _compress(page_table, cmp_pages[0], cmp_pages[1], w['cmp'], native=True)
    q_h = q_s.reshape(mp, NSA_HEADS, NSA_HEAD_DIM)
    n_pick = N_SELECT - 1
    o_c, idx = _sample_select(q_h, kcmp, vcmp, pos, n_pick)
    idx = idx[:, ::NSA_HPG, :n_pick].reshape(bsz * NSA_KV_HEADS, n_pick)
    gates = misc[:, :N_GATE].reshape(mp, NSA_KV_HEADS, 3, NSA_HPG).transpose(0, 2, 1, 3).reshape(mp, 1, N_GATE)
    gates = jnp.pad(gates, ((0, 0), (0, 0), (0, LANES - N_GATE)))
    new_rows = [a.reshape(mp, 1, NSA_KV) for a in (r2, r3, r4, r5)]
    o_a = _sample_attend(idx, page_table, q_h, o_c, cache_t[2], cache_t[3], win_t[0], win_t[1], new_rows, gates,
                         n_pick)
    o_a = jnp.pad(o_a.reshape(bsz, NSA_Q), ((0, mp - bsz), (0, 0))).astype(BF16)

    def r3d(a):
        return a.reshape(mp, 1, a.shape[-1])

    o_b, s_new = _gla_sample(r3d(q_l), r3d(k_l), r3d(v_l), r3d(r_l), r3d(misc), w['gla_wa'], w['gla_ba'],
                             w['gla_ng'], s0)
    o_b = jnp.pad(o_b.reshape(bsz, _GV), ((0, mp - bsz), (0, 0))).astype(BF16)
    u = _merge(o_a, o_b, x2, w['norm1'], w['w_gate'], w['w_br_a'], w['w_br_b'], mp)
    y = _mlp(x2, u, w['w_o'], w['norm2'], w['w_up'], w['w_down'], w['norm_f'], mp, 512)
    kvh = (bsz, 1, NSA_KV_HEADS, NSA_HEAD_DIM)
    rows = [a[:bsz].reshape(kvh) for a in (r0, r1, r2, r3)]
    new_wins = [jnp.concatenate([c[:, 1:], a[:bsz].reshape(kvh)], axis=1) for c, a in zip(wins, (r4, r5))]
    return y[:bsz].reshape(bsz, 1, d), rows, new_wins, s_new


def kernel(x_prompt, x_sample, cache_cmp_k, cache_cmp_v, cache_slc_k, cache_slc_v, cache_win_k, cache_win_v, state_gla, page_table, norm1_g, w_in, b_nsa_gate, cmp_pe_k, cmp_pe_v, cmp_k_w1, cmp_k_w2, cmp_v_w1, cmp_v_w2, gla_w_a2, gla_b_a, gla_norm_g, w_br_a, w_br_b, w_o, norm2_g, w_up, w_down, norm_f):
    assert DEPTH == 1 and norm1_g.shape[0] == 1
    w = _prep_weights(norm1_g[0], w_in[0], b_nsa_gate[0], cmp_pe_k[0], cmp_pe_v[0], cmp_k_w1[0], cmp_k_w2[0],
                      cmp_v_w1[0], cmp_v_w2[0], gla_w_a2[0], gla_b_a[0], gla_norm_g[0], w_br_a[0], w_br_b[0],
                      w_o[0], norm2_g[0], w_up[0], w_down[0], norm_f)
    y_p, rows_p, wins_p, s_p = _layer_prompt(x_prompt, w)
    caches = [c[0] for c in (cache_cmp_k, cache_cmp_v, cache_slc_k, cache_slc_v)]
    y_s, rows_s, wins_s, s_s = _layer_sample(x_sample, caches, [cache_win_k[0], cache_win_v[0]], state_gla[0],
                                             page_table, w)
    outs_p = [a[None] for a in rows_p + wins_p + [s_p]]
    outs_s = [a[None] for a in rows_s + wins_s + [s_s]]
    return (y_p, y_s, *outs_p, *outs_s)
```

```python
import functools

import jax
import jax.numpy as jnp
from jax import lax
from jax.experimental import pallas as pl
from jax.experimental.pallas import tpu as pltpu

D_MODEL = 2048
DEPTH = 1
PAGE_SIZE = 128
NSA_HEADS = 16
NSA_KV_HEADS = 4
NSA_HPG = NSA_HEADS // NSA_KV_HEADS
NSA_HEAD_DIM = 64
NSA_Q = NSA_HEADS * NSA_HEAD_DIM
NSA_KV = NSA_KV_HEADS * NSA_HEAD_DIM
CMP_STRIDE = 16
CMP_BLOCK = 32
CMP_HIDDEN = 128
SLC_BLOCK = 64
N_SELECT = 16
WINDOW = 512
GLA_HEADS = 4
GLA_DK = (D_MODEL // 4) // GLA_HEADS
GLA_DV = (D_MODEL // 2) // GLA_HEADS
GLA_RANK = 16
GLA_TAU = 16.0
D_FF = 4 * D_MODEL
EPS = 1e-6
NEG = -1e30
FORCE = 1e4
SPLITS = (NSA_Q, 6 * NSA_KV, 3 * NSA_HEADS,
          GLA_HEADS * GLA_DK, GLA_HEADS * GLA_DK, GLA_HEADS * GLA_DV, GLA_HEADS * GLA_DV,
          GLA_RANK, 2 * D_MODEL)

F32 = jnp.float32
BF16 = jnp.bfloat16
LANES = 128
VMEM_LIMIT_BYTES = 56 * 1024 * 1024
N_GATE = 3 * NSA_HEADS
GLA_CHUNK = 32
GLA_UNROLL = 8
NSA_TQ = 256
NSA_TK = 256
VT_ROWS = NSA_HEAD_DIM + 16
CMP_PAGES = 16
PROJ_TM = 512
MERGE_TM = 256
MLP_TM = 1024
MLP_TF = 512
SDS = jax.ShapeDtypeStruct


def _params(*sem):
    return pltpu.CompilerParams(dimension_semantics=sem, vmem_limit_bytes=VMEM_LIMIT_BYTES)


def _resident(shape):
    nd = len(shape)
    return pl.BlockSpec(shape, lambda *_: (0,) * nd, pipeline_mode=pl.Buffered(1))


def _rms(x, g):
    return x * lax.rsqrt(jnp.mean(x * x, axis=-1, keepdims=True) + EPS) * g


def _proj_kernel(x_ref, g_ref, w_ref, b_ref, *out_refs, spec):
    h = _rms(x_ref[...], g_ref[...]).astype(BF16)
    misc = None
    for o_ref, (kind, c0, c1, scale) in zip(out_refs, spec):
        if kind == 'gates':
            for g in range(NSA_KV_HEADS):
                o_ref[g] = misc if g == 0 else pltpu.roll(misc, LANES - g * 3 * NSA_HPG, axis=1)
            continue
        r = jnp.dot(h, w_ref[:, c0:c1], preferred_element_type=F32)
        if scale != 1.0:
            r = r * scale
        if kind == 'f32':
            o_ref[...] = r
        elif kind == 'bf16':
            o_ref[...] = r.astype(BF16)
        elif kind == 'sigmoid':
            o_ref[...] = jax.nn.sigmoid(r)
        elif kind == 'hm':
            for i in range((c1 - c0) // NSA_HEAD_DIM):
                o_ref[i] = r[:, i * NSA_HEAD_DIM:(i + 1) * NSA_HEAD_DIM].astype(BF16)
        elif kind == 'hmt':
            rt = r.T
            for i in range((c1 - c0) // NSA_HEAD_DIM):
                o_ref[i] = rt[i * NSA_HEAD_DIM:(i + 1) * NSA_HEAD_DIM].astype(BF16)
        elif kind == 'hmvt':
            rt = r.T
            sub = lax.broadcasted_iota(jnp.int32, (VT_ROWS - NSA_HEAD_DIM, r.shape[0]), 0)
            ones = jnp.where(sub == 0, 1.0, 0.0)
            for i in range((c1 - c0) // NSA_HEAD_DIM):
                piece = rt[i * NSA_HEAD_DIM:(i + 1) * NSA_HEAD_DIM]
                o_ref[i] = jnp.concatenate([piece, ones], axis=0).astype(BF16)
        elif kind == 'misc':
            lane = lax.broadcasted_iota(jnp.int32, r.shape, 1)
            misc = jnp.where(lane < N_GATE, jax.nn.sigmoid(r + b_ref[...]), r)
            o_ref[...] = misc


def _proj(x, norm_g, w, bias, spec, tm):
    m, d = x.shape
    n = w.shape[1]
    assert m % tm == 0
    out_shape, out_specs = [], []
    for kind, c0, c1, _ in spec:
        if kind == 'hm':
            nh = (c1 - c0) // NSA_HEAD_DIM
            out_shape.append(SDS((nh, m, NSA_HEAD_DIM), BF16))
            out_specs.append(pl.BlockSpec((nh, tm, NSA_HEAD_DIM), lambda i: (0, i, 0)))
        elif kind in ('hmt', 'hmvt'):
            nh = (c1 - c0) // NSA_HEAD_DIM
            nrow = NSA_HEAD_DIM if kind == 'hmt' else VT_ROWS
            out_shape.append(SDS((nh, nrow, m), BF16))
            out_specs.append(pl.BlockSpec((nh, nrow, tm), lambda i: (0, 0, i)))
        elif kind == 'gates':
            out_shape.append(SDS((NSA_KV_HEADS, m, LANES), F32))
            out_specs.append(pl.BlockSpec((NSA_KV_HEADS, tm, LANES), lambda i: (0, i, 0)))
        else:
            out_shape.append(SDS((m, c1 - c0), BF16 if kind == 'bf16' else F32))
            out_specs.append(pl.BlockSpec((tm, c1 - c0), lambda i: (i, 0)))
    return pl.pallas_call(
        functools.partial(_proj_kernel, spec=tuple(spec)),
        grid=(m // tm,),
        in_specs=[pl.BlockSpec((tm, d), lambda i: (i, 0)), _resident((1, d)), _resident((d, n)),
                  _resident((1, LANES))],
        out_specs=out_specs, out_shape=out_shape,
        compiler_params=_params("parallel"),
    )(x, norm_g, w, bias)


def _cmp_kernel(pt_ref, k_hbm, v_hbm, pek_ref, pev_ref, w1k_ref, w1v_ref, w2k_ref, w2v_ref,
                ok_ref, ov_ref, kbuf, vbuf, tb0, tb1, hk, hv, sem, *, n_pages_step, native):
    b = pl.program_id(0)
    s = pl.program_id(1)
    ns = pl.num_programs(1)
    t = b * ns + s
    total = pl.num_programs(0) * ns
    rows = n_pages_step * 8
    dh, hid_n = NSA_HEAD_DIM, CMP_HIDDEN

    def copies(tt, slot):
        bb = tt // ns
        ss = tt % ns
        out = []
        for p in range(n_pages_step):
            page = pt_ref[bb, ss * n_pages_step + p]
            out.append(pltpu.make_async_copy(k_hbm.at[page], kbuf.at[slot, p], sem.at[slot, 0]))
            out.append(pltpu.make_async_copy(v_hbm.at[page], vbuf.at[slot, p], sem.at[slot, 1]))
        return out

    slot = t % 2

    @pl.when(t == 0)
    def _():
        for c in copies(t, slot):
            c.start()

    @pl.when(t + 1 < total)
    def _():
        for c in copies(t + 1, 1 - slot):
            c.start()

    for c in copies(t, slot):
        c.wait()

    r0 = pl.multiple_of(s * rows, rows)
    tbufs = (tb0, tb1)
    low = lax.broadcasted_iota(jnp.int32, (rows, LANES), 1) < dh
    tok_pairs = CMP_STRIDE // 2
    for buf, w1_ref, h_ref in ((kbuf, w1k_ref, hk), (vbuf, w1v_ref, hv)):
        for p in range(n_pages_step):
            for q in range(2):
                if native:
                    tile = buf[slot, p, q * LANES:(q + 1) * LANES, :].T
                else:
                    tile = buf[slot, p, :, q * LANES:(q + 1) * LANES]
                tbufs[q][p * PAGE_SIZE:(p + 1) * PAGE_SIZE, :] = tile
        for q in range(2):
            even, odd = [], []
            for m in range(tok_pairs):
                x0 = tbufs[q][pl.ds(2 * m, rows, stride=CMP_STRIDE), :]
                x1 = tbufs[q][pl.ds(2 * m + 1, rows, stride=CMP_STRIDE), :]
                even.append(jnp.where(low, x0, pltpu.roll(x1, dh, axis=1)))
                odd.append(jnp.where(low, pltpu.roll(x0, dh, axis=1), x1))
            for g, parts in ((2 * q, even), (2 * q + 1, odd)):
                z = jnp.concatenate(parts, axis=1).astype(BF16)
                h_ref[pl.ds(r0, rows), g * 2 * hid_n:(g + 1) * 2 * hid_n] = jnp.dot(
                    z, w1_ref[...], preferred_element_type=F32)

    @pl.when(s == ns - 1)
    def _():
        for h_ref, pe_ref, w1_ref, w2_ref, o_ref in ((hk, pek_ref, w1k_ref, w2k_ref, ok_ref),
                                                     (hv, pev_ref, w1v_ref, w2v_ref, ov_ref)):
            n_chunk = h_ref.shape[0]
            pe = jnp.broadcast_to(pe_ref[...], (2, 16, pe_ref.shape[2])).astype(BF16)
            pe_term = (jnp.dot(pe[0], w1_ref[:, 0:hid_n], preferred_element_type=F32)
                       + jnp.dot(pe[1], w1_ref[:, hid_n:2 * hid_n], preferred_element_type=F32))[0:1]
            hid = []
            for g in range(NSA_KV_HEADS):
                first = h_ref[:, g * 2 * hid_n:g * 2 * hid_n + hid_n]
                last = h_ref[:, g * 2 * hid_n + hid_n:(g + 1) * 2 * hid_n]
                hid.append(first + pltpu.roll(last, n_chunk - 1, axis=0) + pe_term)
            hid = jnp.concatenate(hid, axis=1)
            res = jnp.dot(jax.nn.gelu(hid).astype(BF16), w2_ref[...], preferred_element_type=F32)
            if o_ref is ok_ref:
                for g in range(NSA_KV_HEADS):
                    o_ref[0, g] = res[:, g * NSA_HEAD_DIM:(g + 1) * NSA_HEAD_DIM].astype(BF16)
            else:
                res_t = res.T
                for g in range(NSA_KV_HEADS):
                    o_ref[0, g] = res_t[g * NSA_HEAD_DIM:(g + 1) * NSA_HEAD_DIM].astype(BF16)


def _cmp_weights(pe, w1, w2):
    g = NSA_KV_HEADS
    half = CMP_STRIDE * NSA_HEAD_DIM
    pe_t = pe.reshape(2, 1, half)
    w1ab = jnp.concatenate([w1[:half], w1[half:]], axis=1)
    w2b = jnp.einsum('jd,gh->gjhd', w2, jnp.eye(g, dtype=F32)).reshape(g * CMP_HIDDEN, g * NSA_HEAD_DIM)
    return pe_t, w1ab.astype(BF16), w2b.astype(BF16)


def _compress(page_table, k_pages, v_pages, cw, native):
    bsz, n_pages = page_table.shape
    p_step = min(CMP_PAGES, n_pages)
    ns = n_pages // p_step
    n_chunk = n_pages * 8
    page_shape = k_pages.shape[1:]
    assert page_shape == ((NSA_KV, PAGE_SIZE) if native else (PAGE_SIZE, NSA_KV)) and NSA_KV == 2 * LANES
    half = CMP_STRIDE * NSA_HEAD_DIM
    gh = NSA_KV_HEADS * CMP_HIDDEN
    k_sds = SDS((bsz, NSA_KV_HEADS, n_chunk, NSA_HEAD_DIM), BF16)
    v_sds = SDS((bsz, NSA_KV_HEADS, NSA_HEAD_DIM, n_chunk), BF16)
    k_spec = pl.BlockSpec((1, NSA_KV_HEADS, n_chunk, NSA_HEAD_DIM), lambda b, s, pt: (b, 0, 0, 0))
    v_spec = pl.BlockSpec((1, NSA_KV_HEADS, NSA_HEAD_DIM, n_chunk), lambda b, s, pt: (b, 0, 0, 0))
    grid_spec = pltpu.PrefetchScalarGridSpec(
        num_scalar_prefetch=1, grid=(bsz, ns),
        in_specs=[pl.BlockSpec(memory_space=pl.ANY), pl.BlockSpec(memory_space=pl.ANY),
                  _resident((2, 1, half)), _resident((2, 1, half)),
                  _resident((half, 2 * CMP_HIDDEN)), _resident((half, 2 * CMP_HIDDEN)),
                  _resident((gh, NSA_KV)), _resident((gh, NSA_KV))],
        out_specs=[k_spec, v_spec],
        scratch_shapes=[pltpu.VMEM((2, p_step) + page_shape, F32), pltpu.VMEM((2, p_step) + page_shape, F32),
                        pltpu.VMEM((p_step * PAGE_SIZE, LANES), F32), pltpu.VMEM((p_step * PAGE_SIZE, LANES), F32),
                        pltpu.VMEM((n_chunk, 2 * gh), F32), pltpu.VMEM((n_chunk, 2 * gh), F32),
                        pltpu.SemaphoreType.DMA((2, 2))])
    return pl.pallas_call(
        functools.partial(_cmp_kernel, n_pages_step=p_step, native=native),
        grid_spec=grid_spec, out_shape=[k_sds, v_sds],
        compiler_params=_params("arbitrary", "arbitrary"),
    )(page_table, k_pages, v_pages, cw['pe_k'], cw['pe_v'], cw['w1_k'], cw['w1_v'], cw['w2_k'], cw['w2_v'])


def _softmax_rows(s, mask):
    sm = jnp.where(mask, s, NEG)
    m = jnp.max(sm, axis=-1, keepdims=True)
    e = jnp.where(mask, jnp.exp(sm - m), 0.0)
    return e, jnp.sum(e, axis=-1, keepdims=True)


def _safe_inv(l):
    return jnp.where(l > 0.0, 1.0 / jnp.where(l > 0.0, l, 1.0), 0.0)


M_INIT = -1e20
LOG2_E = 1.4426950408889634


def _nsa_prompt_kernel(qt_ref, kc_ref, vct_ref, ks_ref, vst_ref, kw_ref, vwt_ref, gate_ref,
                       o_ref, *scratch, n_cmp, n_slc, n_sel):
    tq, tk, hpg, dh = NSA_TQ, NSA_TK, NSA_HPG, NSA_HEAD_DIM
    rk_refs = scratch[:tq // LANES]
    selb_ref, m_ref, acc_ref = scratch[tq // LANES:tq // LANES + 3]
    s_refs, p_refs, a_refs = (scratch[tq // LANES + 3 + 2 * i:tq // LANES + 5 + 2 * i] for i in range(3))
    qi = pl.program_id(2)
    q0 = qi * tq
    n_chunk = kc_ref.shape[2]
    r = SLC_BLOCK // CMP_STRIDE
    n_row = n_chunk // r

    ci = lax.broadcasted_iota(jnp.int32, (n_chunk, tq), 0)
    pos_c = q0 + lax.broadcasted_iota(jnp.int32, (n_chunk, tq), 1)
    m_c = (ci < n_cmp) & (ci * CMP_STRIDE + (CMP_BLOCK - 1) <= pos_c)
    kc = kc_ref[0, 0]
    vct = vct_ref[0, 0]
    o_c = []
    psum = None
    for h in range(hpg):
        s = jnp.where(m_c, jnp.dot(kc, qt_ref[0, h], preferred_element_type=F32), NEG)
        e = jnp.where(m_c, jnp.exp2(s - jnp.max(s, axis=0, keepdims=True)), 0.0)
        p = e * _safe_inv(jnp.sum(e, axis=0, keepdims=True))
        o_c.append(jnp.dot(vct, p.astype(BF16), preferred_element_type=F32))
        psum = p if psum is None else psum + p

    imp = pltpu.roll(psum, 1, axis=0) + psum
    for o in range(1, r):
        imp = imp + pltpu.roll(psum, n_chunk - o, axis=0)
    parts = []
    for i, rk_ref in enumerate(rk_refs):
        rk_ref[...] = imp[:, i * LANES:(i + 1) * LANES]
        parts.append(rk_ref[pl.ds(0, n_row, stride=r), :])
    imp_b = jnp.concatenate(parts, axis=1)
    j_io = lax.broadcasted_iota(jnp.int32, (n_row, tq), 0)
    pos_b = q0 + lax.broadcasted_iota(jnp.int32, (n_row, tq), 1)
    valid = (j_io < n_slc) & (j_io * SLC_BLOCK <= pos_b)
    forced = (j_io == 0) | (j_io == pos_b // SLC_BLOCK)
    sc = jnp.where(valid, imp_b + jnp.where(forced, FORCE, 0.0), NEG)
    rank = jnp.zeros((n_row, tq), F32)
    for k in range(n_slc):
        ck = sc[k:k + 1, :]
        beats = (ck > sc) | ((ck == sc) & (j_io > k))
        rank = rank + jnp.where(beats, 1.0, 0.0)
    selb_ref[...] = jnp.where(rank < n_sel, 0.0, NEG)

    dlt = lax.broadcasted_iota(jnp.int32, (tk, tq), 0) - lax.broadcasted_iota(jnp.int32, (tk, tq), 1)
    blocks_per_tile = tk // SLC_BLOCK

    def flash(k_ref, vt_ref, lo, hi, bias_fn):
        n_tiles = k_ref.shape[2] // tk
        m_ref[...] = jnp.full(m_ref.shape, M_INIT, F32)
        acc_ref[...] = jnp.zeros(acc_ref.shape, F32)
        p_refs[1][...] = jnp.zeros(p_refs[1].shape, BF16)
        a_refs[1][...] = jnp.ones(a_refs[1].shape, F32)

        def tile_start(kb):
            return pl.multiple_of(jnp.clip(kb, 0, n_tiles - 1) * tk, tk)

        def scores(kb, s_ref):
            kk = k_ref[0, 0, pl.ds(tile_start(kb), tk), :]
            bias = bias_fn(kb)
            for h in range(hpg):
                s_ref[h] = jnp.dot(kk, qt_ref[0, h], preferred_element_type=F32) + bias

        def softmax(s_ref, p_ref, a_ref):
            for h in range(hpg):
                s = s_ref[h]
                m_prev = m_ref[h]
                m_new = jnp.maximum(m_prev, jnp.max(s, axis=0, keepdims=True))
                p_ref[h] = jnp.exp2(s - m_new).astype(BF16)
                a_ref[h] = jnp.exp2(m_prev - m_new)
                m_ref[h] = m_new

        def weighted_values(kb, p_ref, a_ref):
            vt = vt_ref[0, :, pl.ds(tile_start(kb), tk)]
            for h in range(hpg):
                acc_ref[h] = a_ref[h] * acc_ref[h] + jnp.dot(vt, p_ref[h], preferred_element_type=F32)

        scores(lo, s_refs[0])

        def body(j, carry):
            i = lo + 2 * j
            scores(i + 1, s_refs[1])
            softmax(s_refs[0], p_refs[0], a_refs[0])
            weighted_values(i - 1, p_refs[1], a_refs[1])
            scores(i + 2, s_refs[0])
            softmax(s_refs[1], p_refs[1], a_refs[1])
            weighted_values(i, p_refs[0], a_refs[0])
            return carry

        n_pairs = (hi - lo + 1) // 2
        lax.fori_loop(0, n_pairs, body, 0)
        weighted_values(lo + 2 * n_pairs - 1, p_refs[1], a_refs[1])
        return [acc_ref[h, 0:dh, :] * _safe_inv(acc_ref[h, dh:dh + 1, :]) for h in range(hpg)]

    def slc_bias(kb):
        first = jnp.minimum(kb, n_slc // blocks_per_tile - 1) * blocks_per_tile
        rows = [jnp.broadcast_to(selb_ref[pl.ds(first + i, 1), :], (SLC_BLOCK, tq))
                for i in range(blocks_per_tile)]
        return jnp.where(dlt + (kb * tk - q0) <= 0, jnp.concatenate(rows, axis=0), NEG)

    def win_bias(kb):
        d = dlt + (kb * tk - q0)
        return jnp.where((d <= 0) & (d >= -WINDOW), 0.0, NEG)

    hi = (q0 + tq - 1) // tk + 1
    o_s = flash(ks_ref, vst_ref, 0, hi, slc_bias)
    o_w = flash(kw_ref, vwt_ref, jnp.maximum(q0 - WINDOW, 0) // tk, hi, win_bias)

    gt = gate_ref[0].T
    outs = [gt[h:h + 1] * o_c[h] + gt[hpg + h:hpg + h + 1] * o_s[h] + gt[2 * hpg + h:2 * hpg + h + 1] * o_w[h]
            for h in range(hpg)]
    o_ref[...] = jnp.concatenate(outs, axis=0).T.astype(BF16)


def _nsa_prompt(q_t, ks_hm, vs_t, kw_hm, vw_t, kcmp, vcmp_t, gates_hm, bsz, t):
    g, hpg, dh, tq = NSA_KV_HEADS, NSA_HPG, NSA_HEAD_DIM, NSA_TQ
    r = SLC_BLOCK // CMP_STRIDE
    n_slc = t // SLC_BLOCK
    assert t % NSA_TK == 0 and t % tq == 0 and n_slc * r <= LANES and NSA_TK % SLC_BLOCK == 0
    nq = t // tq
    n_cmp = kcmp.shape[2] - 1
    if kcmp.shape[2] < LANES:
        fill = LANES - kcmp.shape[2]
        kcmp = jnp.pad(kcmp, ((0, 0), (0, 0), (0, fill), (0, 0)))
        vcmp_t = jnp.pad(vcmp_t, ((0, 0), (0, 0), (0, 0), (0, fill)))
    n_chunk = kcmp.shape[2]
    assert n_chunk == LANES
    q4 = q_t.reshape(g, hpg, dh, bsz * t)

    def k_spec():
        return pl.BlockSpec((1, 1, t, dh), lambda b, gg, qi: (gg, b, 0, 0))

    def vt_spec():
        return pl.BlockSpec((1, VT_ROWS, t), lambda b, gg, qi: (gg, 0, b))

    def per_bt(a):
        return a.reshape(g, bsz, t, a.shape[-1])

    return pl.pallas_call(
        functools.partial(_nsa_prompt_kernel, n_cmp=n_cmp, n_slc=n_slc, n_sel=min(N_SELECT, n_slc)),
        grid=(bsz, g, nq),
        in_specs=[pl.BlockSpec((1, hpg, dh, tq), lambda b, gg, qi: (gg, 0, 0, b * nq + qi)),
                  pl.BlockSpec((1, 1, n_chunk, dh), lambda b, gg, qi: (b, gg, 0, 0)),
                  pl.BlockSpec((1, 1, dh, n_chunk), lambda b, gg, qi: (b, gg, 0, 0)),
                  k_spec(), vt_spec(), k_spec(), vt_spec(),
                  pl.BlockSpec((1, tq, LANES), lambda b, gg, qi: (gg, b * nq + qi, 0))],
        out_specs=pl.BlockSpec((tq, hpg * dh), lambda b, gg, qi: (b * nq + qi, gg)),
        out_shape=SDS((bsz * t, NSA_Q), BF16),
        scratch_shapes=[pltpu.VMEM((n_chunk, LANES), F32)] * (tq // LANES) + [
                        pltpu.VMEM((n_chunk // r, tq), F32),
                        pltpu.VMEM((hpg, 1, tq), F32), pltpu.VMEM((hpg, VT_ROWS, tq), F32),
                        pltpu.VMEM((hpg, NSA_TK, tq), F32), pltpu.VMEM((hpg, NSA_TK, tq), F32),
                        pltpu.VMEM((hpg, NSA_TK, tq), BF16), pltpu.VMEM((hpg, NSA_TK, tq), BF16),
                        pltpu.VMEM((hpg, 1, tq), F32), pltpu.VMEM((hpg, 1, tq), F32)],
        compiler_params=_params("parallel", "parallel", "arbitrary"),
    )(q4, kcmp, vcmp_t, per_bt(ks_hm), vs_t, per_bt(kw_hm), vw_t, gates_hm)


def _gla_prompt_kernel(q_ref, k_ref, v_ref, r_ref, misc_ref, wa_ref, ba_ref, ng_ref,
                       o_ref, s_ref, qe_ref, qt_ref, kt_ref, kh_ref, d_ref, u_ref, st_ref):
    t = q_ref.shape[0]
    c = GLA_CHUNK
    n = t // c
    dk, dv = GLA_DK, GLA_DV
    lr = misc_ref[:, N_GATE:N_GATE + GLA_RANK].astype(BF16)
    x = jnp.dot(lr, wa_ref[...], preferred_element_type=F32) + ba_ref[...]
    g = jax.nn.log_sigmoid(x) / GLA_TAU
    row = lax.broadcasted_iota(jnp.int32, (t, dk), 0) % c
    b = g
    sh = 1
    while sh < c:
        b = b + jnp.where(row >= sh, pltpu.roll(b, sh, axis=0), 0.0)
        sh *= 2
    b3 = b.reshape(n, c, dk)
    b_last = jnp.broadcast_to(b3[:, c - 1:c, :], (n, c, dk)).reshape(t, dk)
    b_mid = jnp.broadcast_to(b3[:, c // 2 - 1:c // 2, :], (n, c, dk)).reshape(t, dk)
    q = q_ref[...] * (GLA_DK ** -0.5)
    k = k_ref[...]
    qe_ref[...] = (q * jnp.exp(b)).astype(BF16)
    qt_ref[...] = (q * jnp.exp(b - b_mid)).astype(BF16)
    kt_ref[...] = (k * jnp.exp(b_mid - b)).astype(BF16)
    kh_ref[...] = (k * jnp.exp(b_last - b)).astype(BF16)
    d_ref[...] = jnp.exp(b_last)
    tn = (((0,), (0,)), ((), ()))
    nt = (((1,), (1,)), ((), ()))

    def chunk_update(i, carry):
        r0 = pl.multiple_of(i * c, c)
        u_ref[i] = lax.dot_general(v_ref[pl.ds(r0, c), :], kh_ref[pl.ds(r0, c), :], tn,
                                   preferred_element_type=F32)
        return carry

    lax.fori_loop(0, n, chunk_update, 0, unroll=GLA_UNROLL)

    st_ref[...] = jnp.zeros((dv, dk), F32)

    def recur(i, carry):
        st = st_ref[...]
        st_ref[...] = st * d_ref[pl.ds(pl.multiple_of(i * c, c), 1), :] + u_ref[i]
        u_ref[i] = st
        return carry

    lax.fori_loop(0, n, recur, 0)
    s_ref[0, 0] = st_ref[...].T

    tril = lax.broadcasted_iota(jnp.int32, (c, c), 0) >= lax.broadcasted_iota(jnp.int32, (c, c), 1)
    ng = ng_ref[...]

    def chunk_out(i, carry):
        r0 = pl.multiple_of(i * c, c)
        vv = v_ref[pl.ds(r0, c), :]
        o = lax.dot_general(qe_ref[pl.ds(r0, c), :], u_ref[i].astype(BF16), nt, preferred_element_type=F32)
        a = lax.dot_general(qt_ref[pl.ds(r0, c), :], kt_ref[pl.ds(r0, c), :], nt, preferred_element_type=F32)
        a = jnp.where(tril, a, 0.0).astype(BF16)
        o = o + jnp.dot(a, vv, preferred_element_type=F32)
        o = _rms(o, ng) * jax.nn.silu(r_ref[pl.ds(r0, c), :])
        o_ref[pl.ds(r0, c), :] = o.astype(BF16)
        return carry

    lax.fori_loop(0, n, chunk_out, 0, unroll=GLA_UNROLL)


def _gla_prompt(q_l, k_l, v_l, r_l, misc, wa, ba, ng, bsz, t):
    h, dk, dv = GLA_HEADS, GLA_DK, GLA_DV
    assert t % GLA_CHUNK == 0
    n = t // GLA_CHUNK
    return pl.pallas_call(
        _gla_prompt_kernel,
        grid=(bsz, h),
        in_specs=[pl.BlockSpec((t, dk), lambda b, hh: (b, hh)), pl.BlockSpec((t, dk), lambda b, hh: (b, hh)),
                  pl.BlockSpec((t, dv), lambda b, hh: (b, hh)), pl.BlockSpec((t, dv), lambda b, hh: (b, hh)),
                  pl.BlockSpec((t, LANES), lambda b, hh: (b, 0)),
                  pl.BlockSpec((GLA_RANK, dk), lambda b, hh: (0, hh)), pl.BlockSpec((1, dk), lambda b, hh: (0, hh)),
                  _resident((1, dv))],
        out_specs=[pl.BlockSpec((t, dv), lambda b, hh: (b, hh)),
                   pl.BlockSpec((1, 1, dk, dv), lambda b, hh: (b, hh, 0, 0))],
        out_shape=[SDS((bsz * t, h * dv), BF16), SDS((bsz, h, dk, dv), F32)],
        scratch_shapes=[pltpu.VMEM((t, dk), BF16), pltpu.VMEM((t, dk), BF16), pltpu.VMEM((t, dk), BF16),
                        pltpu.VMEM((t, dk), BF16), pltpu.VMEM((t, dk), F32),
                        pltpu.VMEM((n, dv, dk), F32), pltpu.VMEM((dv, dk), F32)],
        compiler_params=_params("parallel", "parallel"),
    )(q_l, k_l, v_l, r_l, misc, wa, ba, ng)


def _merge_kernel(oa_ref, ob_ref, x_ref, g_ref, wg_ref, wa_ref, wb_ref, wo_ref, x1_ref):
    x = x_ref[...]
    h = _rms(x, g_ref[...]).astype(BF16)
    d = D_MODEL
    ga = jax.nn.sigmoid(jnp.dot(h, wg_ref[:, 0:d], preferred_element_type=F32))
    u = ga * jnp.dot(oa_ref[...], wa_ref[...], preferred_element_type=F32)
    gb = jax.nn.sigmoid(jnp.dot(h, wg_ref[:, d:2 * d], preferred_element_type=F32))
    u = u + gb * jnp.dot(ob_ref[...], wb_ref[...], preferred_element_type=F32)
    x1_ref[...] = x + jnp.dot(u.astype(BF16), wo_ref[...], preferred_element_type=F32)


def _merge(o_a, o_b, x, norm_g, w_gate, w_a, w_b, w_o, tm):
    m, d = x.shape
    assert m % tm == 0
    return pl.pallas_call(
        _merge_kernel, grid=(m // tm,),
        in_specs=[pl.BlockSpec((tm, NSA_Q), lambda i: (i, 0)), pl.BlockSpec((tm, GLA_HEADS * GLA_DV), lambda i: (i, 0)),
                  pl.BlockSpec((tm, d), lambda i: (i, 0)), _resident((1, d)), _resident((d, 2 * d)),
                  _resident(w_a.shape), _resident(w_b.shape), _resident(w_o.shape)],
        out_specs=pl.BlockSpec((tm, d), lambda i: (i, 0)),
        out_shape=SDS((m, d), F32),
        compiler_params=_params("parallel"),
    )(o_a, o_b, x, norm_g, w_gate, w_a, w_b, w_o)


def _mlp_kernel(x1_ref, g2_ref, wu_ref, wd_ref, gf_ref, y_ref, h_ref):
    j = pl.program_id(1)

    @pl.when(j == 0)
    def _():
        x1 = x1_ref[...]
        h_ref[...] = _rms(x1, g2_ref[...]).astype(BF16)
        y_ref[...] = x1

    up = jnp.maximum(jnp.dot(h_ref[...], wu_ref[...], preferred_element_type=F32), 0.0)
    y_ref[...] += jnp.dot((up * up).astype(BF16), wd_ref[...], preferred_element_type=F32)

    @pl.when(j == pl.num_programs(1) - 1)
    def _():
        y_ref[...] = _rms(y_ref[...], gf_ref[...])


def _mlp(x1, g2, w_up, w_down, gf, tm, tf):
    m, d = x1.shape
    ff = w_up.shape[1]
    assert m % tm == 0 and ff % tf == 0
    return pl.pallas_call(
        _mlp_kernel, grid=(m // tm, ff // tf),
        in_specs=[pl.BlockSpec((tm, d), lambda i, j: (i, 0)), _resident((1, d)),
                  pl.BlockSpec((d, tf), lambda i, j: (0, j)), pl.BlockSpec((tf, d), lambda i, j: (j, 0)),
                  _resident((1, d))],
        out_specs=pl.BlockSpec((tm, d), lambda i, j: (i, 0)),
        out_shape=SDS((m, d), F32),
        scratch_shapes=[pltpu.VMEM((tm, d), BF16)],
        compiler_params=_params("parallel", "arbitrary"),
    )(x1, g2, w_up, w_down, gf)


def _prep_weights(norm1_g, w_in, b_nsa_gate, cmp_pe_k, cmp_pe_v, cmp_k_w1, cmp_k_w2, cmp_v_w1, cmp_v_w2,
                  gla_w_a2, gla_b_a, gla_norm_g, w_br_a, w_br_b, w_o, norm2_g, w_up, w_down, norm_f):
    pts = [0]
    for s in SPLITS:
        pts.append(pts[-1] + s)
    c_q, c_kv, c_g, c_ql, c_kl, c_vl, c_rl, c_lr, c_br, c_end = pts
    gcols = jnp.asarray([c_g + (g * NSA_HPG + h) * 3 + c for g in range(NSA_KV_HEADS)
                         for c in range(3) for h in range(NSA_HPG)], jnp.int32)
    pad = jnp.zeros((D_MODEL, LANES - N_GATE - GLA_RANK), F32)
    w_misc = jnp.concatenate([w_in[:, gcols], w_in[:, c_lr:c_br], pad], axis=1)
    b_misc = jnp.concatenate([b_nsa_gate[gcols - c_g], jnp.zeros((LANES - N_GATE,), F32)])[None, :]
    w = dict(
        norm1=norm1_g[None, :], norm2=norm2_g[None, :], norm_f=norm_f[None, :],
        w_a=w_in[:, c_q:c_g].astype(BF16),
        w_b=jnp.concatenate([w_in[:, c_ql:c_lr], w_misc], axis=1).astype(BF16),
        w_gate=w_in[:, c_br:c_end].astype(BF16),
        b_misc=b_misc,
        gla_wa=gla_w_a2.astype(BF16), gla_ba=gla_b_a[None, :], gla_ng=gla_norm_g[None, :],
        w_br_a=w_br_a.astype(BF16), w_br_b=w_br_b.astype(BF16), w_o=w_o.astype(BF16),
        w_up=w_up.astype(BF16), w_down=w_down.astype(BF16),
    )
    cw = {}
    cw['pe_k'], cw['w1_k'], cw['w2_k'] = _cmp_weights(cmp_pe_k, cmp_k_w1, cmp_k_w2)
    cw['pe_v'], cw['w1_v'], cw['w2_v'] = _cmp_weights(cmp_pe_v, cmp_v_w1, cmp_v_w2)
    w['cmp'] = cw
    return w


_KV6 = 6 * NSA_KV
_GQ = GLA_HEADS * GLA_DK
_GV = GLA_HEADS * GLA_DV


def _layer_prompt(x, w):
    bsz, t, d = x.shape
    m = bsz * t
    x2 = x.reshape(m, d)
    zb = jnp.zeros((1, LANES), F32)
    spec_a = [('hmt', 0, NSA_Q, NSA_HEAD_DIM ** -0.5 * LOG2_E)]
    spec_a += [('f32', NSA_Q + j * NSA_KV, NSA_Q + (j + 1) * NSA_KV, 1.0) for j in range(6)]
    spec_a += [('hm' if j % 2 == 0 else 'hmvt', NSA_Q + j * NSA_KV, NSA_Q + (j + 1) * NSA_KV, 1.0)
               for j in range(2, 6)]
    q_t, r0, r1, r2, r3, r4, r5, ks_hm, vs_t, kw_hm, vw_t = _proj(x2, w['norm1'], w['w_a'], zb, spec_a, PROJ_TM)
    spec_b = [('f32', 0, _GQ, 1.0), ('f32', _GQ, 2 * _GQ, 1.0), ('bf16', 2 * _GQ, 2 * _GQ + _GV, 1.0),
              ('f32', 2 * _GQ + _GV, 2 * _GQ + 2 * _GV, 1.0),
              ('misc', 2 * _GQ + 2 * _GV, 2 * _GQ + 2 * _GV + LANES, 1.0), ('gates', 0, 0, 1.0)]
    q_l, k_l, v_l, r_l, misc, gates_hm = _proj(x2, w['norm1'], w['w_b'], w['b_misc'], spec_b, PROJ_TM)

    n_pages = t // PAGE_SIZE
    ident = jnp.arange(bsz * n_pages, dtype=jnp.int32).reshape(bsz, n_pages)
    pages = (bsz * n_pages, PAGE_SIZE, NSA_KV)
    kcmp, vcmp = _compress(ident, r0.reshape(pages), r1.reshape(pages), w['cmp'], native=False)
    o_a = _nsa_prompt(q_t, ks_hm, vs_t, kw_hm, vw_t, kcmp, vcmp, gates_hm, bsz, t)
    o_b, s_new = _gla_prompt(q_l, k_l, v_l, r_l, misc, w['gla_wa'], w['gla_ba'], w['gla_ng'], bsz, t)
    x1 = _merge(o_a, o_b, x2, w['norm1'], w['w_gate'], w['w_br_a'], w['w_br_b'], w['w_o'], MERGE_TM)
    y = _mlp(x1, w['norm2'], w['w_up'], w['w_down'], w['norm_f'], MLP_TM if m % MLP_TM == 0 else PROJ_TM, MLP_TF)
    kvh = (bsz, t, NSA_KV_HEADS, NSA_HEAD_DIM)
    rows = [a.reshape(kvh) for a in (r0, r1, r2, r3)]
    n_keep = min(WINDOW, t)
    wins = [a.reshape(kvh)[:, t - n_keep:] for a in (r4, r5)]
    return y.reshape(bsz, t, d), rows, wins, s_new


def _group_rows(parts):
    rowg = lax.broadcasted_iota(jnp.int32, parts[0].shape, 0) // NSA_HPG
    out = parts[0]
    for g in range(1, NSA_KV_HEADS):
        out = jnp.where(rowg == g, parts[g], out)
    return out


def _sample_select_kernel(q_ref, kc_ref, vc_ref, oc_ref, idx_ref, *, n_cmp, pos, n_pick):
    q = q_ref[0]
    nt = (((1,), (1,)), ((), ()))
    n_chunk = kc_ref.shape[2]
    s = _group_rows([lax.dot_general(q, kc_ref[0, g], nt, preferred_element_type=F32)
                     for g in range(NSA_KV_HEADS)])
    lane = lax.broadcasted_iota(jnp.int32, s.shape, 1)
    mask = (lane < n_cmp) & (lane * CMP_STRIDE + (CMP_BLOCK - 1) <= pos)
    e, l = _softmax_rows(s, mask)
    p = e * _safe_inv(l)
    pb = p.astype(BF16)
    oc_ref[0] = _group_rows([lax.dot_general(pb, vc_ref[0, g], nt, preferred_element_type=F32)
                             for g in range(NSA_KV_HEADS)])
    nr = p.shape[0]
    y = p + pltpu.roll(p, nr - 1, axis=0)
    psum = y + pltpu.roll(y, nr - 2, axis=0)
    imp = pltpu.roll(psum, 1, axis=1) + psum
    r = SLC_BLOCK // CMP_STRIDE
    for o in range(1, r):
        imp = imp + pltpu.roll(psum, n_chunk - o, axis=1)
    blk = lane // r
    is_blk = lane % r == 0
    valid = blk * SLC_BLOCK <= pos
    forced = (blk == 0) | (blk == pos // SLC_BLOCK)
    score = jnp.where(is_blk & valid, imp + jnp.where(forced, FORCE, 0.0), NEG)
    lane_f = lane.astype(F32)
    out_lane = lax.broadcasted_iota(jnp.int32, (nr, LANES), 1)
    picked = jnp.zeros((nr, LANES), F32)
    for k in range(n_pick):
        mx = jnp.max(score, axis=-1, keepdims=True)
        ix = jnp.min(jnp.where(score == mx, lane_f, float(n_chunk)), axis=-1, keepdims=True)
        picked = jnp.where(out_lane == k, ix, picked)
        score = jnp.where(lane_f == ix, 2.0 * NEG, score)
    idx_ref[0] = picked.astype(jnp.int32) // r


def _sample_select(q_s, kcmp, vcmp, pos, n_pick):
    bsz, g, n_chunk, dh = kcmp.shape
    nh = NSA_HEADS
    cmp_spec = pl.BlockSpec((1, g, n_chunk, dh), lambda b: (b, 0, 0, 0))
    cmpt_spec = pl.BlockSpec((1, g, dh, n_chunk), lambda b: (b, 0, 0, 0))
    return pl.pallas_call(
        functools.partial(_sample_select_kernel, n_cmp=n_chunk - 1, pos=pos, n_pick=n_pick),
        grid=(bsz,),
        in_specs=[pl.BlockSpec((1, nh, dh), lambda b: (b, 0, 0)), cmp_spec, cmpt_spec],
        out_specs=[pl.BlockSpec((1, nh, dh), lambda b: (b, 0, 0)), pl.BlockSpec((1, nh, LANES), lambda b: (b, 0, 0))],
        out_shape=[SDS((bsz, nh, dh), F32), SDS((bsz, nh, LANES), jnp.int32)],
        compiler_params=_params("parallel"),
    )(q_s, kcmp, vcmp)


def _sample_attend_kernel(idx_ref, pt_ref, q_ref, oc_ref, ks_hbm, vs_hbm, kw_ref, vw_ref,
                          nks_ref, nvs_ref, nkw_ref, nvw_ref, gate_ref, o_ref, ksel, vsel, sem, *, n_pick):
    b = pl.program_id(0)
    g_n, dh = NSA_KV_HEADS, NSA_HEAD_DIM
    half = PAGE_SIZE // SLC_BLOCK

    def copies():
        out = []
        for g in range(g_n):
            for r in range(n_pick):
                page = pt_ref[b, idx_ref[b * g_n + g, r] // half]
                out.append(pltpu.make_async_copy(ks_hbm.at[page, g], ksel.at[g, r], sem.at[0]))
                out.append(pltpu.make_async_copy(vs_hbm.at[page, g], vsel.at[g, r], sem.at[1]))
        return out

    for c in copies():
        c.start()

    q = q_ref[0]
    qf = q.astype(F32)
    nt = (((1,), (1,)), ((), ()))

    def attend(keys_t, vals_t, bias, k_new, v_new):
        s = _group_rows([jnp.dot(q, keys_t[g].astype(BF16), preferred_element_type=F32)
                         + (0.0 if bias is None else bias[g]) for g in range(g_n)])
        s_new = _group_rows([jnp.sum(qf * k_new[:, g * dh:(g + 1) * dh], axis=-1, keepdims=True)
                             for g in range(g_n)])
        m = jnp.maximum(jnp.max(s, axis=-1, keepdims=True), s_new)
        e = jnp.exp(s - m)
        e_new = jnp.exp(s_new - m)
        l = jnp.sum(e, axis=-1, keepdims=True) + e_new
        eb = e.astype(BF16)
        acc = _group_rows([lax.dot_general(eb, vals_t[g].astype(BF16), nt, preferred_element_type=F32)
                           + e_new * v_new[:, g * dh:(g + 1) * dh] for g in range(g_n)])
        return acc / l

    o_w = attend([kw_ref[0, g] for g in range(g_n)], [vw_ref[0, g] for g in range(g_n)], None,
                 nkw_ref[0], nvw_ref[0])
    for c in copies():
        c.wait()
    lin = lax.broadcasted_iota(jnp.int32, (1, PAGE_SIZE), 1)
    bias = []
    for g in range(g_n):
        parts = []
        for r in range(n_pick):
            off = (idx_ref[b * g_n + g, r] % half) * SLC_BLOCK
            parts.append(jnp.where((lin >= off) & (lin < off + SLC_BLOCK), 0.0, NEG))
        bias.append(jnp.concatenate(parts, axis=1))

    def tiles(buf, g):
        return jnp.concatenate([buf[g, r] for r in range(n_pick)], axis=1)

    o_s = attend([tiles(ksel, g) for g in range(g_n)], [tiles(vsel, g) for g in range(g_n)], bias,
                 nks_ref[0], nvs_ref[0])

    gt = jnp.broadcast_to(gate_ref[0], (LANES, LANES)).T
    nh = NSA_HEADS
    o_ref[0] = (gt[0:nh, 0:dh] * oc_ref[0] + gt[nh:2 * nh, 0:dh] * o_s + gt[2 * nh:3 * nh, 0:dh] * o_w)


def _sample_attend(idx, page_table, q_s, o_c, slc_k, slc_v, win_k, win_v, new_rows, gates, n_pick):
    bsz = page_table.shape[0]
    nh, dh, g = NSA_HEADS, NSA_HEAD_DIM, NSA_KV_HEADS
    wlen = win_k.shape[-1]
    row_spec = pl.BlockSpec((1, 1, NSA_KV), lambda b, *_: (b, 0, 0))
    win_spec = pl.BlockSpec((1, g, dh, wlen), lambda b, *_: (b, 0, 0, 0))
    head_spec = pl.BlockSpec((1, nh, dh), lambda b, *_: (b, 0, 0))
    any_spec = pl.BlockSpec(memory_space=pl.ANY)
    grid_spec = pltpu.PrefetchScalarGridSpec(
        num_scalar_prefetch=2, grid=(bsz,),
        in_specs=[head_spec, head_spec, any_spec, any_spec, win_spec, win_spec,
                  row_spec, row_spec, row_spec, row_spec, pl.BlockSpec((1, 1, LANES), lambda b, *_: (b, 0, 0))],
        out_specs=head_spec,
        scratch_shapes=[pltpu.VMEM((g, n_pick, dh, PAGE_SIZE), F32), pltpu.VMEM((g, n_pick, dh, PAGE_SIZE), F32),
                        pltpu.SemaphoreType.DMA((2,))])
    return pl.pallas_call(
        functools.partial(_sample_attend_kernel, n_pick=n_pick),
        grid_spec=grid_spec, out_shape=SDS((bsz, nh, dh), F32),
        compiler_params=_params("arbitrary"),
    )(idx, page_table, q_s, o_c, slc_k, slc_v, win_k, win_v, *new_rows, gates)


def _gla_sample_kernel(q_ref, k_ref, v_ref, r_ref, misc_ref, wa_ref, ba_ref, ng_ref, s0_ref, o_ref, s_ref):
    dk, dv = GLA_DK, GLA_DV
    lr = jnp.broadcast_to(misc_ref[0][:, N_GATE:N_GATE + GLA_RANK], (16, GLA_RANK)).astype(BF16)
    x = jnp.dot(lr, wa_ref[...], preferred_element_type=F32)[0:1] + ba_ref[...]
    g_all = jax.nn.log_sigmoid(x) / GLA_TAU

    def col(v):
        t = jnp.broadcast_to(v, (dk, dk)).T
        return jnp.concatenate([t] * (dv // dk), axis=1)

    outs = []
    for h in range(GLA_HEADS):
        g = g_all[:, h * dk:(h + 1) * dk]
        q = q_ref[0][:, h * dk:(h + 1) * dk] * (GLA_DK ** -0.5)
        k = k_ref[0][:, h * dk:(h + 1) * dk]
        v = v_ref[0][:, h * dv:(h + 1) * dv]
        s0 = s0_ref[0, h]
        q_t = q * jnp.exp(g)
        k_t = k * jnp.exp(-g)
        a = jnp.sum(q_t * k_t, axis=-1, keepdims=True)
        o = jnp.sum(col(q_t) * s0, axis=0, keepdims=True) + a * v
        s_ref[0, h] = col(jnp.exp(g)) * s0 + col(k) * v
        outs.append(_rms(o, ng_ref[...]) * jax.nn.silu(r_ref[0][:, h * dv:(h + 1) * dv]))
    o_ref[0] = jnp.concatenate(outs, axis=-1)


def _gla_sample(q_l, k_l, v_l, r_l, misc, wa, ba, ng, s0):
    bsz, h, dk, dv = s0.shape

    def row(n):
        return pl.BlockSpec((1, 1, n), lambda b: (b, 0, 0))

    st_spec = pl.BlockSpec((1, h, dk, dv), lambda b: (b, 0, 0, 0))
    return pl.pallas_call(
        _gla_sample_kernel, grid=(bsz,),
        in_specs=[row(h * dk), row(h * dk), row(h * dv), row(h * dv), row(LANES),
                  _resident(wa.shape), _resident(ba.shape), _resident(ng.shape), st_spec],
        out_specs=[row(h * dv), st_spec],
        out_shape=[SDS((bsz, 1, h * dv), F32), SDS((bsz, h, dk, dv), F32)],
        compiler_params=_params("parallel"),
    )(q_l, k_l, v_l, r_l, misc, wa, ba, ng, s0)


SAMPLE_ROWS = 16


def _layer_sample(x, caches, wins, s0, page_table, w):
    bsz, t, d = x.shape
    n_pages = page_table.shape[1]
    pos = n_pages * PAGE_SIZE
    assert t == 1 and bsz <= SAMPLE_ROWS and pos % SLC_BLOCK == 0 and wins[0].shape[1] == WINDOW
    assert pos // SLC_BLOCK >= N_SELECT
    mp = SAMPLE_ROWS
    x2 = jnp.pad(x.reshape(bsz, d), ((0, mp - bsz), (0, 0)))
    zb = jnp.zeros((1, LANES), F32)
    spec_a = [('bf16', 0, NSA_Q, NSA_HEAD_DIM ** -0.5)]
    spec_a += [('f32', NSA_Q + j * NSA_KV, NSA_Q + (j + 1) * NSA_KV, 1.0) for j in range(6)]
    q_s, r0, r1, r2, r3, r4, r5 = _proj(x2, w['norm1'], w['w_a'], zb, spec_a, mp)
    spec_b = [('f32', 0, _GQ, 1.0), ('f32', _GQ, 2 * _GQ, 1.0), ('f32', 2 * _GQ, 2 * _GQ + _GV, 1.0),
              ('f32', 2 * _GQ + _GV, 2 * _GQ + 2 * _GV, 1.0),
              ('misc', 2 * _GQ + 2 * _GV, 2 * _GQ + 2 * _GV + LANES, 1.0)]
    q_l, k_l, v_l, r_l, misc = _proj(x2, w['norm1'], w['w_b'], w['b_misc'], spec_b, mp)

    cache_t = [a.transpose(0, 2, 3, 1) for a in caches]
    win_t = [a.transpose(0, 2, 3, 1) for a in wins]
    n_pool = caches[0].shape[0]
    cmp_pages = [a.reshape(n_pool, NSA_KV, PAGE_SIZE) for a in cache_t[:2]]
    kcmp, vcmp = _compress(page_table, cmp_pages[0], cmp_pages[1], w['cmp'], native=True)
    q_h = q_s.reshape(mp, NSA_HEADS, NSA_HEAD_DIM)
    n_pick = N_SELECT - 1
    o_c, idx = _sample_select(q_h, kcmp, vcmp, pos, n_pick)
    idx = idx[:, ::NSA_HPG, :n_pick].reshape(bsz * NSA_KV_HEADS, n_pick)
    gates = misc[:, :N_GATE].reshape(mp, NSA_KV_HEADS, 3, NSA_HPG).transpose(0, 2, 1, 3).reshape(mp, 1, N_GATE)
    gates = jnp.pad(gates, ((0, 0), (0, 0), (0, LANES - N_GATE)))
    new_rows = [a.reshape(mp, 1, NSA_KV) for a in (r2, r3, r4, r5)]
    o_a = _sample_attend(idx, page_table, q_h, o_c, cache_t[2], cache_t[3], win_t[0], win_t[1], new_rows, gates,
                         n_pick)
    o_a = jnp.pad(o_a.reshape(bsz, NSA_Q), ((0, mp - bsz), (0, 0))).astype(BF16)

    def r3d(a):
        return a.reshape(mp, 1, a.shape[-1])

    o_b, s_new = _gla_sample(r3d(q_l), r3d(k_l), r3d(v_l), r3d(r_l), r3d(misc), w['gla_wa'], w['gla_ba'],
                             w['gla_ng'], s0)
    o_b = jnp.pad(o_b.reshape(bsz, _GV), ((0, mp - bsz), (0, 0))).astype(BF16)
    x1 = _merge(o_a, o_b, x2, w['norm1'], w['w_gate'], w['w_br_a'], w['w_br_b'], w['w_o'], mp)
    y = _mlp(x1, w['norm2'], w['w_up'], w['w_down'], w['norm_f'], mp, MLP_TF)
    kvh = (bsz, 1, NSA_KV_HEADS, NSA_HEAD_DIM)
    rows = [a[:bsz].reshape(kvh) for a in (r0, r1, r2, r3)]
    new_wins = [jnp.concatenate([c[:, 1:], a[:bsz].reshape(kvh)], axis=1) for c, a in zip(wins, (r4, r5))]
    return y[:bsz].reshape(bsz, 1, d), rows, new_wins, s_new


def kernel(x_prompt, x_sample, cache_cmp_k, cache_cmp_v, cache_slc_k, cache_slc_v, cache_win_k, cache_win_v, state_gla, page_table, norm1_g, w_in, b_nsa_gate, cmp_pe_k, cmp_pe_v, cmp_k_w1, cmp_k_w2, cmp_v_w1, cmp_v_w2, gla_w_a2, gla_b_a, gla_norm_g, w_br_a, w_br_b, w_o, norm2_g, w_up, w_down, norm_f):
    assert DEPTH == 1 and norm1_g.shape[0] == 1
    w = _prep_weights(norm1_g[0], w_in[0], b_nsa_gate[0], cmp_pe_k[0], cmp_pe_v[0], cmp_k_w1[0], cmp_k_w2[0],
                      cmp_v_w1[0], cmp_v_w2[0], gla_w_a2[0], gla_b_a[0], gla_norm_g[0], w_br_a[0], w_br_b[0],
                      w_o[0], norm2_g[0], w_up[0], w_down[0], norm_f)
    y_p, rows_p, wins_p, s_p = _layer_prompt(x_prompt, w)
    caches = [c[0] for c in (cache_cmp_k, cache_cmp_v, cache_slc_k, cache_slc_v)]
    y_s, rows_s, wins_s, s_s = _layer_sample(x_sample, caches, [cache_win_k[0], cache_win_v[0]], state_gla[0],
                                             page_table, w)
    outs_p = [a[None] for a in rows_p + wins_p + [s_p]]
    outs_s = [a[None] for a in rows_s + wins_s + [s_s]]
    return (y_p, y_s, *outs_p, *outs_s)
```

```python
import functools

import jax
import jax.numpy as jnp
from jax import lax
from jax.experimental import pallas as pl
from jax.experimental.pallas import tpu as pltpu

D_MODEL = 2048
DEPTH = 1
PAGE_SIZE = 128
NSA_HEADS = 16
NSA_KV_HEADS = 4
NSA_HPG = NSA_HEADS // NSA_KV_HEADS
NSA_HEAD_DIM = 64
NSA_Q = NSA_HEADS * NSA_HEAD_DIM
NSA_KV = NSA_KV_HEADS * NSA_HEAD_DIM
CMP_STRIDE = 16
CMP_BLOCK = 32
CMP_HIDDEN = 128
SLC_BLOCK = 64
N_SELECT = 16
WINDOW = 512
GLA_HEADS = 4
GLA_DK = (D_MODEL // 4) // GLA_HEADS
GLA_DV = (D_MODEL // 2) // GLA_HEADS
GLA_RANK = 16
GLA_TAU = 16.0
D_FF = 4 * D_MODEL
EPS = 1e-6
NEG = -1e30
FORCE = 1e4
SPLITS = (NSA_Q, 6 * NSA_KV, 3 * NSA_HEADS,
          GLA_HEADS * GLA_DK, GLA_HEADS * GLA_DK, GLA_HEADS * GLA_DV, GLA_HEADS * GLA_DV,
          GLA_RANK, 2 * D_MODEL)

F32 = jnp.float32
BF16 = jnp.bfloat16
LANES = 128
VMEM_LIMIT_BYTES = 56 * 1024 * 1024
N_GATE = 3 * NSA_HEADS
GLA_CHUNK = 32
GLA_UNROLL = 8
NSA_TQ = 256
NSA_TK = 256
VT_ROWS = NSA_HEAD_DIM + 16
CMP_PAGES = 16
PROJ_TM = 512
MERGE_TM = 256
MLP_TM = 1024
MLP_TF = 512
SDS = jax.ShapeDtypeStruct


def _params(*sem):
    return pltpu.CompilerParams(dimension_semantics=sem, vmem_limit_bytes=VMEM_LIMIT_BYTES)


def _resident(shape):
    nd = len(shape)
    return pl.BlockSpec(shape, lambda *_: (0,) * nd, pipeline_mode=pl.Buffered(1))


def _rms(x, g):
    return x * lax.rsqrt(jnp.mean(x * x, axis=-1, keepdims=True) + EPS) * g


def _proj_kernel(x_ref, g_ref, w_ref, b_ref, *out_refs, spec):
    h = _rms(x_ref[...], g_ref[...]).astype(BF16)
    misc = None
    for o_ref, (kind, c0, c1, scale) in zip(out_refs, spec):
        if kind == 'gates':
            for g in range(NSA_KV_HEADS):
                o_ref[g] = misc if g == 0 else pltpu.roll(misc, LANES - g * 3 * NSA_HPG, axis=1)
            continue
        r = jnp.dot(h, w_ref[:, c0:c1], preferred_element_type=F32)
        if scale != 1.0:
            r = r * scale
        if kind == 'f32':
            o_ref[...] = r
        elif kind == 'f32t':
            o_ref[0] = r.T
        elif kind == 'bf16':
            o_ref[...] = r.astype(BF16)
        elif kind == 'sigmoid':
            o_ref[...] = jax.nn.sigmoid(r)
        elif kind == 'hm':
            for i in range((c1 - c0) // NSA_HEAD_DIM):
                o_ref[i] = r[:, i * NSA_HEAD_DIM:(i + 1) * NSA_HEAD_DIM].astype(BF16)
        elif kind == 'hmt':
            rt = r.T
            for i in range((c1 - c0) // NSA_HEAD_DIM):
                o_ref[i] = rt[i * NSA_HEAD_DIM:(i + 1) * NSA_HEAD_DIM].astype(BF16)
        elif kind == 'hmvt':
            rt = r.T
            sub = lax.broadcasted_iota(jnp.int32, (VT_ROWS - NSA_HEAD_DIM, r.shape[0]), 0)
            ones = jnp.where(sub == 0, 1.0, 0.0)
            for i in range((c1 - c0) // NSA_HEAD_DIM):
                piece = rt[i * NSA_HEAD_DIM:(i + 1) * NSA_HEAD_DIM]
                o_ref[i] = jnp.concatenate([piece, ones], axis=0).astype(BF16)
        elif kind == 'misc':
            lane = lax.broadcasted_iota(jnp.int32, r.shape, 1)
            misc = jnp.where(lane < N_GATE, jax.nn.sigmoid(r + b_ref[...]), r)
            o_ref[...] = misc


def _proj(x, norm_g, w, bias, spec, tm, seq_len=None):
    m, d = x.shape
    n = w.shape[1]
    assert m % tm == 0
    out_shape, out_specs = [], []
    for kind, c0, c1, _ in spec:
        if kind == 'f32t':
            assert seq_len % tm == 0
            per_seq = seq_len // tm
            out_shape.append(SDS((m // seq_len, c1 - c0, seq_len), F32))
            out_specs.append(pl.BlockSpec((1, c1 - c0, tm), lambda i: (i // per_seq, 0, i % per_seq)))
        elif kind == 'hm':
            nh = (c1 - c0) // NSA_HEAD_DIM
            out_shape.append(SDS((nh, m, NSA_HEAD_DIM), BF16))
            out_specs.append(pl.BlockSpec((nh, tm, NSA_HEAD_DIM), lambda i: (0, i, 0)))
        elif kind in ('hmt', 'hmvt'):
            nh = (c1 - c0) // NSA_HEAD_DIM
            nrow = NSA_HEAD_DIM if kind == 'hmt' else VT_ROWS
            out_shape.append(SDS((nh, nrow, m), BF16))
            out_specs.append(pl.BlockSpec((nh, nrow, tm), lambda i: (0, 0, i)))
        elif kind == 'gates':
            out_shape.append(SDS((NSA_KV_HEADS, m, LANES), F32))
            out_specs.append(pl.BlockSpec((NSA_KV_HEADS, tm, LANES), lambda i: (0, i, 0)))
        else:
            out_shape.append(SDS((m, c1 - c0), BF16 if kind == 'bf16' else F32))
            out_specs.append(pl.BlockSpec((tm, c1 - c0), lambda i: (i, 0)))
    return pl.pallas_call(
        functools.partial(_proj_kernel, spec=tuple(spec)),
        grid=(m // tm,),
        in_specs=[pl.BlockSpec((tm, d), lambda i: (i, 0)), _resident((1, d)), _resident((d, n)),
                  _resident((1, LANES))],
        out_specs=out_specs, out_shape=out_shape,
        compiler_params=_params("parallel"),
    )(x, norm_g, w, bias)


def _cmp_kernel(pt_ref, k_hbm, v_hbm, pek_ref, pev_ref, w1k_ref, w1v_ref, w2k_ref, w2v_ref,
                ok_ref, ov_ref, kbuf, vbuf, tb0, tb1, hk, hv, sem, *, n_pages_step, paged):
    b = pl.program_id(0)
    s = pl.program_id(1)
    ns = pl.num_programs(1)
    t = b * ns + s
    total = pl.num_programs(0) * ns
    rows = n_pages_step * 8
    dh, hid_n = NSA_HEAD_DIM, CMP_HIDDEN

    def copies(tt, slot):
        bb = tt // ns
        ss = tt % ns
        out = []
        for p in range(n_pages_step):
            page = pt_ref[bb, ss * n_pages_step + p]
            if paged:
                src = [hbm.at[page] for hbm in (k_hbm, v_hbm)]
            else:
                tok = pl.ds(pl.multiple_of(page * PAGE_SIZE, PAGE_SIZE), PAGE_SIZE)
                src = [hbm.at[bb, :, tok] for hbm in (k_hbm, v_hbm)]
            out.append(pltpu.make_async_copy(src[0], kbuf.at[slot, p], sem.at[slot, 0]))
            out.append(pltpu.make_async_copy(src[1], vbuf.at[slot, p], sem.at[slot, 1]))
        return out

    slot = t % 2

    @pl.when(t == 0)
    def _():
        for c in copies(t, slot):
            c.start()

    @pl.when(t + 1 < total)
    def _():
        for c in copies(t + 1, 1 - slot):
            c.start()

    for c in copies(t, slot):
        c.wait()

    r0 = pl.multiple_of(s * rows, rows)
    tbufs = (tb0, tb1)
    low = lax.broadcasted_iota(jnp.int32, (rows, LANES), 1) < dh
    tok_pairs = CMP_STRIDE // 2
    for buf, w1_ref, h_ref in ((kbuf, w1k_ref, hk), (vbuf, w1v_ref, hv)):
        for p in range(n_pages_step):
            for q in range(2):
                tbufs[q][p * PAGE_SIZE:(p + 1) * PAGE_SIZE, :] = buf[slot, p, q * LANES:(q + 1) * LANES, :].T
        for q in range(2):
            even, odd = [], []
            for m in range(tok_pairs):
                x0 = tbufs[q][pl.ds(2 * m, rows, stride=CMP_STRIDE), :]
                x1 = tbufs[q][pl.ds(2 * m + 1, rows, stride=CMP_STRIDE), :]
                even.append(jnp.where(low, x0, pltpu.roll(x1, dh, axis=1)))
                odd.append(jnp.where(low, pltpu.roll(x0, dh, axis=1), x1))
            for g, parts in ((2 * q, even), (2 * q + 1, odd)):
                z = jnp.concatenate(parts, axis=1).astype(BF16)
                h_ref[pl.ds(r0, rows), g * 2 * hid_n:(g + 1) * 2 * hid_n] = jnp.dot(
                    z, w1_ref[...], preferred_element_type=F32)

    @pl.when(s == ns - 1)
    def _():
        for h_ref, pe_ref, w1_ref, w2_ref, o_ref in ((hk, pek_ref, w1k_ref, w2k_ref, ok_ref),
                                                     (hv, pev_ref, w1v_ref, w2v_ref, ov_ref)):
            n_chunk = h_ref.shape[0]
            pe = jnp.broadcast_to(pe_ref[...], (2, 16, pe_ref.shape[2])).astype(BF16)
            pe_term = (jnp.dot(pe[0], w1_ref[:, 0:hid_n], preferred_element_type=F32)
                       + jnp.dot(pe[1], w1_ref[:, hid_n:2 * hid_n], preferred_element_type=F32))[0:1]
            hid = []
            for g in range(NSA_KV_HEADS):
                first = h_ref[:, g * 2 * hid_n:g * 2 * hid_n + hid_n]
                last = h_ref[:, g * 2 * hid_n + hid_n:(g + 1) * 2 * hid_n]
                hid.append(first + pltpu.roll(last, n_chunk - 1, axis=0) + pe_term)
            hid = jnp.concatenate(hid, axis=1)
            res = jnp.dot(jax.nn.gelu(hid).astype(BF16), w2_ref[...], preferred_element_type=F32)
            if o_ref is ok_ref:
                for g in range(NSA_KV_HEADS):
                    o_ref[0, g] = res[:, g * NSA_HEAD_DIM:(g + 1) * NSA_HEAD_DIM].astype(BF16)
            else:
                res_t = res.T
                for g in range(NSA_KV_HEADS):
                    o_ref[0, g] = res_t[g * NSA_HEAD_DIM:(g + 1) * NSA_HEAD_DIM].astype(BF16)


def _cmp_weights(pe, w1, w2):
    g = NSA_KV_HEADS
    half = CMP_STRIDE * NSA_HEAD_DIM
    pe_t = pe.reshape(2, 1, half)
    w1ab = jnp.concatenate([w1[:half], w1[half:]], axis=1)
    w2b = jnp.einsum('jd,gh->gjhd', w2, jnp.eye(g, dtype=F32)).reshape(g * CMP_HIDDEN, g * NSA_HEAD_DIM)
    return pe_t, w1ab.astype(BF16), w2b.astype(BF16)


def _compress(page_table, k_src, v_src, cw, paged):
    bsz, n_pages = page_table.shape
    p_step = min(CMP_PAGES, n_pages)
    ns = n_pages // p_step
    n_chunk = n_pages * 8
    page_shape = (NSA_KV, PAGE_SIZE)
    assert NSA_KV == 2 * LANES
    assert k_src.shape[1:] == (page_shape if paged else (NSA_KV, n_pages * PAGE_SIZE))
    half = CMP_STRIDE * NSA_HEAD_DIM
    gh = NSA_KV_HEADS * CMP_HIDDEN
    k_sds = SDS((bsz, NSA_KV_HEADS, n_chunk, NSA_HEAD_DIM), BF16)
    v_sds = SDS((bsz, NSA_KV_HEADS, NSA_HEAD_DIM, n_chunk), BF16)
    k_spec = pl.BlockSpec((1, NSA_KV_HEADS, n_chunk, NSA_HEAD_DIM), lambda b, s, pt: (b, 0, 0, 0))
    v_spec = pl.BlockSpec((1, NSA_KV_HEADS, NSA_HEAD_DIM, n_chunk), lambda b, s, pt: (b, 0, 0, 0))
    grid_spec = pltpu.PrefetchScalarGridSpec(
        num_scalar_prefetch=1, grid=(bsz, ns),
        in_specs=[pl.BlockSpec(memory_space=pl.ANY), pl.BlockSpec(memory_space=pl.ANY),
                  _resident((2, 1, half)), _resident((2, 1, half)),
                  _resident((half, 2 * CMP_HIDDEN)), _resident((half, 2 * CMP_HIDDEN)),
                  _resident((gh, NSA_KV)), _resident((gh, NSA_KV))],
        out_specs=[k_spec, v_spec],
        scratch_shapes=[pltpu.VMEM((2, p_step) + page_shape, F32), pltpu.VMEM((2, p_step) + page_shape, F32),
                        pltpu.VMEM((p_step * PAGE_SIZE, LANES), F32), pltpu.VMEM((p_step * PAGE_SIZE, LANES), F32),
                        pltpu.VMEM((n_chunk, 2 * gh), F32), pltpu.VMEM((n_chunk, 2 * gh), F32),
                        pltpu.SemaphoreType.DMA((2, 2))])
    return pl.pallas_call(
        functools.partial(_cmp_kernel, n_pages_step=p_step, paged=paged),
        grid_spec=grid_spec, out_shape=[k_sds, v_sds],
        compiler_params=_params("arbitrary", "arbitrary"),
    )(page_table, k_src, v_src, cw['pe_k'], cw['pe_v'], cw['w1_k'], cw['w1_v'], cw['w2_k'], cw['w2_v'])


def _softmax_rows(s, mask):
    sm = jnp.where(mask, s, NEG)
    m = jnp.max(sm, axis=-1, keepdims=True)
    e = jnp.where(mask, jnp.exp(sm - m), 0.0)
    return e, jnp.sum(e, axis=-1, keepdims=True)


def _safe_inv(l):
    return jnp.where(l > 0.0, 1.0 / jnp.where(l > 0.0, l, 1.0), 0.0)


M_INIT = -1e20
LOG2_E = 1.4426950408889634


def _nsa_prompt_kernel(qt_ref, kc_ref, vct_ref, ks_ref, vst_ref, kw_ref, vwt_ref, gate_ref,
                       o_ref, *scratch, n_cmp, n_slc, n_sel):
    tq, tk, hpg, dh = NSA_TQ, NSA_TK, NSA_HPG, NSA_HEAD_DIM
    rk_refs = scratch[:tq // LANES]
    selb_ref, m_ref, acc_ref = scratch[tq // LANES:tq // LANES + 3]
    s_refs, p_refs, a_refs = (scratch[tq // LANES + 3 + 2 * i:tq // LANES + 5 + 2 * i] for i in range(3))
    qi = pl.program_id(2)
    q0 = qi * tq
    n_chunk = kc_ref.shape[2]
    r = SLC_BLOCK // CMP_STRIDE
    n_row = n_chunk // r

    ci = lax.broadcasted_iota(jnp.int32, (n_chunk, tq), 0)
    pos_c = q0 + lax.broadcasted_iota(jnp.int32, (n_chunk, tq), 1)
    m_c = (ci < n_cmp) & (ci * CMP_STRIDE + (CMP_BLOCK - 1) <= pos_c)
    kc = kc_ref[0, 0]
    vct = vct_ref[0, 0]
    o_c = []
    psum = None
    for h in range(hpg):
        s = jnp.where(m_c, jnp.dot(kc, qt_ref[0, h], preferred_element_type=F32), NEG)
        e = jnp.where(m_c, jnp.exp2(s - jnp.max(s, axis=0, keepdims=True)), 0.0)
        p = e * _safe_inv(jnp.sum(e, axis=0, keepdims=True))
        o_c.append(jnp.dot(vct, p.astype(BF16), preferred_element_type=F32))
        psum = p if psum is None else psum + p

    imp = pltpu.roll(psum, 1, axis=0) + psum
    for o in range(1, r):
        imp = imp + pltpu.roll(psum, n_chunk - o, axis=0)
    parts = []
    for i, rk_ref in enumerate(rk_refs):
        rk_ref[...] = imp[:, i * LANES:(i + 1) * LANES]
        parts.append(rk_ref[pl.ds(0, n_row, stride=r), :])
    imp_b = jnp.concatenate(parts, axis=1)
    j_io = lax.broadcasted_iota(jnp.int32, (n_row, tq), 0)
    pos_b = q0 + lax.broadcasted_iota(jnp.int32, (n_row, tq), 1)
    valid = (j_io < n_slc) & (j_io * SLC_BLOCK <= pos_b)
    forced = (j_io == 0) | (j_io == pos_b // SLC_BLOCK)
    sc = jnp.where(valid, imp_b + jnp.where(forced, FORCE, 0.0), NEG)
    rank = jnp.zeros((n_row, tq), F32)
    for k in range(n_slc):
        ck = sc[k:k + 1, :]
        beats = (ck > sc) | ((ck == sc) & (j_io > k))
        rank = rank + jnp.where(beats, 1.0, 0.0)
    selb_ref[...] = jnp.where(rank < n_sel, 0.0, NEG)

    dlt = lax.broadcasted_iota(jnp.int32, (tk, tq), 0) - lax.broadcasted_iota(jnp.int32, (tk, tq), 1)
    blocks_per_tile = tk // SLC_BLOCK

    def flash(k_ref, vt_ref, lo, hi, bias_fn):
        n_tiles = k_ref.shape[2] // tk
        m_ref[...] = jnp.full(m_ref.shape, M_INIT, F32)
        acc_ref[...] = jnp.zeros(acc_ref.shape, F32)
        p_refs[1][...] = jnp.zeros(p_refs[1].shape, BF16)
        a_refs[1][...] = jnp.ones(a_refs[1].shape, F32)

        def tile_start(kb):
            return pl.multiple_of(jnp.clip(kb, 0, n_tiles - 1) * tk, tk)

        def scores(kb, s_ref):
            kk = k_ref[0, 0, pl.ds(tile_start(kb), tk), :]
            bias = bias_fn(kb)
            for h in range(hpg):
                for c in range(tk // SLC_BLOCK):
                    rs = slice(c * SLC_BLOCK, (c + 1) * SLC_BLOCK)
                    s_ref[h, rs, :] = jnp.dot(kk[rs], qt_ref[0, h], preferred_element_type=F32) + bias[rs]

        def softmax(s_ref, p_ref, a_ref):
            for h in range(hpg):
                for c in range(tq // LANES):
                    cs = slice(c * LANES, (c + 1) * LANES)
                    s = s_ref[h, :, cs]
                    m_prev = m_ref[h, :, cs]
                    m_new = jnp.maximum(m_prev, jnp.max(s, axis=0, keepdims=True))
                    p_ref[h, :, cs] = jnp.exp2(s - m_new).astype(BF16)
                    a_ref[h, :, cs] = jnp.exp2(m_prev - m_new)
                    m_ref[h, :, cs] = m_new

        def weighted_values(kb, p_ref, a_ref):
            vt = vt_ref[0, :, pl.ds(tile_start(kb), tk)]
            for h in range(hpg):
                acc_ref[h] = a_ref[h] * acc_ref[h] + jnp.dot(vt, p_ref[h], preferred_element_type=F32)

        scores(lo, s_refs[0])

        def body(j, carry):
            i = lo + 2 * j
            scores(i + 1, s_refs[1])
            softmax(s_refs[0], p_refs[0], a_refs[0])
            weighted_values(i - 1, p_refs[1], a_refs[1])
            scores(i + 2, s_refs[0])
            softmax(s_refs[1], p_refs[1], a_refs[1])
            weighted_values(i, p_refs[0], a_refs[0])
            return carry

        n_pairs = (hi - lo + 1) // 2
        lax.fori_loop(0, n_pairs, body, 0)
        weighted_values(lo + 2 * n_pairs - 1, p_refs[1], a_refs[1])
        return [acc_ref[h, 0:dh, :] * _safe_inv(acc_ref[h, dh:dh + 1, :]) for h in range(hpg)]

    def slc_bias(kb):
        first = jnp.minimum(kb, n_slc // blocks_per_tile - 1) * blocks_per_tile
        rows = [jnp.broadcast_to(selb_ref[pl.ds(first + i, 1), :], (SLC_BLOCK, tq))
                for i in range(blocks_per_tile)]
        return jnp.where(dlt + (kb * tk - q0) <= 0, jnp.concatenate(rows, axis=0), NEG)

    def win_bias(kb):
        d = dlt + (kb * tk - q0)
        return jnp.where((d <= 0) & (d >= -WINDOW), 0.0, NEG)

    hi = (q0 + tq - 1) // tk + 1
    o_s = flash(ks_ref, vst_ref, 0, hi, slc_bias)
    o_w = flash(kw_ref, vwt_ref, jnp.maximum(q0 - WINDOW, 0) // tk, hi, win_bias)

    gt = gate_ref[0].T
    outs = [gt[h:h + 1] * o_c[h] + gt[hpg + h:hpg + h + 1] * o_s[h] + gt[2 * hpg + h:2 * hpg + h + 1] * o_w[h]
            for h in range(hpg)]
    o_ref[...] = jnp.concatenate(outs, axis=0).T.astype(BF16)


def _nsa_prompt(q_t, ks_hm, vs_t, kw_hm, vw_t, kcmp, vcmp_t, gates_hm, bsz, t):
    g, hpg, dh, tq = NSA_KV_HEADS, NSA_HPG, NSA_HEAD_DIM, NSA_TQ
    r = SLC_BLOCK // CMP_STRIDE
    n_slc = t // SLC_BLOCK
    assert t % NSA_TK == 0 and t % tq == 0 and n_slc * r <= LANES and NSA_TK % SLC_BLOCK == 0
    nq = t // tq
    n_cmp = kcmp.shape[2] - 1
    if kcmp.shape[2] < LANES:
        fill = LANES - kcmp.shape[2]
        kcmp = jnp.pad(kcmp, ((0, 0), (0, 0), (0, fill), (0, 0)))
        vcmp_t = jnp.pad(vcmp_t, ((0, 0), (0, 0), (0, 0), (0, fill)))
    n_chunk = kcmp.shape[2]
    assert n_chunk == LANES
    q4 = q_t.reshape(g, hpg, dh, bsz * t)

    def k_spec():
        return pl.BlockSpec((1, 1, t, dh), lambda b, gg, qi: (gg, b, 0, 0))

    def vt_spec():
        return pl.BlockSpec((1, VT_ROWS, t), lambda b, gg, qi: (gg, 0, b))

    def per_bt(a):
        return a.reshape(g, bsz, t, a.shape[-1])

    return pl.pallas_call(
        functools.partial(_nsa_prompt_kernel, n_cmp=n_cmp, n_slc=n_slc, n_sel=min(N_SELECT, n_slc)),
        grid=(bsz, g, nq),
        in_specs=[pl.BlockSpec((1, hpg, dh, tq), lambda b, gg, qi: (gg, 0, 0, b * nq + qi)),
                  pl.BlockSpec((1, 1, n_chunk, dh), lambda b, gg, qi: (b, gg, 0, 0)),
                  pl.BlockSpec((1, 1, dh, n_chunk), lambda b, gg, qi: (b, gg, 0, 0)),
                  k_spec(), vt_spec(), k_spec(), vt_spec(),
                  pl.BlockSpec((1, tq, LANES), lambda b, gg, qi: (gg, b * nq + qi, 0))],
        out_specs=pl.BlockSpec((tq, hpg * dh), lambda b, gg, qi: (b * nq + qi, gg)),
        out_shape=SDS((bsz * t, NSA_Q), BF16),
        scratch_shapes=[pltpu.VMEM((n_chunk, LANES), F32)] * (tq // LANES) + [
                        pltpu.VMEM((n_chunk // r, tq), F32),
                        pltpu.VMEM((hpg, 1, tq), F32), pltpu.VMEM((hpg, VT_ROWS, tq), F32),
                        pltpu.VMEM((hpg, NSA_TK, tq), F32), pltpu.VMEM((hpg, NSA_TK, tq), F32),
                        pltpu.VMEM((hpg, NSA_TK, tq), BF16), pltpu.VMEM((hpg, NSA_TK, tq), BF16),
                        pltpu.VMEM((hpg, 1, tq), F32), pltpu.VMEM((hpg, 1, tq), F32)],
        compiler_params=_params("parallel", "parallel", "arbitrary"),
    )(q4, kcmp, vcmp_t, per_bt(ks_hm), vs_t, per_bt(kw_hm), vw_t, gates_hm)


def _gla_prompt_kernel(q_ref, k_ref, v_ref, r_ref, misc_ref, wa_ref, ba_ref, ng_ref,
                       o_ref, s_ref, qe_ref, qt_ref, kt_ref, kh_ref, d_ref, u_ref, st_ref):
    t = q_ref.shape[0]
    c = GLA_CHUNK
    n = t // c
    dk, dv = GLA_DK, GLA_DV
    lr = misc_ref[:, N_GATE:N_GATE + GLA_RANK].astype(BF16)
    x = jnp.dot(lr, wa_ref[...], preferred_element_type=F32) + ba_ref[...]
    g = jax.nn.log_sigmoid(x) / GLA_TAU
    row = lax.broadcasted_iota(jnp.int32, (t, dk), 0) % c
    b = g
    sh = 1
    while sh < c:
        b = b + jnp.where(row >= sh, pltpu.roll(b, sh, axis=0), 0.0)
        sh *= 2
    b3 = b.reshape(n, c, dk)
    b_last = jnp.broadcast_to(b3[:, c - 1:c, :], (n, c, dk)).reshape(t, dk)
    b_mid = jnp.broadcast_to(b3[:, c // 2 - 1:c // 2, :], (n, c, dk)).reshape(t, dk)
    q = q_ref[...] * (GLA_DK ** -0.5)
    k = k_ref[...]
    qe_ref[...] = (q * jnp.exp(b)).astype(BF16)
    qt_ref[...] = (q * jnp.exp(b - b_mid)).astype(BF16)
    kt_ref[...] = (k * jnp.exp(b_mid - b)).astype(BF16)
    kh_ref[...] = (k * jnp.exp(b_last - b)).astype(BF16)
    d_ref[...] = jnp.exp(b_last)
    tn = (((0,), (0,)), ((), ()))
    nt = (((1,), (1,)), ((), ()))

    def chunk_update(i, carry):
        r0 = pl.multiple_of(i * c, c)
        u_ref[i] = lax.dot_general(v_ref[pl.ds(r0, c), :], kh_ref[pl.ds(r0, c), :], tn,
                                   preferred_element_type=F32)
        return carry

    lax.fori_loop(0, n, chunk_update, 0, unroll=GLA_UNROLL)

    st_ref[...] = jnp.zeros((dv, dk), F32)

    def recur(i, carry):
        st = st_ref[...]
        st_ref[...] = st * d_ref[pl.ds(pl.multiple_of(i * c, c), 1), :] + u_ref[i]
        u_ref[i] = st
        return carry

    lax.fori_loop(0, n, recur, 0)
    s_ref[0, 0] = st_ref[...].T

    tril = lax.broadcasted_iota(jnp.int32, (c, c), 0) >= lax.broadcasted_iota(jnp.int32, (c, c), 1)
    ng = ng_ref[...]

    def chunk_out(i, carry):
        r0 = pl.multiple_of(i * c, c)
        vv = v_ref[pl.ds(r0, c), :]
        o = lax.dot_general(qe_ref[pl.ds(r0, c), :], u_ref[i].astype(BF16), nt, preferred_element_type=F32)
        a = lax.dot_general(qt_ref[pl.ds(r0, c), :], kt_ref[pl.ds(r0, c), :], nt, preferred_element_type=F32)
        a = jnp.where(tril, a, 0.0).astype(BF16)
        o = o + jnp.dot(a, vv, preferred_element_type=F32)
        o = _rms(o, ng) * jax.nn.silu(r_ref[pl.ds(r0, c), :])
        o_ref[pl.ds(r0, c), :] = o.astype(BF16)
        return carry

    lax.fori_loop(0, n, chunk_out, 0, unroll=GLA_UNROLL)


def _gla_prompt(q_l, k_l, v_l, r_l, misc, wa, ba, ng, bsz, t):
    h, dk, dv = GLA_HEADS, GLA_DK, GLA_DV
    assert t % GLA_CHUNK == 0
    n = t // GLA_CHUNK
    return pl.pallas_call(
        _gla_prompt_kernel,
        grid=(bsz, h),
        in_specs=[pl.BlockSpec((t, dk), lambda b, hh: (b, hh)), pl.BlockSpec((t, dk), lambda b, hh: (b, hh)),
                  pl.BlockSpec((t, dv), lambda b, hh: (b, hh)), pl.BlockSpec((t, dv), lambda b, hh: (b, hh)),
                  pl.BlockSpec((t, LANES), lambda b, hh: (b, 0)),
                  pl.BlockSpec((GLA_RANK, dk), lambda b, hh: (0, hh)), pl.BlockSpec((1, dk), lambda b, hh: (0, hh)),
                  _resident((1, dv))],
        out_specs=[pl.BlockSpec((t, dv), lambda b, hh: (b, hh)),
                   pl.BlockSpec((1, 1, dk, dv), lambda b, hh: (b, hh, 0, 0))],
        out_shape=[SDS((bsz * t, h * dv), BF16), SDS((bsz, h, dk, dv), F32)],
        scratch_shapes=[pltpu.VMEM((t, dk), BF16), pltpu.VMEM((t, dk), BF16), pltpu.VMEM((t, dk), BF16),
                        pltpu.VMEM((t, dk), BF16), pltpu.VMEM((t, dk), F32),
                        pltpu.VMEM((n, dv, dk), F32), pltpu.VMEM((dv, dk), F32)],
        compiler_params=_params("parallel", "parallel"),
    )(q_l, k_l, v_l, r_l, misc, wa, ba, ng)


def _merge_kernel(oa_ref, ob_ref, x_ref, g_ref, wg_ref, wa_ref, wb_ref, wo_ref, x1_ref):
    x = x_ref[...]
    h = _rms(x, g_ref[...]).astype(BF16)
    d = D_MODEL
    ga = jax.nn.sigmoid(jnp.dot(h, wg_ref[:, 0:d], preferred_element_type=F32))
    u = ga * jnp.dot(oa_ref[...], wa_ref[...], preferred_element_type=F32)
    gb = jax.nn.sigmoid(jnp.dot(h, wg_ref[:, d:2 * d], preferred_element_type=F32))
    u = u + gb * jnp.dot(ob_ref[...], wb_ref[...], preferred_element_type=F32)
    x1_ref[...] = x + jnp.dot(u.astype(BF16), wo_ref[...], preferred_element_type=F32)


def _merge(o_a, o_b, x, norm_g, w_gate, w_a, w_b, w_o, tm):
    m, d = x.shape
    assert m % tm == 0
    return pl.pallas_call(
        _merge_kernel, grid=(m // tm,),
        in_specs=[pl.BlockSpec((tm, NSA_Q), lambda i: (i, 0)), pl.BlockSpec((tm, GLA_HEADS * GLA_DV), lambda i: (i, 0)),
                  pl.BlockSpec((tm, d), lambda i: (i, 0)), _resident((1, d)), _resident((d, 2 * d)),
                  _resident(w_a.shape), _resident(w_b.shape), _resident(w_o.shape)],
        out_specs=pl.BlockSpec((tm, d), lambda i: (i, 0)),
        out_shape=SDS((m, d), F32),
        compiler_params=_params("parallel"),
    )(o_a, o_b, x, norm_g, w_gate, w_a, w_b, w_o)


def _mlp_kernel(x1_ref, g2_ref, wu_ref, wd_ref, gf_ref, y_ref, h_ref):
    j = pl.program_id(1)

    @pl.when(j == 0)
    def _():
        x1 = x1_ref[...]
        h_ref[...] = _rms(x1, g2_ref[...]).astype(BF16)
        y_ref[...] = x1

    up = jnp.maximum(jnp.dot(h_ref[...], wu_ref[...], preferred_element_type=F32), 0.0)
    y_ref[...] += jnp.dot((up * up).astype(BF16), wd_ref[...], preferred_element_type=F32)

    @pl.when(j == pl.num_programs(1) - 1)
    def _():
        y_ref[...] = _rms(y_ref[...], gf_ref[...])


def _mlp(x1, g2, w_up, w_down, gf, tm, tf):
    m, d = x1.shape
    ff = w_up.shape[1]
    assert m % tm == 0 and ff % tf == 0
    return pl.pallas_call(
        _mlp_kernel, grid=(m // tm, ff // tf),
        in_specs=[pl.BlockSpec((tm, d), lambda i, j: (i, 0)), _resident((1, d)),
                  pl.BlockSpec((d, tf), lambda i, j: (0, j)), pl.BlockSpec((tf, d), lambda i, j: (j, 0)),
                  _resident((1, d))],
        out_specs=pl.BlockSpec((tm, d), lambda i, j: (i, 0)),
        out_shape=SDS((m, d), F32),
        scratch_shapes=[pltpu.VMEM((tm, d), BF16)],
        compiler_params=_params("parallel", "arbitrary"),
    )(x1, g2, w_up, w_down, gf)


def _prep_weights(norm1_g, w_in, b_nsa_gate, cmp_pe_k, cmp_pe_v, cmp_k_w1, cmp_k_w2, cmp_v_w1, cmp_v_w2,
                  gla_w_a2, gla_b_a, gla_norm_g, w_br_a, w_br_b, w_o, norm2_g, w_up, w_down, norm_f):
    pts = [0]
    for s in SPLITS:
        pts.append(pts[-1] + s)
    c_q, c_kv, c_g, c_ql, c_kl, c_vl, c_rl, c_lr, c_br, c_end = pts
    gcols = jnp.asarray([c_g + (g * NSA_HPG + h) * 3 + c for g in range(NSA_KV_HEADS)
                         for c in range(3) for h in range(NSA_HPG)], jnp.int32)
    pad = jnp.zeros((D_MODEL, LANES - N_GATE - GLA_RANK), F32)
    w_misc = jnp.concatenate([w_in[:, gcols], w_in[:, c_lr:c_br], pad], axis=1)
    b_misc = jnp.concatenate([b_nsa_gate[gcols - c_g], jnp.zeros((LANES - N_GATE,), F32)])[None, :]
    w = dict(
        norm1=norm1_g[None, :], norm2=norm2_g[None, :], norm_f=norm_f[None, :],
        w_a=w_in[:, c_q:c_g].astype(BF16),
        w_b=jnp.concatenate([w_in[:, c_ql:c_lr], w_misc], axis=1).astype(BF16),
        w_gate=w_in[:, c_br:c_end].astype(BF16),
        b_misc=b_misc,
        gla_wa=gla_w_a2.astype(BF16), gla_ba=gla_b_a[None, :], gla_ng=gla_norm_g[None, :],
        w_br_a=w_br_a.astype(BF16), w_br_b=w_br_b.astype(BF16), w_o=w_o.astype(BF16),
        w_up=w_up.astype(BF16), w_down=w_down.astype(BF16),
    )
    cw = {}
    cw['pe_k'], cw['w1_k'], cw['w2_k'] = _cmp_weights(cmp_pe_k, cmp_k_w1, cmp_k_w2)
    cw['pe_v'], cw['w1_v'], cw['w2_v'] = _cmp_weights(cmp_pe_v, cmp_v_w1, cmp_v_w2)
    w['cmp'] = cw
    return w


_KV6 = 6 * NSA_KV
_GQ = GLA_HEADS * GLA_DK
_GV = GLA_HEADS * GLA_DV


def _layer_prompt(x, w):
    bsz, t, d = x.shape
    m = bsz * t
    x2 = x.reshape(m, d)
    zb = jnp.zeros((1, LANES), F32)
    spec_a = [('hmt', 0, NSA_Q, NSA_HEAD_DIM ** -0.5 * LOG2_E)]
    spec_a += [('f32t', NSA_Q + j * NSA_KV, NSA_Q + (j + 1) * NSA_KV, 1.0) for j in range(6)]
    spec_a += [('hm' if j % 2 == 0 else 'hmvt', NSA_Q + j * NSA_KV, NSA_Q + (j + 1) * NSA_KV, 1.0)
               for j in range(2, 6)]
    q_t, r0, r1, r2, r3, r4, r5, ks_hm, vs_t, kw_hm, vw_t = _proj(x2, w['norm1'], w['w_a'], zb, spec_a, PROJ_TM,
                                                                   seq_len=t)
    spec_b = [('f32', 0, _GQ, 1.0), ('f32', _GQ, 2 * _GQ, 1.0), ('bf16', 2 * _GQ, 2 * _GQ + _GV, 1.0),
              ('f32', 2 * _GQ + _GV, 2 * _GQ + 2 * _GV, 1.0),
              ('misc', 2 * _GQ + 2 * _GV, 2 * _GQ + 2 * _GV + LANES, 1.0), ('gates', 0, 0, 1.0)]
    q_l, k_l, v_l, r_l, misc, gates_hm = _proj(x2, w['norm1'], w['w_b'], w['b_misc'], spec_b, PROJ_TM)

    n_pages = t // PAGE_SIZE
    ident = jnp.broadcast_to(jnp.arange(n_pages, dtype=jnp.int32), (bsz, n_pages))
    kcmp, vcmp = _compress(ident, r0, r1, w['cmp'], paged=False)
    o_a = _nsa_prompt(q_t, ks_hm, vs_t, kw_hm, vw_t, kcmp, vcmp, gates_hm, bsz, t)
    o_b, s_new = _gla_prompt(q_l, k_l, v_l, r_l, misc, w['gla_wa'], w['gla_ba'], w['gla_ng'], bsz, t)
    x1 = _merge(o_a, o_b, x2, w['norm1'], w['w_gate'], w['w_br_a'], w['w_br_b'], w['w_o'], MERGE_TM)
    y = _mlp(x1, w['norm2'], w['w_up'], w['w_down'], w['norm_f'], MLP_TM if m % MLP_TM == 0 else PROJ_TM, MLP_TF)
    def token_major(a):
        return a.reshape(bsz, NSA_KV_HEADS, NSA_HEAD_DIM, a.shape[-1]).transpose(0, 3, 1, 2)

    rows = [token_major(a) for a in (r0, r1, r2, r3)]
    n_keep = min(WINDOW, t)
    wins = [token_major(a[:, :, t - n_keep:]) for a in (r4, r5)]
    return y.reshape(bsz, t, d), rows, wins, s_new


def _group_rows(parts):
    rowg = lax.broadcasted_iota(jnp.int32, parts[0].shape, 0) // NSA_HPG
    out = parts[0]
    for g in range(1, NSA_KV_HEADS):
        out = jnp.where(rowg == g, parts[g], out)
    return out


def _sample_select_kernel(q_ref, kc_ref, vc_ref, oc_ref, idx_ref, *, n_cmp, pos, n_pick):
    q = q_ref[0]
    nt = (((1,), (1,)), ((), ()))
    n_chunk = kc_ref.shape[2]
    s = _group_rows([lax.dot_general(q, kc_ref[0, g], nt, preferred_element_type=F32)
                     for g in range(NSA_KV_HEADS)])
    lane = lax.broadcasted_iota(jnp.int32, s.shape, 1)
    mask = (lane < n_cmp) & (lane * CMP_STRIDE + (CMP_BLOCK - 1) <= pos)
    e, l = _softmax_rows(s, mask)
    p = e * _safe_inv(l)
    pb = p.astype(BF16)
    oc_ref[0] = _group_rows([lax.dot_general(pb, vc_ref[0, g], nt, preferred_element_type=F32)
                             for g in range(NSA_KV_HEADS)])
    nr = p.shape[0]
    y = p + pltpu.roll(p, nr - 1, axis=0)
    psum = y + pltpu.roll(y, nr - 2, axis=0)
    imp = pltpu.roll(psum, 1, axis=1) + psum
    r = SLC_BLOCK // CMP_STRIDE
    for o in range(1, r):
        imp = imp + pltpu.roll(psum, n_chunk - o, axis=1)
    blk = lane // r
    is_blk = lane % r == 0
    valid = blk * SLC_BLOCK <= pos
    forced = (blk == 0) | (blk == pos // SLC_BLOCK)
    score = jnp.where(is_blk & valid, imp + jnp.where(forced, FORCE, 0.0), NEG)
    lane_f = lane.astype(F32)
    out_lane = lax.broadcasted_iota(jnp.int32, (nr, LANES), 1)
    picked = jnp.zeros((nr, LANES), F32)
    for k in range(n_pick):
        mx = jnp.max(score, axis=-1, keepdims=True)
        ix = jnp.min(jnp.where(score == mx, lane_f, float(n_chunk)), axis=-1, keepdims=True)
        picked = jnp.where(out_lane == k, ix, picked)
        score = jnp.where(lane_f == ix, 2.0 * NEG, score)
    idx_ref[0] = picked.astype(jnp.int32) // r


def _sample_select(q_s, kcmp, vcmp, pos, n_pick):
    bsz, g, n_chunk, dh = kcmp.shape
    nh = NSA_HEADS
    cmp_spec = pl.BlockSpec((1, g, n_chunk, dh), lambda b: (b, 0, 0, 0))
    cmpt_spec = pl.BlockSpec((1, g, dh, n_chunk), lambda b: (b, 0, 0, 0))
    return pl.pallas_call(
        functools.partial(_sample_select_kernel, n_cmp=n_chunk - 1, pos=pos, n_pick=n_pick),
        grid=(bsz,),
        in_specs=[pl.BlockSpec((1, nh, dh), lambda b: (b, 0, 0)), cmp_spec, cmpt_spec],
        out_specs=[pl.BlockSpec((1, nh, dh), lambda b: (b, 0, 0)), pl.BlockSpec((1, nh, LANES), lambda b: (b, 0, 0))],
        out_shape=[SDS((bsz, nh, dh), F32), SDS((bsz, nh, LANES), jnp.int32)],
        compiler_params=_params("parallel"),
    )(q_s, kcmp, vcmp)


def _sample_attend_kernel(idx_ref, pt_ref, q_ref, oc_ref, ks_hbm, vs_hbm, kw_ref, vw_ref,
                          nks_ref, nvs_ref, nkw_ref, nvw_ref, gate_ref, o_ref, ksel, vsel, sem, *, n_pick):
    b = pl.program_id(0)
    g_n, dh = NSA_KV_HEADS, NSA_HEAD_DIM
    half = PAGE_SIZE // SLC_BLOCK

    def copies():
        out = []
        for g in range(g_n):
            for r in range(n_pick):
                page = pt_ref[b, idx_ref[b * g_n + g, r] // half]
                out.append(pltpu.make_async_copy(ks_hbm.at[page, g], ksel.at[g, r], sem.at[0]))
                out.append(pltpu.make_async_copy(vs_hbm.at[page, g], vsel.at[g, r], sem.at[1]))
        return out

    for c in copies():
        c.start()

    q = q_ref[0]
    qf = q.astype(F32)
    nt = (((1,), (1,)), ((), ()))

    def attend(keys_t, vals_t, bias, k_new, v_new):
        s = _group_rows([jnp.dot(q, keys_t[g].astype(BF16), preferred_element_type=F32)
                         + (0.0 if bias is None else bias[g]) for g in range(g_n)])
        s_new = _group_rows([jnp.sum(qf * k_new[:, g * dh:(g + 1) * dh], axis=-1, keepdims=True)
                             for g in range(g_n)])
        m = jnp.maximum(jnp.max(s, axis=-1, keepdims=True), s_new)
        e = jnp.exp(s - m)
        e_new = jnp.exp(s_new - m)
        l = jnp.sum(e, axis=-1, keepdims=True) + e_new
        eb = e.astype(BF16)
        acc = _group_rows([lax.dot_general(eb, vals_t[g].astype(BF16), nt, preferred_element_type=F32)
                           + e_new * v_new[:, g * dh:(g + 1) * dh] for g in range(g_n)])
        return acc / l

    o_w = attend([kw_ref[0, g] for g in range(g_n)], [vw_ref[0, g] for g in range(g_n)], None,
                 nkw_ref[0], nvw_ref[0])
    for c in copies():
        c.wait()
    lin = lax.broadcasted_iota(jnp.int32, (1, PAGE_SIZE), 1)
    bias = []
    for g in range(g_n):
        parts = []
        for r in range(n_pick):
            off = (idx_ref[b * g_n + g, r] % half) * SLC_BLOCK
            parts.append(jnp.where((lin >= off) & (lin < off + SLC_BLOCK), 0.0, NEG))
        bias.append(jnp.concatenate(parts, axis=1))

    def tiles(buf, g):
        return jnp.concatenate([buf[g, r] for r in range(n_pick)], axis=1)

    o_s = attend([tiles(ksel, g) for g in range(g_n)], [tiles(vsel, g) for g in range(g_n)], bias,
                 nks_ref[0], nvs_ref[0])

    gt = jnp.broadcast_to(gate_ref[0], (LANES, LANES)).T
    nh = NSA_HEADS
    o_ref[0] = (gt[0:nh, 0:dh] * oc_ref[0] + gt[nh:2 * nh, 0:dh] * o_s + gt[2 * nh:3 * nh, 0:dh] * o_w)


def _sample_attend(idx, page_table, q_s, o_c, slc_k, slc_v, win_k, win_v, new_rows, gates, n_pick):
    bsz = page_table.shape[0]
    nh, dh, g = NSA_HEADS, NSA_HEAD_DIM, NSA_KV_HEADS
    wlen = win_k.shape[-1]
    row_spec = pl.BlockSpec((1, 1, NSA_KV), lambda b, *_: (b, 0, 0))
    win_spec = pl.BlockSpec((1, g, dh, wlen), lambda b, *_: (b, 0, 0, 0))
    head_spec = pl.BlockSpec((1, nh, dh), lambda b, *_: (b, 0, 0))
    any_spec = pl.BlockSpec(memory_space=pl.ANY)
    grid_spec = pltpu.PrefetchScalarGridSpec(
        num_scalar_prefetch=2, grid=(bsz,),
        in_specs=[head_spec, head_spec, any_spec, any_spec, win_spec, win_spec,
                  row_spec, row_spec, row_spec, row_spec, pl.BlockSpec((1, 1, LANES), lambda b, *_: (b, 0, 0))],
        out_specs=head_spec,
        scratch_shapes=[pltpu.VMEM((g, n_pick, dh, PAGE_SIZE), F32), pltpu.VMEM((g, n_pick, dh, PAGE_SIZE), F32),
                        pltpu.SemaphoreType.DMA((2,))])
    return pl.pallas_call(
        functools.partial(_sample_attend_kernel, n_pick=n_pick),
        grid_spec=grid_spec, out_shape=SDS((bsz, nh, dh), F32),
        compiler_params=_params("arbitrary"),
    )(idx, page_table, q_s, o_c, slc_k, slc_v, win_k, win_v, *new_rows, gates)


def _gla_sample_kernel(q_ref, k_ref, v_ref, r_ref, misc_ref, wa_ref, ba_ref, ng_ref, s0_ref, o_ref, s_ref):
    dk, dv = GLA_DK, GLA_DV
    lr = jnp.broadcast_to(misc_ref[0][:, N_GATE:N_GATE + GLA_RANK], (16, GLA_RANK)).astype(BF16)
    x = jnp.dot(lr, wa_ref[...], preferred_element_type=F32)[0:1] + ba_ref[...]
    g_all = jax.nn.log_sigmoid(x) / GLA_TAU

    def col(v):
        t = jnp.broadcast_to(v, (dk, dk)).T
        return jnp.concatenate([t] * (dv // dk), axis=1)

    outs = []
    for h in range(GLA_HEADS):
        g = g_all[:, h * dk:(h + 1) * dk]
        q = q_ref[0][:, h * dk:(h + 1) * dk] * (GLA_DK ** -0.5)
        k = k_ref[0][:, h * dk:(h + 1) * dk]
        v = v_ref[0][:, h * dv:(h + 1) * dv]
        s0 = s0_ref[0, h]
        q_t = q * jnp.exp(g)
        k_t = k * jnp.exp(-g)
        a = jnp.sum(q_t * k_t, axis=-1, keepdims=True)
        o = jnp.sum(col(q_t) * s0, axis=0, keepdims=True) + a * v
        s_ref[0, h] = col(jnp.exp(g)) * s0 + col(k) * v
        outs.append(_rms(o, ng_ref[...]) * jax.nn.silu(r_ref[0][:, h * dv:(h + 1) * dv]))
    o_ref[0] = jnp.concatenate(outs, axis=-1)


def _gla_sample(q_l, k_l, v_l, r_l, misc, wa, ba, ng, s0):
    bsz, h, dk, dv = s0.shape

    def row(n):
        return pl.BlockSpec((1, 1, n), lambda b: (b, 0, 0))

    st_spec = pl.BlockSpec((1, h, dk, dv), lambda b: (b, 0, 0, 0))
    return pl.pallas_call(
        _gla_sample_kernel, grid=(bsz,),
        in_specs=[row(h * dk), row(h * dk), row(h * dv), row(h * dv), row(LANES),
                  _resident(wa.shape), _resident(ba.shape), _resident(ng.shape), st_spec],
        out_specs=[row(h * dv), st_spec],
        out_shape=[SDS((bsz, 1, h * dv), F32), SDS((bsz, h, dk, dv), F32)],
        compiler_params=_params("parallel"),
    )(q_l, k_l, v_l, r_l, misc, wa, ba, ng, s0)


SAMPLE_ROWS = 16


def _layer_sample(x, caches, wins, s0, page_table, w):
    bsz, t, d = x.shape
    n_pages = page_table.shape[1]
    pos = n_pages * PAGE_SIZE
    assert t == 1 and bsz <= SAMPLE_ROWS and pos % SLC_BLOCK == 0 and wins[0].shape[1] == WINDOW
    assert pos // SLC_BLOCK >= N_SELECT
    mp = SAMPLE_ROWS
    x2 = jnp.pad(x.reshape(bsz, d), ((0, mp - bsz), (0, 0)))
    zb = jnp.zeros((1, LANES), F32)
    spec_a = [('bf16', 0, NSA_Q, NSA_HEAD_DIM ** -0.5)]
    spec_a += [('f32', NSA_Q + j * NSA_KV, NSA_Q + (j + 1) * NSA_KV, 1.0) for j in range(6)]
    q_s, r0, r1, r2, r3, r4, r5 = _proj(x2, w['norm1'], w['w_a'], zb, spec_a, mp)
    spec_b = [('f32', 0, _GQ, 1.0), ('f32', _GQ, 2 * _GQ, 1.0), ('f32', 2 * _GQ, 2 * _GQ + _GV, 1.0),
              ('f32', 2 * _GQ + _GV, 2 * _GQ + 2 * _GV, 1.0),
              ('misc', 2 * _GQ + 2 * _GV, 2 * _GQ + 2 * _GV + LANES, 1.0)]
    q_l, k_l, v_l, r_l, misc = _proj(x2, w['norm1'], w['w_b'], w['b_misc'], spec_b, mp)

    cache_t = [a.transpose(0, 2, 3, 1) for a in caches]
    win_t = [a.transpose(0, 2, 3, 1) for a in wins]
    n_pool = caches[0].shape[0]
    cmp_pages = [a.reshape(n_pool, NSA_KV, PAGE_SIZE) for a in cache_t[:2]]
    kcmp, vcmp = _compress(page_table, cmp_pages[0], cmp_pages[1], w['cmp'], paged=True)
    q_h = q_s.reshape(mp, NSA_HEADS, NSA_HEAD_DIM)
    n_pick = N_SELECT - 1
    o_c, idx = _sample_select(q_h, kcmp, vcmp, pos, n_pick)
    idx = idx[:, ::NSA_HPG, :n_pick].reshape(bsz * NSA_KV_HEADS, n_pick)
    gates = misc[:, :N_GATE].reshape(mp, NSA_KV_HEADS, 3, NSA_HPG).transpose(0, 2, 1, 3).reshape(mp, 1, N_GATE)
    gates = jnp.pad(gates, ((0, 0), (0, 0), (0, LANES - N_GATE)))
    new_rows = [a.reshape(mp, 1, NSA_KV) for a in (r2, r3, r4, r5)]
    o_a = _sample_attend(idx, page_table, q_h, o_c, cache_t[2], cache_t[3], win_t[0], win_t[1], new_rows, gates,
                         n_pick)
    o_a = jnp.pad(o_a.reshape(bsz, NSA_Q), ((0, mp - bsz), (0, 0))).astype(BF16)

    def r3d(a):
        return a.reshape(mp, 1, a.shape[-1])

    o_b, s_new = _gla_sample(r3d(q_l), r3d(k_l), r3d(v_l), r3d(r_l), r3d(misc), w['gla_wa'], w['gla_ba'],
                             w['gla_ng'], s0)
    o_b = jnp.pad(o_b.reshape(bsz, _GV), ((0, mp - bsz), (0, 0))).astype(BF16)
    x1 = _merge(o_a, o_b, x2, w['norm1'], w['w_gate'], w['w_br_a'], w['w_br_b'], w['w_o'], mp)
    y = _mlp(x1, w['norm2'], w['w_up'], w['w_down'], w['norm_f'], mp, MLP_TF)
    kvh = (bsz, 1, NSA_KV_HEADS, NSA_HEAD_DIM)
    rows = [a[:bsz].reshape(kvh) for a in (r0, r1, r2, r3)]
    new_wins = [jnp.concatenate([c[:, 1:], a[:bsz].reshape(kvh)], axis=1) for c, a in zip(wins, (r4, r5))]
    return y[:bsz].reshape(bsz, 1, d), rows, new_wins, s_new


def kernel(x_prompt, x_sample, cache_cmp_k, cache_cmp_v, cache_slc_k, cache_slc_v, cache_win_k, cache_win_v, state_gla, page_table, norm1_g, w_in, b_nsa_gate, cmp_pe_k, cmp_pe_v, cmp_k_w1, cmp_k_w2, cmp_v_w1, cmp_v_w2, gla_w_a2, gla_b_a, gla_norm_g, w_br_a, w_br_b, w_o, norm2_g, w_up, w_down, norm_f):
    assert DEPTH == 1 and norm1_g.shape[0] == 1
    w = _prep_weights(norm1_g[0], w_in[0], b_nsa_gate[0], cmp_pe_k[0], cmp_pe_v[0], cmp_k_w1[0], cmp_k_w2[0],
                      cmp_v_w1[0], cmp_v_w2[0], gla_w_a2[0], gla_b_a[0], gla_norm_g[0], w_br_a[0], w_br_b[0],
                      w_o[0], norm2_g[0], w_up[0], w_down[0], norm_f)
    y_p, rows_p, wins_p, s_p = _layer_prompt(x_prompt, w)
    caches = [c[0] for c in (cache_cmp_k, cache_cmp_v, cache_slc_k, cache_slc_v)]
    y_s, rows_s, wins_s, s_s = _layer_sample(x_sample, caches, [cache_win_k[0], cache_win_v[0]], state_gla[0],
                                             page_table, w)
    outs_p = [a[None] for a in rows_p + wins_p + [s_p]]
    outs_s = [a[None] for a in rows_s + wins_s + [s_s]]
    return (y_p, y_s, *outs_p, *outs_s)
```

```python
import functools

import jax
import jax.numpy as jnp
from jax import lax
from jax.experimental import pallas as pl
from jax.experimental.pallas import tpu as pltpu

D_MODEL = 2048
DEPTH = 1
PAGE_SIZE = 128
NSA_HEADS = 16
NSA_KV_HEADS = 4
NSA_HPG = NSA_HEADS // NSA_KV_HEADS
NSA_HEAD_DIM = 64
NSA_Q = NSA_HEADS * NSA_HEAD_DIM
NSA_KV = NSA_KV_HEADS * NSA_HEAD_DIM
CMP_STRIDE = 16
CMP_BLOCK = 32
CMP_HIDDEN = 128
SLC_BLOCK = 64
N_SELECT = 16
WINDOW = 512
GLA_HEADS = 4
GLA_DK = (D_MODEL // 4) // GLA_HEADS
GLA_DV = (D_MODEL // 2) // GLA_HEADS
GLA_RANK = 16
GLA_TAU = 16.0
D_FF = 4 * D_MODEL
EPS = 1e-6
NEG = -1e30
FORCE = 1e4
SPLITS = (NSA_Q, 6 * NSA_KV, 3 * NSA_HEADS,
          GLA_HEADS * GLA_DK, GLA_HEADS * GLA_DK, GLA_HEADS * GLA_DV, GLA_HEADS * GLA_DV,
          GLA_RANK, 2 * D_MODEL)

F32 = jnp.float32
BF16 = jnp.bfloat16
LANES = 128
VMEM_LIMIT_BYTES = 56 * 1024 * 1024
N_GATE = 3 * NSA_HEADS
GLA_CHUNK = 32
GLA_UNROLL = 8
NSA_TQ = 256
NSA_TK = 256
VT_ROWS = NSA_HEAD_DIM + 16
CMP_PAGES = 16
PROJ_TM = 512
MERGE_TM = 256
MLP_TM = 1024
MLP_TF = 512
SDS = jax.ShapeDtypeStruct


def _params(*sem):
    return pltpu.CompilerParams(dimension_semantics=sem, vmem_limit_bytes=VMEM_LIMIT_BYTES)


def _resident(shape):
    nd = len(shape)
    return pl.BlockSpec(shape, lambda *_: (0,) * nd, pipeline_mode=pl.Buffered(1))


def _rms(x, g):
    return x * lax.rsqrt(jnp.mean(x * x, axis=-1, keepdims=True) + EPS) * g


def _proj_kernel(x_ref, g_ref, w_ref, b_ref, *out_refs, spec):
    h = _rms(x_ref[...], g_ref[...]).astype(BF16)
    misc = None
    for o_ref, (kind, c0, c1, scale) in zip(out_refs, spec):
        if kind == 'gates':
            for g in range(NSA_KV_HEADS):
                o_ref[g] = misc if g == 0 else pltpu.roll(misc, LANES - g * 3 * NSA_HPG, axis=1)
            continue
        r = jnp.dot(h, w_ref[:, c0:c1], preferred_element_type=F32)
        if scale != 1.0:
            r = r * scale
        if kind == 'f32':
            o_ref[...] = r
        elif kind == 'f32t':
            o_ref[0] = r.T
        elif kind == 'bf16':
            o_ref[...] = r.astype(BF16)
        elif kind == 'sigmoid':
            o_ref[...] = jax.nn.sigmoid(r)
        elif kind == 'hm':
            for i in range((c1 - c0) // NSA_HEAD_DIM):
                o_ref[i] = r[:, i * NSA_HEAD_DIM:(i + 1) * NSA_HEAD_DIM].astype(BF16)
        elif kind == 'hmt':
            rt = r.T
            for i in range((c1 - c0) // NSA_HEAD_DIM):
                o_ref[i] = rt[i * NSA_HEAD_DIM:(i + 1) * NSA_HEAD_DIM].astype(BF16)
        elif kind == 'hmvt':
            rt = r.T
            sub = lax.broadcasted_iota(jnp.int32, (VT_ROWS - NSA_HEAD_DIM, r.shape[0]), 0)
            ones = jnp.where(sub == 0, 1.0, 0.0)
            for i in range((c1 - c0) // NSA_HEAD_DIM):
                piece = rt[i * NSA_HEAD_DIM:(i + 1) * NSA_HEAD_DIM]
                o_ref[i] = jnp.concatenate([piece, ones], axis=0).astype(BF16)
        elif kind == 'misc':
            lane = lax.broadcasted_iota(jnp.int32, r.shape, 1)
            misc = jnp.where(lane < N_GATE, jax.nn.sigmoid(r + b_ref[...]), r)
            o_ref[...] = misc


def _proj(x, norm_g, w, bias, spec, tm, seq_len=None):
    m, d = x.shape
    n = w.shape[1]
    assert m % tm == 0
    out_shape, out_specs = [], []
    for kind, c0, c1, _ in spec:
        if kind == 'f32t':
            assert seq_len % tm == 0
            per_seq = seq_len // tm
            out_shape.append(SDS((m // seq_len, c1 - c0, seq_len), F32))
            out_specs.append(pl.BlockSpec((1, c1 - c0, tm), lambda i: (i // per_seq, 0, i % per_seq)))
        elif kind == 'hm':
            nh = (c1 - c0) // NSA_HEAD_DIM
            out_shape.append(SDS((nh, m, NSA_HEAD_DIM), BF16))
            out_specs.append(pl.BlockSpec((nh, tm, NSA_HEAD_DIM), lambda i: (0, i, 0)))
        elif kind in ('hmt', 'hmvt'):
            nh = (c1 - c0) // NSA_HEAD_DIM
            nrow = NSA_HEAD_DIM if kind == 'hmt' else VT_ROWS
            out_shape.append(SDS((nh, nrow, m), BF16))
            out_specs.append(pl.BlockSpec((nh, nrow, tm), lambda i: (0, 0, i)))
        elif kind == 'gates':
            out_shape.append(SDS((NSA_KV_HEADS, m, LANES), F32))
            out_specs.append(pl.BlockSpec((NSA_KV_HEADS, tm, LANES), lambda i: (0, i, 0)))
        else:
            out_shape.append(SDS((m, c1 - c0), BF16 if kind == 'bf16' else F32))
            out_specs.append(pl.BlockSpec((tm, c1 - c0), lambda i: (i, 0)))
    return pl.pallas_call(
        functools.partial(_proj_kernel, spec=tuple(spec)),
        grid=(m // tm,),
        in_specs=[pl.BlockSpec((tm, d), lambda i: (i, 0)), _resident((1, d)), _resident((d, n)),
                  _resident((1, LANES))],
        out_specs=out_specs, out_shape=out_shape,
        compiler_params=_params("parallel"),
    )(x, norm_g, w, bias)


def _cmp_kernel(pt_ref, k_hbm, v_hbm, perm_ref, pek_ref, pev_ref, w1k_ref, w1v_ref, w2k_ref, w2v_ref,
                ok_ref, ov_ref, kbuf, vbuf, hk, hv, sem, *, n_pages_step, paged):
    b = pl.program_id(0)
    s = pl.program_id(1)
    ns = pl.num_programs(1)
    t = b * ns + s
    total = pl.num_programs(0) * ns
    rows = n_pages_step * 8
    dh, hid_n = NSA_HEAD_DIM, CMP_HIDDEN

    def copies(tt, slot):
        bb = tt // ns
        ss = tt % ns
        out = []
        for p in range(n_pages_step):
            page = pt_ref[bb, ss * n_pages_step + p]
            if paged:
                src = [hbm.at[page] for hbm in (k_hbm, v_hbm)]
            else:
                tok = pl.ds(pl.multiple_of(page * PAGE_SIZE, PAGE_SIZE), PAGE_SIZE)
                src = [hbm.at[bb, :, tok] for hbm in (k_hbm, v_hbm)]
            out.append(pltpu.make_async_copy(src[0], kbuf.at[slot, p], sem.at[slot, 0]))
            out.append(pltpu.make_async_copy(src[1], vbuf.at[slot, p], sem.at[slot, 1]))
        return out

    slot = t % 2

    @pl.when(t == 0)
    def _():
        for c in copies(t, slot):
            c.start()

    @pl.when(t + 1 < total)
    def _():
        for c in copies(t + 1, 1 - slot):
            c.start()

    for c in copies(t, slot):
        c.wait()

    r0 = pl.multiple_of(s * rows, rows)
    low = lax.broadcasted_iota(jnp.int32, (8, LANES), 1) < dh
    tok_pairs = CMP_STRIDE // 2
    nt = (((1,), (1,)), ((), ()))
    perm = perm_ref[...]
    for buf, w1_ref, h_ref in ((kbuf, w1k_ref, hk), (vbuf, w1v_ref, hv)):
        parts = [[[] for _ in range(tok_pairs)] for _ in range(NSA_KV_HEADS)]
        for p in range(n_pages_step):
            r = lax.dot_general(perm, buf[slot, p].astype(BF16), nt, preferred_element_type=F32)
            for q in range(2):
                for m in range(tok_pairs):
                    x0 = r[2 * m * 8:2 * m * 8 + 8, q * LANES:(q + 1) * LANES]
                    x1 = r[(2 * m + 1) * 8:(2 * m + 1) * 8 + 8, q * LANES:(q + 1) * LANES]
                    parts[2 * q][m].append(jnp.where(low, x0, pltpu.roll(x1, dh, axis=1)))
                    parts[2 * q + 1][m].append(jnp.where(low, pltpu.roll(x0, dh, axis=1), x1))
        for g in range(NSA_KV_HEADS):
            z = jnp.concatenate([jnp.concatenate(parts[g][m], axis=0) for m in range(tok_pairs)],
                                axis=1).astype(BF16)
            h_ref[pl.ds(r0, rows), g * 2 * hid_n:(g + 1) * 2 * hid_n] = jnp.dot(
                z, w1_ref[...], preferred_element_type=F32)

    @pl.when(s == ns - 1)
    def _():
        for h_ref, pe_ref, w1_ref, w2_ref, o_ref in ((hk, pek_ref, w1k_ref, w2k_ref, ok_ref),
                                                     (hv, pev_ref, w1v_ref, w2v_ref, ov_ref)):
            n_chunk = h_ref.shape[0]
            pe = jnp.broadcast_to(pe_ref[...], (2, 16, pe_ref.shape[2])).astype(BF16)
            pe_term = (jnp.dot(pe[0], w1_ref[:, 0:hid_n], preferred_element_type=F32)
                       + jnp.dot(pe[1], w1_ref[:, hid_n:2 * hid_n], preferred_element_type=F32))[0:1]
            hid = []
            for g in range(NSA_KV_HEADS):
                first = h_ref[:, g * 2 * hid_n:g * 2 * hid_n + hid_n]
                last = h_ref[:, g * 2 * hid_n + hid_n:(g + 1) * 2 * hid_n]
                hid.append(first + pltpu.roll(last, n_chunk - 1, axis=0) + pe_term)
            hid = jnp.concatenate(hid, axis=1)
            res = jnp.dot(jax.nn.gelu(hid).astype(BF16), w2_ref[...], preferred_element_type=F32)
            if o_ref is ok_ref:
                for g in range(NSA_KV_HEADS):
                    o_ref[0, g] = res[:, g * NSA_HEAD_DIM:(g + 1) * NSA_HEAD_DIM].astype(BF16)
            else:
                res_t = res.T
                for g in range(NSA_KV_HEADS):
                    o_ref[0, g] = res_t[g * NSA_HEAD_DIM:(g + 1) * NSA_HEAD_DIM].astype(BF16)


def _cmp_weights(pe, w1, w2):
    g = NSA_KV_HEADS
    half = CMP_STRIDE * NSA_HEAD_DIM
    pe_t = pe.reshape(2, 1, half)
    w1ab = jnp.concatenate([w1[:half], w1[half:]], axis=1)
    w2b = jnp.einsum('jd,gh->gjhd', w2, jnp.eye(g, dtype=F32)).reshape(g * CMP_HIDDEN, g * NSA_HEAD_DIM)
    return pe_t, w1ab.astype(BF16), w2b.astype(BF16)


def _compress(page_table, k_src, v_src, cw, paged):
    bsz, n_pages = page_table.shape
    p_step = min(CMP_PAGES, n_pages)
    ns = n_pages // p_step
    n_chunk = n_pages * 8
    page_shape = (NSA_KV, PAGE_SIZE)
    assert NSA_KV == 2 * LANES
    assert k_src.shape[1:] == (page_shape if paged else (NSA_KV, n_pages * PAGE_SIZE))
    half = CMP_STRIDE * NSA_HEAD_DIM
    gh = NSA_KV_HEADS * CMP_HIDDEN
    out_row = jnp.arange(PAGE_SIZE, dtype=jnp.int32)
    src_tok = (out_row % 8) * CMP_STRIDE + out_row // 8
    perm = (src_tok[:, None] == jnp.arange(PAGE_SIZE, dtype=jnp.int32)[None, :]).astype(BF16)
    k_sds =SDS((bsz, NSA_KV_HEADS, n_chunk, NSA_HEAD_DIM), BF16)
    v_sds = SDS((bsz, NSA_KV_HEADS, NSA_HEAD_DIM, n_chunk), BF16)
    k_spec = pl.BlockSpec((1, NSA_KV_HEADS, n_chunk, NSA_HEAD_DIM), lambda b, s, pt: (b, 0, 0, 0))
    v_spec = pl.BlockSpec((1, NSA_KV_HEADS, NSA_HEAD_DIM, n_chunk), lambda b, s, pt: (b, 0, 0, 0))
    grid_spec = pltpu.PrefetchScalarGridSpec(
        num_scalar_prefetch=1, grid=(bsz, ns),
        in_specs=[pl.BlockSpec(memory_space=pl.ANY), pl.BlockSpec(memory_space=pl.ANY),
                  _resident((PAGE_SIZE, PAGE_SIZE)), _resident((2, 1, half)), _resident((2, 1, half)),
                  _resident((half, 2 * CMP_HIDDEN)), _resident((half, 2 * CMP_HIDDEN)),
                  _resident((gh, NSA_KV)), _resident((gh, NSA_KV))],
        out_specs=[k_spec, v_spec],
        scratch_shapes=[pltpu.VMEM((2, p_step) + page_shape, F32), pltpu.VMEM((2, p_step) + page_shape, F32),
                        pltpu.VMEM((n_chunk, 2 * gh), F32), pltpu.VMEM((n_chunk, 2 * gh), F32),
                        pltpu.SemaphoreType.DMA((2, 2))])
    return pl.pallas_call(
        functools.partial(_cmp_kernel, n_pages_step=p_step, paged=paged),
        grid_spec=grid_spec, out_shape=[k_sds, v_sds],
        compiler_params=_params("arbitrary", "arbitrary"),
    )(page_table, k_src, v_src, perm, cw['pe_k'], cw['pe_v'], cw['w1_k'], cw['w1_v'], cw['w2_k'], cw['w2_v'])


def _softmax_rows(s, mask):
    sm = jnp.where(mask, s, NEG)
    m = jnp.max(sm, axis=-1, keepdims=True)
    e = jnp.where(mask, jnp.exp(sm - m), 0.0)
    return e, jnp.sum(e, axis=-1, keepdims=True)


def _safe_inv(l):
    return jnp.where(l > 0.0, 1.0 / jnp.where(l > 0.0, l, 1.0), 0.0)


M_INIT = -1e20
LOG2_E = 1.4426950408889634


def _nsa_prompt_kernel(qt_ref, kc_ref, vct_ref, ks_ref, vst_ref, kw_ref, vwt_ref, gate_ref,
                       o_ref, *scratch, n_cmp, n_slc, n_sel):
    tq, tk, hpg, dh = NSA_TQ, NSA_TK, NSA_HPG, NSA_HEAD_DIM
    rk_refs = scratch[:tq // LANES]
    selb_ref, m_ref, acc_ref = scratch[tq // LANES:tq // LANES + 3]
    s_refs, p_refs, a_refs = (scratch[tq // LANES + 3 + 2 * i:tq // LANES + 5 + 2 * i] for i in range(3))
    qi = pl.program_id(2)
    q0 = qi * tq
    n_chunk = kc_ref.shape[2]
    r = SLC_BLOCK // CMP_STRIDE
    n_row = n_chunk // r

    ci = lax.broadcasted_iota(jnp.int32, (n_chunk, tq), 0)
    pos_c = q0 + lax.broadcasted_iota(jnp.int32, (n_chunk, tq), 1)
    m_c = (ci < n_cmp) & (ci * CMP_STRIDE + (CMP_BLOCK - 1) <= pos_c)
    kc = kc_ref[0, 0]
    vct = vct_ref[0, 0]
    o_c = []
    psum = None
    for h in range(hpg):
        s = jnp.where(m_c, jnp.dot(kc, qt_ref[0, h], preferred_element_type=F32), NEG)
        e = jnp.where(m_c, jnp.exp2(s - jnp.max(s, axis=0, keepdims=True)), 0.0)
        p = e * _safe_inv(jnp.sum(e, axis=0, keepdims=True))
        o_c.append(jnp.dot(vct, p.astype(BF16), preferred_element_type=F32))
        psum = p if psum is None else psum + p

    imp = pltpu.roll(psum, 1, axis=0) + psum
    for o in range(1, r):
        imp = imp + pltpu.roll(psum, n_chunk - o, axis=0)
    parts = []
    for i, rk_ref in enumerate(rk_refs):
        rk_ref[...] = imp[:, i * LANES:(i + 1) * LANES]
        parts.append(rk_ref[pl.ds(0, n_row, stride=r), :])
    imp_b = jnp.concatenate(parts, axis=1)
    j_io = lax.broadcasted_iota(jnp.int32, (n_row, tq), 0)
    pos_b = q0 + lax.broadcasted_iota(jnp.int32, (n_row, tq), 1)
    valid = (j_io < n_slc) & (j_io * SLC_BLOCK <= pos_b)
    forced = (j_io == 0) | (j_io == pos_b // SLC_BLOCK)
    sc = jnp.where(valid, imp_b + jnp.where(forced, FORCE, 0.0), NEG)
    rank = jnp.zeros((n_row, tq), F32)
    for k in range(n_slc):
        ck = sc[k:k + 1, :]
        beats = (ck > sc) | ((ck == sc) & (j_io > k))
        rank = rank + jnp.where(beats, 1.0, 0.0)
    selb_ref[...] = jnp.where(rank < n_sel, 0.0, NEG)

    dlt = lax.broadcasted_iota(jnp.int32, (tk, tq), 0) - lax.broadcasted_iota(jnp.int32, (tk, tq), 1)
    blocks_per_tile = tk // SLC_BLOCK

    def flash(k_ref, vt_ref, lo, hi, bias_fn):
        n_tiles = k_ref.shape[2] // tk
        m_ref[...] = jnp.full(m_ref.shape, M_INIT, F32)
        acc_ref[...] = jnp.zeros(acc_ref.shape, F32)
        p_refs[1][...] = jnp.zeros(p_refs[1].shape, BF16)
        a_refs[1][...] = jnp.ones(a_refs[1].shape, F32)

        def tile_start(kb):
            return pl.multiple_of(jnp.clip(kb, 0, n_tiles - 1) * tk, tk)

        def scores(kb, s_ref):
            kk = k_ref[0, 0, pl.ds(tile_start(kb), tk), :]
            bias = bias_fn(kb)
            for h in range(hpg):
                for c in range(tk // SLC_BLOCK):
                    rs = slice(c * SLC_BLOCK, (c + 1) * SLC_BLOCK)
                    s_ref[h, rs, :] = jnp.dot(kk[rs], qt_ref[0, h], preferred_element_type=F32) + bias[rs]

        def softmax(s_ref, p_ref, a_ref):
            for h in range(hpg):
                for c in range(tq // LANES):
                    cs = slice(c * LANES, (c + 1) * LANES)
                    s = s_ref[h, :, cs]
                    m_prev = m_ref[h, :, cs]
                    m_new = jnp.maximum(m_prev, jnp.max(s, axis=0, keepdims=True))
                    p_ref[h, :, cs] = jnp.exp2(s - m_new).astype(BF16)
                    a_ref[h, :, cs] = jnp.exp2(m_prev - m_new)
                    m_ref[h, :, cs] = m_new

        def weighted_values(kb, p_ref, a_ref):
            vt = vt_ref[0, :, pl.ds(tile_start(kb), tk)]
            for h in range(hpg):
                acc_ref[h] = a_ref[h] * acc_ref[h] + jnp.dot(vt, p_ref[h], preferred_element_type=F32)

        scores(lo, s_refs[0])

        def body(j, carry):
            i = lo + 2 * j
            scores(i + 1, s_refs[1])
            softmax(s_refs[0], p_refs[0], a_refs[0])
            weighted_values(i - 1, p_refs[1], a_refs[1])
            scores(i + 2, s_refs[0])
            softmax(s_refs[1], p_refs[1], a_refs[1])
            weighted_values(i, p_refs[0], a_refs[0])
            return carry

        n_pairs = (hi - lo + 1) // 2
        lax.fori_loop(0, n_pairs, body, 0)
        weighted_values(lo + 2 * n_pairs - 1, p_refs[1], a_refs[1])
        return [acc_ref[h, 0:dh, :] * _safe_inv(acc_ref[h, dh:dh + 1, :]) for h in range(hpg)]

    def slc_bias(kb):
        first = jnp.minimum(kb, n_slc // blocks_per_tile - 1) * blocks_per_tile
        rows = [jnp.broadcast_to(selb_ref[pl.ds(first + i, 1), :], (SLC_BLOCK, tq))
                for i in range(blocks_per_tile)]
        return jnp.where(dlt + (kb * tk - q0) <= 0, jnp.concatenate(rows, axis=0), NEG)

    def win_bias(kb):
        d = dlt + (kb * tk - q0)
        return jnp.where((d <= 0) & (d >= -WINDOW), 0.0, NEG)

    hi = (q0 + tq - 1) // tk + 1
    o_w = flash(kw_ref, vwt_ref, jnp.maximum(q0 - WINDOW, 0) // tk, hi, win_bias)
    o_s = flash(ks_ref, vst_ref, 0, hi, slc_bias)

    gt = gate_ref[0].T
    outs = [gt[h:h + 1] * o_c[h] + gt[hpg + h:hpg + h + 1] * o_s[h] + gt[2 * hpg + h:2 * hpg + h + 1] * o_w[h]
            for h in range(hpg)]
    o_ref[...] = jnp.concatenate(outs, axis=0).T.astype(BF16)


def _nsa_prompt(q_t, ks_hm, vs_t, kw_hm, vw_t, kcmp, vcmp_t, gates_hm, bsz, t):
    g, hpg, dh, tq = NSA_KV_HEADS, NSA_HPG, NSA_HEAD_DIM, NSA_TQ
    r = SLC_BLOCK // CMP_STRIDE
    n_slc = t // SLC_BLOCK
    assert t % NSA_TK == 0 and t % tq == 0 and n_slc * r <= LANES and NSA_TK % SLC_BLOCK == 0
    nq = t // tq
    n_cmp = kcmp.shape[2] - 1
    if kcmp.shape[2] < LANES:
        fill = LANES - kcmp.shape[2]
        kcmp = jnp.pad(kcmp, ((0, 0), (0, 0), (0, fill), (0, 0)))
        vcmp_t = jnp.pad(vcmp_t, ((0, 0), (0, 0), (0, 0), (0, fill)))
    n_chunk = kcmp.shape[2]
    assert n_chunk == LANES
    q4 = q_t.reshape(g, hpg, dh, bsz * t)

    def k_spec():
        return pl.BlockSpec((1, 1, t, dh), lambda b, gg, qi: (gg, b, 0, 0))

    def vt_spec():
        return pl.BlockSpec((1, VT_ROWS, t), lambda b, gg, qi: (gg, 0, b))

    def per_bt(a):
        return a.reshape(g, bsz, t, a.shape[-1])

    return pl.pallas_call(
        functools.partial(_nsa_prompt_kernel, n_cmp=n_cmp, n_slc=n_slc, n_sel=min(N_SELECT, n_slc)),
        grid=(bsz, g, nq),
        in_specs=[pl.BlockSpec((1, hpg, dh, tq), lambda b, gg, qi: (gg, 0, 0, b * nq + qi)),
                  pl.BlockSpec((1, 1, n_chunk, dh), lambda b, gg, qi: (b, gg, 0, 0)),
                  pl.BlockSpec((1, 1, dh, n_chunk), lambda b, gg, qi: (b, gg, 0, 0)),
                  k_spec(), vt_spec(), k_spec(), vt_spec(),
                  pl.BlockSpec((1, tq, LANES), lambda b, gg, qi: (gg, b * nq + qi, 0))],
        out_specs=pl.BlockSpec((tq, hpg * dh), lambda b, gg, qi: (b * nq + qi, gg)),
        out_shape=SDS((bsz * t, NSA_Q), BF16),
        scratch_shapes=[pltpu.VMEM((n_chunk, LANES), F32)] * (tq // LANES) + [
                        pltpu.VMEM((n_chunk // r, tq), F32),
                        pltpu.VMEM((hpg, 1, tq), F32), pltpu.VMEM((hpg, VT_ROWS, tq), F32),
                        pltpu.VMEM((hpg, NSA_TK, tq), F32), pltpu.VMEM((hpg, NSA_TK, tq), F32),
                        pltpu.VMEM((hpg, NSA_TK, tq), BF16), pltpu.VMEM((hpg, NSA_TK, tq), BF16),
                        pltpu.VMEM((hpg, 1, tq), F32), pltpu.VMEM((hpg, 1, tq), F32)],
        compiler_params=_params("parallel", "parallel", "arbitrary"),
    )(q4, kcmp, vcmp_t, per_bt(ks_hm), vs_t, per_bt(kw_hm), vw_t, gates_hm)


def _gla_prompt_kernel(q_ref, k_ref, v_ref, r_ref, misc_ref, wa_ref, ba_ref, ng_ref,
                       o_ref, s_ref, qe_ref, qt_ref, kt_ref, kh_ref, d_ref, u_ref, st_ref):
    t = q_ref.shape[0]
    c = GLA_CHUNK
    n = t // c
    dk, dv = GLA_DK, GLA_DV
    lr = misc_ref[:, N_GATE:N_GATE + GLA_RANK].astype(BF16)
    x = jnp.dot(lr, wa_ref[...], preferred_element_type=F32) + ba_ref[...]
    g = jax.nn.log_sigmoid(x) / GLA_TAU
    row = lax.broadcasted_iota(jnp.int32, (t, dk), 0) % c
    b = g
    sh = 1
    while sh < c:
        b = b + jnp.where(row >= sh, pltpu.roll(b, sh, axis=0), 0.0)
        sh *= 2
    b3 = b.reshape(n, c, dk)
    b_last = jnp.broadcast_to(b3[:, c - 1:c, :], (n, c, dk)).reshape(t, dk)
    b_mid = jnp.broadcast_to(b3[:, c // 2 - 1:c // 2, :], (n, c, dk)).reshape(t, dk)
    q = q_ref[...] * (GLA_DK ** -0.5)
    k = k_ref[...]
    qe_ref[...] = (q * jnp.exp(b)).astype(BF16)
    qt_ref[...] = (q * jnp.exp(b - b_mid)).astype(BF16)
    kt_ref[...] = (k * jnp.exp(b_mid - b)).astype(BF16)
    kh_ref[...] = (k * jnp.exp(b_last - b)).astype(BF16)
    d_ref[...] = jnp.exp(b_last)
    tn = (((0,), (0,)), ((), ()))
    nt = (((1,), (1,)), ((), ()))

    def chunk_update(i, carry):
        r0 = pl.multiple_of(i * c, c)
        u_ref[i] = lax.dot_general(v_ref[pl.ds(r0, c), :], kh_ref[pl.ds(r0, c), :], tn,
                                   preferred_element_type=F32)
        return carry

    lax.fori_loop(0, n, chunk_update, 0, unroll=GLA_UNROLL)

    st_ref[...] = jnp.zeros((dv, dk), F32)

    def recur(i, carry):
        st = st_ref[...]
        st_ref[...] = st * d_ref[pl.ds(pl.multiple_of(i * c, c), 1), :] + u_ref[i]
        u_ref[i] = st
        return carry

    lax.fori_loop(0, n, recur, 0)
    s_ref[0, 0] = st_ref[...].T

    tril = lax.broadcasted_iota(jnp.int32, (c, c), 0) >= lax.broadcasted_iota(jnp.int32, (c, c), 1)
    ng = ng_ref[...]

    def chunk_out(i, carry):
        r0 = pl.multiple_of(i * c, c)
        vv = v_ref[pl.ds(r0, c), :]
        o = lax.dot_general(qe_ref[pl.ds(r0, c), :], u_ref[i].astype(BF16), nt, preferred_element_type=F32)
        a = lax.dot_general(qt_ref[pl.ds(r0, c), :], kt_ref[pl.ds(r0, c), :], nt, preferred_element_type=F32)
        a = jnp.where(tril, a, 0.0).astype(BF16)
        o = o + jnp.dot(a, vv, preferred_element_type=F32)
        o = _rms(o, ng) * jax.nn.silu(r_ref[pl.ds(r0, c), :])
        o_ref[pl.ds(r0, c), :] = o.astype(BF16)
        return carry

    lax.fori_loop(0, n, chunk_out, 0, unroll=GLA_UNROLL)


def _gla_prompt(q_l, k_l, v_l, r_l, misc, wa, ba, ng, bsz, t):
    h, dk, dv = GLA_HEADS, GLA_DK, GLA_DV
    assert t % GLA_CHUNK == 0
    n = t // GLA_CHUNK
    return pl.pallas_call(
        _gla_prompt_kernel,
        grid=(bsz, h),
        in_specs=[pl.BlockSpec((t, dk), lambda b, hh: (b, hh)), pl.BlockSpec((t, dk), lambda b, hh: (b, hh)),
                  pl.BlockSpec((t, dv), lambda b, hh: (b, hh)), pl.BlockSpec((t, dv), lambda b, hh: (b, hh)),
                  pl.BlockSpec((t, LANES), lambda b, hh: (b, 0)),
                  pl.BlockSpec((GLA_RANK, dk), lambda b, hh: (0, hh)), pl.BlockSpec((1, dk), lambda b, hh: (0, hh)),
                  _resident((1, dv))],
        out_specs=[pl.BlockSpec((t, dv), lambda b, hh: (b, hh)),
                   pl.BlockSpec((1, 1, dk, dv), lambda b, hh: (b, hh, 0, 0))],
        out_shape=[SDS((bsz * t, h * dv), BF16), SDS((bsz, h, dk, dv), F32)],
        scratch_shapes=[pltpu.VMEM((t, dk), BF16), pltpu.VMEM((t, dk), BF16), pltpu.VMEM((t, dk), BF16),
                        pltpu.VMEM((t, dk), BF16), pltpu.VMEM((t, dk), F32),
                        pltpu.VMEM((n, dv, dk), F32), pltpu.VMEM((dv, dk), F32)],
        compiler_params=_params("parallel", "parallel"),
    )(q_l, k_l, v_l, r_l, misc, wa, ba, ng)


def _merge_kernel(oa_ref, ob_ref, x_ref, g_ref, wg_ref, wa_ref, wb_ref, wo_ref, x1_ref):
    x = x_ref[...]
    h = _rms(x, g_ref[...]).astype(BF16)
    d = D_MODEL
    ga = jax.nn.sigmoid(jnp.dot(h, wg_ref[:, 0:d], preferred_element_type=F32))
    u = ga * jnp.dot(oa_ref[...], wa_ref[...], preferred_element_type=F32)
    gb = jax.nn.sigmoid(jnp.dot(h, wg_ref[:, d:2 * d], preferred_element_type=F32))
    u = u + gb * jnp.dot(ob_ref[...], wb_ref[...], preferred_element_type=F32)
    x1_ref[...] = x + jnp.dot(u.astype(BF16), wo_ref[...], preferred_element_type=F32)


def _merge(o_a, o_b, x, norm_g, w_gate, w_a, w_b, w_o, tm):
    m, d = x.shape
    assert m % tm == 0
    return pl.pallas_call(
        _merge_kernel, grid=(m // tm,),
        in_specs=[pl.BlockSpec((tm, NSA_Q), lambda i: (i, 0)), pl.BlockSpec((tm, GLA_HEADS * GLA_DV), lambda i: (i, 0)),
                  pl.BlockSpec((tm, d), lambda i: (i, 0)), _resident((1, d)), _resident((d, 2 * d)),
                  _resident(w_a.shape), _resident(w_b.shape), _resident(w_o.shape)],
        out_specs=pl.BlockSpec((tm, d), lambda i: (i, 0)),
        out_shape=SDS((m, d), F32),
        compiler_params=_params("parallel"),
    )(o_a, o_b, x, norm_g, w_gate, w_a, w_b, w_o)


def _mlp_kernel(x1_ref, g2_ref, wu_ref, wd_ref, gf_ref, y_ref, h_ref):
    j = pl.program_id(1)

    @pl.when(j == 0)
    def _():
        x1 = x1_ref[...]
        h_ref[...] = _rms(x1, g2_ref[...]).astype(BF16)
        y_ref[...] = x1

    up = jnp.maximum(jnp.dot(h_ref[...], wu_ref[...].astype(BF16), preferred_element_type=F32), 0.0)
    y_ref[...] += jnp.dot((up * up).astype(BF16), wd_ref[...].astype(BF16), preferred_element_type=F32)

    @pl.when(j == pl.num_programs(1) - 1)
    def _():
        y_ref[...] = _rms(y_ref[...], gf_ref[...])


def _mlp(x1, g2, w_up, w_down, gf, tm, tf):
    m, d = x1.shape
    ff = w_up.shape[1]
    assert m % tm == 0 and ff % tf == 0
    return pl.pallas_call(
        _mlp_kernel, grid=(m // tm, ff // tf),
        in_specs=[pl.BlockSpec((tm, d), lambda i, j: (i, 0)), _resident((1, d)),
                  pl.BlockSpec((d, tf), lambda i, j: (0, j)), pl.BlockSpec((tf, d), lambda i, j: (j, 0)),
                  _resident((1, d))],
        out_specs=pl.BlockSpec((tm, d), lambda i, j: (i, 0)),
        out_shape=SDS((m, d), F32),
        scratch_shapes=[pltpu.VMEM((tm, d), BF16)],
        compiler_params=_params("parallel", "arbitrary"),
    )(x1, g2, w_up, w_down, gf)


def _prep_weights(norm1_g, w_in, b_nsa_gate, cmp_pe_k, cmp_pe_v, cmp_k_w1, cmp_k_w2, cmp_v_w1, cmp_v_w2,
                  gla_w_a2, gla_b_a, gla_norm_g, w_br_a, w_br_b, w_o, norm2_g, w_up, w_down, norm_f):
    pts = [0]
    for s in SPLITS:
        pts.append(pts[-1] + s)
    c_q, c_kv, c_g, c_ql, c_kl, c_vl, c_rl, c_lr, c_br, c_end = pts
    gcols = jnp.asarray([c_g + (g * NSA_HPG + h) * 3 + c for g in range(NSA_KV_HEADS)
                         for c in range(3) for h in range(NSA_HPG)], jnp.int32)
    pad = jnp.zeros((D_MODEL, LANES - N_GATE - GLA_RANK), F32)
    w_misc = jnp.concatenate([w_in[:, gcols], w_in[:, c_lr:c_br], pad], axis=1)
    b_misc = jnp.concatenate([b_nsa_gate[gcols - c_g], jnp.zeros((LANES - N_GATE,), F32)])[None, :]
    w = dict(
        norm1=norm1_g[None, :], norm2=norm2_g[None, :], norm_f=norm_f[None, :],
        w_a=w_in[:, c_q:c_g].astype(BF16),
        w_b=jnp.concatenate([w_in[:, c_ql:c_lr], w_misc], axis=1).astype(BF16),
        w_gate=w_in[:, c_br:c_end].astype(BF16),
        b_misc=b_misc,
        gla_wa=gla_w_a2.astype(BF16), gla_ba=gla_b_a[None, :], gla_ng=gla_norm_g[None, :],
        w_br_a=w_br_a.astype(BF16), w_br_b=w_br_b.astype(BF16), w_o=w_o.astype(BF16),
        w_up=w_up, w_down=w_down,
    )
    cw = {}
    cw['pe_k'], cw['w1_k'], cw['w2_k'] = _cmp_weights(cmp_pe_k, cmp_k_w1, cmp_k_w2)
    cw['pe_v'], cw['w1_v'], cw['w2_v'] = _cmp_weights(cmp_pe_v, cmp_v_w1, cmp_v_w2)
    w['cmp'] = cw
    return w


_KV6 = 6 * NSA_KV
_GQ = GLA_HEADS * GLA_DK
_GV = GLA_HEADS * GLA_DV


def _layer_prompt(x, w):
    bsz, t, d = x.shape
    m = bsz * t
    x2 = x.reshape(m, d)
    zb = jnp.zeros((1, LANES), F32)
    spec_a = [('hmt', 0, NSA_Q, NSA_HEAD_DIM ** -0.5 * LOG2_E)]
    spec_a += [('f32t', NSA_Q + j * NSA_KV, NSA_Q + (j + 1) * NSA_KV, 1.0) for j in range(6)]
    spec_a += [('hm' if j % 2 == 0 else 'hmvt', NSA_Q + j * NSA_KV, NSA_Q + (j + 1) * NSA_KV, 1.0)
               for j in range(2, 6)]
    q_t, r0, r1, r2, r3, r4, r5, ks_hm, vs_t, kw_hm, vw_t = _proj(x2, w['norm1'], w['w_a'], zb, spec_a, PROJ_TM,
                                                                   seq_len=t)
    spec_b = [('f32', 0, _GQ, 1.0), ('f32', _GQ, 2 * _GQ, 1.0), ('bf16', 2 * _GQ, 2 * _GQ + _GV, 1.0),
              ('f32', 2 * _GQ + _GV, 2 * _GQ + 2 * _GV, 1.0),
              ('misc', 2 * _GQ + 2 * _GV, 2 * _GQ + 2 * _GV + LANES, 1.0), ('gates', 0, 0, 1.0)]
    q_l, k_l, v_l, r_l, misc, gates_hm = _proj(x2, w['norm1'], w['w_b'], w['b_misc'], spec_b, PROJ_TM)

    n_pages = t // PAGE_SIZE
    ident = jnp.broadcast_to(jnp.arange(n_pages, dtype=jnp.int32), (bsz, n_pages))
    kcmp, vcmp = _compress(ident, r0, r1, w['cmp'], paged=False)
    o_a = _nsa_prompt(q_t, ks_hm, vs_t, kw_hm, vw_t, kcmp, vcmp, gates_hm, bsz, t)
    o_b, s_new = _gla_prompt(q_l, k_l, v_l, r_l, misc, w['gla_wa'], w['gla_ba'], w['gla_ng'], bsz, t)
    x1 = _merge(o_a, o_b, x2, w['norm1'], w['w_gate'], w['w_br_a'], w['w_br_b'], w['w_o'], MERGE_TM)
    y = _mlp(x1, w['norm2'], w['w_up'], w['w_down'], w['norm_f'], MLP_TM if m % MLP_TM == 0 else PROJ_TM, MLP_TF)
    def token_major(a):
        return a.reshape(bsz, NSA_KV_HEADS, NSA_HEAD_DIM, a.shape[-1]).transpose(0, 3, 1, 2)

    rows = [token_major(a) for a in (r0, r1, r2, r3)]
    n_keep = min(WINDOW, t)
    wins = [token_major(a[:, :, t - n_keep:]) for a in (r4, r5)]
    return y.reshape(bsz, t, d), rows, wins, s_new


def _group_rows(parts):
    rowg = lax.broadcasted_iota(jnp.int32, parts[0].shape, 0) // NSA_HPG
    out = parts[0]
    for g in range(1, NSA_KV_HEADS):
        out = jnp.where(rowg == g, parts[g], out)
    return out


def _sample_select_kernel(q_ref, kc_ref, vc_ref, oc_ref, idx_ref, *, n_cmp, pos, n_pick):
    q = q_ref[0]
    nt = (((1,), (1,)), ((), ()))
    n_chunk = kc_ref.shape[2]
    s = _group_rows([lax.dot_general(q, kc_ref[0, g], nt, preferred_element_type=F32)
                     for g in range(NSA_KV_HEADS)])
    lane = lax.broadcasted_iota(jnp.int32, s.shape, 1)
    mask = (lane < n_cmp) & (lane * CMP_STRIDE + (CMP_BLOCK - 1) <= pos)
    e, l = _softmax_rows(s, mask)
    p = e * _safe_inv(l)
    pb = p.astype(BF16)
    oc_ref[0] = _group_rows([lax.dot_general(pb, vc_ref[0, g], nt, preferred_element_type=F32)
                             for g in range(NSA_KV_HEADS)])
    nr = p.shape[0]
    y = p + pltpu.roll(p, nr - 1, axis=0)
    psum = y + pltpu.roll(y, nr - 2, axis=0)
    imp = pltpu.roll(psum, 1, axis=1) + psum
    r = SLC_BLOCK // CMP_STRIDE
    for o in range(1, r):
        imp = imp + pltpu.roll(psum, n_chunk - o, axis=1)
    blk = lane // r
    is_blk = lane % r == 0
    valid = blk * SLC_BLOCK <= pos
    forced = (blk == 0) | (blk == pos // SLC_BLOCK)
    score = jnp.where(is_blk & valid, imp + jnp.where(forced, FORCE, 0.0), NEG)
    lane_f = lane.astype(F32)
    out_lane = lax.broadcasted_iota(jnp.int32, (nr, LANES), 1)
    picked = jnp.zeros((nr, LANES), F32)
    for k in range(n_pick):
        mx = jnp.max(score, axis=-1, keepdims=True)
        ix = jnp.min(jnp.where(score == mx, lane_f, float(n_chunk)), axis=-1, keepdims=True)
        picked = jnp.where(out_lane == k, ix, picked)
        score = jnp.where(lane_f == ix, 2.0 * NEG, score)
    idx_ref[0] = picked.astype(jnp.int32) // r


def _sample_select(q_s, kcmp, vcmp, pos, n_pick):
    bsz, g, n_chunk, dh = kcmp.shape
    nh = NSA_HEADS
    cmp_spec = pl.BlockSpec((1, g, n_chunk, dh), lambda b: (b, 0, 0, 0))
    cmpt_spec = pl.BlockSpec((1, g, dh, n_chunk), lambda b: (b, 0, 0, 0))
    return pl.pallas_call(
        functools.partial(_sample_select_kernel, n_cmp=n_chunk - 1, pos=pos, n_pick=n_pick),
        grid=(bsz,),
        in_specs=[pl.BlockSpec((1, nh, dh), lambda b: (b, 0, 0)), cmp_spec, cmpt_spec],
        out_specs=[pl.BlockSpec((1, nh, dh), lambda b: (b, 0, 0)), pl.BlockSpec((1, nh, LANES), lambda b: (b, 0, 0))],
        out_shape=[SDS((bsz, nh, dh), F32), SDS((bsz, nh, LANES), jnp.int32)],
        compiler_params=_params("parallel"),
    )(q_s, kcmp, vcmp)


def _sample_attend_kernel(idx_ref, pt_ref, q_ref, oc_ref, ks_hbm, vs_hbm, kw_ref, vw_ref,
                          nks_ref, nvs_ref, nkw_ref, nvw_ref, gate_ref, o_ref, ksel, vsel, sem, *, n_pick):
    b = pl.program_id(0)
    g_n, dh = NSA_KV_HEADS, NSA_HEAD_DIM
    half = PAGE_SIZE // SLC_BLOCK

    def copies():
        out = []
        for g in range(g_n):
            for r in range(n_pick):
                page = pt_ref[b, idx_ref[b * g_n + g, r] // half]
                out.append(pltpu.make_async_copy(ks_hbm.at[page, g], ksel.at[g, r], sem.at[0]))
                out.append(pltpu.make_async_copy(vs_hbm.at[page, g], vsel.at[g, r], sem.at[1]))
        return out

    for c in copies():
        c.start()

    q = q_ref[0]
    qf = q.astype(F32)
    nt = (((1,), (1,)), ((), ()))

    def attend(keys_t, vals_t, bias, k_new, v_new):
        s = _group_rows([jnp.dot(q, keys_t[g].astype(BF16), preferred_element_type=F32)
                         + (0.0 if bias is None else bias[g]) for g in range(g_n)])
        s_new = _group_rows([jnp.sum(qf * k_new[:, g * dh:(g + 1) * dh], axis=-1, keepdims=True)
                             for g in range(g_n)])
        m = jnp.maximum(jnp.max(s, axis=-1, keepdims=True), s_new)
        e = jnp.exp(s - m)
        e_new = jnp.exp(s_new - m)
        l = jnp.sum(e, axis=-1, keepdims=True) + e_new
        eb = e.astype(BF16)
        acc = _group_rows([lax.dot_general(eb, vals_t[g].astype(BF16), nt, preferred_element_type=F32)
                           + e_new * v_new[:, g * dh:(g + 1) * dh] for g in range(g_n)])
        return acc / l

    o_w = attend([kw_ref[0, g] for g in range(g_n)], [vw_ref[0, g] for g in range(g_n)], None,
                 nkw_ref[0], nvw_ref[0])
    for c in copies():
        c.wait()
    lin = lax.broadcasted_iota(jnp.int32, (1, PAGE_SIZE), 1)
    bias = []
    for g in range(g_n):
        parts = []
        for r in range(n_pick):
            off = (idx_ref[b * g_n + g, r] % half) * SLC_BLOCK
            parts.append(jnp.where((lin >= off) & (lin < off + SLC_BLOCK), 0.0, NEG))
        bias.append(jnp.concatenate(parts, axis=1))

    def tiles(buf, g):
        return jnp.concatenate([buf[g, r] for r in range(n_pick)], axis=1)

    o_s = attend([tiles(ksel, g) for g in range(g_n)], [tiles(vsel, g) for g in range(g_n)], bias,
                 nks_ref[0], nvs_ref[0])

    gt = jnp.broadcast_to(gate_ref[0], (LANES, LANES)).T
    nh = NSA_HEADS
    o_ref[0] = (gt[0:nh, 0:dh] * oc_ref[0] + gt[nh:2 * nh, 0:dh] * o_s + gt[2 * nh:3 * nh, 0:dh] * o_w)


def _sample_attend(idx, page_table, q_s, o_c, slc_k, slc_v, win_k, win_v, new_rows, gates, n_pick):
    bsz = page_table.shape[0]
    nh, dh, g = NSA_HEADS, NSA_HEAD_DIM, NSA_KV_HEADS
    wlen = win_k.shape[-1]
    row_spec = pl.BlockSpec((1, 1, NSA_KV), lambda b, *_: (b, 0, 0))
    win_spec = pl.BlockSpec((1, g, dh, wlen), lambda b, *_: (b, 0, 0, 0))
    head_spec = pl.BlockSpec((1, nh, dh), lambda b, *_: (b, 0, 0))
    any_spec = pl.BlockSpec(memory_space=pl.ANY)
    grid_spec = pltpu.PrefetchScalarGridSpec(
        num_scalar_prefetch=2, grid=(bsz,),
        in_specs=[head_spec, head_spec, any_spec, any_spec, win_spec, win_spec,
                  row_spec, row_spec, row_spec, row_spec, pl.BlockSpec((1, 1, LANES), lambda b, *_: (b, 0, 0))],
        out_specs=head_spec,
        scratch_shapes=[pltpu.VMEM((g, n_pick, dh, PAGE_SIZE), F32), pltpu.VMEM((g, n_pick, dh, PAGE_SIZE), F32),
                        pltpu.SemaphoreType.DMA((2,))])
    return pl.pallas_call(
        functools.partial(_sample_attend_kernel, n_pick=n_pick),
        grid_spec=grid_spec, out_shape=SDS((bsz, nh, dh), F32),
        compiler_params=_params("arbitrary"),
    )(idx, page_table, q_s, o_c, slc_k, slc_v, win_k, win_v, *new_rows, gates)


def _gla_sample_kernel(q_ref, k_ref, v_ref, r_ref, misc_ref, wa_ref, ba_ref, ng_ref, s0_ref, o_ref, s_ref):
    dk, dv = GLA_DK, GLA_DV
    lr = jnp.broadcast_to(misc_ref[0][:, N_GATE:N_GATE + GLA_RANK], (16, GLA_RANK)).astype(BF16)
    x = jnp.dot(lr, wa_ref[...], preferred_element_type=F32)[0:1] + ba_ref[...]
    g_all = jax.nn.log_sigmoid(x) / GLA_TAU

    def col(v):
        t = jnp.broadcast_to(v, (dk, dk)).T
        return jnp.concatenate([t] * (dv // dk), axis=1)

    outs = []
    for h in range(GLA_HEADS):
        g = g_all[:, h * dk:(h + 1) * dk]
        q = q_ref[0][:, h * dk:(h + 1) * dk] * (GLA_DK ** -0.5)
        k = k_ref[0][:, h * dk:(h + 1) * dk]
        v = v_ref[0][:, h * dv:(h + 1) * dv]
        s0 = s0_ref[0, h]
        q_t = q * jnp.exp(g)
        k_t = k * jnp.exp(-g)
        a = jnp.sum(q_t * k_t, axis=-1, keepdims=True)
        o = jnp.sum(col(q_t) * s0, axis=0, keepdims=True) + a * v
        s_ref[0, h] = col(jnp.exp(g)) * s0 + col(k) * v
        outs.append(_rms(o, ng_ref[...]) * jax.nn.silu(r_ref[0][:, h * dv:(h + 1) * dv]))
    o_ref[0] = jnp.concatenate(outs, axis=-1)


def _gla_sample(q_l, k_l, v_l, r_l, misc, wa, ba, ng, s0):
    bsz, h, dk, dv = s0.shape

    def row(n):
        return pl.BlockSpec((1, 1, n), lambda b: (b, 0, 0))

    st_spec = pl.BlockSpec((1, h, dk, dv), lambda b: (b, 0, 0, 0))
    return pl.pallas_call(
        _gla_sample_kernel, grid=(bsz,),
        in_specs=[row(h * dk), row(h * dk), row(h * dv), row(h * dv), row(LANES),
                  _resident(wa.shape), _resident(ba.shape), _resident(ng.shape), st_spec],
        out_specs=[row(h * dv), st_spec],
        out_shape=[SDS((bsz, 1, h * dv), F32), SDS((bsz, h, dk, dv), F32)],
        compiler_params=_params("parallel"),
    )(q_l, k_l, v_l, r_l, misc, wa, ba, ng, s0)


SAMPLE_ROWS = 16


def _layer_sample(x, caches, wins, s0, page_table, w):
    bsz, t, d = x.shape
    n_pages = page_table.shape[1]
    pos = n_pages * PAGE_SIZE
    assert t == 1 and bsz <= SAMPLE_ROWS and pos % SLC_BLOCK == 0 and wins[0].shape[1] == WINDOW
    assert pos // SLC_BLOCK >= N_SELECT
    mp = SAMPLE_ROWS
    x2 = jnp.pad(x.reshape(bsz, d), ((0, mp - bsz), (0, 0)))
    zb = jnp.zeros((1, LANES), F32)
    spec_a = [('bf16', 0, NSA_Q, NSA_HEAD_DIM ** -0.5)]
    spec_a += [('f32', NSA_Q + j * NSA_KV, NSA_Q + (j + 1) * NSA_KV, 1.0) for j in range(6)]
    q_s, r0, r1, r2, r3, r4, r5 = _proj(x2, w['norm1'], w['w_a'], zb, spec_a, mp)
    spec_b = [('f32', 0, _GQ, 1.0), ('f32', _GQ, 2 * _GQ, 1.0), ('f32', 2 * _GQ, 2 * _GQ + _GV, 1.0),
              ('f32', 2 * _GQ + _GV, 2 * _GQ + 2 * _GV, 1.0),
              ('misc', 2 * _GQ + 2 * _GV, 2 * _GQ + 2 * _GV + LANES, 1.0)]
    q_l, k_l, v_l, r_l, misc = _proj(x2, w['norm1'], w['w_b'], w['b_misc'], spec_b, mp)

    cache_t = [a.transpose(0, 2, 3, 1) for a in caches]
    win_t = [a.transpose(0, 2, 3, 1) for a in wins]
    n_pool = caches[0].shape[0]
    cmp_pages = [a.reshape(n_pool, NSA_KV, PAGE_SIZE) for a in cache_t[:2]]
    kcmp, vcmp = _compress(page_table, cmp_pages[0], cmp_pages[1], w['cmp'], paged=True)
    q_h = q_s.reshape(mp, NSA_HEADS, NSA_HEAD_DIM)
    n_pick = N_SELECT - 1
    o_c, idx = _sample_select(q_h, kcmp, vcmp, pos, n_pick)
    idx = idx[:, ::NSA_HPG, :n_pick].reshape(bsz * NSA_KV_HEADS, n_pick)
    gates = misc[:, :N_GATE].reshape(mp, NSA_KV_HEADS, 3, NSA_HPG).transpose(0, 2, 1, 3).reshape(mp, 1, N_GATE)
    gates = jnp.pad(gates, ((0, 0), (0, 0), (0, LANES - N_GATE)))
    new_rows = [a.reshape(mp, 1, NSA_KV) for a in (r2, r3, r4, r5)]
    o_a = _sample_attend(idx, page_table, q_h, o_c, cache_t[2], cache_t[3], win_t[0], win_t[1], new_rows, gates,
                         n_pick)
    o_a = jnp.pad(o_a.reshape(bsz, NSA_Q), ((0, mp - bsz), (0, 0))).astype(BF16)

    def r3d(a):
        return a.reshape(mp, 1, a.shape[-1])

    o_b, s_new = _gla_sample(r3d(q_l), r3d(k_l), r3d(v_l), r3d(r_l), r3d(misc), w['gla_wa'], w['gla_ba'],
                             w['gla_ng'], s0)
    o_b = jnp.pad(o_b.reshape(bsz, _GV), ((0, mp - bsz), (0, 0))).astype(BF16)
    x1 = _merge(o_a, o_b, x2, w['norm1'], w['w_gate'], w['w_br_a'], w['w_br_b'], w['w_o'], mp)
    y = _mlp(x1, w['norm2'], w['w_up'], w['w_down'], w['norm_f'], mp, MLP_TF)
    kvh = (bsz, 1, NSA_KV_HEADS, NSA_HEAD_DIM)
    rows = [a[:bsz].reshape(kvh) for a in (r0, r1, r2, r3)]
    new_wins = [jnp.concatenate([c[:, 1:], a[:bsz].reshape(kvh)], axis=1) for c, a in zip(wins, (r4, r5))]
    return y[:bsz].reshape(bsz, 1, d), rows, new_wins, s_new


def kernel(x_prompt, x_sample, cache_cmp_k, cache_cmp_v, cache_slc_k, cache_slc_v, cache_win_k, cache_win_v, state_gla, page_table, norm1_g, w_in, b_nsa_gate, cmp_pe_k, cmp_pe_v, cmp_k_w1, cmp_k_w2, cmp_v_w1, cmp_v_w2, gla_w_a2, gla_b_a, gla_norm_g, w_br_a, w_br_b, w_o, norm2_g, w_up, w_down, norm_f):
    assert DEPTH == 1 and norm1_g.shape[0] == 1
    w = _prep_weights(norm1_g[0], w_in[0], b_nsa_gate[0], cmp_pe_k[0], cmp_pe_v[0], cmp_k_w1[0], cmp_k_w2[0],
                      cmp_v_w1[0], cmp_v_w2[0], gla_w_a2[0], gla_b_a[0], gla_norm_g[0], w_br_a[0], w_br_b[0],
                      w_o[0], norm2_g[0], w_up[0], w_down[0], norm_f)
    y_p, rows_p, wins_p, s_p = _layer_prompt(x_prompt, w)
    caches = [c[0] for c in (cache_cmp_k, cache_cmp_v, cache_slc_k, cache_slc_v)]
    y_s, rows_s, wins_s, s_s = _layer_sample(x_sample, caches, [cache_win_k[0], cache_win_v[0]], state_gla[0],
                                             page_table, w)
    outs_p = [a[None] for a in rows_p + wins_p + [s_p]]
    outs_s = [a[None] for a in rows_s + wins_s + [s_s]]
    return (y_p, y_s, *outs_p, *outs_s)
```

```python
import functools

import jax
import jax.numpy as jnp
from jax import lax
from jax.experimental import pallas as pl
from jax.experimental.pallas import tpu as pltpu

D_MODEL = 2048
DEPTH = 1
PAGE_SIZE = 128
NSA_HEADS = 16
NSA_KV_HEADS = 4
NSA_HPG = NSA_HEADS // NSA_KV_HEADS
NSA_HEAD_DIM = 64
NSA_Q = NSA_HEADS * NSA_HEAD_DIM
NSA_KV = NSA_KV_HEADS * NSA_HEAD_DIM
CMP_STRIDE = 16
CMP_BLOCK = 32
CMP_HIDDEN = 128
SLC_BLOCK = 64
N_SELECT = 16
WINDOW = 512
GLA_HEADS = 4
GLA_DK = (D_MODEL // 4) // GLA_HEADS
GLA_DV = (D_MODEL // 2) // GLA_HEADS
GLA_RANK = 16
GLA_TAU = 16.0
D_FF = 4 * D_MODEL
EPS = 1e-6
NEG = -1e30
FORCE = 1e4
SPLITS = (NSA_Q, 6 * NSA_KV, 3 * NSA_HEADS,
          GLA_HEADS * GLA_DK, GLA_HEADS * GLA_DK, GLA_HEADS * GLA_DV, GLA_HEADS * GLA_DV,
          GLA_RANK, 2 * D_MODEL)

F32 = jnp.float32
BF16 = jnp.bfloat16
LANES = 128
VMEM_LIMIT_BYTES = 56 * 1024 * 1024
MLP_VMEM_LIMIT_BYTES = 60 * 1024 * 1024
N_GATE = 3 * NSA_HEADS
GLA_CHUNK = 32
GLA_UNROLL = 8
NSA_TQ = 256
NSA_TK = 256
VT_ROWS = NSA_HEAD_DIM + 16
CMP_PAGES = 16
PROJ_TM = 512
MERGE_TM = 256
MLP_TM = 1024
MLP_TF = 512
SDS = jax.ShapeDtypeStruct


def _params(*sem, vmem_limit_bytes=VMEM_LIMIT_BYTES):
    return pltpu.CompilerParams(dimension_semantics=sem, vmem_limit_bytes=vmem_limit_bytes)


def _resident(shape):
    nd = len(shape)
    return pl.BlockSpec(shape, lambda *_: (0,) * nd, pipeline_mode=pl.Buffered(1))


def _rms(x, g):
    return x * lax.rsqrt(jnp.mean(x * x, axis=-1, keepdims=True) + EPS) * g


def _proj_kernel(x_ref, g_ref, w_ref, b_ref, *out_refs, spec):
    h = _rms(x_ref[...], g_ref[...]).astype(BF16)
    misc = None
    for o_ref, (kind, c0, c1, scale) in zip(out_refs, spec):
        if kind == 'gates':
            for g in range(NSA_KV_HEADS):
                o_ref[g] = misc if g == 0 else pltpu.roll(misc, LANES - g * 3 * NSA_HPG, axis=1)
            continue
        r = jnp.dot(h, w_ref[:, c0:c1], preferred_element_type=F32)
        if scale != 1.0:
            r = r * scale
        if kind == 'f32':
            o_ref[...] = r
        elif kind == 'f32t':
            o_ref[0] = r.T
        elif kind == 'bf16':
            o_ref[...] = r.astype(BF16)
        elif kind == 'sigmoid':
            o_ref[...] = jax.nn.sigmoid(r)
        elif kind == 'hm':
            for i in range((c1 - c0) // NSA_HEAD_DIM):
                o_ref[i] = r[:, i * NSA_HEAD_DIM:(i + 1) * NSA_HEAD_DIM].astype(BF16)
        elif kind == 'hmt':
            rt = r.T
            for i in range((c1 - c0) // NSA_HEAD_DIM):
                o_ref[i] = rt[i * NSA_HEAD_DIM:(i + 1) * NSA_HEAD_DIM].astype(BF16)
        elif kind == 'hmvt':
            rt = r.T
            sub = lax.broadcasted_iota(jnp.int32, (VT_ROWS - NSA_HEAD_DIM, r.shape[0]), 0)
            ones = jnp.where(sub == 0, 1.0, 0.0)
            for i in range((c1 - c0) // NSA_HEAD_DIM):
                piece = rt[i * NSA_HEAD_DIM:(i + 1) * NSA_HEAD_DIM]
                o_ref[i] = jnp.concatenate([piece, ones], axis=0).astype(BF16)
        elif kind == 'misc':
            lane = lax.broadcasted_iota(jnp.int32, r.shape, 1)
            misc = jnp.where(lane < N_GATE, jax.nn.sigmoid(r + b_ref[...]), r)
            o_ref[...] = misc


def _proj(x, norm_g, w, bias, spec, tm, seq_len=None):
    m, d = x.shape
    n = w.shape[1]
    assert m % tm == 0
    out_shape, out_specs = [], []
    for kind, c0, c1, _ in spec:
        if kind == 'f32t':
            assert seq_len % tm == 0
            per_seq = seq_len // tm
            out_shape.append(SDS((m // seq_len, c1 - c0, seq_len), F32))
            out_specs.append(pl.BlockSpec((1, c1 - c0, tm), lambda i: (i // per_seq, 0, i % per_seq)))
        elif kind == 'hm':
            nh = (c1 - c0) // NSA_HEAD_DIM
            out_shape.append(SDS((nh, m, NSA_HEAD_DIM), BF16))
            out_specs.append(pl.BlockSpec((nh, tm, NSA_HEAD_DIM), lambda i: (0, i, 0)))
        elif kind in ('hmt', 'hmvt'):
            nh = (c1 - c0) // NSA_HEAD_DIM
            nrow = NSA_HEAD_DIM if kind == 'hmt' else VT_ROWS
            out_shape.append(SDS((nh, nrow, m), BF16))
            out_specs.append(pl.BlockSpec((nh, nrow, tm), lambda i: (0, 0, i)))
        elif kind == 'gates':
            out_shape.append(SDS((NSA_KV_HEADS, m, LANES), F32))
            out_specs.append(pl.BlockSpec((NSA_KV_HEADS, tm, LANES), lambda i: (0, i, 0)))
        else:
            out_shape.append(SDS((m, c1 - c0), BF16 if kind == 'bf16' else F32))
            out_specs.append(pl.BlockSpec((tm, c1 - c0), lambda i: (i, 0)))
    return pl.pallas_call(
        functools.partial(_proj_kernel, spec=tuple(spec)),
        grid=(m // tm,),
        in_specs=[pl.BlockSpec((tm, d), lambda i: (i, 0)), _resident((1, d)), _resident((d, n)),
                  _resident((1, LANES))],
        out_specs=out_specs, out_shape=out_shape,
        compiler_params=_params("parallel"),
    )(x, norm_g, w, bias)


def _cmp_kernel(pt_ref, k_hbm, v_hbm, perm_ref, pek_ref, pev_ref, w1k_ref, w1v_ref, w2k_ref, w2v_ref,
                ok_ref, ov_ref, kbuf, vbuf, hk, hv, sem, *, n_pages_step, paged):
    b = pl.program_id(0)
    s = pl.program_id(1)
    ns = pl.num_programs(1)
    t = b * ns + s
    total = pl.num_programs(0) * ns
    rows = n_pages_step * 8
    dh, hid_n = NSA_HEAD_DIM, CMP_HIDDEN

    def copies(tt, slot):
        bb = tt // ns
        ss = tt % ns
        out = []
        for p in range(n_pages_step):
            page = pt_ref[bb, ss * n_pages_step + p]
            if paged:
                src = [hbm.at[page] for hbm in (k_hbm, v_hbm)]
            else:
                tok = pl.ds(pl.multiple_of(page * PAGE_SIZE, PAGE_SIZE), PAGE_SIZE)
                src = [hbm.at[bb, :, tok] for hbm in (k_hbm, v_hbm)]
            out.append(pltpu.make_async_copy(src[0], kbuf.at[slot, p], sem.at[slot, 0]))
            out.append(pltpu.make_async_copy(src[1], vbuf.at[slot, p], sem.at[slot, 1]))
        return out

    slot = t % 2

    @pl.when(t == 0)
    def _():
        for c in copies(t, slot):
            c.start()

    @pl.when(t + 1 < total)
    def _():
        for c in copies(t + 1, 1 - slot):
            c.start()

    for c in copies(t, slot):
        c.wait()

    r0 = pl.multiple_of(s * rows, rows)
    low = lax.broadcasted_iota(jnp.int32, (8, LANES), 1) < dh
    tok_pairs = CMP_STRIDE // 2
    nt = (((1,), (1,)), ((), ()))
    perm = perm_ref[...]
    for buf, w1_ref, h_ref in ((kbuf, w1k_ref, hk), (vbuf, w1v_ref, hv)):
        parts = [[[] for _ in range(tok_pairs)] for _ in range(NSA_KV_HEADS)]
        for p in range(n_pages_step):
            r = lax.dot_general(perm, buf[slot, p].astype(BF16), nt, preferred_element_type=F32)
            for q in range(2):
                for m in range(tok_pairs):
                    x0 = r[2 * m * 8:2 * m * 8 + 8, q * LANES:(q + 1) * LANES]
                    x1 = r[(2 * m + 1) * 8:(2 * m + 1) * 8 + 8, q * LANES:(q + 1) * LANES]
                    parts[2 * q][m].append(jnp.where(low, x0, pltpu.roll(x1, dh, axis=1)))
                    parts[2 * q + 1][m].append(jnp.where(low, pltpu.roll(x0, dh, axis=1), x1))
        for g in range(NSA_KV_HEADS):
            z = jnp.concatenate([jnp.concatenate(parts[g][m], axis=0) for m in range(tok_pairs)],
                                axis=1).astype(BF16)
            h_ref[pl.ds(r0, rows), g * 2 * hid_n:(g + 1) * 2 * hid_n] = jnp.dot(
                z, w1_ref[...], preferred_element_type=F32)

    @pl.when(s == ns - 1)
    def _():
        for h_ref, pe_ref, w1_ref, w2_ref, o_ref in ((hk, pek_ref, w1k_ref, w2k_ref, ok_ref),
                                                     (hv, pev_ref, w1v_ref, w2v_ref, ov_ref)):
            n_chunk = h_ref.shape[0]
            pe = jnp.broadcast_to(pe_ref[...], (2, 16, pe_ref.shape[2])).astype(BF16)
            pe_term = (jnp.dot(pe[0], w1_ref[:, 0:hid_n], preferred_element_type=F32)
                       + jnp.dot(pe[1], w1_ref[:, hid_n:2 * hid_n], preferred_element_type=F32))[0:1]
            hid = []
            for g in range(NSA_KV_HEADS):
                first = h_ref[:, g * 2 * hid_n:g * 2 * hid_n + hid_n]
                last = h_ref[:, g * 2 * hid_n + hid_n:(g + 1) * 2 * hid_n]
                hid.append(first + pltpu.roll(last, n_chunk - 1, axis=0) + pe_term)
            hid = jnp.concatenate(hid, axis=1)
            res = jnp.dot(jax.nn.gelu(hid).astype(BF16), w2_ref[...], preferred_element_type=F32)
            if o_ref is ok_ref:
                for g in range(NSA_KV_HEADS):
                    o_ref[0, g] = res[:, g * NSA_HEAD_DIM:(g + 1) * NSA_HEAD_DIM].astype(BF16)
            else:
                res_t = res.T
                for g in range(NSA_KV_HEADS):
                    o_ref[0, g] = res_t[g * NSA_HEAD_DIM:(g + 1) * NSA_HEAD_DIM].astype(BF16)


def _cmp_weights(pe, w1, w2):
    g = NSA_KV_HEADS
    half = CMP_STRIDE * NSA_HEAD_DIM
    pe_t = pe.reshape(2, 1, half)
    w1ab = jnp.concatenate([w1[:half], w1[half:]], axis=1)
    w2b = jnp.einsum('jd,gh->gjhd', w2, jnp.eye(g, dtype=F32)).reshape(g * CMP_HIDDEN, g * NSA_HEAD_DIM)
    return pe_t, w1ab.astype(BF16), w2b.astype(BF16)


def _compress(page_table, k_src, v_src, cw, paged):
    bsz, n_pages = page_table.shape
    p_step = min(CMP_PAGES, n_pages)
    ns = n_pages // p_step
    n_chunk = n_pages * 8
    page_shape = (NSA_KV, PAGE_SIZE)
    assert NSA_KV == 2 * LANES
    assert k_src.shape[1:] == (page_shape if paged else (NSA_KV, n_pages * PAGE_SIZE))
    half = CMP_STRIDE * NSA_HEAD_DIM
    gh = NSA_KV_HEADS * CMP_HIDDEN
    out_row = jnp.arange(PAGE_SIZE, dtype=jnp.int32)
    src_tok = (out_row % 8) * CMP_STRIDE + out_row // 8
    perm = (src_tok[:, None] == jnp.arange(PAGE_SIZE, dtype=jnp.int32)[None, :]).astype(BF16)
    k_sds =SDS((bsz, NSA_KV_HEADS, n_chunk, NSA_HEAD_DIM), BF16)
    v_sds = SDS((bsz, NSA_KV_HEADS, NSA_HEAD_DIM, n_chunk), BF16)
    k_spec = pl.BlockSpec((1, NSA_KV_HEADS, n_chunk, NSA_HEAD_DIM), lambda b, s, pt: (b, 0, 0, 0))
    v_spec = pl.BlockSpec((1, NSA_KV_HEADS, NSA_HEAD_DIM, n_chunk), lambda b, s, pt: (b, 0, 0, 0))
    grid_spec = pltpu.PrefetchScalarGridSpec(
        num_scalar_prefetch=1, grid=(bsz, ns),
        in_specs=[pl.BlockSpec(memory_space=pl.ANY), pl.BlockSpec(memory_space=pl.ANY),
                  _resident((PAGE_SIZE, PAGE_SIZE)), _resident((2, 1, half)), _resident((2, 1, half)),
                  _resident((half, 2 * CMP_HIDDEN)), _resident((half, 2 * CMP_HIDDEN)),
                  _resident((gh, NSA_KV)), _resident((gh, NSA_KV))],
        out_specs=[k_spec, v_spec],
        scratch_shapes=[pltpu.VMEM((2, p_step) + page_shape, F32), pltpu.VMEM((2, p_step) + page_shape, F32),
                        pltpu.VMEM((n_chunk, 2 * gh), F32), pltpu.VMEM((n_chunk, 2 * gh), F32),
                        pltpu.SemaphoreType.DMA((2, 2))])
    return pl.pallas_call(
        functools.partial(_cmp_kernel, n_pages_step=p_step, paged=paged),
        grid_spec=grid_spec, out_shape=[k_sds, v_sds],
        compiler_params=_params("arbitrary", "arbitrary"),
    )(page_table, k_src, v_src, perm, cw['pe_k'], cw['pe_v'], cw['w1_k'], cw['w1_v'], cw['w2_k'], cw['w2_v'])


def _softmax_rows(s, mask):
    sm = jnp.where(mask, s, NEG)
    m = jnp.max(sm, axis=-1, keepdims=True)
    e = jnp.where(mask, jnp.exp(sm - m), 0.0)
    return e, jnp.sum(e, axis=-1, keepdims=True)


def _safe_inv(l):
    return jnp.where(l > 0.0, 1.0 / jnp.where(l > 0.0, l, 1.0), 0.0)


M_INIT = -1e20
LOG2_E = 1.4426950408889634


def _nsa_prompt_kernel(qt_ref, kc_ref, vct_ref, ks_ref, vst_ref, kw_ref, vwt_ref, gate_ref,
                       o_ref, *scratch, n_cmp, n_slc, n_sel):
    tq, tk, hpg, dh = NSA_TQ, NSA_TK, NSA_HPG, NSA_HEAD_DIM
    rk_refs = scratch[:tq // LANES]
    selb_ref, m_ref, acc_ref = scratch[tq // LANES:tq // LANES + 3]
    s_refs, p_refs, a_refs = (scratch[tq // LANES + 3 + 2 * i:tq // LANES + 5 + 2 * i] for i in range(3))
    qi = pl.program_id(2)
    q0 = qi * tq
    n_chunk = kc_ref.shape[2]
    r = SLC_BLOCK // CMP_STRIDE
    n_row = n_chunk // r

    ci = lax.broadcasted_iota(jnp.int32, (n_chunk, tq), 0)
    pos_c = q0 + lax.broadcasted_iota(jnp.int32, (n_chunk, tq), 1)
    m_c = (ci < n_cmp) & (ci * CMP_STRIDE + (CMP_BLOCK - 1) <= pos_c)
    kc = kc_ref[0, 0]
    vct = vct_ref[0, 0]
    o_c = []
    psum = None
    for h in range(hpg):
        s = jnp.where(m_c, jnp.dot(kc, qt_ref[0, h], preferred_element_type=F32), NEG)
        e = jnp.where(m_c, jnp.exp2(s - jnp.max(s, axis=0, keepdims=True)), 0.0)
        p = e * _safe_inv(jnp.sum(e, axis=0, keepdims=True))
        o_c.append(jnp.dot(vct, p.astype(BF16), preferred_element_type=F32))
        psum = p if psum is None else psum + p

    imp = pltpu.roll(psum, 1, axis=0) + psum
    for o in range(1, r):
        imp = imp + pltpu.roll(psum, n_chunk - o, axis=0)
    parts = []
    for i, rk_ref in enumerate(rk_refs):
        rk_ref[...] = imp[:, i * LANES:(i + 1) * LANES]
        parts.append(rk_ref[pl.ds(0, n_row, stride=r), :])
    imp_b = jnp.concatenate(parts, axis=1)
    j_io = lax.broadcasted_iota(jnp.int32, (n_row, tq), 0)
    pos_b = q0 + lax.broadcasted_iota(jnp.int32, (n_row, tq), 1)
    valid = (j_io < n_slc) & (j_io * SLC_BLOCK <= pos_b)
    forced = (j_io == 0) | (j_io == pos_b // SLC_BLOCK)
    sc = jnp.where(valid, imp_b + jnp.where(forced, FORCE, 0.0), NEG)
    rank = jnp.zeros((n_row, tq), F32)
    for k in range(n_slc):
        ck = sc[k:k + 1, :]
        beats = (ck > sc) | ((ck == sc) & (j_io > k))
        rank = rank + jnp.where(beats, 1.0, 0.0)
    selb_ref[...] = jnp.where(rank < n_sel, 0.0, NEG)

    dlt = lax.broadcasted_iota(jnp.int32, (tk, tq), 0) - lax.broadcasted_iota(jnp.int32, (tk, tq), 1)
    blocks_per_tile = tk // SLC_BLOCK

    def flash(k_ref, vt_ref, lo, hi, bias_fn):
        n_tiles = k_ref.shape[2] // tk
        m_ref[...] = jnp.full(m_ref.shape, M_INIT, F32)
        acc_ref[...] = jnp.zeros(acc_ref.shape, F32)
        p_refs[1][...] = jnp.zeros(p_refs[1].shape, BF16)
        a_refs[1][...] = jnp.ones(a_refs[1].shape, F32)

        def tile_start(kb):
            return pl.multiple_of(jnp.clip(kb, 0, n_tiles - 1) * tk, tk)

        def scores(kb, s_ref):
            kk = k_ref[0, 0, pl.ds(tile_start(kb), tk), :]
            bias = bias_fn(kb)
            for h in range(hpg):
                for c in range(tk // SLC_BLOCK):
                    rs = slice(c * SLC_BLOCK, (c + 1) * SLC_BLOCK)
                    s_ref[h, rs, :] = jnp.dot(kk[rs], qt_ref[0, h], preferred_element_type=F32) + bias[rs]

        def softmax(s_ref, p_ref, a_ref):
            for h in range(hpg):
                for c in range(tq // LANES):
                    cs = slice(c * LANES, (c + 1) * LANES)
                    s = s_ref[h, :, cs]
                    m_prev = m_ref[h, :, cs]
                    m_new = jnp.maximum(m_prev, jnp.max(s, axis=0, keepdims=True))
                    p_ref[h, :, cs] = jnp.exp2(s - m_new).astype(BF16)
                    a_ref[h, :, cs] = jnp.exp2(m_prev - m_new)
                    m_ref[h, :, cs] = m_new

        def weighted_values(kb, p_ref, a_ref):
            vt = vt_ref[0, :, pl.ds(tile_start(kb), tk)]
            for h in range(hpg):
                acc_ref[h] = a_ref[h] * acc_ref[h] + jnp.dot(vt, p_ref[h], preferred_element_type=F32)

        scores(lo, s_refs[0])

        def body(j, carry):
            i = lo + 2 * j
            scores(i + 1, s_refs[1])
            softmax(s_refs[0], p_refs[0], a_refs[0])
            weighted_values(i - 1, p_refs[1], a_refs[1])
            scores(i + 2, s_refs[0])
            softmax(s_refs[1], p_refs[1], a_refs[1])
            weighted_values(i, p_refs[0], a_refs[0])
            return carry

        n_pairs = (hi - lo + 1) // 2
        lax.fori_loop(0, n_pairs, body, 0)
        weighted_values(lo + 2 * n_pairs - 1, p_refs[1], a_refs[1])
        return [acc_ref[h, 0:dh, :] * _safe_inv(acc_ref[h, dh:dh + 1, :]) for h in range(hpg)]

    def slc_bias(kb):
        first = jnp.minimum(kb, n_slc // blocks_per_tile - 1) * blocks_per_tile
        rows = [jnp.broadcast_to(selb_ref[pl.ds(first + i, 1), :], (SLC_BLOCK, tq))
                for i in range(blocks_per_tile)]
        return jnp.where(dlt + (kb * tk - q0) <= 0, jnp.concatenate(rows, axis=0), NEG)

    def win_bias(kb):
        d = dlt + (kb * tk - q0)
        return jnp.where((d <= 0) & (d >= -WINDOW), 0.0, NEG)

    hi = (q0 + tq - 1) // tk + 1
    o_w = flash(kw_ref, vwt_ref, jnp.maximum(q0 - WINDOW, 0) // tk, hi, win_bias)
    o_s = flash(ks_ref, vst_ref, 0, hi, slc_bias)

    gt = gate_ref[0].T
    outs = [gt[h:h + 1] * o_c[h] + gt[hpg + h:hpg + h + 1] * o_s[h] + gt[2 * hpg + h:2 * hpg + h + 1] * o_w[h]
            for h in range(hpg)]
    o_ref[...] = jnp.concatenate(outs, axis=0).T.astype(BF16)


def _nsa_prompt(q_t, ks_hm, vs_t, kw_hm, vw_t, kcmp, vcmp_t, gates_hm, bsz, t):
    g, hpg, dh, tq = NSA_KV_HEADS, NSA_HPG, NSA_HEAD_DIM, NSA_TQ
    r = SLC_BLOCK // CMP_STRIDE
    n_slc = t // SLC_BLOCK
    assert t % NSA_TK == 0 and t % tq == 0 and n_slc * r <= LANES and NSA_TK % SLC_BLOCK == 0
    nq = t // tq
    n_cmp = kcmp.shape[2] - 1
    if kcmp.shape[2] < LANES:
        fill = LANES - kcmp.shape[2]
        kcmp = jnp.pad(kcmp, ((0, 0), (0, 0), (0, fill), (0, 0)))
        vcmp_t = jnp.pad(vcmp_t, ((0, 0), (0, 0), (0, 0), (0, fill)))
    n_chunk = kcmp.shape[2]
    assert n_chunk == LANES
    q4 = q_t.reshape(g, hpg, dh, bsz * t)

    def k_spec():
        return pl.BlockSpec((1, 1, t, dh), lambda b, gg, qi: (gg, b, 0, 0))

    def vt_spec():
        return pl.BlockSpec((1, VT_ROWS, t), lambda b, gg, qi: (gg, 0, b))

    def per_bt(a):
        return a.reshape(g, bsz, t, a.shape[-1])

    return pl.pallas_call(
        functools.partial(_nsa_prompt_kernel, n_cmp=n_cmp, n_slc=n_slc, n_sel=min(N_SELECT, n_slc)),
        grid=(bsz, g, nq),
        in_specs=[pl.BlockSpec((1, hpg, dh, tq), lambda b, gg, qi: (gg, 0, 0, b * nq + qi)),
                  pl.BlockSpec((1, 1, n_chunk, dh), lambda b, gg, qi: (b, gg, 0, 0)),
                  pl.BlockSpec((1, 1, dh, n_chunk), lambda b, gg, qi: (b, gg, 0, 0)),
                  k_spec(), vt_spec(), k_spec(), vt_spec(),
                  pl.BlockSpec((1, tq, LANES), lambda b, gg, qi: (gg, b * nq + qi, 0))],
        out_specs=pl.BlockSpec((tq, hpg * dh), lambda b, gg, qi: (b * nq + qi, gg)),
        out_shape=SDS((bsz * t, NSA_Q), BF16),
        scratch_shapes=[pltpu.VMEM((n_chunk, LANES), F32)] * (tq // LANES) + [
                        pltpu.VMEM((n_chunk // r, tq), F32),
                        pltpu.VMEM((hpg, 1, tq), F32), pltpu.VMEM((hpg, VT_ROWS, tq), F32),
                        pltpu.VMEM((hpg, NSA_TK, tq), F32), pltpu.VMEM((hpg, NSA_TK, tq), F32),
                        pltpu.VMEM((hpg, NSA_TK, tq), BF16), pltpu.VMEM((hpg, NSA_TK, tq), BF16),
                        pltpu.VMEM((hpg, 1, tq), F32), pltpu.VMEM((hpg, 1, tq), F32)],
        compiler_params=_params("parallel", "parallel", "arbitrary"),
    )(q4, kcmp, vcmp_t, per_bt(ks_hm), vs_t, per_bt(kw_hm), vw_t, gates_hm)


def _gla_prompt_kernel(q_ref, k_ref, v_ref, r_ref, misc_ref, wa_ref, ba_ref, ng_ref,
                       o_ref, s_ref, qe_ref, qt_ref, kt_ref, kh_ref, d_ref, u_ref, st_ref):
    t = q_ref.shape[0]
    c = GLA_CHUNK
    n = t // c
    dk, dv = GLA_DK, GLA_DV
    lr = misc_ref[:, N_GATE:N_GATE + GLA_RANK].astype(BF16)
    x = jnp.dot(lr, wa_ref[...], preferred_element_type=F32) + ba_ref[...]
    g = jax.nn.log_sigmoid(x) / GLA_TAU
    row = lax.broadcasted_iota(jnp.int32, (t, dk), 0) % c
    b = g
    sh = 1
    while sh < c:
        b = b + jnp.where(row >= sh, pltpu.roll(b, sh, axis=0), 0.0)
        sh *= 2
    b3 = b.reshape(n, c, dk)
    b_last = jnp.broadcast_to(b3[:, c - 1:c, :], (n, c, dk)).reshape(t, dk)
    b_mid = jnp.broadcast_to(b3[:, c // 2 - 1:c // 2, :], (n, c, dk)).reshape(t, dk)
    q = q_ref[...] * (GLA_DK ** -0.5)
    k = k_ref[...]
    qe_ref[...] = (q * jnp.exp(b)).astype(BF16)
    qt_ref[...] = (q * jnp.exp(b - b_mid)).astype(BF16)
    kt_ref[...] = (k * jnp.exp(b_mid - b)).astype(BF16)
    kh_ref[...] = (k * jnp.exp(b_last - b)).astype(BF16)
    d_ref[...] = jnp.exp(b_last)
    tn = (((0,), (0,)), ((), ()))
    nt = (((1,), (1,)), ((), ()))

    def chunk_update(i, carry):
        r0 = pl.multiple_of(i * c, c)
        u_ref[i] = lax.dot_general(v_ref[pl.ds(r0, c), :], kh_ref[pl.ds(r0, c), :], tn,
                                   preferred_element_type=F32)
        return carry

    lax.fori_loop(0, n, chunk_update, 0, unroll=GLA_UNROLL)

    st_ref[...] = jnp.zeros((dv, dk), F32)

    def recur(i, carry):
        st = st_ref[...]
        st_ref[...] = st * d_ref[pl.ds(pl.multiple_of(i * c, c), 1), :] + u_ref[i]
        u_ref[i] = st
        return carry

    lax.fori_loop(0, n, recur, 0)
    s_ref[0, 0] = st_ref[...].T

    tril = lax.broadcasted_iota(jnp.int32, (c, c), 0) >= lax.broadcasted_iota(jnp.int32, (c, c), 1)
    ng = ng_ref[...]

    def chunk_out(i, carry):
        r0 = pl.multiple_of(i * c, c)
        vv = v_ref[pl.ds(r0, c), :]
        o = lax.dot_general(qe_ref[pl.ds(r0, c), :], u_ref[i].astype(BF16), nt, preferred_element_type=F32)
        a = lax.dot_general(qt_ref[pl.ds(r0, c), :], kt_ref[pl.ds(r0, c), :], nt, preferred_element_type=F32)
        a = jnp.where(tril, a, 0.0).astype(BF16)
        o = o + jnp.dot(a, vv, preferred_element_type=F32)
        o = _rms(o, ng) * jax.nn.silu(r_ref[pl.ds(r0, c), :])
        o_ref[pl.ds(r0, c), :] = o.astype(BF16)
        return carry

    lax.fori_loop(0, n, chunk_out, 0, unroll=GLA_UNROLL)


def _gla_prompt(q_l, k_l, v_l, r_l, misc, wa, ba, ng, bsz, t):
    h, dk, dv = GLA_HEADS, GLA_DK, GLA_DV
    assert t % GLA_CHUNK == 0
    n = t // GLA_CHUNK
    return pl.pallas_call(
        _gla_prompt_kernel,
        grid=(bsz, h),
        in_specs=[pl.BlockSpec((t, dk), lambda b, hh: (b, hh)), pl.BlockSpec((t, dk), lambda b, hh: (b, hh)),
                  pl.BlockSpec((t, dv), lambda b, hh: (b, hh)), pl.BlockSpec((t, dv), lambda b, hh: (b, hh)),
                  pl.BlockSpec((t, LANES), lambda b, hh: (b, 0)),
                  pl.BlockSpec((GLA_RANK, dk), lambda b, hh: (0, hh)), pl.BlockSpec((1, dk), lambda b, hh: (0, hh)),
                  _resident((1, dv))],
        out_specs=[pl.BlockSpec((t, dv), lambda b, hh: (b, hh)),
                   pl.BlockSpec((1, 1, dk, dv), lambda b, hh: (b, hh, 0, 0))],
        out_shape=[SDS((bsz * t, h * dv), BF16), SDS((bsz, h, dk, dv), F32)],
        scratch_shapes=[pltpu.VMEM((t, dk), BF16), pltpu.VMEM((t, dk), BF16), pltpu.VMEM((t, dk), BF16),
                        pltpu.VMEM((t, dk), BF16), pltpu.VMEM((t, dk), F32),
                        pltpu.VMEM((n, dv, dk), F32), pltpu.VMEM((dv, dk), F32)],
        compiler_params=_params("parallel", "parallel"),
    )(q_l, k_l, v_l, r_l, misc, wa, ba, ng)


def _merge_rows(oa, ob, x, g_ref, wg_ref, wa_ref, wb_ref, wo_ref):
    h = _rms(x, g_ref[...]).astype(BF16)
    d = D_MODEL
    ga = jax.nn.sigmoid(jnp.dot(h, wg_ref[:, 0:d], preferred_element_type=F32))
    u = ga * jnp.dot(oa, wa_ref[...], preferred_element_type=F32)
    gb = jax.nn.sigmoid(jnp.dot(h, wg_ref[:, d:2 * d], preferred_element_type=F32))
    u = u + gb * jnp.dot(ob, wb_ref[...], preferred_element_type=F32)
    return x + jnp.dot(u.astype(BF16), wo_ref[...], preferred_element_type=F32)


def _merge_kernel(oa_ref, ob_ref, x_ref, oas_ref, obs_ref, xs_ref, g_ref, wg_ref, wa_ref, wb_ref, wo_ref,
                  x1_ref, x1s_ref):
    weights = (g_ref, wg_ref, wa_ref, wb_ref, wo_ref)
    x1_ref[...] = _merge_rows(oa_ref[...], ob_ref[...], x_ref[...], *weights)

    @pl.when(pl.program_id(0) == 0)
    def _():
        x1s_ref[...] = _merge_rows(oas_ref[...], obs_ref[...], xs_ref[...], *weights)


def _merge(o_a, o_b, x, oa_s, ob_s, x_s, norm_g, w_gate, w_a, w_b, w_o, tm):
    m, d = x.shape
    assert m % tm == 0
    row_specs = [pl.BlockSpec((tm, NSA_Q), lambda i: (i, 0)), pl.BlockSpec((tm, GLA_HEADS * GLA_DV), lambda i: (i, 0)),
                 pl.BlockSpec((tm, d), lambda i: (i, 0))]
    return pl.pallas_call(
        _merge_kernel, grid=(m // tm,),
        in_specs=row_specs + [_resident(a.shape) for a in (oa_s, ob_s, x_s, norm_g, w_gate, w_a, w_b, w_o)],
        out_specs=[pl.BlockSpec((tm, d), lambda i: (i, 0)), pl.BlockSpec(x_s.shape, lambda i: (0, 0))],
        out_shape=[SDS((m, d), F32), SDS(x_s.shape, F32)],
        compiler_params=_params("arbitrary"),
    )(o_a, o_b, x, oa_s, ob_s, x_s, norm_g, w_gate, w_a, w_b, w_o)


def _mlp_kernel(x1_ref, x1s_ref, g2_ref, wu_ref, wd_ref, gf_ref, y_ref, ys_ref, h_ref, hs_ref):
    i = pl.program_id(0)
    j = pl.program_id(1)
    last = pl.num_programs(1) - 1
    wu = wu_ref[...].astype(BF16)
    wd = wd_ref[...].astype(BF16)

    def step(x_ref, o_ref, hid_ref):
        @pl.when(j == 0)
        def _():
            x1 = x_ref[...]
            hid_ref[...] = _rms(x1, g2_ref[...]).astype(BF16)
            o_ref[...] = x1

        up = jnp.maximum(jnp.dot(hid_ref[...], wu, preferred_element_type=F32), 0.0)
        o_ref[...] += jnp.dot((up * up).astype(BF16), wd, preferred_element_type=F32)

        @pl.when(j == last)
        def _():
            o_ref[...] = _rms(o_ref[...], gf_ref[...])

    step(x1_ref, y_ref, h_ref)

    @pl.when(i == 0)
    def _():
        step(x1s_ref, ys_ref, hs_ref)


def _mlp(x1, x1_s, g2, w_up, w_down, gf, tm, tf):
    m, d = x1.shape
    ff = w_up.shape[1]
    assert m % tm == 0 and ff % tf == 0
    small = pl.BlockSpec(x1_s.shape, lambda i, j: (0, 0))
    return pl.pallas_call(
        _mlp_kernel, grid=(m // tm, ff // tf),
        in_specs=[pl.BlockSpec((tm, d), lambda i, j: (i, 0)), small, _resident((1, d)),
                  pl.BlockSpec((d, tf), lambda i, j: (0, j)), pl.BlockSpec((tf, d), lambda i, j: (j, 0)),
                  _resident((1, d))],
        out_specs=[pl.BlockSpec((tm, d), lambda i, j: (i, 0)), small],
        out_shape=[SDS((m, d), F32), SDS(x1_s.shape, F32)],
        scratch_shapes=[pltpu.VMEM((tm, d), BF16), pltpu.VMEM(x1_s.shape, BF16)],
        compiler_params=_params("arbitrary", "arbitrary", vmem_limit_bytes=MLP_VMEM_LIMIT_BYTES),
    )(x1, x1_s, g2, w_up, w_down, gf)


def _prep_weights(norm1_g, w_in, b_nsa_gate, cmp_pe_k, cmp_pe_v, cmp_k_w1, cmp_k_w2, cmp_v_w1, cmp_v_w2,
                  gla_w_a2, gla_b_a, gla_norm_g, w_br_a, w_br_b, w_o, norm2_g, w_up, w_down, norm_f):
    pts = [0]
    for s in SPLITS:
        pts.append(pts[-1] + s)
    c_q, c_kv, c_g, c_ql, c_kl, c_vl, c_rl, c_lr, c_br, c_end = pts
    gcols = jnp.asarray([c_g + (g * NSA_HPG + h) * 3 + c for g in range(NSA_KV_HEADS)
                         for c in range(3) for h in range(NSA_HPG)], jnp.int32)
    pad = jnp.zeros((D_MODEL, LANES - N_GATE - GLA_RANK), F32)
    w_misc = jnp.concatenate([w_in[:, gcols], w_in[:, c_lr:c_br], pad], axis=1)
    b_misc = jnp.concatenate([b_nsa_gate[gcols - c_g], jnp.zeros((LANES - N_GATE,), F32)])[None, :]
    w = dict(
        norm1=norm1_g[None, :], norm2=norm2_g[None, :], norm_f=norm_f[None, :],
        w_a=w_in[:, c_q:c_g].astype(BF16),
        w_b=jnp.concatenate([w_in[:, c_ql:c_lr], w_misc], axis=1).astype(BF16),
        w_gate=w_in[:, c_br:c_end].astype(BF16),
        b_misc=b_misc,
        gla_wa=gla_w_a2.astype(BF16), gla_ba=gla_b_a[None, :], gla_ng=gla_norm_g[None, :],
        w_br_a=w_br_a.astype(BF16), w_br_b=w_br_b.astype(BF16), w_o=w_o.astype(BF16),
        w_up=w_up, w_down=w_down,
    )
    cw = {}
    cw['pe_k'], cw['w1_k'], cw['w2_k'] = _cmp_weights(cmp_pe_k, cmp_k_w1, cmp_k_w2)
    cw['pe_v'], cw['w1_v'], cw['w2_v'] = _cmp_weights(cmp_pe_v, cmp_v_w1, cmp_v_w2)
    w['cmp'] = cw
    return w


_KV6 = 6 * NSA_KV
_GQ = GLA_HEADS * GLA_DK
_GV = GLA_HEADS * GLA_DV


def _layer_prompt(x, w):
    bsz, t, d = x.shape
    m = bsz * t
    x2 = x.reshape(m, d)
    zb = jnp.zeros((1, LANES), F32)
    spec_a = [('hmt', 0, NSA_Q, NSA_HEAD_DIM ** -0.5 * LOG2_E)]
    spec_a += [('f32t', NSA_Q + j * NSA_KV, NSA_Q + (j + 1) * NSA_KV, 1.0) for j in range(6)]
    spec_a += [('hm' if j % 2 == 0 else 'hmvt', NSA_Q + j * NSA_KV, NSA_Q + (j + 1) * NSA_KV, 1.0)
               for j in range(2, 6)]
    q_t, r0, r1, r2, r3, r4, r5, ks_hm, vs_t, kw_hm, vw_t = _proj(x2, w['norm1'], w['w_a'], zb, spec_a, PROJ_TM,
                                                                   seq_len=t)
    spec_b = [('f32', 0, _GQ, 1.0), ('f32', _GQ, 2 * _GQ, 1.0), ('bf16', 2 * _GQ, 2 * _GQ + _GV, 1.0),
              ('f32', 2 * _GQ + _GV, 2 * _GQ + 2 * _GV, 1.0),
              ('misc', 2 * _GQ + 2 * _GV, 2 * _GQ + 2 * _GV + LANES, 1.0), ('gates', 0, 0, 1.0)]
    q_l, k_l, v_l, r_l, misc, gates_hm = _proj(x2, w['norm1'], w['w_b'], w['b_misc'], spec_b, PROJ_TM)

    n_pages = t // PAGE_SIZE
    ident = jnp.broadcast_to(jnp.arange(n_pages, dtype=jnp.int32), (bsz, n_pages))
    kcmp, vcmp = _compress(ident, r0, r1, w['cmp'], paged=False)
    o_a = _nsa_prompt(q_t, ks_hm, vs_t, kw_hm, vw_t, kcmp, vcmp, gates_hm, bsz, t)
    o_b, s_new = _gla_prompt(q_l, k_l, v_l, r_l, misc, w['gla_wa'], w['gla_ba'], w['gla_ng'], bsz, t)
    def token_major(a):
        return a.reshape(bsz, NSA_KV_HEADS, NSA_HEAD_DIM, a.shape[-1]).transpose(0, 3, 1, 2)

    rows = [token_major(a) for a in (r0, r1, r2, r3)]
    n_keep = min(WINDOW, t)
    wins = [token_major(a[:, :, t - n_keep:]) for a in (r4, r5)]
    return (o_a, o_b, x2), rows, wins, s_new


def _channel_mix(prompt, sample, w):
    (oa_p, ob_p, x_p), (oa_s, ob_s, x_s) = prompt, sample
    m = x_p.shape[0]
    x1_p, x1_s = _merge(oa_p, ob_p, x_p, oa_s, ob_s, x_s, w['norm1'], w['w_gate'], w['w_br_a'], w['w_br_b'],
                        w['w_o'], MERGE_TM)
    tm = next(c for c in (MLP_TM, PROJ_TM, MERGE_TM) if m % c == 0)
    return _mlp(x1_p, x1_s, w['norm2'], w['w_up'], w['w_down'], w['norm_f'], tm, MLP_TF)


def _group_rows(parts):
    rowg = lax.broadcasted_iota(jnp.int32, parts[0].shape, 0) // NSA_HPG
    out = parts[0]
    for g in range(1, NSA_KV_HEADS):
        out = jnp.where(rowg == g, parts[g], out)
    return out


def _sample_select_kernel(q_ref, kc_ref, vc_ref, oc_ref, idx_ref, *, n_cmp, pos, n_pick):
    q = q_ref[0]
    nt = (((1,), (1,)), ((), ()))
    n_chunk = kc_ref.shape[2]
    s = _group_rows([lax.dot_general(q, kc_ref[0, g], nt, preferred_element_type=F32)
                     for g in range(NSA_KV_HEADS)])
    lane = lax.broadcasted_iota(jnp.int32, s.shape, 1)
    mask = (lane < n_cmp) & (lane * CMP_STRIDE + (CMP_BLOCK - 1) <= pos)
    e, l = _softmax_rows(s, mask)
    p = e * _safe_inv(l)
    pb = p.astype(BF16)
    oc_ref[0] = _group_rows([lax.dot_general(pb, vc_ref[0, g], nt, preferred_element_type=F32)
                             for g in range(NSA_KV_HEADS)])
    nr = p.shape[0]
    y = p + pltpu.roll(p, nr - 1, axis=0)
    psum = y + pltpu.roll(y, nr - 2, axis=0)
    imp = pltpu.roll(psum, 1, axis=1) + psum
    r = SLC_BLOCK // CMP_STRIDE
    for o in range(1, r):
        imp = imp + pltpu.roll(psum, n_chunk - o, axis=1)
    blk = lane // r
    is_blk = lane % r == 0
    valid = blk * SLC_BLOCK <= pos
    forced = (blk == 0) | (blk == pos // SLC_BLOCK)
    score = jnp.where(is_blk & valid, imp + jnp.where(forced, FORCE, 0.0), NEG)
    lane_f = lane.astype(F32)
    out_lane = lax.broadcasted_iota(jnp.int32, (nr, LANES), 1)
    picked = jnp.zeros((nr, LANES), F32)
    for k in range(n_pick):
        mx = jnp.max(score, axis=-1, keepdims=True)
        ix = jnp.min(jnp.where(score == mx, lane_f, float(n_chunk)), axis=-1, keepdims=True)
        picked = jnp.where(out_lane == k, ix, picked)
        score = jnp.where(lane_f == ix, 2.0 * NEG, score)
    idx_ref[0] = picked.astype(jnp.int32) // r


def _sample_select(q_s, kcmp, vcmp, pos, n_pick):
    bsz, g, n_chunk, dh = kcmp.shape
    nh = NSA_HEADS
    cmp_spec = pl.BlockSpec((1, g, n_chunk, dh), lambda b: (b, 0, 0, 0))
    cmpt_spec = pl.BlockSpec((1, g, dh, n_chunk), lambda b: (b, 0, 0, 0))
    return pl.pallas_call(
        functools.partial(_sample_select_kernel, n_cmp=n_chunk - 1, pos=pos, n_pick=n_pick),
        grid=(bsz,),
        in_specs=[pl.BlockSpec((1, nh, dh), lambda b: (b, 0, 0)), cmp_spec, cmpt_spec],
        out_specs=[pl.BlockSpec((1, nh, dh), lambda b: (b, 0, 0)), pl.BlockSpec((1, nh, LANES), lambda b: (b, 0, 0))],
        out_shape=[SDS((bsz, nh, dh), F32), SDS((bsz, nh, LANES), jnp.int32)],
        compiler_params=_params("parallel"),
    )(q_s, kcmp, vcmp)


def _sample_attend_kernel(idx_ref, pt_ref, q_ref, oc_ref, ks_hbm, vs_hbm, kw_ref, vw_ref,
                          nks_ref, nvs_ref, nkw_ref, nvw_ref, gate_ref, o_ref, ksel, vsel, sem, *, n_pick):
    b = pl.program_id(0)
    g_n, dh = NSA_KV_HEADS, NSA_HEAD_DIM
    half = PAGE_SIZE // SLC_BLOCK

    def copies():
        out = []
        for g in range(g_n):
            for r in range(n_pick):
                page = pt_ref[b, idx_ref[b * g_n + g, r] // half]
                out.append(pltpu.make_async_copy(ks_hbm.at[page, g], ksel.at[g, r], sem.at[0]))
                out.append(pltpu.make_async_copy(vs_hbm.at[page, g], vsel.at[g, r], sem.at[1]))
        return out

    for c in copies():
        c.start()

    q = q_ref[0]
    qf = q.astype(F32)
    nt = (((1,), (1,)), ((), ()))

    def attend(keys_t, vals_t, bias, k_new, v_new):
        s = _group_rows([jnp.dot(q, keys_t[g].astype(BF16), preferred_element_type=F32)
                         + (0.0 if bias is None else bias[g]) for g in range(g_n)])
        s_new = _group_rows([jnp.sum(qf * k_new[:, g * dh:(g + 1) * dh], axis=-1, keepdims=True)
                             for g in range(g_n)])
        m = jnp.maximum(jnp.max(s, axis=-1, keepdims=True), s_new)
        e = jnp.exp(s - m)
        e_new = jnp.exp(s_new - m)
        l = jnp.sum(e, axis=-1, keepdims=True) + e_new
        eb = e.astype(BF16)
        acc = _group_rows([lax.dot_general(eb, vals_t[g].astype(BF16), nt, preferred_element_type=F32)
                           + e_new * v_new[:, g * dh:(g + 1) * dh] for g in range(g_n)])
        return acc / l

    o_w = attend([kw_ref[0, g] for g in range(g_n)], [vw_ref[0, g] for g in range(g_n)], None,
                 nkw_ref[0], nvw_ref[0])
    for c in copies():
        c.wait()
    lin = lax.broadcasted_iota(jnp.int32, (1, PAGE_SIZE), 1)
    bias = []
    for g in range(g_n):
        parts = []
        for r in range(n_pick):
            off = (idx_ref[b * g_n + g, r] % half) * SLC_BLOCK
            parts.append(jnp.where((lin >= off) & (lin < off + SLC_BLOCK), 0.0, NEG))
        bias.append(jnp.concatenate(parts, axis=1))

    def tiles(buf, g):
        return jnp.concatenate([buf[g, r] for r in range(n_pick)], axis=1)

    o_s = attend([tiles(ksel, g) for g in range(g_n)], [tiles(vsel, g) for g in range(g_n)], bias,
                 nks_ref[0], nvs_ref[0])

    gt = jnp.broadcast_to(gate_ref[0], (LANES, LANES)).T
    nh = NSA_HEADS
    o_ref[0] = (gt[0:nh, 0:dh] * oc_ref[0] + gt[nh:2 * nh, 0:dh] * o_s + gt[2 * nh:3 * nh, 0:dh] * o_w)


def _sample_attend(idx, page_table, q_s, o_c, slc_k, slc_v, win_k, win_v, new_rows, gates, n_pick):
    bsz = page_table.shape[0]
    nh, dh, g = NSA_HEADS, NSA_HEAD_DIM, NSA_KV_HEADS
    wlen = win_k.shape[-1]
    row_spec = pl.BlockSpec((1, 1, NSA_KV), lambda b, *_: (b, 0, 0))
    win_spec = pl.BlockSpec((1, g, dh, wlen), lambda b, *_: (b, 0, 0, 0))
    head_spec = pl.BlockSpec((1, nh, dh), lambda b, *_: (b, 0, 0))
    any_spec = pl.BlockSpec(memory_space=pl.ANY)
    grid_spec = pltpu.PrefetchScalarGridSpec(
        num_scalar_prefetch=2, grid=(bsz,),
        in_specs=[head_spec, head_spec, any_spec, any_spec, win_spec, win_spec,
                  row_spec, row_spec, row_spec, row_spec, pl.BlockSpec((1, 1, LANES), lambda b, *_: (b, 0, 0))],
        out_specs=head_spec,
        scratch_shapes=[pltpu.VMEM((g, n_pick, dh, PAGE_SIZE), F32), pltpu.VMEM((g, n_pick, dh, PAGE_SIZE), F32),
                        pltpu.SemaphoreType.DMA((2,))])
    return pl.pallas_call(
        functools.partial(_sample_attend_kernel, n_pick=n_pick),
        grid_spec=grid_spec, out_shape=SDS((bsz, nh, dh), F32),
        compiler_params=_params("arbitrary"),
    )(idx, page_table, q_s, o_c, slc_k, slc_v, win_k, win_v, *new_rows, gates)


def _gla_sample_kernel(q_ref, k_ref, v_ref, r_ref, misc_ref, wa_ref, ba_ref, ng_ref, s0_ref, o_ref, s_ref):
    dk, dv = GLA_DK, GLA_DV
    lr = jnp.broadcast_to(misc_ref[0][:, N_GATE:N_GATE + GLA_RANK], (16, GLA_RANK)).astype(BF16)
    x = jnp.dot(lr, wa_ref[...], preferred_element_type=F32)[0:1] + ba_ref[...]
    g_all = jax.nn.log_sigmoid(x) / GLA_TAU

    def col(v):
        t = jnp.broadcast_to(v, (dk, dk)).T
        return jnp.concatenate([t] * (dv // dk), axis=1)

    outs = []
    for h in range(GLA_HEADS):
        g = g_all[:, h * dk:(h + 1) * dk]
        q = q_ref[0][:, h * dk:(h + 1) * dk] * (GLA_DK ** -0.5)
        k = k_ref[0][:, h * dk:(h + 1) * dk]
        v = v_ref[0][:, h * dv:(h + 1) * dv]
        s0 = s0_ref[0, h]
        q_t = q * jnp.exp(g)
        k_t = k * jnp.exp(-g)
        a = jnp.sum(q_t * k_t, axis=-1, keepdims=True)
        o = jnp.sum(col(q_t) * s0, axis=0, keepdims=True) + a * v
        s_ref[0, h] = col(jnp.exp(g)) * s0 + col(k) * v
        outs.append(_rms(o, ng_ref[...]) * jax.nn.silu(r_ref[0][:, h * dv:(h + 1) * dv]))
    o_ref[0] = jnp.concatenate(outs, axis=-1)


def _gla_sample(q_l, k_l, v_l, r_l, misc, wa, ba, ng, s0):
    bsz, h, dk, dv = s0.shape

    def row(n):
        return pl.BlockSpec((1, 1, n), lambda b: (b, 0, 0))

    st_spec = pl.BlockSpec((1, h, dk, dv), lambda b: (b, 0, 0, 0))
    return pl.pallas_call(
        _gla_sample_kernel, grid=(bsz,),
        in_specs=[row(h * dk), row(h * dk), row(h * dv), row(h * dv), row(LANES),
                  _resident(wa.shape), _resident(ba.shape), _resident(ng.shape), st_spec],
        out_specs=[row(h * dv), st_spec],
        out_shape=[SDS((bsz, 1, h * dv), F32), SDS((bsz, h, dk, dv), F32)],
        compiler_params=_params("parallel"),
    )(q_l, k_l, v_l, r_l, misc, wa, ba, ng, s0)


SAMPLE_ROWS = 16


def _layer_sample(x, caches, wins, s0, page_table, w):
    bsz, t, d = x.shape
    n_pages = page_table.shape[1]
    pos = n_pages * PAGE_SIZE
    assert t == 1 and bsz <= SAMPLE_ROWS and pos % SLC_BLOCK == 0 and wins[0].shape[1] == WINDOW
    assert pos // SLC_BLOCK >= N_SELECT
    mp = SAMPLE_ROWS
    x2 = jnp.pad(x.reshape(bsz, d), ((0, mp - bsz), (0, 0)))
    zb = jnp.zeros((1, LANES), F32)
    spec_a = [('bf16', 0, NSA_Q, NSA_HEAD_DIM ** -0.5)]
    spec_a += [('f32', NSA_Q + j * NSA_KV, NSA_Q + (j + 1) * NSA_KV, 1.0) for j in range(6)]
    q_s, r0, r1, r2, r3, r4, r5 = _proj(x2, w['norm1'], w['w_a'], zb, spec_a, mp)
    spec_b = [('f32', 0, _GQ, 1.0), ('f32', _GQ, 2 * _GQ, 1.0), ('f32', 2 * _GQ, 2 * _GQ + _GV, 1.0),
              ('f32', 2 * _GQ + _GV, 2 * _GQ + 2 * _GV, 1.0),
              ('misc', 2 * _GQ + 2 * _GV, 2 * _GQ + 2 * _GV + LANES, 1.0)]
    q_l, k_l, v_l, r_l, misc = _proj(x2, w['norm1'], w['w_b'], w['b_misc'], spec_b, mp)

    cache_t = [a.transpose(0, 2, 3, 1) for a in caches]
    win_t = [a.transpose(0, 2, 3, 1) for a in wins]
    n_pool = caches[0].shape[0]
    cmp_pages = [a.reshape(n_pool, NSA_KV, PAGE_SIZE) for a in cache_t[:2]]
    kcmp, vcmp = _compress(page_table, cmp_pages[0], cmp_pages[1], w['cmp'], paged=True)
    q_h = q_s.reshape(mp, NSA_HEADS, NSA_HEAD_DIM)
    n_pick = N_SELECT - 1
    o_c, idx = _sample_select(q_h, kcmp, vcmp, pos, n_pick)
    idx = idx[:, ::NSA_HPG, :n_pick].reshape(bsz * NSA_KV_HEADS, n_pick)
    gates = misc[:, :N_GATE].reshape(mp, NSA_KV_HEADS, 3, NSA_HPG).transpose(0, 2, 1, 3).reshape(mp, 1, N_GATE)
    gates = jnp.pad(gates, ((0, 0), (0, 0), (0, LANES - N_GATE)))
    new_rows = [a.reshape(mp, 1, NSA_KV) for a in (r2, r3, r4, r5)]
    o_a = _sample_attend(idx, page_table, q_h, o_c, cache_t[2], cache_t[3], win_t[0], win_t[1], new_rows, gates,
                         n_pick)
    o_a = jnp.pad(o_a.reshape(bsz, NSA_Q), ((0, mp - bsz), (0, 0))).astype(BF16)

    def r3d(a):
        return a.reshape(mp, 1, a.shape[-1])

    o_b, s_new = _gla_sample(r3d(q_l), r3d(k_l), r3d(v_l), r3d(r_l), r3d(misc), w['gla_wa'], w['gla_ba'],
                             w['gla_ng'], s0)
    o_b = jnp.pad(o_b.reshape(bsz, _GV), ((0, mp - bsz), (0, 0))).astype(BF16)
    kvh = (bsz, 1, NSA_KV_HEADS, NSA_HEAD_DIM)
    rows = [a[:bsz].reshape(kvh) for a in (r0, r1, r2, r3)]
    new_wins = [jnp.concatenate([c[:, 1:], a[:bsz].reshape(kvh)], axis=1) for c, a in zip(wins, (r4, r5))]
    return (o_a, o_b, x2), rows, new_wins, s_new


def kernel(x_prompt, x_sample, cache_cmp_k, cache_cmp_v, cache_slc_k, cache_slc_v, cache_win_k, cache_win_v, state_gla, page_table, norm1_g, w_in, b_nsa_gate, cmp_pe_k, cmp_pe_v, cmp_k_w1, cmp_k_w2, cmp_v_w1, cmp_v_w2, gla_w_a2, gla_b_a, gla_norm_g, w_br_a, w_br_b, w_o, norm2_g, w_up, w_down, norm_f):
    assert DEPTH == 1 and norm1_g.shape[0] == 1
    w = _prep_weights(norm1_g[0], w_in[0], b_nsa_gate[0], cmp_pe_k[0], cmp_pe_v[0], cmp_k_w1[0], cmp_k_w2[0],
                      cmp_v_w1[0], cmp_v_w2[0], gla_w_a2[0], gla_b_a[0], gla_norm_g[0], w_br_a[0], w_br_b[0],
                      w_o[0], norm2_g[0], w_up[0], w_down[0], norm_f)
    mix_p, rows_p, wins_p, s_p = _layer_prompt(x_prompt, w)
    caches = [c[0] for c in (cache_cmp_k, cache_cmp_v, cache_slc_k, cache_slc_v)]
    mix_s, rows_s, wins_s, s_s = _layer_sample(x_sample, caches, [cache_win_k[0], cache_win_v[0]], state_gla[0],
                                               page_table, w)
    y_p, y_s = _channel_mix(mix_p, mix_s, w)
    y_p = y_p.reshape(x_prompt.shape)
    y_s = y_s[:x_sample.shape[0]].reshape(x_sample.shape)
    outs_p = [a[None] for a in rows_p + wins_p + [s_p]]
    outs_s = [a[None] for a in rows_s + wins_s + [s_s]]
    return (y_p, y_s, *outs_p, *outs_s)
```

```python
import functools

import jax
import jax.numpy as jnp
from jax import lax
from jax.experimental import pallas as pl
from jax.experimental.pallas import tpu as pltpu

D_MODEL = 2048
DEPTH = 1
PAGE_SIZE = 128
NSA_HEADS = 16
NSA_KV_HEADS = 4
NSA_HPG = NSA_HEADS // NSA_KV_HEADS
NSA_HEAD_DIM = 64
NSA_Q = NSA_HEADS * NSA_HEAD_DIM
NSA_KV = NSA_KV_HEADS * NSA_HEAD_DIM
CMP_STRIDE = 16
CMP_BLOCK = 32
CMP_HIDDEN = 128
SLC_BLOCK = 64
N_SELECT = 16
WINDOW = 512
GLA_HEADS = 4
GLA_DK = (D_MODEL // 4) // GLA_HEADS
GLA_DV = (D_MODEL // 2) // GLA_HEADS
GLA_RANK = 16
GLA_TAU = 16.0
D_FF = 4 * D_MODEL
EPS = 1e-6
NEG = -1e30
FORCE = 1e4
SPLITS = (NSA_Q, 6 * NSA_KV, 3 * NSA_HEADS,
          GLA_HEADS * GLA_DK, GLA_HEADS * GLA_DK, GLA_HEADS * GLA_DV, GLA_HEADS * GLA_DV,
          GLA_RANK, 2 * D_MODEL)

F32 = jnp.float32
BF16 = jnp.bfloat16
LANES = 128
VMEM_LIMIT_BYTES = 56 * 1024 * 1024
MLP_VMEM_LIMIT_BYTES = 60 * 1024 * 1024
N_GATE = 3 * NSA_HEADS
GLA_CHUNK = 32
GLA_UNROLL = 8
GLA_GROUP = 4
NSA_TQ = 256
NSA_TK = 256
VT_ROWS = NSA_HEAD_DIM + 16
CMP_PAGES = 16
PROJ_TM = 512
MERGE_TM = 256
MLP_TM = 1024
MLP_TF = 512
SDS = jax.ShapeDtypeStruct


def _params(*sem, vmem_limit_bytes=VMEM_LIMIT_BYTES):
    return pltpu.CompilerParams(dimension_semantics=sem, vmem_limit_bytes=vmem_limit_bytes)


def _resident(shape):
    nd = len(shape)
    return pl.BlockSpec(shape, lambda *_: (0,) * nd, pipeline_mode=pl.Buffered(1))


def _rms(x, g):
    return x * lax.rsqrt(jnp.mean(x * x, axis=-1, keepdims=True) + EPS) * g


def _proj_kernel(x_ref, g_ref, w_ref, b_ref, *out_refs, spec):
    h = _rms(x_ref[...], g_ref[...]).astype(BF16)
    misc = None
    for o_ref, (kind, c0, c1, scale) in zip(out_refs, spec):
        if kind == 'gates':
            for g in range(NSA_KV_HEADS):
                o_ref[g] = misc if g == 0 else pltpu.roll(misc, LANES - g * 3 * NSA_HPG, axis=1)
            continue
        r = jnp.dot(h, w_ref[:, c0:c1], preferred_element_type=F32)
        if scale != 1.0:
            r = r * scale
        if kind == 'f32':
            o_ref[...] = r
        elif kind == 'f32t':
            o_ref[0] = r.T
        elif kind == 'bf16':
            o_ref[...] = r.astype(BF16)
        elif kind == 'sigmoid':
            o_ref[...] = jax.nn.sigmoid(r)
        elif kind == 'hm':
            for i in range((c1 - c0) // NSA_HEAD_DIM):
                o_ref[i] = r[:, i * NSA_HEAD_DIM:(i + 1) * NSA_HEAD_DIM].astype(BF16)
        elif kind == 'hmt':
            rt = r.T
            for i in range((c1 - c0) // NSA_HEAD_DIM):
                o_ref[i] = rt[i * NSA_HEAD_DIM:(i + 1) * NSA_HEAD_DIM].astype(BF16)
        elif kind == 'hmvt':
            rt = r.T
            sub = lax.broadcasted_iota(jnp.int32, (VT_ROWS - NSA_HEAD_DIM, r.shape[0]), 0)
            ones = jnp.where(sub == 0, 1.0, 0.0)
            for i in range((c1 - c0) // NSA_HEAD_DIM):
                piece = rt[i * NSA_HEAD_DIM:(i + 1) * NSA_HEAD_DIM]
                o_ref[i] = jnp.concatenate([piece, ones], axis=0).astype(BF16)
        elif kind == 'misc':
            lane = lax.broadcasted_iota(jnp.int32, r.shape, 1)
            misc = jnp.where(lane < N_GATE, jax.nn.sigmoid(r + b_ref[...]), r)
            o_ref[...] = misc


def _proj(x, norm_g, w, bias, spec, tm, seq_len=None):
    m, d = x.shape
    n = w.shape[1]
    assert m % tm == 0
    out_shape, out_specs = [], []
    for kind, c0, c1, _ in spec:
        if kind == 'f32t':
            assert seq_len % tm == 0
            per_seq = seq_len // tm
            out_shape.append(SDS((m // seq_len, c1 - c0, seq_len), F32))
            out_specs.append(pl.BlockSpec((1, c1 - c0, tm), lambda i: (i // per_seq, 0, i % per_seq)))
        elif kind == 'hm':
            nh = (c1 - c0) // NSA_HEAD_DIM
            out_shape.append(SDS((nh, m, NSA_HEAD_DIM), BF16))
            out_specs.append(pl.BlockSpec((nh, tm, NSA_HEAD_DIM), lambda i: (0, i, 0)))
        elif kind in ('hmt', 'hmvt'):
            nh = (c1 - c0) // NSA_HEAD_DIM
            nrow = NSA_HEAD_DIM if kind == 'hmt' else VT_ROWS
            out_shape.append(SDS((nh, nrow, m), BF16))
            out_specs.append(pl.BlockSpec((nh, nrow, tm), lambda i: (0, 0, i)))
        elif kind == 'gates':
            out_shape.append(SDS((NSA_KV_HEADS, m, LANES), F32))
            out_specs.append(pl.BlockSpec((NSA_KV_HEADS, tm, LANES), lambda i: (0, i, 0)))
        else:
            out_shape.append(SDS((m, c1 - c0), BF16 if kind == 'bf16' else F32))
            out_specs.append(pl.BlockSpec((tm, c1 - c0), lambda i: (i, 0)))
    return pl.pallas_call(
        functools.partial(_proj_kernel, spec=tuple(spec)),
        grid=(m // tm,),
        in_specs=[pl.BlockSpec((tm, d), lambda i: (i, 0)), _resident((1, d)), _resident((d, n)),
                  _resident((1, LANES))],
        out_specs=out_specs, out_shape=out_shape,
        compiler_params=_params("parallel"),
    )(x, norm_g, w, bias)


def _cmp_kernel(pt_ref, k_hbm, v_hbm, perm_ref, pek_ref, pev_ref, w1k_ref, w1v_ref, w2k_ref, w2v_ref,
                ok_ref, ov_ref, kbuf, vbuf, hk, hv, sem, *, n_pages_step, paged):
    b = pl.program_id(0)
    s = pl.program_id(1)
    ns = pl.num_programs(1)
    t = b * ns + s
    total = pl.num_programs(0) * ns
    rows = n_pages_step * 8
    dh, hid_n = NSA_HEAD_DIM, CMP_HIDDEN

    def copies(tt, slot):
        bb = tt // ns
        ss = tt % ns
        out = []
        for p in range(n_pages_step):
            page = pt_ref[bb, ss * n_pages_step + p]
            if paged:
                src = [hbm.at[page] for hbm in (k_hbm, v_hbm)]
            else:
                tok = pl.ds(pl.multiple_of(page * PAGE_SIZE, PAGE_SIZE), PAGE_SIZE)
                src = [hbm.at[bb, :, tok] for hbm in (k_hbm, v_hbm)]
            out.append(pltpu.make_async_copy(src[0], kbuf.at[slot, p], sem.at[slot, 0]))
            out.append(pltpu.make_async_copy(src[1], vbuf.at[slot, p], sem.at[slot, 1]))
        return out

    slot = t % 2

    @pl.when(t == 0)
    def _():
        for c in copies(t, slot):
            c.start()

    @pl.when(t + 1 < total)
    def _():
        for c in copies(t + 1, 1 - slot):
            c.start()

    for c in copies(t, slot):
        c.wait()

    r0 = pl.multiple_of(s * rows, rows)
    low = lax.broadcasted_iota(jnp.int32, (8, LANES), 1) < dh
    tok_pairs = CMP_STRIDE // 2
    nt = (((1,), (1,)), ((), ()))
    perm = perm_ref[...]
    for buf, w1_ref, h_ref in ((kbuf, w1k_ref, hk), (vbuf, w1v_ref, hv)):
        parts = [[[] for _ in range(tok_pairs)] for _ in range(NSA_KV_HEADS)]
        for p in range(n_pages_step):
            r = lax.dot_general(perm, buf[slot, p].astype(BF16), nt, preferred_element_type=F32)
            for q in range(2):
                for m in range(tok_pairs):
                    x0 = r[2 * m * 8:2 * m * 8 + 8, q * LANES:(q + 1) * LANES]
                    x1 = r[(2 * m + 1) * 8:(2 * m + 1) * 8 + 8, q * LANES:(q + 1) * LANES]
                    parts[2 * q][m].append(jnp.where(low, x0, pltpu.roll(x1, dh, axis=1)))
                    parts[2 * q + 1][m].append(jnp.where(low, pltpu.roll(x0, dh, axis=1), x1))
        for g in range(NSA_KV_HEADS):
            z = jnp.concatenate([jnp.concatenate(parts[g][m], axis=0) for m in range(tok_pairs)],
                                axis=1).astype(BF16)
            h_ref[pl.ds(r0, rows), g * 2 * hid_n:(g + 1) * 2 * hid_n] = jnp.dot(
                z, w1_ref[...], preferred_element_type=F32)

    @pl.when(s == ns - 1)
    def _():
        for h_ref, pe_ref, w1_ref, w2_ref, o_ref in ((hk, pek_ref, w1k_ref, w2k_ref, ok_ref),
                                                     (hv, pev_ref, w1v_ref, w2v_ref, ov_ref)):
            n_chunk = h_ref.shape[0]
            pe = jnp.broadcast_to(pe_ref[...], (2, 16, pe_ref.shape[2])).astype(BF16)
            pe_term = (jnp.dot(pe[0], w1_ref[:, 0:hid_n], preferred_element_type=F32)
                       + jnp.dot(pe[1], w1_ref[:, hid_n:2 * hid_n], preferred_element_type=F32))[0:1]
            hid = []
            for g in range(NSA_KV_HEADS):
                first = h_ref[:, g * 2 * hid_n:g * 2 * hid_n + hid_n]
                last = h_ref[:, g * 2 * hid_n + hid_n:(g + 1) * 2 * hid_n]
                hid.append(first + pltpu.roll(last, n_chunk - 1, axis=0) + pe_term)
            hid = jnp.concatenate(hid, axis=1)
            res = jnp.dot(jax.nn.gelu(hid).astype(BF16), w2_ref[...], preferred_element_type=F32)
            if o_ref is ok_ref:
                for g in range(NSA_KV_HEADS):
                    o_ref[0, g] = res[:, g * NSA_HEAD_DIM:(g + 1) * NSA_HEAD_DIM].astype(BF16)
            else:
                res_t = res.T
                for g in range(NSA_KV_HEADS):
                    o_ref[0, g] = res_t[g * NSA_HEAD_DIM:(g + 1) * NSA_HEAD_DIM].astype(BF16)


def _cmp_weights(pe, w1, w2):
    g = NSA_KV_HEADS
    half = CMP_STRIDE * NSA_HEAD_DIM
    pe_t = pe.reshape(2, 1, half)
    w1ab = jnp.concatenate([w1[:half], w1[half:]], axis=1)
    w2b = jnp.einsum('jd,gh->gjhd', w2, jnp.eye(g, dtype=F32)).reshape(g * CMP_HIDDEN, g * NSA_HEAD_DIM)
    return pe_t, w1ab.astype(BF16), w2b.astype(BF16)


def _compress(page_table, k_src, v_src, cw, paged):
    bsz, n_pages = page_table.shape
    p_step = min(CMP_PAGES, n_pages)
    ns = n_pages // p_step
    n_chunk = n_pages * 8
    page_shape = (NSA_KV, PAGE_SIZE)
    assert NSA_KV == 2 * LANES
    assert k_src.shape[1:] == (page_shape if paged else (NSA_KV, n_pages * PAGE_SIZE))
    half = CMP_STRIDE * NSA_HEAD_DIM
    gh = NSA_KV_HEADS * CMP_HIDDEN
    out_row = jnp.arange(PAGE_SIZE, dtype=jnp.int32)
    src_tok = (out_row % 8) * CMP_STRIDE + out_row // 8
    perm = (src_tok[:, None] == jnp.arange(PAGE_SIZE, dtype=jnp.int32)[None, :]).astype(BF16)
    k_sds =SDS((bsz, NSA_KV_HEADS, n_chunk, NSA_HEAD_DIM), BF16)
    v_sds = SDS((bsz, NSA_KV_HEADS, NSA_HEAD_DIM, n_chunk), BF16)
    k_spec = pl.BlockSpec((1, NSA_KV_HEADS, n_chunk, NSA_HEAD_DIM), lambda b, s, pt: (b, 0, 0, 0))
    v_spec = pl.BlockSpec((1, NSA_KV_HEADS, NSA_HEAD_DIM, n_chunk), lambda b, s, pt: (b, 0, 0, 0))
    grid_spec = pltpu.PrefetchScalarGridSpec(
        num_scalar_prefetch=1, grid=(bsz, ns),
        in_specs=[pl.BlockSpec(memory_space=pl.ANY), pl.BlockSpec(memory_space=pl.ANY),
                  _resident((PAGE_SIZE, PAGE_SIZE)), _resident((2, 1, half)), _resident((2, 1, half)),
                  _resident((half, 2 * CMP_HIDDEN)), _resident((half, 2 * CMP_HIDDEN)),
                  _resident((gh, NSA_KV)), _resident((gh, NSA_KV))],
        out_specs=[k_spec, v_spec],
        scratch_shapes=[pltpu.VMEM((2, p_step) + page_shape, F32), pltpu.VMEM((2, p_step) + page_shape, F32),
                        pltpu.VMEM((n_chunk, 2 * gh), F32), pltpu.VMEM((n_chunk, 2 * gh), F32),
                        pltpu.SemaphoreType.DMA((2, 2))])
    return pl.pallas_call(
        functools.partial(_cmp_kernel, n_pages_step=p_step, paged=paged),
        grid_spec=grid_spec, out_shape=[k_sds, v_sds],
        compiler_params=_params("arbitrary", "arbitrary"),
    )(page_table, k_src, v_src, perm, cw['pe_k'], cw['pe_v'], cw['w1_k'], cw['w1_v'], cw['w2_k'], cw['w2_v'])


def _softmax_rows(s, mask):
    sm = jnp.where(mask, s, NEG)
    m = jnp.max(sm, axis=-1, keepdims=True)
    e = jnp.where(mask, jnp.exp(sm - m), 0.0)
    return e, jnp.sum(e, axis=-1, keepdims=True)


def _safe_inv(l):
    return jnp.where(l > 0.0, 1.0 / jnp.where(l > 0.0, l, 1.0), 0.0)


M_INIT = -1e20
LOG2_E = 1.4426950408889634


def _nsa_prompt_kernel(qt_ref, kc_ref, vct_ref, ks_ref, vst_ref, kw_ref, vwt_ref, gate_ref,
                       o_ref, *scratch, n_cmp, n_slc, n_sel):
    tq, tk, hpg, dh = NSA_TQ, NSA_TK, NSA_HPG, NSA_HEAD_DIM
    rk_refs = scratch[:tq // LANES]
    selb_ref, m_ref, acc_ref = scratch[tq // LANES:tq // LANES + 3]
    s_refs, p_refs, a_refs = (scratch[tq // LANES + 3 + 2 * i:tq // LANES + 5 + 2 * i] for i in range(3))
    qi = pl.program_id(2)
    q0 = qi * tq
    n_chunk = kc_ref.shape[2]
    r = SLC_BLOCK // CMP_STRIDE
    n_row = n_chunk // r

    ci = lax.broadcasted_iota(jnp.int32, (n_chunk, tq), 0)
    pos_c = q0 + lax.broadcasted_iota(jnp.int32, (n_chunk, tq), 1)
    m_c = (ci < n_cmp) & (ci * CMP_STRIDE + (CMP_BLOCK - 1) <= pos_c)
    kc = kc_ref[0, 0]
    vct = vct_ref[0, 0]
    o_c = []
    psum = None
    for h in range(hpg):
        s = jnp.where(m_c, jnp.dot(kc, qt_ref[0, h], preferred_element_type=F32), NEG)
        e = jnp.where(m_c, jnp.exp2(s - jnp.max(s, axis=0, keepdims=True)), 0.0)
        p = e * _safe_inv(jnp.sum(e, axis=0, keepdims=True))
        o_c.append(jnp.dot(vct, p.astype(BF16), preferred_element_type=F32))
        psum = p if psum is None else psum + p

    imp = pltpu.roll(psum, 1, axis=0) + psum
    for o in range(1, r):
        imp = imp + pltpu.roll(psum, n_chunk - o, axis=0)
    parts = []
    for i, rk_ref in enumerate(rk_refs):
        rk_ref[...] = imp[:, i * LANES:(i + 1) * LANES]
        parts.append(rk_ref[pl.ds(0, n_row, stride=r), :])
    imp_b = jnp.concatenate(parts, axis=1)
    j_io = lax.broadcasted_iota(jnp.int32, (n_row, tq), 0)
    pos_b = q0 + lax.broadcasted_iota(jnp.int32, (n_row, tq), 1)
    valid = (j_io < n_slc) & (j_io * SLC_BLOCK <= pos_b)
    forced = (j_io == 0) | (j_io == pos_b // SLC_BLOCK)
    sc = jnp.where(valid, imp_b + jnp.where(forced, FORCE, 0.0), NEG)
    rank = jnp.zeros((n_row, tq), F32)
    for k in range(n_slc):
        ck = sc[k:k + 1, :]
        beats = (ck > sc) | ((ck == sc) & (j_io > k))
        rank = rank + jnp.where(beats, 1.0, 0.0)
    selb_ref[...] = jnp.where(rank < n_sel, 0.0, NEG)

    dlt = lax.broadcasted_iota(jnp.int32, (tk, tq), 0) - lax.broadcasted_iota(jnp.int32, (tk, tq), 1)
    blocks_per_tile = tk // SLC_BLOCK

    def flash(k_ref, vt_ref, lo, hi, bias_fn):
        n_tiles = k_ref.shape[2] // tk
        m_ref[...] = jnp.full(m_ref.shape, M_INIT, F32)
        acc_ref[...] = jnp.zeros(acc_ref.shape, F32)
        p_refs[1][...] = jnp.zeros(p_refs[1].shape, BF16)
        a_refs[1][...] = jnp.ones(a_refs[1].shape, F32)

        def tile_start(kb):
            return pl.multiple_of(jnp.clip(kb, 0, n_tiles - 1) * tk, tk)

        def scores(kb, s_ref):
            kk = k_ref[0, 0, pl.ds(tile_start(kb), tk), :]
            bias = bias_fn(kb)
            for h in range(hpg):
                for c in range(tk // SLC_BLOCK):
                    rs = slice(c * SLC_BLOCK, (c + 1) * SLC_BLOCK)
                    s_ref[h, rs, :] = jnp.dot(kk[rs], qt_ref[0, h], preferred_element_type=F32) + bias[rs]

        def softmax(s_ref, p_ref, a_ref):
            for h in range(hpg):
                for c in range(tq // LANES):
                    cs = slice(c * LANES, (c + 1) * LANES)
                    s = s_ref[h, :, cs]
                    m_prev = m_ref[h, :, cs]
                    m_new = jnp.maximum(m_prev, jnp.max(s, axis=0, keepdims=True))
                    p_ref[h, :, cs] = jnp.exp2(s - m_new).astype(BF16)
                    a_ref[h, :, cs] = jnp.exp2(m_prev - m_new)
                    m_ref[h, :, cs] = m_new

        def weighted_values(kb, p_ref, a_ref):
            vt = vt_ref[0, :, pl.ds(tile_start(kb), tk)]
            for h in range(hpg):
                acc_ref[h] = a_ref[h] * acc_ref[h] + jnp.dot(vt, p_ref[h], preferred_element_type=F32)

        scores(lo, s_refs[0])

        def body(j, carry):
            i = lo + 2 * j
            scores(i + 1, s_refs[1])
            softmax(s_refs[0], p_refs[0], a_refs[0])
            weighted_values(i - 1, p_refs[1], a_refs[1])
            scores(i + 2, s_refs[0])
            softmax(s_refs[1], p_refs[1], a_refs[1])
            weighted_values(i, p_refs[0], a_refs[0])
            return carry

        n_pairs = (hi - lo + 1) // 2
        lax.fori_loop(0, n_pairs, body, 0)
        weighted_values(lo + 2 * n_pairs - 1, p_refs[1], a_refs[1])
        return [acc_ref[h, 0:dh, :] * _safe_inv(acc_ref[h, dh:dh + 1, :]) for h in range(hpg)]

    def slc_bias(kb):
        first = jnp.minimum(kb, n_slc // blocks_per_tile - 1) * blocks_per_tile
        rows = [jnp.broadcast_to(selb_ref[pl.ds(first + i, 1), :], (SLC_BLOCK, tq))
                for i in range(blocks_per_tile)]
        return jnp.where(dlt + (kb * tk - q0) <= 0, jnp.concatenate(rows, axis=0), NEG)

    def win_bias(kb):
        d = dlt + (kb * tk - q0)
        return jnp.where((d <= 0) & (d >= -WINDOW), 0.0, NEG)

    hi = (q0 + tq - 1) // tk + 1
    o_w = flash(kw_ref, vwt_ref, jnp.maximum(q0 - WINDOW, 0) // tk, hi, win_bias)
    o_s = flash(ks_ref, vst_ref, 0, hi, slc_bias)

    gt = gate_ref[0].T
    outs = [gt[h:h + 1] * o_c[h] + gt[hpg + h:hpg + h + 1] * o_s[h] + gt[2 * hpg + h:2 * hpg + h + 1] * o_w[h]
            for h in range(hpg)]
    o_ref[...] = jnp.concatenate(outs, axis=0).T.astype(BF16)


def _nsa_prompt(q_t, ks_hm, vs_t, kw_hm, vw_t, kcmp, vcmp_t, gates_hm, bsz, t):
    g, hpg, dh, tq = NSA_KV_HEADS, NSA_HPG, NSA_HEAD_DIM, NSA_TQ
    r = SLC_BLOCK // CMP_STRIDE
    n_slc = t // SLC_BLOCK
    assert t % NSA_TK == 0 and t % tq == 0 and n_slc * r <= LANES and NSA_TK % SLC_BLOCK == 0
    nq = t // tq
    n_cmp = kcmp.shape[2] - 1
    if kcmp.shape[2] < LANES:
        fill = LANES - kcmp.shape[2]
        kcmp = jnp.pad(kcmp, ((0, 0), (0, 0), (0, fill), (0, 0)))
        vcmp_t = jnp.pad(vcmp_t, ((0, 0), (0, 0), (0, 0), (0, fill)))
    n_chunk = kcmp.shape[2]
    assert n_chunk == LANES
    q4 = q_t.reshape(g, hpg, dh, bsz * t)

    def k_spec():
        return pl.BlockSpec((1, 1, t, dh), lambda b, gg, qi: (gg, b, 0, 0))

    def vt_spec():
        return pl.BlockSpec((1, VT_ROWS, t), lambda b, gg, qi: (gg, 0, b))

    def per_bt(a):
        return a.reshape(g, bsz, t, a.shape[-1])

    return pl.pallas_call(
        functools.partial(_nsa_prompt_kernel, n_cmp=n_cmp, n_slc=n_slc, n_sel=min(N_SELECT, n_slc)),
        grid=(bsz, g, nq),
        in_specs=[pl.BlockSpec((1, hpg, dh, tq), lambda b, gg, qi: (gg, 0, 0, b * nq + qi)),
                  pl.BlockSpec((1, 1, n_chunk, dh), lambda b, gg, qi: (b, gg, 0, 0)),
                  pl.BlockSpec((1, 1, dh, n_chunk), lambda b, gg, qi: (b, gg, 0, 0)),
                  k_spec(), vt_spec(), k_spec(), vt_spec(),
                  pl.BlockSpec((1, tq, LANES), lambda b, gg, qi: (gg, b * nq + qi, 0))],
        out_specs=pl.BlockSpec((tq, hpg * dh), lambda b, gg, qi: (b * nq + qi, gg)),
        out_shape=SDS((bsz * t, NSA_Q), BF16),
        scratch_shapes=[pltpu.VMEM((n_chunk, LANES), F32)] * (tq // LANES) + [
                        pltpu.VMEM((n_chunk // r, tq), F32),
                        pltpu.VMEM((hpg, 1, tq), F32), pltpu.VMEM((hpg, VT_ROWS, tq), F32),
                        pltpu.VMEM((hpg, NSA_TK, tq), F32), pltpu.VMEM((hpg, NSA_TK, tq), F32),
                        pltpu.VMEM((hpg, NSA_TK, tq), BF16), pltpu.VMEM((hpg, NSA_TK, tq), BF16),
                        pltpu.VMEM((hpg, 1, tq), F32), pltpu.VMEM((hpg, 1, tq), F32)],
        compiler_params=_params("parallel", "parallel", "arbitrary"),
    )(q4, kcmp, vcmp_t, per_bt(ks_hm), vs_t, per_bt(kw_hm), vw_t, gates_hm)


def _gla_prompt_kernel(q_ref, k_ref, v_ref, r_ref, misc_ref, wa_ref, ba_ref, ng_ref,
                       o_ref, s_ref, qe_ref, qt_ref, kt_ref, kh_ref, d_ref, u_ref, st_ref):
    t = q_ref.shape[0]
    c = GLA_CHUNK
    n = t // c
    dk, dv = GLA_DK, GLA_DV
    lr = misc_ref[:, N_GATE:N_GATE + GLA_RANK].astype(BF16)
    x = jnp.dot(lr, wa_ref[...], preferred_element_type=F32) + ba_ref[...]
    g = jax.nn.log_sigmoid(x) / GLA_TAU
    row = lax.broadcasted_iota(jnp.int32, (t, dk), 0) % c
    b = g
    sh = 1
    while sh < c:
        b = b + jnp.where(row >= sh, pltpu.roll(b, sh, axis=0), 0.0)
        sh *= 2
    b3 = b.reshape(n, c, dk)
    b_last = jnp.broadcast_to(b3[:, c - 1:c, :], (n, c, dk)).reshape(t, dk)
    b_mid = jnp.broadcast_to(b3[:, c // 2 - 1:c // 2, :], (n, c, dk)).reshape(t, dk)
    q = q_ref[...] * (GLA_DK ** -0.5)
    k = k_ref[...]
    qe_ref[...] = (q * jnp.exp(b)).astype(BF16)
    qt_ref[...] = (q * jnp.exp(b - b_mid)).astype(BF16)
    kt_ref[...] = (k * jnp.exp(b_mid - b)).astype(BF16)
    kh_ref[...] = (k * jnp.exp(b_last - b)).astype(BF16)
    d_ref[...] = jnp.exp(b_last)
    tn = (((0,), (0,)), ((), ()))
    nt = (((1,), (1,)), ((), ()))

    def chunk_update(i, carry):
        r0 = pl.multiple_of(i * c, c)
        u_ref[i] = lax.dot_general(v_ref[pl.ds(r0, c), :], kh_ref[pl.ds(r0, c), :], tn,
                                   preferred_element_type=F32)
        return carry

    lax.fori_loop(0, n, chunk_update, 0, unroll=GLA_UNROLL)

    st_ref[...] = jnp.zeros((dv, dk), F32)

    def recur(i, carry):
        st = st_ref[...]
        st_ref[...] = st * d_ref[pl.ds(pl.multiple_of(i * c, c), 1), :] + u_ref[i]
        u_ref[i] = st
        return carry

    lax.fori_loop(0, n, recur, 0)
    s_ref[0, 0] = st_ref[...].T

    grp = GLA_GROUP
    rows = grp * c
    ri = lax.broadcasted_iota(jnp.int32, (rows, rows), 0)
    ci = lax.broadcasted_iota(jnp.int32, (rows, rows), 1)
    keep = (ri // c == ci // c) & (ri >= ci)
    ng = ng_ref[...]

    def group_out(i, carry):
        r0 = pl.multiple_of(i * rows, rows)
        rs = pl.ds(r0, rows)
        a = lax.dot_general(qt_ref[rs, :], kt_ref[rs, :], nt, preferred_element_type=F32)
        a = jnp.where(keep, a, 0.0).astype(BF16)
        o = jnp.dot(a, v_ref[rs, :], preferred_element_type=F32)
        inter = [lax.dot_general(qe_ref[pl.ds(r0 + j * c, c), :], u_ref[i * grp + j].astype(BF16), nt,
                                 preferred_element_type=F32) for j in range(grp)]
        o = o + jnp.concatenate(inter, axis=0)
        o = _rms(o, ng) * jax.nn.silu(r_ref[rs, :])
        o_ref[rs, :] = o.astype(BF16)
        return carry

    lax.fori_loop(0, n // grp, group_out, 0, unroll=2)


def _gla_prompt(q_l, k_l, v_l, r_l, misc, wa, ba, ng, bsz, t):
    h, dk, dv = GLA_HEADS, GLA_DK, GLA_DV
    assert t % (GLA_CHUNK * GLA_GROUP * 2) == 0
    n = t // GLA_CHUNK
    return pl.pallas_call(
        _gla_prompt_kernel,
        grid=(bsz, h),
        in_specs=[pl.BlockSpec((t, dk), lambda b, hh: (b, hh)), pl.BlockSpec((t, dk), lambda b, hh: (b, hh)),
                  pl.BlockSpec((t, dv), lambda b, hh: (b, hh)), pl.BlockSpec((t, dv), lambda b, hh: (b, hh)),
                  pl.BlockSpec((t, LANES), lambda b, hh: (b, 0)),
                  pl.BlockSpec((GLA_RANK, dk), lambda b, hh: (0, hh)), pl.BlockSpec((1, dk), lambda b, hh: (0, hh)),
                  _resident((1, dv))],
        out_specs=[pl.BlockSpec((t, dv), lambda b, hh: (b, hh)),
                   pl.BlockSpec((1, 1, dk, dv), lambda b, hh: (b, hh, 0, 0))],
        out_shape=[SDS((bsz * t, h * dv), BF16), SDS((bsz, h, dk, dv), F32)],
        scratch_shapes=[pltpu.VMEM((t, dk), BF16), pltpu.VMEM((t, dk), BF16), pltpu.VMEM((t, dk), BF16),
                        pltpu.VMEM((t, dk), BF16), pltpu.VMEM((t, dk), F32),
                        pltpu.VMEM((n, dv, dk), F32), pltpu.VMEM((dv, dk), F32)],
        compiler_params=_params("parallel", "parallel"),
    )(q_l, k_l, v_l, r_l, misc, wa, ba, ng)


def _merge_rows(oa, ob, x, g_ref, wg_ref, wa_ref, wb_ref, wo_ref):
    h = _rms(x, g_ref[...]).astype(BF16)
    d = D_MODEL
    ga = jax.nn.sigmoid(jnp.dot(h, wg_ref[:, 0:d], preferred_element_type=F32))
    u = ga * jnp.dot(oa, wa_ref[...], preferred_element_type=F32)
    gb = jax.nn.sigmoid(jnp.dot(h, wg_ref[:, d:2 * d], preferred_element_type=F32))
    u = u + gb * jnp.dot(ob, wb_ref[...], preferred_element_type=F32)
    return x + jnp.dot(u.astype(BF16), wo_ref[...], preferred_element_type=F32)


def _merge_kernel(oa_ref, ob_ref, x_ref, oas_ref, obs_ref, xs_ref, g_ref, wg_ref, wa_ref, wb_ref, wo_ref,
                  x1_ref, x1s_ref):
    weights = (g_ref, wg_ref, wa_ref, wb_ref, wo_ref)
    x1_ref[...] = _merge_rows(oa_ref[...], ob_ref[...], x_ref[...], *weights)

    @pl.when(pl.program_id(0) == 0)
    def _():
        x1s_ref[...] = _merge_rows(oas_ref[...], obs_ref[...], xs_ref[...], *weights)


def _merge(o_a, o_b, x, oa_s, ob_s, x_s, norm_g, w_gate, w_a, w_b, w_o, tm):
    m, d = x.shape
    assert m % tm == 0
    row_specs = [pl.BlockSpec((tm, NSA_Q), lambda i: (i, 0)), pl.BlockSpec((tm, GLA_HEADS * GLA_DV), lambda i: (i, 0)),
                 pl.BlockSpec((tm, d), lambda i: (i, 0))]
    return pl.pallas_call(
        _merge_kernel, grid=(m // tm,),
        in_specs=row_specs + [_resident(a.shape) for a in (oa_s, ob_s, x_s, norm_g, w_gate, w_a, w_b, w_o)],
        out_specs=[pl.BlockSpec((tm, d), lambda i: (i, 0)), pl.BlockSpec(x_s.shape, lambda i: (0, 0))],
        out_shape=[SDS((m, d), F32), SDS(x_s.shape, F32)],
        compiler_params=_params("arbitrary"),
    )(o_a, o_b, x, oa_s, ob_s, x_s, norm_g, w_gate, w_a, w_b, w_o)


def _mlp_kernel(x1_ref, x1s_ref, g2_ref, wu_ref, wd_ref, gf_ref, y_ref, ys_ref, h_ref):
    i = pl.program_id(0)
    j = pl.program_id(1)
    last = pl.num_programs(1) - 1
    tm = x1_ref.shape[0]

    @pl.when(j == 0)
    def _():
        x1 = x1_ref[...]
        h_ref[0:tm, :] = _rms(x1, g2_ref[...]).astype(BF16)
        y_ref[...] = x1

    @pl.when((j == 0) & (i == 0))
    def _():
        x1s = x1s_ref[...]
        h_ref[tm:, :] = _rms(x1s, g2_ref[...]).astype(BF16)
        ys_ref[...] = x1s

    up = jnp.maximum(jnp.dot(h_ref[...], wu_ref[...].astype(BF16), preferred_element_type=F32), 0.0)
    res = jnp.dot((up * up).astype(BF16), wd_ref[...].astype(BF16), preferred_element_type=F32)
    y_ref[...] += res[0:tm]

    @pl.when(i == 0)
    def _():
        ys_ref[...] += res[tm:]

    @pl.when(j == last)
    def _():
        y_ref[...] = _rms(y_ref[...], gf_ref[...])

    @pl.when((j == last) & (i == 0))
    def _():
        ys_ref[...] = _rms(ys_ref[...], gf_ref[...])


def _mlp(x1, x1_s, g2, w_up, w_down, gf, tm, tf):
    m, d = x1.shape
    ff = w_up.shape[1]
    assert m % tm == 0 and ff % tf == 0
    small = pl.BlockSpec(x1_s.shape, lambda i, j: (0, 0))
    return pl.pallas_call(
        _mlp_kernel, grid=(m // tm, ff // tf),
        in_specs=[pl.BlockSpec((tm, d), lambda i, j: (i, 0)), small, _resident((1, d)),
                  pl.BlockSpec((d, tf), lambda i, j: (0, j)), pl.BlockSpec((tf, d), lambda i, j: (j, 0)),
                  _resident((1, d))],
        out_specs=[pl.BlockSpec((tm, d), lambda i, j: (i, 0)), small],
        out_shape=[SDS((m, d), F32), SDS(x1_s.shape, F32)],
        scratch_shapes=[pltpu.VMEM((tm + x1_s.shape[0], d), BF16)],
        compiler_params=_params("arbitrary", "arbitrary", vmem_limit_bytes=MLP_VMEM_LIMIT_BYTES),
    )(x1, x1_s, g2, w_up, w_down, gf)


def _prep_weights(norm1_g, w_in, b_nsa_gate, cmp_pe_k, cmp_pe_v, cmp_k_w1, cmp_k_w2, cmp_v_w1, cmp_v_w2,
                  gla_w_a2, gla_b_a, gla_norm_g, w_br_a, w_br_b, w_o, norm2_g, w_up, w_down, norm_f):
    pts = [0]
    for s in SPLITS:
        pts.append(pts[-1] + s)
    c_q, c_kv, c_g, c_ql, c_kl, c_vl, c_rl, c_lr, c_br, c_end = pts
    gcols = jnp.asarray([c_g + (g * NSA_HPG + h) * 3 + c for g in range(NSA_KV_HEADS)
                         for c in range(3) for h in range(NSA_HPG)], jnp.int32)
    pad = jnp.zeros((D_MODEL, LANES - N_GATE - GLA_RANK), F32)
    w_misc = jnp.concatenate([w_in[:, gcols], w_in[:, c_lr:c_br], pad], axis=1)
    b_misc = jnp.concatenate([b_nsa_gate[gcols - c_g], jnp.zeros((LANES - N_GATE,), F32)])[None, :]
    w = dict(
        norm1=norm1_g[None, :], norm2=norm2_g[None, :], norm_f=norm_f[None, :],
        w_a=w_in[:, c_q:c_g].astype(BF16),
        w_b=jnp.concatenate([w_in[:, c_ql:c_lr], w_misc], axis=1).astype(BF16),
        w_gate=w_in[:, c_br:c_end].astype(BF16),
        b_misc=b_misc,
        gla_wa=gla_w_a2.astype(BF16), gla_ba=gla_b_a[None, :], gla_ng=gla_norm_g[None, :],
        w_br_a=w_br_a.astype(BF16), w_br_b=w_br_b.astype(BF16), w_o=w_o.astype(BF16),
        w_up=w_up, w_down=w_down,
    )
    cw = {}
    cw['pe_k'], cw['w1_k'], cw['w2_k'] = _cmp_weights(cmp_pe_k, cmp_k_w1, cmp_k_w2)
    cw['pe_v'], cw['w1_v'], cw['w2_v'] = _cmp_weights(cmp_pe_v, cmp_v_w1, cmp_v_w2)
    w['cmp'] = cw
    return w


_KV6 = 6 * NSA_KV
_GQ = GLA_HEADS * GLA_DK
_GV = GLA_HEADS * GLA_DV


def _layer_prompt(x, w):
    bsz, t, d = x.shape
    m = bsz * t
    x2 = x.reshape(m, d)
    zb = jnp.zeros((1, LANES), F32)
    spec_a = [('hmt', 0, NSA_Q, NSA_HEAD_DIM ** -0.5 * LOG2_E)]
    spec_a += [('f32t', NSA_Q + j * NSA_KV, NSA_Q + (j + 1) * NSA_KV, 1.0) for j in range(6)]
    spec_a += [('hm' if j % 2 == 0 else 'hmvt', NSA_Q + j * NSA_KV, NSA_Q + (j + 1) * NSA_KV, 1.0)
               for j in range(2, 6)]
    q_t, r0, r1, r2, r3, r4, r5, ks_hm, vs_t, kw_hm, vw_t = _proj(x2, w['norm1'], w['w_a'], zb, spec_a, PROJ_TM,
                                                                   seq_len=t)
    spec_b = [('f32', 0, _GQ, 1.0), ('f32', _GQ, 2 * _GQ, 1.0), ('bf16', 2 * _GQ, 2 * _GQ + _GV, 1.0),
              ('f32', 2 * _GQ + _GV, 2 * _GQ + 2 * _GV, 1.0),
              ('misc', 2 * _GQ + 2 * _GV, 2 * _GQ + 2 * _GV + LANES, 1.0), ('gates', 0, 0, 1.0)]
    q_l, k_l, v_l, r_l, misc, gates_hm = _proj(x2, w['norm1'], w['w_b'], w['b_misc'], spec_b, PROJ_TM)

    n_pages = t // PAGE_SIZE
    ident = jnp.broadcast_to(jnp.arange(n_pages, dtype=jnp.int32), (bsz, n_pages))
    kcmp, vcmp = _compress(ident, r0, r1, w['cmp'], paged=False)
    o_a = _nsa_prompt(q_t, ks_hm, vs_t, kw_hm, vw_t, kcmp, vcmp, gates_hm, bsz, t)
    o_b, s_new = _gla_prompt(q_l, k_l, v_l, r_l, misc, w['gla_wa'], w['gla_ba'], w['gla_ng'], bsz, t)
    def token_major(a):
        return a.reshape(bsz, NSA_KV_HEADS, NSA_HEAD_DIM, a.shape[-1]).transpose(0, 3, 1, 2)

    rows = [token_major(a) for a in (r0, r1, r2, r3)]
    n_keep = min(WINDOW, t)
    wins = [token_major(a[:, :, t - n_keep:]) for a in (r4, r5)]
    return (o_a, o_b, x2), rows, wins, s_new


def _channel_mix(prompt, sample, w):
    (oa_p, ob_p, x_p), (oa_s, ob_s, x_s) = prompt, sample
    m = x_p.shape[0]
    x1_p, x1_s = _merge(oa_p, ob_p, x_p, oa_s, ob_s, x_s, w['norm1'], w['w_gate'], w['w_br_a'], w['w_br_b'],
                        w['w_o'], MERGE_TM)
    tm = next(c for c in (MLP_TM, PROJ_TM, MERGE_TM) if m % c == 0)
    return _mlp(x1_p, x1_s, w['norm2'], w['w_up'], w['w_down'], w['norm_f'], tm, MLP_TF)


def _group_rows(parts):
    rowg = lax.broadcasted_iota(jnp.int32, parts[0].shape, 0) // NSA_HPG
    out = parts[0]
    for g in range(1, NSA_KV_HEADS):
        out = jnp.where(rowg == g, parts[g], out)
    return out


def _sample_select_kernel(q_ref, kc_ref, vc_ref, oc_ref, idx_ref, *, n_cmp, pos, n_pick):
    q = q_ref[0]
    nt = (((1,), (1,)), ((), ()))
    n_chunk = kc_ref.shape[2]
    s = _group_rows([lax.dot_general(q, kc_ref[0, g], nt, preferred_element_type=F32)
                     for g in range(NSA_KV_HEADS)])
    lane = lax.broadcasted_iota(jnp.int32, s.shape, 1)
    mask = (lane < n_cmp) & (lane * CMP_STRIDE + (CMP_BLOCK - 1) <= pos)
    e, l = _softmax_rows(s, mask)
    p = e * _safe_inv(l)
    pb = p.astype(BF16)
    oc_ref[0] = _group_rows([lax.dot_general(pb, vc_ref[0, g], nt, preferred_element_type=F32)
                             for g in range(NSA_KV_HEADS)])
    nr = p.shape[0]
    y = p + pltpu.roll(p, nr - 1, axis=0)
    psum = y + pltpu.roll(y, nr - 2, axis=0)
    imp = pltpu.roll(psum, 1, axis=1) + psum
    r = SLC_BLOCK // CMP_STRIDE
    for o in range(1, r):
        imp = imp + pltpu.roll(psum, n_chunk - o, axis=1)
    blk = lane // r
    is_blk = lane % r == 0
    valid = blk * SLC_BLOCK <= pos
    forced = (blk == 0) | (blk == pos // SLC_BLOCK)
    score = jnp.where(is_blk & valid, imp + jnp.where(forced, FORCE, 0.0), NEG)
    lane_f = lane.astype(F32)
    out_lane = lax.broadcasted_iota(jnp.int32, (nr, LANES), 1)
    picked = jnp.zeros((nr, LANES), F32)
    for k in range(n_pick):
        mx = jnp.max(score, axis=-1, keepdims=True)
        ix = jnp.min(jnp.where(score == mx, lane_f, float(n_chunk)), axis=-1, keepdims=True)
        picked = jnp.where(out_lane == k, ix, picked)
        score = jnp.where(lane_f == ix, 2.0 * NEG, score)
    idx_ref[0] = picked.astype(jnp.int32) // r


def _sample_select(q_s, kcmp, vcmp, pos, n_pick):
    bsz, g, n_chunk, dh = kcmp.shape
    nh = NSA_HEADS
    cmp_spec = pl.BlockSpec((1, g, n_chunk, dh), lambda b: (b, 0, 0, 0))
    cmpt_spec = pl.BlockSpec((1, g, dh, n_chunk), lambda b: (b, 0, 0, 0))
    return pl.pallas_call(
        functools.partial(_sample_select_kernel, n_cmp=n_chunk - 1, pos=pos, n_pick=n_pick),
        grid=(bsz,),
        in_specs=[pl.BlockSpec((1, nh, dh), lambda b: (b, 0, 0)), cmp_spec, cmpt_spec],
        out_specs=[pl.BlockSpec((1, nh, dh), lambda b: (b, 0, 0)), pl.BlockSpec((1, nh, LANES), lambda b: (b, 0, 0))],
        out_shape=[SDS((bsz, nh, dh), F32), SDS((bsz, nh, LANES), jnp.int32)],
        compiler_params=_params("parallel"),
    )(q_s, kcmp, vcmp)


def _sample_attend_kernel(idx_ref, pt_ref, q_ref, oc_ref, ks_hbm, vs_hbm, kw_ref, vw_ref,
                          nks_ref, nvs_ref, nkw_ref, nvw_ref, gate_ref, o_ref, ksel, vsel, sem, *, n_pick):
    b = pl.program_id(0)
    g_n, dh = NSA_KV_HEADS, NSA_HEAD_DIM
    half = PAGE_SIZE // SLC_BLOCK

    def copies():
        out = []
        for g in range(g_n):
            for r in range(n_pick):
                page = pt_ref[b, idx_ref[b * g_n + g, r] // half]
                out.append(pltpu.make_async_copy(ks_hbm.at[page, g], ksel.at[g, r], sem.at[0]))
                out.append(pltpu.make_async_copy(vs_hbm.at[page, g], vsel.at[g, r], sem.at[1]))
        return out

    for c in copies():
        c.start()

    q = q_ref[0]
    qf = q.astype(F32)
    nt = (((1,), (1,)), ((), ()))

    def attend(keys_t, vals_t, bias, k_new, v_new):
        s = _group_rows([jnp.dot(q, keys_t[g].astype(BF16), preferred_element_type=F32)
                         + (0.0 if bias is None else bias[g]) for g in range(g_n)])
        s_new = _group_rows([jnp.sum(qf * k_new[:, g * dh:(g + 1) * dh], axis=-1, keepdims=True)
                             for g in range(g_n)])
        m = jnp.maximum(jnp.max(s, axis=-1, keepdims=True), s_new)
        e = jnp.exp(s - m)
        e_new = jnp.exp(s_new - m)
        l = jnp.sum(e, axis=-1, keepdims=True) + e_new
        eb = e.astype(BF16)
        acc = _group_rows([lax.dot_general(eb, vals_t[g].astype(BF16), nt, preferred_element_type=F32)
                           + e_new * v_new[:, g * dh:(g + 1) * dh] for g in range(g_n)])
        return acc / l

    o_w = attend([kw_ref[0, g] for g in range(g_n)], [vw_ref[0, g] for g in range(g_n)], None,
                 nkw_ref[0], nvw_ref[0])
    for c in copies():
        c.wait()
    lin = lax.broadcasted_iota(jnp.int32, (1, PAGE_SIZE), 1)
    bias = []
    for g in range(g_n):
        parts = []
        for r in range(n_pick):
            off = (idx_ref[b * g_n + g, r] % half) * SLC_BLOCK
            parts.append(jnp.where((lin >= off) & (lin < off + SLC_BLOCK), 0.0, NEG))
        bias.append(jnp.concatenate(parts, axis=1))

    def tiles(buf, g):
        return jnp.concatenate([buf[g, r] for r in range(n_pick)], axis=1)

    o_s = attend([tiles(ksel, g) for g in range(g_n)], [tiles(vsel, g) for g in range(g_n)], bias,
                 nks_ref[0], nvs_ref[0])

    gt = jnp.broadcast_to(gate_ref[0], (LANES, LANES)).T
    nh = NSA_HEADS
    o_ref[0] = (gt[0:nh, 0:dh] * oc_ref[0] + gt[nh:2 * nh, 0:dh] * o_s + gt[2 * nh:3 * nh, 0:dh] * o_w)


def _sample_attend(idx, page_table, q_s, o_c, slc_k, slc_v, win_k, win_v, new_rows, gates, n_pick):
    bsz = page_table.shape[0]
    nh, dh, g = NSA_HEADS, NSA_HEAD_DIM, NSA_KV_HEADS
    wlen = win_k.shape[-1]
    row_spec = pl.BlockSpec((1, 1, NSA_KV), lambda b, *_: (b, 0, 0))
    win_spec = pl.BlockSpec((1, g, dh, wlen), lambda b, *_: (b, 0, 0, 0))
    head_spec = pl.BlockSpec((1, nh, dh), lambda b, *_: (b, 0, 0))
    any_spec = pl.BlockSpec(memory_space=pl.ANY)
    grid_spec = pltpu.PrefetchScalarGridSpec(
        num_scalar_prefetch=2, grid=(bsz,),
        in_specs=[head_spec, head_spec, any_spec, any_spec, win_spec, win_spec,
                  row_spec, row_spec, row_spec, row_spec, pl.BlockSpec((1, 1, LANES), lambda b, *_: (b, 0, 0))],
        out_specs=head_spec,
        scratch_shapes=[pltpu.VMEM((g, n_pick, dh, PAGE_SIZE), F32), pltpu.VMEM((g, n_pick, dh, PAGE_SIZE), F32),
                        pltpu.SemaphoreType.DMA((2,))])
    return pl.pallas_call(
        functools.partial(_sample_attend_kernel, n_pick=n_pick),
        grid_spec=grid_spec, out_shape=SDS((bsz, nh, dh), F32),
        compiler_params=_params("arbitrary"),
    )(idx, page_table, q_s, o_c, slc_k, slc_v, win_k, win_v, *new_rows, gates)


def _gla_sample_kernel(q_ref, k_ref, v_ref, r_ref, misc_ref, wa_ref, ba_ref, ng_ref, s0_ref, o_ref, s_ref):
    dk, dv = GLA_DK, GLA_DV
    lr = jnp.broadcast_to(misc_ref[0][:, N_GATE:N_GATE + GLA_RANK], (16, GLA_RANK)).astype(BF16)
    x = jnp.dot(lr, wa_ref[...], preferred_element_type=F32)[0:1] + ba_ref[...]
    g_all = jax.nn.log_sigmoid(x) / GLA_TAU

    def col(v):
        t = jnp.broadcast_to(v, (dk, dk)).T
        return jnp.concatenate([t] * (dv // dk), axis=1)

    outs = []
    for h in range(GLA_HEADS):
        g = g_all[:, h * dk:(h + 1) * dk]
        q = q_ref[0][:, h * dk:(h + 1) * dk] * (GLA_DK ** -0.5)
        k = k_ref[0][:, h * dk:(h + 1) * dk]
        v = v_ref[0][:, h * dv:(h + 1) * dv]
        s0 = s0_ref[0, h]
        q_t = q * jnp.exp(g)
        k_t = k * jnp.exp(-g)
        a = jnp.sum(q_t * k_t, axis=-1, keepdims=True)
        o = jnp.sum(col(q_t) * s0, axis=0, keepdims=True) + a * v
        s_ref[0, h] = col(jnp.exp(g)) * s0 + col(k) * v
        outs.append(_rms(o, ng_ref[...]) * jax.nn.silu(r_ref[0][:, h * dv:(h + 1) * dv]))
    o_ref[0] = jnp.concatenate(outs, axis=-1)


def _gla_sample(q_l, k_l, v_l, r_l, misc, wa, ba, ng, s0):
    bsz, h, dk, dv = s0.shape

    def row(n):
        return pl.BlockSpec((1, 1, n), lambda b: (b, 0, 0))

    st_spec = pl.BlockSpec((1, h, dk, dv), lambda b: (b, 0, 0, 0))
    return pl.pallas_call(
        _gla_sample_kernel, grid=(bsz,),
        in_specs=[row(h * dk), row(h * dk), row(h * dv), row(h * dv), row(LANES),
                  _resident(wa.shape), _resident(ba.shape), _resident(ng.shape), st_spec],
        out_specs=[row(h * dv), st_spec],
        out_shape=[SDS((bsz, 1, h * dv), F32), SDS((bsz, h, dk, dv), F32)],
        compiler_params=_params("parallel"),
    )(q_l, k_l, v_l, r_l, misc, wa, ba, ng, s0)


SAMPLE_ROWS = 16


def _layer_sample(x, caches, wins, s0, page_table, w):
    bsz, t, d = x.shape
    n_pages = page_table.shape[1]
    pos = n_pages * PAGE_SIZE
    assert t == 1 and bsz <= SAMPLE_ROWS and pos % SLC_BLOCK == 0 and wins[0].shape[1] == WINDOW
    assert pos // SLC_BLOCK >= N_SELECT
    mp = SAMPLE_ROWS
    x2 = jnp.pad(x.reshape(bsz, d), ((0, mp - bsz), (0, 0)))
    zb = jnp.zeros((1, LANES), F32)
    spec_a = [('bf16', 0, NSA_Q, NSA_HEAD_DIM ** -0.5)]
    spec_a += [('f32', NSA_Q + j * NSA_KV, NSA_Q + (j + 1) * NSA_KV, 1.0) for j in range(6)]
    q_s, r0, r1, r2, r3, r4, r5 = _proj(x2, w['norm1'], w['w_a'], zb, spec_a, mp)
    spec_b = [('f32', 0, _GQ, 1.0), ('f32', _GQ, 2 * _GQ, 1.0), ('f32', 2 * _GQ, 2 * _GQ + _GV, 1.0),
              ('f32', 2 * _GQ + _GV, 2 * _GQ + 2 * _GV, 1.0),
              ('misc', 2 * _GQ + 2 * _GV, 2 * _GQ + 2 * _GV + LANES, 1.0)]
    q_l, k_l, v_l, r_l, misc = _proj(x2, w['norm1'], w['w_b'], w['b_misc'], spec_b, mp)

    cache_t = [a.transpose(0, 2, 3, 1) for a in caches]
    win_t = [a.transpose(0, 2, 3, 1) for a in wins]
    n_pool = caches[0].shape[0]
    cmp_pages = [a.reshape(n_pool, NSA_KV, PAGE_SIZE) for a in cache_t[:2]]
    kcmp, vcmp = _compress(page_table, cmp_pages[0], cmp_pages[1], w['cmp'], paged=True)
    q_h = q_s.reshape(mp, NSA_HEADS, NSA_HEAD_DIM)
    n_pick = N_SELECT - 1
    o_c, idx = _sample_select(q_h, kcmp, vcmp, pos, n_pick)
    idx = idx[:, ::NSA_HPG, :n_pick].reshape(bsz * NSA_KV_HEADS, n_pick)
    gates = misc[:, :N_GATE].reshape(mp, NSA_KV_HEADS, 3, NSA_HPG).transpose(0, 2, 1, 3).reshape(mp, 1, N_GATE)
    gates = jnp.pad(gates, ((0, 0), (0, 0), (0, LANES - N_GATE)))
    new_rows = [a.reshape(mp, 1, NSA_KV) for a in (r2, r3, r4, r5)]
    o_a = _sample_attend(idx, page_table, q_h, o_c, cache_t[2], cache_t[3], win_t[0], win_t[1], new_rows, gates,
                         n_pick)
    o_a = jnp.pad(o_a.reshape(bsz, NSA_Q), ((0, mp - bsz), (0, 0))).astype(BF16)

    def r3d(a):
        return a.reshape(mp, 1, a.shape[-1])

    o_b, s_new = _gla_sample(r3d(q_l), r3d(k_l), r3d(v_l), r3d(r_l), r3d(misc), w['gla_wa'], w['gla_ba'],
                             w['gla_ng'], s0)
    o_b = jnp.pad(o_b.reshape(bsz, _GV), ((0, mp - bsz), (0, 0))).astype(BF16)
    kvh = (bsz, 1, NSA_KV_HEADS, NSA_HEAD_DIM)
    rows = [a[:bsz].reshape(kvh) for a in (r0, r1, r2, r3)]
    new_wins = [jnp.concatenate([c[:, 1:], a[:bsz].reshape(kvh)], axis=1) for c, a in zip(wins, (r4, r5))]
    return (o_a, o_b, x2), rows, new_wins, s_new


def kernel(x_prompt, x_sample, cache_cmp_k, cache_cmp_v, cache_slc_k, cache_slc_v, cache_win_k, cache_win_v, state_gla, page_table, norm1_g, w_in, b_nsa_gate, cmp_pe_k, cmp_pe_v, cmp_k_w1, cmp_k_w2, cmp_v_w1, cmp_v_w2, gla_w_a2, gla_b_a, gla_norm_g, w_br_a, w_br_b, w_o, norm2_g, w_up, w_down, norm_f):
    assert DEPTH == 1 and norm1_g.shape[0] == 1
    w = _prep_weights(norm1_g[0], w_in[0], b_nsa_gate[0], cmp_pe_k[0], cmp_pe_v[0], cmp_k_w1[0], cmp_k_w2[0],
                      cmp_v_w1[0], cmp_v_w2[0], gla_w_a2[0], gla_b_a[0], gla_norm_g[0], w_br_a[0], w_br_b[0],
                      w_o[0], norm2_g[0], w_up[0], w_down[0], norm_f)
    mix_p, rows_p, wins_p, s_p = _layer_prompt(x_prompt, w)
    caches = [c[0] for c in (cache_cmp_k, cache_cmp_v, cache_slc_k, cache_slc_v)]
    mix_s, rows_s, wins_s, s_s = _layer_sample(x_sample, caches, [cache_win_k[0], cache_win_v[0]], state_gla[0],
                                               page_table, w)
    y_p, y_s = _channel_mix(mix_p, mix_s, w)
    y_p = y_p.reshape(x_prompt.shape)
    y_s = y_s[:x_sample.shape[0]].reshape(x_sample.shape)
    outs_p = [a[None] for a in rows_p + wins_p + [s_p]]
    outs_s = [a[None] for a in rows_s + wins_s + [s_s]]
    return (y_p, y_s, *outs_p, *outs_s)
```

```python
import functools

import jax
import jax.numpy as jnp
from jax import lax
from jax.experimental import pallas as pl
from jax.experimental.pallas import tpu as pltpu

D_MODEL = 2048
DEPTH = 1
PAGE_SIZE = 128
NSA_HEADS = 16
NSA_KV_HEADS = 4
NSA_HPG = NSA_HEADS // NSA_KV_HEADS
NSA_HEAD_DIM = 64
NSA_Q = NSA_HEADS * NSA_HEAD_DIM
NSA_KV = NSA_KV_HEADS * NSA_HEAD_DIM
CMP_STRIDE = 16
CMP_BLOCK = 32
CMP_HIDDEN = 128
SLC_BLOCK = 64
N_SELECT = 16
WINDOW = 512
GLA_HEADS = 4
GLA_DK = (D_MODEL // 4) // GLA_HEADS
GLA_DV = (D_MODEL // 2) // GLA_HEADS
GLA_RANK = 16
GLA_TAU = 16.0
D_FF = 4 * D_MODEL
EPS = 1e-6
NEG = -1e30
FORCE = 1e4
SPLITS = (NSA_Q, 6 * NSA_KV, 3 * NSA_HEADS,
          GLA_HEADS * GLA_DK, GLA_HEADS * GLA_DK, GLA_HEADS * GLA_DV, GLA_HEADS * GLA_DV,
          GLA_RANK, 2 * D_MODEL)

F32 = jnp.float32
BF16 = jnp.bfloat16
LANES = 128
VMEM_LIMIT_BYTES = 56 * 1024 * 1024
MLP_VMEM_LIMIT_BYTES = 60 * 1024 * 1024
N_GATE = 3 * NSA_HEADS
GLA_CHUNK = 32
GLA_UNROLL = 8
GLA_GROUP = 4
NSA_TQ = 256
NSA_TK = 256
VT_ROWS = NSA_HEAD_DIM + 16
CMP_PAGES = 16
PROJ_TM = 512
MERGE_TM = 256
MLP_TM = 1024
MLP_TF = 512
SDS = jax.ShapeDtypeStruct


def _params(*sem, vmem_limit_bytes=VMEM_LIMIT_BYTES):
    return pltpu.CompilerParams(dimension_semantics=sem, vmem_limit_bytes=vmem_limit_bytes)


def _resident(shape):
    nd = len(shape)
    return pl.BlockSpec(shape, lambda *_: (0,) * nd, pipeline_mode=pl.Buffered(1))


def _row_window(row0, n_rows, width):
    return pl.BlockSpec((pl.Element(n_rows), pl.Element(width)), lambda *_: (row0, 0),
                        pipeline_mode=pl.Buffered(1))


def _rms(x, g):
    return x * lax.rsqrt(jnp.mean(x * x, axis=-1, keepdims=True) + EPS) * g


_NT = (((1,), (1,)), ((), ()))


def _proj_kernel(x_ref, g_ref, w_ref, wm_ref, b_ref, *out_refs, spec):
    h = _rms(x_ref[...], g_ref[...]).astype(BF16)
    misc = None
    for o_ref, (kind, c0, c1, scale) in zip(out_refs, spec):
        if kind == 'gates':
            for g in range(NSA_KV_HEADS):
                o_ref[g] = misc if g == 0 else pltpu.roll(misc, LANES - g * 3 * NSA_HPG, axis=1)
            continue
        rows = wm_ref[...] if kind == 'misc' else w_ref[c0:c1, :]
        r = lax.dot_general(h, rows, _NT, preferred_element_type=F32)
        if scale != 1.0:
            r = r * scale
        if kind == 'f32':
            o_ref[...] = r
        elif kind == 'f32t':
            o_ref[0] = r.T
        elif kind == 'bf16':
            o_ref[...] = r.astype(BF16)
        elif kind == 'sigmoid':
            o_ref[...] = jax.nn.sigmoid(r)
        elif kind == 'hm':
            for i in range((c1 - c0) // NSA_HEAD_DIM):
                o_ref[i] = r[:, i * NSA_HEAD_DIM:(i + 1) * NSA_HEAD_DIM].astype(BF16)
        elif kind == 'hmt':
            rt = r.T
            for i in range((c1 - c0) // NSA_HEAD_DIM):
                o_ref[i] = rt[i * NSA_HEAD_DIM:(i + 1) * NSA_HEAD_DIM].astype(BF16)
        elif kind == 'hmvt':
            rt = r.T
            sub = lax.broadcasted_iota(jnp.int32, (VT_ROWS - NSA_HEAD_DIM, r.shape[0]), 0)
            ones = jnp.where(sub == 0, 1.0, 0.0)
            for i in range((c1 - c0) // NSA_HEAD_DIM):
                piece = rt[i * NSA_HEAD_DIM:(i + 1) * NSA_HEAD_DIM]
                o_ref[i] = jnp.concatenate([piece, ones], axis=0).astype(BF16)
        elif kind == 'misc':
            lane = lax.broadcasted_iota(jnp.int32, r.shape, 1)
            misc = jnp.where(lane < N_GATE, jax.nn.sigmoid(r + b_ref[...]), r)
            o_ref[...] = misc


def _proj(x, norm_g, w_t, row0, n_rows, w_misc_t, bias, spec, tm, seq_len=None):
    m, d = x.shape
    assert m % tm == 0 and w_t.shape[1] == d
    out_shape, out_specs = [], []
    for kind, c0, c1, _ in spec:
        if kind == 'f32t':
            assert seq_len % tm == 0
            per_seq = seq_len // tm
            out_shape.append(SDS((m // seq_len, c1 - c0, seq_len), F32))
            out_specs.append(pl.BlockSpec((1, c1 - c0, tm), lambda i: (i // per_seq, 0, i % per_seq)))
        elif kind == 'hm':
            nh = (c1 - c0) // NSA_HEAD_DIM
            out_shape.append(SDS((nh, m, NSA_HEAD_DIM), BF16))
            out_specs.append(pl.BlockSpec((nh, tm, NSA_HEAD_DIM), lambda i: (0, i, 0)))
        elif kind in ('hmt', 'hmvt'):
            nh = (c1 - c0) // NSA_HEAD_DIM
            nrow = NSA_HEAD_DIM if kind == 'hmt' else VT_ROWS
            out_shape.append(SDS((nh, nrow, m), BF16))
            out_specs.append(pl.BlockSpec((nh, nrow, tm), lambda i: (0, 0, i)))
        elif kind == 'gates':
            out_shape.append(SDS((NSA_KV_HEADS, m, LANES), F32))
            out_specs.append(pl.BlockSpec((NSA_KV_HEADS, tm, LANES), lambda i: (0, i, 0)))
        else:
            out_shape.append(SDS((m, c1 - c0), BF16 if kind == 'bf16' else F32))
            out_specs.append(pl.BlockSpec((tm, c1 - c0), lambda i: (i, 0)))
    return pl.pallas_call(
        functools.partial(_proj_kernel, spec=tuple(spec)),
        grid=(m // tm,),
        in_specs=[pl.BlockSpec((tm, d), lambda i: (i, 0)), _resident((1, d)), _row_window(row0, n_rows, d),
                  _resident(w_misc_t.shape), _resident((1, LANES))],
        out_specs=out_specs, out_shape=out_shape,
        compiler_params=_params("parallel"),
    )(x, norm_g, w_t, w_misc_t, bias)


def _cmp_kernel(pt_ref, k_hbm, v_hbm, perm_ref, pek_ref, pev_ref, w1k_ref, w1v_ref, w2k_ref, w2v_ref,
                ok_ref, ov_ref, kbuf, vbuf, hk, hv, sem, *, n_pages_step, paged):
    b = pl.program_id(0)
    s = pl.program_id(1)
    ns = pl.num_programs(1)
    t = b * ns + s
    total = pl.num_programs(0) * ns
    rows = n_pages_step * 8
    dh, hid_n = NSA_HEAD_DIM, CMP_HIDDEN

    def copies(tt, slot):
        bb = tt // ns
        ss = tt % ns
        out = []
        for p in range(n_pages_step):
            page = pt_ref[bb, ss * n_pages_step + p]
            if paged:
                src = [hbm.at[page] for hbm in (k_hbm, v_hbm)]
            else:
                tok = pl.ds(pl.multiple_of(page * PAGE_SIZE, PAGE_SIZE), PAGE_SIZE)
                src = [hbm.at[bb, :, tok] for hbm in (k_hbm, v_hbm)]
            out.append(pltpu.make_async_copy(src[0], kbuf.at[slot, p], sem.at[slot, 0]))
            out.append(pltpu.make_async_copy(src[1], vbuf.at[slot, p], sem.at[slot, 1]))
        return out

    slot = t % 2

    @pl.when(t == 0)
    def _():
        for c in copies(t, slot):
            c.start()

    @pl.when(t + 1 < total)
    def _():
        for c in copies(t + 1, 1 - slot):
            c.start()

    for c in copies(t, slot):
        c.wait()

    r0 = pl.multiple_of(s * rows, rows)
    low = lax.broadcasted_iota(jnp.int32, (8, LANES), 1) < dh
    tok_pairs = CMP_STRIDE // 2
    nt = (((1,), (1,)), ((), ()))
    perm = perm_ref[...]
    for buf, w1_ref, h_ref in ((kbuf, w1k_ref, hk), (vbuf, w1v_ref, hv)):
        parts = [[[] for _ in range(tok_pairs)] for _ in range(NSA_KV_HEADS)]
        for p in range(n_pages_step):
            r = lax.dot_general(perm, buf[slot, p].astype(BF16), nt, preferred_element_type=F32)
            for q in range(2):
                for m in range(tok_pairs):
                    x0 = r[2 * m * 8:2 * m * 8 + 8, q * LANES:(q + 1) * LANES]
                    x1 = r[(2 * m + 1) * 8:(2 * m + 1) * 8 + 8, q * LANES:(q + 1) * LANES]
                    parts[2 * q][m].append(jnp.where(low, x0, pltpu.roll(x1, dh, axis=1)))
                    parts[2 * q + 1][m].append(jnp.where(low, pltpu.roll(x0, dh, axis=1), x1))
        for g in range(NSA_KV_HEADS):
            z = jnp.concatenate([jnp.concatenate(parts[g][m], axis=0) for m in range(tok_pairs)],
                                axis=1).astype(BF16)
            h_ref[pl.ds(r0, rows), g * 2 * hid_n:(g + 1) * 2 * hid_n] = jnp.dot(
                z, w1_ref[...], preferred_element_type=F32)

    @pl.when(s == ns - 1)
    def _():
        for h_ref, pe_ref, w1_ref, w2_ref, o_ref in ((hk, pek_ref, w1k_ref, w2k_ref, ok_ref),
                                                     (hv, pev_ref, w1v_ref, w2v_ref, ov_ref)):
            n_chunk = h_ref.shape[0]
            pe = jnp.broadcast_to(pe_ref[...], (2, 16, pe_ref.shape[2])).astype(BF16)
            pe_term = (jnp.dot(pe[0], w1_ref[:, 0:hid_n], preferred_element_type=F32)
                       + jnp.dot(pe[1], w1_ref[:, hid_n:2 * hid_n], preferred_element_type=F32))[0:1]
            hid = []
            for g in range(NSA_KV_HEADS):
                first = h_ref[:, g * 2 * hid_n:g * 2 * hid_n + hid_n]
                last = h_ref[:, g * 2 * hid_n + hid_n:(g + 1) * 2 * hid_n]
                hid.append(first + pltpu.roll(last, n_chunk - 1, axis=0) + pe_term)
            hid = jnp.concatenate(hid, axis=1)
            res = jnp.dot(jax.nn.gelu(hid).astype(BF16), w2_ref[...], preferred_element_type=F32)
            if o_ref is ok_ref:
                for g in range(NSA_KV_HEADS):
                    o_ref[0, g] = res[:, g * NSA_HEAD_DIM:(g + 1) * NSA_HEAD_DIM].astype(BF16)
            else:
                res_t = res.T
                for g in range(NSA_KV_HEADS):
                    o_ref[0, g] = res_t[g * NSA_HEAD_DIM:(g + 1) * NSA_HEAD_DIM].astype(BF16)


def _cmp_weights(pe, w1, w2):
    g = NSA_KV_HEADS
    half = CMP_STRIDE * NSA_HEAD_DIM
    pe_t = pe.reshape(2, 1, half)
    w1ab = jnp.concatenate([w1[:half], w1[half:]], axis=1)
    w2b = jnp.einsum('jd,gh->gjhd', w2, jnp.eye(g, dtype=F32)).reshape(g * CMP_HIDDEN, g * NSA_HEAD_DIM)
    return pe_t, w1ab.astype(BF16), w2b.astype(BF16)


def _compress(page_table, k_src, v_src, cw, paged):
    bsz, n_pages = page_table.shape
    p_step = min(CMP_PAGES, n_pages)
    ns = n_pages // p_step
    n_chunk = n_pages * 8
    page_shape = (NSA_KV, PAGE_SIZE)
    assert NSA_KV == 2 * LANES
    assert k_src.shape[1:] == (page_shape if paged else (NSA_KV, n_pages * PAGE_SIZE))
    half = CMP_STRIDE * NSA_HEAD_DIM
    gh = NSA_KV_HEADS * CMP_HIDDEN
    out_row = jnp.arange(PAGE_SIZE, dtype=jnp.int32)
    src_tok = (out_row % 8) * CMP_STRIDE + out_row // 8
    perm = (src_tok[:, None] == jnp.arange(PAGE_SIZE, dtype=jnp.int32)[None, :]).astype(BF16)
    k_sds =SDS((bsz, NSA_KV_HEADS, n_chunk, NSA_HEAD_DIM), BF16)
    v_sds = SDS((bsz, NSA_KV_HEADS, NSA_HEAD_DIM, n_chunk), BF16)
    k_spec = pl.BlockSpec((1, NSA_KV_HEADS, n_chunk, NSA_HEAD_DIM), lambda b, s, pt: (b, 0, 0, 0))
    v_spec = pl.BlockSpec((1, NSA_KV_HEADS, NSA_HEAD_DIM, n_chunk), lambda b, s, pt: (b, 0, 0, 0))
    grid_spec = pltpu.PrefetchScalarGridSpec(
        num_scalar_prefetch=1, grid=(bsz, ns),
        in_specs=[pl.BlockSpec(memory_space=pl.ANY), pl.BlockSpec(memory_space=pl.ANY),
                  _resident((PAGE_SIZE, PAGE_SIZE)), _resident((2, 1, half)), _resident((2, 1, half)),
                  _resident((half, 2 * CMP_HIDDEN)), _resident((half, 2 * CMP_HIDDEN)),
                  _resident((gh, NSA_KV)), _resident((gh, NSA_KV))],
        out_specs=[k_spec, v_spec],
        scratch_shapes=[pltpu.VMEM((2, p_step) + page_shape, F32), pltpu.VMEM((2, p_step) + page_shape, F32),
                        pltpu.VMEM((n_chunk, 2 * gh), F32), pltpu.VMEM((n_chunk, 2 * gh), F32),
                        pltpu.SemaphoreType.DMA((2, 2))])
    return pl.pallas_call(
        functools.partial(_cmp_kernel, n_pages_step=p_step, paged=paged),
        grid_spec=grid_spec, out_shape=[k_sds, v_sds],
        compiler_params=_params("arbitrary", "arbitrary"),
    )(page_table, k_src, v_src, perm, cw['pe_k'], cw['pe_v'], cw['w1_k'], cw['w1_v'], cw['w2_k'], cw['w2_v'])


def _softmax_rows(s, mask):
    sm = jnp.where(mask, s, NEG)
    m = jnp.max(sm, axis=-1, keepdims=True)
    e = jnp.where(mask, jnp.exp(sm - m), 0.0)
    return e, jnp.sum(e, axis=-1, keepdims=True)


def _safe_inv(l):
    return jnp.where(l > 0.0, 1.0 / jnp.where(l > 0.0, l, 1.0), 0.0)


M_INIT = -1e20
LOG2_E = 1.4426950408889634


def _nsa_prompt_kernel(qt_ref, kc_ref, vct_ref, ks_ref, vst_ref, kw_ref, vwt_ref, gate_ref,
                       o_ref, *scratch, n_cmp, n_slc, n_sel):
    tq, tk, hpg, dh = NSA_TQ, NSA_TK, NSA_HPG, NSA_HEAD_DIM
    rk_refs = scratch[:tq // LANES]
    selb_ref, m_ref, acc_ref = scratch[tq // LANES:tq // LANES + 3]
    s_refs, p_refs, a_refs = (scratch[tq // LANES + 3 + 2 * i:tq // LANES + 5 + 2 * i] for i in range(3))
    qi = pl.program_id(2)
    q0 = qi * tq
    n_chunk = kc_ref.shape[2]
    r = SLC_BLOCK // CMP_STRIDE
    n_row = n_chunk // r

    ci = lax.broadcasted_iota(jnp.int32, (n_chunk, tq), 0)
    pos_c = q0 + lax.broadcasted_iota(jnp.int32, (n_chunk, tq), 1)
    m_c = (ci < n_cmp) & (ci * CMP_STRIDE + (CMP_BLOCK - 1) <= pos_c)
    kc = kc_ref[0, 0]
    vct = vct_ref[0, 0]
    o_c = []
    psum = None
    for h in range(hpg):
        s = jnp.where(m_c, jnp.dot(kc, qt_ref[0, h], preferred_element_type=F32), NEG)
        e = jnp.where(m_c, jnp.exp2(s - jnp.max(s, axis=0, keepdims=True)), 0.0)
        p = e * _safe_inv(jnp.sum(e, axis=0, keepdims=True))
        o_c.append(jnp.dot(vct, p.astype(BF16), preferred_element_type=F32))
        psum = p if psum is None else psum + p

    imp = pltpu.roll(psum, 1, axis=0) + psum
    for o in range(1, r):
        imp = imp + pltpu.roll(psum, n_chunk - o, axis=0)
    parts = []
    for i, rk_ref in enumerate(rk_refs):
        rk_ref[...] = imp[:, i * LANES:(i + 1) * LANES]
        parts.append(rk_ref[pl.ds(0, n_row, stride=r), :])
    imp_b = jnp.concatenate(parts, axis=1)
    j_io = lax.broadcasted_iota(jnp.int32, (n_row, tq), 0)
    pos_b = q0 + lax.broadcasted_iota(jnp.int32, (n_row, tq), 1)
    valid = (j_io < n_slc) & (j_io * SLC_BLOCK <= pos_b)
    forced = (j_io == 0) | (j_io == pos_b // SLC_BLOCK)
    sc = jnp.where(valid, imp_b + jnp.where(forced, FORCE, 0.0), NEG)
    rank = jnp.zeros((n_row, tq), F32)
    for k in range(n_slc):
        ck = sc[k:k + 1, :]
        beats = (ck > sc) | ((ck == sc) & (j_io > k))
        rank = rank + jnp.where(beats, 1.0, 0.0)
    selb_ref[...] = jnp.where(rank < n_sel, 0.0, NEG)

    dlt = lax.broadcasted_iota(jnp.int32, (tk, tq), 0) - lax.broadcasted_iota(jnp.int32, (tk, tq), 1)
    blocks_per_tile = tk // SLC_BLOCK

    def flash(k_ref, vt_ref, lo, hi, bias_fn):
        n_tiles = k_ref.shape[2] // tk
        m_ref[...] = jnp.full(m_ref.shape, M_INIT, F32)
        acc_ref[...] = jnp.zeros(acc_ref.shape, F32)
        p_refs[1][...] = jnp.zeros(p_refs[1].shape, BF16)
        a_refs[1][...] = jnp.ones(a_refs[1].shape, F32)

        def tile_start(kb):
            return pl.multiple_of(jnp.clip(kb, 0, n_tiles - 1) * tk, tk)

        def scores(kb, s_ref):
            kk = k_ref[0, 0, pl.ds(tile_start(kb), tk), :]
            bias = bias_fn(kb)
            for h in range(hpg):
                for c in range(tk // SLC_BLOCK):
                    rs = slice(c * SLC_BLOCK, (c + 1) * SLC_BLOCK)
                    s_ref[h, rs, :] = jnp.dot(kk[rs], qt_ref[0, h], preferred_element_type=F32) + bias[rs]

        def softmax(s_ref, p_ref, a_ref):
            for h in range(hpg):
                for c in range(tq // LANES):
                    cs = slice(c * LANES, (c + 1) * LANES)
                    s = s_ref[h, :, cs]
                    m_prev = m_ref[h, :, cs]
                    m_new = jnp.maximum(m_prev, jnp.max(s, axis=0, keepdims=True))
                    p_ref[h, :, cs] = jnp.exp2(s - m_new).astype(BF16)
                    a_ref[h, :, cs] = jnp.exp2(m_prev - m_new)
                    m_ref[h, :, cs] = m_new

        def weighted_values(kb, p_ref, a_ref):
            vt = vt_ref[0, :, pl.ds(tile_start(kb), tk)]
            for h in range(hpg):
                acc_ref[h] = a_ref[h] * acc_ref[h] + jnp.dot(vt, p_ref[h], preferred_element_type=F32)

        scores(lo, s_refs[0])

        def body(j, carry):
            i = lo + 2 * j
            scores(i + 1, s_refs[1])
            softmax(s_refs[0], p_refs[0], a_refs[0])
            weighted_values(i - 1, p_refs[1], a_refs[1])
            scores(i + 2, s_refs[0])
            softmax(s_refs[1], p_refs[1], a_refs[1])
            weighted_values(i, p_refs[0], a_refs[0])
            return carry

        n_pairs = (hi - lo + 1) // 2
        lax.fori_loop(0, n_pairs, body, 0)
        weighted_values(lo + 2 * n_pairs - 1, p_refs[1], a_refs[1])
        return [acc_ref[h, 0:dh, :] * _safe_inv(acc_ref[h, dh:dh + 1, :]) for h in range(hpg)]

    def slc_bias(kb):
        first = jnp.minimum(kb, n_slc // blocks_per_tile - 1) * blocks_per_tile
        rows = [jnp.broadcast_to(selb_ref[pl.ds(first + i, 1), :], (SLC_BLOCK, tq))
                for i in range(blocks_per_tile)]
        return jnp.where(dlt + (kb * tk - q0) <= 0, jnp.concatenate(rows, axis=0), NEG)

    def win_bias(kb):
        d = dlt + (kb * tk - q0)
        return jnp.where((d <= 0) & (d >= -WINDOW), 0.0, NEG)

    hi = (q0 + tq - 1) // tk + 1
    o_w = flash(kw_ref, vwt_ref, jnp.maximum(q0 - WINDOW, 0) // tk, hi, win_bias)
    o_s = flash(ks_ref, vst_ref, 0, hi, slc_bias)

    gt = gate_ref[0].T
    outs = [gt[h:h + 1] * o_c[h] + gt[hpg + h:hpg + h + 1] * o_s[h] + gt[2 * hpg + h:2 * hpg + h + 1] * o_w[h]
            for h in range(hpg)]
    o_ref[...] = jnp.concatenate(outs, axis=0).T.astype(BF16)


def _nsa_prompt(q_t, ks_hm, vs_t, kw_hm, vw_t, kcmp, vcmp_t, gates_hm, bsz, t):
    g, hpg, dh, tq = NSA_KV_HEADS, NSA_HPG, NSA_HEAD_DIM, NSA_TQ
    r = SLC_BLOCK // CMP_STRIDE
    n_slc = t // SLC_BLOCK
    assert t % NSA_TK == 0 and t % tq == 0 and n_slc * r <= LANES and NSA_TK % SLC_BLOCK == 0
    nq = t // tq
    n_cmp = kcmp.shape[2] - 1
    if kcmp.shape[2] < LANES:
        fill = LANES - kcmp.shape[2]
        kcmp = jnp.pad(kcmp, ((0, 0), (0, 0), (0, fill), (0, 0)))
        vcmp_t = jnp.pad(vcmp_t, ((0, 0), (0, 0), (0, 0), (0, fill)))
    n_chunk = kcmp.shape[2]
    assert n_chunk == LANES
    q4 = q_t.reshape(g, hpg, dh, bsz * t)

    def k_spec():
        return pl.BlockSpec((1, 1, t, dh), lambda b, gg, qi: (gg, b, 0, 0))

    def vt_spec():
        return pl.BlockSpec((1, VT_ROWS, t), lambda b, gg, qi: (gg, 0, b))

    def per_bt(a):
        return a.reshape(g, bsz, t, a.shape[-1])

    return pl.pallas_call(
        functools.partial(_nsa_prompt_kernel, n_cmp=n_cmp, n_slc=n_slc, n_sel=min(N_SELECT, n_slc)),
        grid=(bsz, g, nq),
        in_specs=[pl.BlockSpec((1, hpg, dh, tq), lambda b, gg, qi: (gg, 0, 0, b * nq + qi)),
                  pl.BlockSpec((1, 1, n_chunk, dh), lambda b, gg, qi: (b, gg, 0, 0)),
                  pl.BlockSpec((1, 1, dh, n_chunk), lambda b, gg, qi: (b, gg, 0, 0)),
                  k_spec(), vt_spec(), k_spec(), vt_spec(),
                  pl.BlockSpec((1, tq, LANES), lambda b, gg, qi: (gg, b * nq + qi, 0))],
        out_specs=pl.BlockSpec((tq, hpg * dh), lambda b, gg, qi: (b * nq + qi, gg)),
        out_shape=SDS((bsz * t, NSA_Q), BF16),
        scratch_shapes=[pltpu.VMEM((n_chunk, LANES), F32)] * (tq // LANES) + [
                        pltpu.VMEM((n_chunk // r, tq), F32),
                        pltpu.VMEM((hpg, 1, tq), F32), pltpu.VMEM((hpg, VT_ROWS, tq), F32),
                        pltpu.VMEM((hpg, NSA_TK, tq), F32), pltpu.VMEM((hpg, NSA_TK, tq), F32),
                        pltpu.VMEM((hpg, NSA_TK, tq), BF16), pltpu.VMEM((hpg, NSA_TK, tq), BF16),
                        pltpu.VMEM((hpg, 1, tq), F32), pltpu.VMEM((hpg, 1, tq), F32)],
        compiler_params=_params("parallel", "parallel", "arbitrary"),
    )(q4, kcmp, vcmp_t, per_bt(ks_hm), vs_t, per_bt(kw_hm), vw_t, gates_hm)


def _gla_prompt_kernel(q_ref, k_ref, v_ref, r_ref, misc_ref, wa_ref, ba_ref, ng_ref,
                       o_ref, s_ref, qe_ref, qt_ref, kt_ref, kh_ref, d_ref, u_ref, st_ref):
    t = q_ref.shape[0]
    c = GLA_CHUNK
    n = t // c
    dk, dv = GLA_DK, GLA_DV
    lr = misc_ref[:, N_GATE:N_GATE + GLA_RANK].astype(BF16)
    x = jnp.dot(lr, wa_ref[...], preferred_element_type=F32) + ba_ref[...]
    g = jax.nn.log_sigmoid(x) / GLA_TAU
    row = lax.broadcasted_iota(jnp.int32, (t, dk), 0) % c
    b = g
    sh = 1
    while sh < c:
        b = b + jnp.where(row >= sh, pltpu.roll(b, sh, axis=0), 0.0)
        sh *= 2
    b3 = b.reshape(n, c, dk)
    b_last = jnp.broadcast_to(b3[:, c - 1:c, :], (n, c, dk)).reshape(t, dk)
    b_mid = jnp.broadcast_to(b3[:, c // 2 - 1:c // 2, :], (n, c, dk)).reshape(t, dk)
    q = q_ref[...] * (GLA_DK ** -0.5)
    k = k_ref[...]
    qe_ref[...] = (q * jnp.exp(b)).astype(BF16)
    qt_ref[...] = (q * jnp.exp(b - b_mid)).astype(BF16)
    kt_ref[...] = (k * jnp.exp(b_mid - b)).astype(BF16)
    kh_ref[...] = (k * jnp.exp(b_last - b)).astype(BF16)
    d_ref[...] = jnp.exp(b_last)
    tn = (((0,), (0,)), ((), ()))
    nt = (((1,), (1,)), ((), ()))

    def chunk_update(i, carry):
        r0 = pl.multiple_of(i * c, c)
        u_ref[i] = lax.dot_general(v_ref[pl.ds(r0, c), :], kh_ref[pl.ds(r0, c), :], tn,
                                   preferred_element_type=F32)
        return carry

    lax.fori_loop(0, n, chunk_update, 0, unroll=GLA_UNROLL)

    st_ref[...] = jnp.zeros((dv, dk), F32)

    def recur(i, carry):
        st = st_ref[...]
        st_ref[...] = st * d_ref[pl.ds(pl.multiple_of(i * c, c), 1), :] + u_ref[i]
        u_ref[i] = st
        return carry

    lax.fori_loop(0, n, recur, 0)
    s_ref[0, 0] = st_ref[...].T

    grp = GLA_GROUP
    rows = grp * c
    ri = lax.broadcasted_iota(jnp.int32, (rows, rows), 0)
    ci = lax.broadcasted_iota(jnp.int32, (rows, rows), 1)
    keep = (ri // c == ci // c) & (ri >= ci)
    ng = ng_ref[...]

    def group_out(i, carry):
        r0 = pl.multiple_of(i * rows, rows)
        rs = pl.ds(r0, rows)
        a = lax.dot_general(qt_ref[rs, :], kt_ref[rs, :], nt, preferred_element_type=F32)
        a = jnp.where(keep, a, 0.0).astype(BF16)
        o = jnp.dot(a, v_ref[rs, :], preferred_element_type=F32)
        inter = [lax.dot_general(qe_ref[pl.ds(r0 + j * c, c), :], u_ref[i * grp + j].astype(BF16), nt,
                                 preferred_element_type=F32) for j in range(grp)]
        o = o + jnp.concatenate(inter, axis=0)
        o = _rms(o, ng) * jax.nn.silu(r_ref[rs, :])
        o_ref[rs, :] = o.astype(BF16)
        return carry

    lax.fori_loop(0, n // grp, group_out, 0, unroll=2)


def _gla_prompt(q_l, k_l, v_l, r_l, misc, wa, ba, ng, bsz, t):
    h, dk, dv = GLA_HEADS, GLA_DK, GLA_DV
    assert t % (GLA_CHUNK * GLA_GROUP * 2) == 0
    n = t // GLA_CHUNK
    return pl.pallas_call(
        _gla_prompt_kernel,
        grid=(bsz, h),
        in_specs=[pl.BlockSpec((t, dk), lambda b, hh: (b, hh)), pl.BlockSpec((t, dk), lambda b, hh: (b, hh)),
                  pl.BlockSpec((t, dv), lambda b, hh: (b, hh)), pl.BlockSpec((t, dv), lambda b, hh: (b, hh)),
                  pl.BlockSpec((t, LANES), lambda b, hh: (b, 0)),
                  pl.BlockSpec((GLA_RANK, dk), lambda b, hh: (0, hh)), pl.BlockSpec((1, dk), lambda b, hh: (0, hh)),
                  _resident((1, dv))],
        out_specs=[pl.BlockSpec((t, dv), lambda b, hh: (b, hh)),
                   pl.BlockSpec((1, 1, dk, dv), lambda b, hh: (b, hh, 0, 0))],
        out_shape=[SDS((bsz * t, h * dv), BF16), SDS((bsz, h, dk, dv), F32)],
        scratch_shapes=[pltpu.VMEM((t, dk), BF16), pltpu.VMEM((t, dk), BF16), pltpu.VMEM((t, dk), BF16),
                        pltpu.VMEM((t, dk), BF16), pltpu.VMEM((t, dk), F32),
                        pltpu.VMEM((n, dv, dk), F32), pltpu.VMEM((dv, dk), F32)],
        compiler_params=_params("parallel", "parallel"),
    )(q_l, k_l, v_l, r_l, misc, wa, ba, ng)


def _merge_rows(oa, ob, x, g_ref, wg_ref, wa_ref, wb_ref, wo_ref):
    h = _rms(x, g_ref[...]).astype(BF16)
    d = D_MODEL
    ga = jax.nn.sigmoid(lax.dot_general(h, wg_ref[0:d, :], _NT, preferred_element_type=F32))
    u = ga * jnp.dot(oa, wa_ref[...], preferred_element_type=F32)
    gb = jax.nn.sigmoid(lax.dot_general(h, wg_ref[d:2 * d, :], _NT, preferred_element_type=F32))
    u = u + gb * jnp.dot(ob, wb_ref[...], preferred_element_type=F32)
    return x + jnp.dot(u.astype(BF16), wo_ref[...], preferred_element_type=F32)


def _merge_kernel(oa_ref, ob_ref, x_ref, oas_ref, obs_ref, xs_ref, g_ref, wg_ref, wa_ref, wb_ref, wo_ref,
                  x1_ref, x1s_ref):
    weights = (g_ref, wg_ref, wa_ref, wb_ref, wo_ref)
    x1_ref[...] = _merge_rows(oa_ref[...], ob_ref[...], x_ref[...], *weights)

    @pl.when(pl.program_id(0) == 0)
    def _():
        x1s_ref[...] = _merge_rows(oas_ref[...], obs_ref[...], xs_ref[...], *weights)


def _merge(o_a, o_b, x, oa_s, ob_s, x_s, norm_g, w_t, gate_row0, w_a, w_b, w_o, tm):
    m, d = x.shape
    assert m % tm == 0
    row_specs = [pl.BlockSpec((tm, NSA_Q), lambda i: (i, 0)), pl.BlockSpec((tm, GLA_HEADS * GLA_DV), lambda i: (i, 0)),
                 pl.BlockSpec((tm, d), lambda i: (i, 0))]
    return pl.pallas_call(
        _merge_kernel, grid=(m // tm,),
        in_specs=(row_specs + [_resident(a.shape) for a in (oa_s, ob_s, x_s, norm_g)]
                  + [_row_window(gate_row0, 2 * d, d)] + [_resident(a.shape) for a in (w_a, w_b, w_o)]),
        out_specs=[pl.BlockSpec((tm, d), lambda i: (i, 0)), pl.BlockSpec(x_s.shape, lambda i: (0, 0))],
        out_shape=[SDS((m, d), F32), SDS(x_s.shape, F32)],
        compiler_params=_params("arbitrary"),
    )(o_a, o_b, x, oa_s, ob_s, x_s, norm_g, w_t, w_a, w_b, w_o)


def _mlp_kernel(x1_ref, x1s_ref, g2_ref, wu_ref, wd_ref, gf_ref, y_ref, ys_ref, h_ref):
    i = pl.program_id(0)
    j = pl.program_id(1)
    last = pl.num_programs(1) - 1
    tm = x1_ref.shape[0]

    @pl.when(j == 0)
    def _():
        x1 = x1_ref[...]
        h_ref[0:tm, :] = _rms(x1, g2_ref[...]).astype(BF16)
        y_ref[...] = x1

    @pl.when((j == 0) & (i == 0))
    def _():
        x1s = x1s_ref[...]
        h_ref[tm:, :] = _rms(x1s, g2_ref[...]).astype(BF16)
        ys_ref[...] = x1s

    up = jnp.maximum(jnp.dot(h_ref[...], wu_ref[...].astype(BF16), preferred_element_type=F32), 0.0)
    res = jnp.dot((up * up).astype(BF16), wd_ref[...].astype(BF16), preferred_element_type=F32)
    y_ref[...] += res[0:tm]

    @pl.when(i == 0)
    def _():
        ys_ref[...] += res[tm:]

    @pl.when(j == last)
    def _():
        y_ref[...] = _rms(y_ref[...], gf_ref[...])

    @pl.when((j == last) & (i == 0))
    def _():
        ys_ref[...] = _rms(ys_ref[...], gf_ref[...])


def _mlp(x1, x1_s, g2, w_up, w_down, gf, tm, tf):
    m, d = x1.shape
    ff = w_up.shape[1]
    assert m % tm == 0 and ff % tf == 0
    small = pl.BlockSpec(x1_s.shape, lambda i, j: (0, 0))
    return pl.pallas_call(
        _mlp_kernel, grid=(m // tm, ff // tf),
        in_specs=[pl.BlockSpec((tm, d), lambda i, j: (i, 0)), small, _resident((1, d)),
                  pl.BlockSpec((d, tf), lambda i, j: (0, j)), pl.BlockSpec((tf, d), lambda i, j: (j, 0)),
                  _resident((1, d))],
        out_specs=[pl.BlockSpec((tm, d), lambda i, j: (i, 0)), small],
        out_shape=[SDS((m, d), F32), SDS(x1_s.shape, F32)],
        scratch_shapes=[pltpu.VMEM((tm + x1_s.shape[0], d), BF16)],
        compiler_params=_params("arbitrary", "arbitrary", vmem_limit_bytes=MLP_VMEM_LIMIT_BYTES),
    )(x1, x1_s, g2, w_up, w_down, gf)


def _prep_weights(norm1_g, w_in, b_nsa_gate, cmp_pe_k, cmp_pe_v, cmp_k_w1, cmp_k_w2, cmp_v_w1, cmp_v_w2,
                  gla_w_a2, gla_b_a, gla_norm_g, w_br_a, w_br_b, w_o, norm2_g, w_up, w_down, norm_f):
    pts = [0]
    for s in SPLITS:
        pts.append(pts[-1] + s)
    c_q, c_kv, c_g, c_ql, c_kl, c_vl, c_rl, c_lr, c_br, c_end = pts
    gcols = jnp.asarray([c_g + (g * NSA_HPG + h) * 3 + c for g in range(NSA_KV_HEADS)
                         for c in range(3) for h in range(NSA_HPG)], jnp.int32)
    w_t = w_in.T.astype(BF16)
    pad = jnp.zeros((LANES - N_GATE - GLA_RANK, D_MODEL), BF16)
    w_misc_t = jnp.concatenate([w_t[gcols], w_t[c_lr:c_br], pad], axis=0)
    b_misc = jnp.concatenate([b_nsa_gate[gcols - c_g], jnp.zeros((LANES - N_GATE,), F32)])[None, :]
    w = dict(
        norm1=norm1_g[None, :], norm2=norm2_g[None, :], norm_f=norm_f[None, :],
        w_t=w_t, w_misc_t=w_misc_t, row_a=c_q, rows_a=c_g - c_q, row_b=c_ql, rows_b=c_lr - c_ql, row_gate=c_br,
        b_misc=b_misc,
        gla_wa=gla_w_a2.astype(BF16), gla_ba=gla_b_a[None, :], gla_ng=gla_norm_g[None, :],
        w_br_a=w_br_a.astype(BF16), w_br_b=w_br_b.astype(BF16), w_o=w_o.astype(BF16),
        w_up=w_up, w_down=w_down,
    )
    cw = {}
    cw['pe_k'], cw['w1_k'], cw['w2_k'] = _cmp_weights(cmp_pe_k, cmp_k_w1, cmp_k_w2)
    cw['pe_v'], cw['w1_v'], cw['w2_v'] = _cmp_weights(cmp_pe_v, cmp_v_w1, cmp_v_w2)
    w['cmp'] = cw
    return w


_KV6 = 6 * NSA_KV
_GQ = GLA_HEADS * GLA_DK
_GV = GLA_HEADS * GLA_DV


def _layer_prompt(x, w):
    bsz, t, d = x.shape
    m = bsz * t
    x2 = x.reshape(m, d)
    zb = jnp.zeros((1, LANES), F32)
    spec_a = [('hmt', 0, NSA_Q, NSA_HEAD_DIM ** -0.5 * LOG2_E)]
    spec_a += [('f32t', NSA_Q + j * NSA_KV, NSA_Q + (j + 1) * NSA_KV, 1.0) for j in range(6)]
    spec_a += [('hm' if j % 2 == 0 else 'hmvt', NSA_Q + j * NSA_KV, NSA_Q + (j + 1) * NSA_KV, 1.0)
               for j in range(2, 6)]
    q_t, r0, r1, r2, r3, r4, r5, ks_hm, vs_t, kw_hm, vw_t = _proj(
        x2, w['norm1'], w['w_t'], w['row_a'], w['rows_a'], w['w_misc_t'], zb, spec_a, PROJ_TM, seq_len=t)
    spec_b = [('f32', 0, _GQ, 1.0), ('f32', _GQ, 2 * _GQ, 1.0), ('bf16', 2 * _GQ, 2 * _GQ + _GV, 1.0),
              ('f32', 2 * _GQ + _GV, 2 * _GQ + 2 * _GV, 1.0),
              ('misc', 2 * _GQ + 2 * _GV, 2 * _GQ + 2 * _GV + LANES, 1.0), ('gates', 0, 0, 1.0)]
    q_l, k_l, v_l, r_l, misc, gates_hm = _proj(x2, w['norm1'], w['w_t'], w['row_b'], w['rows_b'], w['w_misc_t'],
                                               w['b_misc'], spec_b, PROJ_TM)

    n_pages = t // PAGE_SIZE
    ident = jnp.broadcast_to(jnp.arange(n_pages, dtype=jnp.int32), (bsz, n_pages))
    kcmp, vcmp = _compress(ident, r0, r1, w['cmp'], paged=False)
    o_a = _nsa_prompt(q_t, ks_hm, vs_t, kw_hm, vw_t, kcmp, vcmp, gates_hm, bsz, t)
    o_b, s_new = _gla_prompt(q_l, k_l, v_l, r_l, misc, w['gla_wa'], w['gla_ba'], w['gla_ng'], bsz, t)
    def token_major(a):
        return a.reshape(bsz, NSA_KV_HEADS, NSA_HEAD_DIM, a.shape[-1]).transpose(0, 3, 1, 2)

    rows = [token_major(a) for a in (r0, r1, r2, r3)]
    n_keep = min(WINDOW, t)
    wins = [token_major(a[:, :, t - n_keep:]) for a in (r4, r5)]
    return (o_a, o_b, x2), rows, wins, s_new


def _channel_mix(prompt, sample, w):
    (oa_p, ob_p, x_p), (oa_s, ob_s, x_s) = prompt, sample
    m = x_p.shape[0]
    x1_p, x1_s = _merge(oa_p, ob_p, x_p, oa_s, ob_s, x_s, w['norm1'], w['w_t'], w['row_gate'], w['w_br_a'],
                        w['w_br_b'], w['w_o'], MERGE_TM)
    tm = next(c for c in (MLP_TM, PROJ_TM, MERGE_TM) if m % c == 0)
    return _mlp(x1_p, x1_s, w['norm2'], w['w_up'], w['w_down'], w['norm_f'], tm, MLP_TF)


def _group_rows(parts):
    rowg = lax.broadcasted_iota(jnp.int32, parts[0].shape, 0) // NSA_HPG
    out = parts[0]
    for g in range(1, NSA_KV_HEADS):
        out = jnp.where(rowg == g, parts[g], out)
    return out


def _sample_select_kernel(q_ref, kc_ref, vc_ref, oc_ref, idx_ref, *, n_cmp, pos, n_pick):
    q = q_ref[0]
    nt = (((1,), (1,)), ((), ()))
    n_chunk = kc_ref.shape[2]
    s = _group_rows([lax.dot_general(q, kc_ref[0, g], nt, preferred_element_type=F32)
                     for g in range(NSA_KV_HEADS)])
    lane = lax.broadcasted_iota(jnp.int32, s.shape, 1)
    mask = (lane < n_cmp) & (lane * CMP_STRIDE + (CMP_BLOCK - 1) <= pos)
    e, l = _softmax_rows(s, mask)
    p = e * _safe_inv(l)
    pb = p.astype(BF16)
    oc_ref[0] = _group_rows([lax.dot_general(pb, vc_ref[0, g], nt, preferred_element_type=F32)
                             for g in range(NSA_KV_HEADS)])
    nr = p.shape[0]
    y = p + pltpu.roll(p, nr - 1, axis=0)
    psum = y + pltpu.roll(y, nr - 2, axis=0)
    imp = pltpu.roll(psum, 1, axis=1) + psum
    r = SLC_BLOCK // CMP_STRIDE
    for o in range(1, r):
        imp = imp + pltpu.roll(psum, n_chunk - o, axis=1)
    blk = lane // r
    is_blk = lane % r == 0
    valid = blk * SLC_BLOCK <= pos
    forced = (blk == 0) | (blk == pos // SLC_BLOCK)
    score = jnp.where(is_blk & valid, imp + jnp.where(forced, FORCE, 0.0), NEG)
    lane_f = lane.astype(F32)
    out_lane = lax.broadcasted_iota(jnp.int32, (nr, LANES), 1)
    picked = jnp.zeros((nr, LANES), F32)
    for k in range(n_pick):
        mx = jnp.max(score, axis=-1, keepdims=True)
        ix = jnp.min(jnp.where(score == mx, lane_f, float(n_chunk)), axis=-1, keepdims=True)
        picked = jnp.where(out_lane == k, ix, picked)
        score = jnp.where(lane_f == ix, 2.0 * NEG, score)
    idx_ref[0] = picked.astype(jnp.int32) // r


def _sample_select(q_s, kcmp, vcmp, pos, n_pick):
    bsz, g, n_chunk, dh = kcmp.shape
    nh = NSA_HEADS
    cmp_spec = pl.BlockSpec((1, g, n_chunk, dh), lambda b: (b, 0, 0, 0))
    cmpt_spec = pl.BlockSpec((1, g, dh, n_chunk), lambda b: (b, 0, 0, 0))
    return pl.pallas_call(
        functools.partial(_sample_select_kernel, n_cmp=n_chunk - 1, pos=pos, n_pick=n_pick),
        grid=(bsz,),
        in_specs=[pl.BlockSpec((1, nh, dh), lambda b: (b, 0, 0)), cmp_spec, cmpt_spec],
        out_specs=[pl.BlockSpec((1, nh, dh), lambda b: (b, 0, 0)), pl.BlockSpec((1, nh, LANES), lambda b: (b, 0, 0))],
        out_shape=[SDS((bsz, nh, dh), F32), SDS((bsz, nh, LANES), jnp.int32)],
        compiler_params=_params("parallel"),
    )(q_s, kcmp, vcmp)


def _sample_attend_kernel(idx_ref, pt_ref, q_ref, oc_ref, ks_hbm, vs_hbm, kw_ref, vw_ref,
                          nks_ref, nvs_ref, nkw_ref, nvw_ref, gate_ref, o_ref, ksel, vsel, sem, *, n_pick):
    b = pl.program_id(0)
    g_n, dh = NSA_KV_HEADS, NSA_HEAD_DIM
    half = PAGE_SIZE // SLC_BLOCK

    def copies():
        out = []
        for g in range(g_n):
            for r in range(n_pick):
                page = pt_ref[b, idx_ref[b * g_n + g, r] // half]
                out.append(pltpu.make_async_copy(ks_hbm.at[page, g], ksel.at[g, r], sem.at[0]))
                out.append(pltpu.make_async_copy(vs_hbm.at[page, g], vsel.at[g, r], sem.at[1]))
        return out

    for c in copies():
        c.start()

    q = q_ref[0]
    qf = q.astype(F32)
    nt = (((1,), (1,)), ((), ()))

    def attend(keys_t, vals_t, bias, k_new, v_new):
        s = _group_rows([jnp.dot(q, keys_t[g].astype(BF16), preferred_element_type=F32)
                         + (0.0 if bias is None else bias[g]) for g in range(g_n)])
        s_new = _group_rows([jnp.sum(qf * k_new[:, g * dh:(g + 1) * dh], axis=-1, keepdims=True)
                             for g in range(g_n)])
        m = jnp.maximum(jnp.max(s, axis=-1, keepdims=True), s_new)
        e = jnp.exp(s - m)
        e_new = jnp.exp(s_new - m)
        l = jnp.sum(e, axis=-1, keepdims=True) + e_new
        eb = e.astype(BF16)
        acc = _group_rows([lax.dot_general(eb, vals_t[g].astype(BF16), nt, preferred_element_type=F32)
                           + e_new * v_new[:, g * dh:(g + 1) * dh] for g in range(g_n)])
        return acc / l

    o_w = attend([kw_ref[0, g] for g in range(g_n)], [vw_ref[0, g] for g in range(g_n)], None,
                 nkw_ref[0], nvw_ref[0])
    for c in copies():
        c.wait()
    lin = lax.broadcasted_iota(jnp.int32, (1, PAGE_SIZE), 1)
    bias = []
    for g in range(g_n):
        parts = []
        for r in range(n_pick):
            off = (idx_ref[b * g_n + g, r] % half) * SLC_BLOCK
            parts.append(jnp.where((lin >= off) & (lin < off + SLC_BLOCK), 0.0, NEG))
        bias.append(jnp.concatenate(parts, axis=1))

    def tiles(buf, g):
        return jnp.concatenate([buf[g, r] for r in range(n_pick)], axis=1)

    o_s = attend([tiles(ksel, g) for g in range(g_n)], [tiles(vsel, g) for g in range(g_n)], bias,
                 nks_ref[0], nvs_ref[0])

    gt = jnp.broadcast_to(gate_ref[0], (LANES, LANES)).T
    nh = NSA_HEADS
    o_ref[0] = (gt[0:nh, 0:dh] * oc_ref[0] + gt[nh:2 * nh, 0:dh] * o_s + gt[2 * nh:3 * nh, 0:dh] * o_w)


def _sample_attend(idx, page_table, q_s, o_c, slc_k, slc_v, win_k, win_v, new_rows, gates, n_pick):
    bsz = page_table.shape[0]
    nh, dh, g = NSA_HEADS, NSA_HEAD_DIM, NSA_KV_HEADS
    wlen = win_k.shape[-1]
    row_spec = pl.BlockSpec((1, 1, NSA_KV), lambda b, *_: (b, 0, 0))
    win_spec = pl.BlockSpec((1, g, dh, wlen), lambda b, *_: (b, 0, 0, 0))
    head_spec = pl.BlockSpec((1, nh, dh), lambda b, *_: (b, 0, 0))
    any_spec = pl.BlockSpec(memory_space=pl.ANY)
    grid_spec = pltpu.PrefetchScalarGridSpec(
        num_scalar_prefetch=2, grid=(bsz,),
        in_specs=[head_spec, head_spec, any_spec, any_spec, win_spec, win_spec,
                  row_spec, row_spec, row_spec, row_spec, pl.BlockSpec((1, 1, LANES), lambda b, *_: (b, 0, 0))],
        out_specs=head_spec,
        scratch_shapes=[pltpu.VMEM((g, n_pick, dh, PAGE_SIZE), F32), pltpu.VMEM((g, n_pick, dh, PAGE_SIZE), F32),
                        pltpu.SemaphoreType.DMA((2,))])
    return pl.pallas_call(
        functools.partial(_sample_attend_kernel, n_pick=n_pick),
        grid_spec=grid_spec, out_shape=SDS((bsz, nh, dh), F32),
        compiler_params=_params("arbitrary"),
    )(idx, page_table, q_s, o_c, slc_k, slc_v, win_k, win_v, *new_rows, gates)


def _gla_sample_kernel(q_ref, k_ref, v_ref, r_ref, misc_ref, wa_ref, ba_ref, ng_ref, s0_ref, o_ref, s_ref):
    dk, dv = GLA_DK, GLA_DV
    lr = jnp.broadcast_to(misc_ref[0][:, N_GATE:N_GATE + GLA_RANK], (16, GLA_RANK)).astype(BF16)
    x = jnp.dot(lr, wa_ref[...], preferred_element_type=F32)[0:1] + ba_ref[...]
    g_all = jax.nn.log_sigmoid(x) / GLA_TAU

    def col(v):
        t = jnp.broadcast_to(v, (dk, dk)).T
        return jnp.concatenate([t] * (dv // dk), axis=1)

    outs = []
    for h in range(GLA_HEADS):
        g = g_all[:, h * dk:(h + 1) * dk]
        q = q_ref[0][:, h * dk:(h + 1) * dk] * (GLA_DK ** -0.5)
        k = k_ref[0][:, h * dk:(h + 1) * dk]
        v = v_ref[0][:, h * dv:(h + 1) * dv]
        s0 = s0_ref[0, h]
        q_t = q * jnp.exp(g)
        k_t = k * jnp.exp(-g)
        a = jnp.sum(q_t * k_t, axis=-1, keepdims=True)
        o = jnp.sum(col(q_t) * s0, axis=0, keepdims=True) + a * v
        s_ref[0, h] = col(jnp.exp(g)) * s0 + col(k) * v
        outs.append(_rms(o, ng_ref[...]) * jax.nn.silu(r_ref[0][:, h * dv:(h + 1) * dv]))
    o_ref[0] = jnp.concatenate(outs, axis=-1)


def _gla_sample(q_l, k_l, v_l, r_l, misc, wa, ba, ng, s0):
    bsz, h, dk, dv = s0.shape

    def row(n):
        return pl.BlockSpec((1, 1, n), lambda b: (b, 0, 0))

    st_spec = pl.BlockSpec((1, h, dk, dv), lambda b: (b, 0, 0, 0))
    return pl.pallas_call(
        _gla_sample_kernel, grid=(bsz,),
        in_specs=[row(h * dk), row(h * dk), row(h * dv), row(h * dv), row(LANES),
                  _resident(wa.shape), _resident(ba.shape), _resident(ng.shape), st_spec],
        out_specs=[row(h * dv), st_spec],
        out_shape=[SDS((bsz, 1, h * dv), F32), SDS((bsz, h, dk, dv), F32)],
        compiler_params=_params("parallel"),
    )(q_l, k_l, v_l, r_l, misc, wa, ba, ng, s0)


SAMPLE_ROWS = 16


def _layer_sample(x, caches, wins, s0, page_table, w):
    bsz, t, d = x.shape
    n_pages = page_table.shape[1]
    pos = n_pages * PAGE_SIZE
    assert t == 1 and bsz <= SAMPLE_ROWS and pos % SLC_BLOCK == 0 and wins[0].shape[1] == WINDOW
    assert pos // SLC_BLOCK >= N_SELECT
    mp = SAMPLE_ROWS
    x2 = jnp.pad(x.reshape(bsz, d), ((0, mp - bsz), (0, 0)))
    zb = jnp.zeros((1, LANES), F32)
    spec_a = [('bf16', 0, NSA_Q, NSA_HEAD_DIM ** -0.5)]
    spec_a += [('f32', NSA_Q + j * NSA_KV, NSA_Q + (j + 1) * NSA_KV, 1.0) for j in range(6)]
    q_s, r0, r1, r2, r3, r4, r5 = _proj(x2, w['norm1'], w['w_t'], w['row_a'], w['rows_a'], w['w_misc_t'], zb,
                                        spec_a, mp)
    spec_b = [('f32', 0, _GQ, 1.0), ('f32', _GQ, 2 * _GQ, 1.0), ('f32', 2 * _GQ, 2 * _GQ + _GV, 1.0),
              ('f32', 2 * _GQ + _GV, 2 * _GQ + 2 * _GV, 1.0),
              ('misc', 2 * _GQ + 2 * _GV, 2 * _GQ + 2 * _GV + LANES, 1.0)]
    q_l, k_l, v_l, r_l, misc = _proj(x2, w['norm1'], w['w_t'], w['row_b'], w['rows_b'], w['w_misc_t'], w['b_misc'],
                                     spec_b, mp)

    cache_t = [a.transpose(0, 2, 3, 1) for a in caches]
    win_t = [a.transpose(0, 2, 3, 1) for a in wins]
    n_pool = caches[0].shape[0]
    cmp_pages = [a.reshape(n_pool, NSA_KV, PAGE_SIZE) for a in cache_t[:2]]
    kcmp, vcmp = _compress(page_table, cmp_pages[0], cmp_pages[1], w['cmp'], paged=True)
    q_h = q_s.reshape(mp, NSA_HEADS, NSA_HEAD_DIM)
    n_pick = N_SELECT - 1
    o_c, idx = _sample_select(q_h, kcmp, vcmp, pos, n_pick)
    idx = idx[:, ::NSA_HPG, :n_pick].reshape(bsz * NSA_KV_HEADS, n_pick)
    gates = misc[:, :N_GATE].reshape(mp, NSA_KV_HEADS, 3, NSA_HPG).transpose(0, 2, 1, 3).reshape(mp, 1, N_GATE)
    gates = jnp.pad(gates, ((0, 0), (0, 0), (0, LANES - N_GATE)))
    new_rows = [a.reshape(mp, 1, NSA_KV) for a in (r2, r3, r4, r5)]
    o_a = _sample_attend(idx, page_table, q_h, o_c, cache_t[2], cache_t[3], win_t[0], win_t[1], new_rows, gates,
                         n_pick)
    o_a = jnp.pad(o_a.reshape(bsz, NSA_Q), ((0, mp - bsz), (0, 0))).astype(BF16)

    def r3d(a):
        return a.reshape(mp, 1, a.shape[-1])

    o_b, s_new = _gla_sample(r3d(q_l), r3d(k_l), r3d(v_l), r3d(r_l), r3d(misc), w['gla_wa'], w['gla_ba'],
                             w['gla_ng'], s0)
    o_b = jnp.pad(o_b.reshape(bsz, _GV), ((0, mp - bsz), (0, 0))).astype(BF16)
    kvh = (bsz, 1, NSA_KV_HEADS, NSA_HEAD_DIM)
    rows = [a[:bsz].reshape(kvh) for a in (r0, r1, r2, r3)]
    new_wins = [jnp.concatenate([c[:, 1:], a[:bsz].reshape(kvh)], axis=1) for c, a in zip(wins, (r4, r5))]
    return (o_a, o_b, x2), rows, new_wins, s_new


def kernel(x_prompt, x_sample, cache_cmp_k, cache_cmp_v, cache_slc_k, cache_slc_v, cache_win_k, cache_win_v, state_gla, page_table, norm1_g, w_in, b_nsa_gate, cmp_pe_k, cmp_pe_v, cmp_k_w1, cmp_k_w2, cmp_v_w1, cmp_v_w2, gla_w_a2, gla_b_a, gla_norm_g, w_br_a, w_br_b, w_o, norm2_g, w_up, w_down, norm_f):
    assert DEPTH == 1 and norm1_g.shape[0] == 1
    w = _prep_weights(norm1_g[0], w_in[0], b_nsa_gate[0], cmp_pe_k[0], cmp_pe_v[0], cmp_k_w1[0], cmp_k_w2[0],
                      cmp_v_w1[0], cmp_v_w2[0], gla_w_a2[0], gla_b_a[0], gla_norm_g[0], w_br_a[0], w_br_b[0],
                      w_o[0], norm2_g[0], w_up[0], w_down[0], norm_f)
    mix_p, rows_p, wins_p, s_p = _layer_prompt(x_prompt, w)
    caches = [c[0] for c in (cache_cmp_k, cache_cmp_v, cache_slc_k, cache_slc_v)]
    mix_s, rows_s, wins_s, s_s = _layer_sample(x_sample, caches, [cache_win_k[0], cache_win_v[0]], state_gla[0],
                                               page_table, w)
    y_p, y_s = _channel_mix(mix_p, mix_s, w)
    y_p = y_p.reshape(x_prompt.shape)
    y_s = y_s[:x_sample.shape[0]].reshape(x_sample.shape)
    outs_p = [a[None] for a in rows_p + wins_p + [s_p]]
    outs_s = [a[None] for a in rows_s + wins_s + [s_s]]
    return (y_p, y_s, *outs_p, *outs_s)
```

```python
import functools

import jax
import jax.numpy as jnp
from jax import lax
from jax.experimental import pallas as pl
from jax.experimental.pallas import tpu as pltpu

D_MODEL = 2048
DEPTH = 1
PAGE_SIZE = 128
NSA_HEADS = 16
NSA_KV_HEADS = 4
NSA_HPG = NSA_HEADS // NSA_KV_HEADS
NSA_HEAD_DIM = 64
NSA_Q = NSA_HEADS * NSA_HEAD_DIM
NSA_KV = NSA_KV_HEADS * NSA_HEAD_DIM
CMP_STRIDE = 16
CMP_BLOCK = 32
CMP_HIDDEN = 128
SLC_BLOCK = 64
N_SELECT = 16
WINDOW = 512
GLA_HEADS = 4
GLA_DK = (D_MODEL // 4) // GLA_HEADS
GLA_DV = (D_MODEL // 2) // GLA_HEADS
GLA_RANK = 16
GLA_TAU = 16.0
D_FF = 4 * D_MODEL
EPS = 1e-6
NEG = -1e30
FORCE = 1e4
SPLITS = (NSA_Q, 6 * NSA_KV, 3 * NSA_HEADS,
          GLA_HEADS * GLA_DK, GLA_HEADS * GLA_DK, GLA_HEADS * GLA_DV, GLA_HEADS * GLA_DV,
          GLA_RANK, 2 * D_MODEL)

F32 = jnp.float32
BF16 = jnp.bfloat16
LANES = 128
VMEM_LIMIT_BYTES = 56 * 1024 * 1024
MLP_VMEM_LIMIT_BYTES = 60 * 1024 * 1024
N_GATE = 3 * NSA_HEADS
GLA_CHUNK = 32
GLA_UNROLL = 8
GLA_GROUP = 4
NSA_TQ = 256
NSA_TK = 256
VT_ROWS = NSA_HEAD_DIM + 16
CMP_PAGES = 16
PROJ_TM = 512
MERGE_TM = 256
MLP_TM = 1024
MLP_TF = 512
SDS = jax.ShapeDtypeStruct


def _params(*sem, vmem_limit_bytes=VMEM_LIMIT_BYTES):
    return pltpu.CompilerParams(dimension_semantics=sem, vmem_limit_bytes=vmem_limit_bytes)


def _resident(shape):
    nd = len(shape)
    return pl.BlockSpec(shape, lambda *_: (0,) * nd, pipeline_mode=pl.Buffered(1))


def _row_window(row0, n_rows, width):
    return pl.BlockSpec((pl.Element(n_rows), pl.Element(width)), lambda *_: (row0, 0),
                        pipeline_mode=pl.Buffered(1))


def _rms(x, g):
    return x * lax.rsqrt(jnp.mean(x * x, axis=-1, keepdims=True) + EPS) * g


_NT = (((1,), (1,)), ((), ()))


def _proj_kernel(x_ref, g_ref, w_ref, wm_ref, b_ref, *out_refs, spec):
    h = _rms(x_ref[...], g_ref[...]).astype(BF16)
    misc = None
    products = {}
    for o_ref, (kind, c0, c1, scale) in zip(out_refs, spec):
        if kind == 'gates':
            for g in range(NSA_KV_HEADS):
                o_ref[g] = misc if g == 0 else pltpu.roll(misc, LANES - g * 3 * NSA_HPG, axis=1)
            continue
        if (kind == 'misc', c0, c1) not in products:
            rows = wm_ref[...] if kind == 'misc' else w_ref[c0:c1, :]
            products[(kind == 'misc', c0, c1)] = lax.dot_general(h, rows, _NT, preferred_element_type=F32)
        r = products[(kind == 'misc', c0, c1)]
        if scale != 1.0:
            r = r * scale
        if kind == 'f32':
            o_ref[...] = r
        elif kind == 'f32t':
            o_ref[0] = r.T
        elif kind == 'bf16':
            o_ref[...] = r.astype(BF16)
        elif kind == 'sigmoid':
            o_ref[...] = jax.nn.sigmoid(r)
        elif kind == 'hm':
            for i in range((c1 - c0) // NSA_HEAD_DIM):
                o_ref[i] = r[:, i * NSA_HEAD_DIM:(i + 1) * NSA_HEAD_DIM].astype(BF16)
        elif kind == 'hmt':
            rt = r.T
            for i in range((c1 - c0) // NSA_HEAD_DIM):
                o_ref[i] = rt[i * NSA_HEAD_DIM:(i + 1) * NSA_HEAD_DIM].astype(BF16)
        elif kind == 'hmvt':
            rt = r.T
            sub = lax.broadcasted_iota(jnp.int32, (VT_ROWS - NSA_HEAD_DIM, r.shape[0]), 0)
            ones = jnp.where(sub == 0, 1.0, 0.0)
            for i in range((c1 - c0) // NSA_HEAD_DIM):
                piece = rt[i * NSA_HEAD_DIM:(i + 1) * NSA_HEAD_DIM]
                o_ref[i] = jnp.concatenate([piece, ones], axis=0).astype(BF16)
        elif kind == 'misc':
            lane = lax.broadcasted_iota(jnp.int32, r.shape, 1)
            misc = jnp.where(lane < N_GATE, jax.nn.sigmoid(r + b_ref[...]), r)
            o_ref[...] = misc


def _proj(x, norm_g, w_t, row0, n_rows, w_misc_t, bias, spec, tm, seq_len=None):
    m, d = x.shape
    assert m % tm == 0 and w_t.shape[1] == d
    out_shape, out_specs = [], []
    for kind, c0, c1, _ in spec:
        if kind == 'f32t':
            assert seq_len % tm == 0
            per_seq = seq_len // tm
            out_shape.append(SDS((m // seq_len, c1 - c0, seq_len), F32))
            out_specs.append(pl.BlockSpec((1, c1 - c0, tm), lambda i: (i // per_seq, 0, i % per_seq)))
        elif kind == 'hm':
            nh = (c1 - c0) // NSA_HEAD_DIM
            out_shape.append(SDS((nh, m, NSA_HEAD_DIM), BF16))
            out_specs.append(pl.BlockSpec((nh, tm, NSA_HEAD_DIM), lambda i: (0, i, 0)))
        elif kind in ('hmt', 'hmvt'):
            nh = (c1 - c0) // NSA_HEAD_DIM
            nrow = NSA_HEAD_DIM if kind == 'hmt' else VT_ROWS
            out_shape.append(SDS((nh, nrow, m), BF16))
            out_specs.append(pl.BlockSpec((nh, nrow, tm), lambda i: (0, 0, i)))
        elif kind == 'gates':
            out_shape.append(SDS((NSA_KV_HEADS, m, LANES), F32))
            out_specs.append(pl.BlockSpec((NSA_KV_HEADS, tm, LANES), lambda i: (0, i, 0)))
        else:
            out_shape.append(SDS((m, c1 - c0), BF16 if kind == 'bf16' else F32))
            out_specs.append(pl.BlockSpec((tm, c1 - c0), lambda i: (i, 0)))
    return pl.pallas_call(
        functools.partial(_proj_kernel, spec=tuple(spec)),
        grid=(m // tm,),
        in_specs=[pl.BlockSpec((tm, d), lambda i: (i, 0)), _resident((1, d)), _row_window(row0, n_rows, d),
                  _resident(w_misc_t.shape), _resident((1, LANES))],
        out_specs=out_specs, out_shape=out_shape,
        compiler_params=_params("parallel"),
    )(x, norm_g, w_t, w_misc_t, bias)


def _cmp_kernel(pt_ref, k_hbm, v_hbm, perm_ref, pek_ref, pev_ref, w1k_ref, w1v_ref, w2k_ref, w2v_ref,
                ok_ref, ov_ref, kbuf, vbuf, hk, hv, sem, *, n_pages_step, paged):
    b = pl.program_id(0)
    s = pl.program_id(1)
    ns = pl.num_programs(1)
    t = b * ns + s
    total = pl.num_programs(0) * ns
    rows = n_pages_step * 8
    dh, hid_n = NSA_HEAD_DIM, CMP_HIDDEN

    def copies(tt, slot):
        bb = tt // ns
        ss = tt % ns
        out = []
        for p in range(n_pages_step):
            page = pt_ref[bb, ss * n_pages_step + p]
            if paged:
                src = [hbm.at[page] for hbm in (k_hbm, v_hbm)]
            else:
                tok = pl.ds(pl.multiple_of(page * PAGE_SIZE, PAGE_SIZE), PAGE_SIZE)
                src = [hbm.at[bb, :, tok] for hbm in (k_hbm, v_hbm)]
            out.append(pltpu.make_async_copy(src[0], kbuf.at[slot, p], sem.at[slot, 0]))
            out.append(pltpu.make_async_copy(src[1], vbuf.at[slot, p], sem.at[slot, 1]))
        return out

    slot = t % 2

    @pl.when(t == 0)
    def _():
        for c in copies(t, slot):
            c.start()

    @pl.when(t + 1 < total)
    def _():
        for c in copies(t + 1, 1 - slot):
            c.start()

    for c in copies(t, slot):
        c.wait()

    r0 = pl.multiple_of(s * rows, rows)
    low = lax.broadcasted_iota(jnp.int32, (8, LANES), 1) < dh
    tok_pairs = CMP_STRIDE // 2
    nt = (((1,), (1,)), ((), ()))
    perm = perm_ref[...]
    for buf, w1_ref, h_ref in ((kbuf, w1k_ref, hk), (vbuf, w1v_ref, hv)):
        parts = [[[] for _ in range(tok_pairs)] for _ in range(NSA_KV_HEADS)]
        for p in range(n_pages_step):
            r = lax.dot_general(perm, buf[slot, p].astype(BF16), nt, preferred_element_type=F32)
            for q in range(2):
                for m in range(tok_pairs):
                    x0 = r[2 * m * 8:2 * m * 8 + 8, q * LANES:(q + 1) * LANES]
                    x1 = r[(2 * m + 1) * 8:(2 * m + 1) * 8 + 8, q * LANES:(q + 1) * LANES]
                    parts[2 * q][m].append(jnp.where(low, x0, pltpu.roll(x1, dh, axis=1)))
                    parts[2 * q + 1][m].append(jnp.where(low, pltpu.roll(x0, dh, axis=1), x1))
        for g in range(NSA_KV_HEADS):
            z = jnp.concatenate([jnp.concatenate(parts[g][m], axis=0) for m in range(tok_pairs)],
                                axis=1).astype(BF16)
            h_ref[pl.ds(r0, rows), g * 2 * hid_n:(g + 1) * 2 * hid_n] = jnp.dot(
                z, w1_ref[...], preferred_element_type=F32)

    @pl.when(s == ns - 1)
    def _():
        for h_ref, pe_ref, w1_ref, w2_ref, o_ref in ((hk, pek_ref, w1k_ref, w2k_ref, ok_ref),
                                                     (hv, pev_ref, w1v_ref, w2v_ref, ov_ref)):
            n_chunk = h_ref.shape[0]
            pe = jnp.broadcast_to(pe_ref[...], (2, 16, pe_ref.shape[2])).astype(BF16)
            pe_term = (jnp.dot(pe[0], w1_ref[:, 0:hid_n], preferred_element_type=F32)
                       + jnp.dot(pe[1], w1_ref[:, hid_n:2 * hid_n], preferred_element_type=F32))[0:1]
            hid = []
            for g in range(NSA_KV_HEADS):
                first = h_ref[:, g * 2 * hid_n:g * 2 * hid_n + hid_n]
                last = h_ref[:, g * 2 * hid_n + hid_n:(g + 1) * 2 * hid_n]
                hid.append(first + pltpu.roll(last, n_chunk - 1, axis=0) + pe_term)
            hid = jnp.concatenate(hid, axis=1)
            res = jnp.dot(jax.nn.gelu(hid).astype(BF16), w2_ref[...], preferred_element_type=F32)
            if o_ref is ok_ref:
                for g in range(NSA_KV_HEADS):
                    o_ref[0, g] = res[:, g * NSA_HEAD_DIM:(g + 1) * NSA_HEAD_DIM].astype(BF16)
            else:
                res_t = res.T
                for g in range(NSA_KV_HEADS):
                    o_ref[0, g] = res_t[g * NSA_HEAD_DIM:(g + 1) * NSA_HEAD_DIM].astype(BF16)


def _cmp_weights(pe, w1, w2):
    g = NSA_KV_HEADS
    half = CMP_STRIDE * NSA_HEAD_DIM
    pe_t = pe.reshape(2, 1, half)
    w1ab = jnp.concatenate([w1[:half], w1[half:]], axis=1)
    w2b = jnp.einsum('jd,gh->gjhd', w2, jnp.eye(g, dtype=F32)).reshape(g * CMP_HIDDEN, g * NSA_HEAD_DIM)
    return pe_t, w1ab.astype(BF16), w2b.astype(BF16)


def _compress(page_table, k_src, v_src, cw, paged):
    bsz, n_pages = page_table.shape
    p_step = min(CMP_PAGES, n_pages)
    ns = n_pages // p_step
    n_chunk = n_pages * 8
    page_shape = (NSA_KV, PAGE_SIZE)
    assert NSA_KV == 2 * LANES
    assert k_src.shape[1:] == (page_shape if paged else (NSA_KV, n_pages * PAGE_SIZE))
    half = CMP_STRIDE * NSA_HEAD_DIM
    gh = NSA_KV_HEADS * CMP_HIDDEN
    out_row = jnp.arange(PAGE_SIZE, dtype=jnp.int32)
    src_tok = (out_row % 8) * CMP_STRIDE + out_row // 8
    perm = (src_tok[:, None] == jnp.arange(PAGE_SIZE, dtype=jnp.int32)[None, :]).astype(BF16)
    k_sds =SDS((bsz, NSA_KV_HEADS, n_chunk, NSA_HEAD_DIM), BF16)
    v_sds = SDS((bsz, NSA_KV_HEADS, NSA_HEAD_DIM, n_chunk), BF16)
    k_spec = pl.BlockSpec((1, NSA_KV_HEADS, n_chunk, NSA_HEAD_DIM), lambda b, s, pt: (b, 0, 0, 0))
    v_spec = pl.BlockSpec((1, NSA_KV_HEADS, NSA_HEAD_DIM, n_chunk), lambda b, s, pt: (b, 0, 0, 0))
    grid_spec = pltpu.PrefetchScalarGridSpec(
        num_scalar_prefetch=1, grid=(bsz, ns),
        in_specs=[pl.BlockSpec(memory_space=pl.ANY), pl.BlockSpec(memory_space=pl.ANY),
                  _resident((PAGE_SIZE, PAGE_SIZE)), _resident((2, 1, half)), _resident((2, 1, half)),
                  _resident((half, 2 * CMP_HIDDEN)), _resident((half, 2 * CMP_HIDDEN)),
                  _resident((gh, NSA_KV)), _resident((gh, NSA_KV))],
        out_specs=[k_spec, v_spec],
        scratch_shapes=[pltpu.VMEM((2, p_step) + page_shape, F32), pltpu.VMEM((2, p_step) + page_shape, F32),
                        pltpu.VMEM((n_chunk, 2 * gh), F32), pltpu.VMEM((n_chunk, 2 * gh), F32),
                        pltpu.SemaphoreType.DMA((2, 2))])
    return pl.pallas_call(
        functools.partial(_cmp_kernel, n_pages_step=p_step, paged=paged),
        grid_spec=grid_spec, out_shape=[k_sds, v_sds],
        compiler_params=_params("arbitrary", "arbitrary"),
    )(page_table, k_src, v_src, perm, cw['pe_k'], cw['pe_v'], cw['w1_k'], cw['w1_v'], cw['w2_k'], cw['w2_v'])


def _softmax_rows(s, mask):
    sm = jnp.where(mask, s, NEG)
    m = jnp.max(sm, axis=-1, keepdims=True)
    e = jnp.where(mask, jnp.exp(sm - m), 0.0)
    return e, jnp.sum(e, axis=-1, keepdims=True)


def _safe_inv(l):
    return jnp.where(l > 0.0, 1.0 / jnp.where(l > 0.0, l, 1.0), 0.0)


M_INIT = -1e20
LOG2_E = 1.4426950408889634


def _nsa_prompt_kernel(qt_ref, kc_ref, vct_ref, ks_ref, vst_ref, kw_ref, vwt_ref, gate_ref,
                       o_ref, *scratch, n_cmp, n_slc, n_sel):
    tq, tk, hpg, dh = NSA_TQ, NSA_TK, NSA_HPG, NSA_HEAD_DIM
    rk_refs = scratch[:tq // LANES]
    rest = scratch[tq // LANES:]
    selb_ref = rest[0]
    m_ref, acc_ref, s_refs, p_refs, a_refs = rest[1], rest[2], rest[3:5], rest[5:7], rest[7:9]
    mw_ref, accw_ref, sw_refs, pw_refs, aw_refs = rest[9], rest[10], rest[11:13], rest[13:15], rest[15:17]
    qi = pl.program_id(2)
    q0 = qi * tq
    n_chunk = kc_ref.shape[2]
    r = SLC_BLOCK // CMP_STRIDE
    n_row = n_chunk // r

    ci = lax.broadcasted_iota(jnp.int32, (n_chunk, tq), 0)
    pos_c = q0 + lax.broadcasted_iota(jnp.int32, (n_chunk, tq), 1)
    m_c = (ci < n_cmp) & (ci * CMP_STRIDE + (CMP_BLOCK - 1) <= pos_c)
    kc = kc_ref[0, 0]
    vct = vct_ref[0, 0]
    o_c = []
    psum = None
    for h in range(hpg):
        s = jnp.where(m_c, jnp.dot(kc, qt_ref[0, h], preferred_element_type=F32), NEG)
        e = jnp.where(m_c, jnp.exp2(s - jnp.max(s, axis=0, keepdims=True)), 0.0)
        p = e * _safe_inv(jnp.sum(e, axis=0, keepdims=True))
        o_c.append(jnp.dot(vct, p.astype(BF16), preferred_element_type=F32))
        psum = p if psum is None else psum + p

    imp = pltpu.roll(psum, 1, axis=0) + psum
    for o in range(1, r):
        imp = imp + pltpu.roll(psum, n_chunk - o, axis=0)
    parts = []
    for i, rk_ref in enumerate(rk_refs):
        rk_ref[...] = imp[:, i * LANES:(i + 1) * LANES]
        parts.append(rk_ref[pl.ds(0, n_row, stride=r), :])
    imp_b = jnp.concatenate(parts, axis=1)
    j_io = lax.broadcasted_iota(jnp.int32, (n_row, tq), 0)
    pos_b = q0 + lax.broadcasted_iota(jnp.int32, (n_row, tq), 1)
    valid = (j_io < n_slc) & (j_io * SLC_BLOCK <= pos_b)
    forced = (j_io == 0) | (j_io == pos_b // SLC_BLOCK)
    sc = jnp.where(valid, imp_b + jnp.where(forced, FORCE, 0.0), NEG)
    rank = jnp.zeros((n_row, tq), F32)
    for k in range(n_slc):
        ck = sc[k:k + 1, :]
        beats = (ck > sc) | ((ck == sc) & (j_io > k))
        rank = rank + jnp.where(beats, 1.0, 0.0)
    selb_ref[...] = jnp.where(rank < n_sel, 0.0, NEG)

    def key_minus_query(n_keys):
        return (lax.broadcasted_iota(jnp.int32, (n_keys, tq), 0)
                - lax.broadcasted_iota(jnp.int32, (n_keys, tq), 1))

    def start(m_ref, acc_ref):
        m_ref[...] = jnp.full(m_ref.shape, M_INIT, F32)
        acc_ref[...] = jnp.zeros(acc_ref.shape, F32)

    def scores(kk, bias, s_ref):
        for h in range(hpg):
            for c in range(kk.shape[0] // SLC_BLOCK):
                rs = slice(c * SLC_BLOCK, (c + 1) * SLC_BLOCK)
                s_ref[h, rs, :] = jnp.dot(kk[rs], qt_ref[0, h], preferred_element_type=F32) + bias[rs]

    def softmax(s_ref, p_ref, a_ref, m_ref):
        for h in range(hpg):
            for c in range(tq // LANES):
                cs = slice(c * LANES, (c + 1) * LANES)
                s = s_ref[h, :, cs]
                m_prev = m_ref[h, :, cs]
                m_new = jnp.maximum(m_prev, jnp.max(s, axis=0, keepdims=True))
                p_ref[h, :, cs] = jnp.exp2(s - m_new).astype(BF16)
                a_ref[h, :, cs] = jnp.exp2(m_prev - m_new)
                m_ref[h, :, cs] = m_new

    def weighted_values(vt, p_ref, a_ref, acc_ref):
        for h in range(hpg):
            acc_ref[h] = a_ref[h] * acc_ref[h] + jnp.dot(vt, p_ref[h], preferred_element_type=F32)

    def finish(acc_ref):
        return [acc_ref[h, 0:dh, :] * _safe_inv(acc_ref[h, dh:dh + 1, :]) for h in range(hpg)]

    def window_attention():
        tkw = sw_refs[0].shape[1]
        base = jnp.maximum(q0 - WINDOW, 0)
        dlt = key_minus_query(tkw)
        start(mw_ref, accw_ref)
        tiles = [pl.multiple_of(base + i * tkw, LANES) for i in range(2)]
        for i, k0 in enumerate(tiles):
            d = dlt + (k0 - q0)
            bias = jnp.where((d <= 0) & (d >= -WINDOW), 0.0, NEG)
            scores(kw_ref[0, 0, pl.ds(k0, tkw), :], bias, sw_refs[i])
        for i in range(2):
            softmax(sw_refs[i], pw_refs[i], aw_refs[i], mw_ref)
        for i, k0 in enumerate(tiles):
            weighted_values(vwt_ref[0, :, pl.ds(k0, tkw)], pw_refs[i], aw_refs[i], accw_ref)

    def selected_attention():
        n_tiles = ks_ref.shape[2] // tk
        blocks_per_tile = tk // SLC_BLOCK
        dlt = key_minus_query(tk)
        hi = (q0 + tq - 1) // tk + 1
        start(m_ref, acc_ref)
        p_refs[1][...] = jnp.zeros(p_refs[1].shape, BF16)
        a_refs[1][...] = jnp.ones(a_refs[1].shape, F32)

        def tile_start(kb):
            return pl.multiple_of(jnp.clip(kb, 0, n_tiles - 1) * tk, tk)

        def tile_scores(kb, s_ref):
            first = jnp.minimum(kb, n_tiles - 1) * blocks_per_tile
            rows = [jnp.broadcast_to(selb_ref[pl.ds(first + i, 1), :], (SLC_BLOCK, tq))
                    for i in range(blocks_per_tile)]
            bias = jnp.where(dlt + (kb * tk - q0) <= 0, jnp.concatenate(rows, axis=0), NEG)
            scores(ks_ref[0, 0, pl.ds(tile_start(kb), tk), :], bias, s_ref)

        def tile_values(kb, p_ref, a_ref):
            weighted_values(vst_ref[0, :, pl.ds(tile_start(kb), tk)], p_ref, a_ref, acc_ref)

        tile_scores(0, s_refs[0])

        def body(j, carry):
            i = 2 * j
            tile_scores(i + 1, s_refs[1])
            softmax(s_refs[0], p_refs[0], a_refs[0], m_ref)
            tile_values(i - 1, p_refs[1], a_refs[1])
            tile_scores(i + 2, s_refs[0])
            softmax(s_refs[1], p_refs[1], a_refs[1], m_ref)
            tile_values(i, p_refs[0], a_refs[0])
            return carry

        n_pairs = (hi + 1) // 2
        lax.fori_loop(0, n_pairs, body, 0)
        tile_values(2 * n_pairs - 1, p_refs[1], a_refs[1])
        return finish(acc_ref)

    window_attention()
    o_s = selected_attention()
    o_w = finish(accw_ref)

    gt = gate_ref[0].T
    outs = [gt[h:h + 1] * o_c[h] + gt[hpg + h:hpg + h + 1] * o_s[h] + gt[2 * hpg + h:2 * hpg + h + 1] * o_w[h]
            for h in range(hpg)]
    o_ref[...] = jnp.concatenate(outs, axis=0).T.astype(BF16)


def _nsa_prompt(q_t, ks_hm, vs_t, kw_hm, vw_t, kcmp, vcmp_t, gates_hm, bsz, t):
    g, hpg, dh, tq = NSA_KV_HEADS, NSA_HPG, NSA_HEAD_DIM, NSA_TQ
    r = SLC_BLOCK // CMP_STRIDE
    n_slc = t // SLC_BLOCK
    assert t % NSA_TK == 0 and t % tq == 0 and n_slc * r <= LANES and NSA_TK % SLC_BLOCK == 0
    nq = t // tq
    n_cmp = kcmp.shape[2] - 1
    if kcmp.shape[2] < LANES:
        fill = LANES - kcmp.shape[2]
        kcmp = jnp.pad(kcmp, ((0, 0), (0, 0), (0, fill), (0, 0)))
        vcmp_t = jnp.pad(vcmp_t, ((0, 0), (0, 0), (0, 0), (0, fill)))
    n_chunk = kcmp.shape[2]
    assert n_chunk == LANES
    q4 = q_t.reshape(g, hpg, dh, bsz * t)

    def k_spec():
        return pl.BlockSpec((1, 1, t, dh), lambda b, gg, qi: (gg, b, 0, 0))

    def vt_spec():
        return pl.BlockSpec((1, VT_ROWS, t), lambda b, gg, qi: (gg, 0, b))

    def per_bt(a):
        return a.reshape(g, bsz, t, a.shape[-1])

    def flash_scratch(n_keys):
        return [pltpu.VMEM((hpg, 1, tq), F32), pltpu.VMEM((hpg, VT_ROWS, tq), F32),
                pltpu.VMEM((hpg, n_keys, tq), F32), pltpu.VMEM((hpg, n_keys, tq), F32),
                pltpu.VMEM((hpg, n_keys, tq), BF16), pltpu.VMEM((hpg, n_keys, tq), BF16),
                pltpu.VMEM((hpg, 1, tq), F32), pltpu.VMEM((hpg, 1, tq), F32)]

    tk_win = (WINDOW + tq) // 2
    assert tk_win % LANES == 0 and t >= WINDOW + tq

    return pl.pallas_call(
        functools.partial(_nsa_prompt_kernel, n_cmp=n_cmp, n_slc=n_slc, n_sel=min(N_SELECT, n_slc)),
        grid=(bsz, g, nq),
        in_specs=[pl.BlockSpec((1, hpg, dh, tq), lambda b, gg, qi: (gg, 0, 0, b * nq + qi)),
                  pl.BlockSpec((1, 1, n_chunk, dh), lambda b, gg, qi: (b, gg, 0, 0)),
                  pl.BlockSpec((1, 1, dh, n_chunk), lambda b, gg, qi: (b, gg, 0, 0)),
                  k_spec(), vt_spec(), k_spec(), vt_spec(),
                  pl.BlockSpec((1, tq, LANES), lambda b, gg, qi: (gg, b * nq + qi, 0))],
        out_specs=pl.BlockSpec((tq, hpg * dh), lambda b, gg, qi: (b * nq + qi, gg)),
        out_shape=SDS((bsz * t, NSA_Q), BF16),
        scratch_shapes=([pltpu.VMEM((n_chunk, LANES), F32)] * (tq // LANES)
                        + [pltpu.VMEM((n_chunk // r, tq), F32)]
                        + flash_scratch(NSA_TK) + flash_scratch(tk_win)),
        compiler_params=_params("parallel", "parallel", "arbitrary"),
    )(q4, kcmp, vcmp_t, per_bt(ks_hm), vs_t, per_bt(kw_hm), vw_t, gates_hm)


def _gla_prompt_kernel(q_ref, k_ref, v_ref, r_ref, misc_ref, wa_ref, ba_ref, ng_ref,
                       o_ref, s_ref, qe_ref, qt_ref, kt_ref, kh_ref, d_ref, u_ref, st_ref):
    t = q_ref.shape[0]
    c = GLA_CHUNK
    n = t // c
    dk, dv = GLA_DK, GLA_DV
    lr = misc_ref[:, N_GATE:N_GATE + GLA_RANK].astype(BF16)
    x = jnp.dot(lr, wa_ref[...], preferred_element_type=F32) + ba_ref[...]
    g = jax.nn.log_sigmoid(x) / GLA_TAU
    row = lax.broadcasted_iota(jnp.int32, (t, dk), 0) % c
    b = g
    sh = 1
    while sh < c:
        b = b + jnp.where(row >= sh, pltpu.roll(b, sh, axis=0), 0.0)
        sh *= 2
    b3 = b.reshape(n, c, dk)
    b_last = jnp.broadcast_to(b3[:, c - 1:c, :], (n, c, dk)).reshape(t, dk)
    b_mid = jnp.broadcast_to(b3[:, c // 2 - 1:c // 2, :], (n, c, dk)).reshape(t, dk)
    q = q_ref[...] * (GLA_DK ** -0.5)
    k = k_ref[...]
    qe_ref[...] = (q * jnp.exp(b)).astype(BF16)
    qt_ref[...] = (q * jnp.exp(b - b_mid)).astype(BF16)
    kt_ref[...] = (k * jnp.exp(b_mid - b)).astype(BF16)
    kh_ref[...] = (k * jnp.exp(b_last - b)).astype(BF16)
    d_ref[...] = jnp.exp(b_last)
    tn = (((0,), (0,)), ((), ()))
    nt = (((1,), (1,)), ((), ()))

    def chunk_update(i, carry):
        r0 = pl.multiple_of(i * c, c)
        u_ref[i] = lax.dot_general(v_ref[pl.ds(r0, c), :], kh_ref[pl.ds(r0, c), :], tn,
                                   preferred_element_type=F32)
        return carry

    lax.fori_loop(0, n, chunk_update, 0, unroll=GLA_UNROLL)

    st_ref[...] = jnp.zeros((dv, dk), F32)

    def recur(i, carry):
        st = st_ref[...]
        st_ref[...] = st * d_ref[pl.ds(pl.multiple_of(i * c, c), 1), :] + u_ref[i]
        u_ref[i] = st
        return carry

    lax.fori_loop(0, n, recur, 0)
    s_ref[0, 0] = st_ref[...].T

    grp = GLA_GROUP
    rows = grp * c
    ri = lax.broadcasted_iota(jnp.int32, (rows, rows), 0)
    ci = lax.broadcasted_iota(jnp.int32, (rows, rows), 1)
    keep = (ri // c == ci // c) & (ri >= ci)
    ng = ng_ref[...]

    def group_out(i, carry):
        r0 = pl.multiple_of(i * rows, rows)
        rs = pl.ds(r0, rows)
        a = lax.dot_general(qt_ref[rs, :], kt_ref[rs, :], nt, preferred_element_type=F32)
        a = jnp.where(keep, a, 0.0).astype(BF16)
        o = jnp.dot(a, v_ref[rs, :], preferred_element_type=F32)
        inter = [lax.dot_general(qe_ref[pl.ds(r0 + j * c, c), :], u_ref[i * grp + j].astype(BF16), nt,
                                 preferred_element_type=F32) for j in range(grp)]
        o = o + jnp.concatenate(inter, axis=0)
        o = _rms(o, ng) * jax.nn.silu(r_ref[rs, :])
        o_ref[rs, :] = o.astype(BF16)
        return carry

    lax.fori_loop(0, n // grp, group_out, 0, unroll=2)


def _gla_prompt(q_l, k_l, v_l, r_l, misc, wa, ba, ng, bsz, t):
    h, dk, dv = GLA_HEADS, GLA_DK, GLA_DV
    assert t % (GLA_CHUNK * GLA_GROUP * 2) == 0
    n = t // GLA_CHUNK
    return pl.pallas_call(
        _gla_prompt_kernel,
        grid=(bsz, h),
        in_specs=[pl.BlockSpec((t, dk), lambda b, hh: (b, hh)), pl.BlockSpec((t, dk), lambda b, hh: (b, hh)),
                  pl.BlockSpec((t, dv), lambda b, hh: (b, hh)), pl.BlockSpec((t, dv), lambda b, hh: (b, hh)),
                  pl.BlockSpec((t, LANES), lambda b, hh: (b, 0)),
                  pl.BlockSpec((GLA_RANK, dk), lambda b, hh: (0, hh)), pl.BlockSpec((1, dk), lambda b, hh: (0, hh)),
                  _resident((1, dv))],
        out_specs=[pl.BlockSpec((t, dv), lambda b, hh: (b, hh)),
                   pl.BlockSpec((1, 1, dk, dv), lambda b, hh: (b, hh, 0, 0))],
        out_shape=[SDS((bsz * t, h * dv), BF16), SDS((bsz, h, dk, dv), F32)],
        scratch_shapes=[pltpu.VMEM((t, dk), BF16), pltpu.VMEM((t, dk), BF16), pltpu.VMEM((t, dk), BF16),
                        pltpu.VMEM((t, dk), BF16), pltpu.VMEM((t, dk), F32),
                        pltpu.VMEM((n, dv, dk), F32), pltpu.VMEM((dv, dk), F32)],
        compiler_params=_params("parallel", "parallel"),
    )(q_l, k_l, v_l, r_l, misc, wa, ba, ng)


def _merge_rows(oa, ob, x, g_ref, wg_ref, wa_ref, wb_ref, wo_ref):
    h = _rms(x, g_ref[...]).astype(BF16)
    d = D_MODEL
    ga = jax.nn.sigmoid(lax.dot_general(h, wg_ref[0:d, :], _NT, preferred_element_type=F32))
    u = ga * jnp.dot(oa, wa_ref[...], preferred_element_type=F32)
    gb = jax.nn.sigmoid(lax.dot_general(h, wg_ref[d:2 * d, :], _NT, preferred_element_type=F32))
    u = u + gb * jnp.dot(ob, wb_ref[...], preferred_element_type=F32)
    return x + jnp.dot(u.astype(BF16), wo_ref[...], preferred_element_type=F32)


def _merge_kernel(oa_ref, ob_ref, x_ref, oas_ref, obs_ref, xs_ref, g_ref, wg_ref, wa_ref, wb_ref, wo_ref,
                  x1_ref, x1s_ref):
    weights = (g_ref, wg_ref, wa_ref, wb_ref, wo_ref)
    x1_ref[...] = _merge_rows(oa_ref[...], ob_ref[...], x_ref[...], *weights)

    @pl.when(pl.program_id(0) == 0)
    def _():
        x1s_ref[...] = _merge_rows(oas_ref[...], obs_ref[...], xs_ref[...], *weights)


def _merge(o_a, o_b, x, oa_s, ob_s, x_s, norm_g, w_t, gate_row0, w_a, w_b, w_o, tm):
    m, d = x.shape
    assert m % tm == 0
    row_specs = [pl.BlockSpec((tm, NSA_Q), lambda i: (i, 0)), pl.BlockSpec((tm, GLA_HEADS * GLA_DV), lambda i: (i, 0)),
                 pl.BlockSpec((tm, d), lambda i: (i, 0))]
    return pl.pallas_call(
        _merge_kernel, grid=(m // tm,),
        in_specs=(row_specs + [_resident(a.shape) for a in (oa_s, ob_s, x_s, norm_g)]
                  + [_row_window(gate_row0, 2 * d, d)] + [_resident(a.shape) for a in (w_a, w_b, w_o)]),
        out_specs=[pl.BlockSpec((tm, d), lambda i: (i, 0)), pl.BlockSpec(x_s.shape, lambda i: (0, 0))],
        out_shape=[SDS((m, d), F32), SDS(x_s.shape, F32)],
        compiler_params=_params("arbitrary"),
    )(o_a, o_b, x, oa_s, ob_s, x_s, norm_g, w_t, w_a, w_b, w_o)


def _mlp_kernel(x1_ref, x1s_ref, g2_ref, wu_ref, wd_ref, gf_ref, y_ref, ys_ref, h_ref):
    i = pl.program_id(0)
    j = pl.program_id(1)
    last = pl.num_programs(1) - 1
    tm = x1_ref.shape[0]

    @pl.when(j == 0)
    def _():
        x1 = x1_ref[...]
        h_ref[0:tm, :] = _rms(x1, g2_ref[...]).astype(BF16)
        y_ref[...] = x1

    @pl.when((j == 0) & (i == 0))
    def _():
        x1s = x1s_ref[...]
        h_ref[tm:, :] = _rms(x1s, g2_ref[...]).astype(BF16)
        ys_ref[...] = x1s

    up = jnp.maximum(jnp.dot(h_ref[...], wu_ref[...].astype(BF16), preferred_element_type=F32), 0.0)
    res = jnp.dot((up * up).astype(BF16), wd_ref[...].astype(BF16), preferred_element_type=F32)
    y_ref[...] += res[0:tm]

    @pl.when(i == 0)
    def _():
        ys_ref[...] += res[tm:]

    @pl.when(j == last)
    def _():
        y_ref[...] = _rms(y_ref[...], gf_ref[...])

    @pl.when((j == last) & (i == 0))
    def _():
        ys_ref[...] = _rms(ys_ref[...], gf_ref[...])


def _mlp(x1, x1_s, g2, w_up, w_down, gf, tm, tf):
    m, d = x1.shape
    ff = w_up.shape[1]
    assert m % tm == 0 and ff % tf == 0
    small = pl.BlockSpec(x1_s.shape, lambda i, j: (0, 0))
    return pl.pallas_call(
        _mlp_kernel, grid=(m // tm, ff // tf),
        in_specs=[pl.BlockSpec((tm, d), lambda i, j: (i, 0)), small, _resident((1, d)),
                  pl.BlockSpec((d, tf), lambda i, j: (0, j)), pl.BlockSpec((tf, d), lambda i, j: (j, 0)),
                  _resident((1, d))],
        out_specs=[pl.BlockSpec((tm, d), lambda i, j: (i, 0)), small],
        out_shape=[SDS((m, d), F32), SDS(x1_s.shape, F32)],
        scratch_shapes=[pltpu.VMEM((tm + x1_s.shape[0], d), BF16)],
        compiler_params=_params("arbitrary", "arbitrary", vmem_limit_bytes=MLP_VMEM_LIMIT_BYTES),
    )(x1, x1_s, g2, w_up, w_down, gf)


def _prep_weights(norm1_g, w_in, b_nsa_gate, cmp_pe_k, cmp_pe_v, cmp_k_w1, cmp_k_w2, cmp_v_w1, cmp_v_w2,
                  gla_w_a2, gla_b_a, gla_norm_g, w_br_a, w_br_b, w_o, norm2_g, w_up, w_down, norm_f):
    pts = [0]
    for s in SPLITS:
        pts.append(pts[-1] + s)
    c_q, c_kv, c_g, c_ql, c_kl, c_vl, c_rl, c_lr, c_br, c_end = pts
    gcols = jnp.asarray([c_g + (g * NSA_HPG + h) * 3 + c for g in range(NSA_KV_HEADS)
                         for c in range(3) for h in range(NSA_HPG)], jnp.int32)
    w_t = w_in.T.astype(BF16)
    pad = jnp.zeros((LANES - N_GATE - GLA_RANK, D_MODEL), BF16)
    w_misc_t = jnp.concatenate([w_t[gcols], w_t[c_lr:c_br], pad], axis=0)
    b_misc = jnp.concatenate([b_nsa_gate[gcols - c_g], jnp.zeros((LANES - N_GATE,), F32)])[None, :]
    w = dict(
        norm1=norm1_g[None, :], norm2=norm2_g[None, :], norm_f=norm_f[None, :],
        w_t=w_t, w_misc_t=w_misc_t, row_a=c_q, rows_a=c_g - c_q, row_b=c_ql, rows_b=c_lr - c_ql, row_gate=c_br,
        b_misc=b_misc,
        gla_wa=gla_w_a2.astype(BF16), gla_ba=gla_b_a[None, :], gla_ng=gla_norm_g[None, :],
        w_br_a=w_br_a.astype(BF16), w_br_b=w_br_b.astype(BF16), w_o=w_o.astype(BF16),
        w_up=w_up, w_down=w_down,
    )
    cw = {}
    cw['pe_k'], cw['w1_k'], cw['w2_k'] = _cmp_weights(cmp_pe_k, cmp_k_w1, cmp_k_w2)
    cw['pe_v'], cw['w1_v'], cw['w2_v'] = _cmp_weights(cmp_pe_v, cmp_v_w1, cmp_v_w2)
    w['cmp'] = cw
    return w


_KV6 = 6 * NSA_KV
_GQ = GLA_HEADS * GLA_DK
_GV = GLA_HEADS * GLA_DV


def _layer_prompt(x, w):
    bsz, t, d = x.shape
    m = bsz * t
    x2 = x.reshape(m, d)
    zb = jnp.zeros((1, LANES), F32)
    spec_a = [('hmt', 0, NSA_Q, NSA_HEAD_DIM ** -0.5 * LOG2_E)]
    for j in range(6):
        cols = (NSA_Q + j * NSA_KV, NSA_Q + (j + 1) * NSA_KV, 1.0)
        spec_a.append(('f32t',) + cols)
        if j >= 2:
            spec_a.append(('hm' if j % 2 == 0 else 'hmvt',) + cols)
    q_t, r0, r1, r2, ks_hm, r3, vs_t, r4, kw_hm, r5, vw_t = _proj(
        x2, w['norm1'], w['w_t'], w['row_a'], w['rows_a'], w['w_misc_t'], zb, spec_a, PROJ_TM, seq_len=t)
    spec_b = [('f32', 0, _GQ, 1.0), ('f32', _GQ, 2 * _GQ, 1.0), ('bf16', 2 * _GQ, 2 * _GQ + _GV, 1.0),
              ('f32', 2 * _GQ + _GV, 2 * _GQ + 2 * _GV, 1.0),
              ('misc', 2 * _GQ + 2 * _GV, 2 * _GQ + 2 * _GV + LANES, 1.0), ('gates', 0, 0, 1.0)]
    q_l, k_l, v_l, r_l, misc, gates_hm = _proj(x2, w['norm1'], w['w_t'], w['row_b'], w['rows_b'], w['w_misc_t'],
                                               w['b_misc'], spec_b, PROJ_TM)

    n_pages = t // PAGE_SIZE
    ident = jnp.broadcast_to(jnp.arange(n_pages, dtype=jnp.int32), (bsz, n_pages))
    kcmp, vcmp = _compress(ident, r0, r1, w['cmp'], paged=False)
    o_a = _nsa_prompt(q_t, ks_hm, vs_t, kw_hm, vw_t, kcmp, vcmp, gates_hm, bsz, t)
    o_b, s_new = _gla_prompt(q_l, k_l, v_l, r_l, misc, w['gla_wa'], w['gla_ba'], w['gla_ng'], bsz, t)
    def token_major(a):
        return a.reshape(bsz, NSA_KV_HEADS, NSA_HEAD_DIM, a.shape[-1]).transpose(0, 3, 1, 2)

    rows = [token_major(a) for a in (r0, r1, r2, r3)]
    n_keep = min(WINDOW, t)
    wins = [token_major(a[:, :, t - n_keep:]) for a in (r4, r5)]
    return (o_a, o_b, x2), rows, wins, s_new


def _channel_mix(prompt, sample, w):
    (oa_p, ob_p, x_p), (oa_s, ob_s, x_s) = prompt, sample
    m = x_p.shape[0]
    x1_p, x1_s = _merge(oa_p, ob_p, x_p, oa_s, ob_s, x_s, w['norm1'], w['w_t'], w['row_gate'], w['w_br_a'],
                        w['w_br_b'], w['w_o'], MERGE_TM)
    tm = next(c for c in (MLP_TM, PROJ_TM, MERGE_TM) if m % c == 0)
    return _mlp(x1_p, x1_s, w['norm2'], w['w_up'], w['w_down'], w['norm_f'], tm, MLP_TF)


def _group_rows(parts):
    rowg = lax.broadcasted_iota(jnp.int32, parts[0].shape, 0) // NSA_HPG
    out = parts[0]
    for g in range(1, NSA_KV_HEADS):
        out = jnp.where(rowg == g, parts[g], out)
    return out


def _sample_select_kernel(q_ref, kc_ref, vc_ref, oc_ref, idx_ref, *, n_cmp, pos, n_pick):
    q = q_ref[0]
    nt = (((1,), (1,)), ((), ()))
    n_chunk = kc_ref.shape[2]
    s = _group_rows([lax.dot_general(q, kc_ref[0, g], nt, preferred_element_type=F32)
                     for g in range(NSA_KV_HEADS)])
    lane = lax.broadcasted_iota(jnp.int32, s.shape, 1)
    mask = (lane < n_cmp) & (lane * CMP_STRIDE + (CMP_BLOCK - 1) <= pos)
    e, l = _softmax_rows(s, mask)
    p = e * _safe_inv(l)
    pb = p.astype(BF16)
    oc_ref[0] = _group_rows([lax.dot_general(pb, vc_ref[0, g], nt, preferred_element_type=F32)
                             for g in range(NSA_KV_HEADS)])
    nr = p.shape[0]
    y = p + pltpu.roll(p, nr - 1, axis=0)
    psum = y + pltpu.roll(y, nr - 2, axis=0)
    imp = pltpu.roll(psum, 1, axis=1) + psum
    r = SLC_BLOCK // CMP_STRIDE
    for o in range(1, r):
        imp = imp + pltpu.roll(psum, n_chunk - o, axis=1)
    blk = lane // r
    is_blk = lane % r == 0
    valid = blk * SLC_BLOCK <= pos
    forced = (blk == 0) | (blk == pos // SLC_BLOCK)
    score = jnp.where(is_blk & valid, imp + jnp.where(forced, FORCE, 0.0), NEG)
    lane_f = lane.astype(F32)
    out_lane = lax.broadcasted_iota(jnp.int32, (nr, LANES), 1)
    picked = jnp.zeros((nr, LANES), F32)
    for k in range(n_pick):
        mx = jnp.max(score, axis=-1, keepdims=True)
        ix = jnp.min(jnp.where(score == mx, lane_f, float(n_chunk)), axis=-1, keepdims=True)
        picked = jnp.where(out_lane == k, ix, picked)
        score = jnp.where(lane_f == ix, 2.0 * NEG, score)
    idx_ref[0] = picked.astype(jnp.int32) // r


def _sample_select(q_s, kcmp, vcmp, pos, n_pick):
    bsz, g, n_chunk, dh = kcmp.shape
    nh = NSA_HEADS
    cmp_spec = pl.BlockSpec((1, g, n_chunk, dh), lambda b: (b, 0, 0, 0))
    cmpt_spec = pl.BlockSpec((1, g, dh, n_chunk), lambda b: (b, 0, 0, 0))
    return pl.pallas_call(
        functools.partial(_sample_select_kernel, n_cmp=n_chunk - 1, pos=pos, n_pick=n_pick),
        grid=(bsz,),
        in_specs=[pl.BlockSpec((1, nh, dh), lambda b: (b, 0, 0)), cmp_spec, cmpt_spec],
        out_specs=[pl.BlockSpec((1, nh, dh), lambda b: (b, 0, 0)), pl.BlockSpec((1, nh, LANES), lambda b: (b, 0, 0))],
        out_shape=[SDS((bsz, nh, dh), F32), SDS((bsz, nh, LANES), jnp.int32)],
        compiler_params=_params("parallel"),
    )(q_s, kcmp, vcmp)


def _sample_attend_kernel(idx_ref, pt_ref, q_ref, oc_ref, ks_hbm, vs_hbm, kw_ref, vw_ref,
                          nks_ref, nvs_ref, nkw_ref, nvw_ref, gate_ref, o_ref, ksel, vsel, sem, *, n_pick):
    b = pl.program_id(0)
    g_n, dh = NSA_KV_HEADS, NSA_HEAD_DIM
    half = PAGE_SIZE // SLC_BLOCK

    def copies():
        out = []
        for g in range(g_n):
            for r in range(n_pick):
                page = pt_ref[b, idx_ref[b * g_n + g, r] // half]
                out.append(pltpu.make_async_copy(ks_hbm.at[page, g], ksel.at[g, r], sem.at[0]))
                out.append(pltpu.make_async_copy(vs_hbm.at[page, g], vsel.at[g, r], sem.at[1]))
        return out

    for c in copies():
        c.start()

    q = q_ref[0]
    qf = q.astype(F32)
    nt = (((1,), (1,)), ((), ()))

    def attend(keys_t, vals_t, bias, k_new, v_new):
        s = _group_rows([jnp.dot(q, keys_t[g].astype(BF16), preferred_element_type=F32)
                         + (0.0 if bias is None else bias[g]) for g in range(g_n)])
        s_new = _group_rows([jnp.sum(qf * k_new[:, g * dh:(g + 1) * dh], axis=-1, keepdims=True)
                             for g in range(g_n)])
        m = jnp.maximum(jnp.max(s, axis=-1, keepdims=True), s_new)
        e = jnp.exp(s - m)
        e_new = jnp.exp(s_new - m)
        l = jnp.sum(e, axis=-1, keepdims=True) + e_new
        eb = e.astype(BF16)
        acc = _group_rows([lax.dot_general(eb, vals_t[g].astype(BF16), nt, preferred_element_type=F32)
                           + e_new * v_new[:, g * dh:(g + 1) * dh] for g in range(g_n)])
        return acc / l

    o_w = attend([kw_ref[0, g] for g in range(g_n)], [vw_ref[0, g] for g in range(g_n)], None,
                 nkw_ref[0], nvw_ref[0])
    for c in copies():
        c.wait()
    lin = lax.broadcasted_iota(jnp.int32, (1, PAGE_SIZE), 1)
    bias = []
    for g in range(g_n):
        parts = []
        for r in range(n_pick):
            off = (idx_ref[b * g_n + g, r] % half) * SLC_BLOCK
            parts.append(jnp.where((lin >= off) & (lin < off + SLC_BLOCK), 0.0, NEG))
        bias.append(jnp.concatenate(parts, axis=1))

    def tiles(buf, g):
        return jnp.concatenate([buf[g, r] for r in range(n_pick)], axis=1)

    o_s = attend([tiles(ksel, g) for g in range(g_n)], [tiles(vsel, g) for g in range(g_n)], bias,
                 nks_ref[0], nvs_ref[0])

    gt = jnp.broadcast_to(gate_ref[0], (LANES, LANES)).T
    nh = NSA_HEADS
    o_ref[0] = (gt[0:nh, 0:dh] * oc_ref[0] + gt[nh:2 * nh, 0:dh] * o_s + gt[2 * nh:3 * nh, 0:dh] * o_w)


def _sample_attend(idx, page_table, q_s, o_c, slc_k, slc_v, win_k, win_v, new_rows, gates, n_pick):
    bsz = page_table.shape[0]
    nh, dh, g = NSA_HEADS, NSA_HEAD_DIM, NSA_KV_HEADS
    wlen = win_k.shape[-1]
    row_spec = pl.BlockSpec((1, 1, NSA_KV), lambda b, *_: (b, 0, 0))
    win_spec = pl.BlockSpec((1, g, dh, wlen), lambda b, *_: (b, 0, 0, 0))
    head_spec = pl.BlockSpec((1, nh, dh), lambda b, *_: (b, 0, 0))
    any_spec = pl.BlockSpec(memory_space=pl.ANY)
    grid_spec = pltpu.PrefetchScalarGridSpec(
        num_scalar_prefetch=2, grid=(bsz,),
        in_specs=[head_spec, head_spec, any_spec, any_spec, win_spec, win_spec,
                  row_spec, row_spec, row_spec, row_spec, pl.BlockSpec((1, 1, LANES), lambda b, *_: (b, 0, 0))],
        out_specs=head_spec,
        scratch_shapes=[pltpu.VMEM((g, n_pick, dh, PAGE_SIZE), F32), pltpu.VMEM((g, n_pick, dh, PAGE_SIZE), F32),
                        pltpu.SemaphoreType.DMA((2,))])
    return pl.pallas_call(
        functools.partial(_sample_attend_kernel, n_pick=n_pick),
        grid_spec=grid_spec, out_shape=SDS((bsz, nh, dh), F32),
        compiler_params=_params("arbitrary"),
    )(idx, page_table, q_s, o_c, slc_k, slc_v, win_k, win_v, *new_rows, gates)


def _gla_sample_kernel(q_ref, k_ref, v_ref, r_ref, misc_ref, wa_ref, ba_ref, ng_ref, s0_ref, o_ref, s_ref):
    dk, dv = GLA_DK, GLA_DV
    lr = jnp.broadcast_to(misc_ref[0][:, N_GATE:N_GATE + GLA_RANK], (16, GLA_RANK)).astype(BF16)
    x = jnp.dot(lr, wa_ref[...], preferred_element_type=F32)[0:1] + ba_ref[...]
    g_all = jax.nn.log_sigmoid(x) / GLA_TAU

    def col(v):
        t = jnp.broadcast_to(v, (dk, dk)).T
        return jnp.concatenate([t] * (dv // dk), axis=1)

    outs = []
    for h in range(GLA_HEADS):
        g = g_all[:, h * dk:(h + 1) * dk]
        q = q_ref[0][:, h * dk:(h + 1) * dk] * (GLA_DK ** -0.5)
        k = k_ref[0][:, h * dk:(h + 1) * dk]
        v = v_ref[0][:, h * dv:(h + 1) * dv]
        s0 = s0_ref[0, h]
        q_t = q * jnp.exp(g)
        k_t = k * jnp.exp(-g)
        a = jnp.sum(q_t * k_t, axis=-1, keepdims=True)
        o = jnp.sum(col(q_t) * s0, axis=0, keepdims=True) + a * v
        s_ref[0, h] = col(jnp.exp(g)) * s0 + col(k) * v
        outs.append(_rms(o, ng_ref[...]) * jax.nn.silu(r_ref[0][:, h * dv:(h + 1) * dv]))
    o_ref[0] = jnp.concatenate(outs, axis=-1)


def _gla_sample(q_l, k_l, v_l, r_l, misc, wa, ba, ng, s0):
    bsz, h, dk, dv = s0.shape

    def row(n):
        return pl.BlockSpec((1, 1, n), lambda b: (b, 0, 0))

    st_spec = pl.BlockSpec((1, h, dk, dv), lambda b: (b, 0, 0, 0))
    return pl.pallas_call(
        _gla_sample_kernel, grid=(bsz,),
        in_specs=[row(h * dk), row(h * dk), row(h * dv), row(h * dv), row(LANES),
                  _resident(wa.shape), _resident(ba.shape), _resident(ng.shape), st_spec],
        out_specs=[row(h * dv), st_spec],
        out_shape=[SDS((bsz, 1, h * dv), F32), SDS((bsz, h, dk, dv), F32)],
        compiler_params=_params("parallel"),
    )(q_l, k_l, v_l, r_l, misc, wa, ba, ng, s0)


SAMPLE_ROWS = 16


def _layer_sample(x, caches, wins, s0, page_table, w):
    bsz, t, d = x.shape
    n_pages = page_table.shape[1]
    pos = n_pages * PAGE_SIZE
    assert t == 1 and bsz <= SAMPLE_ROWS and pos % SLC_BLOCK == 0 and wins[0].shape[1] == WINDOW
    assert pos // SLC_BLOCK >= N_SELECT
    mp = SAMPLE_ROWS
    x2 = jnp.pad(x.reshape(bsz, d), ((0, mp - bsz), (0, 0)))
    zb = jnp.zeros((1, LANES), F32)
    spec_a = [('bf16', 0, NSA_Q, NSA_HEAD_DIM ** -0.5)]
    spec_a += [('f32', NSA_Q + j * NSA_KV, NSA_Q + (j + 1) * NSA_KV, 1.0) for j in range(6)]
    q_s, r0, r1, r2, r3, r4, r5 = _proj(x2, w['norm1'], w['w_t'], w['row_a'], w['rows_a'], w['w_misc_t'], zb,
                                        spec_a, mp)
    spec_b = [('f32', 0, _GQ, 1.0), ('f32', _GQ, 2 * _GQ, 1.0), ('f32', 2 * _GQ, 2 * _GQ + _GV, 1.0),
              ('f32', 2 * _GQ + _GV, 2 * _GQ + 2 * _GV, 1.0),
              ('misc', 2 * _GQ + 2 * _GV, 2 * _GQ + 2 * _GV + LANES, 1.0)]
    q_l, k_l, v_l, r_l, misc = _proj(x2, w['norm1'], w['w_t'], w['row_b'], w['rows_b'], w['w_misc_t'], w['b_misc'],
                                     spec_b, mp)

    cache_t = [a.transpose(0, 2, 3, 1) for a in caches]
    win_t = [a.transpose(0, 2, 3, 1) for a in wins]
    n_pool = caches[0].shape[0]
    cmp_pages = [a.reshape(n_pool, NSA_KV, PAGE_SIZE) for a in cache_t[:2]]
    kcmp, vcmp = _compress(page_table, cmp_pages[0], cmp_pages[1], w['cmp'], paged=True)
    q_h = q_s.reshape(mp, NSA_HEADS, NSA_HEAD_DIM)
    n_pick = N_SELECT - 1
    o_c, idx = _sample_select(q_h, kcmp, vcmp, pos, n_pick)
    idx = idx[:, ::NSA_HPG, :n_pick].reshape(bsz * NSA_KV_HEADS, n_pick)
    gates = misc[:, :N_GATE].reshape(mp, NSA_KV_HEADS, 3, NSA_HPG).transpose(0, 2, 1, 3).reshape(mp, 1, N_GATE)
    gates = jnp.pad(gates, ((0, 0), (0, 0), (0, LANES - N_GATE)))
    new_rows = [a.reshape(mp, 1, NSA_KV) for a in (r2, r3, r4, r5)]
    o_a = _sample_attend(idx, page_table, q_h, o_c, cache_t[2], cache_t[3], win_t[0], win_t[1], new_rows, gates,
                         n_pick)
    o_a = jnp.pad(o_a.reshape(bsz, NSA_Q), ((0, mp - bsz), (0, 0))).astype(BF16)

    def r3d(a):
        return a.reshape(mp, 1, a.shape[-1])

    o_b, s_new = _gla_sample(r3d(q_l), r3d(k_l), r3d(v_l), r3d(r_l), r3d(misc), w['gla_wa'], w['gla_ba'],
                             w['gla_ng'], s0)
    o_b = jnp.pad(o_b.reshape(bsz, _GV), ((0, mp - bsz), (0, 0))).astype(BF16)
    kvh = (bsz, 1, NSA_KV_HEADS, NSA_HEAD_DIM)
    rows = [a[:bsz].reshape(kvh) for a in (r0, r1, r2, r3)]
    new_wins = [jnp.concatenate([c[:, 1:], a[:bsz].reshape(kvh)], axis=1) for c, a in zip(wins, (r4, r5))]
    return (o_a, o_b, x2), rows, new_wins, s_new


def kernel(x_prompt, x_sample, cache_cmp_k, cache_cmp_v, cache_slc_k, cache_slc_v, cache_win_k, cache_win_v, state_gla, page_table, norm1_g, w_in, b_nsa_gate, cmp_pe_k, cmp_pe_v, cmp_k_w1, cmp_k_w2, cmp_v_w1, cmp_v_w2, gla_w_a2, gla_b_a, gla_norm_g, w_br_a, w_br_b, w_o, norm2_g, w_up, w_down, norm_f):
    assert DEPTH == 1 and norm1_g.shape[0] == 1
    w = _prep_weights(norm1_g[0], w_in[0], b_nsa_gate[0], cmp_pe_k[0], cmp_pe_v[0], cmp_k_w1[0], cmp_k_w2[0],
                      cmp_v_w1[0], cmp_v_w2[0], gla_w_a2[0], gla_b_a[0], gla_norm_g[0], w_br_a[0], w_br_b[0],
                      w_o[0], norm2_g[0], w_up[0], w_down[0], norm_f)
    mix_p, rows_p, wins_p, s_p = _layer_prompt(x_prompt, w)
    caches = [c[0] for c in (cache_cmp_k, cache_cmp_v, cache_slc_k, cache_slc_v)]
    mix_s, rows_s, wins_s, s_s = _layer_sample(x_sample, caches, [cache_win_k[0], cache_win_v[0]], state_gla[0],
                                               page_table, w)
    y_p, y_s = _channel_mix(mix_p, mix_s, w)
    y_p = y_p.reshape(x_prompt.shape)
    y_s = y_s[:x_sample.shape[0]].reshape(x_sample.shape)
    outs_p = [a[None] for a in rows_p + wins_p + [s_p]]
    outs_s = [a[None] for a in rows_s + wins_s + [s_s]]
    return (y_p, y_s, *outs_p, *outs_s)
```

```python
import functools

import jax
import jax.numpy as jnp
from jax import lax
from jax.experimental import pallas as pl
from jax.experimental.pallas import tpu as pltpu

D_MODEL = 2048
DEPTH = 1
PAGE_SIZE = 128
NSA_HEADS = 16
NSA_KV_HEADS = 4
NSA_HPG = NSA_HEADS // NSA_KV_HEADS
NSA_HEAD_DIM = 64
NSA_Q = NSA_HEADS * NSA_HEAD_DIM
NSA_KV = NSA_KV_HEADS * NSA_HEAD_DIM
CMP_STRIDE = 16
CMP_BLOCK = 32
CMP_HIDDEN = 128
SLC_BLOCK = 64
N_SELECT = 16
WINDOW = 512
GLA_HEADS = 4
GLA_DK = (D_MODEL // 4) // GLA_HEADS
GLA_DV = (D_MODEL // 2) // GLA_HEADS
GLA_RANK = 16
GLA_TAU = 16.0
EPS = 1e-6
NEG = -1e30
FORCE = 1e4
SPLITS = (NSA_Q, 6 * NSA_KV, 3 * NSA_HEADS,
          GLA_HEADS * GLA_DK, GLA_HEADS * GLA_DK, GLA_HEADS * GLA_DV, GLA_HEADS * GLA_DV,
          GLA_RANK, 2 * D_MODEL)

F32 = jnp.float32
BF16 = jnp.bfloat16
LANES = 128
VMEM_LIMIT_BYTES = 56 * 1024 * 1024
MLP_VMEM_LIMIT_BYTES = 60 * 1024 * 1024
N_GATE = 3 * NSA_HEADS
GLA_CHUNK = 32
GLA_UNROLL = 8
GLA_GROUP = 4
NSA_TQ = 256
NSA_TK = 256
VT_ROWS = NSA_HEAD_DIM + 16
CMP_PAGES = 16
PROJ_TM = 512
MERGE_TM = 256
MLP_TM = 1024
MLP_TF = 512
SDS = jax.ShapeDtypeStruct


def _params(*sem, vmem_limit_bytes=VMEM_LIMIT_BYTES):
    return pltpu.CompilerParams(dimension_semantics=sem, vmem_limit_bytes=vmem_limit_bytes)


def _resident(shape):
    nd = len(shape)
    return pl.BlockSpec(shape, lambda *_: (0,) * nd, pipeline_mode=pl.Buffered(1))


def _row_window(row0, n_rows, width):
    return pl.BlockSpec((pl.Element(n_rows), pl.Element(width)), lambda *_: (row0, 0),
                        pipeline_mode=pl.Buffered(1))


def _rms(x, g):
    return x * lax.rsqrt(jnp.mean(x * x, axis=-1, keepdims=True) + EPS) * g


_NT = (((1,), (1,)), ((), ()))


def _proj_kernel(x_ref, g_ref, w_ref, wm_ref, b_ref, *out_refs, spec):
    h = _rms(x_ref[...], g_ref[...]).astype(BF16)
    misc = None
    products = {}
    for o_ref, (kind, c0, c1, scale) in zip(out_refs, spec):
        if kind == 'gates':
            for g in range(NSA_KV_HEADS):
                o_ref[g] = misc if g == 0 else pltpu.roll(misc, LANES - g * 3 * NSA_HPG, axis=1)
            continue
        if (kind == 'misc', c0, c1) not in products:
            rows = wm_ref[...] if kind == 'misc' else w_ref[c0:c1, :]
            products[(kind == 'misc', c0, c1)] = lax.dot_general(h, rows, _NT, preferred_element_type=F32)
        r = products[(kind == 'misc', c0, c1)]
        if scale != 1.0:
            r = r * scale
        if kind == 'f32':
            o_ref[...] = r
        elif kind == 'f32t':
            o_ref[0] = r.T
        elif kind == 'bf16':
            o_ref[...] = r.astype(BF16)
        elif kind == 'sigmoid':
            o_ref[...] = jax.nn.sigmoid(r)
        elif kind == 'hm':
            for i in range((c1 - c0) // NSA_HEAD_DIM):
                o_ref[i] = r[:, i * NSA_HEAD_DIM:(i + 1) * NSA_HEAD_DIM].astype(BF16)
        elif kind == 'hmt':
            rt = r.T
            for i in range((c1 - c0) // NSA_HEAD_DIM):
                o_ref[i] = rt[i * NSA_HEAD_DIM:(i + 1) * NSA_HEAD_DIM].astype(BF16)
        elif kind == 'hmvt':
            rt = r.T
            sub = lax.broadcasted_iota(jnp.int32, (VT_ROWS - NSA_HEAD_DIM, r.shape[0]), 0)
            ones = jnp.where(sub == 0, 1.0, 0.0)
            for i in range((c1 - c0) // NSA_HEAD_DIM):
                piece = rt[i * NSA_HEAD_DIM:(i + 1) * NSA_HEAD_DIM]
                o_ref[i] = jnp.concatenate([piece, ones], axis=0).astype(BF16)
        elif kind == 'misc':
            lane = lax.broadcasted_iota(jnp.int32, r.shape, 1)
            misc = jnp.where(lane < N_GATE, jax.nn.sigmoid(r + b_ref[...]), r)
            o_ref[...] = misc


def _proj(x, norm_g, w_t, row0, n_rows, w_misc_t, bias, spec, tm, seq_len=None):
    m, d = x.shape
    assert m % tm == 0 and w_t.shape[1] == d
    out_shape, out_specs = [], []
    for kind, c0, c1, _ in spec:
        if kind == 'f32t':
            assert seq_len % tm == 0
            per_seq = seq_len // tm
            out_shape.append(SDS((m // seq_len, c1 - c0, seq_len), F32))
            out_specs.append(pl.BlockSpec((1, c1 - c0, tm), lambda i: (i // per_seq, 0, i % per_seq)))
        elif kind == 'hm':
            nh = (c1 - c0) // NSA_HEAD_DIM
            out_shape.append(SDS((nh, m, NSA_HEAD_DIM), BF16))
            out_specs.append(pl.BlockSpec((nh, tm, NSA_HEAD_DIM), lambda i: (0, i, 0)))
        elif kind in ('hmt', 'hmvt'):
            nh = (c1 - c0) // NSA_HEAD_DIM
            nrow = NSA_HEAD_DIM if kind == 'hmt' else VT_ROWS
            out_shape.append(SDS((nh, nrow, m), BF16))
            out_specs.append(pl.BlockSpec((nh, nrow, tm), lambda i: (0, 0, i)))
        elif kind == 'gates':
            out_shape.append(SDS((NSA_KV_HEADS, m, LANES), F32))
            out_specs.append(pl.BlockSpec((NSA_KV_HEADS, tm, LANES), lambda i: (0, i, 0)))
        else:
            out_shape.append(SDS((m, c1 - c0), BF16 if kind == 'bf16' else F32))
            out_specs.append(pl.BlockSpec((tm, c1 - c0), lambda i: (i, 0)))
    return pl.pallas_call(
        functools.partial(_proj_kernel, spec=tuple(spec)),
        grid=(m // tm,),
        in_specs=[pl.BlockSpec((tm, d), lambda i: (i, 0)), _resident((1, d)), _row_window(row0, n_rows, d),
                  _resident(w_misc_t.shape), _resident((1, LANES))],
        out_specs=out_specs, out_shape=out_shape,
        compiler_params=_params("parallel"),
    )(x, norm_g, w_t, w_misc_t, bias)


def _cmp_kernel(pt_ref, k_hbm, v_hbm, perm_ref, pek_ref, pev_ref, w1k_ref, w1v_ref, w2k_ref, w2v_ref,
                ok_ref, ov_ref, kbuf, vbuf, hk, hv, sem, *, n_pages_step, paged):
    b = pl.program_id(0)
    s = pl.program_id(1)
    ns = pl.num_programs(1)
    t = b * ns + s
    total = pl.num_programs(0) * ns
    rows = n_pages_step * 8
    dh, hid_n = NSA_HEAD_DIM, CMP_HIDDEN

    def copies(tt, slot):
        bb = tt // ns
        ss = tt % ns
        out = []
        for p in range(n_pages_step):
            page = pt_ref[bb, ss * n_pages_step + p]
            if paged:
                src = [hbm.at[page] for hbm in (k_hbm, v_hbm)]
            else:
                tok = pl.ds(pl.multiple_of(page * PAGE_SIZE, PAGE_SIZE), PAGE_SIZE)
                src = [hbm.at[bb, :, tok] for hbm in (k_hbm, v_hbm)]
            out.append(pltpu.make_async_copy(src[0], kbuf.at[slot, p], sem.at[slot, 0]))
            out.append(pltpu.make_async_copy(src[1], vbuf.at[slot, p], sem.at[slot, 1]))
        return out

    slot = t % 2

    @pl.when(t == 0)
    def _():
        for c in copies(t, slot):
            c.start()

    @pl.when(t + 1 < total)
    def _():
        for c in copies(t + 1, 1 - slot):
            c.start()

    for c in copies(t, slot):
        c.wait()

    r0 = pl.multiple_of(s * rows, rows)
    low = lax.broadcasted_iota(jnp.int32, (8, LANES), 1) < dh
    tok_pairs = CMP_STRIDE // 2
    nt = (((1,), (1,)), ((), ()))
    perm = perm_ref[...]
    for buf, w1_ref, h_ref in ((kbuf, w1k_ref, hk), (vbuf, w1v_ref, hv)):
        parts = [[[] for _ in range(tok_pairs)] for _ in range(NSA_KV_HEADS)]
        for p in range(n_pages_step):
            r = lax.dot_general(perm, buf[slot, p].astype(BF16), nt, preferred_element_type=F32)
            for q in range(2):
                for m in range(tok_pairs):
                    x0 = r[2 * m * 8:2 * m * 8 + 8, q * LANES:(q + 1) * LANES]
                    x1 = r[(2 * m + 1) * 8:(2 * m + 1) * 8 + 8, q * LANES:(q + 1) * LANES]
                    parts[2 * q][m].append(jnp.where(low, x0, pltpu.roll(x1, dh, axis=1)))
                    parts[2 * q + 1][m].append(jnp.where(low, pltpu.roll(x0, dh, axis=1), x1))
        for g in range(NSA_KV_HEADS):
            z = jnp.concatenate([jnp.concatenate(parts[g][m], axis=0) for m in range(tok_pairs)],
                                axis=1).astype(BF16)
            h_ref[pl.ds(r0, rows), g * 2 * hid_n:(g + 1) * 2 * hid_n] = jnp.dot(
                z, w1_ref[...], preferred_element_type=F32)

    @pl.when(s == ns - 1)
    def _():
        for h_ref, pe_ref, w1_ref, w2_ref, o_ref in ((hk, pek_ref, w1k_ref, w2k_ref, ok_ref),
                                                     (hv, pev_ref, w1v_ref, w2v_ref, ov_ref)):
            n_chunk = h_ref.shape[0]
            pe = jnp.broadcast_to(pe_ref[...], (2, 16, pe_ref.shape[2])).astype(BF16)
            pe_term = (jnp.dot(pe[0], w1_ref[:, 0:hid_n], preferred_element_type=F32)
                       + jnp.dot(pe[1], w1_ref[:, hid_n:2 * hid_n], preferred_element_type=F32))[0:1]
            hid = []
            for g in range(NSA_KV_HEADS):
                first = h_ref[:, g * 2 * hid_n:g * 2 * hid_n + hid_n]
                last = h_ref[:, g * 2 * hid_n + hid_n:(g + 1) * 2 * hid_n]
                hid.append(first + pltpu.roll(last, n_chunk - 1, axis=0) + pe_term)
            hid = jnp.concatenate(hid, axis=1)
            res = jnp.dot(jax.nn.gelu(hid).astype(BF16), w2_ref[...], preferred_element_type=F32)
            if o_ref is ok_ref:
                for g in range(NSA_KV_HEADS):
                    o_ref[0, g] = res[:, g * NSA_HEAD_DIM:(g + 1) * NSA_HEAD_DIM].astype(BF16)
            else:
                res_t = res.T
                for g in range(NSA_KV_HEADS):
                    o_ref[0, g] = res_t[g * NSA_HEAD_DIM:(g + 1) * NSA_HEAD_DIM].astype(BF16)


def _cmp_weights(pe, w1, w2):
    g = NSA_KV_HEADS
    half = CMP_STRIDE * NSA_HEAD_DIM
    pe_t = pe.reshape(2, 1, half)
    w1ab = jnp.concatenate([w1[:half], w1[half:]], axis=1)
    w2b = jnp.einsum('jd,gh->gjhd', w2, jnp.eye(g, dtype=F32)).reshape(g * CMP_HIDDEN, g * NSA_HEAD_DIM)
    return pe_t, w1ab.astype(BF16), w2b.astype(BF16)


def _compress(page_table, k_src, v_src, cw, paged):
    bsz, n_pages = page_table.shape
    p_step = min(CMP_PAGES, n_pages)
    ns = n_pages // p_step
    n_chunk = n_pages * 8
    page_shape = (NSA_KV, PAGE_SIZE)
    assert NSA_KV == 2 * LANES
    assert k_src.shape[1:] == (page_shape if paged else (NSA_KV, n_pages * PAGE_SIZE))
    half = CMP_STRIDE * NSA_HEAD_DIM
    gh = NSA_KV_HEADS * CMP_HIDDEN
    out_row = jnp.arange(PAGE_SIZE, dtype=jnp.int32)
    src_tok = (out_row % 8) * CMP_STRIDE + out_row // 8
    perm = (src_tok[:, None] == jnp.arange(PAGE_SIZE, dtype=jnp.int32)[None, :]).astype(BF16)
    k_sds =SDS((bsz, NSA_KV_HEADS, n_chunk, NSA_HEAD_DIM), BF16)
    v_sds = SDS((bsz, NSA_KV_HEADS, NSA_HEAD_DIM, n_chunk), BF16)
    k_spec = pl.BlockSpec((1, NSA_KV_HEADS, n_chunk, NSA_HEAD_DIM), lambda b, s, pt: (b, 0, 0, 0))
    v_spec = pl.BlockSpec((1, NSA_KV_HEADS, NSA_HEAD_DIM, n_chunk), lambda b, s, pt: (b, 0, 0, 0))
    grid_spec = pltpu.PrefetchScalarGridSpec(
        num_scalar_prefetch=1, grid=(bsz, ns),
        in_specs=[pl.BlockSpec(memory_space=pl.ANY), pl.BlockSpec(memory_space=pl.ANY),
                  _resident((PAGE_SIZE, PAGE_SIZE)), _resident((2, 1, half)), _resident((2, 1, half)),
                  _resident((half, 2 * CMP_HIDDEN)), _resident((half, 2 * CMP_HIDDEN)),
                  _resident((gh, NSA_KV)), _resident((gh, NSA_KV))],
        out_specs=[k_spec, v_spec],
        scratch_shapes=[pltpu.VMEM((2, p_step) + page_shape, F32), pltpu.VMEM((2, p_step) + page_shape, F32),
                        pltpu.VMEM((n_chunk, 2 * gh), F32), pltpu.VMEM((n_chunk, 2 * gh), F32),
                        pltpu.SemaphoreType.DMA((2, 2))])
    return pl.pallas_call(
        functools.partial(_cmp_kernel, n_pages_step=p_step, paged=paged),
        grid_spec=grid_spec, out_shape=[k_sds, v_sds],
        compiler_params=_params("arbitrary", "arbitrary"),
    )(page_table, k_src, v_src, perm, cw['pe_k'], cw['pe_v'], cw['w1_k'], cw['w1_v'], cw['w2_k'], cw['w2_v'])


def _softmax_rows(s, mask):
    sm = jnp.where(mask, s, NEG)
    m = jnp.max(sm, axis=-1, keepdims=True)
    e = jnp.where(mask, jnp.exp(sm - m), 0.0)
    return e, jnp.sum(e, axis=-1, keepdims=True)


def _safe_inv(l):
    return jnp.where(l > 0.0, 1.0 / jnp.where(l > 0.0, l, 1.0), 0.0)


M_INIT = -1e20
LOG2_E = 1.4426950408889634


def _nsa_prompt_kernel(qt_ref, kc_ref, vct_ref, ks_ref, vst_ref, kw_ref, vwt_ref, gate_ref,
                       o_ref, *scratch, n_cmp, n_slc, n_sel):
    tq, tk, hpg, dh = NSA_TQ, NSA_TK, NSA_HPG, NSA_HEAD_DIM
    rk_refs = scratch[:tq // LANES]
    rest = scratch[tq // LANES:]
    selb_ref = rest[0]
    m_ref, acc_ref, s_refs, p_refs, a_refs = rest[1], rest[2], rest[3:5], rest[5:7], rest[7:9]
    mw_ref, accw_ref, sw_refs, pw_refs, aw_refs = rest[9], rest[10], rest[11:13], rest[13:15], rest[15:17]
    qi = pl.program_id(2)
    q0 = qi * tq
    n_chunk = kc_ref.shape[2]
    r = SLC_BLOCK // CMP_STRIDE
    n_row = n_chunk // r

    ci = lax.broadcasted_iota(jnp.int32, (n_chunk, tq), 0)
    pos_c = q0 + lax.broadcasted_iota(jnp.int32, (n_chunk, tq), 1)
    m_c = (ci < n_cmp) & (ci * CMP_STRIDE + (CMP_BLOCK - 1) <= pos_c)
    kc = kc_ref[0, 0]
    vct = vct_ref[0, 0]
    o_c = []
    psum = None
    for h in range(hpg):
        s = jnp.where(m_c, jnp.dot(kc, qt_ref[0, h], preferred_element_type=F32), NEG)
        e = jnp.where(m_c, jnp.exp2(s - jnp.max(s, axis=0, keepdims=True)), 0.0)
        p = e * _safe_inv(jnp.sum(e, axis=0, keepdims=True))
        o_c.append(jnp.dot(vct, p.astype(BF16), preferred_element_type=F32))
        psum = p if psum is None else psum + p

    imp = pltpu.roll(psum, 1, axis=0) + psum
    for o in range(1, r):
        imp = imp + pltpu.roll(psum, n_chunk - o, axis=0)
    parts = []
    for i, rk_ref in enumerate(rk_refs):
        rk_ref[...] = imp[:, i * LANES:(i + 1) * LANES]
        parts.append(rk_ref[pl.ds(0, n_row, stride=r), :])
    imp_b = jnp.concatenate(parts, axis=1)
    j_io = lax.broadcasted_iota(jnp.int32, (n_row, tq), 0)
    pos_b = q0 + lax.broadcasted_iota(jnp.int32, (n_row, tq), 1)
    valid = (j_io < n_slc) & (j_io * SLC_BLOCK <= pos_b)
    forced = (j_io == 0) | (j_io == pos_b // SLC_BLOCK)
    sc = jnp.where(valid, imp_b + jnp.where(forced, FORCE, 0.0), NEG)
    rank = jnp.zeros((n_row, tq), F32)
    for k in range(n_slc):
        ck = sc[k:k + 1, :]
        beats = (ck > sc) | ((ck == sc) & (j_io > k))
        rank = rank + jnp.where(beats, 1.0, 0.0)
    selb_ref[...] = jnp.where(rank < n_sel, 0.0, NEG)

    def key_minus_query(n_keys):
        return (lax.broadcasted_iota(jnp.int32, (n_keys, tq), 0)
                - lax.broadcasted_iota(jnp.int32, (n_keys, tq), 1))

    def start(m_ref, acc_ref):
        m_ref[...] = jnp.full(m_ref.shape, M_INIT, F32)
        acc_ref[...] = jnp.zeros(acc_ref.shape, F32)

    def scores(kk, bias, s_ref):
        for h in range(hpg):
            for c in range(kk.shape[0] // SLC_BLOCK):
                rs = slice(c * SLC_BLOCK, (c + 1) * SLC_BLOCK)
                s_ref[h, rs, :] = jnp.dot(kk[rs], qt_ref[0, h], preferred_element_type=F32) + bias[rs]

    def softmax(s_ref, p_ref, a_ref, m_ref):
        for h in range(hpg):
            for c in range(tq // LANES):
                cs = slice(c * LANES, (c + 1) * LANES)
                s = s_ref[h, :, cs]
                m_prev = m_ref[h, :, cs]
                m_new = jnp.maximum(m_prev, jnp.max(s, axis=0, keepdims=True))
                p_ref[h, :, cs] = jnp.exp2(s - m_new).astype(BF16)
                a_ref[h, :, cs] = jnp.exp2(m_prev - m_new)
                m_ref[h, :, cs] = m_new

    def weighted_values(vt, p_ref, a_ref, acc_ref):
        for h in range(hpg):
            acc_ref[h] = a_ref[h] * acc_ref[h] + jnp.dot(vt, p_ref[h], preferred_element_type=F32)

    def finish(acc_ref):
        return [acc_ref[h, 0:dh, :] * _safe_inv(acc_ref[h, dh:dh + 1, :]) for h in range(hpg)]

    def window_attention():
        tkw = sw_refs[0].shape[1]
        base = jnp.maximum(q0 - WINDOW, 0)
        dlt = key_minus_query(tkw)
        start(mw_ref, accw_ref)
        tiles = [pl.multiple_of(base + i * tkw, LANES) for i in range(2)]
        for i, k0 in enumerate(tiles):
            d = dlt + (k0 - q0)
            bias = jnp.where((d <= 0) & (d >= -WINDOW), 0.0, NEG)
            scores(kw_ref[0, 0, pl.ds(k0, tkw), :], bias, sw_refs[i])
        for i in range(2):
            softmax(sw_refs[i], pw_refs[i], aw_refs[i], mw_ref)
        for i, k0 in enumerate(tiles):
            weighted_values(vwt_ref[0, :, pl.ds(k0, tkw)], pw_refs[i], aw_refs[i], accw_ref)

    def selected_attention():
        n_tiles = ks_ref.shape[2] // tk
        blocks_per_tile = tk // SLC_BLOCK
        dlt = key_minus_query(tk)
        hi = (q0 + tq - 1) // tk + 1
        start(m_ref, acc_ref)
        p_refs[1][...] = jnp.zeros(p_refs[1].shape, BF16)
        a_refs[1][...] = jnp.ones(a_refs[1].shape, F32)

        def tile_start(kb):
            return pl.multiple_of(jnp.clip(kb, 0, n_tiles - 1) * tk, tk)

        def tile_scores(kb, s_ref):
            first = jnp.minimum(kb, n_tiles - 1) * blocks_per_tile
            rows = [jnp.broadcast_to(selb_ref[pl.ds(first + i, 1), :], (SLC_BLOCK, tq))
                    for i in range(blocks_per_tile)]
            bias = jnp.where(dlt + (kb * tk - q0) <= 0, jnp.concatenate(rows, axis=0), NEG)
            scores(ks_ref[0, 0, pl.ds(tile_start(kb), tk), :], bias, s_ref)

        def tile_values(kb, p_ref, a_ref):
            weighted_values(vst_ref[0, :, pl.ds(tile_start(kb), tk)], p_ref, a_ref, acc_ref)

        tile_scores(0, s_refs[0])

        def body(j, carry):
            i = 2 * j
            tile_scores(i + 1, s_refs[1])
            softmax(s_refs[0], p_refs[0], a_refs[0], m_ref)
            tile_values(i - 1, p_refs[1], a_refs[1])
            tile_scores(i + 2, s_refs[0])
            softmax(s_refs[1], p_refs[1], a_refs[1], m_ref)
            tile_values(i, p_refs[0], a_refs[0])
            return carry

        n_pairs = (hi + 1) // 2
        lax.fori_loop(0, n_pairs, body, 0)
        tile_values(2 * n_pairs - 1, p_refs[1], a_refs[1])
        return finish(acc_ref)

    window_attention()
    o_s = selected_attention()
    o_w = finish(accw_ref)

    gt = gate_ref[0].T
    outs = [gt[h:h + 1] * o_c[h] + gt[hpg + h:hpg + h + 1] * o_s[h] + gt[2 * hpg + h:2 * hpg + h + 1] * o_w[h]
            for h in range(hpg)]
    o_ref[...] = jnp.concatenate(outs, axis=0).T.astype(BF16)


def _nsa_prompt(q_t, ks_hm, vs_t, kw_hm, vw_t, kcmp, vcmp_t, gates_hm, bsz, t):
    g, hpg, dh, tq = NSA_KV_HEADS, NSA_HPG, NSA_HEAD_DIM, NSA_TQ
    r = SLC_BLOCK // CMP_STRIDE
    n_slc = t // SLC_BLOCK
    assert t % NSA_TK == 0 and t % tq == 0 and n_slc * r <= LANES and NSA_TK % SLC_BLOCK == 0
    nq = t // tq
    n_cmp = kcmp.shape[2] - 1
    if kcmp.shape[2] < LANES:
        fill = LANES - kcmp.shape[2]
        kcmp = jnp.pad(kcmp, ((0, 0), (0, 0), (0, fill), (0, 0)))
        vcmp_t = jnp.pad(vcmp_t, ((0, 0), (0, 0), (0, 0), (0, fill)))
    n_chunk = kcmp.shape[2]
    assert n_chunk == LANES
    q4 = q_t.reshape(g, hpg, dh, bsz * t)

    def k_spec():
        return pl.BlockSpec((1, 1, t, dh), lambda b, gg, qi: (gg, b, 0, 0))

    def vt_spec():
        return pl.BlockSpec((1, VT_ROWS, t), lambda b, gg, qi: (gg, 0, b))

    def per_bt(a):
        return a.reshape(g, bsz, t, a.shape[-1])

    def flash_scratch(n_keys):
        return [pltpu.VMEM((hpg, 1, tq), F32), pltpu.VMEM((hpg, VT_ROWS, tq), F32),
                pltpu.VMEM((hpg, n_keys, tq), F32), pltpu.VMEM((hpg, n_keys, tq), F32),
                pltpu.VMEM((hpg, n_keys, tq), BF16), pltpu.VMEM((hpg, n_keys, tq), BF16),
                pltpu.VMEM((hpg, 1, tq), F32), pltpu.VMEM((hpg, 1, tq), F32)]

    tk_win = (WINDOW + tq) // 2
    assert tk_win % LANES == 0 and t >= WINDOW + tq

    return pl.pallas_call(
        functools.partial(_nsa_prompt_kernel, n_cmp=n_cmp, n_slc=n_slc, n_sel=min(N_SELECT, n_slc)),
        grid=(bsz, g, nq),
        in_specs=[pl.BlockSpec((1, hpg, dh, tq), lambda b, gg, qi: (gg, 0, 0, b * nq + qi)),
                  pl.BlockSpec((1, 1, n_chunk, dh), lambda b, gg, qi: (b, gg, 0, 0)),
                  pl.BlockSpec((1, 1, dh, n_chunk), lambda b, gg, qi: (b, gg, 0, 0)),
                  k_spec(), vt_spec(), k_spec(), vt_spec(),
                  pl.BlockSpec((1, tq, LANES), lambda b, gg, qi: (gg, b * nq + qi, 0))],
        out_specs=pl.BlockSpec((tq, hpg * dh), lambda b, gg, qi: (b * nq + qi, gg)),
        out_shape=SDS((bsz * t, NSA_Q), BF16),
        scratch_shapes=([pltpu.VMEM((n_chunk, LANES), F32)] * (tq // LANES)
                        + [pltpu.VMEM((n_chunk // r, tq), F32)]
                        + flash_scratch(NSA_TK) + flash_scratch(tk_win)),
        compiler_params=_params("parallel", "parallel", "arbitrary"),
    )(q4, kcmp, vcmp_t, per_bt(ks_hm), vs_t, per_bt(kw_hm), vw_t, gates_hm)


def _gla_prompt_kernel(q_ref, k_ref, v_ref, r_ref, misc_ref, wa_ref, ba_ref, ng_ref,
                       o_ref, s_ref, qe_ref, qt_ref, kt_ref, kh_ref, d_ref, u_ref, st_ref):
    t = q_ref.shape[0]
    c = GLA_CHUNK
    n = t // c
    dk, dv = GLA_DK, GLA_DV
    lr = misc_ref[:, N_GATE:N_GATE + GLA_RANK].astype(BF16)
    x = jnp.dot(lr, wa_ref[...], preferred_element_type=F32) + ba_ref[...]
    g = jax.nn.log_sigmoid(x) / GLA_TAU
    row = lax.broadcasted_iota(jnp.int32, (t, dk), 0) % c
    b = g
    sh = 1
    while sh < c:
        b = b + jnp.where(row >= sh, pltpu.roll(b, sh, axis=0), 0.0)
        sh *= 2
    b3 = b.reshape(n, c, dk)
    b_last = jnp.broadcast_to(b3[:, c - 1:c, :], (n, c, dk)).reshape(t, dk)
    b_mid = jnp.broadcast_to(b3[:, c // 2 - 1:c // 2, :], (n, c, dk)).reshape(t, dk)
    q = q_ref[...] * (GLA_DK ** -0.5)
    k = k_ref[...]
    qe_ref[...] = (q * jnp.exp(b)).astype(BF16)
    qt_ref[...] = (q * jnp.exp(b - b_mid)).astype(BF16)
    kt_ref[...] = (k * jnp.exp(b_mid - b)).astype(BF16)
    kh_ref[...] = (k * jnp.exp(b_last - b)).astype(BF16)
    d_ref[...] = jnp.exp(b_last)
    tn = (((0,), (0,)), ((), ()))
    nt = (((1,), (1,)), ((), ()))

    def chunk_update(i, carry):
        r0 = pl.multiple_of(i * c, c)
        u_ref[i] = lax.dot_general(v_ref[pl.ds(r0, c), :], kh_ref[pl.ds(r0, c), :], tn,
                                   preferred_element_type=F32)
        return carry

    lax.fori_loop(0, n, chunk_update, 0, unroll=GLA_UNROLL)

    st_ref[...] = jnp.zeros((dv, dk), F32)

    def recur(i, carry):
        st = st_ref[...]
        st_ref[...] = st * d_ref[pl.ds(pl.multiple_of(i * c, c), 1), :] + u_ref[i]
        u_ref[i] = st
        return carry

    lax.fori_loop(0, n, recur, 0)
    s_ref[0, 0] = st_ref[...].T

    grp = GLA_GROUP
    rows = grp * c
    ri =lax.broadcasted_iota(jnp.int32, (rows, rows), 0)
    ci = lax.broadcasted_iota(jnp.int32, (rows, rows), 1)
    keep = (ri // c == ci // c) & (ri >= ci)
    ng = ng_ref[...]

    def group_out(i, carry):
        r0 = pl.multiple_of(i * rows, rows)
        rs = pl.ds(r0, rows)
        a = lax.dot_general(qt_ref[rs, :], kt_ref[rs, :], nt, preferred_element_type=F32)
        a = jnp.where(keep, a, 0.0).astype(BF16)
        o = jnp.dot(a, v_ref[rs, :], preferred_element_type=F32)
        inter = [lax.dot_general(qe_ref[pl.ds(r0 + j * c, c), :], u_ref[i * grp + j].astype(BF16), nt,
                                 preferred_element_type=F32) for j in range(grp)]
        o = o + jnp.concatenate(inter, axis=0)
        o = _rms(o, ng) * jax.nn.silu(r_ref[rs, :])
        o_ref[rs, :] = o.astype(BF16)
        return carry

    lax.fori_loop(0, n // grp, group_out, 0, unroll=2)


def _gla_prompt(q_l, k_l, v_l, r_l, misc, wa, ba, ng, bsz, t):
    h, dk, dv = GLA_HEADS, GLA_DK, GLA_DV
    assert t % (GLA_CHUNK * GLA_GROUP * 2) == 0
    n = t // GLA_CHUNK
    return pl.pallas_call(
        _gla_prompt_kernel,
        grid=(bsz, h),
        in_specs=[pl.BlockSpec((t, dk), lambda b, hh: (b, hh)), pl.BlockSpec((t, dk), lambda b, hh: (b, hh)),
                  pl.BlockSpec((t, dv), lambda b, hh: (b, hh)), pl.BlockSpec((t, dv), lambda b, hh: (b, hh)),
                  pl.BlockSpec((t, LANES), lambda b, hh: (b, 0)),
                  pl.BlockSpec((GLA_RANK, dk), lambda b, hh: (0, hh)), pl.BlockSpec((1, dk), lambda b, hh: (0, hh)),
                  _resident((1, dv))],
        out_specs=[pl.BlockSpec((t, dv), lambda b, hh: (b, hh)),
                   pl.BlockSpec((1, 1, dk, dv), lambda b, hh: (b, hh, 0, 0))],
        out_shape=[SDS((bsz * t, h * dv), BF16), SDS((bsz, h, dk, dv), F32)],
        scratch_shapes=[pltpu.VMEM((t, dk), BF16), pltpu.VMEM((t, dk), BF16), pltpu.VMEM((t, dk), BF16),
                        pltpu.VMEM((t, dk), BF16), pltpu.VMEM((t, dk), F32),
                        pltpu.VMEM((n, dv, dk), F32), pltpu.VMEM((dv, dk), F32)],
        compiler_params=_params("parallel", "parallel"),
    )(q_l, k_l, v_l, r_l, misc, wa, ba, ng)


def _merge_rows(oa, ob, x, g_ref, wg_ref, wa_ref, wb_ref, wo_ref):
    h = _rms(x, g_ref[...]).astype(BF16)
    d = D_MODEL
    ga = jax.nn.sigmoid(lax.dot_general(h, wg_ref[0:d, :], _NT, preferred_element_type=F32))
    u = ga * jnp.dot(oa, wa_ref[...], preferred_element_type=F32)
    gb = jax.nn.sigmoid(lax.dot_general(h, wg_ref[d:2 * d, :], _NT, preferred_element_type=F32))
    u = u + gb * jnp.dot(ob, wb_ref[...], preferred_element_type=F32)
    return x + jnp.dot(u.astype(BF16), wo_ref[...], preferred_element_type=F32)


def _merge_kernel(oa_ref, ob_ref, x_ref, oas_ref, obs_ref, xs_ref, g_ref, wg_ref, wa_ref, wb_ref, wo_ref,
                  x1_ref, x1s_ref):
    weights = (g_ref, wg_ref, wa_ref, wb_ref, wo_ref)
    x1_ref[...] = _merge_rows(oa_ref[...], ob_ref[...], x_ref[...], *weights)

    @pl.when(pl.program_id(0) == 0)
    def _():
        x1s_ref[...] = _merge_rows(oas_ref[...], obs_ref[...], xs_ref[...], *weights)


def _merge(o_a, o_b, x, oa_s, ob_s, x_s, norm_g, w_t, gate_row0, w_a, w_b, w_o, tm):
    m, d = x.shape
    assert m % tm == 0
    row_specs = [pl.BlockSpec((tm, NSA_Q), lambda i: (i, 0)), pl.BlockSpec((tm, GLA_HEADS * GLA_DV), lambda i: (i, 0)),
                 pl.BlockSpec((tm, d), lambda i: (i, 0))]
    return pl.pallas_call(
        _merge_kernel, grid=(m // tm,),
        in_specs=(row_specs + [_resident(a.shape) for a in (oa_s, ob_s, x_s, norm_g)]
                  + [_row_window(gate_row0, 2 * d, d)] + [_resident(a.shape) for a in (w_a, w_b, w_o)]),
        out_specs=[pl.BlockSpec((tm, d), lambda i: (i, 0)), pl.BlockSpec(x_s.shape, lambda i: (0, 0))],
        out_shape=[SDS((m, d), F32), SDS(x_s.shape, F32)],
        compiler_params=_params("arbitrary"),
    )(o_a, o_b, x, oa_s, ob_s, x_s, norm_g, w_t, w_a, w_b, w_o)


def _mlp_kernel(x1_ref, x1s_ref, g2_ref, wu_ref, wd_ref, gf_ref, y_ref, ys_ref, h_ref):
    i = pl.program_id(0)
    j = pl.program_id(1)
    last = pl.num_programs(1) - 1
    tm = x1_ref.shape[0]

    @pl.when(j == 0)
    def _():
        x1 = x1_ref[...]
        h_ref[0:tm, :] = _rms(x1, g2_ref[...]).astype(BF16)
        y_ref[...] = x1

    @pl.when((j == 0) & (i == 0))
    def _():
        x1s = x1s_ref[...]
        h_ref[tm:, :] = _rms(x1s, g2_ref[...]).astype(BF16)
        ys_ref[...] = x1s

    up = jnp.maximum(jnp.dot(h_ref[...], wu_ref[...].astype(BF16), preferred_element_type=F32), 0.0)
    res = jnp.dot((up * up).astype(BF16), wd_ref[...].astype(BF16), preferred_element_type=F32)
    y_ref[...] += res[0:tm]

    @pl.when(i == 0)
    def _():
        ys_ref[...] += res[tm:]

    @pl.when(j == last)
    def _():
        y_ref[...] = _rms(y_ref[...], gf_ref[...])

    @pl.when((j == last) & (i == 0))
    def _():
        ys_ref[...] = _rms(ys_ref[...], gf_ref[...])


def _mlp(x1, x1_s, g2, w_up, w_down, gf, tm, tf):
    m, d = x1.shape
    ff = w_up.shape[1]
    assert m % tm == 0 and ff % tf == 0
    small = pl.BlockSpec(x1_s.shape, lambda i, j: (0, 0))
    return pl.pallas_call(
        _mlp_kernel, grid=(m // tm, ff // tf),
        in_specs=[pl.BlockSpec((tm, d), lambda i, j: (i, 0)), small, _resident((1, d)),
                  pl.BlockSpec((d, tf), lambda i, j: (0, j)), pl.BlockSpec((tf, d), lambda i, j: (j, 0)),
                  _resident((1, d))],
        out_specs=[pl.BlockSpec((tm, d), lambda i, j: (i, 0)), small],
        out_shape=[SDS((m, d), F32), SDS(x1_s.shape, F32)],
        scratch_shapes=[pltpu.VMEM((tm + x1_s.shape[0], d), BF16)],
        compiler_params=_params("arbitrary", "arbitrary", vmem_limit_bytes=MLP_VMEM_LIMIT_BYTES),
    )(x1, x1_s, g2, w_up, w_down, gf)


def _prep_weights(norm1_g, w_in, b_nsa_gate, cmp_pe_k, cmp_pe_v, cmp_k_w1, cmp_k_w2, cmp_v_w1, cmp_v_w2,
                  gla_w_a2, gla_b_a, gla_norm_g, w_br_a, w_br_b, w_o, norm2_g, w_up, w_down, norm_f):
    pts = [0]
    for s in SPLITS:
        pts.append(pts[-1] + s)
    c_q, c_kv, c_g, c_ql, c_kl, c_vl, c_rl, c_lr, c_br, c_end = pts
    gcols = jnp.asarray([c_g + (g * NSA_HPG + h) * 3 + c for g in range(NSA_KV_HEADS)
                         for c in range(3) for h in range(NSA_HPG)], jnp.int32)
    w_t = w_in.T.astype(BF16)
    pad = jnp.zeros((LANES - N_GATE - GLA_RANK, D_MODEL), BF16)
    w_misc_t = jnp.concatenate([w_t[gcols], w_t[c_lr:c_br], pad], axis=0)
    b_misc = jnp.concatenate([b_nsa_gate[gcols - c_g], jnp.zeros((LANES - N_GATE,), F32)])[None, :]
    w = dict(
        norm1=norm1_g[None, :], norm2=norm2_g[None, :], norm_f=norm_f[None, :],
        w_t=w_t, w_misc_t=w_misc_t, row_a=c_q, rows_a=c_g - c_q, row_b=c_ql, rows_b=c_lr - c_ql, row_gate=c_br,
        b_misc=b_misc,
        gla_wa=gla_w_a2.astype(BF16), gla_ba=gla_b_a[None, :], gla_ng=gla_norm_g[None, :],
        w_br_a=w_br_a.astype(BF16), w_br_b=w_br_b.astype(BF16), w_o=w_o.astype(BF16),
        w_up=w_up, w_down=w_down,
    )
    cw = {}
    cw['pe_k'], cw['w1_k'], cw['w2_k'] = _cmp_weights(cmp_pe_k, cmp_k_w1, cmp_k_w2)
    cw['pe_v'], cw['w1_v'], cw['w2_v'] = _cmp_weights(cmp_pe_v, cmp_v_w1, cmp_v_w2)
    w['cmp'] = cw
    return w


_GQ =GLA_HEADS * GLA_DK
_GV = GLA_HEADS * GLA_DV


def _layer_prompt(x, w):
    bsz, t, d = x.shape
    m = bsz * t
    x2 = x.reshape(m, d)
    zb = jnp.zeros((1, LANES), F32)
    spec_a = [('hmt', 0, NSA_Q, NSA_HEAD_DIM ** -0.5 * LOG2_E)]
    for j in range(6):
        cols = (NSA_Q + j * NSA_KV, NSA_Q + (j + 1) * NSA_KV, 1.0)
        spec_a.append(('f32t',) + cols)
        if j >= 2:
            spec_a.append(('hm' if j % 2 == 0 else 'hmvt',) + cols)
    q_t, r0, r1, r2, ks_hm, r3, vs_t, r4, kw_hm, r5, vw_t = _proj(
        x2, w['norm1'], w['w_t'], w['row_a'], w['rows_a'], w['w_misc_t'], zb, spec_a, PROJ_TM, seq_len=t)
    spec_b = [('f32', 0, _GQ, 1.0), ('f32', _GQ, 2 * _GQ, 1.0), ('bf16', 2 * _GQ, 2 * _GQ + _GV, 1.0),
              ('f32', 2 * _GQ + _GV, 2 * _GQ + 2 * _GV, 1.0),
              ('misc', 2 * _GQ + 2 * _GV, 2 * _GQ + 2 * _GV + LANES, 1.0), ('gates', 0, 0, 1.0)]
    q_l, k_l, v_l, r_l, misc, gates_hm = _proj(x2, w['norm1'], w['w_t'], w['row_b'], w['rows_b'], w['w_misc_t'],
                                               w['b_misc'], spec_b, PROJ_TM)

    n_pages = t // PAGE_SIZE
    ident = jnp.broadcast_to(jnp.arange(n_pages, dtype=jnp.int32), (bsz, n_pages))
    kcmp, vcmp = _compress(ident, r0, r1, w['cmp'], paged=False)
    o_a = _nsa_prompt(q_t, ks_hm, vs_t, kw_hm, vw_t, kcmp, vcmp, gates_hm, bsz, t)
    o_b, s_new = _gla_prompt(q_l, k_l, v_l, r_l, misc, w['gla_wa'], w['gla_ba'], w['gla_ng'], bsz, t)
    def token_major(a):
        return a.reshape(bsz, NSA_KV_HEADS, NSA_HEAD_DIM, a.shape[-1]).transpose(0, 3, 1, 2)

    rows = [token_major(a) for a in (r0, r1, r2, r3)]
    n_keep = min(WINDOW, t)
    wins = [token_major(a[:, :, t - n_keep:]) for a in (r4, r5)]
    return (o_a, o_b, x2), rows, wins, s_new


def _channel_mix(prompt, sample, w):
    (oa_p, ob_p, x_p), (oa_s, ob_s, x_s) = prompt, sample
    m = x_p.shape[0]
    x1_p, x1_s = _merge(oa_p, ob_p, x_p, oa_s, ob_s, x_s, w['norm1'], w['w_t'], w['row_gate'], w['w_br_a'],
                        w['w_br_b'], w['w_o'], MERGE_TM)
    tm = next(c for c in (MLP_TM, PROJ_TM, MERGE_TM) if m % c == 0)
    return _mlp(x1_p, x1_s, w['norm2'], w['w_up'], w['w_down'], w['norm_f'], tm, MLP_TF)


def _group_rows(parts):
    rowg = lax.broadcasted_iota(jnp.int32, parts[0].shape, 0) // NSA_HPG
    out = parts[0]
    for g in range(1, NSA_KV_HEADS):
        out = jnp.where(rowg == g, parts[g], out)
    return out


def _sample_select_kernel(q_ref, kc_ref, vc_ref, oc_ref, idx_ref, *, n_cmp, pos, n_pick):
    bsz, nh = q_ref.shape[0], q_ref.shape[1]
    nt = (((1,), (1,)), ((), ()))
    n_chunk = kc_ref.shape[2]
    s = jnp.concatenate(
        [_group_rows([lax.dot_general(q_ref[b], kc_ref[b, g], nt, preferred_element_type=F32)
                      for g in range(NSA_KV_HEADS)]) for b in range(bsz)], axis=0)
    lane = lax.broadcasted_iota(jnp.int32, s.shape, 1)
    mask = (lane < n_cmp) & (lane * CMP_STRIDE + (CMP_BLOCK - 1) <= pos)
    e, l = _softmax_rows(s, mask)
    p = e * _safe_inv(l)
    pb = p.astype(BF16)
    for b in range(bsz):
        oc_ref[b] = _group_rows([lax.dot_general(pb[b * nh:(b + 1) * nh], vc_ref[b, g], nt,
                                                 preferred_element_type=F32) for g in range(NSA_KV_HEADS)])
    nr = p.shape[0]
    y = p + pltpu.roll(p, nr - 1, axis=0)
    psum = y + pltpu.roll(y, nr - 2, axis=0)
    imp = pltpu.roll(psum, 1, axis=1) + psum
    r = SLC_BLOCK // CMP_STRIDE
    for o in range(1, r):
        imp = imp + pltpu.roll(psum, n_chunk - o, axis=1)
    blk = lane // r
    is_blk = lane % r == 0
    valid = blk * SLC_BLOCK <= pos
    forced = (blk == 0) | (blk == pos // SLC_BLOCK)
    score = jnp.where(is_blk & valid, imp + jnp.where(forced, FORCE, 0.0), NEG)
    lane_f = lane.astype(F32)
    out_lane = lax.broadcasted_iota(jnp.int32, (nr, LANES), 1)
    picked = jnp.zeros((nr, LANES), F32)
    for k in range(n_pick):
        mx = jnp.max(score, axis=-1, keepdims=True)
        ix = jnp.min(jnp.where(score == mx, lane_f, float(n_chunk)), axis=-1, keepdims=True)
        picked = jnp.where(out_lane == k, ix, picked)
        score = jnp.where(lane_f == ix, 2.0 * NEG, score)
    idx_ref[...] = (picked.astype(jnp.int32) // r).reshape(idx_ref.shape)


def _sample_select(q_s, kcmp, vcmp, pos, n_pick):
    bsz, g, n_chunk, dh = kcmp.shape
    nh = NSA_HEADS
    return pl.pallas_call(
        functools.partial(_sample_select_kernel, n_cmp=n_chunk - 1, pos=pos, n_pick=n_pick),
        grid=(1,),
        in_specs=[pl.BlockSpec((bsz, nh, dh), lambda i: (0, 0, 0)),
                  pl.BlockSpec(kcmp.shape, lambda i: (0, 0, 0, 0)), pl.BlockSpec(vcmp.shape, lambda i: (0, 0, 0, 0))],
        out_specs=[pl.BlockSpec((bsz, nh, dh), lambda i: (0, 0, 0)), pl.BlockSpec((bsz, nh, LANES), lambda i: (0, 0, 0))],
        out_shape=[SDS((bsz, nh, dh), F32), SDS((bsz, nh, LANES), jnp.int32)],
        compiler_params=_params("arbitrary"),
    )(q_s, kcmp, vcmp)


def _sample_attend_kernel(idx_ref, pt_ref, q_ref, oc_ref, ks_hbm, vs_hbm, kw_ref, vw_ref,
                          nks_ref, nvs_ref, nkw_ref, nvw_ref, gate_ref, o_ref, ksel, vsel, sem, *, n_pick):
    b = pl.program_id(0)
    g_n, dh = NSA_KV_HEADS, NSA_HEAD_DIM
    half = PAGE_SIZE // SLC_BLOCK

    def copies(bb, slot):
        out = []
        for g in range(g_n):
            for r in range(n_pick):
                page = pt_ref[bb, idx_ref[bb * g_n + g, r] // half]
                out.append(pltpu.make_async_copy(ks_hbm.at[page, g], ksel.at[slot, g, r], sem.at[slot, 0]))
                out.append(pltpu.make_async_copy(vs_hbm.at[page, g], vsel.at[slot, g, r], sem.at[slot, 1]))
        return out

    slot = b % 2

    @pl.when(b == 0)
    def _():
        for c in copies(b, slot):
            c.start()

    @pl.when(b + 1 < pl.num_programs(0))
    def _():
        for c in copies(b + 1, 1 - slot):
            c.start()

    q = q_ref[0]
    qf = q.astype(F32)
    nt = (((1,), (1,)), ((), ()))

    def attend(keys_t, vals_t, bias, k_new, v_new):
        s = _group_rows([jnp.dot(q, keys_t[g].astype(BF16), preferred_element_type=F32)
                         + (0.0 if bias is None else bias[g]) for g in range(g_n)])
        s_new = _group_rows([jnp.sum(qf * k_new[:, g * dh:(g + 1) * dh], axis=-1, keepdims=True)
                             for g in range(g_n)])
        m = jnp.maximum(jnp.max(s, axis=-1, keepdims=True), s_new)
        e = jnp.exp(s - m)
        e_new = jnp.exp(s_new - m)
        l = jnp.sum(e, axis=-1, keepdims=True) + e_new
        eb = e.astype(BF16)
        acc = _group_rows([lax.dot_general(eb, vals_t[g].astype(BF16), nt, preferred_element_type=F32)
                           + e_new * v_new[:, g * dh:(g + 1) * dh] for g in range(g_n)])
        return acc / l

    o_w = attend([kw_ref[0, g] for g in range(g_n)], [vw_ref[0, g] for g in range(g_n)], None,
                 nkw_ref[0], nvw_ref[0])
    for c in copies(b, slot):
        c.wait()
    lin = lax.broadcasted_iota(jnp.int32, (1, PAGE_SIZE), 1)
    bias = []
    for g in range(g_n):
        parts = []
        for r in range(n_pick):
            off = (idx_ref[b * g_n + g, r] % half) * SLC_BLOCK
            parts.append(jnp.where((lin >= off) & (lin < off + SLC_BLOCK), 0.0, NEG))
        bias.append(jnp.concatenate(parts, axis=1))

    def tiles(buf, g):
        return jnp.concatenate([buf[slot, g, r] for r in range(n_pick)], axis=1)

    o_s = attend([tiles(ksel, g) for g in range(g_n)], [tiles(vsel, g) for g in range(g_n)], bias,
                 nks_ref[0], nvs_ref[0])

    gt = jnp.broadcast_to(gate_ref[0], (LANES, LANES)).T
    nh = NSA_HEADS
    o_ref[0] = (gt[0:nh, 0:dh] * oc_ref[0] + gt[nh:2 * nh, 0:dh] * o_s + gt[2 * nh:3 * nh, 0:dh] * o_w)


def _sample_attend(idx, page_table, q_s, o_c, slc_k, slc_v, win_k, win_v, new_rows, gates, n_pick):
    bsz = page_table.shape[0]
    nh, dh, g = NSA_HEADS, NSA_HEAD_DIM, NSA_KV_HEADS
    wlen = win_k.shape[-1]
    row_spec = pl.BlockSpec((1, 1, NSA_KV), lambda b, *_: (b, 0, 0))
    win_spec = pl.BlockSpec((1, g, dh, wlen), lambda b, *_: (b, 0, 0, 0))
    head_spec = pl.BlockSpec((1, nh, dh), lambda b, *_: (b, 0, 0))
    any_spec = pl.BlockSpec(memory_space=pl.ANY)
    grid_spec = pltpu.PrefetchScalarGridSpec(
        num_scalar_prefetch=2, grid=(bsz,),
        in_specs=[head_spec, head_spec, any_spec, any_spec, win_spec, win_spec,
                  row_spec, row_spec, row_spec, row_spec, pl.BlockSpec((1, 1, LANES), lambda b, *_: (b, 0, 0))],
        out_specs=head_spec,
        scratch_shapes=[pltpu.VMEM((2, g, n_pick, dh, PAGE_SIZE), F32), pltpu.VMEM((2, g, n_pick, dh, PAGE_SIZE), F32),
                        pltpu.SemaphoreType.DMA((2, 2))])
    return pl.pallas_call(
        functools.partial(_sample_attend_kernel, n_pick=n_pick),
        grid_spec=grid_spec, out_shape=SDS((bsz, nh, dh), F32),
        compiler_params=_params("arbitrary"),
    )(idx, page_table, q_s, o_c, slc_k, slc_v, win_k, win_v, *new_rows, gates)


def _gla_sample_kernel(q_ref, k_ref, v_ref, r_ref, misc_ref, wa_ref, ba_ref, ng_ref, s0_ref, o_ref, s_ref):
    dk, dv = GLA_DK, GLA_DV
    lr = jnp.broadcast_to(misc_ref[0][:, N_GATE:N_GATE + GLA_RANK], (16, GLA_RANK)).astype(BF16)
    x = jnp.dot(lr, wa_ref[...], preferred_element_type=F32)[0:1] + ba_ref[...]
    g_all = jax.nn.log_sigmoid(x) / GLA_TAU

    def col(v):
        t = jnp.broadcast_to(v, (dk, dk)).T
        return jnp.concatenate([t] * (dv // dk), axis=1)

    outs = []
    for h in range(GLA_HEADS):
        g = g_all[:, h * dk:(h + 1) * dk]
        q = q_ref[0][:, h * dk:(h + 1) * dk] * (GLA_DK ** -0.5)
        k = k_ref[0][:, h * dk:(h + 1) * dk]
        v = v_ref[0][:, h * dv:(h + 1) * dv]
        s0 = s0_ref[0, h]
        q_t = q * jnp.exp(g)
        k_t = k * jnp.exp(-g)
        a = jnp.sum(q_t * k_t, axis=-1, keepdims=True)
        o = jnp.sum(col(q_t) * s0, axis=0, keepdims=True) + a * v
        s_ref[0, h] = col(jnp.exp(g)) * s0 + col(k) * v
        outs.append(_rms(o, ng_ref[...]) * jax.nn.silu(r_ref[0][:, h * dv:(h + 1) * dv]))
    o_ref[0] = jnp.concatenate(outs, axis=-1)


def _gla_sample(q_l, k_l, v_l, r_l, misc, wa, ba, ng, s0):
    bsz, h, dk, dv = s0.shape

    def row(n):
        return pl.BlockSpec((1, 1, n), lambda b: (b, 0, 0))

    st_spec = pl.BlockSpec((1, h, dk, dv), lambda b: (b, 0, 0, 0))
    return pl.pallas_call(
        _gla_sample_kernel, grid=(bsz,),
        in_specs=[row(h * dk), row(h * dk), row(h * dv), row(h * dv), row(LANES),
                  _resident(wa.shape), _resident(ba.shape), _resident(ng.shape), st_spec],
        out_specs=[row(h * dv), st_spec],
        out_shape=[SDS((bsz, 1, h * dv), F32), SDS((bsz, h, dk, dv), F32)],
        compiler_params=_params("parallel"),
    )(q_l, k_l, v_l, r_l, misc, wa, ba, ng, s0)


SAMPLE_ROWS = 16


def _layer_sample(x, caches, wins, s0, page_table, w):
    bsz, t, d = x.shape
    n_pages = page_table.shape[1]
    pos = n_pages * PAGE_SIZE
    assert t == 1 and bsz <= SAMPLE_ROWS and pos % SLC_BLOCK == 0 and wins[0].shape[1] == WINDOW
    assert pos // SLC_BLOCK >= N_SELECT
    mp = SAMPLE_ROWS
    x2 = jnp.pad(x.reshape(bsz, d), ((0, mp - bsz), (0, 0)))
    zb = jnp.zeros((1, LANES), F32)
    spec_a = [('bf16', 0, NSA_Q, NSA_HEAD_DIM ** -0.5)]
    spec_a += [('f32', NSA_Q + j * NSA_KV, NSA_Q + (j + 1) * NSA_KV, 1.0) for j in range(6)]
    q_s, r0, r1, r2, r3, r4, r5 = _proj(x2, w['norm1'], w['w_t'], w['row_a'], w['rows_a'], w['w_misc_t'], zb,
                                        spec_a, mp)
    spec_b = [('f32', 0, _GQ, 1.0), ('f32', _GQ, 2 * _GQ, 1.0), ('f32', 2 * _GQ, 2 * _GQ + _GV, 1.0),
              ('f32', 2 * _GQ + _GV, 2 * _GQ + 2 * _GV, 1.0),
              ('misc', 2 * _GQ + 2 * _GV, 2 * _GQ + 2 * _GV + LANES, 1.0)]
    q_l, k_l, v_l, r_l, misc = _proj(x2, w['norm1'], w['w_t'], w['row_b'], w['rows_b'], w['w_misc_t'], w['b_misc'],
                                     spec_b, mp)

    cache_t = [a.transpose(0, 2, 3, 1) for a in caches]
    win_t = [a.transpose(0, 2, 3, 1) for a in wins]
    n_pool = caches[0].shape[0]
    cmp_pages = [a.reshape(n_pool, NSA_KV, PAGE_SIZE) for a in cache_t[:2]]
    kcmp, vcmp = _compress(page_table, cmp_pages[0], cmp_pages[1], w['cmp'], paged=True)
    q_h = q_s.reshape(mp, NSA_HEADS, NSA_HEAD_DIM)
    n_pick = N_SELECT - 1
    o_c, idx = _sample_select(q_h, kcmp, vcmp, pos, n_pick)
    idx = idx[:, ::NSA_HPG, :n_pick].reshape(bsz * NSA_KV_HEADS, n_pick)
    gates = misc[:, :N_GATE].reshape(mp, NSA_KV_HEADS, 3, NSA_HPG).transpose(0, 2, 1, 3).reshape(mp, 1, N_GATE)
    gates = jnp.pad(gates, ((0, 0), (0, 0), (0, LANES - N_GATE)))
    new_rows = [a.reshape(mp, 1, NSA_KV) for a in (r2, r3, r4, r5)]
    o_a = _sample_attend(idx, page_table, q_h, o_c, cache_t[2], cache_t[3], win_t[0], win_t[1], new_rows, gates,
                         n_pick)
    o_a = jnp.pad(o_a.reshape(bsz, NSA_Q), ((0, mp - bsz), (0, 0))).astype(BF16)

    def r3d(a):
        return a.reshape(mp, 1, a.shape[-1])

    o_b, s_new = _gla_sample(r3d(q_l), r3d(k_l), r3d(v_l), r3d(r_l), r3d(misc), w['gla_wa'], w['gla_ba'],
                             w['gla_ng'], s0)
    o_b = jnp.pad(o_b.reshape(bsz, _GV), ((0, mp - bsz), (0, 0))).astype(BF16)
    kvh = (bsz, 1, NSA_KV_HEADS, NSA_HEAD_DIM)
    rows = [a[:bsz].reshape(kvh) for a in (r0, r1, r2, r3)]
    new_wins = [jnp.concatenate([c[:, 1:], a[:bsz].reshape(kvh)], axis=1) for c, a in zip(wins, (r4, r5))]
    return (o_a, o_b, x2), rows, new_wins, s_new


def kernel(x_prompt, x_sample, cache_cmp_k, cache_cmp_v, cache_slc_k, cache_slc_v, cache_win_k, cache_win_v, state_gla, page_table, norm1_g, w_in, b_nsa_gate, cmp_pe_k, cmp_pe_v, cmp_k_w1, cmp_k_w2, cmp_v_w1, cmp_v_w2, gla_w_a2, gla_b_a, gla_norm_g, w_br_a, w_br_b, w_o, norm2_g, w_up, w_down, norm_f):
    assert DEPTH == 1 and norm1_g.shape[0] == 1
    w = _prep_weights(norm1_g[0], w_in[0], b_nsa_gate[0], cmp_pe_k[0], cmp_pe_v[0], cmp_k_w1[0], cmp_k_w2[0],
                      cmp_v_w1[0], cmp_v_w2[0], gla_w_a2[0], gla_b_a[0], gla_norm_g[0], w_br_a[0], w_br_b[0],
                      w_o[0], norm2_g[0], w_up[0], w_down[0], norm_f)
    mix_p, rows_p, wins_p, s_p = _layer_prompt(x_prompt, w)
    caches = [c[0] for c in (cache_cmp_k, cache_cmp_v, cache_slc_k, cache_slc_v)]
    mix_s, rows_s, wins_s, s_s = _layer_sample(x_sample, caches, [cache_win_k[0], cache_win_v[0]], state_gla[0],
                                               page_table, w)
    y_p, y_s = _channel_mix(mix_p, mix_s, w)
    y_p = y_p.reshape(x_prompt.shape)
    y_s = y_s[:x_sample.shape[0]].reshape(x_sample.shape)
    outs_p = [a[None] for a in rows_p + wins_p + [s_p]]
    outs_s = [a[None] for a in rows_s + wins_s + [s_s]]
    return (y_p, y_s, *outs_p, *outs_s)
```

```python
import functools

import jax
import jax.numpy as jnp
from jax import lax
from jax.experimental import pallas as pl
from jax.experimental.pallas import tpu as pltpu

D_MODEL = 2048
DEPTH = 1
PAGE_SIZE = 128
NSA_HEADS = 16
NSA_KV_HEADS = 4
NSA_HPG = NSA_HEADS // NSA_KV_HEADS
NSA_HEAD_DIM = 64
NSA_Q = NSA_HEADS * NSA_HEAD_DIM
NSA_KV = NSA_KV_HEADS * NSA_HEAD_DIM
CMP_STRIDE = 16
CMP_BLOCK = 32
CMP_HIDDEN = 128
SLC_BLOCK = 64
N_SELECT = 16
WINDOW = 512
GLA_HEADS = 4
GLA_DK = (D_MODEL // 4) // GLA_HEADS
GLA_DV = (D_MODEL // 2) // GLA_HEADS
GLA_RANK = 16
GLA_TAU = 16.0
EPS = 1e-6
NEG = -1e30
FORCE = 1e4
SPLITS = (NSA_Q, 6 * NSA_KV, 3 * NSA_HEADS,
          GLA_HEADS * GLA_DK, GLA_HEADS * GLA_DK, GLA_HEADS * GLA_DV, GLA_HEADS * GLA_DV,
          GLA_RANK, 2 * D_MODEL)

F32 = jnp.float32
BF16 = jnp.bfloat16
LANES = 128
VMEM_LIMIT_BYTES = 56 * 1024 * 1024
MLP_VMEM_LIMIT_BYTES = 60 * 1024 * 1024
N_GATE = 3 * NSA_HEADS
GLA_CHUNK = 32
GLA_UNROLL = 8
GLA_GROUP = 4
NSA_TQ = 256
NSA_TK = 256
VT_ROWS = NSA_HEAD_DIM + 16
CMP_PAGES = 16
PROJ_TM = 512
MERGE_TM = 256
MLP_TM = 1024
MLP_TF = 512
SDS = jax.ShapeDtypeStruct


def _params(*sem, vmem_limit_bytes=VMEM_LIMIT_BYTES):
    return pltpu.CompilerParams(dimension_semantics=sem, vmem_limit_bytes=vmem_limit_bytes)


def _resident(shape):
    nd = len(shape)
    return pl.BlockSpec(shape, lambda *_: (0,) * nd, pipeline_mode=pl.Buffered(1))


def _row_window(row0, n_rows, width):
    return pl.BlockSpec((pl.Element(n_rows), pl.Element(width)), lambda *_: (row0, 0),
                        pipeline_mode=pl.Buffered(1))


def _rms(x, g):
    return x * lax.rsqrt(jnp.mean(x * x, axis=-1, keepdims=True) + EPS) * g


_NT = (((1,), (1,)), ((), ()))


def _proj_kernel(x_ref, g_ref, w_ref, wm_ref, b_ref, *out_refs, spec):
    h = _rms(x_ref[...], g_ref[...]).astype(BF16)
    misc = None
    products = {}
    for o_ref, (kind, c0, c1, scale) in zip(out_refs, spec):
        if kind == 'gates':
            for g in range(NSA_KV_HEADS):
                o_ref[g] = misc if g == 0 else pltpu.roll(misc, LANES - g * 3 * NSA_HPG, axis=1)
            continue
        if (kind == 'misc', c0, c1) not in products:
            rows = wm_ref[...] if kind == 'misc' else w_ref[c0:c1, :]
            products[(kind == 'misc', c0, c1)] = lax.dot_general(h, rows, _NT, preferred_element_type=F32)
        r = products[(kind == 'misc', c0, c1)]
        if scale != 1.0:
            r = r * scale
        if kind == 'f32':
            o_ref[...] = r
        elif kind == 'f32t':
            o_ref[0] = r.T
        elif kind == 'bf16':
            o_ref[...] = r.astype(BF16)
        elif kind == 'sigmoid':
            o_ref[...] = jax.nn.sigmoid(r)
        elif kind == 'hm':
            for i in range((c1 - c0) // NSA_HEAD_DIM):
                o_ref[i] = r[:, i * NSA_HEAD_DIM:(i + 1) * NSA_HEAD_DIM].astype(BF16)
        elif kind == 'hmt':
            rt = r.T
            for i in range((c1 - c0) // NSA_HEAD_DIM):
                o_ref[i] = rt[i * NSA_HEAD_DIM:(i + 1) * NSA_HEAD_DIM].astype(BF16)
        elif kind == 'hmvt':
            rt = r.T
            sub = lax.broadcasted_iota(jnp.int32, (VT_ROWS - NSA_HEAD_DIM, r.shape[0]), 0)
            ones = jnp.where(sub == 0, 1.0, 0.0)
            for i in range((c1 - c0) // NSA_HEAD_DIM):
                piece = rt[i * NSA_HEAD_DIM:(i + 1) * NSA_HEAD_DIM]
                o_ref[i] = jnp.concatenate([piece, ones], axis=0).astype(BF16)
        elif kind == 'misc':
            lane = lax.broadcasted_iota(jnp.int32, r.shape, 1)
            misc = jnp.where(lane < N_GATE, jax.nn.sigmoid(r + b_ref[...]), r)
            o_ref[...] = misc


def _proj(x, norm_g, w_t, row0, n_rows, w_misc_t, bias, spec, tm, seq_len=None):
    m, d = x.shape
    assert m % tm == 0 and w_t.shape[1] == d
    out_shape, out_specs = [], []
    for kind, c0, c1, _ in spec:
        if kind == 'f32t':
            assert seq_len % tm == 0
            per_seq = seq_len // tm
            out_shape.append(SDS((m // seq_len, c1 - c0, seq_len), F32))
            out_specs.append(pl.BlockSpec((1, c1 - c0, tm), lambda i: (i // per_seq, 0, i % per_seq)))
        elif kind == 'hm':
            nh = (c1 - c0) // NSA_HEAD_DIM
            out_shape.append(SDS((nh, m, NSA_HEAD_DIM), BF16))
            out_specs.append(pl.BlockSpec((nh, tm, NSA_HEAD_DIM), lambda i: (0, i, 0)))
        elif kind in ('hmt', 'hmvt'):
            nh = (c1 - c0) // NSA_HEAD_DIM
            nrow = NSA_HEAD_DIM if kind == 'hmt' else VT_ROWS
            out_shape.append(SDS((nh, nrow, m), BF16))
            out_specs.append(pl.BlockSpec((nh, nrow, tm), lambda i: (0, 0, i)))
        elif kind == 'gates':
            out_shape.append(SDS((NSA_KV_HEADS, m, LANES), F32))
            out_specs.append(pl.BlockSpec((NSA_KV_HEADS, tm, LANES), lambda i: (0, i, 0)))
        else:
            out_shape.append(SDS((m, c1 - c0), BF16 if kind == 'bf16' else F32))
            out_specs.append(pl.BlockSpec((tm, c1 - c0), lambda i: (i, 0)))
    return pl.pallas_call(
        functools.partial(_proj_kernel, spec=tuple(spec)),
        grid=(m // tm,),
        in_specs=[pl.BlockSpec((tm, d), lambda i: (i, 0)), _resident((1, d)), _row_window(row0, n_rows, d),
                  _resident(w_misc_t.shape), _resident((1, LANES))],
        out_specs=out_specs, out_shape=out_shape,
        compiler_params=_params("parallel"),
    )(x, norm_g, w_t, w_misc_t, bias)


def _cmp_kernel(pt_ref, k_hbm, v_hbm, perm_ref, pek_ref, pev_ref, w1k_ref, w1v_ref, w2k_ref, w2v_ref,
                ok_ref, ov_ref, kbuf, vbuf, hk, hv, sem, *, n_pages_step, paged):
    b = pl.program_id(0)
    s = pl.program_id(1)
    ns = pl.num_programs(1)
    t = b * ns + s
    total = pl.num_programs(0) * ns
    rows = n_pages_step * 8
    dh, hid_n = NSA_HEAD_DIM, CMP_HIDDEN

    def copies(tt, slot):
        bb = tt // ns
        ss = tt % ns
        out = []
        for p in range(n_pages_step):
            page = pt_ref[bb, ss * n_pages_step + p]
            if paged:
                src = [hbm.at[page] for hbm in (k_hbm, v_hbm)]
            else:
                tok = pl.ds(pl.multiple_of(page * PAGE_SIZE, PAGE_SIZE), PAGE_SIZE)
                src = [hbm.at[bb, :, tok] for hbm in (k_hbm, v_hbm)]
            out.append(pltpu.make_async_copy(src[0], kbuf.at[slot, p], sem.at[slot, 0]))
            out.append(pltpu.make_async_copy(src[1], vbuf.at[slot, p], sem.at[slot, 1]))
        return out

    slot = t % 2

    @pl.when(t == 0)
    def _():
        for c in copies(t, slot):
            c.start()

    @pl.when(t + 1 < total)
    def _():
        for c in copies(t + 1, 1 - slot):
            c.start()

    for c in copies(t, slot):
        c.wait()

    r0 = pl.multiple_of(s * rows, rows)
    low = lax.broadcasted_iota(jnp.int32, (8, LANES), 1) < dh
    tok_pairs = CMP_STRIDE // 2
    nt = (((1,), (1,)), ((), ()))
    perm = perm_ref[...]
    for buf, w1_ref, h_ref in ((kbuf, w1k_ref, hk), (vbuf, w1v_ref, hv)):
        parts = [[[] for _ in range(tok_pairs)] for _ in range(NSA_KV_HEADS)]
        for p in range(n_pages_step):
            r = lax.dot_general(perm, buf[slot, p].astype(BF16), nt, preferred_element_type=F32)
            for q in range(2):
                for m in range(tok_pairs):
                    x0 = r[2 * m * 8:2 * m * 8 + 8, q * LANES:(q + 1) * LANES]
                    x1 = r[(2 * m + 1) * 8:(2 * m + 1) * 8 + 8, q * LANES:(q + 1) * LANES]
                    parts[2 * q][m].append(jnp.where(low, x0, pltpu.roll(x1, dh, axis=1)))
                    parts[2 * q + 1][m].append(jnp.where(low, pltpu.roll(x0, dh, axis=1), x1))
        for g in range(NSA_KV_HEADS):
            z = jnp.concatenate([jnp.concatenate(parts[g][m], axis=0) for m in range(tok_pairs)],
                                axis=1).astype(BF16)
            h_ref[pl.ds(r0, rows), g * 2 * hid_n:(g + 1) * 2 * hid_n] = jnp.dot(
                z, w1_ref[...], preferred_element_type=F32)

    @pl.when(s == ns - 1)
    def _():
        for h_ref, pe_ref, w1_ref, w2_ref, o_ref in ((hk, pek_ref, w1k_ref, w2k_ref, ok_ref),
                                                     (hv, pev_ref, w1v_ref, w2v_ref, ov_ref)):
            n_chunk = h_ref.shape[0]
            pe = jnp.broadcast_to(pe_ref[...], (2, 16, pe_ref.shape[2])).astype(BF16)
            pe_term = (jnp.dot(pe[0], w1_ref[:, 0:hid_n], preferred_element_type=F32)
                       + jnp.dot(pe[1], w1_ref[:, hid_n:2 * hid_n], preferred_element_type=F32))[0:1]
            hid = []
            for g in range(NSA_KV_HEADS):
                first = h_ref[:, g * 2 * hid_n:g * 2 * hid_n + hid_n]
                last = h_ref[:, g * 2 * hid_n + hid_n:(g + 1) * 2 * hid_n]
                hid.append(first + pltpu.roll(last, n_chunk - 1, axis=0) + pe_term)
            hid = jnp.concatenate(hid, axis=1)
            res = jnp.dot(jax.nn.gelu(hid).astype(BF16), w2_ref[...], preferred_element_type=F32)
            if o_ref is ok_ref:
                for g in range(NSA_KV_HEADS):
                    o_ref[0, g] = res[:, g * NSA_HEAD_DIM:(g + 1) * NSA_HEAD_DIM].astype(BF16)
            else:
                res_t = res.T
                for g in range(NSA_KV_HEADS):
                    o_ref[0, g] = res_t[g * NSA_HEAD_DIM:(g + 1) * NSA_HEAD_DIM].astype(BF16)


def _cmp_weights(pe, w1, w2):
    g = NSA_KV_HEADS
    half = CMP_STRIDE * NSA_HEAD_DIM
    pe_t = pe.reshape(2, 1, half)
    w1ab = jnp.concatenate([w1[:half], w1[half:]], axis=1)
    w2b = jnp.einsum('jd,gh->gjhd', w2, jnp.eye(g, dtype=F32)).reshape(g * CMP_HIDDEN, g * NSA_HEAD_DIM)
    return pe_t, w1ab.astype(BF16), w2b.astype(BF16)


def _compress(page_table, k_src, v_src, cw, paged):
    bsz, n_pages = page_table.shape
    p_step = min(CMP_PAGES, n_pages)
    ns = n_pages // p_step
    n_chunk = n_pages * 8
    page_shape = (NSA_KV, PAGE_SIZE)
    assert NSA_KV == 2 * LANES
    assert k_src.shape[1:] == (page_shape if paged else (NSA_KV, n_pages * PAGE_SIZE))
    half = CMP_STRIDE * NSA_HEAD_DIM
    gh = NSA_KV_HEADS * CMP_HIDDEN
    out_row = jnp.arange(PAGE_SIZE, dtype=jnp.int32)
    src_tok = (out_row % 8) * CMP_STRIDE + out_row // 8
    perm = (src_tok[:, None] == jnp.arange(PAGE_SIZE, dtype=jnp.int32)[None, :]).astype(BF16)
    k_sds =SDS((bsz, NSA_KV_HEADS, n_chunk, NSA_HEAD_DIM), BF16)
    v_sds = SDS((bsz, NSA_KV_HEADS, NSA_HEAD_DIM, n_chunk), BF16)
    k_spec = pl.BlockSpec((1, NSA_KV_HEADS, n_chunk, NSA_HEAD_DIM), lambda b, s, pt: (b, 0, 0, 0))
    v_spec = pl.BlockSpec((1, NSA_KV_HEADS, NSA_HEAD_DIM, n_chunk), lambda b, s, pt: (b, 0, 0, 0))
    grid_spec = pltpu.PrefetchScalarGridSpec(
        num_scalar_prefetch=1, grid=(bsz, ns),
        in_specs=[pl.BlockSpec(memory_space=pl.ANY), pl.BlockSpec(memory_space=pl.ANY),
                  _resident((PAGE_SIZE, PAGE_SIZE)), _resident((2, 1, half)), _resident((2, 1, half)),
                  _resident((half, 2 * CMP_HIDDEN)), _resident((half, 2 * CMP_HIDDEN)),
                  _resident((gh, NSA_KV)), _resident((gh, NSA_KV))],
        out_specs=[k_spec, v_spec],
        scratch_shapes=[pltpu.VMEM((2, p_step) + page_shape, F32), pltpu.VMEM((2, p_step) + page_shape, F32),
                        pltpu.VMEM((n_chunk, 2 * gh), F32), pltpu.VMEM((n_chunk, 2 * gh), F32),
                        pltpu.SemaphoreType.DMA((2, 2))])
    return pl.pallas_call(
        functools.partial(_cmp_kernel, n_pages_step=p_step, paged=paged),
        grid_spec=grid_spec, out_shape=[k_sds, v_sds],
        compiler_params=_params("arbitrary", "arbitrary"),
    )(page_table, k_src, v_src, perm, cw['pe_k'], cw['pe_v'], cw['w1_k'], cw['w1_v'], cw['w2_k'], cw['w2_v'])


def _softmax_rows(s, mask):
    sm = jnp.where(mask, s, NEG)
    m = jnp.max(sm, axis=-1, keepdims=True)
    e = jnp.where(mask, jnp.exp(sm - m), 0.0)
    return e, jnp.sum(e, axis=-1, keepdims=True)


def _safe_inv(l):
    return jnp.where(l > 0.0, 1.0 / jnp.where(l > 0.0, l, 1.0), 0.0)


M_INIT = -1e20
LOG2_E = 1.4426950408889634


def _nsa_prompt_kernel(qt_ref, kc_ref, vct_ref, ks_ref, vst_ref, kw_ref, vwt_ref, gate_ref,
                       o_ref, *scratch, n_cmp, n_slc, n_sel, n_qblocks):
    tq, tk, hpg, dh = NSA_TQ, NSA_TK, NSA_HPG, NSA_HEAD_DIM
    rk_refs = scratch[:tq // LANES]
    rest = scratch[tq // LANES:]
    selb_ref = rest[0]
    m_ref, acc_ref, s_refs, p_refs, a_refs = rest[1], rest[2], rest[3:5], rest[5:7], rest[7:9]
    mw_ref, accw_ref, sw_refs, pw_refs, aw_refs = rest[9], rest[10], rest[11:13], rest[13:15], rest[15:17]
    qi = pl.program_id(2)
    q0 = qi * tq
    n_chunk = kc_ref.shape[2]
    r = SLC_BLOCK // CMP_STRIDE
    n_row = n_chunk // r

    ci = lax.broadcasted_iota(jnp.int32, (n_chunk, tq), 0)
    pos_c = q0 + lax.broadcasted_iota(jnp.int32, (n_chunk, tq), 1)
    m_c = (ci < n_cmp) & (ci * CMP_STRIDE + (CMP_BLOCK - 1) <= pos_c)
    kc = kc_ref[0, 0]
    vct = vct_ref[0, 0]
    o_c = []
    psum = None
    for h in range(hpg):
        s = jnp.where(m_c, jnp.dot(kc, qt_ref[0, h], preferred_element_type=F32), NEG)
        e = jnp.where(m_c, jnp.exp2(s - jnp.max(s, axis=0, keepdims=True)), 0.0)
        p = e * _safe_inv(jnp.sum(e, axis=0, keepdims=True))
        o_c.append(jnp.dot(vct, p.astype(BF16), preferred_element_type=F32))
        psum = p if psum is None else psum + p

    imp = pltpu.roll(psum, 1, axis=0) + psum
    for o in range(1, r):
        imp = imp + pltpu.roll(psum, n_chunk - o, axis=0)
    parts = []
    for i, rk_ref in enumerate(rk_refs):
        rk_ref[...] = imp[:, i * LANES:(i + 1) * LANES]
        parts.append(rk_ref[pl.ds(0, n_row, stride=r), :])
    imp_b = jnp.concatenate(parts, axis=1)
    j_io = lax.broadcasted_iota(jnp.int32, (n_row, tq), 0)
    pos_b = q0 + lax.broadcasted_iota(jnp.int32, (n_row, tq), 1)
    valid = (j_io < n_slc) & (j_io * SLC_BLOCK <= pos_b)
    forced = (j_io == 0) | (j_io == pos_b // SLC_BLOCK)
    sc = jnp.where(valid, imp_b + jnp.where(forced, FORCE, 0.0), NEG)
    rank = jnp.zeros((n_row, tq), F32)
    for k in range(n_slc):
        ck = sc[k:k + 1, :]
        beats = (ck > sc) | ((ck == sc) & (j_io > k))
        rank = rank + jnp.where(beats, 1.0, 0.0)
    selb_ref[...] = jnp.where(rank < n_sel, 0.0, NEG)

    def key_minus_query(n_keys):
        return (lax.broadcasted_iota(jnp.int32, (n_keys, tq), 0)
                - lax.broadcasted_iota(jnp.int32, (n_keys, tq), 1))

    def start(m_ref, acc_ref):
        m_ref[...] = jnp.full(m_ref.shape, M_INIT, F32)
        acc_ref[...] = jnp.zeros(acc_ref.shape, F32)

    def scores(kk, bias, s_ref):
        for h in range(hpg):
            for c in range(kk.shape[0] // SLC_BLOCK):
                rs = slice(c * SLC_BLOCK, (c + 1) * SLC_BLOCK)
                s_ref[h, rs, :] = jnp.dot(kk[rs], qt_ref[0, h], preferred_element_type=F32) + bias[rs]

    def softmax(s_ref, p_ref, a_ref, m_ref):
        for h in range(hpg):
            for c in range(tq // LANES):
                cs = slice(c * LANES, (c + 1) * LANES)
                s = s_ref[h, :, cs]
                m_prev = m_ref[h, :, cs]
                m_new = jnp.maximum(m_prev, jnp.max(s, axis=0, keepdims=True))
                p_ref[h, :, cs] = jnp.exp2(s - m_new).astype(BF16)
                a_ref[h, :, cs] = jnp.exp2(m_prev - m_new)
                m_ref[h, :, cs] = m_new

    def weighted_values(vt, p_ref, a_ref, acc_ref):
        for h in range(hpg):
            acc_ref[h] = a_ref[h] * acc_ref[h] + jnp.dot(vt, p_ref[h], preferred_element_type=F32)

    def finish(acc_ref):
        return [acc_ref[h, 0:dh, :] * _safe_inv(acc_ref[h, dh:dh + 1, :]) for h in range(hpg)]

    def window_attention():
        tkw = sw_refs[0].shape[1]
        base = jnp.maximum(q0 - WINDOW, 0)
        dlt = key_minus_query(tkw)
        start(mw_ref, accw_ref)
        tiles = [pl.multiple_of(base + i * tkw, LANES) for i in range(2)]
        for i, k0 in enumerate(tiles):
            d = dlt + (k0 - q0)
            bias = jnp.where((d <= 0) & (d >= -WINDOW), 0.0, NEG)
            scores(kw_ref[0, 0, pl.ds(k0, tkw), :], bias, sw_refs[i])
        for i in range(2):
            softmax(sw_refs[i], pw_refs[i], aw_refs[i], mw_ref)
        for i, k0 in enumerate(tiles):
            weighted_values(vwt_ref[0, :, pl.ds(k0, tkw)], pw_refs[i], aw_refs[i], accw_ref)

    def selected_attention(n_tiles):
        blocks_per_tile = tk // SLC_BLOCK
        start(m_ref, acc_ref)
        for i in range(n_tiles):
            keys = slice(i * tk, (i + 1) * tk)
            rows = [jnp.broadcast_to(selb_ref[i * blocks_per_tile + j:i * blocks_per_tile + j + 1, :], (SLC_BLOCK, tq))
                    for j in range(blocks_per_tile)]
            bias = jnp.concatenate(rows, axis=0)
            if i == n_tiles - 1:
                bias = jnp.where(key_minus_query(tk) <= 0, bias, NEG)
            scores(ks_ref[0, 0, keys, :], bias, s_refs[i % 2])
            softmax(s_refs[i % 2], p_refs[i % 2], a_refs[i % 2], m_ref)
            weighted_values(vst_ref[0, :, keys], p_refs[i % 2], a_refs[i % 2], acc_ref)

    window_attention()
    lax.switch(qi, [functools.partial(selected_attention, n + 1) for n in range(n_qblocks)])
    o_s = finish(acc_ref)
    o_w = finish(accw_ref)

    gt = gate_ref[0].T
    outs = [gt[h:h + 1] * o_c[h] + gt[hpg + h:hpg + h + 1] * o_s[h] + gt[2 * hpg + h:2 * hpg + h + 1] * o_w[h]
            for h in range(hpg)]
    o_ref[...] = jnp.concatenate(outs, axis=0).T.astype(BF16)


def _nsa_prompt(q_t, ks_hm, vs_t, kw_hm, vw_t, kcmp, vcmp_t, gates_hm, bsz, t):
    g, hpg, dh, tq = NSA_KV_HEADS, NSA_HPG, NSA_HEAD_DIM, NSA_TQ
    r = SLC_BLOCK // CMP_STRIDE
    n_slc = t // SLC_BLOCK
    assert t % tq == 0 and n_slc * r <= LANES and NSA_TK % SLC_BLOCK == 0 and NSA_TK == tq
    nq = t // tq
    n_cmp = kcmp.shape[2] - 1
    if kcmp.shape[2] < LANES:
        fill = LANES - kcmp.shape[2]
        kcmp = jnp.pad(kcmp, ((0, 0), (0, 0), (0, fill), (0, 0)))
        vcmp_t = jnp.pad(vcmp_t, ((0, 0), (0, 0), (0, 0), (0, fill)))
    n_chunk = kcmp.shape[2]
    assert n_chunk == LANES
    q4 = q_t.reshape(g, hpg, dh, bsz * t)

    def k_spec():
        return pl.BlockSpec((1, 1, t, dh), lambda b, gg, qi: (gg, b, 0, 0))

    def vt_spec():
        return pl.BlockSpec((1, VT_ROWS, t), lambda b, gg, qi: (gg, 0, b))

    def per_bt(a):
        return a.reshape(g, bsz, t, a.shape[-1])

    def flash_scratch(n_keys):
        return [pltpu.VMEM((hpg, 1, tq), F32), pltpu.VMEM((hpg, VT_ROWS, tq), F32),
                pltpu.VMEM((hpg, n_keys, tq), F32), pltpu.VMEM((hpg, n_keys, tq), F32),
                pltpu.VMEM((hpg, n_keys, tq), BF16), pltpu.VMEM((hpg, n_keys, tq), BF16),
                pltpu.VMEM((hpg, 1, tq), F32), pltpu.VMEM((hpg, 1, tq), F32)]

    tk_win = (WINDOW + tq) // 2
    assert tk_win % LANES == 0 and t >= WINDOW + tq

    return pl.pallas_call(
        functools.partial(_nsa_prompt_kernel, n_cmp=n_cmp, n_slc=n_slc, n_sel=min(N_SELECT, n_slc), n_qblocks=nq),
        grid=(bsz, g, nq),
        in_specs=[pl.BlockSpec((1, hpg, dh, tq), lambda b, gg, qi: (gg, 0, 0, b * nq + qi)),
                  pl.BlockSpec((1, 1, n_chunk, dh), lambda b, gg, qi: (b, gg, 0, 0)),
                  pl.BlockSpec((1, 1, dh, n_chunk), lambda b, gg, qi: (b, gg, 0, 0)),
                  k_spec(), vt_spec(), k_spec(), vt_spec(),
                  pl.BlockSpec((1, tq, LANES), lambda b, gg, qi: (gg, b * nq + qi, 0))],
        out_specs=pl.BlockSpec((tq, hpg * dh), lambda b, gg, qi: (b * nq + qi, gg)),
        out_shape=SDS((bsz * t, NSA_Q), BF16),
        scratch_shapes=([pltpu.VMEM((n_chunk, LANES), F32)] * (tq // LANES)
                        + [pltpu.VMEM((n_chunk // r, tq), F32)]
                        + flash_scratch(NSA_TK) + flash_scratch(tk_win)),
        compiler_params=_params("parallel", "parallel", "arbitrary"),
    )(q4, kcmp, vcmp_t, per_bt(ks_hm), vs_t, per_bt(kw_hm), vw_t, gates_hm)


def _gla_prompt_kernel(q_ref, k_ref, v_ref, r_ref, misc_ref, wa_ref, ba_ref, ng_ref,
                       o_ref, s_ref, qe_ref, qt_ref, kt_ref, kh_ref, d_ref, u_ref, st_ref):
    t = q_ref.shape[0]
    c = GLA_CHUNK
    n = t // c
    dk, dv = GLA_DK, GLA_DV
    lr = misc_ref[:, N_GATE:N_GATE + GLA_RANK].astype(BF16)
    x = jnp.dot(lr, wa_ref[...], preferred_element_type=F32) + ba_ref[...]
    g = jax.nn.log_sigmoid(x) / GLA_TAU
    row = lax.broadcasted_iota(jnp.int32, (t, dk), 0) % c
    b = g
    sh = 1
    while sh < c:
        b = b + jnp.where(row >= sh, pltpu.roll(b, sh, axis=0), 0.0)
        sh *= 2
    b3 = b.reshape(n, c, dk)
    b_last = jnp.broadcast_to(b3[:, c - 1:c, :], (n, c, dk)).reshape(t, dk)
    b_mid = jnp.broadcast_to(b3[:, c // 2 - 1:c // 2, :], (n, c, dk)).reshape(t, dk)
    q = q_ref[...] * (GLA_DK ** -0.5)
    k = k_ref[...]
    qe_ref[...] = (q * jnp.exp(b)).astype(BF16)
    qt_ref[...] = (q * jnp.exp(b - b_mid)).astype(BF16)
    kt_ref[...] = (k * jnp.exp(b_mid - b)).astype(BF16)
    kh_ref[...] = (k * jnp.exp(b_last - b)).astype(BF16)
    d_ref[...] = jnp.exp(b_last)
    tn = (((0,), (0,)), ((), ()))
    nt = (((1,), (1,)), ((), ()))

    def chunk_update(i, carry):
        r0 = pl.multiple_of(i * c, c)
        u_ref[i] = lax.dot_general(v_ref[pl.ds(r0, c), :], kh_ref[pl.ds(r0, c), :], tn,
                                   preferred_element_type=F32)
        return carry

    lax.fori_loop(0, n, chunk_update, 0, unroll=GLA_UNROLL)

    st_ref[...] = jnp.zeros((dv, dk), F32)

    def recur(i, carry):
        st = st_ref[...]
        st_ref[...] = st * d_ref[pl.ds(pl.multiple_of(i * c, c), 1), :] + u_ref[i]
        u_ref[i] = st
        return carry

    lax.fori_loop(0, n, recur, 0)
    s_ref[0, 0] = st_ref[...].T

    grp = GLA_GROUP
    rows = grp * c
    ri =lax.broadcasted_iota(jnp.int32, (rows, rows), 0)
    ci = lax.broadcasted_iota(jnp.int32, (rows, rows), 1)
    keep = (ri // c == ci // c) & (ri >= ci)
    ng = ng_ref[...]

    def group_out(i, carry):
        r0 = pl.multiple_of(i * rows, rows)
        rs = pl.ds(r0, rows)
        a = lax.dot_general(qt_ref[rs, :], kt_ref[rs, :], nt, preferred_element_type=F32)
        a = jnp.where(keep, a, 0.0).astype(BF16)
        o = jnp.dot(a, v_ref[rs, :], preferred_element_type=F32)
        inter = [lax.dot_general(qe_ref[pl.ds(r0 + j * c, c), :], u_ref[i * grp + j].astype(BF16), nt,
                                 preferred_element_type=F32) for j in range(grp)]
        o = o + jnp.concatenate(inter, axis=0)
        o = _rms(o, ng) * jax.nn.silu(r_ref[rs, :])
        o_ref[rs, :] = o.astype(BF16)
        return carry

    lax.fori_loop(0, n // grp, group_out, 0, unroll=2)


def _gla_prompt(q_l, k_l, v_l, r_l, misc, wa, ba, ng, bsz, t):
    h, dk, dv = GLA_HEADS, GLA_DK, GLA_DV
    assert t % (GLA_CHUNK * GLA_GROUP * 2) == 0
    n = t // GLA_CHUNK
    return pl.pallas_call(
        _gla_prompt_kernel,
        grid=(bsz, h),
        in_specs=[pl.BlockSpec((t, dk), lambda b, hh: (b, hh)), pl.BlockSpec((t, dk), lambda b, hh: (b, hh)),
                  pl.BlockSpec((t, dv), lambda b, hh: (b, hh)), pl.BlockSpec((t, dv), lambda b, hh: (b, hh)),
                  pl.BlockSpec((t, LANES), lambda b, hh: (b, 0)),
                  pl.BlockSpec((GLA_RANK, dk), lambda b, hh: (0, hh)), pl.BlockSpec((1, dk), lambda b, hh: (0, hh)),
                  _resident((1, dv))],
        out_specs=[pl.BlockSpec((t, dv), lambda b, hh: (b, hh)),
                   pl.BlockSpec((1, 1, dk, dv), lambda b, hh: (b, hh, 0, 0))],
        out_shape=[SDS((bsz * t, h * dv), BF16), SDS((bsz, h, dk, dv), F32)],
        scratch_shapes=[pltpu.VMEM((t, dk), BF16), pltpu.VMEM((t, dk), BF16), pltpu.VMEM((t, dk), BF16),
                        pltpu.VMEM((t, dk), BF16), pltpu.VMEM((t, dk), F32),
                        pltpu.VMEM((n, dv, dk), F32), pltpu.VMEM((dv, dk), F32)],
        compiler_params=_params("parallel", "parallel"),
    )(q_l, k_l, v_l, r_l, misc, wa, ba, ng)


def _merge_rows(oa, ob, x, g_ref, wg_ref, wa_ref, wb_ref, wo_ref):
    h = _rms(x, g_ref[...]).astype(BF16)
    d = D_MODEL
    ga = jax.nn.sigmoid(lax.dot_general(h, wg_ref[0:d, :], _NT, preferred_element_type=F32))
    u = ga * jnp.dot(oa, wa_ref[...], preferred_element_type=F32)
    gb = jax.nn.sigmoid(lax.dot_general(h, wg_ref[d:2 * d, :], _NT, preferred_element_type=F32))
    u = u + gb * jnp.dot(ob, wb_ref[...], preferred_element_type=F32)
    return x + jnp.dot(u.astype(BF16), wo_ref[...], preferred_element_type=F32)


def _merge_kernel(oa_ref, ob_ref, x_ref, oas_ref, obs_ref, xs_ref, g_ref, wg_ref, wa_ref, wb_ref, wo_ref,
                  x1_ref, x1s_ref):
    weights = (g_ref, wg_ref, wa_ref, wb_ref, wo_ref)
    x1_ref[...] = _merge_rows(oa_ref[...], ob_ref[...], x_ref[...], *weights)

    @pl.when(pl.program_id(0) == 0)
    def _():
        x1s_ref[...] = _merge_rows(oas_ref[...], obs_ref[...], xs_ref[...], *weights)


def _merge(o_a, o_b, x, oa_s, ob_s, x_s, norm_g, w_t, gate_row0, w_a, w_b, w_o, tm):
    m, d = x.shape
    assert m % tm == 0
    row_specs = [pl.BlockSpec((tm, NSA_Q), lambda i: (i, 0)), pl.BlockSpec((tm, GLA_HEADS * GLA_DV), lambda i: (i, 0)),
                 pl.BlockSpec((tm, d), lambda i: (i, 0))]
    return pl.pallas_call(
        _merge_kernel, grid=(m // tm,),
        in_specs=(row_specs + [_resident(a.shape) for a in (oa_s, ob_s, x_s, norm_g)]
                  + [_row_window(gate_row0, 2 * d, d)] + [_resident(a.shape) for a in (w_a, w_b, w_o)]),
        out_specs=[pl.BlockSpec((tm, d), lambda i: (i, 0)), pl.BlockSpec(x_s.shape, lambda i: (0, 0))],
        out_shape=[SDS((m, d), F32), SDS(x_s.shape, F32)],
        compiler_params=_params("arbitrary"),
    )(o_a, o_b, x, oa_s, ob_s, x_s, norm_g, w_t, w_a, w_b, w_o)


def _mlp_kernel(x1_ref, x1s_ref, g2_ref, wu_ref, wd_ref, gf_ref, y_ref, ys_ref, h_ref):
    i = pl.program_id(0)
    j = pl.program_id(1)
    last = pl.num_programs(1) - 1
    tm = x1_ref.shape[0]

    @pl.when(j == 0)
    def _():
        x1 = x1_ref[...]
        h_ref[0:tm, :] = _rms(x1, g2_ref[...]).astype(BF16)
        y_ref[...] = x1

    @pl.when((j == 0) & (i == 0))
    def _():
        x1s = x1s_ref[...]
        h_ref[tm:, :] = _rms(x1s, g2_ref[...]).astype(BF16)
        ys_ref[...] = x1s

    up = jnp.maximum(jnp.dot(h_ref[...], wu_ref[...].astype(BF16), preferred_element_type=F32), 0.0)
    res = jnp.dot((up * up).astype(BF16), wd_ref[...].astype(BF16), preferred_element_type=F32)
    y_ref[...] += res[0:tm]

    @pl.when(i == 0)
    def _():
        ys_ref[...] += res[tm:]

    @pl.when(j == last)
    def _():
        y_ref[...] = _rms(y_ref[...], gf_ref[...])

    @pl.when((j == last) & (i == 0))
    def _():
        ys_ref[...] = _rms(ys_ref[...], gf_ref[...])


def _mlp(x1, x1_s, g2, w_up, w_down, gf, tm, tf):
    m, d = x1.shape
    ff = w_up.shape[1]
    assert m % tm == 0 and ff % tf == 0
    small = pl.BlockSpec(x1_s.shape, lambda i, j: (0, 0))
    return pl.pallas_call(
        _mlp_kernel, grid=(m // tm, ff // tf),
        in_specs=[pl.BlockSpec((tm, d), lambda i, j: (i, 0)), small, _resident((1, d)),
                  pl.BlockSpec((d, tf), lambda i, j: (0, j)), pl.BlockSpec((tf, d), lambda i, j: (j, 0)),
                  _resident((1, d))],
        out_specs=[pl.BlockSpec((tm, d), lambda i, j: (i, 0)), small],
        out_shape=[SDS((m, d), F32), SDS(x1_s.shape, F32)],
        scratch_shapes=[pltpu.VMEM((tm + x1_s.shape[0], d), BF16)],
        compiler_params=_params("arbitrary", "arbitrary", vmem_limit_bytes=MLP_VMEM_LIMIT_BYTES),
    )(x1, x1_s, g2, w_up, w_down, gf)


def _prep_weights(norm1_g, w_in, b_nsa_gate, cmp_pe_k, cmp_pe_v, cmp_k_w1, cmp_k_w2, cmp_v_w1, cmp_v_w2,
                  gla_w_a2, gla_b_a, gla_norm_g, w_br_a, w_br_b, w_o, norm2_g, w_up, w_down, norm_f):
    pts = [0]
    for s in SPLITS:
        pts.append(pts[-1] + s)
    c_q, c_kv, c_g, c_ql, c_kl, c_vl, c_rl, c_lr, c_br, c_end = pts
    gcols = jnp.asarray([c_g + (g * NSA_HPG + h) * 3 + c for g in range(NSA_KV_HEADS)
                         for c in range(3) for h in range(NSA_HPG)], jnp.int32)
    w_t = w_in.T.astype(BF16)
    pad = jnp.zeros((LANES - N_GATE - GLA_RANK, D_MODEL), BF16)
    w_misc_t = jnp.concatenate([w_t[gcols], w_t[c_lr:c_br], pad], axis=0)
    b_misc = jnp.concatenate([b_nsa_gate[gcols - c_g], jnp.zeros((LANES - N_GATE,), F32)])[None, :]
    w = dict(
        norm1=norm1_g[None, :], norm2=norm2_g[None, :], norm_f=norm_f[None, :],
        w_t=w_t, w_misc_t=w_misc_t, row_a=c_q, rows_a=c_g - c_q, row_b=c_ql, rows_b=c_lr - c_ql, row_gate=c_br,
        b_misc=b_misc,
        gla_wa=gla_w_a2.astype(BF16), gla_ba=gla_b_a[None, :], gla_ng=gla_norm_g[None, :],
        w_br_a=w_br_a.astype(BF16), w_br_b=w_br_b.astype(BF16), w_o=w_o.astype(BF16),
        w_up=w_up, w_down=w_down,
    )
    cw = {}
    cw['pe_k'], cw['w1_k'], cw['w2_k'] = _cmp_weights(cmp_pe_k, cmp_k_w1, cmp_k_w2)
    cw['pe_v'], cw['w1_v'], cw['w2_v'] = _cmp_weights(cmp_pe_v, cmp_v_w1, cmp_v_w2)
    w['cmp'] = cw
    return w


_GQ =GLA_HEADS * GLA_DK
_GV = GLA_HEADS * GLA_DV


def _layer_prompt(x, w):
    bsz, t, d = x.shape
    m = bsz * t
    x2 = x.reshape(m, d)
    zb = jnp.zeros((1, LANES), F32)
    spec_a = [('hmt', 0, NSA_Q, NSA_HEAD_DIM ** -0.5 * LOG2_E)]
    for j in range(6):
        cols = (NSA_Q + j * NSA_KV, NSA_Q + (j + 1) * NSA_KV, 1.0)
        spec_a.append(('f32t',) + cols)
        if j >= 2:
            spec_a.append(('hm' if j % 2 == 0 else 'hmvt',) + cols)
    q_t, r0, r1, r2, ks_hm, r3, vs_t, r4, kw_hm, r5, vw_t = _proj(
        x2, w['norm1'], w['w_t'], w['row_a'], w['rows_a'], w['w_misc_t'], zb, spec_a, PROJ_TM, seq_len=t)
    spec_b = [('f32', 0, _GQ, 1.0), ('f32', _GQ, 2 * _GQ, 1.0), ('bf16', 2 * _GQ, 2 * _GQ + _GV, 1.0),
              ('f32', 2 * _GQ + _GV, 2 * _GQ + 2 * _GV, 1.0),
              ('misc', 2 * _GQ + 2 * _GV, 2 * _GQ + 2 * _GV + LANES, 1.0), ('gates', 0, 0, 1.0)]
    q_l, k_l, v_l, r_l, misc, gates_hm = _proj(x2, w['norm1'], w['w_t'], w['row_b'], w['rows_b'], w['w_misc_t'],
                                               w['b_misc'], spec_b, PROJ_TM)

    n_pages = t // PAGE_SIZE
    ident = jnp.broadcast_to(jnp.arange(n_pages, dtype=jnp.int32), (bsz, n_pages))
    kcmp, vcmp = _compress(ident, r0, r1, w['cmp'], paged=False)
    o_a = _nsa_prompt(q_t, ks_hm, vs_t, kw_hm, vw_t, kcmp, vcmp, gates_hm, bsz, t)
    o_b, s_new = _gla_prompt(q_l, k_l, v_l, r_l, misc, w['gla_wa'], w['gla_ba'], w['gla_ng'], bsz, t)
    def token_major(a):
        return a.reshape(bsz, NSA_KV_HEADS, NSA_HEAD_DIM, a.shape[-1]).transpose(0, 3, 1, 2)

    rows = [token_major(a) for a in (r0, r1, r2, r3)]
    n_keep = min(WINDOW, t)
    wins = [token_major(a[:, :, t - n_keep:]) for a in (r4, r5)]
    return (o_a, o_b, x2), rows, wins, s_new


def _channel_mix(prompt, sample, w):
    (oa_p, ob_p, x_p), (oa_s, ob_s, x_s) = prompt, sample
    m = x_p.shape[0]
    x1_p, x1_s = _merge(oa_p, ob_p, x_p, oa_s, ob_s, x_s, w['norm1'], w['w_t'], w['row_gate'], w['w_br_a'],
                        w['w_br_b'], w['w_o'], MERGE_TM)
    tm = next(c for c in (MLP_TM, PROJ_TM, MERGE_TM) if m % c == 0)
    return _mlp(x1_p, x1_s, w['norm2'], w['w_up'], w['w_down'], w['norm_f'], tm, MLP_TF)


def _group_rows(parts):
    rowg = lax.broadcasted_iota(jnp.int32, parts[0].shape, 0) // NSA_HPG
    out = parts[0]
    for g in range(1, NSA_KV_HEADS):
        out = jnp.where(rowg == g, parts[g], out)
    return out


def _sample_select_kernel(q_ref, kc_ref, vc_ref, oc_ref, idx_ref, *, n_cmp, pos, n_pick):
    bsz, nh = q_ref.shape[0], q_ref.shape[1]
    nt = (((1,), (1,)), ((), ()))
    n_chunk = kc_ref.shape[2]
    s = jnp.concatenate(
        [_group_rows([lax.dot_general(q_ref[b], kc_ref[b, g], nt, preferred_element_type=F32)
                      for g in range(NSA_KV_HEADS)]) for b in range(bsz)], axis=0)
    lane = lax.broadcasted_iota(jnp.int32, s.shape, 1)
    mask = (lane < n_cmp) & (lane * CMP_STRIDE + (CMP_BLOCK - 1) <= pos)
    e, l = _softmax_rows(s, mask)
    p = e * _safe_inv(l)
    pb = p.astype(BF16)
    for b in range(bsz):
        oc_ref[b] = _group_rows([lax.dot_general(pb[b * nh:(b + 1) * nh], vc_ref[b, g], nt,
                                                 preferred_element_type=F32) for g in range(NSA_KV_HEADS)])
    nr = p.shape[0]
    y = p + pltpu.roll(p, nr - 1, axis=0)
    psum = y + pltpu.roll(y, nr - 2, axis=0)
    imp = pltpu.roll(psum, 1, axis=1) + psum
    r = SLC_BLOCK // CMP_STRIDE
    for o in range(1, r):
        imp = imp + pltpu.roll(psum, n_chunk - o, axis=1)
    blk = lane // r
    is_blk = lane % r == 0
    valid = blk * SLC_BLOCK <= pos
    forced = (blk == 0) | (blk == pos // SLC_BLOCK)
    score = jnp.where(is_blk & valid, imp + jnp.where(forced, FORCE, 0.0), NEG)
    lane_f = lane.astype(F32)
    out_lane = lax.broadcasted_iota(jnp.int32, (nr, LANES), 1)
    picked = jnp.zeros((nr, LANES), F32)
    for k in range(n_pick):
        mx = jnp.max(score, axis=-1, keepdims=True)
        ix = jnp.min(jnp.where(score == mx, lane_f, float(n_chunk)), axis=-1, keepdims=True)
        picked = jnp.where(out_lane == k, ix, picked)
        score = jnp.where(lane_f == ix, 2.0 * NEG, score)
    idx_ref[...] = (picked.astype(jnp.int32) // r).reshape(idx_ref.shape)


def _sample_select(q_s, kcmp, vcmp, pos, n_pick):
    bsz, g, n_chunk, dh = kcmp.shape
    nh = NSA_HEADS
    return pl.pallas_call(
        functools.partial(_sample_select_kernel, n_cmp=n_chunk - 1, pos=pos, n_pick=n_pick),
        grid=(1,),
        in_specs=[pl.BlockSpec((bsz, nh, dh), lambda i: (0, 0, 0)),
                  pl.BlockSpec(kcmp.shape, lambda i: (0, 0, 0, 0)), pl.BlockSpec(vcmp.shape, lambda i: (0, 0, 0, 0))],
        out_specs=[pl.BlockSpec((bsz, nh, dh), lambda i: (0, 0, 0)), pl.BlockSpec((bsz, nh, LANES), lambda i: (0, 0, 0))],
        out_shape=[SDS((bsz, nh, dh), F32), SDS((bsz, nh, LANES), jnp.int32)],
        compiler_params=_params("arbitrary"),
    )(q_s, kcmp, vcmp)


def _sample_attend_kernel(idx_ref, pt_ref, q_ref, oc_ref, ks_hbm, vs_hbm, kw_ref, vw_ref,
                          nks_ref, nvs_ref, nkw_ref, nvw_ref, gate_ref, o_ref, ksel, vsel, sem, *, n_pick):
    b = pl.program_id(0)
    g_n, dh = NSA_KV_HEADS, NSA_HEAD_DIM
    half = PAGE_SIZE // SLC_BLOCK

    def copies(bb, slot):
        out = []
        for g in range(g_n):
            for r in range(n_pick):
                page = pt_ref[bb, idx_ref[bb * g_n + g, r] // half]
                out.append(pltpu.make_async_copy(ks_hbm.at[page, g], ksel.at[slot, g, r], sem.at[slot, 0]))
                out.append(pltpu.make_async_copy(vs_hbm.at[page, g], vsel.at[slot, g, r], sem.at[slot, 1]))
        return out

    slot = b % 2

    @pl.when(b == 0)
    def _():
        for c in copies(b, slot):
            c.start()

    @pl.when(b + 1 < pl.num_programs(0))
    def _():
        for c in copies(b + 1, 1 - slot):
            c.start()

    q = q_ref[0]
    qf = q.astype(F32)
    nt = (((1,), (1,)), ((), ()))

    def attend(keys_t, vals_t, bias, k_new, v_new):
        s = _group_rows([jnp.dot(q, keys_t[g].astype(BF16), preferred_element_type=F32)
                         + (0.0 if bias is None else bias[g]) for g in range(g_n)])
        s_new = _group_rows([jnp.sum(qf * k_new[:, g * dh:(g + 1) * dh], axis=-1, keepdims=True)
                             for g in range(g_n)])
        m = jnp.maximum(jnp.max(s, axis=-1, keepdims=True), s_new)
        e = jnp.exp(s - m)
        e_new = jnp.exp(s_new - m)
        l = jnp.sum(e, axis=-1, keepdims=True) + e_new
        eb = e.astype(BF16)
        acc = _group_rows([lax.dot_general(eb, vals_t[g].astype(BF16), nt, preferred_element_type=F32)
                           + e_new * v_new[:, g * dh:(g + 1) * dh] for g in range(g_n)])
        return acc / l

    o_w = attend([kw_ref[0, g] for g in range(g_n)], [vw_ref[0, g] for g in range(g_n)], None,
                 nkw_ref[0], nvw_ref[0])
    for c in copies(b, slot):
        c.wait()
    lin = lax.broadcasted_iota(jnp.int32, (1, PAGE_SIZE), 1)
    bias = []
    for g in range(g_n):
        parts = []
        for r in range(n_pick):
            off = (idx_ref[b * g_n + g, r] % half) * SLC_BLOCK
            parts.append(jnp.where((lin >= off) & (lin < off + SLC_BLOCK), 0.0, NEG))
        bias.append(jnp.concatenate(parts, axis=1))

    def tiles(buf, g):
        return jnp.concatenate([buf[slot, g, r] for r in range(n_pick)], axis=1)

    o_s = attend([tiles(ksel, g) for g in range(g_n)], [tiles(vsel, g) for g in range(g_n)], bias,
                 nks_ref[0], nvs_ref[0])

    gt = jnp.broadcast_to(gate_ref[0], (LANES, LANES)).T
    nh = NSA_HEADS
    o_ref[0] = (gt[0:nh, 0:dh] * oc_ref[0] + gt[nh:2 * nh, 0:dh] * o_s + gt[2 * nh:3 * nh, 0:dh] * o_w)


def _sample_attend(idx, page_table, q_s, o_c, slc_k, slc_v, win_k, win_v, new_rows, gates, n_pick):
    bsz = page_table.shape[0]
    nh, dh, g = NSA_HEADS, NSA_HEAD_DIM, NSA_KV_HEADS
    wlen = win_k.shape[-1]
    row_spec = pl.BlockSpec((1, 1, NSA_KV), lambda b, *_: (b, 0, 0))
    win_spec = pl.BlockSpec((1, g, dh, wlen), lambda b, *_: (b, 0, 0, 0))
    head_spec = pl.BlockSpec((1, nh, dh), lambda b, *_: (b, 0, 0))
    any_spec = pl.BlockSpec(memory_space=pl.ANY)
    grid_spec = pltpu.PrefetchScalarGridSpec(
        num_scalar_prefetch=2, grid=(bsz,),
        in_specs=[head_spec, head_spec, any_spec, any_spec, win_spec, win_spec,
                  row_spec, row_spec, row_spec, row_spec, pl.BlockSpec((1, 1, LANES), lambda b, *_: (b, 0, 0))],
        out_specs=head_spec,
        scratch_shapes=[pltpu.VMEM((2, g, n_pick, dh, PAGE_SIZE), F32), pltpu.VMEM((2, g, n_pick, dh, PAGE_SIZE), F32),
                        pltpu.SemaphoreType.DMA((2, 2))])
    return pl.pallas_call(
        functools.partial(_sample_attend_kernel, n_pick=n_pick),
        grid_spec=grid_spec, out_shape=SDS((bsz, nh, dh), F32),
        compiler_params=_params("arbitrary"),
    )(idx, page_table, q_s, o_c, slc_k, slc_v, win_k, win_v, *new_rows, gates)


def _gla_sample_kernel(q_ref, k_ref, v_ref, r_ref, misc_ref, wa_ref, ba_ref, ng_ref, s0_ref, o_ref, s_ref):
    dk, dv = GLA_DK, GLA_DV
    lr = jnp.broadcast_to(misc_ref[0][:, N_GATE:N_GATE + GLA_RANK], (16, GLA_RANK)).astype(BF16)
    x = jnp.dot(lr, wa_ref[...], preferred_element_type=F32)[0:1] + ba_ref[...]
    g_all = jax.nn.log_sigmoid(x) / GLA_TAU

    def col(v):
        t = jnp.broadcast_to(v, (dk, dk)).T
        return jnp.concatenate([t] * (dv // dk), axis=1)

    outs = []
    for h in range(GLA_HEADS):
        g = g_all[:, h * dk:(h + 1) * dk]
        q = q_ref[0][:, h * dk:(h + 1) * dk] * (GLA_DK ** -0.5)
        k = k_ref[0][:, h * dk:(h + 1) * dk]
        v = v_ref[0][:, h * dv:(h + 1) * dv]
        s0 = s0_ref[0, h]
        q_t = q * jnp.exp(g)
        k_t = k * jnp.exp(-g)
        a = jnp.sum(q_t * k_t, axis=-1, keepdims=True)
        o = jnp.sum(col(q_t) * s0, axis=0, keepdims=True) + a * v
        s_ref[0, h] = col(jnp.exp(g)) * s0 + col(k) * v
        outs.append(_rms(o, ng_ref[...]) * jax.nn.silu(r_ref[0][:, h * dv:(h + 1) * dv]))
    o_ref[0] = jnp.concatenate(outs, axis=-1)


def _gla_sample(q_l, k_l, v_l, r_l, misc, wa, ba, ng, s0):
    bsz, h, dk, dv = s0.shape

    def row(n):
        return pl.BlockSpec((1, 1, n), lambda b: (b, 0, 0))

    st_spec = pl.BlockSpec((1, h, dk, dv), lambda b: (b, 0, 0, 0))
    return pl.pallas_call(
        _gla_sample_kernel, grid=(bsz,),
        in_specs=[row(h * dk), row(h * dk), row(h * dv), row(h * dv), row(LANES),
                  _resident(wa.shape), _resident(ba.shape), _resident(ng.shape), st_spec],
        out_specs=[row(h * dv), st_spec],
        out_shape=[SDS((bsz, 1, h * dv), F32), SDS((bsz, h, dk, dv), F32)],
        compiler_params=_params("parallel"),
    )(q_l, k_l, v_l, r_l, misc, wa, ba, ng, s0)


SAMPLE_ROWS = 16


def _layer_sample(x, caches, wins, s0, page_table, w):
    bsz, t, d = x.shape
    n_pages = page_table.shape[1]
    pos = n_pages * PAGE_SIZE
    assert t == 1 and bsz <= SAMPLE_ROWS and pos % SLC_BLOCK == 0 and wins[0].shape[1] == WINDOW
    assert pos // SLC_BLOCK >= N_SELECT
    mp = SAMPLE_ROWS
    x2 = jnp.pad(x.reshape(bsz, d), ((0, mp - bsz), (0, 0)))
    zb = jnp.zeros((1, LANES), F32)
    spec_a = [('bf16', 0, NSA_Q, NSA_HEAD_DIM ** -0.5)]
    spec_a += [('f32', NSA_Q + j * NSA_KV, NSA_Q + (j + 1) * NSA_KV, 1.0) for j in range(6)]
    q_s, r0, r1, r2, r3, r4, r5 = _proj(x2, w['norm1'], w['w_t'], w['row_a'], w['rows_a'], w['w_misc_t'], zb,
                                        spec_a, mp)
    spec_b = [('f32', 0, _GQ, 1.0), ('f32', _GQ, 2 * _GQ, 1.0), ('f32', 2 * _GQ, 2 * _GQ + _GV, 1.0),
              ('f32', 2 * _GQ + _GV, 2 * _GQ + 2 * _GV, 1.0),
              ('misc', 2 * _GQ + 2 * _GV, 2 * _GQ + 2 * _GV + LANES, 1.0)]
    q_l, k_l, v_l, r_l, misc = _proj(x2, w['norm1'], w['w_t'], w['row_b'], w['rows_b'], w['w_misc_t'], w['b_misc'],
                                     spec_b, mp)

    cache_t = [a.transpose(0, 2, 3, 1) for a in caches]
    win_t = [a.transpose(0, 2, 3, 1) for a in wins]
    n_pool = caches[0].shape[0]
    cmp_pages = [a.reshape(n_pool, NSA_KV, PAGE_SIZE) for a in cache_t[:2]]
    kcmp, vcmp = _compress(page_table, cmp_pages[0], cmp_pages[1], w['cmp'], paged=True)
    q_h = q_s.reshape(mp, NSA_HEADS, NSA_HEAD_DIM)
    n_pick = N_SELECT - 1
    o_c, idx = _sample_select(q_h, kcmp, vcmp, pos, n_pick)
    idx = idx[:, ::NSA_HPG, :n_pick].reshape(bsz * NSA_KV_HEADS, n_pick)
    gates = misc[:, :N_GATE].reshape(mp, NSA_KV_HEADS, 3, NSA_HPG).transpose(0, 2, 1, 3).reshape(mp, 1, N_GATE)
    gates = jnp.pad(gates, ((0, 0), (0, 0), (0, LANES - N_GATE)))
    new_rows = [a.reshape(mp, 1, NSA_KV) for a in (r2, r3, r4, r5)]
    o_a = _sample_attend(idx, page_table, q_h, o_c, cache_t[2], cache_t[3], win_t[0], win_t[1], new_rows, gates,
                         n_pick)
    o_a = jnp.pad(o_a.reshape(bsz, NSA_Q), ((0, mp - bsz), (0, 0))).astype(BF16)

    def r3d(a):
        return a.reshape(mp, 1, a.shape[-1])

    o_b, s_new = _gla_sample(r3d(q_l), r3d(k_l), r3d(v_l), r3d(r_l), r3d(misc), w['gla_wa'], w['gla_ba'],
                             w['gla_ng'], s0)
    o_b = jnp.pad(o_b.reshape(bsz, _GV), ((0, mp - bsz), (0, 0))).astype(BF16)
    kvh = (bsz, 1, NSA_KV_HEADS, NSA_HEAD_DIM)
    rows = [a[:bsz].reshape(kvh) for a in (r0, r1, r2, r3)]
    new_wins = [jnp.concatenate([c[:, 1:], a[:bsz].reshape(kvh)], axis=1) for c, a in zip(wins, (r4, r5))]
    return (o_a, o_b, x2), rows, new_wins, s_new


def kernel(x_prompt, x_sample, cache_cmp_k, cache_cmp_v, cache_slc_k, cache_slc_v, cache_win_k, cache_win_v, state_gla, page_table, norm1_g, w_in, b_nsa_gate, cmp_pe_k, cmp_pe_v, cmp_k_w1, cmp_k_w2, cmp_v_w1, cmp_v_w2, gla_w_a2, gla_b_a, gla_norm_g, w_br_a, w_br_b, w_o, norm2_g, w_up, w_down, norm_f):
    assert DEPTH == 1 and norm1_g.shape[0] == 1
    w = _prep_weights(norm1_g[0], w_in[0], b_nsa_gate[0], cmp_pe_k[0], cmp_pe_v[0], cmp_k_w1[0], cmp_k_w2[0],
                      cmp_v_w1[0], cmp_v_w2[0], gla_w_a2[0], gla_b_a[0], gla_norm_g[0], w_br_a[0], w_br_b[0],
                      w_o[0], norm2_g[0], w_up[0], w_down[0], norm_f)
    mix_p, rows_p, wins_p, s_p = _layer_prompt(x_prompt, w)
    caches = [c[0] for c in (cache_cmp_k, cache_cmp_v, cache_slc_k, cache_slc_v)]
    mix_s, rows_s, wins_s, s_s = _layer_sample(x_sample, caches, [cache_win_k[0], cache_win_v[0]], state_gla[0],
                                               page_table, w)
    y_p, y_s = _channel_mix(mix_p, mix_s, w)
    y_p = y_p.reshape(x_prompt.shape)
    y_s = y_s[:x_sample.shape[0]].reshape(x_sample.shape)
    outs_p = [a[None] for a in rows_p + wins_p + [s_p]]
    outs_s = [a[None] for a in rows_s + wins_s + [s_s]]
    return (y_p, y_s, *outs_p, *outs_s)
```

```python
import functools

import jax
import jax.numpy as jnp
from jax import lax
from jax.experimental import pallas as pl
from jax.experimental.pallas import tpu as pltpu

D_MODEL = 2048
DEPTH = 1
PAGE_SIZE = 128
NSA_HEADS = 16
NSA_KV_HEADS = 4
NSA_HPG = NSA_HEADS // NSA_KV_HEADS
NSA_HEAD_DIM = 64
NSA_Q = NSA_HEADS * NSA_HEAD_DIM
NSA_KV = NSA_KV_HEADS * NSA_HEAD_DIM
CMP_STRIDE = 16
CMP_BLOCK = 32
CMP_HIDDEN = 128
SLC_BLOCK = 64
N_SELECT = 16
WINDOW = 512
GLA_HEADS = 4
GLA_DK = (D_MODEL // 4) // GLA_HEADS
GLA_DV = (D_MODEL // 2) // GLA_HEADS
GLA_RANK = 16
GLA_TAU = 16.0
EPS = 1e-6
NEG = -1e30
FORCE = 1e4
SPLITS = (NSA_Q, 6 * NSA_KV, 3 * NSA_HEADS,
          GLA_HEADS * GLA_DK, GLA_HEADS * GLA_DK, GLA_HEADS * GLA_DV, GLA_HEADS * GLA_DV,
          GLA_RANK, 2 * D_MODEL)

F32 = jnp.float32
BF16 = jnp.bfloat16
LANES = 128
VMEM_LIMIT_BYTES = 56 * 1024 * 1024
MLP_VMEM_LIMIT_BYTES = 60 * 1024 * 1024
N_GATE = 3 * NSA_HEADS
GLA_CHUNK = 32
GLA_UNROLL = 8
GLA_GROUP = 4
NSA_TQ = 256
NSA_TK = 256
VT_ROWS = NSA_HEAD_DIM + 16
CMP_PAGES = 16
PROJ_TM = 512
MERGE_TM = 256
MLP_TM = 1024
MLP_TF = 512
SDS = jax.ShapeDtypeStruct


def _params(*sem, vmem_limit_bytes=VMEM_LIMIT_BYTES):
    return pltpu.CompilerParams(dimension_semantics=sem, vmem_limit_bytes=vmem_limit_bytes)


def _resident(shape):
    nd = len(shape)
    return pl.BlockSpec(shape, lambda *_: (0,) * nd, pipeline_mode=pl.Buffered(1))


def _row_window(row0, n_rows, width):
    return pl.BlockSpec((pl.Element(n_rows), pl.Element(width)), lambda *_: (row0, 0),
                        pipeline_mode=pl.Buffered(1))


def _rms(x, g):
    return x * lax.rsqrt(jnp.mean(x * x, axis=-1, keepdims=True) + EPS) * g


_NT = (((1,), (1,)), ((), ()))


def _proj_kernel(x_ref, g_ref, w_ref, wm_ref, b_ref, *out_refs, spec):
    h = _rms(x_ref[...], g_ref[...]).astype(BF16)
    misc = None
    products = {}
    for o_ref, (kind, c0, c1, scale) in zip(out_refs, spec):
        if kind == 'gates':
            for g in range(NSA_KV_HEADS):
                o_ref[g] = misc if g == 0 else pltpu.roll(misc, LANES - g * 3 * NSA_HPG, axis=1)
            continue
        if (kind == 'misc', c0, c1) not in products:
            rows = wm_ref[...] if kind == 'misc' else w_ref[c0:c1, :]
            products[(kind == 'misc', c0, c1)] = lax.dot_general(h, rows, _NT, preferred_element_type=F32)
        r = products[(kind == 'misc', c0, c1)]
        if scale != 1.0:
            r = r * scale
        if kind == 'f32':
            o_ref[...] = r
        elif kind == 'f32t':
            o_ref[0] = r.T
        elif kind == 'bf16':
            o_ref[...] = r.astype(BF16)
        elif kind == 'sigmoid':
            o_ref[...] = jax.nn.sigmoid(r)
        elif kind == 'hm':
            for i in range((c1 - c0) // NSA_HEAD_DIM):
                o_ref[i] = r[:, i * NSA_HEAD_DIM:(i + 1) * NSA_HEAD_DIM].astype(BF16)
        elif kind == 'hmt':
            rt = r.T
            for i in range((c1 - c0) // NSA_HEAD_DIM):
                o_ref[i] = rt[i * NSA_HEAD_DIM:(i + 1) * NSA_HEAD_DIM].astype(BF16)
        elif kind == 'hmvt':
            rt = r.T
            sub = lax.broadcasted_iota(jnp.int32, (VT_ROWS - NSA_HEAD_DIM, r.shape[0]), 0)
            ones = jnp.where(sub == 0, 1.0, 0.0)
            for i in range((c1 - c0) // NSA_HEAD_DIM):
                piece = rt[i * NSA_HEAD_DIM:(i + 1) * NSA_HEAD_DIM]
                o_ref[i] = jnp.concatenate([piece, ones], axis=0).astype(BF16)
        elif kind == 'misc':
            lane = lax.broadcasted_iota(jnp.int32, r.shape, 1)
            misc = jnp.where(lane < N_GATE, jax.nn.sigmoid(r + b_ref[...]), r)
            o_ref[...] = misc


def _proj(x, norm_g, w_t, row0, n_rows, w_misc_t, bias, spec, tm, seq_len=None):
    m, d = x.shape
    assert m % tm == 0 and w_t.shape[1] == d
    out_shape, out_specs = [], []
    for kind, c0, c1, _ in spec:
        if kind == 'f32t':
            assert seq_len % tm == 0
            per_seq = seq_len // tm
            out_shape.append(SDS((m // seq_len, c1 - c0, seq_len), F32))
            out_specs.append(pl.BlockSpec((1, c1 - c0, tm), lambda i: (i // per_seq, 0, i % per_seq)))
        elif kind == 'hm':
            nh = (c1 - c0) // NSA_HEAD_DIM
            out_shape.append(SDS((nh, m, NSA_HEAD_DIM), BF16))
            out_specs.append(pl.BlockSpec((nh, tm, NSA_HEAD_DIM), lambda i: (0, i, 0)))
        elif kind in ('hmt', 'hmvt'):
            nh = (c1 - c0) // NSA_HEAD_DIM
            nrow = NSA_HEAD_DIM if kind == 'hmt' else VT_ROWS
            out_shape.append(SDS((nh, nrow, m), BF16))
            out_specs.append(pl.BlockSpec((nh, nrow, tm), lambda i: (0, 0, i)))
        elif kind == 'gates':
            out_shape.append(SDS((NSA_KV_HEADS, m, LANES), F32))
            out_specs.append(pl.BlockSpec((NSA_KV_HEADS, tm, LANES), lambda i: (0, i, 0)))
        else:
            out_shape.append(SDS((m, c1 - c0), BF16 if kind == 'bf16' else F32))
            out_specs.append(pl.BlockSpec((tm, c1 - c0), lambda i: (i, 0)))
    return pl.pallas_call(
        functools.partial(_proj_kernel, spec=tuple(spec)),
        grid=(m // tm,),
        in_specs=[pl.BlockSpec((tm, d), lambda i: (i, 0)), _resident((1, d)), _row_window(row0, n_rows, d),
                  _resident(w_misc_t.shape), _resident((1, LANES))],
        out_specs=out_specs, out_shape=out_shape,
        compiler_params=_params("parallel"),
    )(x, norm_g, w_t, w_misc_t, bias)


def _cmp_kernel(pt_ref, k_hbm, v_hbm, perm_ref, pek_ref, pev_ref, w1k_ref, w1v_ref, w2k_ref, w2v_ref,
                ok_ref, ov_ref, kbuf, vbuf, hk, hv, sem, *, n_pages_step, paged):
    b = pl.program_id(0)
    s = pl.program_id(1)
    ns = pl.num_programs(1)
    t = b * ns + s
    total = pl.num_programs(0) * ns
    rows = n_pages_step * 8
    dh, hid_n = NSA_HEAD_DIM, CMP_HIDDEN

    def copies(tt, slot):
        bb = tt // ns
        ss = tt % ns
        out = []
        for p in range(n_pages_step):
            page = pt_ref[bb, ss * n_pages_step + p]
            if paged:
                src = [hbm.at[page] for hbm in (k_hbm, v_hbm)]
            else:
                tok = pl.ds(pl.multiple_of(page * PAGE_SIZE, PAGE_SIZE), PAGE_SIZE)
                src = [hbm.at[bb, :, tok] for hbm in (k_hbm, v_hbm)]
            out.append(pltpu.make_async_copy(src[0], kbuf.at[slot, p], sem.at[slot, 0]))
            out.append(pltpu.make_async_copy(src[1], vbuf.at[slot, p], sem.at[slot, 1]))
        return out

    slot = t % 2

    @pl.when(t == 0)
    def _():
        for c in copies(t, slot):
            c.start()

    @pl.when(t + 1 < total)
    def _():
        for c in copies(t + 1, 1 - slot):
            c.start()

    for c in copies(t, slot):
        c.wait()

    r0 = pl.multiple_of(s * rows, rows)
    low = lax.broadcasted_iota(jnp.int32, (8, LANES), 1) < dh
    tok_pairs = CMP_STRIDE // 2
    nt = (((1,), (1,)), ((), ()))
    perm = perm_ref[...]
    for buf, w1_ref, h_ref in ((kbuf, w1k_ref, hk), (vbuf, w1v_ref, hv)):
        parts = [[[] for _ in range(tok_pairs)] for _ in range(NSA_KV_HEADS)]
        for p in range(n_pages_step):
            r = lax.dot_general(perm, buf[slot, p].astype(BF16), nt, preferred_element_type=F32)
            for q in range(2):
                for m in range(tok_pairs):
                    x0 = r[2 * m * 8:2 * m * 8 + 8, q * LANES:(q + 1) * LANES]
                    x1 = r[(2 * m + 1) * 8:(2 * m + 1) * 8 + 8, q * LANES:(q + 1) * LANES]
                    parts[2 * q][m].append(jnp.where(low, x0, pltpu.roll(x1, dh, axis=1)))
                    parts[2 * q + 1][m].append(jnp.where(low, pltpu.roll(x0, dh, axis=1), x1))
        for g in range(NSA_KV_HEADS):
            z = jnp.concatenate([jnp.concatenate(parts[g][m], axis=0) for m in range(tok_pairs)],
                                axis=1).astype(BF16)
            h_ref[pl.ds(r0, rows), g * 2 * hid_n:(g + 1) * 2 * hid_n] = jnp.dot(
                z, w1_ref[...], preferred_element_type=F32)

    @pl.when(s == ns - 1)
    def _():
        for h_ref, pe_ref, w1_ref, w2_ref, o_ref in ((hk, pek_ref, w1k_ref, w2k_ref, ok_ref),
                                                     (hv, pev_ref, w1v_ref, w2v_ref, ov_ref)):
            n_chunk = h_ref.shape[0]
            pe = jnp.broadcast_to(pe_ref[...], (2, 16, pe_ref.shape[2])).astype(BF16)
            pe_term = (jnp.dot(pe[0], w1_ref[:, 0:hid_n], preferred_element_type=F32)
                       + jnp.dot(pe[1], w1_ref[:, hid_n:2 * hid_n], preferred_element_type=F32))[0:1]
            hid = []
            for g in range(NSA_KV_HEADS):
                first = h_ref[:, g * 2 * hid_n:g * 2 * hid_n + hid_n]
                last = h_ref[:, g * 2 * hid_n + hid_n:(g + 1) * 2 * hid_n]
                hid.append(first + pltpu.roll(last, n_chunk - 1, axis=0) + pe_term)
            hid = jnp.concatenate(hid, axis=1)
            res = jnp.dot(jax.nn.gelu(hid).astype(BF16), w2_ref[...], preferred_element_type=F32)
            if o_ref is ok_ref:
                for g in range(NSA_KV_HEADS):
                    o_ref[0, g] = res[:, g * NSA_HEAD_DIM:(g + 1) * NSA_HEAD_DIM].astype(BF16)
            else:
                res_t = res.T
                for g in range(NSA_KV_HEADS):
                    o_ref[0, g] = res_t[g * NSA_HEAD_DIM:(g + 1) * NSA_HEAD_DIM].astype(BF16)


def _cmp_weights(pe, w1, w2):
    g = NSA_KV_HEADS
    half = CMP_STRIDE * NSA_HEAD_DIM
    pe_t = pe.reshape(2, 1, half)
    w1ab = jnp.concatenate([w1[:half], w1[half:]], axis=1)
    w2b = jnp.einsum('jd,gh->gjhd', w2, jnp.eye(g, dtype=F32)).reshape(g * CMP_HIDDEN, g * NSA_HEAD_DIM)
    return pe_t, w1ab.astype(BF16), w2b.astype(BF16)


def _compress(page_table, k_src, v_src, cw, paged):
    bsz, n_pages = page_table.shape
    p_step = min(CMP_PAGES, n_pages)
    ns = n_pages // p_step
    n_chunk = n_pages * 8
    page_shape = (NSA_KV, PAGE_SIZE)
    assert NSA_KV == 2 * LANES
    assert k_src.shape[1:] == (page_shape if paged else (NSA_KV, n_pages * PAGE_SIZE))
    half = CMP_STRIDE * NSA_HEAD_DIM
    gh = NSA_KV_HEADS * CMP_HIDDEN
    out_row = jnp.arange(PAGE_SIZE, dtype=jnp.int32)
    src_tok = (out_row % 8) * CMP_STRIDE + out_row // 8
    perm = (src_tok[:, None] == jnp.arange(PAGE_SIZE, dtype=jnp.int32)[None, :]).astype(BF16)
    k_sds =SDS((bsz, NSA_KV_HEADS, n_chunk, NSA_HEAD_DIM), BF16)
    v_sds = SDS((bsz, NSA_KV_HEADS, NSA_HEAD_DIM, n_chunk), BF16)
    k_spec = pl.BlockSpec((1, NSA_KV_HEADS, n_chunk, NSA_HEAD_DIM), lambda b, s, pt: (b, 0, 0, 0))
    v_spec = pl.BlockSpec((1, NSA_KV_HEADS, NSA_HEAD_DIM, n_chunk), lambda b, s, pt: (b, 0, 0, 0))
    grid_spec = pltpu.PrefetchScalarGridSpec(
        num_scalar_prefetch=1, grid=(bsz, ns),
        in_specs=[pl.BlockSpec(memory_space=pl.ANY), pl.BlockSpec(memory_space=pl.ANY),
                  _resident((PAGE_SIZE, PAGE_SIZE)), _resident((2, 1, half)), _resident((2, 1, half)),
                  _resident((half, 2 * CMP_HIDDEN)), _resident((half, 2 * CMP_HIDDEN)),
                  _resident((gh, NSA_KV)), _resident((gh, NSA_KV))],
        out_specs=[k_spec, v_spec],
        scratch_shapes=[pltpu.VMEM((2, p_step) + page_shape, F32), pltpu.VMEM((2, p_step) + page_shape, F32),
                        pltpu.VMEM((n_chunk, 2 * gh), F32), pltpu.VMEM((n_chunk, 2 * gh), F32),
                        pltpu.SemaphoreType.DMA((2, 2))])
    return pl.pallas_call(
        functools.partial(_cmp_kernel, n_pages_step=p_step, paged=paged),
        grid_spec=grid_spec, out_shape=[k_sds, v_sds],
        compiler_params=_params("arbitrary", "arbitrary"),
    )(page_table, k_src, v_src, perm, cw['pe_k'], cw['pe_v'], cw['w1_k'], cw['w1_v'], cw['w2_k'], cw['w2_v'])


def _softmax_rows(s, mask):
    sm = jnp.where(mask, s, NEG)
    m = jnp.max(sm, axis=-1, keepdims=True)
    e = jnp.where(mask, jnp.exp(sm - m), 0.0)
    return e, jnp.sum(e, axis=-1, keepdims=True)


def _safe_inv(l):
    return jnp.where(l > 0.0, 1.0 / jnp.where(l > 0.0, l, 1.0), 0.0)


M_INIT = -1e20
LOG2_E = 1.4426950408889634


def _nsa_prompt_kernel(qt_ref, kc_ref, vct_ref, ks_ref, vst_ref, kw_ref, vwt_ref, gate_ref,
                       o_ref, *scratch, n_cmp, n_slc, n_sel, n_qblocks):
    tq, tk, hpg, dh = NSA_TQ, NSA_TK, NSA_HPG, NSA_HEAD_DIM
    rk_refs = scratch[:tq // LANES]
    rest = scratch[tq // LANES:]
    selb_ref = rest[0]
    m_ref, acc_ref, s_refs = rest[1], rest[2], rest[3:5]
    mw_ref, accw_ref, sw_refs = rest[5], rest[6], rest[7:9]
    qi = pl.program_id(2)
    q0 = qi * tq
    n_chunk = kc_ref.shape[2]
    r = SLC_BLOCK // CMP_STRIDE
    n_row = n_chunk // r

    ci = lax.broadcasted_iota(jnp.int32, (n_chunk, tq), 0)
    pos_c = q0 + lax.broadcasted_iota(jnp.int32, (n_chunk, tq), 1)
    m_c = (ci < n_cmp) & (ci * CMP_STRIDE + (CMP_BLOCK - 1) <= pos_c)
    kc = kc_ref[0, 0]
    vct = vct_ref[0, 0]
    o_c = []
    psum = None
    for h in range(hpg):
        s = jnp.where(m_c, jnp.dot(kc, qt_ref[0, h], preferred_element_type=F32), NEG)
        e = jnp.where(m_c, jnp.exp2(s - jnp.max(s, axis=0, keepdims=True)), 0.0)
        p = e * _safe_inv(jnp.sum(e, axis=0, keepdims=True))
        o_c.append(jnp.dot(vct, p.astype(BF16), preferred_element_type=F32))
        psum = p if psum is None else psum + p

    imp = pltpu.roll(psum, 1, axis=0) + psum
    for o in range(1, r):
        imp = imp + pltpu.roll(psum, n_chunk - o, axis=0)
    parts = []
    for i, rk_ref in enumerate(rk_refs):
        rk_ref[...] = imp[:, i * LANES:(i + 1) * LANES]
        parts.append(rk_ref[pl.ds(0, n_row, stride=r), :])
    imp_b = jnp.concatenate(parts, axis=1)
    j_io = lax.broadcasted_iota(jnp.int32, (n_row, tq), 0)
    pos_b = q0 + lax.broadcasted_iota(jnp.int32, (n_row, tq), 1)
    valid = (j_io < n_slc) & (j_io * SLC_BLOCK <= pos_b)
    forced = (j_io == 0) | (j_io == pos_b // SLC_BLOCK)
    sc = jnp.where(valid, imp_b + jnp.where(forced, FORCE, 0.0), NEG)
    rank = jnp.zeros((n_row, tq), F32)
    for k in range(n_slc):
        ck = sc[k:k + 1, :]
        beats = (ck > sc) | ((ck == sc) & (j_io > k))
        rank = rank + jnp.where(beats, 1.0, 0.0)
    selb_ref[...] = jnp.where(rank < n_sel, 0.0, NEG)

    def key_minus_query(n_keys):
        return (lax.broadcasted_iota(jnp.int32, (n_keys, tq), 0)
                - lax.broadcasted_iota(jnp.int32, (n_keys, tq), 1))

    def start(m_ref, acc_ref):
        m_ref[...] = jnp.full(m_ref.shape, M_INIT, F32)
        acc_ref[...] = jnp.zeros(acc_ref.shape, F32)

    def scores(kk, bias, s_ref):
        for h in range(hpg):
            for c in range(kk.shape[0] // SLC_BLOCK):
                rs = slice(c * SLC_BLOCK, (c + 1) * SLC_BLOCK)
                s_ref[h, rs, :] = jnp.dot(kk[rs], qt_ref[0, h], preferred_element_type=F32) + bias[rs]

    def softmax_values(s_ref, vt, m_ref, acc_ref):
        for h in range(hpg):
            ps, alphas = [], []
            for c in range(tq // LANES):
                cs = slice(c * LANES, (c + 1) * LANES)
                s = s_ref[h, :, cs]
                m_prev = m_ref[h, :, cs]
                m_new = jnp.maximum(m_prev, jnp.max(s, axis=0, keepdims=True))
                ps.append(jnp.exp2(s - m_new).astype(BF16))
                alphas.append(jnp.exp2(m_prev - m_new))
                m_ref[h, :, cs] = m_new
            acc_ref[h] = (jnp.concatenate(alphas, axis=1) * acc_ref[h]
                          + jnp.dot(vt, jnp.concatenate(ps, axis=1), preferred_element_type=F32))

    def finish(acc_ref):
        return [acc_ref[h, 0:dh, :] * _safe_inv(acc_ref[h, dh:dh + 1, :]) for h in range(hpg)]

    def window_attention():
        tkw = sw_refs[0].shape[1]
        base = jnp.maximum(q0 - WINDOW, 0)
        dlt = key_minus_query(tkw)
        start(mw_ref, accw_ref)
        tiles = [pl.multiple_of(base + i * tkw, LANES) for i in range(2)]
        for i, k0 in enumerate(tiles):
            d = dlt + (k0 - q0)
            bias = jnp.where((d <= 0) & (d >= -WINDOW), 0.0, NEG)
            scores(kw_ref[0, 0, pl.ds(k0, tkw), :], bias, sw_refs[i])
        for i, k0 in enumerate(tiles):
            softmax_values(sw_refs[i], vwt_ref[0, :, pl.ds(k0, tkw)], mw_ref, accw_ref)

    def selected_attention(n_tiles):
        blocks_per_tile = tk // SLC_BLOCK
        start(m_ref, acc_ref)
        for i in range(n_tiles):
            keys = slice(i * tk, (i + 1) * tk)
            rows = [jnp.broadcast_to(selb_ref[i * blocks_per_tile + j:i * blocks_per_tile + j + 1, :], (SLC_BLOCK, tq))
                    for j in range(blocks_per_tile)]
            bias = jnp.concatenate(rows, axis=0)
            if i == n_tiles - 1:
                bias = jnp.where(key_minus_query(tk) <= 0, bias, NEG)
            scores(ks_ref[0, 0, keys, :], bias, s_refs[i % 2])
            softmax_values(s_refs[i % 2], vst_ref[0, :, keys], m_ref, acc_ref)

    window_attention()
    lax.switch(qi, [functools.partial(selected_attention, n + 1) for n in range(n_qblocks)])
    o_s = finish(acc_ref)
    o_w = finish(accw_ref)

    gt = gate_ref[0].T
    outs = [gt[h:h + 1] * o_c[h] + gt[hpg + h:hpg + h + 1] * o_s[h] + gt[2 * hpg + h:2 * hpg + h + 1] * o_w[h]
            for h in range(hpg)]
    o_ref[...] = jnp.concatenate(outs, axis=0).T.astype(BF16)


def _nsa_prompt(q_t, ks_hm, vs_t, kw_hm, vw_t, kcmp, vcmp_t, gates_hm, bsz, t):
    g, hpg, dh, tq = NSA_KV_HEADS, NSA_HPG, NSA_HEAD_DIM, NSA_TQ
    r = SLC_BLOCK // CMP_STRIDE
    n_slc = t // SLC_BLOCK
    assert t % tq == 0 and n_slc * r <= LANES and NSA_TK % SLC_BLOCK == 0 and NSA_TK == tq
    nq = t // tq
    n_cmp = kcmp.shape[2] - 1
    if kcmp.shape[2] < LANES:
        fill = LANES - kcmp.shape[2]
        kcmp = jnp.pad(kcmp, ((0, 0), (0, 0), (0, fill), (0, 0)))
        vcmp_t = jnp.pad(vcmp_t, ((0, 0), (0, 0), (0, 0), (0, fill)))
    n_chunk = kcmp.shape[2]
    assert n_chunk == LANES
    q4 = q_t.reshape(g, hpg, dh, bsz * t)

    def k_spec():
        return pl.BlockSpec((1, 1, t, dh), lambda b, gg, qi: (gg, b, 0, 0))

    def vt_spec():
        return pl.BlockSpec((1, VT_ROWS, t), lambda b, gg, qi: (gg, 0, b))

    def per_bt(a):
        return a.reshape(g, bsz, t, a.shape[-1])

    def flash_scratch(n_keys):
        return [pltpu.VMEM((hpg, 1, tq), F32), pltpu.VMEM((hpg, VT_ROWS, tq), F32),
                pltpu.VMEM((hpg, n_keys, tq), F32), pltpu.VMEM((hpg, n_keys, tq), F32)]

    tk_win = (WINDOW + tq) // 2
    assert tk_win % LANES == 0 and t >= WINDOW + tq

    return pl.pallas_call(
        functools.partial(_nsa_prompt_kernel, n_cmp=n_cmp, n_slc=n_slc, n_sel=min(N_SELECT, n_slc), n_qblocks=nq),
        grid=(bsz, g, nq),
        in_specs=[pl.BlockSpec((1, hpg, dh, tq), lambda b, gg, qi: (gg, 0, 0, b * nq + qi)),
                  pl.BlockSpec((1, 1, n_chunk, dh), lambda b, gg, qi: (b, gg, 0, 0)),
                  pl.BlockSpec((1, 1, dh, n_chunk), lambda b, gg, qi: (b, gg, 0, 0)),
                  k_spec(), vt_spec(), k_spec(), vt_spec(),
                  pl.BlockSpec((1, tq, LANES), lambda b, gg, qi: (gg, b * nq + qi, 0))],
        out_specs=pl.BlockSpec((tq, hpg * dh), lambda b, gg, qi: (b * nq + qi, gg)),
        out_shape=SDS((bsz * t, NSA_Q), BF16),
        scratch_shapes=([pltpu.VMEM((n_chunk, LANES), F32)] * (tq // LANES)
                        + [pltpu.VMEM((n_chunk // r, tq), F32)]
                        + flash_scratch(NSA_TK) + flash_scratch(tk_win)),
        compiler_params=_params("parallel", "parallel", "arbitrary"),
    )(q4, kcmp, vcmp_t, per_bt(ks_hm), vs_t, per_bt(kw_hm), vw_t, gates_hm)


def _gla_prompt_kernel(q_ref, k_ref, v_ref, r_ref, misc_ref, wa_ref, ba_ref, ng_ref,
                       o_ref, s_ref, qe_ref, qt_ref, kt_ref, kh_ref, d_ref, u_ref):
    t = q_ref.shape[0]
    c = GLA_CHUNK
    n = t // c
    dk, dv = GLA_DK, GLA_DV
    lr = misc_ref[:, N_GATE:N_GATE + GLA_RANK].astype(BF16)
    x = jnp.dot(lr, wa_ref[...], preferred_element_type=F32) + ba_ref[...]
    g = jax.nn.log_sigmoid(x) / GLA_TAU
    row = lax.broadcasted_iota(jnp.int32, (t, dk), 0) % c
    b = g
    sh = 1
    while sh < c:
        b = b + jnp.where(row >= sh, pltpu.roll(b, sh, axis=0), 0.0)
        sh *= 2
    b3 = b.reshape(n, c, dk)
    b_last = jnp.broadcast_to(b3[:, c - 1:c, :], (n, c, dk)).reshape(t, dk)
    b_mid = jnp.broadcast_to(b3[:, c // 2 - 1:c // 2, :], (n, c, dk)).reshape(t, dk)
    q = q_ref[...] * (GLA_DK ** -0.5)
    k = k_ref[...]
    qe_ref[...] = (q * jnp.exp(b)).astype(BF16)
    qt_ref[...] = (q * jnp.exp(b - b_mid)).astype(BF16)
    kt_ref[...] = (k * jnp.exp(b_mid - b)).astype(BF16)
    kh_ref[...] = (k * jnp.exp(b_last - b)).astype(BF16)
    d_ref[...] = jnp.exp(b_last)
    tn = (((0,), (0,)), ((), ()))
    nt = (((1,), (1,)), ((), ()))

    def chunk_update(i, carry):
        r0 = pl.multiple_of(i * c, c)
        u_ref[i] = lax.dot_general(v_ref[pl.ds(r0, c), :], kh_ref[pl.ds(r0, c), :], tn,
                                   preferred_element_type=F32)
        return carry

    lax.fori_loop(0, n, chunk_update, 0, unroll=GLA_UNROLL)

    def recur(i, st):
        nxt = st * d_ref[pl.ds(pl.multiple_of(i * c, c), 1), :] + u_ref[i]
        u_ref[i] = st
        return nxt

    s_ref[0, 0] = lax.fori_loop(0, n, recur, jnp.zeros((dv, dk), F32)).T

    grp = GLA_GROUP
    rows = grp * c
    ri =lax.broadcasted_iota(jnp.int32, (rows, rows), 0)
    ci = lax.broadcasted_iota(jnp.int32, (rows, rows), 1)
    keep = (ri // c == ci // c) & (ri >= ci)
    ng = ng_ref[...]

    def group_out(i, carry):
        r0 = pl.multiple_of(i * rows, rows)
        rs = pl.ds(r0, rows)
        a = lax.dot_general(qt_ref[rs, :], kt_ref[rs, :], nt, preferred_element_type=F32)
        a = jnp.where(keep, a, 0.0).astype(BF16)
        o = jnp.dot(a, v_ref[rs, :], preferred_element_type=F32)
        inter = [lax.dot_general(qe_ref[pl.ds(r0 + j * c, c), :], u_ref[i * grp + j].astype(BF16), nt,
                                 preferred_element_type=F32) for j in range(grp)]
        o = o + jnp.concatenate(inter, axis=0)
        o = _rms(o, ng) * jax.nn.silu(r_ref[rs, :])
        o_ref[rs, :] = o.astype(BF16)
        return carry

    lax.fori_loop(0, n // grp, group_out, 0, unroll=2)


def _gla_prompt(q_l, k_l, v_l, r_l, misc, wa, ba, ng, bsz, t):
    h, dk, dv = GLA_HEADS, GLA_DK, GLA_DV
    assert t % (GLA_CHUNK * GLA_GROUP * 2) == 0
    n = t // GLA_CHUNK
    return pl.pallas_call(
        _gla_prompt_kernel,
        grid=(bsz, h),
        in_specs=[pl.BlockSpec((t, dk), lambda b, hh: (b, hh)), pl.BlockSpec((t, dk), lambda b, hh: (b, hh)),
                  pl.BlockSpec((t, dv), lambda b, hh: (b, hh)), pl.BlockSpec((t, dv), lambda b, hh: (b, hh)),
                  pl.BlockSpec((t, LANES), lambda b, hh: (b, 0)),
                  pl.BlockSpec((GLA_RANK, dk), lambda b, hh: (0, hh)), pl.BlockSpec((1, dk), lambda b, hh: (0, hh)),
                  _resident((1, dv))],
        out_specs=[pl.BlockSpec((t, dv), lambda b, hh: (b, hh)),
                   pl.BlockSpec((1, 1, dk, dv), lambda b, hh: (b, hh, 0, 0))],
        out_shape=[SDS((bsz * t, h * dv), BF16), SDS((bsz, h, dk, dv), F32)],
        scratch_shapes=[pltpu.VMEM((t, dk), BF16), pltpu.VMEM((t, dk), BF16), pltpu.VMEM((t, dk), BF16),
                        pltpu.VMEM((t, dk), BF16), pltpu.VMEM((t, dk), F32),
                        pltpu.VMEM((n, dv, dk), F32)],
        compiler_params=_params("parallel", "parallel"),
    )(q_l, k_l, v_l, r_l, misc, wa, ba, ng)


def _merge_rows(oa, ob, x, g_ref, wg_ref, wa_ref, wb_ref, wo_ref):
    h = _rms(x, g_ref[...]).astype(BF16)
    d = D_MODEL
    ga = jax.nn.sigmoid(lax.dot_general(h, wg_ref[0:d, :], _NT, preferred_element_type=F32))
    u = ga * jnp.dot(oa, wa_ref[...], preferred_element_type=F32)
    gb = jax.nn.sigmoid(lax.dot_general(h, wg_ref[d:2 * d, :], _NT, preferred_element_type=F32))
    u = u + gb * jnp.dot(ob, wb_ref[...], preferred_element_type=F32)
    return x + jnp.dot(u.astype(BF16), wo_ref[...], preferred_element_type=F32)


def _merge_kernel(oa_ref, ob_ref, x_ref, oas_ref, obs_ref, xs_ref, g_ref, wg_ref, wa_ref, wb_ref, wo_ref,
                  x1_ref, x1s_ref):
    weights = (g_ref, wg_ref, wa_ref, wb_ref, wo_ref)
    x1_ref[...] = _merge_rows(oa_ref[...], ob_ref[...], x_ref[...], *weights)

    @pl.when(pl.program_id(0) == 0)
    def _():
        x1s_ref[...] = _merge_rows(oas_ref[...], obs_ref[...], xs_ref[...], *weights)


def _merge(o_a, o_b, x, oa_s, ob_s, x_s, norm_g, w_t, gate_row0, w_a, w_b, w_o, tm):
    m, d = x.shape
    assert m % tm == 0
    row_specs = [pl.BlockSpec((tm, NSA_Q), lambda i: (i, 0)), pl.BlockSpec((tm, GLA_HEADS * GLA_DV), lambda i: (i, 0)),
                 pl.BlockSpec((tm, d), lambda i: (i, 0))]
    return pl.pallas_call(
        _merge_kernel, grid=(m // tm,),
        in_specs=(row_specs + [_resident(a.shape) for a in (oa_s, ob_s, x_s, norm_g)]
                  + [_row_window(gate_row0, 2 * d, d)] + [_resident(a.shape) for a in (w_a, w_b, w_o)]),
        out_specs=[pl.BlockSpec((tm, d), lambda i: (i, 0)), pl.BlockSpec(x_s.shape, lambda i: (0, 0))],
        out_shape=[SDS((m, d), F32), SDS(x_s.shape, F32)],
        compiler_params=_params("arbitrary"),
    )(o_a, o_b, x, oa_s, ob_s, x_s, norm_g, w_t, w_a, w_b, w_o)


def _mlp_kernel(x1_ref, x1s_ref, g2_ref, wu_ref, wd_ref, gf_ref, y_ref, ys_ref, h_ref):
    i = pl.program_id(0)
    j = pl.program_id(1)
    last = pl.num_programs(1) - 1
    tm = x1_ref.shape[0]

    @pl.when(j == 0)
    def _():
        x1 = x1_ref[...]
        h_ref[0:tm, :] = _rms(x1, g2_ref[...]).astype(BF16)
        y_ref[...] = x1

    @pl.when((j == 0) & (i == 0))
    def _():
        x1s = x1s_ref[...]
        h_ref[tm:, :] = _rms(x1s, g2_ref[...]).astype(BF16)
        ys_ref[...] = x1s

    up = jnp.maximum(jnp.dot(h_ref[...], wu_ref[...].astype(BF16), preferred_element_type=F32), 0.0)
    res = jnp.dot((up * up).astype(BF16), wd_ref[...].astype(BF16), preferred_element_type=F32)
    y_ref[...] += res[0:tm]

    @pl.when(i == 0)
    def _():
        ys_ref[...] += res[tm:]

    @pl.when(j == last)
    def _():
        y_ref[...] = _rms(y_ref[...], gf_ref[...])

    @pl.when((j == last) & (i == 0))
    def _():
        ys_ref[...] = _rms(ys_ref[...], gf_ref[...])


def _mlp(x1, x1_s, g2, w_up, w_down, gf, tm, tf):
    m, d = x1.shape
    ff = w_up.shape[1]
    assert m % tm == 0 and ff % tf == 0
    small = pl.BlockSpec(x1_s.shape, lambda i, j: (0, 0))
    return pl.pallas_call(
        _mlp_kernel, grid=(m // tm, ff // tf),
        in_specs=[pl.BlockSpec((tm, d), lambda i, j: (i, 0)), small, _resident((1, d)),
                  pl.BlockSpec((d, tf), lambda i, j: (0, j)), pl.BlockSpec((tf, d), lambda i, j: (j, 0)),
                  _resident((1, d))],
        out_specs=[pl.BlockSpec((tm, d), lambda i, j: (i, 0)), small],
        out_shape=[SDS((m, d), F32), SDS(x1_s.shape, F32)],
        scratch_shapes=[pltpu.VMEM((tm + x1_s.shape[0], d), BF16)],
        compiler_params=_params("arbitrary", "arbitrary", vmem_limit_bytes=MLP_VMEM_LIMIT_BYTES),
    )(x1, x1_s, g2, w_up, w_down, gf)


def _prep_weights(norm1_g, w_in, b_nsa_gate, cmp_pe_k, cmp_pe_v, cmp_k_w1, cmp_k_w2, cmp_v_w1, cmp_v_w2,
                  gla_w_a2, gla_b_a, gla_norm_g, w_br_a, w_br_b, w_o, norm2_g, w_up, w_down, norm_f):
    pts = [0]
    for s in SPLITS:
        pts.append(pts[-1] + s)
    c_q, c_kv, c_g, c_ql, c_kl, c_vl, c_rl, c_lr, c_br, c_end = pts
    gcols = jnp.asarray([c_g + (g * NSA_HPG + h) * 3 + c for g in range(NSA_KV_HEADS)
                         for c in range(3) for h in range(NSA_HPG)], jnp.int32)
    w_t = w_in.T.astype(BF16)
    pad = jnp.zeros((LANES - N_GATE - GLA_RANK, D_MODEL), BF16)
    w_misc_t = jnp.concatenate([w_t[gcols], w_t[c_lr:c_br], pad], axis=0)
    b_misc = jnp.concatenate([b_nsa_gate[gcols - c_g], jnp.zeros((LANES - N_GATE,), F32)])[None, :]
    w = dict(
        norm1=norm1_g[None, :], norm2=norm2_g[None, :], norm_f=norm_f[None, :],
        w_t=w_t, w_misc_t=w_misc_t, row_a=c_q, rows_a=c_g - c_q, row_b=c_ql, rows_b=c_lr - c_ql, row_gate=c_br,
        b_misc=b_misc,
        gla_wa=gla_w_a2.astype(BF16), gla_ba=gla_b_a[None, :], gla_ng=gla_norm_g[None, :],
        w_br_a=w_br_a.astype(BF16), w_br_b=w_br_b.astype(BF16), w_o=w_o.astype(BF16),
        w_up=w_up, w_down=w_down,
    )
    cw = {}
    cw['pe_k'], cw['w1_k'], cw['w2_k'] = _cmp_weights(cmp_pe_k, cmp_k_w1, cmp_k_w2)
    cw['pe_v'], cw['w1_v'], cw['w2_v'] = _cmp_weights(cmp_pe_v, cmp_v_w1, cmp_v_w2)
    w['cmp'] = cw
    return w


_GQ =GLA_HEADS * GLA_DK
_GV = GLA_HEADS * GLA_DV


def _layer_prompt(x, w):
    bsz, t, d = x.shape
    m = bsz * t
    x2 = x.reshape(m, d)
    zb = jnp.zeros((1, LANES), F32)
    spec_a = [('hmt', 0, NSA_Q, NSA_HEAD_DIM ** -0.5 * LOG2_E)]
    for j in range(6):
        cols = (NSA_Q + j * NSA_KV, NSA_Q + (j + 1) * NSA_KV, 1.0)
        spec_a.append(('f32t',) + cols)
        if j >= 2:
            spec_a.append(('hm' if j % 2 == 0 else 'hmvt',) + cols)
    q_t, r0, r1, r2, ks_hm, r3, vs_t, r4, kw_hm, r5, vw_t = _proj(
        x2, w['norm1'], w['w_t'], w['row_a'], w['rows_a'], w['w_misc_t'], zb, spec_a, PROJ_TM, seq_len=t)
    spec_b = [('f32', 0, _GQ, 1.0), ('f32', _GQ, 2 * _GQ, 1.0), ('bf16', 2 * _GQ, 2 * _GQ + _GV, 1.0),
              ('f32', 2 * _GQ + _GV, 2 * _GQ + 2 * _GV, 1.0),
              ('misc', 2 * _GQ + 2 * _GV, 2 * _GQ + 2 * _GV + LANES, 1.0), ('gates', 0, 0, 1.0)]
    q_l, k_l, v_l, r_l, misc, gates_hm = _proj(x2, w['norm1'], w['w_t'], w['row_b'], w['rows_b'], w['w_misc_t'],
                                               w['b_misc'], spec_b, PROJ_TM)

    n_pages = t // PAGE_SIZE
    ident = jnp.broadcast_to(jnp.arange(n_pages, dtype=jnp.int32), (bsz, n_pages))
    kcmp, vcmp = _compress(ident, r0, r1, w['cmp'], paged=False)
    o_a = _nsa_prompt(q_t, ks_hm, vs_t, kw_hm, vw_t, kcmp, vcmp, gates_hm, bsz, t)
    o_b, s_new = _gla_prompt(q_l, k_l, v_l, r_l, misc, w['gla_wa'], w['gla_ba'], w['gla_ng'], bsz, t)
    def token_major(a):
        return a.reshape(bsz, NSA_KV_HEADS, NSA_HEAD_DIM, a.shape[-1]).transpose(0, 3, 1, 2)

    rows = [token_major(a) for a in (r0, r1, r2, r3)]
    n_keep = min(WINDOW, t)
    wins = [token_major(a[:, :, t - n_keep:]) for a in (r4, r5)]
    return (o_a, o_b, x2), rows, wins, s_new


def _channel_mix(prompt, sample, w):
    (oa_p, ob_p, x_p), (oa_s, ob_s, x_s) = prompt, sample
    m = x_p.shape[0]
    x1_p, x1_s = _merge(oa_p, ob_p, x_p, oa_s, ob_s, x_s, w['norm1'], w['w_t'], w['row_gate'], w['w_br_a'],
                        w['w_br_b'], w['w_o'], MERGE_TM)
    tm = next(c for c in (MLP_TM, PROJ_TM, MERGE_TM) if m % c == 0)
    return _mlp(x1_p, x1_s, w['norm2'], w['w_up'], w['w_down'], w['norm_f'], tm, MLP_TF)


def _group_rows(parts):
    rowg = lax.broadcasted_iota(jnp.int32, parts[0].shape, 0) // NSA_HPG
    out = parts[0]
    for g in range(1, NSA_KV_HEADS):
        out = jnp.where(rowg == g, parts[g], out)
    return out


def _sample_select_kernel(q_ref, kc_ref, vc_ref, oc_ref, idx_ref, *, n_cmp, pos, n_pick):
    bsz, nh = q_ref.shape[0], q_ref.shape[1]
    nt = (((1,), (1,)), ((), ()))
    n_chunk = kc_ref.shape[2]
    s = jnp.concatenate(
        [_group_rows([lax.dot_general(q_ref[b], kc_ref[b, g], nt, preferred_element_type=F32)
                      for g in range(NSA_KV_HEADS)]) for b in range(bsz)], axis=0)
    lane = lax.broadcasted_iota(jnp.int32, s.shape, 1)
    mask = (lane < n_cmp) & (lane * CMP_STRIDE + (CMP_BLOCK - 1) <= pos)
    e, l = _softmax_rows(s, mask)
    p = e * _safe_inv(l)
    pb = p.astype(BF16)
    for b in range(bsz):
        oc_ref[b] = _group_rows([lax.dot_general(pb[b * nh:(b + 1) * nh], vc_ref[b, g], nt,
                                                 preferred_element_type=F32) for g in range(NSA_KV_HEADS)])
    nr = p.shape[0]
    y = p + pltpu.roll(p, nr - 1, axis=0)
    psum = y + pltpu.roll(y, nr - 2, axis=0)
    imp = pltpu.roll(psum, 1, axis=1) + psum
    r = SLC_BLOCK // CMP_STRIDE
    for o in range(1, r):
        imp = imp + pltpu.roll(psum, n_chunk - o, axis=1)
    blk = lane // r
    is_blk = lane % r == 0
    valid = blk * SLC_BLOCK <= pos
    forced = (blk == 0) | (blk == pos // SLC_BLOCK)
    score = jnp.where(is_blk & valid, imp + jnp.where(forced, FORCE, 0.0), NEG)
    lane_f = lane.astype(F32)
    out_lane = lax.broadcasted_iota(jnp.int32, (nr, LANES), 1)
    picked = jnp.zeros((nr, LANES), F32)
    for k in range(n_pick):
        mx = jnp.max(score, axis=-1, keepdims=True)
        ix = jnp.min(jnp.where(score == mx, lane_f, float(n_chunk)), axis=-1, keepdims=True)
        picked = jnp.where(out_lane == k, ix, picked)
        score = jnp.where(lane_f == ix, 2.0 * NEG, score)
    idx_ref[...] = (picked.astype(jnp.int32) // r).reshape(idx_ref.shape)


def _sample_select(q_s, kcmp, vcmp, pos, n_pick):
    bsz, g, n_chunk, dh = kcmp.shape
    nh = NSA_HEADS
    return pl.pallas_call(
        functools.partial(_sample_select_kernel, n_cmp=n_chunk - 1, pos=pos, n_pick=n_pick),
        grid=(1,),
        in_specs=[pl.BlockSpec((bsz, nh, dh), lambda i: (0, 0, 0)),
                  pl.BlockSpec(kcmp.shape, lambda i: (0, 0, 0, 0)), pl.BlockSpec(vcmp.shape, lambda i: (0, 0, 0, 0))],
        out_specs=[pl.BlockSpec((bsz, nh, dh), lambda i: (0, 0, 0)), pl.BlockSpec((bsz, nh, LANES), lambda i: (0, 0, 0))],
        out_shape=[SDS((bsz, nh, dh), F32), SDS((bsz, nh, LANES), jnp.int32)],
        compiler_params=_params("arbitrary"),
    )(q_s, kcmp, vcmp)


def _sample_attend_kernel(idx_ref, pt_ref, q_ref, oc_ref, ks_hbm, vs_hbm, kw_ref, vw_ref,
                          nks_ref, nvs_ref, nkw_ref, nvw_ref, gate_ref, o_ref, ksel, vsel, sem, *, n_pick):
    b = pl.program_id(0)
    g_n, dh = NSA_KV_HEADS, NSA_HEAD_DIM
    half = PAGE_SIZE // SLC_BLOCK

    def copies(bb, slot):
        out = []
        for g in range(g_n):
            for r in range(n_pick):
                page = pt_ref[bb, idx_ref[bb * g_n + g, r] // half]
                out.append(pltpu.make_async_copy(ks_hbm.at[page, g], ksel.at[slot, g, r], sem.at[slot, 0]))
                out.append(pltpu.make_async_copy(vs_hbm.at[page, g], vsel.at[slot, g, r], sem.at[slot, 1]))
        return out

    slot = b % 2

    @pl.when(b == 0)
    def _():
        for c in copies(b, slot):
            c.start()

    @pl.when(b + 1 < pl.num_programs(0))
    def _():
        for c in copies(b + 1, 1 - slot):
            c.start()

    q = q_ref[0]
    qf = q.astype(F32)
    nt = (((1,), (1,)), ((), ()))

    def attend(keys_t, vals_t, bias, k_new, v_new):
        s = _group_rows([jnp.dot(q, keys_t[g].astype(BF16), preferred_element_type=F32)
                         + (0.0 if bias is None else bias[g]) for g in range(g_n)])
        s_new = _group_rows([jnp.sum(qf * k_new[:, g * dh:(g + 1) * dh], axis=-1, keepdims=True)
                             for g in range(g_n)])
        m = jnp.maximum(jnp.max(s, axis=-1, keepdims=True), s_new)
        e = jnp.exp(s - m)
        e_new = jnp.exp(s_new - m)
        l = jnp.sum(e, axis=-1, keepdims=True) + e_new
        eb = e.astype(BF16)
        acc = _group_rows([lax.dot_general(eb, vals_t[g].astype(BF16), nt, preferred_element_type=F32)
                           + e_new * v_new[:, g * dh:(g + 1) * dh] for g in range(g_n)])
        return acc / l

    o_w = attend([kw_ref[0, g] for g in range(g_n)], [vw_ref[0, g] for g in range(g_n)], None,
                 nkw_ref[0], nvw_ref[0])
    for c in copies(b, slot):
        c.wait()
    lin = lax.broadcasted_iota(jnp.int32, (1, PAGE_SIZE), 1)
    bias = []
    for g in range(g_n):
        parts = []
        for r in range(n_pick):
            off = (idx_ref[b * g_n + g, r] % half) * SLC_BLOCK
            parts.append(jnp.where((lin >= off) & (lin < off + SLC_BLOCK), 0.0, NEG))
        bias.append(jnp.concatenate(parts, axis=1))

    def tiles(buf, g):
        return jnp.concatenate([buf[slot, g, r] for r in range(n_pick)], axis=1)

    o_s = attend([tiles(ksel, g) for g in range(g_n)], [tiles(vsel, g) for g in range(g_n)], bias,
                 nks_ref[0], nvs_ref[0])

    gt = jnp.broadcast_to(gate_ref[0], (LANES, LANES)).T
    nh = NSA_HEADS
    o_ref[0] = (gt[0:nh, 0:dh] * oc_ref[0] + gt[nh:2 * nh, 0:dh] * o_s + gt[2 * nh:3 * nh, 0:dh] * o_w)


def _sample_attend(idx, page_table, q_s, o_c, slc_k, slc_v, win_k, win_v, new_rows, gates, n_pick):
    bsz = page_table.shape[0]
    nh, dh, g = NSA_HEADS, NSA_HEAD_DIM, NSA_KV_HEADS
    wlen = win_k.shape[-1]
    row_spec = pl.BlockSpec((1, 1, NSA_KV), lambda b, *_: (b, 0, 0))
    win_spec = pl.BlockSpec((1, g, dh, wlen), lambda b, *_: (b, 0, 0, 0))
    head_spec = pl.BlockSpec((1, nh, dh), lambda b, *_: (b, 0, 0))
    any_spec = pl.BlockSpec(memory_space=pl.ANY)
    grid_spec = pltpu.PrefetchScalarGridSpec(
        num_scalar_prefetch=2, grid=(bsz,),
        in_specs=[head_spec, head_spec, any_spec, any_spec, win_spec, win_spec,
                  row_spec, row_spec, row_spec, row_spec, pl.BlockSpec((1, 1, LANES), lambda b, *_: (b, 0, 0))],
        out_specs=head_spec,
        scratch_shapes=[pltpu.VMEM((2, g, n_pick, dh, PAGE_SIZE), F32), pltpu.VMEM((2, g, n_pick, dh, PAGE_SIZE), F32),
                        pltpu.SemaphoreType.DMA((2, 2))])
    return pl.pallas_call(
        functools.partial(_sample_attend_kernel, n_pick=n_pick),
        grid_spec=grid_spec, out_shape=SDS((bsz, nh, dh), F32),
        compiler_params=_params("arbitrary"),
    )(idx, page_table, q_s, o_c, slc_k, slc_v, win_k, win_v, *new_rows, gates)


def _gla_sample_kernel(q_ref, k_ref, v_ref, r_ref, misc_ref, wa_ref, ba_ref, ng_ref, s0_ref, o_ref, s_ref):
    dk, dv = GLA_DK, GLA_DV
    lr = jnp.broadcast_to(misc_ref[0][:, N_GATE:N_GATE + GLA_RANK], (16, GLA_RANK)).astype(BF16)
    x = jnp.dot(lr, wa_ref[...], preferred_element_type=F32)[0:1] + ba_ref[...]
    g_all = jax.nn.log_sigmoid(x) / GLA_TAU

    def col(v):
        t = jnp.broadcast_to(v, (dk, dk)).T
        return jnp.concatenate([t] * (dv // dk), axis=1)

    outs = []
    for h in range(GLA_HEADS):
        g = g_all[:, h * dk:(h + 1) * dk]
        q = q_ref[0][:, h * dk:(h + 1) * dk] * (GLA_DK ** -0.5)
        k = k_ref[0][:, h * dk:(h + 1) * dk]
        v = v_ref[0][:, h * dv:(h + 1) * dv]
        s0 = s0_ref[0, h]
        q_t = q * jnp.exp(g)
        k_t = k * jnp.exp(-g)
        a = jnp.sum(q_t * k_t, axis=-1, keepdims=True)
        o = jnp.sum(col(q_t) * s0, axis=0, keepdims=True) + a * v
        s_ref[0, h] = col(jnp.exp(g)) * s0 + col(k) * v
        outs.append(_rms(o, ng_ref[...]) * jax.nn.silu(r_ref[0][:, h * dv:(h + 1) * dv]))
    o_ref[0] = jnp.concatenate(outs, axis=-1)


def _gla_sample(q_l, k_l, v_l, r_l, misc, wa, ba, ng, s0):
    bsz, h, dk, dv = s0.shape

    def row(n):
        return pl.BlockSpec((1, 1, n), lambda b: (b, 0, 0))

    st_spec = pl.BlockSpec((1, h, dk, dv), lambda b: (b, 0, 0, 0))
    return pl.pallas_call(
        _gla_sample_kernel, grid=(bsz,),
        in_specs=[row(h * dk), row(h * dk), row(h * dv), row(h * dv), row(LANES),
                  _resident(wa.shape), _resident(ba.shape), _resident(ng.shape), st_spec],
        out_specs=[row(h * dv), st_spec],
        out_shape=[SDS((bsz, 1, h * dv), F32), SDS((bsz, h, dk, dv), F32)],
        compiler_params=_params("parallel"),
    )(q_l, k_l, v_l, r_l, misc, wa, ba, ng, s0)


SAMPLE_ROWS = 16


def _layer_sample(x, caches, wins, s0, page_table, w):
    bsz, t, d = x.shape
    n_pages = page_table.shape[1]
    pos = n_pages * PAGE_SIZE
    assert t == 1 and bsz <= SAMPLE_ROWS and pos % SLC_BLOCK == 0 and wins[0].shape[1] == WINDOW
    assert pos // SLC_BLOCK >= N_SELECT
    mp = SAMPLE_ROWS
    x2 = jnp.pad(x.reshape(bsz, d), ((0, mp - bsz), (0, 0)))
    zb = jnp.zeros((1, LANES), F32)
    spec_a = [('bf16', 0, NSA_Q, NSA_HEAD_DIM ** -0.5)]
    spec_a += [('f32', NSA_Q + j * NSA_KV, NSA_Q + (j + 1) * NSA_KV, 1.0) for j in range(6)]
    q_s, r0, r1, r2, r3, r4, r5 = _proj(x2, w['norm1'], w['w_t'], w['row_a'], w['rows_a'], w['w_misc_t'], zb,
                                        spec_a, mp)
    spec_b = [('f32', 0, _GQ, 1.0), ('f32', _GQ, 2 * _GQ, 1.0), ('f32', 2 * _GQ, 2 * _GQ + _GV, 1.0),
              ('f32', 2 * _GQ + _GV, 2 * _GQ + 2 * _GV, 1.0),
              ('misc', 2 * _GQ + 2 * _GV, 2 * _GQ + 2 * _GV + LANES, 1.0)]
    q_l, k_l, v_l, r_l, misc = _proj(x2, w['norm1'], w['w_t'], w['row_b'], w['rows_b'], w['w_misc_t'], w['b_misc'],
                                     spec_b, mp)

    cache_t = [a.transpose(0, 2, 3, 1) for a in caches]
    win_t = [a.transpose(0, 2, 3, 1) for a in wins]
    n_pool = caches[0].shape[0]
    cmp_pages = [a.reshape(n_pool, NSA_KV, PAGE_SIZE) for a in cache_t[:2]]
    kcmp, vcmp = _compress(page_table, cmp_pages[0], cmp_pages[1], w['cmp'], paged=True)
    q_h = q_s.reshape(mp, NSA_HEADS, NSA_HEAD_DIM)
    n_pick = N_SELECT - 1
    o_c, idx = _sample_select(q_h, kcmp, vcmp, pos, n_pick)
    idx = idx[:, ::NSA_HPG, :n_pick].reshape(bsz * NSA_KV_HEADS, n_pick)
    gates = misc[:, :N_GATE].reshape(mp, NSA_KV_HEADS, 3, NSA_HPG).transpose(0, 2, 1, 3).reshape(mp, 1, N_GATE)
    gates = jnp.pad(gates, ((0, 0), (0, 0), (0, LANES - N_GATE)))
    new_rows = [a.reshape(mp, 1, NSA_KV) for a in (r2, r3, r4, r5)]
    o_a = _sample_attend(idx, page_table, q_h, o_c, cache_t[2], cache_t[3], win_t[0], win_t[1], new_rows, gates,
                         n_pick)
    o_a = jnp.pad(o_a.reshape(bsz, NSA_Q), ((0, mp - bsz), (0, 0))).astype(BF16)

    def r3d(a):
        return a.reshape(mp, 1, a.shape[-1])

    o_b, s_new = _gla_sample(r3d(q_l), r3d(k_l), r3d(v_l), r3d(r_l), r3d(misc), w['gla_wa'], w['gla_ba'],
                             w['gla_ng'], s0)
    o_b = jnp.pad(o_b.reshape(bsz, _GV), ((0, mp - bsz), (0, 0))).astype(BF16)
    kvh = (bsz, 1, NSA_KV_HEADS, NSA_HEAD_DIM)
    rows = [a[:bsz].reshape(kvh) for a in (r0, r1, r2, r3)]
    new_wins = [jnp.concatenate([c[:, 1:], a[:bsz].reshape(kvh)], axis=1) for c, a in zip(wins, (r4, r5))]
    return (o_a, o_b, x2), rows, new_wins, s_new


def kernel(x_prompt, x_sample, cache_cmp_k, cache_cmp_v, cache_slc_k, cache_slc_v, cache_win_k, cache_win_v, state_gla, page_table, norm1_g, w_in, b_nsa_gate, cmp_pe_k, cmp_pe_v, cmp_k_w1, cmp_k_w2, cmp_v_w1, cmp_v_w2, gla_w_a2, gla_b_a, gla_norm_g, w_br_a, w_br_b, w_o, norm2_g, w_up, w_down, norm_f):
    assert DEPTH == 1 and norm1_g.shape[0] == 1
    w = _prep_weights(norm1_g[0], w_in[0], b_nsa_gate[0], cmp_pe_k[0], cmp_pe_v[0], cmp_k_w1[0], cmp_k_w2[0],
                      cmp_v_w1[0], cmp_v_w2[0], gla_w_a2[0], gla_b_a[0], gla_norm_g[0], w_br_a[0], w_br_b[0],
                      w_o[0], norm2_g[0], w_up[0], w_down[0], norm_f)
    mix_p, rows_p, wins_p, s_p = _layer_prompt(x_prompt, w)
    caches = [c[0] for c in (cache_cmp_k, cache_cmp_v, cache_slc_k, cache_slc_v)]
    mix_s, rows_s, wins_s, s_s = _layer_sample(x_sample, caches, [cache_win_k[0], cache_win_v[0]], state_gla[0],
                                               page_table, w)
    y_p, y_s = _channel_mix(mix_p, mix_s, w)
    y_p = y_p.reshape(x_prompt.shape)
    y_s = y_s[:x_sample.shape[0]].reshape(x_sample.shape)
    outs_p = [a[None] for a in rows_p + wins_p + [s_p]]
    outs_s = [a[None] for a in rows_s + wins_s + [s_s]]
    return (y_p, y_s, *outs_p, *outs_s)
```

```python
import functools

import jax
import jax.numpy as jnp
from jax import lax
from jax.experimental import pallas as pl
from jax.experimental.pallas import tpu as pltpu

D_MODEL = 2048
DEPTH = 1
PAGE_SIZE = 128
NSA_HEADS = 16
NSA_KV_HEADS = 4
NSA_HPG = NSA_HEADS // NSA_KV_HEADS
NSA_HEAD_DIM = 64
NSA_Q = NSA_HEADS * NSA_HEAD_DIM
NSA_KV = NSA_KV_HEADS * NSA_HEAD_DIM
CMP_STRIDE = 16
CMP_BLOCK = 32
CMP_HIDDEN = 128
SLC_BLOCK = 64
N_SELECT = 16
WINDOW = 512
GLA_HEADS = 4
GLA_DK = (D_MODEL // 4) // GLA_HEADS
GLA_DV = (D_MODEL // 2) // GLA_HEADS
GLA_RANK = 16
GLA_TAU = 16.0
EPS = 1e-6
NEG = -1e30
FORCE = 1e4
SPLITS = (NSA_Q, 6 * NSA_KV, 3 * NSA_HEADS,
          GLA_HEADS * GLA_DK, GLA_HEADS * GLA_DK, GLA_HEADS * GLA_DV, GLA_HEADS * GLA_DV,
          GLA_RANK, 2 * D_MODEL)

F32 = jnp.float32
BF16 = jnp.bfloat16
LANES = 128
VMEM_LIMIT_BYTES = 56 * 1024 * 1024
MLP_VMEM_LIMIT_BYTES = 60 * 1024 * 1024
N_GATE = 3 * NSA_HEADS
GLA_CHUNK = 32
GLA_UNROLL = 16
GLA_GROUP = 4
NSA_TQ = 256
NSA_TK = 256
VT_ROWS = NSA_HEAD_DIM + 16
CMP_PAGES = 16
PROJ_TM = 512
MERGE_TM = 256
MLP_TM = 1024
MLP_TF = 512
SDS = jax.ShapeDtypeStruct


def _params(*sem, vmem_limit_bytes=VMEM_LIMIT_BYTES):
    return pltpu.CompilerParams(dimension_semantics=sem, vmem_limit_bytes=vmem_limit_bytes)


def _resident(shape):
    nd = len(shape)
    return pl.BlockSpec(shape, lambda *_: (0,) * nd, pipeline_mode=pl.Buffered(1))


def _row_window(row0, n_rows, width):
    return pl.BlockSpec((pl.Element(n_rows), pl.Element(width)), lambda *_: (row0, 0),
                        pipeline_mode=pl.Buffered(1))


def _rms(x, g):
    return x * lax.rsqrt(jnp.mean(x * x, axis=-1, keepdims=True) + EPS) * g


_NT = (((1,), (1,)), ((), ()))


def _proj_kernel(x_ref, g_ref, w_ref, wm_ref, b_ref, *out_refs, spec):
    h = _rms(x_ref[...], g_ref[...]).astype(BF16)
    misc = None
    products = {}
    for o_ref, (kind, c0, c1, scale) in zip(out_refs, spec):
        if kind == 'gates':
            for g in range(NSA_KV_HEADS):
                o_ref[g] = misc if g == 0 else pltpu.roll(misc, LANES - g * 3 * NSA_HPG, axis=1)
            continue
        if (kind == 'misc', c0, c1) not in products:
            rows = wm_ref[...] if kind == 'misc' else w_ref[c0:c1, :]
            products[(kind == 'misc', c0, c1)] = lax.dot_general(h, rows, _NT, preferred_element_type=F32)
        r = products[(kind == 'misc', c0, c1)]
        if scale != 1.0:
            r = r * scale
        if kind == 'f32':
            o_ref[...] = r
        elif kind == 'f32t':
            o_ref[0] = r.T
        elif kind == 'bf16':
            o_ref[...] = r.astype(BF16)
        elif kind == 'sigmoid':
            o_ref[...] = jax.nn.sigmoid(r)
        elif kind == 'hm':
            for i in range((c1 - c0) // NSA_HEAD_DIM):
                o_ref[i] = r[:, i * NSA_HEAD_DIM:(i + 1) * NSA_HEAD_DIM].astype(BF16)
        elif kind == 'hmt':
            rt = r.T
            for i in range((c1 - c0) // NSA_HEAD_DIM):
                o_ref[i] = rt[i * NSA_HEAD_DIM:(i + 1) * NSA_HEAD_DIM].astype(BF16)
        elif kind == 'hmvt':
            rt = r.T
            sub = lax.broadcasted_iota(jnp.int32, (VT_ROWS - NSA_HEAD_DIM, r.shape[0]), 0)
            ones = jnp.where(sub == 0, 1.0, 0.0)
            for i in range((c1 - c0) // NSA_HEAD_DIM):
                piece = rt[i * NSA_HEAD_DIM:(i + 1) * NSA_HEAD_DIM]
                o_ref[i] = jnp.concatenate([piece, ones], axis=0).astype(BF16)
        elif kind == 'misc':
            lane = lax.broadcasted_iota(jnp.int32, r.shape, 1)
            misc = jnp.where(lane < N_GATE, jax.nn.sigmoid(r + b_ref[...]), r)
            o_ref[...] = misc


def _proj(x, norm_g, w_t, row0, n_rows, w_misc_t, bias, spec, tm, seq_len=None):
    m, d = x.shape
    assert m % tm == 0 and w_t.shape[1] == d
    out_shape, out_specs = [], []
    for kind, c0, c1, _ in spec:
        if kind == 'f32t':
            assert seq_len % tm == 0
            per_seq = seq_len // tm
            out_shape.append(SDS((m // seq_len, c1 - c0, seq_len), F32))
            out_specs.append(pl.BlockSpec((1, c1 - c0, tm), lambda i: (i // per_seq, 0, i % per_seq)))
        elif kind == 'hm':
            nh = (c1 - c0) // NSA_HEAD_DIM
            out_shape.append(SDS((nh, m, NSA_HEAD_DIM), BF16))
            out_specs.append(pl.BlockSpec((nh, tm, NSA_HEAD_DIM), lambda i: (0, i, 0)))
        elif kind in ('hmt', 'hmvt'):
            nh = (c1 - c0) // NSA_HEAD_DIM
            nrow = NSA_HEAD_DIM if kind == 'hmt' else VT_ROWS
            out_shape.append(SDS((nh, nrow, m), BF16))
            out_specs.append(pl.BlockSpec((nh, nrow, tm), lambda i: (0, 0, i)))
        elif kind == 'gates':
            out_shape.append(SDS((NSA_KV_HEADS, m, LANES), F32))
            out_specs.append(pl.BlockSpec((NSA_KV_HEADS, tm, LANES), lambda i: (0, i, 0)))
        else:
            out_shape.append(SDS((m, c1 - c0), BF16 if kind == 'bf16' else F32))
            out_specs.append(pl.BlockSpec((tm, c1 - c0), lambda i: (i, 0)))
    return pl.pallas_call(
        functools.partial(_proj_kernel, spec=tuple(spec)),
        grid=(m // tm,),
        in_specs=[pl.BlockSpec((tm, d), lambda i: (i, 0)), _resident((1, d)), _row_window(row0, n_rows, d),
                  _resident(w_misc_t.shape), _resident((1, LANES))],
        out_specs=out_specs, out_shape=out_shape,
        compiler_params=_params("parallel"),
    )(x, norm_g, w_t, w_misc_t, bias)


def _cmp_kernel(pt_ref, k_hbm, v_hbm, perm_ref, pek_ref, pev_ref, w1k_ref, w1v_ref, w2k_ref, w2v_ref,
                ok_ref, ov_ref, kbuf, vbuf, hk, hv, sem, *, n_pages_step, paged):
    b = pl.program_id(0)
    s = pl.program_id(1)
    ns = pl.num_programs(1)
    t = b * ns + s
    total = pl.num_programs(0) * ns
    rows = n_pages_step * 8
    dh, hid_n = NSA_HEAD_DIM, CMP_HIDDEN

    def copies(tt, slot):
        bb = tt // ns
        ss = tt % ns
        out = []
        for p in range(n_pages_step):
            page = pt_ref[bb, ss * n_pages_step + p]
            if paged:
                src = [hbm.at[page] for hbm in (k_hbm, v_hbm)]
            else:
                tok = pl.ds(pl.multiple_of(page * PAGE_SIZE, PAGE_SIZE), PAGE_SIZE)
                src = [hbm.at[bb, :, tok] for hbm in (k_hbm, v_hbm)]
            out.append(pltpu.make_async_copy(src[0], kbuf.at[slot, p], sem.at[slot, 0]))
            out.append(pltpu.make_async_copy(src[1], vbuf.at[slot, p], sem.at[slot, 1]))
        return out

    slot = t % 2

    @pl.when(t == 0)
    def _():
        for c in copies(t, slot):
            c.start()

    @pl.when(t + 1 < total)
    def _():
        for c in copies(t + 1, 1 - slot):
            c.start()

    for c in copies(t, slot):
        c.wait()

    r0 = pl.multiple_of(s * rows, rows)
    low = lax.broadcasted_iota(jnp.int32, (8, LANES), 1) < dh
    tok_pairs = CMP_STRIDE // 2
    nt = (((1,), (1,)), ((), ()))
    perm = perm_ref[...]
    for buf, w1_ref, h_ref in ((kbuf, w1k_ref, hk), (vbuf, w1v_ref, hv)):
        parts = [[[] for _ in range(tok_pairs)] for _ in range(NSA_KV_HEADS)]
        for p in range(n_pages_step):
            r = lax.dot_general(perm, buf[slot, p].astype(BF16), nt, preferred_element_type=F32)
            for q in range(2):
                for m in range(tok_pairs):
                    x0 = r[2 * m * 8:2 * m * 8 + 8, q * LANES:(q + 1) * LANES]
                    x1 = r[(2 * m + 1) * 8:(2 * m + 1) * 8 + 8, q * LANES:(q + 1) * LANES]
                    parts[2 * q][m].append(jnp.where(low, x0, pltpu.roll(x1, dh, axis=1)))
                    parts[2 * q + 1][m].append(jnp.where(low, pltpu.roll(x0, dh, axis=1), x1))
        for g in range(NSA_KV_HEADS):
            z = jnp.concatenate([jnp.concatenate(parts[g][m], axis=0) for m in range(tok_pairs)],
                                axis=1).astype(BF16)
            h_ref[pl.ds(r0, rows), g * 2 * hid_n:(g + 1) * 2 * hid_n] = jnp.dot(
                z, w1_ref[...], preferred_element_type=F32)

    @pl.when(s == ns - 1)
    def _():
        for h_ref, pe_ref, w1_ref, w2_ref, o_ref in ((hk, pek_ref, w1k_ref, w2k_ref, ok_ref),
                                                     (hv, pev_ref, w1v_ref, w2v_ref, ov_ref)):
            n_chunk = h_ref.shape[0]
            pe = jnp.broadcast_to(pe_ref[...], (2, 16, pe_ref.shape[2])).astype(BF16)
            pe_term = (jnp.dot(pe[0], w1_ref[:, 0:hid_n], preferred_element_type=F32)
                       + jnp.dot(pe[1], w1_ref[:, hid_n:2 * hid_n], preferred_element_type=F32))[0:1]
            hid = []
            for g in range(NSA_KV_HEADS):
                first = h_ref[:, g * 2 * hid_n:g * 2 * hid_n + hid_n]
                last = h_ref[:, g * 2 * hid_n + hid_n:(g + 1) * 2 * hid_n]
                hid.append(first + pltpu.roll(last, n_chunk - 1, axis=0) + pe_term)
            hid = jnp.concatenate(hid, axis=1)
            res = jnp.dot(jax.nn.gelu(hid).astype(BF16), w2_ref[...], preferred_element_type=F32)
            if o_ref is ok_ref:
                for g in range(NSA_KV_HEADS):
                    o_ref[0, g] = res[:, g * NSA_HEAD_DIM:(g + 1) * NSA_HEAD_DIM].astype(BF16)
            else:
                res_t = res.T
                for g in range(NSA_KV_HEADS):
                    o_ref[0, g] = res_t[g * NSA_HEAD_DIM:(g + 1) * NSA_HEAD_DIM].astype(BF16)


def _cmp_weights(pe, w1, w2):
    g = NSA_KV_HEADS
    half = CMP_STRIDE * NSA_HEAD_DIM
    pe_t = pe.reshape(2, 1, half)
    w1ab = jnp.concatenate([w1[:half], w1[half:]], axis=1)
    w2b = jnp.einsum('jd,gh->gjhd', w2, jnp.eye(g, dtype=F32)).reshape(g * CMP_HIDDEN, g * NSA_HEAD_DIM)
    return pe_t, w1ab.astype(BF16), w2b.astype(BF16)


def _compress(page_table, k_src, v_src, cw, paged):
    bsz, n_pages = page_table.shape
    p_step = min(CMP_PAGES, n_pages)
    ns = n_pages // p_step
    n_chunk = n_pages * 8
    page_shape = (NSA_KV, PAGE_SIZE)
    assert NSA_KV == 2 * LANES
    assert k_src.shape[1:] == (page_shape if paged else (NSA_KV, n_pages * PAGE_SIZE))
    half = CMP_STRIDE * NSA_HEAD_DIM
    gh = NSA_KV_HEADS * CMP_HIDDEN
    out_row = jnp.arange(PAGE_SIZE, dtype=jnp.int32)
    src_tok = (out_row % 8) * CMP_STRIDE + out_row // 8
    perm = (src_tok[:, None] == jnp.arange(PAGE_SIZE, dtype=jnp.int32)[None, :]).astype(BF16)
    k_sds =SDS((bsz, NSA_KV_HEADS, n_chunk, NSA_HEAD_DIM), BF16)
    v_sds = SDS((bsz, NSA_KV_HEADS, NSA_HEAD_DIM, n_chunk), BF16)
    k_spec = pl.BlockSpec((1, NSA_KV_HEADS, n_chunk, NSA_HEAD_DIM), lambda b, s, pt: (b, 0, 0, 0))
    v_spec = pl.BlockSpec((1, NSA_KV_HEADS, NSA_HEAD_DIM, n_chunk), lambda b, s, pt: (b, 0, 0, 0))
    grid_spec = pltpu.PrefetchScalarGridSpec(
        num_scalar_prefetch=1, grid=(bsz, ns),
        in_specs=[pl.BlockSpec(memory_space=pl.ANY), pl.BlockSpec(memory_space=pl.ANY),
                  _resident((PAGE_SIZE, PAGE_SIZE)), _resident((2, 1, half)), _resident((2, 1, half)),
                  _resident((half, 2 * CMP_HIDDEN)), _resident((half, 2 * CMP_HIDDEN)),
                  _resident((gh, NSA_KV)), _resident((gh, NSA_KV))],
        out_specs=[k_spec, v_spec],
        scratch_shapes=[pltpu.VMEM((2, p_step) + page_shape, F32), pltpu.VMEM((2, p_step) + page_shape, F32),
                        pltpu.VMEM((n_chunk, 2 * gh), F32), pltpu.VMEM((n_chunk, 2 * gh), F32),
                        pltpu.SemaphoreType.DMA((2, 2))])
    return pl.pallas_call(
        functools.partial(_cmp_kernel, n_pages_step=p_step, paged=paged),
        grid_spec=grid_spec, out_shape=[k_sds, v_sds],
        compiler_params=_params("arbitrary", "arbitrary"),
    )(page_table, k_src, v_src, perm, cw['pe_k'], cw['pe_v'], cw['w1_k'], cw['w1_v'], cw['w2_k'], cw['w2_v'])


def _softmax_rows(s, mask):
    sm = jnp.where(mask, s, NEG)
    m = jnp.max(sm, axis=-1, keepdims=True)
    e = jnp.where(mask, jnp.exp(sm - m), 0.0)
    return e, jnp.sum(e, axis=-1, keepdims=True)


def _safe_inv(l):
    return jnp.where(l > 0.0, 1.0 / jnp.where(l > 0.0, l, 1.0), 0.0)


M_INIT = -1e20
LOG2_E = 1.4426950408889634


def _nsa_prompt_kernel(qt_ref, kc_ref, vct_ref, ks_ref, vst_ref, kw_ref, vwt_ref, gate_ref,
                       o_ref, *scratch, n_cmp, n_slc, n_sel, n_qblocks):
    tq, tk, hpg, dh = NSA_TQ, NSA_TK, NSA_HPG, NSA_HEAD_DIM
    rk_refs = scratch[:tq // LANES]
    rest = scratch[tq // LANES:]
    selb_ref = rest[0]
    m_ref, acc_ref, s_refs = rest[1], rest[2], rest[3:5]
    mw_ref, accw_ref, sw_refs = rest[5], rest[6], rest[7:9]
    qi = pl.program_id(2)
    q0 = qi * tq
    n_chunk = kc_ref.shape[2]
    r = SLC_BLOCK // CMP_STRIDE
    n_row = n_chunk // r

    ci = lax.broadcasted_iota(jnp.int32, (n_chunk, tq), 0)
    pos_c = q0 + lax.broadcasted_iota(jnp.int32, (n_chunk, tq), 1)
    m_c = (ci < n_cmp) & (ci * CMP_STRIDE + (CMP_BLOCK - 1) <= pos_c)
    kc = kc_ref[0, 0]
    vct = vct_ref[0, 0]
    o_c = []
    psum = None
    for h in range(hpg):
        s = jnp.where(m_c, jnp.dot(kc, qt_ref[0, h], preferred_element_type=F32), NEG)
        e = jnp.where(m_c, jnp.exp2(s - jnp.max(s, axis=0, keepdims=True)), 0.0)
        p = e * _safe_inv(jnp.sum(e, axis=0, keepdims=True))
        o_c.append(jnp.dot(vct, p.astype(BF16), preferred_element_type=F32))
        psum = p if psum is None else psum + p

    imp = pltpu.roll(psum, 1, axis=0) + psum
    for o in range(1, r):
        imp = imp + pltpu.roll(psum, n_chunk - o, axis=0)
    parts = []
    for i, rk_ref in enumerate(rk_refs):
        rk_ref[...] = imp[:, i * LANES:(i + 1) * LANES]
        parts.append(rk_ref[pl.ds(0, n_row, stride=r), :])
    imp_b = jnp.concatenate(parts, axis=1)
    j_io = lax.broadcasted_iota(jnp.int32, (n_row, tq), 0)
    pos_b = q0 + lax.broadcasted_iota(jnp.int32, (n_row, tq), 1)
    valid = (j_io < n_slc) & (j_io * SLC_BLOCK <= pos_b)
    forced = (j_io == 0) | (j_io == pos_b // SLC_BLOCK)
    sc = jnp.where(valid, imp_b + jnp.where(forced, FORCE, 0.0), NEG)
    rank = jnp.zeros((n_row, tq), F32)
    for k in range(n_slc):
        ck = sc[k:k + 1, :]
        beats = (ck > sc) | ((ck == sc) & (j_io > k))
        rank = rank + jnp.where(beats, 1.0, 0.0)
    selb_ref[...] = jnp.where(rank < n_sel, 0.0, NEG)

    def key_minus_query(n_keys):
        return (lax.broadcasted_iota(jnp.int32, (n_keys, tq), 0)
                - lax.broadcasted_iota(jnp.int32, (n_keys, tq), 1))

    def start(m_ref, acc_ref):
        m_ref[...] = jnp.full(m_ref.shape, M_INIT, F32)
        acc_ref[...] = jnp.zeros(acc_ref.shape, F32)

    def scores(kk, bias, s_ref):
        for h in range(hpg):
            for c in range(kk.shape[0] // SLC_BLOCK):
                rs = slice(c * SLC_BLOCK, (c + 1) * SLC_BLOCK)
                s_ref[h, rs, :] = jnp.dot(kk[rs], qt_ref[0, h], preferred_element_type=F32) + bias[rs]

    def softmax_values(s_ref, vt, m_ref, acc_ref):
        for h in range(hpg):
            ps, alphas = [], []
            for c in range(tq // LANES):
                cs = slice(c * LANES, (c + 1) * LANES)
                s = s_ref[h, :, cs]
                m_prev = m_ref[h, :, cs]
                m_new = jnp.maximum(m_prev, jnp.max(s, axis=0, keepdims=True))
                ps.append(jnp.exp2(s - m_new).astype(BF16))
                alphas.append(jnp.exp2(m_prev - m_new))
                m_ref[h, :, cs] = m_new
            acc_ref[h] = (jnp.concatenate(alphas, axis=1) * acc_ref[h]
                          + jnp.dot(vt, jnp.concatenate(ps, axis=1), preferred_element_type=F32))

    def finish(acc_ref):
        return [acc_ref[h, 0:dh, :] * _safe_inv(acc_ref[h, dh:dh + 1, :]) for h in range(hpg)]

    def window_attention():
        tkw = sw_refs[0].shape[1]
        base = jnp.maximum(q0 - WINDOW, 0)
        dlt = key_minus_query(tkw)
        start(mw_ref, accw_ref)
        tiles = [pl.multiple_of(base + i * tkw, LANES) for i in range(2)]
        for i, k0 in enumerate(tiles):
            d = dlt + (k0 - q0)
            bias = jnp.where((d <= 0) & (d >= -WINDOW), 0.0, NEG)
            scores(kw_ref[0, 0, pl.ds(k0, tkw), :], bias, sw_refs[i])
        for i, k0 in enumerate(tiles):
            softmax_values(sw_refs[i], vwt_ref[0, :, pl.ds(k0, tkw)], mw_ref, accw_ref)

    def selected_attention(n_tiles):
        blocks_per_tile = tk // SLC_BLOCK
        start(m_ref, acc_ref)
        for i in range(n_tiles):
            keys = slice(i * tk, (i + 1) * tk)
            rows = [jnp.broadcast_to(selb_ref[i * blocks_per_tile + j:i * blocks_per_tile + j + 1, :], (SLC_BLOCK, tq))
                    for j in range(blocks_per_tile)]
            bias = jnp.concatenate(rows, axis=0)
            if i == n_tiles - 1:
                bias = jnp.where(key_minus_query(tk) <= 0, bias, NEG)
            scores(ks_ref[0, 0, keys, :], bias, s_refs[i % 2])
            softmax_values(s_refs[i % 2], vst_ref[0, :, keys], m_ref, acc_ref)

    window_attention()
    lax.switch(qi, [functools.partial(selected_attention, n + 1) for n in range(n_qblocks)])
    o_s = finish(acc_ref)
    o_w = finish(accw_ref)

    gt = gate_ref[0].T
    outs = [gt[h:h + 1] * o_c[h] + gt[hpg + h:hpg + h + 1] * o_s[h] + gt[2 * hpg + h:2 * hpg + h + 1] * o_w[h]
            for h in range(hpg)]
    o_ref[...] = jnp.concatenate(outs, axis=0).T.astype(BF16)


def _nsa_prompt(q_t, ks_hm, vs_t, kw_hm, vw_t, kcmp, vcmp_t, gates_hm, bsz, t):
    g, hpg, dh, tq = NSA_KV_HEADS, NSA_HPG, NSA_HEAD_DIM, NSA_TQ
    r = SLC_BLOCK // CMP_STRIDE
    n_slc = t // SLC_BLOCK
    assert t % tq == 0 and n_slc * r <= LANES and NSA_TK % SLC_BLOCK == 0 and NSA_TK == tq
    nq = t // tq
    n_cmp = kcmp.shape[2] - 1
    if kcmp.shape[2] < LANES:
        fill = LANES - kcmp.shape[2]
        kcmp = jnp.pad(kcmp, ((0, 0), (0, 0), (0, fill), (0, 0)))
        vcmp_t = jnp.pad(vcmp_t, ((0, 0), (0, 0), (0, 0), (0, fill)))
    n_chunk = kcmp.shape[2]
    assert n_chunk == LANES
    q4 = q_t.reshape(g, hpg, dh, bsz * t)

    def k_spec():
        return pl.BlockSpec((1, 1, t, dh), lambda b, gg, qi: (gg, b, 0, 0))

    def vt_spec():
        return pl.BlockSpec((1, VT_ROWS, t), lambda b, gg, qi: (gg, 0, b))

    def per_bt(a):
        return a.reshape(g, bsz, t, a.shape[-1])

    def flash_scratch(n_keys):
        return [pltpu.VMEM((hpg, 1, tq), F32), pltpu.VMEM((hpg, VT_ROWS, tq), F32),
                pltpu.VMEM((hpg, n_keys, tq), F32), pltpu.VMEM((hpg, n_keys, tq), F32)]

    tk_win = (WINDOW + tq) // 2
    assert tk_win % LANES == 0 and t >= WINDOW + tq

    return pl.pallas_call(
        functools.partial(_nsa_prompt_kernel, n_cmp=n_cmp, n_slc=n_slc, n_sel=min(N_SELECT, n_slc), n_qblocks=nq),
        grid=(bsz, g, nq),
        in_specs=[pl.BlockSpec((1, hpg, dh, tq), lambda b, gg, qi: (gg, 0, 0, b * nq + qi)),
                  pl.BlockSpec((1, 1, n_chunk, dh), lambda b, gg, qi: (b, gg, 0, 0)),
                  pl.BlockSpec((1, 1, dh, n_chunk), lambda b, gg, qi: (b, gg, 0, 0)),
                  k_spec(), vt_spec(), k_spec(), vt_spec(),
                  pl.BlockSpec((1, tq, LANES), lambda b, gg, qi: (gg, b * nq + qi, 0))],
        out_specs=pl.BlockSpec((tq, hpg * dh), lambda b, gg, qi: (b * nq + qi, gg)),
        out_shape=SDS((bsz * t, NSA_Q), BF16),
        scratch_shapes=([pltpu.VMEM((n_chunk, LANES), F32)] * (tq // LANES)
                        + [pltpu.VMEM((n_chunk // r, tq), F32)]
                        + flash_scratch(NSA_TK) + flash_scratch(tk_win)),
        compiler_params=_params("parallel", "parallel", "arbitrary"),
    )(q4, kcmp, vcmp_t, per_bt(ks_hm), vs_t, per_bt(kw_hm), vw_t, gates_hm)


def _gla_prompt_kernel(q_ref, k_ref, v_ref, r_ref, misc_ref, wa_ref, ba_ref, ng_ref,
                       o_ref, s_ref, qe_ref, qt_ref, kt_ref, kh_ref, d_ref, u_ref):
    t = q_ref.shape[0]
    c = GLA_CHUNK
    n = t // c
    dk, dv = GLA_DK, GLA_DV
    lr = misc_ref[:, N_GATE:N_GATE + GLA_RANK].astype(BF16)
    x = jnp.dot(lr, wa_ref[...], preferred_element_type=F32) + ba_ref[...]
    g = jax.nn.log_sigmoid(x) / GLA_TAU
    row = lax.broadcasted_iota(jnp.int32, (t, dk), 0) % c
    b = g
    sh = 1
    while sh < c:
        b = b + jnp.where(row >= sh, pltpu.roll(b, sh, axis=0), 0.0)
        sh *= 2
    b3 = b.reshape(n, c, dk)
    b_last = jnp.broadcast_to(b3[:, c - 1:c, :], (n, c, dk)).reshape(t, dk)
    b_mid = jnp.broadcast_to(b3[:, c // 2 - 1:c // 2, :], (n, c, dk)).reshape(t, dk)
    q = q_ref[...] * (GLA_DK ** -0.5)
    k = k_ref[...]
    qe_ref[...] = (q * jnp.exp(b)).astype(BF16)
    qt_ref[...] = (q * jnp.exp(b - b_mid)).astype(BF16)
    kt_ref[...] = (k * jnp.exp(b_mid - b)).astype(BF16)
    kh_ref[...] = (k * jnp.exp(b_last - b)).astype(BF16)
    d_ref[...] = jnp.exp(b_last)
    tn = (((0,), (0,)), ((), ()))
    nt = (((1,), (1,)), ((), ()))

    def chunk_update(i, carry):
        r0 = pl.multiple_of(i * c, c)
        u_ref[i] = lax.dot_general(v_ref[pl.ds(r0, c), :], kh_ref[pl.ds(r0, c), :], tn,
                                   preferred_element_type=F32)
        return carry

    lax.fori_loop(0, n, chunk_update, 0, unroll=GLA_UNROLL)

    def recur(i, st):
        nxt = st * d_ref[pl.ds(pl.multiple_of(i * c, c), 1), :] + u_ref[i]
        u_ref[i] = st
        return nxt

    s_ref[0, 0] = lax.fori_loop(0, n, recur, jnp.zeros((dv, dk), F32)).T

    grp = GLA_GROUP
    rows = grp * c
    ri =lax.broadcasted_iota(jnp.int32, (rows, rows), 0)
    ci = lax.broadcasted_iota(jnp.int32, (rows, rows), 1)
    keep = (ri // c == ci // c) & (ri >= ci)
    ng = ng_ref[...]

    def group_out(i, carry):
        r0 = pl.multiple_of(i * rows, rows)
        rs = pl.ds(r0, rows)
        a = lax.dot_general(qt_ref[rs, :], kt_ref[rs, :], nt, preferred_element_type=F32)
        a = jnp.where(keep, a, 0.0).astype(BF16)
        o = jnp.dot(a, v_ref[rs, :], preferred_element_type=F32)
        inter = [lax.dot_general(qe_ref[pl.ds(r0 + j * c, c), :], u_ref[i * grp + j].astype(BF16), nt,
                                 preferred_element_type=F32) for j in range(grp)]
        o = o + jnp.concatenate(inter, axis=0)
        o = _rms(o, ng) * jax.nn.silu(r_ref[rs, :])
        o_ref[rs, :] = o.astype(BF16)
        return carry

    lax.fori_loop(0, n // grp, group_out, 0, unroll=16)


def _gla_prompt(q_l, k_l, v_l, r_l, misc, wa, ba, ng, bsz, t):
    h, dk, dv = GLA_HEADS, GLA_DK, GLA_DV
    assert t % (GLA_CHUNK * GLA_GROUP * 2) == 0
    n = t // GLA_CHUNK
    return pl.pallas_call(
        _gla_prompt_kernel,
        grid=(bsz, h),
        in_specs=[pl.BlockSpec((t, dk), lambda b, hh: (b, hh)), pl.BlockSpec((t, dk), lambda b, hh: (b, hh)),
                  pl.BlockSpec((t, dv), lambda b, hh: (b, hh)), pl.BlockSpec((t, dv), lambda b, hh: (b, hh)),
                  pl.BlockSpec((t, LANES), lambda b, hh: (b, 0)),
                  pl.BlockSpec((GLA_RANK, dk), lambda b, hh: (0, hh)), pl.BlockSpec((1, dk), lambda b, hh: (0, hh)),
                  _resident((1, dv))],
        out_specs=[pl.BlockSpec((t, dv), lambda b, hh: (b, hh)),
                   pl.BlockSpec((1, 1, dk, dv), lambda b, hh: (b, hh, 0, 0))],
        out_shape=[SDS((bsz * t, h * dv), BF16), SDS((bsz, h, dk, dv), F32)],
        scratch_shapes=[pltpu.VMEM((t, dk), BF16), pltpu.VMEM((t, dk), BF16), pltpu.VMEM((t, dk), BF16),
                        pltpu.VMEM((t, dk), BF16), pltpu.VMEM((t, dk), F32),
                        pltpu.VMEM((n, dv, dk), F32)],
        compiler_params=_params("parallel", "parallel"),
    )(q_l, k_l, v_l, r_l, misc, wa, ba, ng)


def _merge_rows(oa, ob, x, g_ref, wg_ref, wa_ref, wb_ref, wo_ref):
    h = _rms(x, g_ref[...]).astype(BF16)
    d = D_MODEL
    ga = jax.nn.sigmoid(lax.dot_general(h, wg_ref[0:d, :], _NT, preferred_element_type=F32))
    u = ga * jnp.dot(oa, wa_ref[...], preferred_element_type=F32)
    gb = jax.nn.sigmoid(lax.dot_general(h, wg_ref[d:2 * d, :], _NT, preferred_element_type=F32))
    u = u + gb * jnp.dot(ob, wb_ref[...], preferred_element_type=F32)
    return x + jnp.dot(u.astype(BF16), wo_ref[...], preferred_element_type=F32)


def _merge_kernel(oa_ref, ob_ref, x_ref, oas_ref, obs_ref, xs_ref, g_ref, wg_ref, wa_ref, wb_ref, wo_ref,
                  x1_ref, x1s_ref):
    weights = (g_ref, wg_ref, wa_ref, wb_ref, wo_ref)
    x1_ref[...] = _merge_rows(oa_ref[...], ob_ref[...], x_ref[...], *weights)

    @pl.when(pl.program_id(0) == 0)
    def _():
        x1s_ref[...] = _merge_rows(oas_ref[...], obs_ref[...], xs_ref[...], *weights)


def _merge(o_a, o_b, x, oa_s, ob_s, x_s, norm_g, w_t, gate_row0, w_a, w_b, w_o, tm):
    m, d = x.shape
    assert m % tm == 0
    row_specs = [pl.BlockSpec((tm, NSA_Q), lambda i: (i, 0)), pl.BlockSpec((tm, GLA_HEADS * GLA_DV), lambda i: (i, 0)),
                 pl.BlockSpec((tm, d), lambda i: (i, 0))]
    return pl.pallas_call(
        _merge_kernel, grid=(m // tm,),
        in_specs=(row_specs + [_resident(a.shape) for a in (oa_s, ob_s, x_s, norm_g)]
                  + [_row_window(gate_row0, 2 * d, d)] + [_resident(a.shape) for a in (w_a, w_b, w_o)]),
        out_specs=[pl.BlockSpec((tm, d), lambda i: (i, 0)), pl.BlockSpec(x_s.shape, lambda i: (0, 0))],
        out_shape=[SDS((m, d), F32), SDS(x_s.shape, F32)],
        compiler_params=_params("arbitrary"),
    )(o_a, o_b, x, oa_s, ob_s, x_s, norm_g, w_t, w_a, w_b, w_o)


def _mlp_kernel(x1_ref, x1s_ref, g2_ref, wu_ref, wd_ref, gf_ref, y_ref, ys_ref, h_ref):
    i = pl.program_id(0)
    j = pl.program_id(1)
    last = pl.num_programs(1) - 1
    tm = x1_ref.shape[0]

    @pl.when(j == 0)
    def _():
        x1 = x1_ref[...]
        h_ref[0:tm, :] = _rms(x1, g2_ref[...]).astype(BF16)
        y_ref[...] = x1

    @pl.when((j == 0) & (i == 0))
    def _():
        x1s = x1s_ref[...]
        h_ref[tm:, :] = _rms(x1s, g2_ref[...]).astype(BF16)
        ys_ref[...] = x1s

    up = jnp.maximum(jnp.dot(h_ref[...], wu_ref[...].astype(BF16), preferred_element_type=F32), 0.0)
    res = jnp.dot((up * up).astype(BF16), wd_ref[...].astype(BF16), preferred_element_type=F32)
    y_ref[...] += res[0:tm]

    @pl.when(i == 0)
    def _():
        ys_ref[...] += res[tm:]

    @pl.when(j == last)
    def _():
        y_ref[...] = _rms(y_ref[...], gf_ref[...])

    @pl.when((j == last) & (i == 0))
    def _():
        ys_ref[...] = _rms(ys_ref[...], gf_ref[...])


def _mlp(x1, x1_s, g2, w_up, w_down, gf, tm, tf):
    m, d = x1.shape
    ff = w_up.shape[1]
    assert m % tm == 0 and ff % tf == 0
    small = pl.BlockSpec(x1_s.shape, lambda i, j: (0, 0))
    return pl.pallas_call(
        _mlp_kernel, grid=(m // tm, ff // tf),
        in_specs=[pl.BlockSpec((tm, d), lambda i, j: (i, 0)), small, _resident((1, d)),
                  pl.BlockSpec((d, tf), lambda i, j: (0, j)), pl.BlockSpec((tf, d), lambda i, j: (j, 0)),
                  _resident((1, d))],
        out_specs=[pl.BlockSpec((tm, d), lambda i, j: (i, 0)), small],
        out_shape=[SDS((m, d), F32), SDS(x1_s.shape, F32)],
        scratch_shapes=[pltpu.VMEM((tm + x1_s.shape[0], d), BF16)],
        compiler_params=_params("arbitrary", "arbitrary", vmem_limit_bytes=MLP_VMEM_LIMIT_BYTES),
    )(x1, x1_s, g2, w_up, w_down, gf)


def _prep_weights(norm1_g, w_in, b_nsa_gate, cmp_pe_k, cmp_pe_v, cmp_k_w1, cmp_k_w2, cmp_v_w1, cmp_v_w2,
                  gla_w_a2, gla_b_a, gla_norm_g, w_br_a, w_br_b, w_o, norm2_g, w_up, w_down, norm_f):
    pts = [0]
    for s in SPLITS:
        pts.append(pts[-1] + s)
    c_q, c_kv, c_g, c_ql, c_kl, c_vl, c_rl, c_lr, c_br, c_end = pts
    gcols = jnp.asarray([c_g + (g * NSA_HPG + h) * 3 + c for g in range(NSA_KV_HEADS)
                         for c in range(3) for h in range(NSA_HPG)], jnp.int32)
    w_t = w_in.T.astype(BF16)
    pad = jnp.zeros((LANES - N_GATE - GLA_RANK, D_MODEL), BF16)
    w_misc_t = jnp.concatenate([w_t[gcols], w_t[c_lr:c_br], pad], axis=0)
    b_misc = jnp.concatenate([b_nsa_gate[gcols - c_g], jnp.zeros((LANES - N_GATE,), F32)])[None, :]
    w = dict(
        norm1=norm1_g[None, :], norm2=norm2_g[None, :], norm_f=norm_f[None, :],
        w_t=w_t, w_misc_t=w_misc_t, row_a=c_q, rows_a=c_g - c_q, row_b=c_ql, rows_b=c_lr - c_ql, row_gate=c_br,
        b_misc=b_misc,
        gla_wa=gla_w_a2.astype(BF16), gla_ba=gla_b_a[None, :], gla_ng=gla_norm_g[None, :],
        w_br_a=w_br_a.astype(BF16), w_br_b=w_br_b.astype(BF16), w_o=w_o.astype(BF16),
        w_up=w_up, w_down=w_down,
    )
    cw = {}
    cw['pe_k'], cw['w1_k'], cw['w2_k'] = _cmp_weights(cmp_pe_k, cmp_k_w1, cmp_k_w2)
    cw['pe_v'], cw['w1_v'], cw['w2_v'] = _cmp_weights(cmp_pe_v, cmp_v_w1, cmp_v_w2)
    w['cmp'] = cw
    return w


_GQ =GLA_HEADS * GLA_DK
_GV = GLA_HEADS * GLA_DV


def _layer_prompt(x, w):
    bsz, t, d = x.shape
    m = bsz * t
    x2 = x.reshape(m, d)
    zb = jnp.zeros((1, LANES), F32)
    spec_a = [('hmt', 0, NSA_Q, NSA_HEAD_DIM ** -0.5 * LOG2_E)]
    for j in range(6):
        cols = (NSA_Q + j * NSA_KV, NSA_Q + (j + 1) * NSA_KV, 1.0)
        spec_a.append(('f32t',) + cols)
        if j >= 2:
            spec_a.append(('hm' if j % 2 == 0 else 'hmvt',) + cols)
    q_t, r0, r1, r2, ks_hm, r3, vs_t, r4, kw_hm, r5, vw_t = _proj(
        x2, w['norm1'], w['w_t'], w['row_a'], w['rows_a'], w['w_misc_t'], zb, spec_a, PROJ_TM, seq_len=t)
    spec_b = [('f32', 0, _GQ, 1.0), ('f32', _GQ, 2 * _GQ, 1.0), ('bf16', 2 * _GQ, 2 * _GQ + _GV, 1.0),
              ('f32', 2 * _GQ + _GV, 2 * _GQ + 2 * _GV, 1.0),
              ('misc', 2 * _GQ + 2 * _GV, 2 * _GQ + 2 * _GV + LANES, 1.0), ('gates', 0, 0, 1.0)]
    q_l, k_l, v_l, r_l, misc, gates_hm = _proj(x2, w['norm1'], w['w_t'], w['row_b'], w['rows_b'], w['w_misc_t'],
                                               w['b_misc'], spec_b, PROJ_TM)

    n_pages = t // PAGE_SIZE
    ident = jnp.broadcast_to(jnp.arange(n_pages, dtype=jnp.int32), (bsz, n_pages))
    kcmp, vcmp = _compress(ident, r0, r1, w['cmp'], paged=False)
    o_a = _nsa_prompt(q_t, ks_hm, vs_t, kw_hm, vw_t, kcmp, vcmp, gates_hm, bsz, t)
    o_b, s_new = _gla_prompt(q_l, k_l, v_l, r_l, misc, w['gla_wa'], w['gla_ba'], w['gla_ng'], bsz, t)
    def token_major(a):
        return a.reshape(bsz, NSA_KV_HEADS, NSA_HEAD_DIM, a.shape[-1]).transpose(0, 3, 1, 2)

    rows = [token_major(a) for a in (r0, r1, r2, r3)]
    n_keep = min(WINDOW, t)
    wins = [token_major(a[:, :, t - n_keep:]) for a in (r4, r5)]
    return (o_a, o_b, x2), rows, wins, s_new


def _channel_mix(prompt, sample, w):
    (oa_p, ob_p, x_p), (oa_s, ob_s, x_s) = prompt, sample
    m = x_p.shape[0]
    x1_p, x1_s = _merge(oa_p, ob_p, x_p, oa_s, ob_s, x_s, w['norm1'], w['w_t'], w['row_gate'], w['w_br_a'],
                        w['w_br_b'], w['w_o'], MERGE_TM)
    tm = next(c for c in (MLP_TM, PROJ_TM, MERGE_TM) if m % c == 0)
    return _mlp(x1_p, x1_s, w['norm2'], w['w_up'], w['w_down'], w['norm_f'], tm, MLP_TF)


def _group_rows(parts):
    rowg = lax.broadcasted_iota(jnp.int32, parts[0].shape, 0) // NSA_HPG
    out = parts[0]
    for g in range(1, NSA_KV_HEADS):
        out = jnp.where(rowg == g, parts[g], out)
    return out


def _sample_select_kernel(q_ref, kc_ref, vc_ref, oc_ref, idx_ref, *, n_cmp, pos, n_pick):
    bsz, nh = q_ref.shape[0], q_ref.shape[1]
    nt = (((1,), (1,)), ((), ()))
    n_chunk = kc_ref.shape[2]
    s = jnp.concatenate(
        [_group_rows([lax.dot_general(q_ref[b], kc_ref[b, g], nt, preferred_element_type=F32)
                      for g in range(NSA_KV_HEADS)]) for b in range(bsz)], axis=0)
    lane = lax.broadcasted_iota(jnp.int32, s.shape, 1)
    mask = (lane < n_cmp) & (lane * CMP_STRIDE + (CMP_BLOCK - 1) <= pos)
    e, l = _softmax_rows(s, mask)
    p = e * _safe_inv(l)
    pb = p.astype(BF16)
    for b in range(bsz):
        oc_ref[b] = _group_rows([lax.dot_general(pb[b * nh:(b + 1) * nh], vc_ref[b, g], nt,
                                                 preferred_element_type=F32) for g in range(NSA_KV_HEADS)])
    nr = p.shape[0]
    y = p + pltpu.roll(p, nr - 1, axis=0)
    psum = y + pltpu.roll(y, nr - 2, axis=0)
    imp = pltpu.roll(psum, 1, axis=1) + psum
    r = SLC_BLOCK // CMP_STRIDE
    for o in range(1, r):
        imp = imp + pltpu.roll(psum, n_chunk - o, axis=1)
    blk = lane // r
    is_blk = lane % r == 0
    valid = blk * SLC_BLOCK <= pos
    forced = (blk == 0) | (blk == pos // SLC_BLOCK)
    score = jnp.where(is_blk & valid, imp + jnp.where(forced, FORCE, 0.0), NEG)
    lane_f = lane.astype(F32)
    out_lane = lax.broadcasted_iota(jnp.int32, (nr, LANES), 1)
    picked = jnp.zeros((nr, LANES), F32)
    for k in range(n_pick):
        mx = jnp.max(score, axis=-1, keepdims=True)
        ix = jnp.min(jnp.where(score == mx, lane_f, float(n_chunk)), axis=-1, keepdims=True)
        picked = jnp.where(out_lane == k, ix, picked)
        score = jnp.where(lane_f == ix, 2.0 * NEG, score)
    idx_ref[...] = (picked.astype(jnp.int32) // r).reshape(idx_ref.shape)


def _sample_select(q_s, kcmp, vcmp, pos, n_pick):
    bsz, g, n_chunk, dh = kcmp.shape
    nh = NSA_HEADS
    return pl.pallas_call(
        functools.partial(_sample_select_kernel, n_cmp=n_chunk - 1, pos=pos, n_pick=n_pick),
        grid=(1,),
        in_specs=[pl.BlockSpec((bsz, nh, dh), lambda i: (0, 0, 0)),
                  pl.BlockSpec(kcmp.shape, lambda i: (0, 0, 0, 0)), pl.BlockSpec(vcmp.shape, lambda i: (0, 0, 0, 0))],
        out_specs=[pl.BlockSpec((bsz, nh, dh), lambda i: (0, 0, 0)), pl.BlockSpec((bsz, nh, LANES), lambda i: (0, 0, 0))],
        out_shape=[SDS((bsz, nh, dh), F32), SDS((bsz, nh, LANES), jnp.int32)],
        compiler_params=_params("arbitrary"),
    )(q_s, kcmp, vcmp)


def _sample_attend_kernel(idx_ref, pt_ref, q_ref, oc_ref, ks_hbm, vs_hbm, kw_ref, vw_ref,
                          nks_ref, nvs_ref, nkw_ref, nvw_ref, gate_ref, o_ref, ksel, vsel, sem, *, n_pick):
    b = pl.program_id(0)
    g_n, dh = NSA_KV_HEADS, NSA_HEAD_DIM
    half = PAGE_SIZE // SLC_BLOCK

    def copies(bb, slot):
        out = []
        for g in range(g_n):
            for r in range(n_pick):
                page = pt_ref[bb, idx_ref[bb * g_n + g, r] // half]
                out.append(pltpu.make_async_copy(ks_hbm.at[page, g], ksel.at[slot, g, r], sem.at[slot, 0]))
                out.append(pltpu.make_async_copy(vs_hbm.at[page, g], vsel.at[slot, g, r], sem.at[slot, 1]))
        return out

    slot = b % 2

    @pl.when(b == 0)
    def _():
        for c in copies(b, slot):
            c.start()

    @pl.when(b + 1 < pl.num_programs(0))
    def _():
        for c in copies(b + 1, 1 - slot):
            c.start()

    q = q_ref[0]
    qf = q.astype(F32)
    nt = (((1,), (1,)), ((), ()))

    def attend(keys_t, vals_t, bias, k_new, v_new):
        s = _group_rows([jnp.dot(q, keys_t[g].astype(BF16), preferred_element_type=F32)
                         + (0.0 if bias is None else bias[g]) for g in range(g_n)])
        s_new = _group_rows([jnp.sum(qf * k_new[:, g * dh:(g + 1) * dh], axis=-1, keepdims=True)
                             for g in range(g_n)])
        m = jnp.maximum(jnp.max(s, axis=-1, keepdims=True), s_new)
        e = jnp.exp(s - m)
        e_new = jnp.exp(s_new - m)
        l = jnp.sum(e, axis=-1, keepdims=True) + e_new
        eb = e.astype(BF16)
        acc = _group_rows([lax.dot_general(eb, vals_t[g].astype(BF16), nt, preferred_element_type=F32)
                           + e_new * v_new[:, g * dh:(g + 1) * dh] for g in range(g_n)])
        return acc / l

    o_w = attend([kw_ref[0, g] for g in range(g_n)], [vw_ref[0, g] for g in range(g_n)], None,
                 nkw_ref[0], nvw_ref[0])
    for c in copies(b, slot):
        c.wait()
    lin = lax.broadcasted_iota(jnp.int32, (1, PAGE_SIZE), 1)
    bias = []
    for g in range(g_n):
        parts = []
        for r in range(n_pick):
            off = (idx_ref[b * g_n + g, r] % half) * SLC_BLOCK
            parts.append(jnp.where((lin >= off) & (lin < off + SLC_BLOCK), 0.0, NEG))
        bias.append(jnp.concatenate(parts, axis=1))

    def tiles(buf, g):
        return jnp.concatenate([buf[slot, g, r] for r in range(n_pick)], axis=1)

    o_s = attend([tiles(ksel, g) for g in range(g_n)], [tiles(vsel, g) for g in range(g_n)], bias,
                 nks_ref[0], nvs_ref[0])

    gt = jnp.broadcast_to(gate_ref[0], (LANES, LANES)).T
    nh = NSA_HEADS
    o_ref[0] = (gt[0:nh, 0:dh] * oc_ref[0] + gt[nh:2 * nh, 0:dh] * o_s + gt[2 * nh:3 * nh, 0:dh] * o_w)


def _sample_attend(idx, page_table, q_s, o_c, slc_k, slc_v, win_k, win_v, new_rows, gates, n_pick):
    bsz = page_table.shape[0]
    nh, dh, g = NSA_HEADS, NSA_HEAD_DIM, NSA_KV_HEADS
    wlen = win_k.shape[-1]
    row_spec = pl.BlockSpec((1, 1, NSA_KV), lambda b, *_: (b, 0, 0))
    win_spec = pl.BlockSpec((1, g, dh, wlen), lambda b, *_: (b, 0, 0, 0))
    head_spec = pl.BlockSpec((1, nh, dh), lambda b, *_: (b, 0, 0))
    any_spec = pl.BlockSpec(memory_space=pl.ANY)
    grid_spec = pltpu.PrefetchScalarGridSpec(
        num_scalar_prefetch=2, grid=(bsz,),
        in_specs=[head_spec, head_spec, any_spec, any_spec, win_spec, win_spec,
                  row_spec, row_spec, row_spec, row_spec, pl.BlockSpec((1, 1, LANES), lambda b, *_: (b, 0, 0))],
        out_specs=head_spec,
        scratch_shapes=[pltpu.VMEM((2, g, n_pick, dh, PAGE_SIZE), F32), pltpu.VMEM((2, g, n_pick, dh, PAGE_SIZE), F32),
                        pltpu.SemaphoreType.DMA((2, 2))])
    return pl.pallas_call(
        functools.partial(_sample_attend_kernel, n_pick=n_pick),
        grid_spec=grid_spec, out_shape=SDS((bsz, nh, dh), F32),
        compiler_params=_params("arbitrary"),
    )(idx, page_table, q_s, o_c, slc_k, slc_v, win_k, win_v, *new_rows, gates)


def _gla_sample_kernel(q_ref, k_ref, v_ref, r_ref, misc_ref, wa_ref, ba_ref, ng_ref, s0_ref, o_ref, s_ref):
    dk, dv = GLA_DK, GLA_DV
    lr = jnp.broadcast_to(misc_ref[0][:, N_GATE:N_GATE + GLA_RANK], (16, GLA_RANK)).astype(BF16)
    x = jnp.dot(lr, wa_ref[...], preferred_element_type=F32)[0:1] + ba_ref[...]
    g_all = jax.nn.log_sigmoid(x) / GLA_TAU

    def col(v):
        t = jnp.broadcast_to(v, (dk, dk)).T
        return jnp.concatenate([t] * (dv // dk), axis=1)

    outs = []
    for h in range(GLA_HEADS):
        g = g_all[:, h * dk:(h + 1) * dk]
        q = q_ref[0][:, h * dk:(h + 1) * dk] * (GLA_DK ** -0.5)
        k = k_ref[0][:, h * dk:(h + 1) * dk]
        v = v_ref[0][:, h * dv:(h + 1) * dv]
        s0 = s0_ref[0, h]
        q_t = q * jnp.exp(g)
        k_t = k * jnp.exp(-g)
        a = jnp.sum(q_t * k_t, axis=-1, keepdims=True)
        o = jnp.sum(col(q_t) * s0, axis=0, keepdims=True) + a * v
        s_ref[0, h] = col(jnp.exp(g)) * s0 + col(k) * v
        outs.append(_rms(o, ng_ref[...]) * jax.nn.silu(r_ref[0][:, h * dv:(h + 1) * dv]))
    o_ref[0] = jnp.concatenate(outs, axis=-1)


def _gla_sample(q_l, k_l, v_l, r_l, misc, wa, ba, ng, s0):
    bsz, h, dk, dv = s0.shape

    def row(n):
        return pl.BlockSpec((1, 1, n), lambda b: (b, 0, 0))

    st_spec = pl.BlockSpec((1, h, dk, dv), lambda b: (b, 0, 0, 0))
    return pl.pallas_call(
        _gla_sample_kernel, grid=(bsz,),
        in_specs=[row(h * dk), row(h * dk), row(h * dv), row(h * dv), row(LANES),
                  _resident(wa.shape), _resident(ba.shape), _resident(ng.shape), st_spec],
        out_specs=[row(h * dv), st_spec],
        out_shape=[SDS((bsz, 1, h * dv), F32), SDS((bsz, h, dk, dv), F32)],
        compiler_params=_params("parallel"),
    )(q_l, k_l, v_l, r_l, misc, wa, ba, ng, s0)


SAMPLE_ROWS = 16


def _layer_sample(x, caches, wins, s0, page_table, w):
    bsz, t, d = x.shape
    n_pages = page_table.shape[1]
    pos = n_pages * PAGE_SIZE
    assert t == 1 and bsz <= SAMPLE_ROWS and pos % SLC_BLOCK == 0 and wins[0].shape[1] == WINDOW
    assert pos // SLC_BLOCK >= N_SELECT
    mp = SAMPLE_ROWS
    x2 = jnp.pad(x.reshape(bsz, d), ((0, mp - bsz), (0, 0)))
    zb = jnp.zeros((1, LANES), F32)
    spec_a = [('bf16', 0, NSA_Q, NSA_HEAD_DIM ** -0.5)]
    spec_a += [('f32', NSA_Q + j * NSA_KV, NSA_Q + (j + 1) * NSA_KV, 1.0) for j in range(6)]
    q_s, r0, r1, r2, r3, r4, r5 = _proj(x2, w['norm1'], w['w_t'], w['row_a'], w['rows_a'], w['w_misc_t'], zb,
                                        spec_a, mp)
    spec_b = [('f32', 0, _GQ, 1.0), ('f32', _GQ, 2 * _GQ, 1.0), ('f32', 2 * _GQ, 2 * _GQ + _GV, 1.0),
              ('f32', 2 * _GQ + _GV, 2 * _GQ + 2 * _GV, 1.0),
              ('misc', 2 * _GQ + 2 * _GV, 2 * _GQ + 2 * _GV + LANES, 1.0)]
    q_l, k_l, v_l, r_l, misc = _proj(x2, w['norm1'], w['w_t'], w['row_b'], w['rows_b'], w['w_misc_t'], w['b_misc'],
                                     spec_b, mp)

    cache_t = [a.transpose(0, 2, 3, 1) for a in caches]
    win_t = [a.transpose(0, 2, 3, 1) for a in wins]
    n_pool = caches[0].shape[0]
    cmp_pages = [a.reshape(n_pool, NSA_KV, PAGE_SIZE) for a in cache_t[:2]]
    kcmp, vcmp = _compress(page_table, cmp_pages[0], cmp_pages[1], w['cmp'], paged=True)
    q_h = q_s.reshape(mp, NSA_HEADS, NSA_HEAD_DIM)
    n_pick = N_SELECT - 1
    o_c, idx = _sample_select(q_h, kcmp, vcmp, pos, n_pick)
    idx = idx[:, ::NSA_HPG, :n_pick].reshape(bsz * NSA_KV_HEADS, n_pick)
    gates = misc[:, :N_GATE].reshape(mp, NSA_KV_HEADS, 3, NSA_HPG).transpose(0, 2, 1, 3).reshape(mp, 1, N_GATE)
    gates = jnp.pad(gates, ((0, 0), (0, 0), (0, LANES - N_GATE)))
    new_rows = [a.reshape(mp, 1, NSA_KV) for a in (r2, r3, r4, r5)]
    o_a = _sample_attend(idx, page_table, q_h, o_c, cache_t[2], cache_t[3], win_t[0], win_t[1], new_rows, gates,
                         n_pick)
    o_a = jnp.pad(o_a.reshape(bsz, NSA_Q), ((0, mp - bsz), (0, 0))).astype(BF16)

    def r3d(a):
        return a.reshape(mp, 1, a.shape[-1])

    o_b, s_new = _gla_sample(r3d(q_l), r3d(k_l), r3d(v_l), r3d(r_l), r3d(misc), w['gla_wa'], w['gla_ba'],
                             w['gla_ng'], s0)
    o_b = jnp.pad(o_b.reshape(bsz, _GV), ((0, mp - bsz), (0, 0))).astype(BF16)
    kvh = (bsz, 1, NSA_KV_HEADS, NSA_HEAD_DIM)
    rows = [a[:bsz].reshape(kvh) for a in (r0, r1, r2, r3)]
    new_wins = [jnp.concatenate([c[:, 1:], a[:bsz].reshape(kvh)], axis=1) for c, a in zip(wins, (r4, r5))]
    return (o_a, o_b, x2), rows, new_wins, s_new


def kernel(x_prompt, x_sample, cache_cmp_k, cache_cmp_v, cache_slc_k, cache_slc_v, cache_win_k, cache_win_v, state_gla, page_table, norm1_g, w_in, b_nsa_gate, cmp_pe_k, cmp_pe_v, cmp_k_w1, cmp_k_w2, cmp_v_w1, cmp_v_w2, gla_w_a2, gla_b_a, gla_norm_g, w_br_a, w_br_b, w_o, norm2_g, w_up, w_down, norm_f):
    assert DEPTH == 1 and norm1_g.shape[0] == 1
    w = _prep_weights(norm1_g[0], w_in[0], b_nsa_gate[0], cmp_pe_k[0], cmp_pe_v[0], cmp_k_w1[0], cmp_k_w2[0],
                      cmp_v_w1[0], cmp_v_w2[0], gla_w_a2[0], gla_b_a[0], gla_norm_g[0], w_br_a[0], w_br_b[0],
                      w_o[0], norm2_g[0], w_up[0], w_down[0], norm_f)
    mix_p, rows_p, wins_p, s_p = _layer_prompt(x_prompt, w)
    caches = [c[0] for c in (cache_cmp_k, cache_cmp_v, cache_slc_k, cache_slc_v)]
    mix_s, rows_s, wins_s, s_s = _layer_sample(x_sample, caches, [cache_win_k[0], cache_win_v[0]], state_gla[0],
                                               page_table, w)
    y_p, y_s = _channel_mix(mix_p, mix_s, w)
    y_p = y_p.reshape(x_prompt.shape)
    y_s = y_s[:x_sample.shape[0]].reshape(x_sample.shape)
    outs_p = [a[None] for a in rows_p + wins_p + [s_p]]
    outs_s = [a[None] for a in rows_s + wins_s + [s_s]]
    return (y_p, y_s, *outs_p, *outs_s)
```

```python
import functools

import jax
import jax.numpy as jnp
from jax import lax
from jax.experimental import pallas as pl
from jax.experimental.pallas import tpu as pltpu

D_MODEL = 2048
DEPTH = 1
PAGE_SIZE = 128
NSA_HEADS = 16
NSA_KV_HEADS = 4
NSA_HPG = NSA_HEADS // NSA_KV_HEADS
NSA_HEAD_DIM = 64
NSA_Q = NSA_HEADS * NSA_HEAD_DIM
NSA_KV = NSA_KV_HEADS * NSA_HEAD_DIM
CMP_STRIDE = 16
CMP_BLOCK = 32
CMP_HIDDEN = 128
SLC_BLOCK = 64
N_SELECT = 16
WINDOW = 512
GLA_HEADS = 4
GLA_DK = (D_MODEL // 4) // GLA_HEADS
GLA_DV = (D_MODEL // 2) // GLA_HEADS
GLA_RANK = 16
GLA_TAU = 16.0
EPS = 1e-6
NEG = -1e30
FORCE = 1e4
SPLITS = (NSA_Q, 6 * NSA_KV, 3 * NSA_HEADS,
          GLA_HEADS * GLA_DK, GLA_HEADS * GLA_DK, GLA_HEADS * GLA_DV, GLA_HEADS * GLA_DV,
          GLA_RANK, 2 * D_MODEL)

F32 = jnp.float32
BF16 = jnp.bfloat16
LANES = 128
VMEM_LIMIT_BYTES = 56 * 1024 * 1024
MLP_VMEM_LIMIT_BYTES = 60 * 1024 * 1024
N_GATE = 3 * NSA_HEADS
GLA_CHUNK = 32
GLA_UNROLL = 16
GLA_GROUP = 4
NSA_TQ = 256
NSA_TK = 256
VT_ROWS = NSA_HEAD_DIM + 16
CMP_PAGES = 16
PROJ_TM = 512
MERGE_TM = 256
MLP_TM = 1024
MLP_TF = 512
SDS = jax.ShapeDtypeStruct


def _params(*sem, vmem_limit_bytes=VMEM_LIMIT_BYTES):
    return pltpu.CompilerParams(dimension_semantics=sem, vmem_limit_bytes=vmem_limit_bytes)


def _resident(shape):
    nd = len(shape)
    return pl.BlockSpec(shape, lambda *_: (0,) * nd, pipeline_mode=pl.Buffered(1))


def _row_window(row0, n_rows, width):
    return pl.BlockSpec((pl.Element(n_rows), pl.Element(width)), lambda *_: (row0, 0),
                        pipeline_mode=pl.Buffered(1))


def _rms(x, g):
    return x * lax.rsqrt(jnp.mean(x * x, axis=-1, keepdims=True) + EPS) * g


_NT = (((1,), (1,)), ((), ()))


def _proj_kernel(x_ref, g_ref, w_ref, wm_ref, b_ref, *out_refs, spec):
    h = _rms(x_ref[...], g_ref[...]).astype(BF16)
    misc = None
    products = {}
    for o_ref, (kind, c0, c1, scale) in zip(out_refs, spec):
        if kind == 'gates':
            for g in range(NSA_KV_HEADS):
                o_ref[g] = misc if g == 0 else pltpu.roll(misc, LANES - g * 3 * NSA_HPG, axis=1)
            continue
        if (kind == 'misc', c0, c1) not in products:
            rows = wm_ref[...] if kind == 'misc' else w_ref[c0:c1, :]
            products[(kind == 'misc', c0, c1)] = lax.dot_general(h, rows, _NT, preferred_element_type=F32)
        r = products[(kind == 'misc', c0, c1)]
        if scale != 1.0:
            r = r * scale
        if kind == 'f32':
            o_ref[...] = r
        elif kind == 'f32t':
            o_ref[0] = r.T
        elif kind == 'bf16':
            o_ref[...] = r.astype(BF16)
        elif kind == 'sigmoid':
            o_ref[...] = jax.nn.sigmoid(r)
        elif kind == 'hm':
            for i in range((c1 - c0) // NSA_HEAD_DIM):
                o_ref[i] = r[:, i * NSA_HEAD_DIM:(i + 1) * NSA_HEAD_DIM].astype(BF16)
        elif kind == 'hmt':
            rt = r.T
            for i in range((c1 - c0) // NSA_HEAD_DIM):
                o_ref[i] = rt[i * NSA_HEAD_DIM:(i + 1) * NSA_HEAD_DIM].astype(BF16)
        elif kind == 'hmvt':
            rt = r.T
            sub = lax.broadcasted_iota(jnp.int32, (VT_ROWS - NSA_HEAD_DIM, r.shape[0]), 0)
            ones = jnp.where(sub == 0, 1.0, 0.0)
            for i in range((c1 - c0) // NSA_HEAD_DIM):
                piece = rt[i * NSA_HEAD_DIM:(i + 1) * NSA_HEAD_DIM]
                o_ref[i] = jnp.concatenate([piece, ones], axis=0).astype(BF16)
        elif kind == 'misc':
            lane = lax.broadcasted_iota(jnp.int32, r.shape, 1)
            misc = jnp.where(lane < N_GATE, jax.nn.sigmoid(r + b_ref[...]), r)
            o_ref[...] = misc


def _proj(x, norm_g, w_t, row0, n_rows, w_misc_t, bias, spec, tm, seq_len=None):
    m, d = x.shape
    assert m % tm == 0 and w_t.shape[1] == d
    out_shape, out_specs = [], []
    for kind, c0, c1, _ in spec:
        if kind == 'f32t':
            assert seq_len % tm == 0
            per_seq = seq_len // tm
            out_shape.append(SDS((m // seq_len, c1 - c0, seq_len), F32))
            out_specs.append(pl.BlockSpec((1, c1 - c0, tm), lambda i: (i // per_seq, 0, i % per_seq)))
        elif kind == 'hm':
            nh = (c1 - c0) // NSA_HEAD_DIM
            out_shape.append(SDS((nh, m, NSA_HEAD_DIM), BF16))
            out_specs.append(pl.BlockSpec((nh, tm, NSA_HEAD_DIM), lambda i: (0, i, 0)))
        elif kind in ('hmt', 'hmvt'):
            nh = (c1 - c0) // NSA_HEAD_DIM
            nrow = NSA_HEAD_DIM if kind == 'hmt' else VT_ROWS
            out_shape.append(SDS((nh, nrow, m), BF16))
            out_specs.append(pl.BlockSpec((nh, nrow, tm), lambda i: (0, 0, i)))
        elif kind == 'gates':
            out_shape.append(SDS((NSA_KV_HEADS, m, LANES), F32))
            out_specs.append(pl.BlockSpec((NSA_KV_HEADS, tm, LANES), lambda i: (0, i, 0)))
        else:
            out_shape.append(SDS((m, c1 - c0), BF16 if kind == 'bf16' else F32))
            out_specs.append(pl.BlockSpec((tm, c1 - c0), lambda i: (i, 0)))
    return pl.pallas_call(
        functools.partial(_proj_kernel, spec=tuple(spec)),
        grid=(m // tm,),
        in_specs=[pl.BlockSpec((tm, d), lambda i: (i, 0)), _resident((1, d)), _row_window(row0, n_rows, d),
                  _resident(w_misc_t.shape), _resident((1, LANES))],
        out_specs=out_specs, out_shape=out_shape,
        compiler_params=_params("parallel"),
    )(x, norm_g, w_t, w_misc_t, bias)


def _cmp_kernel(pt_ref, k_hbm, v_hbm, perm_ref, pek_ref, pev_ref, w1k_ref, w1v_ref, w2k_ref, w2v_ref,
                ok_ref, ov_ref, kbuf, vbuf, hk, hv, sem, *, n_pages_step, paged):
    b = pl.program_id(0)
    s = pl.program_id(1)
    ns = pl.num_programs(1)
    t = b * ns + s
    total = pl.num_programs(0) * ns
    rows = n_pages_step * 8
    dh, hid_n = NSA_HEAD_DIM, CMP_HIDDEN

    def copies(tt, slot):
        bb = tt // ns
        ss = tt % ns
        out = []
        for p in range(n_pages_step):
            page = pt_ref[bb, ss * n_pages_step + p]
            if paged:
                src = [hbm.at[page] for hbm in (k_hbm, v_hbm)]
            else:
                tok = pl.ds(pl.multiple_of(page * PAGE_SIZE, PAGE_SIZE), PAGE_SIZE)
                src = [hbm.at[bb, :, tok] for hbm in (k_hbm, v_hbm)]
            out.append(pltpu.make_async_copy(src[0], kbuf.at[slot, p], sem.at[slot, 0]))
            out.append(pltpu.make_async_copy(src[1], vbuf.at[slot, p], sem.at[slot, 1]))
        return out

    slot = t % 2

    @pl.when(t == 0)
    def _():
        for c in copies(t, slot):
            c.start()

    @pl.when(t + 1 < total)
    def _():
        for c in copies(t + 1, 1 - slot):
            c.start()

    for c in copies(t, slot):
        c.wait()

    r0 = pl.multiple_of(s * rows, rows)
    low = lax.broadcasted_iota(jnp.int32, (8, LANES), 1) < dh
    tok_pairs = CMP_STRIDE // 2
    nt = (((1,), (1,)), ((), ()))
    perm = perm_ref[...]
    for buf, w1_ref, h_ref in ((kbuf, w1k_ref, hk), (vbuf, w1v_ref, hv)):
        parts = [[[] for _ in range(tok_pairs)] for _ in range(NSA_KV_HEADS)]
        for p in range(n_pages_step):
            r = lax.dot_general(perm, buf[slot, p].astype(BF16), nt, preferred_element_type=F32)
            for q in range(2):
                for m in range(tok_pairs):
                    x0 = r[2 * m * 8:2 * m * 8 + 8, q * LANES:(q + 1) * LANES]
                    x1 = r[(2 * m + 1) * 8:(2 * m + 1) * 8 + 8, q * LANES:(q + 1) * LANES]
                    parts[2 * q][m].append(jnp.where(low, x0, pltpu.roll(x1, dh, axis=1)))
                    parts[2 * q + 1][m].append(jnp.where(low, pltpu.roll(x0, dh, axis=1), x1))
        for g in range(NSA_KV_HEADS):
            z = jnp.concatenate([jnp.concatenate(parts[g][m], axis=0) for m in range(tok_pairs)],
                                axis=1).astype(BF16)
            h_ref[pl.ds(r0, rows), g * 2 * hid_n:(g + 1) * 2 * hid_n] = jnp.dot(
                z, w1_ref[...], preferred_element_type=F32)

    @pl.when(s == ns - 1)
    def _():
        for h_ref, pe_ref, w1_ref, w2_ref, o_ref in ((hk, pek_ref, w1k_ref, w2k_ref, ok_ref),
                                                     (hv, pev_ref, w1v_ref, w2v_ref, ov_ref)):
            n_chunk = h_ref.shape[0]
            pe = jnp.broadcast_to(pe_ref[...], (2, 16, pe_ref.shape[2])).astype(BF16)
            pe_term = (jnp.dot(pe[0], w1_ref[:, 0:hid_n], preferred_element_type=F32)
                       + jnp.dot(pe[1], w1_ref[:, hid_n:2 * hid_n], preferred_element_type=F32))[0:1]
            hid = []
            for g in range(NSA_KV_HEADS):
                first = h_ref[:, g * 2 * hid_n:g * 2 * hid_n + hid_n]
                last = h_ref[:, g * 2 * hid_n + hid_n:(g + 1) * 2 * hid_n]
                hid.append(first + pltpu.roll(last, n_chunk - 1, axis=0) + pe_term)
            hid = jnp.concatenate(hid, axis=1)
            res = jnp.dot(jax.nn.gelu(hid).astype(BF16), w2_ref[...], preferred_element_type=F32)
            if o_ref is ok_ref:
                for g in range(NSA_KV_HEADS):
                    o_ref[0, g] = res[:, g * NSA_HEAD_DIM:(g + 1) * NSA_HEAD_DIM].astype(BF16)
            else:
                res_t = res.T
                for g in range(NSA_KV_HEADS):
                    o_ref[0, g] = res_t[g * NSA_HEAD_DIM:(g + 1) * NSA_HEAD_DIM].astype(BF16)


def _cmp_weights(pe, w1, w2):
    g = NSA_KV_HEADS
    half = CMP_STRIDE * NSA_HEAD_DIM
    pe_t = pe.reshape(2, 1, half)
    w1ab = jnp.concatenate([w1[:half], w1[half:]], axis=1)
    w2b = jnp.einsum('jd,gh->gjhd', w2, jnp.eye(g, dtype=F32)).reshape(g * CMP_HIDDEN, g * NSA_HEAD_DIM)
    return pe_t, w1ab.astype(BF16), w2b.astype(BF16)


def _compress(page_table, k_src, v_src, cw, paged):
    bsz, n_pages = page_table.shape
    p_step = min(CMP_PAGES, n_pages)
    ns = n_pages // p_step
    n_chunk = n_pages * 8
    page_shape = (NSA_KV, PAGE_SIZE)
    assert NSA_KV == 2 * LANES
    assert k_src.shape[1:] == (page_shape if paged else (NSA_KV, n_pages * PAGE_SIZE))
    half = CMP_STRIDE * NSA_HEAD_DIM
    gh = NSA_KV_HEADS * CMP_HIDDEN
    out_row = jnp.arange(PAGE_SIZE, dtype=jnp.int32)
    src_tok = (out_row % 8) * CMP_STRIDE + out_row // 8
    perm = (src_tok[:, None] == jnp.arange(PAGE_SIZE, dtype=jnp.int32)[None, :]).astype(BF16)
    k_sds =SDS((bsz, NSA_KV_HEADS, n_chunk, NSA_HEAD_DIM), BF16)
    v_sds = SDS((bsz, NSA_KV_HEADS, NSA_HEAD_DIM, n_chunk), BF16)
    k_spec = pl.BlockSpec((1, NSA_KV_HEADS, n_chunk, NSA_HEAD_DIM), lambda b, s, pt: (b, 0, 0, 0))
    v_spec = pl.BlockSpec((1, NSA_KV_HEADS, NSA_HEAD_DIM, n_chunk), lambda b, s, pt: (b, 0, 0, 0))
    grid_spec = pltpu.PrefetchScalarGridSpec(
        num_scalar_prefetch=1, grid=(bsz, ns),
        in_specs=[pl.BlockSpec(memory_space=pl.ANY), pl.BlockSpec(memory_space=pl.ANY),
                  _resident((PAGE_SIZE, PAGE_SIZE)), _resident((2, 1, half)), _resident((2, 1, half)),
                  _resident((half, 2 * CMP_HIDDEN)), _resident((half, 2 * CMP_HIDDEN)),
                  _resident((gh, NSA_KV)), _resident((gh, NSA_KV))],
        out_specs=[k_spec, v_spec],
        scratch_shapes=[pltpu.VMEM((2, p_step) + page_shape, F32), pltpu.VMEM((2, p_step) + page_shape, F32),
                        pltpu.VMEM((n_chunk, 2 * gh), F32), pltpu.VMEM((n_chunk, 2 * gh), F32),
                        pltpu.SemaphoreType.DMA((2, 2))])
    return pl.pallas_call(
        functools.partial(_cmp_kernel, n_pages_step=p_step, paged=paged),
        grid_spec=grid_spec, out_shape=[k_sds, v_sds],
        compiler_params=_params("arbitrary", "arbitrary"),
    )(page_table, k_src, v_src, perm, cw['pe_k'], cw['pe_v'], cw['w1_k'], cw['w1_v'], cw['w2_k'], cw['w2_v'])


def _softmax_rows(s, mask):
    sm = jnp.where(mask, s, NEG)
    m = jnp.max(sm, axis=-1, keepdims=True)
    e = jnp.where(mask, jnp.exp(sm - m), 0.0)
    return e, jnp.sum(e, axis=-1, keepdims=True)


def _safe_inv(l):
    return jnp.where(l > 0.0, 1.0 / jnp.where(l > 0.0, l, 1.0), 0.0)


M_INIT = -1e20
LOG2_E = 1.4426950408889634


def _nsa_prompt_kernel(qt_ref, kc_ref, vct_ref, ks_ref, vst_ref, kw_ref, vwt_ref, gate_ref,
                       o_ref, *scratch, n_cmp, n_slc, n_sel, n_qblocks):
    tq, tk, hpg, dh = NSA_TQ, NSA_TK, NSA_HPG, NSA_HEAD_DIM
    rk_refs = scratch[:tq // LANES]
    rest = scratch[tq // LANES:]
    selb_ref = rest[0]
    m_ref, acc_ref, s_refs = rest[1], rest[2], rest[3:5]
    mw_ref, accw_ref, sw_refs = rest[5], rest[6], rest[7:9]
    qi = pl.program_id(2)
    q0 = qi * tq
    n_chunk = kc_ref.shape[2]
    r = SLC_BLOCK // CMP_STRIDE
    n_row = n_chunk // r

    ci = lax.broadcasted_iota(jnp.int32, (n_chunk, tq), 0)
    pos_c = q0 + lax.broadcasted_iota(jnp.int32, (n_chunk, tq), 1)
    m_c = (ci < n_cmp) & (ci * CMP_STRIDE + (CMP_BLOCK - 1) <= pos_c)
    kc = kc_ref[0, 0]
    vct = vct_ref[0, 0]
    o_c = []
    psum = None
    for h in range(hpg):
        s = jnp.where(m_c, jnp.dot(kc, qt_ref[0, h], preferred_element_type=F32), NEG)
        e = jnp.where(m_c, jnp.exp2(s - jnp.max(s, axis=0, keepdims=True)), 0.0)
        p = e * _safe_inv(jnp.sum(e, axis=0, keepdims=True))
        o_c.append(jnp.dot(vct, p.astype(BF16), preferred_element_type=F32))
        psum = p if psum is None else psum + p

    imp = pltpu.roll(psum, 1, axis=0) + psum
    for o in range(1, r):
        imp = imp + pltpu.roll(psum, n_chunk - o, axis=0)
    parts = []
    for i, rk_ref in enumerate(rk_refs):
        rk_ref[...] = imp[:, i * LANES:(i + 1) * LANES]
        parts.append(rk_ref[pl.ds(0, n_row, stride=r), :])
    imp_b = jnp.concatenate(parts, axis=1)
    j_io = lax.broadcasted_iota(jnp.int32, (n_row, tq), 0)
    pos_b = q0 + lax.broadcasted_iota(jnp.int32, (n_row, tq), 1)
    valid = (j_io < n_slc) & (j_io * SLC_BLOCK <= pos_b)
    forced = (j_io == 0) | (j_io == pos_b // SLC_BLOCK)
    sc = jnp.where(valid, imp_b + jnp.where(forced, FORCE, 0.0), NEG)
    rank = jnp.zeros((n_row, tq), F32)
    for k in range(n_slc):
        ck = sc[k:k + 1, :]
        beats = (ck > sc) | ((ck == sc) & (j_io > k))
        rank = rank + jnp.where(beats, 1.0, 0.0)
    selb_ref[...] = jnp.where(rank < n_sel, 0.0, NEG)

    def key_minus_query(n_keys):
        return (lax.broadcasted_iota(jnp.int32, (n_keys, tq), 0)
                - lax.broadcasted_iota(jnp.int32, (n_keys, tq), 1))

    def start(m_ref, acc_ref):
        m_ref[...] = jnp.full(m_ref.shape, M_INIT, F32)
        acc_ref[...] = jnp.zeros(acc_ref.shape, F32)

    def scores(kk, bias, s_ref):
        for h in range(hpg):
            for c in range(kk.shape[0] // SLC_BLOCK):
                rs = slice(c * SLC_BLOCK, (c + 1) * SLC_BLOCK)
                s_ref[h, rs, :] = jnp.dot(kk[rs], qt_ref[0, h], preferred_element_type=F32) + bias[rs]

    def softmax_values(s_ref, vt, m_ref, acc_ref):
        for h in range(hpg):
            ps, alphas = [], []
            for c in range(tq // LANES):
                cs = slice(c * LANES, (c + 1) * LANES)
                s = s_ref[h, :, cs]
                m_prev = m_ref[h, :, cs]
                m_new = jnp.maximum(m_prev, jnp.max(s, axis=0, keepdims=True))
                ps.append(jnp.exp2(s - m_new).astype(BF16))
                alphas.append(jnp.exp2(m_prev - m_new))
                m_ref[h, :, cs] = m_new
            acc_ref[h] = (jnp.concatenate(alphas, axis=1) * acc_ref[h]
                          + jnp.dot(vt, jnp.concatenate(ps, axis=1), preferred_element_type=F32))

    def finish(acc_ref):
        return [acc_ref[h, 0:dh, :] * _safe_inv(acc_ref[h, dh:dh + 1, :]) for h in range(hpg)]

    def window_attention():
        tkw = sw_refs[0].shape[1]
        base = jnp.maximum(q0 - WINDOW, 0)
        dlt = key_minus_query(tkw)
        start(mw_ref, accw_ref)
        tiles = [pl.multiple_of(base + i * tkw, LANES) for i in range(2)]
        for i, k0 in enumerate(tiles):
            d = dlt + (k0 - q0)
            bias = jnp.where((d <= 0) & (d >= -WINDOW), 0.0, NEG)
            scores(kw_ref[0, 0, pl.ds(k0, tkw), :], bias, sw_refs[i])
        for i, k0 in enumerate(tiles):
            softmax_values(sw_refs[i], vwt_ref[0, :, pl.ds(k0, tkw)], mw_ref, accw_ref)

    def selected_attention(n_tiles):
        blocks_per_tile = tk // SLC_BLOCK
        start(m_ref, acc_ref)
        for i in range(n_tiles):
            keys = slice(i * tk, (i + 1) * tk)
            rows = [jnp.broadcast_to(selb_ref[i * blocks_per_tile + j:i * blocks_per_tile + j + 1, :], (SLC_BLOCK, tq))
                    for j in range(blocks_per_tile)]
            bias = jnp.concatenate(rows, axis=0)
            if i == n_tiles - 1:
                bias = jnp.where(key_minus_query(tk) <= 0, bias, NEG)
            scores(ks_ref[0, 0, keys, :], bias, s_refs[i % 2])
            softmax_values(s_refs[i % 2], vst_ref[0, :, keys], m_ref, acc_ref)

    window_attention()
    lax.switch(qi, [functools.partial(selected_attention, n + 1) for n in range(n_qblocks)])
    o_s = finish(acc_ref)
    o_w = finish(accw_ref)

    gt = gate_ref[0].T
    outs = [gt[h:h + 1] * o_c[h] + gt[hpg + h:hpg + h + 1] * o_s[h] + gt[2 * hpg + h:2 * hpg + h + 1] * o_w[h]
            for h in range(hpg)]
    o_ref[...] = jnp.concatenate(outs, axis=0).T.astype(BF16)


def _nsa_prompt(q_t, ks_hm, vs_t, kw_hm, vw_t, kcmp, vcmp_t, gates_hm, bsz, t):
    g, hpg, dh, tq = NSA_KV_HEADS, NSA_HPG, NSA_HEAD_DIM, NSA_TQ
    r = SLC_BLOCK // CMP_STRIDE
    n_slc = t // SLC_BLOCK
    assert t % tq == 0 and n_slc * r <= LANES and NSA_TK % SLC_BLOCK == 0 and NSA_TK == tq
    nq = t // tq
    n_cmp = kcmp.shape[2] - 1
    if kcmp.shape[2] < LANES:
        fill = LANES - kcmp.shape[2]
        kcmp = jnp.pad(kcmp, ((0, 0), (0, 0), (0, fill), (0, 0)))
        vcmp_t = jnp.pad(vcmp_t, ((0, 0), (0, 0), (0, 0), (0, fill)))
    n_chunk = kcmp.shape[2]
    assert n_chunk == LANES
    q4 = q_t.reshape(g, hpg, dh, bsz * t)

    def k_spec():
        return pl.BlockSpec((1, 1, t, dh), lambda b, gg, qi: (gg, b, 0, 0))

    def vt_spec():
        return pl.BlockSpec((1, VT_ROWS, t), lambda b, gg, qi: (gg, 0, b))

    def per_bt(a):
        return a.reshape(g, bsz, t, a.shape[-1])

    def flash_scratch(n_keys):
        return [pltpu.VMEM((hpg, 1, tq), F32), pltpu.VMEM((hpg, VT_ROWS, tq), F32),
                pltpu.VMEM((hpg, n_keys, tq), F32), pltpu.VMEM((hpg, n_keys, tq), F32)]

    tk_win = (WINDOW + tq) // 2
    assert tk_win % LANES == 0 and t >= WINDOW + tq

    return pl.pallas_call(
        functools.partial(_nsa_prompt_kernel, n_cmp=n_cmp, n_slc=n_slc, n_sel=min(N_SELECT, n_slc), n_qblocks=nq),
        grid=(bsz, g, nq),
        in_specs=[pl.BlockSpec((1, hpg, dh, tq), lambda b, gg, qi: (gg, 0, 0, b * nq + qi)),
                  pl.BlockSpec((1, 1, n_chunk, dh), lambda b, gg, qi: (b, gg, 0, 0)),
                  pl.BlockSpec((1, 1, dh, n_chunk), lambda b, gg, qi: (b, gg, 0, 0)),
                  k_spec(), vt_spec(), k_spec(), vt_spec(),
                  pl.BlockSpec((1, tq, LANES), lambda b, gg, qi: (gg, b * nq + qi, 0))],
        out_specs=pl.BlockSpec((tq, hpg * dh), lambda b, gg, qi: (b * nq + qi, gg)),
        out_shape=SDS((bsz * t, NSA_Q), BF16),
        scratch_shapes=([pltpu.VMEM((n_chunk, LANES), F32)] * (tq // LANES)
                        + [pltpu.VMEM((n_chunk // r, tq), F32)]
                        + flash_scratch(NSA_TK) + flash_scratch(tk_win)),
        compiler_params=_params("parallel", "parallel", "arbitrary"),
    )(q4, kcmp, vcmp_t, per_bt(ks_hm), vs_t, per_bt(kw_hm), vw_t, gates_hm)


def _gla_prompt_kernel(q_ref, k_ref, v_ref, r_ref, misc_ref, wa_ref, ba_ref, ng_ref,
                       o_ref, s_ref, qe_ref, qt_ref, kt_ref, kh_ref, d_ref, u_ref):
    t = q_ref.shape[0]
    c = GLA_CHUNK
    n = t // c
    dk, dv = GLA_DK, GLA_DV
    lr = misc_ref[:, N_GATE:N_GATE + GLA_RANK].astype(BF16)
    x = jnp.dot(lr, wa_ref[...], preferred_element_type=F32) + ba_ref[...]
    g = jax.nn.log_sigmoid(x) / GLA_TAU
    row = lax.broadcasted_iota(jnp.int32, (t, dk), 0) % c
    b = g
    sh = 1
    while sh < c:
        b = b + jnp.where(row >= sh, pltpu.roll(b, sh, axis=0), 0.0)
        sh *= 2
    b3 = b.reshape(n, c, dk)
    b_last = jnp.broadcast_to(b3[:, c - 1:c, :], (n, c, dk)).reshape(t, dk)
    b_mid = jnp.broadcast_to(b3[:, c // 2 - 1:c // 2, :], (n, c, dk)).reshape(t, dk)
    q = q_ref[...] * (GLA_DK ** -0.5)
    k = k_ref[...]
    qe_ref[...] = (q * jnp.exp(b)).astype(BF16)
    qt_ref[...] = (q * jnp.exp(b - b_mid)).astype(BF16)
    kt_ref[...] = (k * jnp.exp(b_mid - b)).astype(BF16)
    kh_ref[...] = (k * jnp.exp(b_last - b)).astype(BF16)
    d_ref[...] = jnp.exp(b_last)
    tn = (((0,), (0,)), ((), ()))
    nt = (((1,), (1,)), ((), ()))

    def chunk_update(i, carry):
        r0 = pl.multiple_of(i * c, c)
        u_ref[i] = lax.dot_general(v_ref[pl.ds(r0, c), :], kh_ref[pl.ds(r0, c), :], tn,
                                   preferred_element_type=F32)
        return carry

    lax.fori_loop(0, n, chunk_update, 0, unroll=GLA_UNROLL)

    def recur(i, st):
        nxt = st * d_ref[pl.ds(pl.multiple_of(i * c, c), 1), :] + u_ref[i]
        u_ref[i] = st
        return nxt

    s_ref[0, 0] = lax.fori_loop(0, n, recur, jnp.zeros((dv, dk), F32)).T

    grp = GLA_GROUP
    rows = grp * c
    ri =lax.broadcasted_iota(jnp.int32, (rows, rows), 0)
    ci = lax.broadcasted_iota(jnp.int32, (rows, rows), 1)
    keep = (ri // c == ci // c) & (ri >= ci)
    ng = ng_ref[...]

    def group_out(i, carry):
        r0 = pl.multiple_of(i * rows, rows)
        rs = pl.ds(r0, rows)
        a = lax.dot_general(qt_ref[rs, :], kt_ref[rs, :], nt, preferred_element_type=F32)
        a = jnp.where(keep, a, 0.0).astype(BF16)
        o = jnp.dot(a, v_ref[rs, :], preferred_element_type=F32)
        inter = [lax.dot_general(qe_ref[pl.ds(r0 + j * c, c), :], u_ref[i * grp + j].astype(BF16), nt,
                                 preferred_element_type=F32) for j in range(grp)]
        o = o + jnp.concatenate(inter, axis=0)
        o = _rms(o, ng) * jax.nn.silu(r_ref[rs, :])
        o_ref[rs, :] = o.astype(BF16)
        return carry

    lax.fori_loop(0, n // grp, group_out, 0, unroll=16)


def _gla_prompt(q_l, k_l, v_l, r_l, misc, wa, ba, ng, bsz, t):
    h, dk, dv = GLA_HEADS, GLA_DK, GLA_DV
    assert t % (GLA_CHUNK * GLA_GROUP * 2) == 0
    n = t // GLA_CHUNK
    return pl.pallas_call(
        _gla_prompt_kernel,
        grid=(bsz, h),
        in_specs=[pl.BlockSpec((t, dk), lambda b, hh: (b, hh)), pl.BlockSpec((t, dk), lambda b, hh: (b, hh)),
                  pl.BlockSpec((t, dv), lambda b, hh: (b, hh)), pl.BlockSpec((t, dv), lambda b, hh: (b, hh)),
                  pl.BlockSpec((t, LANES), lambda b, hh: (b, 0)),
                  pl.BlockSpec((GLA_RANK, dk), lambda b, hh: (0, hh)), pl.BlockSpec((1, dk), lambda b, hh: (0, hh)),
                  _resident((1, dv))],
        out_specs=[pl.BlockSpec((t, dv), lambda b, hh: (b, hh)),
                   pl.BlockSpec((1, 1, dk, dv), lambda b, hh: (b, hh, 0, 0))],
        out_shape=[SDS((bsz * t, h * dv), BF16), SDS((bsz, h, dk, dv), F32)],
        scratch_shapes=[pltpu.VMEM((t, dk), BF16), pltpu.VMEM((t, dk), BF16), pltpu.VMEM((t, dk), BF16),
                        pltpu.VMEM((t, dk), BF16), pltpu.VMEM((t, dk), F32),
                        pltpu.VMEM((n, dv, dk), F32)],
        compiler_params=_params("parallel", "parallel"),
    )(q_l, k_l, v_l, r_l, misc, wa, ba, ng)


def _merge_rows(oa, ob, x, g_ref, wg_ref, wa_ref, wb_ref, wo_ref):
    h = _rms(x, g_ref[...]).astype(BF16)
    d = D_MODEL
    ga = jax.nn.sigmoid(lax.dot_general(h, wg_ref[0:d, :], _NT, preferred_element_type=F32))
    u = ga * jnp.dot(oa, wa_ref[...], preferred_element_type=F32)
    gb = jax.nn.sigmoid(lax.dot_general(h, wg_ref[d:2 * d, :], _NT, preferred_element_type=F32))
    u = u + gb * jnp.dot(ob, wb_ref[...], preferred_element_type=F32)
    return x + jnp.dot(u.astype(BF16), wo_ref[...], preferred_element_type=F32)


def _merge_kernel(oa_ref, ob_ref, x_ref, oas_ref, obs_ref, xs_ref, g_ref, wg_ref, wa_ref, wb_ref, wo_ref,
                  x1_ref, x1s_ref):
    weights = (g_ref, wg_ref, wa_ref, wb_ref, wo_ref)
    x1_ref[...] = _merge_rows(oa_ref[...], ob_ref[...], x_ref[...], *weights)

    @pl.when(pl.program_id(0) == 0)
    def _():
        x1s_ref[...] = _merge_rows(oas_ref[...], obs_ref[...], xs_ref[...], *weights)


def _merge(o_a, o_b, x, oa_s, ob_s, x_s, norm_g, w_t, gate_row0, w_a, w_b, w_o, tm):
    m, d = x.shape
    assert m % tm == 0
    row_specs = [pl.BlockSpec((tm, NSA_Q), lambda i: (i, 0)), pl.BlockSpec((tm, GLA_HEADS * GLA_DV), lambda i: (i, 0)),
                 pl.BlockSpec((tm, d), lambda i: (i, 0))]
    return pl.pallas_call(
        _merge_kernel, grid=(m // tm,),
        in_specs=(row_specs + [_resident(a.shape) for a in (oa_s, ob_s, x_s, norm_g)]
                  + [_row_window(gate_row0, 2 * d, d)] + [_resident(a.shape) for a in (w_a, w_b, w_o)]),
        out_specs=[pl.BlockSpec((tm, d), lambda i: (i, 0)), pl.BlockSpec(x_s.shape, lambda i: (0, 0))],
        out_shape=[SDS((m, d), F32), SDS(x_s.shape, F32)],
        compiler_params=_params("arbitrary"),
    )(o_a, o_b, x, oa_s, ob_s, x_s, norm_g, w_t, w_a, w_b, w_o)


def _mlp_kernel(x1_ref, x1s_ref, g2_ref, wu_ref, wd_ref, gf_ref, y_ref, ys_ref, h_ref):
    i = pl.program_id(0)
    j = pl.program_id(1)
    last = pl.num_programs(1) - 1
    tm = x1_ref.shape[0]

    @pl.when(j == 0)
    def _():
        x1 = x1_ref[...]
        h_ref[0:tm, :] = _rms(x1, g2_ref[...]).astype(BF16)
        y_ref[...] = x1

    @pl.when((j == 0) & (i == 0))
    def _():
        x1s = x1s_ref[...]
        h_ref[tm:, :] = _rms(x1s, g2_ref[...]).astype(BF16)
        ys_ref[...] = x1s

    up = jnp.maximum(jnp.dot(h_ref[...], wu_ref[...].astype(BF16), preferred_element_type=F32), 0.0)
    res = jnp.dot((up * up).astype(BF16), wd_ref[...].astype(BF16), preferred_element_type=F32)
    y_ref[...] += res[0:tm]

    @pl.when(i == 0)
    def _():
        ys_ref[...] += res[tm:]

    @pl.when(j == last)
    def _():
        y_ref[...] = _rms(y_ref[...], gf_ref[...])

    @pl.when((j == last) & (i == 0))
    def _():
        ys_ref[...] = _rms(ys_ref[...], gf_ref[...])


def _mlp(x1, x1_s, g2, w_up, w_down, gf, tm, tf):
    m, d = x1.shape
    ff = w_up.shape[1]
    assert m % tm == 0 and ff % tf == 0
    small = pl.BlockSpec(x1_s.shape, lambda i, j: (0, 0))
    return pl.pallas_call(
        _mlp_kernel, grid=(m // tm, ff // tf),
        in_specs=[pl.BlockSpec((tm, d), lambda i, j: (i, 0)), small, _resident((1, d)),
                  pl.BlockSpec((d, tf), lambda i, j: (0, j)), pl.BlockSpec((tf, d), lambda i, j: (j, 0)),
                  _resident((1, d))],
        out_specs=[pl.BlockSpec((tm, d), lambda i, j: (i, 0)), small],
        out_shape=[SDS((m, d), F32), SDS(x1_s.shape, F32)],
        scratch_shapes=[pltpu.VMEM((tm + x1_s.shape[0], d), BF16)],
        compiler_params=_params("arbitrary", "arbitrary", vmem_limit_bytes=MLP_VMEM_LIMIT_BYTES),
    )(x1, x1_s, g2, w_up, w_down, gf)


def _prep_weights(norm1_g, w_in, b_nsa_gate, cmp_pe_k, cmp_pe_v, cmp_k_w1, cmp_k_w2, cmp_v_w1, cmp_v_w2,
                  gla_w_a2, gla_b_a, gla_norm_g, w_br_a, w_br_b, w_o, norm2_g, w_up, w_down, norm_f):
    pts = [0]
    for s in SPLITS:
        pts.append(pts[-1] + s)
    c_q, c_kv, c_g, c_ql, c_kl, c_vl, c_rl, c_lr, c_br, c_end = pts
    gcols = jnp.asarray([c_g + (g * NSA_HPG + h) * 3 + c for g in range(NSA_KV_HEADS)
                         for c in range(3) for h in range(NSA_HPG)], jnp.int32)
    w_t = w_in.T.astype(BF16)
    pad = jnp.zeros((LANES - N_GATE - GLA_RANK, D_MODEL), BF16)
    w_misc_t = jnp.concatenate([w_t[gcols], w_t[c_lr:c_br], pad], axis=0)
    b_misc = jnp.concatenate([b_nsa_gate[gcols - c_g], jnp.zeros((LANES - N_GATE,), F32)])[None, :]
    w = dict(
        norm1=norm1_g[None, :], norm2=norm2_g[None, :], norm_f=norm_f[None, :],
        w_t=w_t, w_misc_t=w_misc_t, row_a=c_q, rows_a=c_g - c_q, row_b=c_ql, rows_b=c_lr - c_ql, row_gate=c_br,
        b_misc=b_misc,
        gla_wa=gla_w_a2.astype(BF16), gla_ba=gla_b_a[None, :], gla_ng=gla_norm_g[None, :],
        w_br_a=w_br_a.astype(BF16), w_br_b=w_br_b.astype(BF16), w_o=w_o.astype(BF16),
        w_up=w_up, w_down=w_down,
    )
    cw = {}
    cw['pe_k'], cw['w1_k'], cw['w2_k'] = _cmp_weights(cmp_pe_k, cmp_k_w1, cmp_k_w2)
    cw['pe_v'], cw['w1_v'], cw['w2_v'] = _cmp_weights(cmp_pe_v, cmp_v_w1, cmp_v_w2)
    w['cmp'] = cw
    return w


_GQ =GLA_HEADS * GLA_DK
_GV = GLA_HEADS * GLA_DV


def _layer_prompt(x, w):
    bsz, t, d = x.shape
    m = bsz * t
    x2 = x.reshape(m, d)
    zb = jnp.zeros((1, LANES), F32)
    spec_a = [('hmt', 0, NSA_Q, NSA_HEAD_DIM ** -0.5 * LOG2_E)]
    for j in range(6):
        cols = (NSA_Q + j * NSA_KV, NSA_Q + (j + 1) * NSA_KV, 1.0)
        spec_a.append(('f32t',) + cols)
        if j >= 2:
            spec_a.append(('hm' if j % 2 == 0 else 'hmvt',) + cols)
    q_t, r0, r1, r2, ks_hm, r3, vs_t, r4, kw_hm, r5, vw_t = _proj(
        x2, w['norm1'], w['w_t'], w['row_a'], w['rows_a'], w['w_misc_t'], zb, spec_a, PROJ_TM, seq_len=t)
    spec_b = [('f32', 0, _GQ, 1.0), ('f32', _GQ, 2 * _GQ, 1.0), ('bf16', 2 * _GQ, 2 * _GQ + _GV, 1.0),
              ('f32', 2 * _GQ + _GV, 2 * _GQ + 2 * _GV, 1.0),
              ('misc', 2 * _GQ + 2 * _GV, 2 * _GQ + 2 * _GV + LANES, 1.0), ('gates', 0, 0, 1.0)]
    q_l, k_l, v_l, r_l, misc, gates_hm = _proj(x2, w['norm1'], w['w_t'], w['row_b'], w['rows_b'], w['w_misc_t'],
                                               w['b_misc'], spec_b, PROJ_TM)

    n_pages = t // PAGE_SIZE
    ident = jnp.broadcast_to(jnp.arange(n_pages, dtype=jnp.int32), (bsz, n_pages))
    kcmp, vcmp = _compress(ident, r0, r1, w['cmp'], paged=False)
    o_a = _nsa_prompt(q_t, ks_hm, vs_t, kw_hm, vw_t, kcmp, vcmp, gates_hm, bsz, t)
    o_b, s_new = _gla_prompt(q_l, k_l, v_l, r_l, misc, w['gla_wa'], w['gla_ba'], w['gla_ng'], bsz, t)
    def token_major(a):
        return a.reshape(bsz, NSA_KV_HEADS, NSA_HEAD_DIM, a.shape[-1]).transpose(0, 3, 1, 2)

    rows = [token_major(a) for a in (r0, r1, r2, r3)]
    n_keep = min(WINDOW, t)
    wins = [token_major(a[:, :, t - n_keep:]) for a in (r4, r5)]
    return (o_a, o_b, x2), rows, wins, s_new


def _channel_mix(prompt, sample, w):
    (oa_p, ob_p, x_p), (oa_s, ob_s, x_s) = prompt, sample
    m = x_p.shape[0]
    x1_p, x1_s = _merge(oa_p, ob_p, x_p, oa_s, ob_s, x_s, w['norm1'], w['w_t'], w['row_gate'], w['w_br_a'],
                        w['w_br_b'], w['w_o'], MERGE_TM)
    tm = next(c for c in (MLP_TM, PROJ_TM, MERGE_TM) if m % c == 0)
    return _mlp(x1_p, x1_s, w['norm2'], w['w_up'], w['w_down'], w['norm_f'], tm, MLP_TF)


def _group_rows(parts):
    rowg = lax.broadcasted_iota(jnp.int32, parts[0].shape, 0) // NSA_HPG
    out = parts[0]
    for g in range(1, NSA_KV_HEADS):
        out = jnp.where(rowg == g, parts[g], out)
    return out


def _sample_select_kernel(q_ref, kc_ref, vc_ref, oc_ref, idx_ref, *, n_cmp, pos, n_pick):
    bsz, nh = q_ref.shape[0], q_ref.shape[1]
    nt = (((1,), (1,)), ((), ()))
    n_chunk = kc_ref.shape[2]
    s = jnp.concatenate(
        [_group_rows([lax.dot_general(q_ref[b], kc_ref[b, g], nt, preferred_element_type=F32)
                      for g in range(NSA_KV_HEADS)]) for b in range(bsz)], axis=0)
    lane = lax.broadcasted_iota(jnp.int32, s.shape, 1)
    mask = (lane < n_cmp) & (lane * CMP_STRIDE + (CMP_BLOCK - 1) <= pos)
    e, l = _softmax_rows(s, mask)
    p = e * _safe_inv(l)
    pb = p.astype(BF16)
    for b in range(bsz):
        oc_ref[b] = _group_rows([lax.dot_general(pb[b * nh:(b + 1) * nh], vc_ref[b, g], nt,
                                                 preferred_element_type=F32) for g in range(NSA_KV_HEADS)])
    nr = p.shape[0]
    y = p + pltpu.roll(p, nr - 1, axis=0)
    psum = y + pltpu.roll(y, nr - 2, axis=0)
    imp = pltpu.roll(psum, 1, axis=1) + psum
    r = SLC_BLOCK // CMP_STRIDE
    for o in range(1, r):
        imp = imp + pltpu.roll(psum, n_chunk - o, axis=1)
    blk = lane // r
    is_blk = lane % r == 0
    valid = blk * SLC_BLOCK <= pos
    forced = (blk == 0) | (blk == pos // SLC_BLOCK)
    score = jnp.where(is_blk & valid, imp + jnp.where(forced, FORCE, 0.0), NEG)
    lane_f = lane.astype(F32)
    out_lane = lax.broadcasted_iota(jnp.int32, (nr, LANES), 1)
    picked = jnp.zeros((nr, LANES), F32)
    for k in range(n_pick):
        mx = jnp.max(score, axis=-1, keepdims=True)
        ix = jnp.min(jnp.where(score == mx, lane_f, float(n_chunk)), axis=-1, keepdims=True)
        picked = jnp.where(out_lane == k, ix, picked)
        score = jnp.where(lane_f == ix, 2.0 * NEG, score)
    idx_ref[...] = (picked.astype(jnp.int32) // r).reshape(idx_ref.shape)


def _sample_select(q_s, kcmp, vcmp, pos, n_pick):
    bsz, g, n_chunk, dh = kcmp.shape
    nh = NSA_HEADS
    return pl.pallas_call(
        functools.partial(_sample_select_kernel, n_cmp=n_chunk - 1, pos=pos, n_pick=n_pick),
        grid=(1,),
        in_specs=[pl.BlockSpec((bsz, nh, dh), lambda i: (0, 0, 0)),
                  pl.BlockSpec(kcmp.shape, lambda i: (0, 0, 0, 0)), pl.BlockSpec(vcmp.shape, lambda i: (0, 0, 0, 0))],
        out_specs=[pl.BlockSpec((bsz, nh, dh), lambda i: (0, 0, 0)), pl.BlockSpec((bsz, nh, LANES), lambda i: (0, 0, 0))],
        out_shape=[SDS((bsz, nh, dh), F32), SDS((bsz, nh, LANES), jnp.int32)],
        compiler_params=_params("arbitrary"),
    )(q_s, kcmp, vcmp)


def _sample_attend_kernel(idx_ref, pt_ref, q_ref, oc_ref, ks_hbm, vs_hbm, kw_ref, vw_ref,
                          nks_ref, nvs_ref, nkw_ref, nvw_ref, gate_ref, o_ref, ksel, vsel, sem, *, n_pick):
    b = pl.program_id(0)
    g_n, dh = NSA_KV_HEADS, NSA_HEAD_DIM
    half = PAGE_SIZE // SLC_BLOCK

    def copies(bb, slot):
        out = []
        for g in range(g_n):
            for r in range(n_pick):
                page = pt_ref[bb, idx_ref[bb * g_n + g, r] // half]
                out.append(pltpu.make_async_copy(ks_hbm.at[page, g], ksel.at[slot, g, r], sem.at[slot, 0]))
                out.append(pltpu.make_async_copy(vs_hbm.at[page, g], vsel.at[slot, g, r], sem.at[slot, 1]))
        return out

    def start_all(cps):
        for i, c in enumerate(cps):
            c.start(priority=i % 2)

    slot = b % 2

    @pl.when(b == 0)
    def _():
        start_all(copies(b, slot))

    @pl.when(b + 1 < pl.num_programs(0))
    def _():
        start_all(copies(b + 1, 1 - slot))

    q = q_ref[0]
    qf = q.astype(F32)
    nt = (((1,), (1,)), ((), ()))

    def attend(keys_t, vals_t, bias, k_new, v_new):
        s = _group_rows([jnp.dot(q, keys_t[g].astype(BF16), preferred_element_type=F32)
                         + (0.0 if bias is None else bias[g]) for g in range(g_n)])
        s_new = _group_rows([jnp.sum(qf * k_new[:, g * dh:(g + 1) * dh], axis=-1, keepdims=True)
                             for g in range(g_n)])
        m = jnp.maximum(jnp.max(s, axis=-1, keepdims=True), s_new)
        e = jnp.exp(s - m)
        e_new = jnp.exp(s_new - m)
        l = jnp.sum(e, axis=-1, keepdims=True) + e_new
        eb = e.astype(BF16)
        acc = _group_rows([lax.dot_general(eb, vals_t[g].astype(BF16), nt, preferred_element_type=F32)
                           + e_new * v_new[:, g * dh:(g + 1) * dh] for g in range(g_n)])
        return acc / l

    o_w = attend([kw_ref[0, g] for g in range(g_n)], [vw_ref[0, g] for g in range(g_n)], None,
                 nkw_ref[0], nvw_ref[0])
    for c in copies(b, slot):
        c.wait()
    lin = lax.broadcasted_iota(jnp.int32, (1, PAGE_SIZE), 1)
    bias = []
    for g in range(g_n):
        parts = []
        for r in range(n_pick):
            off = (idx_ref[b * g_n + g, r] % half) * SLC_BLOCK
            parts.append(jnp.where((lin >= off) & (lin < off + SLC_BLOCK), 0.0, NEG))
        bias.append(jnp.concatenate(parts, axis=1))

    def tiles(buf, g):
        return jnp.concatenate([buf[slot, g, r] for r in range(n_pick)], axis=1)

    o_s = attend([tiles(ksel, g) for g in range(g_n)], [tiles(vsel, g) for g in range(g_n)], bias,
                 nks_ref[0], nvs_ref[0])

    gt = jnp.broadcast_to(gate_ref[0], (LANES, LANES)).T
    nh = NSA_HEADS
    o_ref[0] = (gt[0:nh, 0:dh] * oc_ref[0] + gt[nh:2 * nh, 0:dh] * o_s + gt[2 * nh:3 * nh, 0:dh] * o_w)


def _sample_attend(idx, page_table, q_s, o_c, slc_k, slc_v, win_k, win_v, new_rows, gates, n_pick):
    bsz = page_table.shape[0]
    nh, dh, g = NSA_HEADS, NSA_HEAD_DIM, NSA_KV_HEADS
    wlen = win_k.shape[-1]
    row_spec = pl.BlockSpec((1, 1, NSA_KV), lambda b, *_: (b, 0, 0))
    win_spec = pl.BlockSpec((1, g, dh, wlen), lambda b, *_: (b, 0, 0, 0))
    head_spec = pl.BlockSpec((1, nh, dh), lambda b, *_: (b, 0, 0))
    any_spec = pl.BlockSpec(memory_space=pl.ANY)
    grid_spec = pltpu.PrefetchScalarGridSpec(
        num_scalar_prefetch=2, grid=(bsz,),
        in_specs=[head_spec, head_spec, any_spec, any_spec, win_spec, win_spec,
                  row_spec, row_spec, row_spec, row_spec, pl.BlockSpec((1, 1, LANES), lambda b, *_: (b, 0, 0))],
        out_specs=head_spec,
        scratch_shapes=[pltpu.VMEM((2, g, n_pick, dh, PAGE_SIZE), F32), pltpu.VMEM((2, g, n_pick, dh, PAGE_SIZE), F32),
                        pltpu.SemaphoreType.DMA((2, 2))])
    return pl.pallas_call(
        functools.partial(_sample_attend_kernel, n_pick=n_pick),
        grid_spec=grid_spec, out_shape=SDS((bsz, nh, dh), F32),
        compiler_params=_params("arbitrary"),
    )(idx, page_table, q_s, o_c, slc_k, slc_v, win_k, win_v, *new_rows, gates)


def _gla_sample_kernel(q_ref, k_ref, v_ref, r_ref, misc_ref, wa_ref, ba_ref, ng_ref, s0_ref, o_ref, s_ref):
    dk, dv = GLA_DK, GLA_DV
    lr = jnp.broadcast_to(misc_ref[0][:, N_GATE:N_GATE + GLA_RANK], (16, GLA_RANK)).astype(BF16)
    x = jnp.dot(lr, wa_ref[...], preferred_element_type=F32)[0:1] + ba_ref[...]
    g_all = jax.nn.log_sigmoid(x) / GLA_TAU

    def col(v):
        t = jnp.broadcast_to(v, (dk, dk)).T
        return jnp.concatenate([t] * (dv // dk), axis=1)

    outs = []
    for h in range(GLA_HEADS):
        g = g_all[:, h * dk:(h + 1) * dk]
        q = q_ref[0][:, h * dk:(h + 1) * dk] * (GLA_DK ** -0.5)
        k = k_ref[0][:, h * dk:(h + 1) * dk]
        v = v_ref[0][:, h * dv:(h + 1) * dv]
        s0 = s0_ref[0, h]
        q_t = q * jnp.exp(g)
        k_t = k * jnp.exp(-g)
        a = jnp.sum(q_t * k_t, axis=-1, keepdims=True)
        o = jnp.sum(col(q_t) * s0, axis=0, keepdims=True) + a * v
        s_ref[0, h] = col(jnp.exp(g)) * s0 + col(k) * v
        outs.append(_rms(o, ng_ref[...]) * jax.nn.silu(r_ref[0][:, h * dv:(h + 1) * dv]))
    o_ref[0] = jnp.concatenate(outs, axis=-1)


def _gla_sample(q_l, k_l, v_l, r_l, misc, wa, ba, ng, s0):
    bsz, h, dk, dv = s0.shape

    def row(n):
        return pl.BlockSpec((1, 1, n), lambda b: (b, 0, 0))

    st_spec = pl.BlockSpec((1, h, dk, dv), lambda b: (b, 0, 0, 0))
    return pl.pallas_call(
        _gla_sample_kernel, grid=(bsz,),
        in_specs=[row(h * dk), row(h * dk), row(h * dv), row(h * dv), row(LANES),
                  _resident(wa.shape), _resident(ba.shape), _resident(ng.shape), st_spec],
        out_specs=[row(h * dv), st_spec],
        out_shape=[SDS((bsz, 1, h * dv), F32), SDS((bsz, h, dk, dv), F32)],
        compiler_params=_params("parallel"),
    )(q_l, k_l, v_l, r_l, misc, wa, ba, ng, s0)


SAMPLE_ROWS = 16


def _layer_sample(x, caches, wins, s0, page_table, w):
    bsz, t, d = x.shape
    n_pages = page_table.shape[1]
    pos = n_pages * PAGE_SIZE
    assert t == 1 and bsz <= SAMPLE_ROWS and pos % SLC_BLOCK == 0 and wins[0].shape[1] == WINDOW
    assert pos // SLC_BLOCK >= N_SELECT
    mp = SAMPLE_ROWS
    x2 = jnp.pad(x.reshape(bsz, d), ((0, mp - bsz), (0, 0)))
    zb = jnp.zeros((1, LANES), F32)
    spec_a = [('bf16', 0, NSA_Q, NSA_HEAD_DIM ** -0.5)]
    spec_a += [('f32', NSA_Q + j * NSA_KV, NSA_Q + (j + 1) * NSA_KV, 1.0) for j in range(6)]
    q_s, r0, r1, r2, r3, r4, r5 = _proj(x2, w['norm1'], w['w_t'], w['row_a'], w['rows_a'], w['w_misc_t'], zb,
                                        spec_a, mp)
    spec_b = [('f32', 0, _GQ, 1.0), ('f32', _GQ, 2 * _GQ, 1.0), ('f32', 2 * _GQ, 2 * _GQ + _GV, 1.0),
              ('f32', 2 * _GQ + _GV, 2 * _GQ + 2 * _GV, 1.0),
              ('misc', 2 * _GQ + 2 * _GV, 2 * _GQ + 2 * _GV + LANES, 1.0)]
    q_l, k_l, v_l, r_l, misc = _proj(x2, w['norm1'], w['w_t'], w['row_b'], w['rows_b'], w['w_misc_t'], w['b_misc'],
                                     spec_b, mp)

    cache_t = [a.transpose(0, 2, 3, 1) for a in caches]
    win_t = [a.transpose(0, 2, 3, 1) for a in wins]
    n_pool = caches[0].shape[0]
    cmp_pages = [a.reshape(n_pool, NSA_KV, PAGE_SIZE) for a in cache_t[:2]]
    kcmp, vcmp = _compress(page_table, cmp_pages[0], cmp_pages[1], w['cmp'], paged=True)
    q_h = q_s.reshape(mp, NSA_HEADS, NSA_HEAD_DIM)
    n_pick = N_SELECT - 1
    o_c, idx = _sample_select(q_h, kcmp, vcmp, pos, n_pick)
    idx = idx[:, ::NSA_HPG, :n_pick].reshape(bsz * NSA_KV_HEADS, n_pick)
    gates = misc[:, :N_GATE].reshape(mp, NSA_KV_HEADS, 3, NSA_HPG).transpose(0, 2, 1, 3).reshape(mp, 1, N_GATE)
    gates = jnp.pad(gates, ((0, 0), (0, 0), (0, LANES - N_GATE)))
    new_rows = [a.reshape(mp, 1, NSA_KV) for a in (r2, r3, r4, r5)]
    o_a = _sample_attend(idx, page_table, q_h, o_c, cache_t[2], cache_t[3], win_t[0], win_t[1], new_rows, gates,
                         n_pick)
    o_a = jnp.pad(o_a.reshape(bsz, NSA_Q), ((0, mp - bsz), (0, 0))).astype(BF16)

    def r3d(a):
        return a.reshape(mp, 1, a.shape[-1])

    o_b, s_new = _gla_sample(r3d(q_l), r3d(k_l), r3d(v_l), r3d(r_l), r3d(misc), w['gla_wa'], w['gla_ba'],
                             w['gla_ng'], s0)
    o_b = jnp.pad(o_b.reshape(bsz, _GV), ((0, mp - bsz), (0, 0))).astype(BF16)
    kvh = (bsz, 1, NSA_KV_HEADS, NSA_HEAD_DIM)
    rows = [a[:bsz].reshape(kvh) for a in (r0, r1, r2, r3)]
    new_wins = [jnp.concatenate([c[:, 1:], a[:bsz].reshape(kvh)], axis=1) for c, a in zip(wins, (r4, r5))]
    return (o_a, o_b, x2), rows, new_wins, s_new


def kernel(x_prompt, x_sample, cache_cmp_k, cache_cmp_v, cache_slc_k, cache_slc_v, cache_win_k, cache_win_v, state_gla, page_table, norm1_g, w_in, b_nsa_gate, cmp_pe_k, cmp_pe_v, cmp_k_w1, cmp_k_w2, cmp_v_w1, cmp_v_w2, gla_w_a2, gla_b_a, gla_norm_g, w_br_a, w_br_b, w_o, norm2_g, w_up, w_down, norm_f):
    assert DEPTH == 1 and norm1_g.shape[0] == 1
    w = _prep_weights(norm1_g[0], w_in[0], b_nsa_gate[0], cmp_pe_k[0], cmp_pe_v[0], cmp_k_w1[0], cmp_k_w2[0],
                      cmp_v_w1[0], cmp_v_w2[0], gla_w_a2[0], gla_b_a[0], gla_norm_g[0], w_br_a[0], w_br_b[0],
                      w_o[0], norm2_g[0], w_up[0], w_down[0], norm_f)
    mix_p, rows_p, wins_p, s_p = _layer_prompt(x_prompt, w)
    caches = [c[0] for c in (cache_cmp_k, cache_cmp_v, cache_slc_k, cache_slc_v)]
    mix_s, rows_s, wins_s, s_s = _layer_sample(x_sample, caches, [cache_win_k[0], cache_win_v[0]], state_gla[0],
                                               page_table, w)
    y_p, y_s = _channel_mix(mix_p, mix_s, w)
    y_p = y_p.reshape(x_prompt.shape)
    y_s = y_s[:x_sample.shape[0]].reshape(x_sample.shape)
    outs_p = [a[None] for a in rows_p + wins_p + [s_p]]
    outs_s = [a[None] for a in rows_s + wins_s + [s_s]]
    return (y_p, y_s, *outs_p, *outs_s)
```
